```python
import math
import jax, jax.numpy as jnp
from jax import lax
import numpy as np

D_MODEL = 2048
BATCH = 8
SEQ = 8192
DEPTH = 1

MEM_LEN = 256
SWA_HEADS = 16
SWA_KV_HEADS = 2
SWA_HEAD_DIM = 64
WINDOW = 128
BLOCK = 128
REL_BUCKETS = 32
REL_MAX_DIST = 128
GDN_HEADS = 8
GDN_HEAD_DIM = 128
GDN_CONV = 4
GDN_CHUNK = 64
MEM_HEADS = 4
MEM_HEAD_DIM = 128
D_FF = 5504
FFN_CONV = 3

NORM_EPS = 1e-5
DEEPNORM_ALPHA = (2 * DEPTH) ** 0.25
DEEPNORM_BETA = (8 * DEPTH) ** -0.25
NEG_INF = -1e30

SWA_Q = SWA_HEADS * SWA_HEAD_DIM
SWA_KV = SWA_KV_HEADS * SWA_HEAD_DIM
GDN_W = GDN_HEADS * GDN_HEAD_DIM
MEM_W = MEM_HEADS * MEM_HEAD_DIM
IN_WIDTHS = (SWA_Q, SWA_KV, SWA_KV, GDN_W, GDN_W, GDN_W, GDN_W, GDN_HEADS, GDN_HEADS, D_MODEL, D_MODEL)
IN_DIM = sum(IN_WIDTHS)

kernel_name = "hybrid_swa_gdn_gated_merge_deepnorm"


def split_columns(t, widths):
    points, acc = [], 0
    for w in widths[:-1]:
        acc += w
        points.append(acc)
    return jnp.split(t, points, axis=-1)


def layer_norm(x, g, b):
    xf = x.astype(jnp.float32)
    mu = jnp.mean(xf, axis=-1, keepdims=True)
    var = jnp.mean(jnp.square(xf - mu), axis=-1, keepdims=True)
    y = (xf - mu) * lax.rsqrt(var + NORM_EPS) * g.astype(jnp.float32) + b.astype(jnp.float32)
    return y.astype(x.dtype)


def causal_dwconv(x, w):
    width, ch = w.shape
    return lax.conv_general_dilated(
        x, w[:, None, :].astype(x.dtype), window_strides=(1,), padding=[(width - 1, 0)],
        dimension_numbers=('NWC', 'WIO', 'NWC'), feature_group_count=ch)


def t5_causal_bucket(dist):
    max_exact = REL_BUCKETS // 2
    d = jnp.maximum(dist, 1).astype(jnp.float32)
    large = max_exact + (jnp.log(d / max_exact) / math.log(REL_MAX_DIST / max_exact)
                         * (REL_BUCKETS - max_exact)).astype(jnp.int32)
    large = jnp.minimum(large, REL_BUCKETS - 1)
    return jnp.where(dist < max_exact, dist, large)


def swa_attention(q, k, v, sinks, rel_bias):
    b, s = q.shape[:2]
    nb = s // BLOCK
    grp = SWA_HEADS // SWA_KV_HEADS
    qb = q.reshape(b, nb, BLOCK, SWA_KV_HEADS, grp, SWA_HEAD_DIM)

    def band(t):
        tb = t.reshape(b, nb, BLOCK, SWA_KV_HEADS, SWA_HEAD_DIM)
        prev = jnp.pad(tb[:, :-1], ((0, 0), (1, 0), (0, 0), (0, 0), (0, 0)))
        return jnp.concatenate([prev, tb], axis=2)

    kb, vb = band(k), band(v)
    scores = jnp.einsum('bnqhgd,bnkhd->bnhgqk', qb, kb).astype(jnp.float32) * (SWA_HEAD_DIM ** -0.5)

    qi = jnp.arange(BLOCK)[:, None]
    kj = jnp.arange(2 * BLOCK)[None, :]
    dist = qi + BLOCK - kj
    in_window = (dist >= 0) & (dist < WINDOW)
    has_prev = (jnp.arange(nb)[:, None, None] > 0) | (kj >= BLOCK)[None]
    mask = in_window[None] & has_prev
    bias = rel_bias.astype(jnp.float32)[t5_causal_bucket(jnp.maximum(dist, 0))]
    bias = bias.transpose(2, 0, 1).reshape(SWA_KV_HEADS, grp, BLOCK, 2 * BLOCK)
    scores = jnp.where(mask[None, :, None, None], scores + bias, NEG_INF)

    sink = jnp.broadcast_to(sinks.astype(jnp.float32).reshape(1, 1, SWA_KV_HEADS, grp, 1, 1),
                            scores.shape[:-1] + (1,))
    probs = jax.nn.softmax(jnp.concatenate([scores, sink], axis=-1), axis=-1)[..., :-1]
    out = jnp.einsum('bnhgqk,bnkhd->bnqhgd', probs.astype(vb.dtype), vb)
    return out.reshape(b, s, SWA_Q)


def l2norm(t):
    return t * lax.rsqrt(jnp.sum(jnp.square(t), axis=-1, keepdims=True) + 1e-6)


def gated_delta_rule(q, k, v, g, beta):
    b, s, h, dk = q.shape
    dv = v.shape[-1]
    c = GDN_CHUNK
    n = s // c
    q = q * (dk ** -0.5)

    def chunk(t):
        return jnp.swapaxes(t.reshape((b, n, c, h) + t.shape[3:]), 2, 3)

    qc, kc, vc = chunk(q), chunk(k), chunk(v)
    gc = jnp.cumsum(chunk(g), axis=-1)
    bc = chunk(beta)
    kbeta = kc * bc[..., None]
    vbeta = vc * bc[..., None]

    tril = jnp.tril(jnp.ones((c, c), dtype=bool))
    strict = jnp.tril(jnp.ones((c, c), dtype=bool), -1)
    diff = gc[..., :, None] - gc[..., None, :]
    decay = jnp.where(tril, jnp.exp(jnp.where(tril, diff, 0.0)), 0.0)

    a_low = jnp.where(strict, jnp.einsum('bnhid,bnhjd->bnhij', kbeta, kc) * decay, 0.0)
    t_mat = a_low + jnp.eye(c, dtype=jnp.float32)
    rhs = jnp.concatenate([vbeta, kbeta * jnp.exp(gc)[..., None]], axis=-1)
    sol = lax.linalg.triangular_solve(t_mat, rhs, left_side=True, lower=True)
    u, w = sol[..., :dv], sol[..., dv:]

    attn_intra = jnp.where(tril, jnp.einsum('bnhid,bnhjd->bnhij', qc, kc) * decay, 0.0)
    q_dec = qc * jnp.exp(gc)[..., None]
    k_dec = kc * jnp.exp(gc[..., -1:] - gc)[..., None]
    g_last = jnp.exp(gc[..., -1])

    def step(state, inp):
        qd, kd, uu, ww, ai, gl = inp
        v_new = uu - jnp.einsum('bhck,bhkv->bhcv', ww, state)
        o = jnp.einsum('bhck,bhkv->bhcv', qd, state) + jnp.einsum('bhij,bhjv->bhiv', ai, v_new)
        state = state * gl[..., None, None] + jnp.einsum('bhck,bhcv->bhkv', kd, v_new)
        return state, o

    xs = tuple(jnp.moveaxis(t, 1, 0) for t in (q_dec, k_dec, u, w, attn_intra, g_last))
    state0 = jnp.zeros((b, h, dk, dv), jnp.float32)
    _, o = lax.scan(step, state0, xs)
    return jnp.swapaxes(jnp.moveaxis(o, 0, 1), 2, 3).reshape(b, s, h, dv)


def gated_deltanet(gq, gk, gv, gz, gb, ga, conv_w, a_log, dt_bias, norm_w):
    b, s, _ = gq.shape
    dtype = gq.dtype
    qkv = jax.nn.silu(causal_dwconv(jnp.concatenate([gq, gk, gv], axis=-1), conv_w))
    q, k, v = split_columns(qkv.astype(jnp.float32), (GDN_W, GDN_W, GDN_W))
    q = l2norm(q.reshape(b, s, GDN_HEADS, GDN_HEAD_DIM))
    k = l2norm(k.reshape(b, s, GDN_HEADS, GDN_HEAD_DIM))
    v = v.reshape(b, s, GDN_HEADS, GDN_HEAD_DIM)
    beta = jax.nn.sigmoid(gb.astype(jnp.float32))
    g = -jnp.exp(a_log.astype(jnp.float32)) * jax.nn.softplus(ga.astype(jnp.float32) + dt_bias.astype(jnp.float32))
    o = gated_delta_rule(q, k, v, g, beta)
    o = o * lax.rsqrt(jnp.mean(jnp.square(o), axis=-1, keepdims=True) + 1e-6) * norm_w.astype(jnp.float32)
    z = jax.nn.silu(gz.astype(jnp.float32)).reshape(b, s, GDN_HEADS, GDN_HEAD_DIM)
    return (o * z).reshape(b, s, GDN_W).astype(dtype)


def memory_attention(x, mem, w_q, w_kv, w_o):
    b, s, _ = x.shape
    m = mem.shape[1]
    q = (x @ w_q).reshape(b, s, MEM_HEADS, MEM_HEAD_DIM)
    k, v = split_columns(mem @ w_kv, (MEM_W, MEM_W))
    k = k.reshape(b, m, MEM_HEADS, MEM_HEAD_DIM)
    v = v.reshape(b, m, MEM_HEADS, MEM_HEAD_DIM)
    scores = jnp.einsum('bshd,bmhd->bhsm', q, k).astype(jnp.float32) * (MEM_HEAD_DIM ** -0.5)
    p = jax.nn.softmax(scores, axis=-1)
    o = jnp.einsum('bhsm,bmhd->bshd', p.astype(v.dtype), v).reshape(b, s, MEM_W)
    return o @ w_o


def _fwd_setup_inputs(seed: int = 0) -> dict:
    key = jax.random.key(seed)
    ks = jax.random.split(key, 32)
    f32 = jnp.float32
    L = DEPTH

    def nrm(k, shape, scale):
        return jax.random.normal(k, shape, f32) * scale

    dt = jnp.exp(jax.random.uniform(ks[7], (L, GDN_HEADS), f32, math.log(1e-3), math.log(1e-1)))
    return {
        "x": nrm(ks[0], (BATCH, SEQ, D_MODEL), 1.0),
        "mem": nrm(ks[1], (BATCH, MEM_LEN, D_MODEL), 1.0),
        "w_in": nrm(ks[2], (L, D_MODEL, IN_DIM), D_MODEL ** -0.5),
        "rel_bias": nrm(ks[3], (REL_BUCKETS, SWA_HEADS), 0.5),
        "swa_sinks": nrm(ks[4], (L, SWA_HEADS), 1.0),
        "gdn_conv_w": nrm(ks[5], (L, GDN_CONV, 3 * GDN_W), GDN_CONV ** -0.5),
        "gdn_a_log": jnp.log(jax.random.uniform(ks[6], (L, GDN_HEADS), f32, 1.0, 16.0)),
        "gdn_dt_bias": dt + jnp.log(-jnp.expm1(-dt)),
        "gdn_norm_w": 1.0 + nrm(ks[8], (L, GDN_HEAD_DIM), 0.02),
        "w_br_swa": nrm(ks[9], (L, SWA_Q, D_MODEL), SWA_Q ** -0.5),
        "w_br_gdn": nrm(ks[10], (L, GDN_W, D_MODEL), GDN_W ** -0.5),
        "w_mix_o": nrm(ks[11], (L, D_MODEL, D_MODEL), D_MODEL ** -0.5 * DEEPNORM_BETA),
        "ln1_g": 1.0 + nrm(ks[12], (L, D_MODEL), 0.02),
        "ln1_b": nrm(ks[13], (L, D_MODEL), 0.02),
        "w_mem_q": nrm(ks[14], (L, D_MODEL, MEM_W), D_MODEL ** -0.5),
        "w_mem_kv": nrm(ks[15], (L, D_MODEL, 2 * MEM_W), D_MODEL ** -0.5),
        "w_mem_o": nrm(ks[16], (L, MEM_W, D_MODEL), MEM_W ** -0.5 * DEEPNORM_BETA),
        "ln2_g": 1.0 + nrm(ks[17], (L, D_MODEL), 0.02),
        "ln2_b": nrm(ks[18], (L, D_MODEL), 0.02),
        "w_up": nrm(ks[19], (L, D_MODEL, 2 * D_FF), D_MODEL ** -0.5),
        "ffn_conv_w": nrm(ks[20], (L, FFN_CONV, 2 * D_FF), FFN_CONV ** -0.5),
        "ffn_conv_b": nrm(ks[21], (L, 2 * D_FF), 0.02),
        "w_down": nrm(ks[22], (L, D_FF, D_MODEL), D_FF ** -0.5 * DEEPNORM_BETA),
        "ln3_g": 1.0 + nrm(ks[23], (L, D_MODEL), 0.02),
        "ln3_b": nrm(ks[24], (L, D_MODEL), 0.02),
    }


def _fwd_reference(x, mem, w_in, rel_bias, swa_sinks, gdn_conv_w, gdn_a_log, gdn_dt_bias, gdn_norm_w,
              w_br_swa, w_br_gdn, w_mix_o, ln1_g, ln1_b, w_mem_q, w_mem_kv, w_mem_o, ln2_g, ln2_b,
              w_up, ffn_conv_w, ffn_conv_b, w_down, ln3_g, ln3_b):
    b, s, _ = x.shape
    for l in range(DEPTH):
        proj = x @ w_in[l]
        (sq, sk, sv, gq, gk, gv, gz, gb, ga, gate_swa, gate_gdn) = split_columns(proj, IN_WIDTHS)
        y_swa = swa_attention(
            sq.reshape(b, s, SWA_HEADS, SWA_HEAD_DIM),
            sk.reshape(b, s, SWA_KV_HEADS, SWA_HEAD_DIM),
            sv.reshape(b, s, SWA_KV_HEADS, SWA_HEAD_DIM),
            swa_sinks[l], rel_bias) @ w_br_swa[l]
        y_gdn = gated_deltanet(gq, gk, gv, gz, gb, ga, gdn_conv_w[l], gdn_a_log[l],
                               gdn_dt_bias[l], gdn_norm_w[l]) @ w_br_gdn[l]
        mixed = jax.nn.sigmoid(gate_swa) * y_swa + jax.nn.sigmoid(gate_gdn) * y_gdn
        x = layer_norm(DEEPNORM_ALPHA * x + mixed @ w_mix_o[l], ln1_g[l], ln1_b[l])
        c = memory_attention(x, mem, w_mem_q[l], w_mem_kv[l], w_mem_o[l])
        x = layer_norm(DEEPNORM_ALPHA * x + c, ln2_g[l], ln2_b[l])
        hcat = causal_dwconv(x @ w_up[l], ffn_conv_w[l]) + ffn_conv_b[l]
        h_gate, h_up = split_columns(hcat, (D_FF, D_FF))
        f = (jax.nn.silu(h_gate) * h_up) @ w_down[l]
        x = layer_norm(DEEPNORM_ALPHA * x + f, ln3_g[l], ln3_b[l])
    return x


import jax as _jax
import jax.numpy as _jnp

TWIN_FORMAT = 'train_step'
FWD_PARAMS = ['x', 'mem', 'w_in', 'rel_bias', 'swa_sinks', 'gdn_conv_w', 'gdn_a_log', 'gdn_dt_bias', 'gdn_norm_w', 'w_br_swa', 'w_br_gdn', 'w_mix_o', 'ln1_g', 'ln1_b', 'w_mem_q', 'w_mem_kv', 'w_mem_o', 'ln2_g', 'ln2_b', 'w_up', 'ffn_conv_w', 'ffn_conv_b', 'w_down', 'ln3_g', 'ln3_b']
TWIN_WEIGHTS = ['w_in', 'rel_bias', 'swa_sinks', 'gdn_conv_w', 'gdn_a_log', 'gdn_dt_bias', 'gdn_norm_w', 'w_br_swa', 'w_br_gdn', 'w_mix_o', 'ln1_g', 'ln1_b', 'w_mem_q', 'w_mem_kv', 'w_mem_o', 'ln2_g', 'ln2_b', 'w_up', 'ffn_conv_w', 'ffn_conv_b', 'w_down', 'ln3_g', 'ln3_b']
TWIN_DIFF_INPUT = 'x'
TWIN_INPUTS = ['x', 'mem', 'w_in', 'rel_bias', 'swa_sinks', 'gdn_conv_w', 'gdn_a_log', 'gdn_dt_bias', 'gdn_norm_w', 'w_br_swa', 'w_br_gdn', 'w_mix_o', 'ln1_g', 'ln1_b', 'w_mem_q', 'w_mem_kv', 'w_mem_o', 'ln2_g', 'ln2_b', 'w_up', 'ffn_conv_w', 'ffn_conv_b', 'w_down', 'ln3_g', 'ln3_b', 'loss_target', 'm_w_in', 'm_rel_bias', 'm_swa_sinks', 'm_gdn_conv_w', 'm_gdn_a_log', 'm_gdn_dt_bias', 'm_gdn_norm_w', 'm_w_br_swa', 'm_w_br_gdn', 'm_w_mix_o', 'm_ln1_g', 'm_ln1_b', 'm_w_mem_q', 'm_w_mem_kv', 'm_w_mem_o', 'm_ln2_g', 'm_ln2_b', 'm_w_up', 'm_ffn_conv_w', 'm_ffn_conv_b', 'm_w_down', 'm_ln3_g', 'm_ln3_b', 'v_w_in', 'v_rel_bias', 'v_swa_sinks', 'v_gdn_conv_w', 'v_gdn_a_log', 'v_gdn_dt_bias', 'v_gdn_norm_w', 'v_w_br_swa', 'v_w_br_gdn', 'v_w_mix_o', 'v_ln1_g', 'v_ln1_b', 'v_w_mem_q', 'v_w_mem_kv', 'v_w_mem_o', 'v_ln2_g', 'v_ln2_b', 'v_w_up', 'v_ffn_conv_w', 'v_ffn_conv_b', 'v_w_down', 'v_ln3_g', 'v_ln3_b']
TWIN_OUTPUTS = ['loss', 'grad_x', 'grad_w_in', 'grad_rel_bias', 'grad_swa_sinks', 'grad_gdn_conv_w', 'grad_gdn_a_log', 'grad_gdn_dt_bias', 'grad_gdn_norm_w', 'grad_w_br_swa', 'grad_w_br_gdn', 'grad_w_mix_o', 'grad_ln1_g', 'grad_ln1_b', 'grad_w_mem_q', 'grad_w_mem_kv', 'grad_w_mem_o', 'grad_ln2_g', 'grad_ln2_b', 'grad_w_up', 'grad_ffn_conv_w', 'grad_ffn_conv_b', 'grad_w_down', 'grad_ln3_g', 'grad_ln3_b', 'delta_w_in', 'delta_rel_bias', 'delta_swa_sinks', 'delta_gdn_conv_w', 'delta_gdn_a_log', 'delta_gdn_dt_bias', 'delta_gdn_norm_w', 'delta_w_br_swa', 'delta_w_br_gdn', 'delta_w_mix_o', 'delta_ln1_g', 'delta_ln1_b', 'delta_w_mem_q', 'delta_w_mem_kv', 'delta_w_mem_o', 'delta_ln2_g', 'delta_ln2_b', 'delta_w_up', 'delta_ffn_conv_w', 'delta_ffn_conv_b', 'delta_w_down', 'delta_ln3_g', 'delta_ln3_b', 'new_m_w_in', 'new_m_rel_bias', 'new_m_swa_sinks', 'new_m_gdn_conv_w', 'new_m_gdn_a_log', 'new_m_gdn_dt_bias', 'new_m_gdn_norm_w', 'new_m_w_br_swa', 'new_m_w_br_gdn', 'new_m_w_mix_o', 'new_m_ln1_g', 'new_m_ln1_b', 'new_m_w_mem_q', 'new_m_w_mem_kv', 'new_m_w_mem_o', 'new_m_ln2_g', 'new_m_ln2_b', 'new_m_w_up', 'new_m_ffn_conv_w', 'new_m_ffn_conv_b', 'new_m_w_down', 'new_m_ln3_g', 'new_m_ln3_b', 'new_v_w_in', 'new_v_rel_bias', 'new_v_swa_sinks', 'new_v_gdn_conv_w', 'new_v_gdn_a_log', 'new_v_gdn_dt_bias', 'new_v_gdn_norm_w', 'new_v_w_br_swa', 'new_v_w_br_gdn', 'new_v_w_mix_o', 'new_v_ln1_g', 'new_v_ln1_b', 'new_v_w_mem_q', 'new_v_w_mem_kv', 'new_v_w_mem_o', 'new_v_ln2_g', 'new_v_ln2_b', 'new_v_w_up', 'new_v_ffn_conv_w', 'new_v_ffn_conv_b', 'new_v_w_down', 'new_v_ln3_g', 'new_v_ln3_b']
TWIN_LEAF_KINDS = {'loss': 'loss', 'grad_x': 'grad_x', 'grad_w_in': 'grad_w', 'grad_rel_bias': 'grad_w', 'grad_swa_sinks': 'grad_w', 'grad_gdn_conv_w': 'grad_w', 'grad_gdn_a_log': 'grad_w', 'grad_gdn_dt_bias': 'grad_w', 'grad_gdn_norm_w': 'grad_w', 'grad_w_br_swa': 'grad_w', 'grad_w_br_gdn': 'grad_w', 'grad_w_mix_o': 'grad_w', 'grad_ln1_g': 'grad_w', 'grad_ln1_b': 'grad_w', 'grad_w_mem_q': 'grad_w', 'grad_w_mem_kv': 'grad_w', 'grad_w_mem_o': 'grad_w', 'grad_ln2_g': 'grad_w', 'grad_ln2_b': 'grad_w', 'grad_w_up': 'grad_w', 'grad_ffn_conv_w': 'grad_w', 'grad_ffn_conv_b': 'grad_w', 'grad_w_down': 'grad_w', 'grad_ln3_g': 'grad_w', 'grad_ln3_b': 'grad_w', 'delta_w_in': 'delta_w', 'delta_rel_bias': 'delta_w', 'delta_swa_sinks': 'delta_w', 'delta_gdn_conv_w': 'delta_w', 'delta_gdn_a_log': 'delta_w', 'delta_gdn_dt_bias': 'delta_w', 'delta_gdn_norm_w': 'delta_w', 'delta_w_br_swa': 'delta_w', 'delta_w_br_gdn': 'delta_w', 'delta_w_mix_o': 'delta_w', 'delta_ln1_g': 'delta_w', 'delta_ln1_b': 'delta_w', 'delta_w_mem_q': 'delta_w', 'delta_w_mem_kv': 'delta_w', 'delta_w_mem_o': 'delta_w', 'delta_ln2_g': 'delta_w', 'delta_ln2_b': 'delta_w', 'delta_w_up': 'delta_w', 'delta_ffn_conv_w': 'delta_w', 'delta_ffn_conv_b': 'delta_w', 'delta_w_down': 'delta_w', 'delta_ln3_g': 'delta_w', 'delta_ln3_b': 'delta_w', 'new_m_w_in': 'new_m', 'new_m_rel_bias': 'new_m', 'new_m_swa_sinks': 'new_m', 'new_m_gdn_conv_w': 'new_m', 'new_m_gdn_a_log': 'new_m', 'new_m_gdn_dt_bias': 'new_m', 'new_m_gdn_norm_w': 'new_m', 'new_m_w_br_swa': 'new_m', 'new_m_w_br_gdn': 'new_m', 'new_m_w_mix_o': 'new_m', 'new_m_ln1_g': 'new_m', 'new_m_ln1_b': 'new_m', 'new_m_w_mem_q': 'new_m', 'new_m_w_mem_kv': 'new_m', 'new_m_w_mem_o': 'new_m', 'new_m_ln2_g': 'new_m', 'new_m_ln2_b': 'new_m', 'new_m_w_up': 'new_m', 'new_m_ffn_conv_w': 'new_m', 'new_m_ffn_conv_b': 'new_m', 'new_m_w_down': 'new_m', 'new_m_ln3_g': 'new_m', 'new_m_ln3_b': 'new_m', 'new_v_w_in': 'new_v', 'new_v_rel_bias': 'new_v', 'new_v_swa_sinks': 'new_v', 'new_v_gdn_conv_w': 'new_v', 'new_v_gdn_a_log': 'new_v', 'new_v_gdn_dt_bias': 'new_v', 'new_v_gdn_norm_w': 'new_v', 'new_v_w_br_swa': 'new_v', 'new_v_w_br_gdn': 'new_v', 'new_v_w_mix_o': 'new_v', 'new_v_ln1_g': 'new_v', 'new_v_ln1_b': 'new_v', 'new_v_w_mem_q': 'new_v', 'new_v_w_mem_kv': 'new_v', 'new_v_w_mem_o': 'new_v', 'new_v_ln2_g': 'new_v', 'new_v_ln2_b': 'new_v', 'new_v_w_up': 'new_v', 'new_v_ffn_conv_w': 'new_v', 'new_v_ffn_conv_b': 'new_v', 'new_v_w_down': 'new_v', 'new_v_ln3_g': 'new_v', 'new_v_ln3_b': 'new_v'}


def _forward(args):
    return _fwd_reference(*[args[k] for k in FWD_PARAMS])


def _output_shape():
    def fwd():
        inp = _fwd_setup_inputs(0)
        return _fwd_reference(*[inp[k] for k in FWD_PARAMS])
    out = _jax.eval_shape(fwd)
    return out.shape, out.dtype

N_MICROBATCH = 1
ADAM_LR = 0.001
ADAM_B1 = 0.9
ADAM_B2 = 0.999
ADAM_EPS = 1e-08
ADAM_WD = 0.01
ADAM_STEP = 10
PER_EXAMPLE_BATCH_AXIS = {'x': 0, 'mem': 0, 'loss_target': 0}
SHARED_INPUTS = []
_WEIGHT_DTYPES = {'w_in': _jnp.float32, 'rel_bias': _jnp.float32, 'swa_sinks': _jnp.float32, 'gdn_conv_w': _jnp.float32, 'gdn_a_log': _jnp.float32, 'gdn_dt_bias': _jnp.float32, 'gdn_norm_w': _jnp.float32, 'w_br_swa': _jnp.float32, 'w_br_gdn': _jnp.float32, 'w_mix_o': _jnp.float32, 'ln1_g': _jnp.float32, 'ln1_b': _jnp.float32, 'w_mem_q': _jnp.float32, 'w_mem_kv': _jnp.float32, 'w_mem_o': _jnp.float32, 'ln2_g': _jnp.float32, 'ln2_b': _jnp.float32, 'w_up': _jnp.float32, 'ffn_conv_w': _jnp.float32, 'ffn_conv_b': _jnp.float32, 'w_down': _jnp.float32, 'ln3_g': _jnp.float32, 'ln3_b': _jnp.float32}
MOMENT_SCALE = {'w_in': 1.731031e-02, 'rel_bias': 1.269791e-02, 'swa_sinks': 6.391685e-03, 'gdn_conv_w': 2.290206e-02, 'gdn_a_log': 1.623494e-01, 'gdn_dt_bias': 1.598558e-01, 'gdn_norm_w': 8.241655e-02, 'w_br_swa': 7.695908e-03, 'w_br_gdn': 2.353806e-02, 'w_mix_o': 3.924554e-02, 'ln1_g': 1.023871e+00, 'ln1_b': 4.871007e-01, 'w_mem_q': 1.346006e-02, 'w_mem_kv': 1.463350e-02, 'w_mem_o': 1.310650e-02, 'ln2_g': 1.031198e+00, 'ln2_b': 4.884717e-01, 'w_up': 2.274331e-02, 'ffn_conv_w': 2.290969e-02, 'ffn_conv_b': 2.871605e-02, 'w_down': 6.183746e-02, 'ln3_g': 3.200690e+01, 'ln3_b': 1.309486e+00}


def _to_microbatches(a, axis):
    t = _jnp.moveaxis(a, axis, 0)
    t = t.reshape((N_MICROBATCH, t.shape[0] // N_MICROBATCH) + t.shape[1:])
    return _jnp.moveaxis(t, 1, axis + 1)


def setup_inputs(seed: int = 0) -> dict:
    inp = _fwd_setup_inputs(seed)
    key = _jax.random.fold_in(_jax.random.key(seed), 7919)
    shape, _ = _output_shape()
    out = dict(inp)
    out["loss_target"] = _jax.random.normal(_jax.random.fold_in(key, 0), shape, _jnp.float32)
    for i, name in enumerate(TWIN_WEIGHTS):
        w = inp[name].astype(_jnp.float32)
        if MOMENT_SCALE is None:
            s = _jnp.sqrt(_jnp.mean(_jnp.square(w)) + 1e-30)
        else:
            s = MOMENT_SCALE[name]
        km, kv = _jax.random.split(_jax.random.fold_in(key, i + 1))
        out[name] = w
        out["m_" + name] = s * _jax.random.normal(km, w.shape, _jnp.float32)
        out["v_" + name] = (s * s) * _jax.random.uniform(kv, w.shape, _jnp.float32, 0.5, 1.5)
    if N_MICROBATCH > 1:
        for name, axis in PER_EXAMPLE_BATCH_AXIS.items():
            out[name] = _to_microbatches(out[name], axis)
    return {'x': out['x'], 'mem': out['mem'], 'w_in': out['w_in'], 'rel_bias': out['rel_bias'], 'swa_sinks': out['swa_sinks'], 'gdn_conv_w': out['gdn_conv_w'], 'gdn_a_log': out['gdn_a_log'], 'gdn_dt_bias': out['gdn_dt_bias'], 'gdn_norm_w': out['gdn_norm_w'], 'w_br_swa': out['w_br_swa'], 'w_br_gdn': out['w_br_gdn'], 'w_mix_o': out['w_mix_o'], 'ln1_g': out['ln1_g'], 'ln1_b': out['ln1_b'], 'w_mem_q': out['w_mem_q'], 'w_mem_kv': out['w_mem_kv'], 'w_mem_o': out['w_mem_o'], 'ln2_g': out['ln2_g'], 'ln2_b': out['ln2_b'], 'w_up': out['w_up'], 'ffn_conv_w': out['ffn_conv_w'], 'ffn_conv_b': out['ffn_conv_b'], 'w_down': out['w_down'], 'ln3_g': out['ln3_g'], 'ln3_b': out['ln3_b'], 'loss_target': out['loss_target'], 'm_w_in': out['m_w_in'], 'm_rel_bias': out['m_rel_bias'], 'm_swa_sinks': out['m_swa_sinks'], 'm_gdn_conv_w': out['m_gdn_conv_w'], 'm_gdn_a_log': out['m_gdn_a_log'], 'm_gdn_dt_bias': out['m_gdn_dt_bias'], 'm_gdn_norm_w': out['m_gdn_norm_w'], 'm_w_br_swa': out['m_w_br_swa'], 'm_w_br_gdn': out['m_w_br_gdn'], 'm_w_mix_o': out['m_w_mix_o'], 'm_ln1_g': out['m_ln1_g'], 'm_ln1_b': out['m_ln1_b'], 'm_w_mem_q': out['m_w_mem_q'], 'm_w_mem_kv': out['m_w_mem_kv'], 'm_w_mem_o': out['m_w_mem_o'], 'm_ln2_g': out['m_ln2_g'], 'm_ln2_b': out['m_ln2_b'], 'm_w_up': out['m_w_up'], 'm_ffn_conv_w': out['m_ffn_conv_w'], 'm_ffn_conv_b': out['m_ffn_conv_b'], 'm_w_down': out['m_w_down'], 'm_ln3_g': out['m_ln3_g'], 'm_ln3_b': out['m_ln3_b'], 'v_w_in': out['v_w_in'], 'v_rel_bias': out['v_rel_bias'], 'v_swa_sinks': out['v_swa_sinks'], 'v_gdn_conv_w': out['v_gdn_conv_w'], 'v_gdn_a_log': out['v_gdn_a_log'], 'v_gdn_dt_bias': out['v_gdn_dt_bias'], 'v_gdn_norm_w': out['v_gdn_norm_w'], 'v_w_br_swa': out['v_w_br_swa'], 'v_w_br_gdn': out['v_w_br_gdn'], 'v_w_mix_o': out['v_w_mix_o'], 'v_ln1_g': out['v_ln1_g'], 'v_ln1_b': out['v_ln1_b'], 'v_w_mem_q': out['v_w_mem_q'], 'v_w_mem_kv': out['v_w_mem_kv'], 'v_w_mem_o': out['v_w_mem_o'], 'v_ln2_g': out['v_ln2_g'], 'v_ln2_b': out['v_ln2_b'], 'v_w_up': out['v_w_up'], 'v_ffn_conv_w': out['v_ffn_conv_w'], 'v_ffn_conv_b': out['v_ffn_conv_b'], 'v_w_down': out['v_w_down'], 'v_ln3_g': out['v_ln3_g'], 'v_ln3_b': out['v_ln3_b']}


def _loss(weights, diff, rest, loss_target):
    with _jax.named_scope("forward"):
        args = {**rest, TWIN_DIFF_INPUT: diff, **{k: w.astype(_WEIGHT_DTYPES[k]) for k, w in weights.items()}}
        y = _forward(args)
    with _jax.named_scope("loss_head"):
        err = _jnp.square(y.astype(_jnp.float32) - loss_target)
        return 0.5 * _jnp.sum(_jnp.mean(err, axis=-1)) if err.ndim else 0.5 * err


def _adamw(w, g, m, v):
    m = ADAM_B1 * m + (1.0 - ADAM_B1) * g
    v = ADAM_B2 * v + (1.0 - ADAM_B2) * _jnp.square(g)
    m_hat = m / (1.0 - ADAM_B1 ** ADAM_STEP)
    v_hat = v / (1.0 - ADAM_B2 ** ADAM_STEP)
    delta = -ADAM_LR * (m_hat / (_jnp.sqrt(v_hat) + ADAM_EPS) + ADAM_WD * w)
    return delta, m, v


def reference(x, mem, w_in, rel_bias, swa_sinks, gdn_conv_w, gdn_a_log, gdn_dt_bias, gdn_norm_w, w_br_swa, w_br_gdn, w_mix_o, ln1_g, ln1_b, w_mem_q, w_mem_kv, w_mem_o, ln2_g, ln2_b, w_up, ffn_conv_w, ffn_conv_b, w_down, ln3_g, ln3_b, loss_target, m_w_in, m_rel_bias, m_swa_sinks, m_gdn_conv_w, m_gdn_a_log, m_gdn_dt_bias, m_gdn_norm_w, m_w_br_swa, m_w_br_gdn, m_w_mix_o, m_ln1_g, m_ln1_b, m_w_mem_q, m_w_mem_kv, m_w_mem_o, m_ln2_g, m_ln2_b, m_w_up, m_ffn_conv_w, m_ffn_conv_b, m_w_down, m_ln3_g, m_ln3_b, v_w_in, v_rel_bias, v_swa_sinks, v_gdn_conv_w, v_gdn_a_log, v_gdn_dt_bias, v_gdn_norm_w, v_w_br_swa, v_w_br_gdn, v_w_mix_o, v_ln1_g, v_ln1_b, v_w_mem_q, v_w_mem_kv, v_w_mem_o, v_ln2_g, v_ln2_b, v_w_up, v_ffn_conv_w, v_ffn_conv_b, v_w_down, v_ln3_g, v_ln3_b):
    given = dict(x=x, mem=mem, w_in=w_in, rel_bias=rel_bias, swa_sinks=swa_sinks, gdn_conv_w=gdn_conv_w, gdn_a_log=gdn_a_log, gdn_dt_bias=gdn_dt_bias, gdn_norm_w=gdn_norm_w, w_br_swa=w_br_swa, w_br_gdn=w_br_gdn, w_mix_o=w_mix_o, ln1_g=ln1_g, ln1_b=ln1_b, w_mem_q=w_mem_q, w_mem_kv=w_mem_kv, w_mem_o=w_mem_o, ln2_g=ln2_g, ln2_b=ln2_b, w_up=w_up, ffn_conv_w=ffn_conv_w, ffn_conv_b=ffn_conv_b, w_down=w_down, ln3_g=ln3_g, ln3_b=ln3_b, loss_target=loss_target, m_w_in=m_w_in, m_rel_bias=m_rel_bias, m_swa_sinks=m_swa_sinks, m_gdn_conv_w=m_gdn_conv_w, m_gdn_a_log=m_gdn_a_log, m_gdn_dt_bias=m_gdn_dt_bias, m_gdn_norm_w=m_gdn_norm_w, m_w_br_swa=m_w_br_swa, m_w_br_gdn=m_w_br_gdn, m_w_mix_o=m_w_mix_o, m_ln1_g=m_ln1_g, m_ln1_b=m_ln1_b, m_w_mem_q=m_w_mem_q, m_w_mem_kv=m_w_mem_kv, m_w_mem_o=m_w_mem_o, m_ln2_g=m_ln2_g, m_ln2_b=m_ln2_b, m_w_up=m_w_up, m_ffn_conv_w=m_ffn_conv_w, m_ffn_conv_b=m_ffn_conv_b, m_w_down=m_w_down, m_ln3_g=m_ln3_g, m_ln3_b=m_ln3_b, v_w_in=v_w_in, v_rel_bias=v_rel_bias, v_swa_sinks=v_swa_sinks, v_gdn_conv_w=v_gdn_conv_w, v_gdn_a_log=v_gdn_a_log, v_gdn_dt_bias=v_gdn_dt_bias, v_gdn_norm_w=v_gdn_norm_w, v_w_br_swa=v_w_br_swa, v_w_br_gdn=v_w_br_gdn, v_w_mix_o=v_w_mix_o, v_ln1_g=v_ln1_g, v_ln1_b=v_ln1_b, v_w_mem_q=v_w_mem_q, v_w_mem_kv=v_w_mem_kv, v_w_mem_o=v_w_mem_o, v_ln2_g=v_ln2_g, v_ln2_b=v_ln2_b, v_w_up=v_w_up, v_ffn_conv_w=v_ffn_conv_w, v_ffn_conv_b=v_ffn_conv_b, v_w_down=v_w_down, v_ln3_g=v_ln3_g, v_ln3_b=v_ln3_b)
    weights = {n: given[n] for n in TWIN_WEIGHTS}
    shared = {n: given[n] for n in SHARED_INPUTS}
    per_example = {n: given[n] for n in ['x', 'mem']}
    grad_fn = _jax.value_and_grad(_loss, argnums=(0, 1))

    def one_microbatch(ex, loss_target):
        ex = dict(ex)
        diff = ex.pop(TWIN_DIFF_INPUT)
        return grad_fn(weights, diff, {**shared, **ex}, loss_target)

    if N_MICROBATCH == 1:
        loss, (grad_w, grad_x) = one_microbatch(per_example, given["loss_target"])
    else:
        def body(carry, xs):
            loss_sum, grad_sum = carry
            l_k, (gw_k, gx_k) = one_microbatch(xs[0], xs[1])
            with _jax.named_scope("update"):
                return (loss_sum + l_k, _jax.tree.map(_jnp.add, grad_sum, gw_k)), gx_k

        init = (_jnp.zeros((), _jnp.float32), _jax.tree.map(_jnp.zeros_like, weights))
        (loss, grad_w), grad_x = _jax.lax.scan(body, init, (per_example, given["loss_target"]))
    with _jax.named_scope("update"):
        delta_w, new_m, new_v = {}, {}, {}
        for n in TWIN_WEIGHTS:
            delta_w[n], new_m[n], new_v[n] = _adamw(weights[n], grad_w[n], given["m_" + n], given["v_" + n])
    return (loss, grad_x, *[grad_w[n] for n in TWIN_WEIGHTS], *[delta_w[n] for n in TWIN_WEIGHTS],
            *[new_m[n] for n in TWIN_WEIGHTS], *[new_v[n] for n in TWIN_WEIGHTS])
```

```python
import functools
import math

import jax
import jax.numpy as jnp
import numpy as np
from jax import lax
from jax.experimental import pallas as pl
from jax.experimental.pallas import tpu as pltpu

F32 = jnp.float32
BF16 = jnp.bfloat16
_CDT = BF16

D_MODEL = 2048
SWA_HEADS, SWA_KV_HEADS, SWA_HEAD_DIM, SWA_BLOCK = 16, 2, 64, 128
SWA_GRP = SWA_HEADS // SWA_KV_HEADS
REL_BUCKETS, REL_MAX_DIST = 32, 128
GDN_HEADS, GDN_HEAD_DIM, GDN_CONV, GDN_CHUNK = 8, 128, 4, 64
MEM_HEADS, MEM_HEAD_DIM = 4, 128
D_FF, D_FF_PAD, FFN_CONV = 5504, 5632, 3
SWA_Q, SWA_KV, GDN_W, MEM_W = 1024, 128, 1024, 512
IN_WIDTHS = (SWA_Q, SWA_KV, SWA_KV, GDN_W, GDN_W, GDN_W, GDN_W, GDN_HEADS, GDN_HEADS, D_MODEL, D_MODEL)
NORM_EPS = 1e-5
ALPHA = 2.0 ** 0.25
NEG_INF = -1e30
ADAM_LR, ADAM_B1, ADAM_B2, ADAM_EPS, ADAM_WD, ADAM_STEP = 0.001, 0.9, 0.999, 1e-08, 0.01, 10
LANES, SUBLANES = 128, 8
VMEM_LIMIT = 56 * 1024 * 1024

P_GS, P_GG, P_GQKV, P_GZ, P_SQ, P_SK, P_SV, P_BA, P_END = 0, 2048, 4096, 7168, 8192, 9216, 9344, 9472, 9600


def _tile(dim, pref, align=LANES):
    if dim <= pref:
        return dim
    t = (pref // align) * align
    while t >= align:
        if dim % t == 0:
            return t
        t -= align
    return dim


_DIMS = {"nn": (((1,), (0,)), ((), ())), "nt": (((1,), (1,)), ((), ())), "tn": (((0,), (0,)), ((), ()))}


def _raw_dot(a, b, form, hi):
    if hi:
        return lax.dot_general(a.astype(F32), b.astype(F32), _DIMS[form], precision=lax.Precision.HIGHEST,
                               preferred_element_type=F32)
    return lax.dot_general(a.astype(_CDT), b.astype(_CDT), _DIMS[form], preferred_element_type=F32)


@functools.partial(jax.custom_vjp, nondiff_argnums=(2, 3))
def _dot(a, b, form, hi=False):
    return _raw_dot(a, b, form, hi)


def _dot_fwd(a, b, form, hi):
    return _raw_dot(a, b, form, hi), (a, b)


def _dot_bwd(form, hi, res, g):
    a, b = res
    if form == "nn":
        da, db = _raw_dot(g, b, "nt", hi), _raw_dot(a, g, "tn", hi)
    elif form == "nt":
        da, db = _raw_dot(g, b, "nn", hi), _raw_dot(g, a, "tn", hi)
    else:
        da, db = _raw_dot(b, g, "nt", hi), _raw_dot(a, g, "nn", hi)
    return da.astype(a.dtype), db.astype(b.dtype)


_dot.defvjp(_dot_fwd, _dot_bwd)


@functools.partial(jax.custom_vjp, nondiff_argnums=(1,))
def _shift_rows(x, k):
    return pltpu.roll(x, k, 0)


def _shift_rows_fwd(x, k):
    return pltpu.roll(x, k, 0), None


def _shift_rows_bwd(k, _, g):
    return (pltpu.roll(g, g.shape[0] - k, 0),)


_shift_rows.defvjp(_shift_rows_fwd, _shift_rows_bwd)


def _sigmoid(x):
    return 1.0 / (1.0 + jnp.exp(-x))


def _silu(x):
    return x * _sigmoid(x)


def _softplus(x):
    return jnp.maximum(x, 0.0) + jnp.log(1.0 + jnp.exp(-jnp.abs(x)))


def _iota(shape, axis):
    return lax.broadcasted_iota(jnp.int32, shape, axis)


def _cparams(sem):
    return pltpu.CompilerParams(dimension_semantics=sem, vmem_limit_bytes=VMEM_LIMIT)


def _mm(name, a, b, form, out_dtype=F32, add=None, add_scale=1.0, hi=False, tm=512, tn=512, tk=1024):
    if form == "nn":
        (M, K), (K2, N) = a.shape, b.shape
    elif form == "nt":
        (M, K), (N, K2) = a.shape, b.shape
    else:
        (K, M), (K2, N) = a.shape, b.shape
    assert K == K2, (name, a.shape, b.shape, form)
    tm, tn, tk = _tile(M, tm), _tile(N, tn), _tile(K, tk)
    nk = K // tk
    a_spec = pl.BlockSpec((tk, tm), lambda i, j, k: (k, i)) if form == "tn" else pl.BlockSpec((tm, tk), lambda i, j, k: (i, k))
    b_spec = pl.BlockSpec((tn, tk), lambda i, j, k: (j, k)) if form == "nt" else pl.BlockSpec((tk, tn), lambda i, j, k: (k, j))
    o_spec = pl.BlockSpec((tm, tn), lambda i, j, k: (i, j))
    has_add = add is not None

    def body(*refs):
        if has_add:
            a_ref, b_ref, c_ref, o_ref, acc = refs
        else:
            a_ref, b_ref, o_ref, acc = refs
        k = pl.program_id(2)

        @pl.when(k == 0)
        def _():
            acc[...] = jnp.zeros_like(acc)

        acc[...] += _raw_dot(a_ref[...], b_ref[...], form, hi)

        @pl.when(k == nk - 1)
        def _():
            r = acc[...]
            if has_add:
                r = r + add_scale * c_ref[...].astype(F32)
            o_ref[...] = r.astype(out_dtype)

    ins = [a, b] + ([add] if has_add else [])
    specs = [a_spec, b_spec] + ([o_spec] if has_add else [])
    return pl.pallas_call(
        body, name=name, grid=(M // tm, N // tn, nk), in_specs=specs, out_specs=o_spec,
        out_shape=jax.ShapeDtypeStruct((M, N), out_dtype),
        scratch_shapes=[pltpu.VMEM((tm, tn), F32)],
        compiler_params=_cparams(("parallel", "parallel", "arbitrary")),
    )(*ins)


class Row:
    def __init__(self, arr, blk, imap, hblk=None, hmap=None, gshape=None, gmap=None):
        self.arr, self.blk, self.imap, self.hblk, self.hmap, self.gshape, self.gmap = arr, blk, imap, hblk, hmap, gshape, gmap


class Par:
    def __init__(self, arr, blk=None, imap=None, gshape=None, gmap=None):
        self.arr = arr
        self.blk = tuple(arr.shape) if blk is None else blk
        nd = len(self.blk)
        self.imap = (lambda j: (0,) * nd) if imap is None else imap
        self.gshape, self.gmap = gshape, gmap


class Out:
    def __init__(self, shape, dtype, blk, imap):
        self.shape, self.dtype, self.blk, self.imap = shape, dtype, blk, imap


def _rows_of(blk):
    return [d for d in blk if d is not None][0]


def _rowmap(name, fn, ncol, nblk, rows, pars, outs, accs=()):
    in_specs, ins = [], []
    for r in rows:
        ins.append(r.arr)
        in_specs.append(pl.BlockSpec(r.blk, r.imap))
        if r.hblk is not None:
            ins.append(r.arr)
            in_specs.append(pl.BlockSpec(r.hblk, r.hmap))
    for p in pars:
        ins.append(p.arr)
        in_specs.append(pl.BlockSpec(p.blk, (lambda im: (lambda j, n: im(j)))(p.imap)))
    out_specs = [pl.BlockSpec(o.blk, o.imap) for o in outs]
    out_shape = [jax.ShapeDtypeStruct(o.shape, o.dtype) for o in outs]
    for a in accs:
        out_specs.append(pl.BlockSpec(a, (lambda nd: (lambda j, n: (0,) * nd))(len(a))))
        out_shape.append(jax.ShapeDtypeStruct(a, F32))
    n_in = len(ins)

    def body(*refs):
        j, n = pl.program_id(0), pl.program_id(1)
        it = iter(refs[:n_in])
        rvals = []
        for r in rows:
            cur = next(it)[...]
            rvals.append((next(it)[...], cur) if r.hblk is not None else cur)
        pvals = [next(it)[...] for _ in pars]
        o_refs = refs[n_in:n_in + len(outs)]
        a_refs = refs[n_in + len(outs):]
        ovals, avals = fn(j, n == 0, rvals, pvals)
        for ref, v in zip(o_refs, ovals):
            ref[...] = v.astype(ref.dtype)
        if accs:
            @pl.when((j == 0) & (n == 0))
            def _():
                for ref in a_refs:
                    ref[...] = jnp.zeros_like(ref)
            for ref, v in zip(a_refs, avals):
                ref[...] += v

    res = pl.pallas_call(
        body, name=name, grid=(ncol, nblk), in_specs=in_specs, out_specs=out_specs, out_shape=out_shape,
        compiler_params=_cparams(("arbitrary", "arbitrary")),
    )(*ins)
    return res


def _rowmap_bwd(name, fn, ncol, nblk, rows, pars, cts):
    rev = lambda im: (lambda j, s: im(j, nblk - 1 - s))
    in_specs, ins = [], []
    for r in rows:
        ins.append(r.arr)
        in_specs.append(pl.BlockSpec(r.blk, rev(r.imap)))
        if r.hblk is not None:
            ins.append(r.arr)
            in_specs.append(pl.BlockSpec(r.hblk, rev(r.hmap)))
    for p in pars:
        ins.append(p.arr)
        in_specs.append(pl.BlockSpec(p.blk, (lambda im: (lambda j, s: im(j)))(p.imap)))
    for c in cts:
        ins.append(c.arr)
        in_specs.append(pl.BlockSpec(c.blk, rev(c.imap)))
    n_in = len(ins)
    drows = [i for i, r in enumerate(rows) if r.gshape is not None]
    dpars = [i for i, p in enumerate(pars) if p.gshape is not None]
    out_specs, out_shape, scratch = [], [], []
    for i in drows:
        r = rows[i]
        out_specs.append(pl.BlockSpec(r.blk, rev(r.gmap)))
        out_shape.append(jax.ShapeDtypeStruct(r.gshape, F32))
        if r.hblk is not None:
            scratch.append(pltpu.VMEM(tuple(d for d in r.hblk if d is not None), F32))
    for i in dpars:
        p = pars[i]
        out_specs.append(pl.BlockSpec(p.blk, (lambda im: (lambda j, s: im(j)))(p.gmap)))
        out_shape.append(jax.ShapeDtypeStruct(p.gshape, F32))

    def body(*refs):
        j, s = pl.program_id(0), pl.program_id(1)
        first = s == nblk - 1
        it = iter(refs[:n_in])
        rvals = []
        for r in rows:
            cur = next(it)[...]
            rvals.append((next(it)[...], cur) if r.hblk is not None else cur)
        pvals = [next(it)[...] for _ in pars]
        cvals = [next(it)[...].astype(F32) for _ in cts]
        g_refs = refs[n_in:n_in + len(drows) + len(dpars)]
        carries = iter(refs[n_in + len(drows) + len(dpars):])

        def f(dr, dp):
            rv, pv = list(rvals), list(pvals)
            for i, v in zip(drows, dr):
                rv[i] = v
            for i, v in zip(dpars, dp):
                pv[i] = v
            return fn(j, first, rv, pv)

        _, vjp = jax.vjp(f, [rvals[i] for i in drows], [pvals[i] for i in dpars])
        g_r, g_p = vjp(cvals)
        for ref, i, g in zip(g_refs, drows, g_r):
            r = rows[i]
            if r.hblk is None:
                ref[...] = g
            else:
                g_prev, g_cur = g
                carry = next(carries)
                nr, nh = g_cur.shape[0], g_prev.shape[0]
                ref[...] = g_cur

                @pl.when(s > 0)
                def _():
                    ref[nr - nh:nr, :] += carry[...]

                carry[...] = g_prev
        for ref, g in zip(g_refs[len(drows):], g_p):
            @pl.when(s == 0)
            def _():
                ref[...] = jnp.zeros_like(ref)
            ref[...] += g

    return pl.pallas_call(
        body, name=name, grid=(ncol, nblk), in_specs=in_specs, out_specs=out_specs, out_shape=out_shape,
        scratch_shapes=scratch, compiler_params=_cparams(("arbitrary", "arbitrary")),
    )(*ins)


def _rowspec(arr, tb, cw, c0, cstep=1, halo=0, grad=False, ncol=1):
    T = arr.shape[0]
    imap = lambda j, n: (n, c0 + cstep * j)
    hblk = hmap = None
    if halo:
        q = tb // halo
        hblk, hmap = (halo, cw), (lambda j, n: (jnp.maximum(n * q - 1, 0), c0 + cstep * j))
    gshape = (T, cw * (ncol if cstep else 1)) if grad else None
    gmap = (lambda j, n: (n, cstep * j)) if grad else None
    return Row(arr, (tb, cw), imap, hblk, hmap, gshape, gmap)


def _parspec(arr, cw=None, c0=0, grad=False, ncol=1):
    if cw is None:
        return Par(arr, gshape=tuple(arr.shape) if grad else None,
                   gmap=(lambda nd: (lambda j: (0,) * nd))(arr.ndim) if grad else None)
    r = arr.shape[0]
    return Par(arr, (r, cw), lambda j: (0, c0 + j), (r, cw * ncol) if grad else None, (lambda j: (0, j)) if grad else None)


def _ln(r, g, b):
    mu = jnp.mean(r, axis=-1, keepdims=True)
    xc = r - mu
    var = jnp.mean(xc * xc, axis=-1, keepdims=True)
    return xc * lax.rsqrt(var + NORM_EPS) * g + b


def _ln_fn(j, first, rv, pv):
    return [_ln(rv[0], pv[0], pv[1])]


def _ln_fwd_fn(j, first, rv, pv):
    y = _ln(rv[0], pv[0], pv[1])
    return [y, y], []


def _loss_fn(j, first, rv, pv):
    r3, tgt = rv
    g, b = pv
    y, vjp = jax.vjp(_ln, r3, g, b)
    diff = y - tgt
    part = 0.5 * jnp.sum(diff * diff) / D_MODEL
    dr, dg, db = vjp(diff * (1.0 / D_MODEL))
    return [dr], [jnp.full((SUBLANES, LANES), part, F32), dg, db]


def _mix_fn(j, first, rv, pv):
    gs, gg, ys, yg = rv
    return [_sigmoid(gs) * ys + _sigmoid(gg) * yg]


def _row_pick(x, i):
    return jnp.sum(jnp.where(_iota(x.shape, 0) == i, x, 0.0), axis=0, keepdims=True)


def _causal_conv(prev, cur, w, first):
    width = w.shape[0]
    nh = prev.shape[0]
    row = _iota(cur.shape, 0)
    prev = jnp.where(first, 0.0, prev)
    y = cur * _row_pick(w, width - 1)
    for d in range(1, width):
        sh = _shift_rows(cur, d)
        for t in range(d):
            sh = jnp.where(row == t, _row_pick(prev, nh - d + t), sh)
        y = y + sh * _row_pick(w, width - 1 - d)
    return y


def _ffn_act_fn(j, first, rv, pv):
    (pg, cg), (pu, cu) = rv
    wg, wu, bg, bu = pv
    hg = _causal_conv(pg, cg, wg, first) + bg
    hu = _causal_conv(pu, cu, wu, first) + bu
    return [_silu(hg) * hu]


def _gdn_pre_fn(j, first, rv, pv):
    (prev, cur), = rv
    w, = pv
    t = _silu(_causal_conv(prev, cur, w, first))
    tn = t * lax.rsqrt(jnp.sum(t * t, axis=-1, keepdims=True) + 1e-6)
    return [jnp.where(j < 2 * GDN_HEADS, tn, t)]


def _gdn_gate_fn(j, first, rv, pv):
    gba, = rv
    alog, dtb, eb, eg = pv
    tb = gba.shape[0]
    beta = _sigmoid(gba)
    g = -jnp.exp(alog) * _softplus(gba + dtb)
    ri, ci = _iota((tb, tb), 0), _iota((tb, tb), 1)
    tril = jnp.where((ri // GDN_CHUNK == ci // GDN_CHUNK) & (ci <= ri), 1.0, 0.0)
    gc = _dot(tril, g, "nn", True)
    return [_dot(beta, eb, "nn", True), _dot(gc, eg, "nn", True)]


def _swa_fn(j, first, rv, pv):
    q, (kp, kc), (vp, vc) = rv
    bp, bc, sk = pv
    sp = _dot(q, kp, "nt") * (SWA_HEAD_DIM ** -0.5) + bp
    sc = _dot(q, kc, "nt") * (SWA_HEAD_DIM ** -0.5) + bc
    qi = _iota(sp.shape, 0) % SWA_BLOCK
    kj = _iota(sp.shape, 1)
    sp = jnp.where((kj > qi) & jnp.logical_not(first), sp, NEG_INF)
    sc = jnp.where(kj <= qi, sc, NEG_INF)
    m = jnp.maximum(jnp.maximum(jnp.max(sp, axis=-1, keepdims=True), jnp.max(sc, axis=-1, keepdims=True)), sk)
    m = lax.stop_gradient(m)
    ep, ec, es = jnp.exp(sp - m), jnp.exp(sc - m), jnp.exp(sk - m)
    inv = 1.0 / (jnp.sum(ep, axis=-1, keepdims=True) + jnp.sum(ec, axis=-1, keepdims=True) + es)
    vp = jnp.where(first, 0.0, vp)
    return [_dot(ep * inv, vp, "nn") + _dot(ec * inv, vc, "nn")]


def _memattn_fn(j, first, rv, pv):
    q, = rv
    k, v = pv
    s = _dot(q, k, "nt") * (MEM_HEAD_DIM ** -0.5)
    m = lax.stop_gradient(jnp.max(s, axis=-1, keepdims=True))
    e = jnp.exp(s - m)
    p = e / jnp.sum(e, axis=-1, keepdims=True)
    return [_dot(p, v, "nn")]


def _gdn_head(q, k, v, bx, gx, g64, z, nw, S):
    c = GDN_CHUNK
    q = q * (GDN_HEAD_DIM ** -0.5)
    kb, vb = k * bx, v * bx
    ri, ci = _iota((c, c), 0), _iota((c, c), 1)
    tril, strict, eye = ci <= ri, ci < ri, ci == ri
    grow = jnp.sum(jnp.where(eye, g64, 0.0), axis=0, keepdims=True)
    decay = jnp.where(tril, jnp.exp(jnp.where(tril, g64 - grow, 0.0)), 0.0)
    a = jnp.where(strict, _dot(kb, k, "nt") * decay, 0.0)
    tinv = jnp.where(eye, 1.0, 0.0) - a
    x = _dot(a, a, "nn", True)
    for i in range(5):
        tinv = tinv + _dot(tinv, x, "nn", True)
        if i < 4:
            x = _dot(x, x, "nn", True)
    eg = jnp.exp(gx)
    u = _dot(tinv, vb, "nn", True)
    w = _dot(tinv, kb * eg, "nn", True)
    ai = jnp.where(tril, _dot(q, k, "nt") * decay, 0.0)
    glast = _row_pick(gx, c - 1)
    v_new = u - _dot(w, S, "nn")
    o = _dot(q * eg, S, "nn") + _dot(ai, v_new, "nn")
    s_new = S * jnp.exp(glast) + _dot(k * jnp.exp(glast - gx), v_new, "tn")
    o = o * lax.rsqrt(jnp.mean(o * o, axis=-1, keepdims=True) + 1e-6) * nw
    return o * _silu(z), s_new


def _gdn_chunks_fwd(qkv, bx, gx, proj, nw):
    T = qkv.shape[0]
    nc, c, hd, nh = T // GDN_CHUNK, GDN_CHUNK, GDN_HEAD_DIM, GDN_HEADS

    def body(qkv_ref, bx_ref, gx_ref, z_ref, nw_ref, y_ref, st_ref, S):
        @pl.when(pl.program_id(0) == 0)
        def _():
            S[...] = jnp.zeros_like(S)

        for h in range(nh):
            sl = slice(h * hd, (h + 1) * hd)
            s_h = S[h]
            st_ref[h] = s_h
            y, s_new = _gdn_head(qkv_ref[:, sl], qkv_ref[:, GDN_W + h * hd:GDN_W + (h + 1) * hd],
                                 qkv_ref[:, 2 * GDN_W + h * hd:2 * GDN_W + (h + 1) * hd],
                                 bx_ref[:, sl], gx_ref[:, sl], gx_ref[:, h * hd:h * hd + c], z_ref[:, sl], nw_ref[...], s_h)
            y_ref[:, sl] = y.astype(y_ref.dtype)
            S[h] = s_new

    row = lambda w, cb: pl.BlockSpec((c, w), lambda n: (n, cb))
    return pl.pallas_call(
        body, name="gdn_chunks_fwd", grid=(nc,),
        in_specs=[row(3 * GDN_W, 0), row(GDN_W, 0), row(GDN_W, 0), row(GDN_W, P_GZ // GDN_W), pl.BlockSpec((1, hd), lambda n: (0, 0))],
        out_specs=[row(GDN_W, 0), pl.BlockSpec((None, nh, hd, hd), lambda n: (n, 0, 0, 0))],
        out_shape=[jax.ShapeDtypeStruct((T, GDN_W), BF16), jax.ShapeDtypeStruct((nc, nh, hd, hd), F32)],
        scratch_shapes=[pltpu.VMEM((nh, hd, hd), F32)],
        compiler_params=_cparams(("arbitrary",)),
    )(qkv, bx, gx, proj, nw)


def _gdn_chunks_bwd(qkv, bx, gx, proj, nw, states, dy):
    T = qkv.shape[0]
    nc, c, hd, nh = T // GDN_CHUNK, GDN_CHUNK, GDN_HEAD_DIM, GDN_HEADS

    def body(qkv_ref, bx_ref, gx_ref, z_ref, nw_ref, st_ref, dy_ref, dqkv_ref, dbx_ref, dgx_ref, dz_ref, dnw_ref, dS):
        @pl.when(pl.program_id(0) == 0)
        def _():
            dS[...] = jnp.zeros_like(dS)
            dnw_ref[...] = jnp.zeros_like(dnw_ref)

        for h in range(nh):
            sl = slice(h * hd, (h + 1) * hd)
            sk = slice(GDN_W + h * hd, GDN_W + (h + 1) * hd)
            sv = slice(2 * GDN_W + h * hd, 2 * GDN_W + (h + 1) * hd)
            s64 = slice(h * hd, h * hd + c)
            args = (qkv_ref[:, sl], qkv_ref[:, sk], qkv_ref[:, sv], bx_ref[:, sl], gx_ref[:, sl], gx_ref[:, s64],
                    z_ref[:, sl], nw_ref[...], st_ref[h])
            _, vjp = jax.vjp(_gdn_head, *args)
            dq, dk, dv, dbx, dgx, dg64, dz, dnw, dsp = vjp((dy_ref[:, sl], dS[h]))
            dqkv_ref[:, sl] = dq
            dqkv_ref[:, sk] = dk
            dqkv_ref[:, sv] = dv
            dbx_ref[:, sl] = dbx
            dgx_ref[:, sl] = dgx
            dgx_ref[:, s64] += dg64
            dz_ref[:, sl] = dz
            dnw_ref[...] += dnw
            dS[h] = dsp

    row = lambda w, cb: pl.BlockSpec((c, w), lambda s: (nc - 1 - s, cb))
    return pl.pallas_call(
        body, name="gdn_chunks_bwd", grid=(nc,),
        in_specs=[row(3 * GDN_W, 0), row(GDN_W, 0), row(GDN_W, 0), row(GDN_W, P_GZ // GDN_W), pl.BlockSpec((1, hd), lambda s: (0, 0)),
                  pl.BlockSpec((None, nh, hd, hd), lambda s: (nc - 1 - s, 0, 0, 0)), row(GDN_W, 0)],
        out_specs=[row(3 * GDN_W, 0), row(GDN_W, 0), row(GDN_W, 0), row(GDN_W, 0), pl.BlockSpec((1, hd), lambda s: (0, 0))],
        out_shape=[jax.ShapeDtypeStruct((T, 3 * GDN_W), F32), jax.ShapeDtypeStruct((T, GDN_W), F32),
                   jax.ShapeDtypeStruct((T, GDN_W), F32), jax.ShapeDtypeStruct((T, GDN_W), F32), jax.ShapeDtypeStruct((1, hd), F32)],
        scratch_shapes=[pltpu.VMEM((nh, hd, hd), F32)],
        compiler_params=_cparams(("arbitrary",)),
    )(qkv, bx, gx, proj, nw, states, dy)


def _adamw(name, w, g, m, v):
    R, C = w.shape
    tr = _tile(R, 128, SUBLANES)

    def body(w_ref, g_ref, m_ref, v_ref, d_ref, m2_ref, v2_ref):
        g_ = g_ref[...]
        m2 = ADAM_B1 * m_ref[...] + (1.0 - ADAM_B1) * g_
        v2 = ADAM_B2 * v_ref[...] + (1.0 - ADAM_B2) * (g_ * g_)
        m_hat = m2 / (1.0 - ADAM_B1 ** ADAM_STEP)
        v_hat = v2 / (1.0 - ADAM_B2 ** ADAM_STEP)
        d_ref[...] = -ADAM_LR * (m_hat / (jnp.sqrt(v_hat) + ADAM_EPS) + ADAM_WD * w_ref[...])
        m2_ref[...] = m2
        v2_ref[...] = v2

    spec = pl.BlockSpec((tr, C), lambda i: (i, 0))
    return pl.pallas_call(
        body, name=name, grid=(R // tr,), in_specs=[spec] * 4, out_specs=[spec] * 3,
        out_shape=[jax.ShapeDtypeStruct((R, C), F32)] * 3, compiler_params=_cparams(("parallel",)),
    )(w, g, m, v)


def _addn(name, parts):
    parts = [p if isinstance(p, tuple) else (p, None) for p in parts]
    a0, k0 = parts[0]
    R, C = a0.shape[-2:]
    tr = _tile(R, 256, SUBLANES)
    specs = []
    for a, k in parts:
        if k is None:
            specs.append(pl.BlockSpec((tr, C), lambda i: (i, 0)))
        else:
            specs.append(pl.BlockSpec((None, tr, C), (lambda kk: (lambda i: (kk, i, 0)))(k)))

    def body(*refs):
        acc = refs[0][...]
        for r in refs[1:-1]:
            acc = acc + r[...]
        refs[-1][...] = acc

    return pl.pallas_call(
        body, name=name, grid=(R // tr,), in_specs=specs, out_specs=pl.BlockSpec((tr, C), lambda i: (i, 0)),
        out_shape=jax.ShapeDtypeStruct((R, C), F32), compiler_params=_cparams(("parallel",)),
    )(*[a for a, _ in parts])


MESH = pl.DeviceIdType.MESH
_HBM = pl.BlockSpec(memory_space=pltpu.HBM)


def _place():
    x, y, c = lax.axis_index("x"), lax.axis_index("y"), lax.axis_index("c")
    return x, y, c, [(1 - x, y), (x, 1 - y), (1 - x, 1 - y)]


def _chip_allgather(ts):
    nt = len(ts)

    def body(*refs):
        ins, outs = refs[:nt], refs[nt:2 * nt]
        send, recv, lsem = refs[2 * nt:]
        x, y, c, rel = _place()
        me, sib = 2 * x + y, (x, y, 1 - c)

        def half(t, hc):
            rh = ts[t].shape[0] // 2
            return pl.ds(pl.multiple_of(hc * rh, 16), rh)

        def rcopy(t, k, src, dst, to):
            return pltpu.make_async_remote_copy(src_ref=src, dst_ref=dst, send_sem=send.at[6 * t + k], recv_sem=recv.at[6 * t + k],
                                                device_id=to, device_id_type=MESH)

        started = []
        for t in range(nt):
            own = pltpu.make_async_copy(ins[t], outs[t].at[me], lsem.at[t])
            own.start()
            started.append(own)
        sends = []
        for t in range(nt):
            for r, (px, py) in enumerate(rel):
                cp = rcopy(t, r, ins[t].at[half(t, c)], outs[t].at[me, half(t, c)], (px, py, c))
                cp.start()
                sends.append(cp)
        for t in range(nt):
            for r, (px, py) in enumerate(rel):
                got = outs[t].at[2 * px + py, half(t, c)]
                rcopy(t, r, got, got, (px, py, c)).wait_recv()
                fw = rcopy(t, 3 + r, got, got, sib)
                fw.start()
                sends.append(fw)
        for t in range(nt):
            for r, (px, py) in enumerate(rel):
                got = outs[t].at[2 * px + py, half(t, 1 - c)]
                rcopy(t, 3 + r, got, got, sib).wait_recv()
        for cp in sends:
            cp.wait_send()
        for own in started:
            own.wait()

    return pl.pallas_call(
        body, name="chip_allgather", in_specs=[_HBM] * nt, out_specs=[_HBM] * nt,
        out_shape=[jax.ShapeDtypeStruct((4,) + tuple(t.shape), t.dtype) for t in ts],
        scratch_shapes=[pltpu.SemaphoreType.DMA((6 * nt,)), pltpu.SemaphoreType.DMA((6 * nt,)), pltpu.SemaphoreType.DMA((nt,))],
    )(*ts)


def _pair_swap(ts):
    nt = len(ts)

    def body(*refs):
        ins, outs = refs[:nt], refs[nt:2 * nt]
        send, recv = refs[2 * nt:]
        x, y, c, _ = _place()
        cps = []
        for t in range(nt):
            rh = ts[t].shape[1] // 2
            src = ins[t].at[:, pl.ds(pl.multiple_of((1 - c) * rh, 8), rh), :]
            cp = pltpu.make_async_remote_copy(src_ref=src, dst_ref=outs[t], send_sem=send.at[t], recv_sem=recv.at[t],
                                              device_id=(x, y, 1 - c), device_id_type=MESH)
            cp.start()
            cps.append(cp)
        for cp in cps:
            cp.wait()

    return pl.pallas_call(
        body, name="pair_swap", in_specs=[_HBM] * nt, out_specs=[_HBM] * nt,
        out_shape=[jax.ShapeDtypeStruct((4, t.shape[1] // 2, t.shape[2]), t.dtype) for t in ts],
        scratch_shapes=[pltpu.SemaphoreType.DMA((nt,)), pltpu.SemaphoreType.DMA((nt,))],
    )(*ts)


def _chip_scatter(ps):
    nt = len(ps)

    def body(*refs):
        ins, outs = refs[:nt], refs[nt:2 * nt]
        send, recv = refs[2 * nt:]
        x, y, c, rel = _place()
        cps = []
        for t in range(nt):
            for r, (px, py) in enumerate(rel):
                cp = pltpu.make_async_remote_copy(src_ref=ins[t].at[2 * px + py], dst_ref=outs[t].at[r], send_sem=send.at[3 * t + r],
                                                  recv_sem=recv.at[3 * t + r], device_id=(px, py, c), device_id_type=MESH)
                cp.start()
                cps.append(cp)
        for cp in cps:
            cp.wait()

    return pl.pallas_call(
        body, name="chip_scatter", in_specs=[_HBM] * nt, out_specs=[_HBM] * nt,
        out_shape=[jax.ShapeDtypeStruct((3,) + tuple(p.shape[1:]), p.dtype) for p in ps],
        scratch_shapes=[pltpu.SemaphoreType.DMA((3 * nt,)), pltpu.SemaphoreType.DMA((3 * nt,))],
    )(*ps)


def _pair_gather(gs):
    nt = len(gs)

    def body(*refs):
        ins, outs = refs[:nt], refs[nt:2 * nt]
        send, recv, lsem = refs[2 * nt:]
        x, y, c, _ = _place()
        cps, own = [], []
        for t in range(nt):
            o = pltpu.make_async_copy(ins[t], outs[t].at[c], lsem.at[t])
            o.start()
            own.append(o)
            cp = pltpu.make_async_remote_copy(src_ref=ins[t], dst_ref=outs[t].at[c], send_sem=send.at[t], recv_sem=recv.at[t],
                                              device_id=(x, y, 1 - c), device_id_type=MESH)
            cp.start()
            cps.append(cp)
        for t in range(nt):
            cps[t].wait_send()
            got = outs[t].at[1 - c]
            pltpu.make_async_remote_copy(src_ref=got, dst_ref=got, send_sem=send.at[t], recv_sem=recv.at[t],
                                         device_id=(x, y, 1 - c), device_id_type=MESH).wait_recv()
            own[t].wait()

    return pl.pallas_call(
        body, name="pair_gather", in_specs=[_HBM] * nt, out_specs=[_HBM] * nt,
        out_shape=[jax.ShapeDtypeStruct((2,) + tuple(g.shape), g.dtype) for g in gs],
        scratch_shapes=[pltpu.SemaphoreType.DMA((nt,)), pltpu.SemaphoreType.DMA((nt,)), pltpu.SemaphoreType.DMA((nt,))],
    )(*gs)


def _allgather8(v):
    m, n = v.shape

    def body(x_ref, out_ref, send, recv, lsem):
        x, y, c, rel = _place()
        me, sib = (x, y, c), (x, y, 1 - c)

        def blk(px, py, pc):
            return out_ref.at[4 * px + 2 * py + pc]

        def copy(k, block, to, src=None):
            return pltpu.make_async_remote_copy(src_ref=blk(*block) if src is None else src, dst_ref=blk(*block), send_sem=send.at[k],
                                                recv_sem=recv.at[k], device_id=to, device_id_type=MESH)

        mine = pltpu.make_async_copy(x_ref, blk(*me), lsem)
        mine.start()
        first = [copy(0, me, sib, src=x_ref)] + [copy(1 + r, me, (*ch, c), src=x_ref) for r, ch in enumerate(rel)]
        for cp in first:
            cp.start()
        passed = [copy(4 + r, (*ch, c), sib) for r, ch in enumerate(rel)]
        for r, ch in enumerate(rel):
            copy(1 + r, (*ch, c), me).wait_recv()
            passed[r].start()
        copy(0, sib, me).wait_recv()
        for r, ch in enumerate(rel):
            copy(4 + r, (*ch, 1 - c), me).wait_recv()
        for cp in first + passed:
            cp.wait_send()
        mine.wait()

    return pl.pallas_call(
        body, name="allgather8", in_specs=[pl.BlockSpec(memory_space=pltpu.VMEM)], out_specs=pl.BlockSpec(memory_space=pltpu.VMEM),
        out_shape=jax.ShapeDtypeStruct((8, m, n), v.dtype),
        scratch_shapes=[pltpu.SemaphoreType.DMA((7,)), pltpu.SemaphoreType.DMA((7,)), pltpu.SemaphoreType.DMA],
    )(v)


def _t5_bucket(dist):
    max_exact = REL_BUCKETS // 2
    d = jnp.maximum(dist, 1).astype(F32)
    large = max_exact + (jnp.log(d / max_exact) / math.log(REL_MAX_DIST / max_exact) * (REL_BUCKETS - max_exact)).astype(jnp.int32)
    large = jnp.minimum(large, REL_BUCKETS - 1)
    return jnp.where(dist < max_exact, dist, large)


def _bias_onehot():
    qi = jnp.arange(SWA_BLOCK)[:, None]
    kj = jnp.arange(SWA_BLOCK)[None, :]
    dist = jnp.concatenate([(qi + SWA_BLOCK - kj).reshape(-1), (qi - kj).reshape(-1)])
    bucket = _t5_bucket(jnp.maximum(dist, 0))
    return (bucket[None, :] == jnp.arange(REL_BUCKETS)[:, None]).astype(F32)


def _head_spread():
    lane = jnp.arange(LANES)[:, None]
    head = jnp.arange(GDN_W)[None, :] // GDN_HEAD_DIM
    return (lane == head).astype(F32), (lane == head + GDN_HEADS).astype(F32)


def _lane16(v8):
    return jnp.pad(v8.astype(F32), (GDN_HEADS, LANES - 2 * GDN_HEADS)).reshape(1, LANES)


def _stack_heads(t, nb):
    return t.reshape(nb, SWA_BLOCK, SWA_KV_HEADS, SWA_GRP, SWA_HEAD_DIM).transpose(2, 0, 3, 1, 4).reshape(
        SWA_KV_HEADS, nb * SWA_GRP * SWA_BLOCK, SWA_HEAD_DIM)


def _unstack_heads(t, nb):
    return t.reshape(SWA_KV_HEADS, nb, SWA_GRP, SWA_BLOCK, SWA_HEAD_DIM).transpose(1, 3, 0, 2, 4).reshape(nb * SWA_BLOCK, SWA_Q)


def _kv_heads(t):
    return t.reshape(t.shape[0], SWA_KV_HEADS, SWA_HEAD_DIM).transpose(1, 0, 2)


def _swa_specs(qs, ks, vs, bp, bc, sk, grad):
    T = ks.shape[1]
    qr = SWA_GRP * SWA_BLOCK
    g = lambda a: tuple(a.shape) if grad else None
    m3 = lambda j, n: (j, n, 0)
    h3 = lambda j, n: (j, jnp.maximum(n - 1, 0), 0)
    p3 = lambda j: (j, 0, 0)
    rows = [Row(qs, (None, qr, SWA_HEAD_DIM), m3, gshape=g(qs), gmap=m3),
            Row(ks, (None, SWA_BLOCK, SWA_HEAD_DIM), m3, (None, SWA_BLOCK, SWA_HEAD_DIM), h3, g(ks), m3),
            Row(vs, (None, SWA_BLOCK, SWA_HEAD_DIM), m3, (None, SWA_BLOCK, SWA_HEAD_DIM), h3, g(vs), m3)]
    pars = [Par(bp, (None, qr, SWA_BLOCK), p3, g(bp), p3), Par(bc, (None, qr, SWA_BLOCK), p3, g(bc), p3),
            Par(sk, (None, qr, 1), p3, g(sk), p3)]
    return rows, pars, T // SWA_BLOCK


def _fwd_bwd(x, mem, tgt, W):
    T = x.shape[0]
    nb = T // SWA_BLOCK
    tb = min(256, T)
    tbl = min(512, T)
    fwd = lambda f: (lambda *a: (f(*a), []))
    full = lambda cols, dt, t, cw: Out((T, cols), dt, (t, cw), lambda j, n: (n, j))

    proj = _mm("proj", x, W["in_p"], "nn")

    onehot_t = _bias_onehot()
    bias_flat = _mm("swa_bias", W["rel_bias"].T, onehot_t, "nn", hi=True)
    half = SWA_BLOCK * SWA_BLOCK
    bp = bias_flat[:, :half].reshape(SWA_KV_HEADS, SWA_GRP * SWA_BLOCK, SWA_BLOCK)
    bc = bias_flat[:, half:].reshape(SWA_KV_HEADS, SWA_GRP * SWA_BLOCK, SWA_BLOCK)
    sk = jnp.broadcast_to(W["swa_sinks"].reshape(SWA_KV_HEADS, SWA_GRP, 1, 1), (SWA_KV_HEADS, SWA_GRP, SWA_BLOCK, 1)).reshape(
        SWA_KV_HEADS, SWA_GRP * SWA_BLOCK, 1)
    qs = _stack_heads(proj[:, P_SQ:P_SQ + SWA_Q], nb)
    ks = _kv_heads(proj[:, P_SK:P_SK + SWA_KV])
    vs = _kv_heads(proj[:, P_SV:P_SV + SWA_KV])
    rows, pars, nblk = _swa_specs(qs, ks, vs, bp, bc, sk, False)
    o_s, = _rowmap("swa_fwd", fwd(_swa_fn), SWA_KV_HEADS, nblk, rows, pars,
                   [Out(tuple(qs.shape), F32, (None, SWA_GRP * SWA_BLOCK, SWA_HEAD_DIM), lambda j, n: (j, n, 0))])
    o_swa = _unstack_heads(o_s, nb).astype(_CDT)

    ncq = 3 * GDN_W // LANES
    pre_rows = lambda grad: [_rowspec(proj, tbl, LANES, P_GQKV // LANES, halo=SUBLANES, grad=grad, ncol=ncq)]
    pre_pars = lambda grad: [_parspec(W["gdn_conv_w"], LANES, 0, grad=grad, ncol=ncq)]
    qkv_n, = _rowmap("gdn_pre_fwd", fwd(_gdn_pre_fn), ncq, T // tbl, pre_rows(False), pre_pars(False),
                     [full(3 * GDN_W, F32, tbl, LANES)])
    eb, eg = _head_spread()
    alog_row, dtb_row = _lane16(W["gdn_a_log"]), _lane16(W["gdn_dt_bias"])
    gate_rows = lambda grad: [_rowspec(proj, tbl, LANES, P_BA // LANES, cstep=0, grad=grad)]
    gate_pars = lambda grad: [_parspec(alog_row, grad=grad), _parspec(dtb_row, grad=grad), _parspec(eb), _parspec(eg)]
    bx, gx = _rowmap("gdn_gate_fwd", fwd(_gdn_gate_fn), 1, T // tbl, gate_rows(False), gate_pars(False),
                     [full(GDN_W, F32, tbl, GDN_W), full(GDN_W, F32, tbl, GDN_W)])
    nw = W["gdn_norm_w"].reshape(1, GDN_HEAD_DIM)
    o_gdn, states = _gdn_chunks_fwd(qkv_n, bx, gx, proj, nw)

    ys = _mm("y_swa", o_swa, W["br_swa"], "nn")
    yg = _mm("y_gdn", o_gdn, W["br_gdn"], "nn")
    cwm = 512
    mix_rows = lambda grad: [_rowspec(proj, tb, cwm, P_GS // cwm, grad=grad, ncol=D_MODEL // cwm),
                             _rowspec(proj, tb, cwm, P_GG // cwm, grad=grad, ncol=D_MODEL // cwm),
                             _rowspec(ys, tb, cwm, 0, grad=grad, ncol=D_MODEL // cwm),
                             _rowspec(yg, tb, cwm, 0, grad=grad, ncol=D_MODEL // cwm)]
    mixed, = _rowmap("mix_fwd", fwd(_mix_fn), D_MODEL // cwm, T // tb, mix_rows(False), [], [full(D_MODEL, _CDT, tb, cwm)])
    r1 = _mm("r1", mixed, W["mix_o"], "nn", add=x, add_scale=ALPHA)

    def ln_fwd(name, r, g, b):
        return _rowmap(name, _ln_fwd_fn, 1, T // tb, [_rowspec(r, tb, D_MODEL, 0)], [_parspec(g), _parspec(b)],
                       [full(D_MODEL, F32, tb, D_MODEL), full(D_MODEL, _CDT, tb, D_MODEL)])

    def ln_bwd(name, r, g, b, ct):
        return _rowmap_bwd(name, _ln_fn, 1, T // tb, [_rowspec(r, tb, D_MODEL, 0, grad=True)],
                           [_parspec(g, grad=True), _parspec(b, grad=True)], [_rowspec(ct, tb, D_MODEL, 0)])

    g1, b1 = W["ln1_g"].reshape(1, -1), W["ln1_b"].reshape(1, -1)
    g2, b2 = W["ln2_g"].reshape(1, -1), W["ln2_b"].reshape(1, -1)
    g3, b3 = W["ln3_g"].reshape(1, -1), W["ln3_b"].reshape(1, -1)
    x1, x1b = ln_fwd("ln1_fwd", r1, g1, b1)

    qm = _mm("mem_q", x1b, W["mem_q"], "nn")
    kvm = _mm("mem_kv", mem, W["mem_kv"], "nn")
    ma_rows = lambda grad: [_rowspec(qm, tbl, MEM_HEAD_DIM, 0, grad=grad, ncol=MEM_HEADS)]
    ma_pars = lambda grad: [_parspec(kvm, MEM_HEAD_DIM, 0, grad=grad, ncol=MEM_HEADS),
                            _parspec(kvm, MEM_HEAD_DIM, MEM_HEADS, grad=grad, ncol=MEM_HEADS)]
    om, = _rowmap("memattn_fwd", fwd(_memattn_fn), MEM_HEADS, T // tbl, ma_rows(False), ma_pars(False),
                  [full(MEM_W, _CDT, tbl, MEM_HEAD_DIM)])
    r2 = _mm("r2", om, W["mem_o"], "nn", add=x1, add_scale=ALPHA)
    x2, x2b = ln_fwd("ln2_fwd", r2, g2, b2)

    hcat = _mm("ffn_up", x2b, W["up_p"], "nn")
    cwf = 512
    ncf = D_FF_PAD // cwf
    cw_p, cb_p = W["ffn_conv_w_p"], W["ffn_conv_b_p"]
    ffn_rows = lambda grad: [_rowspec(hcat, tb, cwf, 0, halo=SUBLANES, grad=grad, ncol=ncf),
                             _rowspec(hcat, tb, cwf, ncf, halo=SUBLANES, grad=grad, ncol=ncf)]
    ffn_pars = lambda grad: [_parspec(cw_p, cwf, 0, grad=grad, ncol=ncf), _parspec(cw_p, cwf, ncf, grad=grad, ncol=ncf),
                             _parspec(cb_p, cwf, 0, grad=grad, ncol=ncf), _parspec(cb_p, cwf, ncf, grad=grad, ncol=ncf)]
    act, = _rowmap("ffn_act_fwd", fwd(_ffn_act_fn), ncf, T // tb, ffn_rows(False), ffn_pars(False), [full(D_FF_PAD, _CDT, tb, cwf)])
    r3 = _mm("r3", act, W["down_p"], "nn", add=x2, add_scale=ALPHA)
    dr3, lacc, dg3, db3 = _rowmap("ln3_loss", _loss_fn, 1, T // tb, [_rowspec(r3, tb, D_MODEL, 0), _rowspec(tgt, tb, D_MODEL, 0)],
                                  [_parspec(g3), _parspec(b3)], [full(D_MODEL, F32, tb, D_MODEL)],
                                  accs=[(SUBLANES, LANES), (1, D_MODEL), (1, D_MODEL)])
    loss = lacc[0, 0]

    G = {}
    G["down_p"] = _mm("dw_down", act, dr3, "tn")
    dact = _mm("d_act", dr3, W["down_p"], "nt")
    dhg, dhu, dcwg, dcwu, dcbg, dcbu = _rowmap_bwd("ffn_act_bwd", _ffn_act_fn, ncf, T // tb, ffn_rows(True), ffn_pars(True),
                                                   [_rowspec(dact, tb, cwf, 0)])
    w_gate, w_upp = W["up_p"][:, :D_FF_PAD], W["up_p"][:, D_FF_PAD:]
    dx2 = _mm("dx2_gate", dhg, w_gate, "nt", add=dr3, add_scale=ALPHA)
    dx2 = _mm("dx2_up", dhu, w_upp, "nt", add=dx2)
    G["up_p"] = jnp.concatenate([_mm("dw_gate", x2b, dhg, "tn"), _mm("dw_up", x2b, dhu, "tn")], axis=1)
    G["ffn_conv_w"] = jnp.concatenate([dcwg[:, :D_FF], dcwu[:, :D_FF]], axis=1)
    G["ffn_conv_b"] = jnp.concatenate([dcbg[0, :D_FF], dcbu[0, :D_FF]])
    G["ln3_g"], G["ln3_b"] = dg3[0], db3[0]

    dr2, dg2, db2 = ln_bwd("ln2_bwd", r2, g2, b2, dx2)
    G["ln2_g"], G["ln2_b"] = dg2[0], db2[0]
    G["mem_o"] = _mm("dw_mem_o", om, dr2, "tn")
    dom = _mm("d_om", dr2, W["mem_o"], "nt")
    dqm, dkm, dvm = _rowmap_bwd("memattn_bwd", _memattn_fn, MEM_HEADS, T // tbl, ma_rows(True), ma_pars(True),
                                [_rowspec(dom, tbl, MEM_HEAD_DIM, 0)])
    G["mem_kv"] = _mm("dw_mem_kv", mem, jnp.concatenate([dkm, dvm], axis=1), "tn")
    G["mem_q"] = _mm("dw_mem_q", x1b, dqm, "tn")
    dx1 = _mm("dx1", dqm, W["mem_q"], "nt", add=dr2, add_scale=ALPHA)

    dr1, dg1, db1 = ln_bwd("ln1_bwd", r1, g1, b1, dx1)
    G["ln1_g"], G["ln1_b"] = dg1[0], db1[0]
    G["mix_o"] = _mm("dw_mix_o", mixed, dr1, "tn")
    dmixed = _mm("d_mixed", dr1, W["mix_o"], "nt")
    dgs, dgg, dys, dyg = _rowmap_bwd("mix_bwd", _mix_fn, D_MODEL // cwm, T // tb, mix_rows(True), [], [_rowspec(dmixed, tb, cwm, 0)])
    G["br_swa"] = _mm("dw_br_swa", o_swa, dys, "tn")
    G["br_gdn"] = _mm("dw_br_gdn", o_gdn, dyg, "tn")
    do_swa = _mm("d_o_swa", dys, W["br_swa"], "nt")
    do_gdn = _mm("d_o_gdn", dyg, W["br_gdn"], "nt")

    rows, pars, nblk = _swa_specs(qs, ks, vs, bp, bc, sk, True)
    m3 = lambda j, n: (j, n, 0)
    dqs, dks, dvs, dbp, dbc, dsk = _rowmap_bwd("swa_bwd", _swa_fn, SWA_KV_HEADS, nblk, rows, pars,
                                               [Row(_stack_heads(do_swa, nb), (None, SWA_GRP * SWA_BLOCK, SWA_HEAD_DIM), m3)])
    d_swa = jnp.concatenate([_unstack_heads(dqs, nb), dks.transpose(1, 0, 2).reshape(T, SWA_KV),
                             dvs.transpose(1, 0, 2).reshape(T, SWA_KV)], axis=1)
    dbias = jnp.concatenate([dbp.reshape(SWA_HEADS, half), dbc.reshape(SWA_HEADS, half)], axis=1)
    G["rel_bias"] = _mm("d_rel_bias", dbias, onehot_t.T, "nn", hi=True).T
    G["swa_sinks"] = _mm("d_sinks", dsk.reshape(SWA_HEADS, SWA_BLOCK), jnp.ones((SWA_BLOCK, LANES), F32), "nn", hi=True)[:, 0]

    dqkv_n, dbx, dgx, dz, dnw = _gdn_chunks_bwd(qkv_n, bx, gx, proj, nw, states, do_gdn)
    G["gdn_norm_w"] = dnw[0]
    dgba, dalog, ddtb = _rowmap_bwd("gdn_gate_bwd", _gdn_gate_fn, 1, T // tbl, gate_rows(True), gate_pars(True),
                                    [_rowspec(dbx, tbl, GDN_W, 0), _rowspec(dgx, tbl, GDN_W, 0)])
    G["gdn_a_log"], G["gdn_dt_bias"] = dalog[0, GDN_HEADS:2 * GDN_HEADS], ddtb[0, GDN_HEADS:2 * GDN_HEADS]
    dgqkv, dcw_gdn = _rowmap_bwd("gdn_pre_bwd", _gdn_pre_fn, ncq, T // tbl, pre_rows(True), pre_pars(True),
                                 [_rowspec(dqkv_n, tbl, LANES, 0)])
    G["gdn_conv_w"] = dcw_gdn

    dproj = jnp.concatenate([dgs, dgg, dgqkv, dz, d_swa, dgba], axis=1)
    dx = _mm("dx", dproj, W["in_p"], "nt", add=dr1, add_scale=ALPHA)
    G["in_p"] = _mm("dw_in", x, dproj, "tn")
    return loss, dx, G


W_NAMES = ["w_in", "rel_bias", "swa_sinks", "gdn_conv_w", "gdn_a_log", "gdn_dt_bias", "gdn_norm_w", "w_br_swa", "w_br_gdn",
           "w_mix_o", "ln1_g", "ln1_b", "w_mem_q", "w_mem_kv", "w_mem_o", "ln2_g", "ln2_b", "w_up", "ffn_conv_w", "ffn_conv_b",
           "w_down", "ln3_g", "ln3_b"]
BIG = ["w_in", "w_br_swa", "w_br_gdn", "w_mix_o", "w_mem_q", "w_mem_kv", "w_mem_o", "w_up", "w_down"]
SMALL = [n for n in W_NAMES if n not in BIG]
COL_SHARDED = ["w_in", "w_br_swa", "w_br_gdn", "w_mem_o", "w_up"]


def _pack(arrs):
    flat = []
    for a in arrs:
        f = a.reshape(-1).astype(F32)
        flat.append(jnp.pad(f, (0, (-f.shape[0]) % LANES)))
    f = jnp.concatenate(flat)
    f = jnp.pad(f, (0, (-f.shape[0]) % (16 * LANES)))
    return f.reshape(-1, LANES)


def _unpack(p, shapes):
    f, out, off = p.reshape(-1), [], 0
    for s in shapes:
        n = int(np.prod(s)) if len(s) else 1
        out.append(f[off:off + n].reshape(s))
        off += n + (-n) % LANES
    return out


def _merge_shards(d):
    cat = lambda names: jnp.concatenate([d[n] for n in names], axis=-2)
    return [d["w_in"], d["w_up"], cat(["w_br_swa", "w_br_gdn", "w_mem_q", "w_mem_o"]), cat(["w_mix_o", "w_down"]), d["w_mem_kv"]]


def _split_shards(ts):
    a, b, c, dd, e = ts
    return {"w_in": a, "w_up": b, "w_br_swa": c[..., 0:1024, :], "w_br_gdn": c[..., 1024:2048, :], "w_mem_q": c[..., 2048:2560, :],
            "w_mem_o": c[..., 2560:3072, :], "w_mix_o": dd[..., 0:512, :], "w_down": dd[..., 512:, :], "w_mem_kv": e}


def _to_full(name, t):
    if name in COL_SHARDED:
        return t.transpose(1, 0, 2).reshape(t.shape[1], 4 * t.shape[2])
    return t.reshape(4 * t.shape[1], t.shape[2])


def _to_chips(name, t):
    if name in COL_SHARDED:
        return t.reshape(t.shape[0], 4, t.shape[1] // 4).transpose(1, 0, 2)
    return t.reshape(4, t.shape[0] // 4, t.shape[1])


_IN_OFF = np.cumsum((0,) + IN_WIDTHS)


def _in_to_padded(w):
    o = _IN_OFF
    cut = lambda i, k: w[:, o[i]:o[k]]
    return jnp.concatenate([cut(9, 10), cut(10, 11), cut(3, 6), cut(6, 7), cut(0, 1), cut(1, 2), cut(2, 3), cut(7, 9),
                            jnp.zeros((w.shape[0], LANES - 2 * GDN_HEADS), w.dtype)], axis=1)


def _in_from_padded(p):
    return jnp.concatenate([p[:, P_SQ:P_SQ + SWA_Q], p[:, P_SK:P_SK + SWA_KV], p[:, P_SV:P_SV + SWA_KV], p[:, P_GQKV:P_GQKV + 3 * GDN_W],
                            p[:, P_GZ:P_GZ + GDN_W], p[:, P_BA:P_BA + 2 * GDN_HEADS], p[:, P_GS:P_GS + D_MODEL], p[:, P_GG:P_GG + D_MODEL]],
                           axis=1)


def _ff_pad(t, axis):
    g, u = jnp.split(t, 2, axis=axis)
    pad = [(0, 0)] * t.ndim
    pad[axis] = (0, D_FF_PAD - D_FF)
    return jnp.concatenate([jnp.pad(g, pad), jnp.pad(u, pad)], axis=axis)


def _ff_unpad(t, axis):
    g, u = jnp.split(t, 2, axis=axis)
    return jnp.concatenate([lax.slice_in_dim(g, 0, D_FF, axis=axis), lax.slice_in_dim(u, 0, D_FF, axis=axis)], axis=axis)


def _assemble_weights(full, small):
    W = dict(small)
    W["in_p"] = _in_to_padded(full["w_in"])
    W["up_p"] = _ff_pad(full["w_up"], 1)
    W["down_p"] = jnp.pad(full["w_down"], ((0, D_FF_PAD - D_FF), (0, 0)))
    W["br_swa"], W["br_gdn"], W["mix_o"] = full["w_br_swa"], full["w_br_gdn"], full["w_mix_o"]
    W["mem_q"], W["mem_kv"], W["mem_o"] = full["w_mem_q"], full["w_mem_kv"], full["w_mem_o"]
    W["ffn_conv_w_p"] = _ff_pad(small["ffn_conv_w"], 1)
    W["ffn_conv_b_p"] = _ff_pad(small["ffn_conv_b"].reshape(1, -1), 1)
    return W


def _full_grads(G):
    return {"w_in": _in_from_padded(G["in_p"]), "w_up": _ff_unpad(G["up_p"], 1), "w_down": G["down_p"][:D_FF],
            "w_br_swa": G["br_swa"], "w_br_gdn": G["br_gdn"], "w_mix_o": G["mix_o"], "w_mem_q": G["mem_q"],
            "w_mem_kv": G["mem_kv"], "w_mem_o": G["mem_o"]}


def kernel(x, mem, w_in, rel_bias, swa_sinks, gdn_conv_w, gdn_a_log, gdn_dt_bias, gdn_norm_w, w_br_swa, w_br_gdn, w_mix_o, ln1_g, ln1_b, w_mem_q, w_mem_kv, w_mem_o, ln2_g, ln2_b, w_up, ffn_conv_w, ffn_conv_b, w_down, ln3_g, ln3_b, loss_target, m_w_in, m_rel_bias, m_swa_sinks, m_gdn_conv_w, m_gdn_a_log, m_gdn_dt_bias, m_gdn_norm_w, m_w_br_swa, m_w_br_gdn, m_w_mix_o, m_ln1_g, m_ln1_b, m_w_mem_q, m_w_mem_kv, m_w_mem_o, m_ln2_g, m_ln2_b, m_w_up, m_ffn_conv_w, m_ffn_conv_b, m_w_down, m_ln3_g, m_ln3_b, v_w_in, v_rel_bias, v_swa_sinks, v_gdn_conv_w, v_gdn_a_log, v_gdn_dt_bias, v_gdn_norm_w, v_w_br_swa, v_w_br_gdn, v_w_mix_o, v_ln1_g, v_ln1_b, v_w_mem_q, v_w_mem_kv, v_w_mem_o, v_ln2_g, v_ln2_b, v_w_up, v_ffn_conv_w, v_ffn_conv_b, v_w_down, v_ln3_g, v_ln3_b):
    a = dict(locals())
    w = {n: a[n] for n in W_NAMES}
    m = {n: a["m_" + n] for n in W_NAMES}
    v = {n: a["v_" + n] for n in W_NAMES}
    chip = 2 * lax.axis_index("x") + lax.axis_index("y")
    core = lax.axis_index("c")
    sq = lambda t: t.reshape(t.shape[1:]) if (t.ndim > 1 and t.shape[0] == 1 and t is not rel_bias) else t

    shards = _merge_shards({n: sq(w[n]).astype(_CDT) for n in BIG})
    conv_pack = _pack([sq(ffn_conv_w), sq(gdn_conv_w)])
    gathered = _chip_allgather(shards + [conv_pack])
    full = {n: _to_full(n, t) for n, t in _split_shards(gathered[:-1]).items()}
    fcw_sh, gcw_sh = sq(ffn_conv_w).shape, sq(gdn_conv_w).shape
    conv = [_unpack(gathered[-1][k], [fcw_sh, gcw_sh]) for k in range(4)]
    small = {n: sq(w[n]) for n in SMALL}
    small["ffn_conv_w"] = jnp.concatenate([cv[0] for cv in conv], axis=1)
    small["gdn_conv_w"] = jnp.concatenate([cv[1] for cv in conv], axis=1)
    W = _assemble_weights(full, small)

    loss, dx, G = _fwd_bwd(x[0], mem[0], loss_target[0], W)

    small_names = SMALL
    small_shapes = [()] + [tuple(G[n].shape) for n in small_names]
    packed = _pack([loss] + [G[n] for n in small_names])
    allp = _allgather8(packed)
    tot = _addn("small_sum", [(allp, k) for k in range(8)])
    parts = _unpack(tot, small_shapes)
    loss_tot, gsmall = parts[0], dict(zip(small_names, parts[1:]))
    gsmall["ffn_conv_w"] = lax.dynamic_slice_in_dim(gsmall["ffn_conv_w"], chip * fcw_sh[1], fcw_sh[1], axis=1)
    gsmall["gdn_conv_w"] = lax.dynamic_slice_in_dim(gsmall["gdn_conv_w"], chip * gcw_sh[1], gcw_sh[1], axis=1)

    gfull = _full_grads(G)
    gch = _merge_shards({n: _to_chips(n, gfull[n]) for n in BIG})
    theirs = _pair_swap(gch)
    pair = []
    for t, (mine, got) in enumerate(zip(gch, theirs)):
        rh = mine.shape[1] // 2
        mine_h = lax.dynamic_slice_in_dim(mine, core * rh, rh, axis=1)
        pair.append(_addn(f"pair_sum{t}", [mine_h.reshape(4 * rh, -1), got.reshape(4 * rh, -1)]).reshape(4, rh, -1))
    others = _chip_scatter(pair)
    halves = []
    for t, (p, o) in enumerate(zip(pair, others)):
        own = lax.dynamic_index_in_dim(p, chip, 0, keepdims=False)
        halves.append(_addn(f"chip_sum{t}", [own, (o, 0), (o, 1), (o, 2)]))
    both = _pair_gather(halves)
    gbig = _split_shards([b.reshape(2 * b.shape[1], b.shape[2]) for b in both])

    outs = {}
    for n in BIG:
        d_, m_, v_ = _adamw("adamw_" + n, sq(w[n]), gbig[n], sq(m[n]), sq(v[n]))
        outs[n] = (gbig[n], d_, m_, v_)
    sm_shapes = [sq(w[n]).shape if n != "rel_bias" else w[n].shape for n in SMALL]
    sw, sg, sm_, sv = (_pack([sq(t[n]) if n != "rel_bias" else t[n] for n in SMALL]) for t in (w, gsmall, m, v))
    d_, m_, v_ = _adamw("adamw_small", sw, sg, sm_, sv)
    for n, g_, dd, mm_, vv in zip(SMALL, _unpack(sg, sm_shapes), _unpack(d_, sm_shapes), _unpack(m_, sm_shapes), _unpack(v_, sm_shapes)):
        outs[n] = (g_, dd, mm_, vv)

    res = [loss_tot.reshape(()), dx.reshape(x.shape)]
    for k in range(4):
        res += [outs[n][k].reshape(w[n].shape) for n in W_NAMES]
    return tuple(res)
```

```python
import functools
import math

import jax
import jax.numpy as jnp
import numpy as np
from jax import lax
from jax.experimental import pallas as pl
from jax.experimental.pallas import tpu as pltpu

F32 = jnp.float32
BF16 = jnp.bfloat16
_CDT = BF16
_GDT = BF16

D_MODEL = 2048
SWA_HEADS, SWA_KV_HEADS, SWA_HEAD_DIM, SWA_BLOCK = 16, 2, 64, 128
SWA_GRP = SWA_HEADS // SWA_KV_HEADS
REL_BUCKETS, REL_MAX_DIST = 32, 128
GDN_HEADS, GDN_HEAD_DIM, GDN_CONV, GDN_CHUNK = 8, 128, 4, 64
MEM_HEADS, MEM_HEAD_DIM = 4, 128
D_FF, D_FF_PAD, FFN_CONV = 5504, 5632, 3
SWA_Q, SWA_KV, GDN_W, MEM_W = 1024, 128, 1024, 512
IN_WIDTHS = (SWA_Q, SWA_KV, SWA_KV, GDN_W, GDN_W, GDN_W, GDN_W, GDN_HEADS, GDN_HEADS, D_MODEL, D_MODEL)
NORM_EPS = 1e-5
ALPHA = 2.0 ** 0.25
NEG_INF = -1e30
ADAM_LR, ADAM_B1, ADAM_B2, ADAM_EPS, ADAM_WD, ADAM_STEP = 0.001, 0.9, 0.999, 1e-08, 0.01, 10
LANES, SUBLANES = 128, 8
VMEM_LIMIT = 56 * 1024 * 1024

P_GS, P_GG, P_GQKV, P_GZ, P_SQ, P_SK, P_SV, P_BA, P_USED, P_END = 0, 2048, 4096, 7168, 8192, 9216, 9344, 9472, 9600, 9728


def _tile(dim, pref, align=LANES):
    if dim <= pref:
        return dim
    t = (pref // align) * align
    while t >= align:
        if dim % t == 0:
            return t
        t -= align
    return dim


_DIMS = {"nn": (((1,), (0,)), ((), ())), "nt": (((1,), (1,)), ((), ())), "tn": (((0,), (0,)), ((), ()))}
_BDIMS = {"nn": (((2,), (1,)), ((0,), (0,))), "nt": (((2,), (2,)), ((0,), (0,))), "tn": (((1,), (1,)), ((0,), (0,)))}


def _raw_dot(a, b, form, hi):
    dims = (_BDIMS if a.ndim == 3 else _DIMS)[form]
    if hi == "x3":
        a, b = a.astype(F32), b.astype(F32)
        ah, bh = a.astype(BF16), b.astype(BF16)
        al, bl = (a - ah.astype(F32)).astype(BF16), (b - bh.astype(F32)).astype(BF16)
        d = lambda p, q: lax.dot_general(p, q, dims, preferred_element_type=F32)
        return d(ah, bh) + (d(ah, bl) + d(al, bh))
    if hi:
        return lax.dot_general(a.astype(F32), b.astype(F32), dims, precision=lax.Precision.HIGHEST, preferred_element_type=F32)
    return lax.dot_general(a.astype(_CDT), b.astype(_CDT), dims, preferred_element_type=F32)


@functools.partial(jax.custom_vjp, nondiff_argnums=(2, 3))
def _dot(a, b, form, hi=False):
    return _raw_dot(a, b, form, hi)


def _dot_fwd(a, b, form, hi):
    return _raw_dot(a, b, form, hi), (a, b)


def _dot_bwd(form, hi, res, g):
    a, b = res
    if form == "nn":
        da, db = _raw_dot(g, b, "nt", hi), _raw_dot(a, g, "tn", hi)
    elif form == "nt":
        da, db = _raw_dot(g, b, "nn", hi), _raw_dot(g, a, "tn", hi)
    else:
        da, db = _raw_dot(b, g, "nt", hi), _raw_dot(a, g, "nn", hi)
    return da.astype(a.dtype), db.astype(b.dtype)


_dot.defvjp(_dot_fwd, _dot_bwd)


@functools.partial(jax.custom_vjp, nondiff_argnums=(1,))
def _shift_rows(x, k):
    return pltpu.roll(x, k, 0)


def _shift_rows_fwd(x, k):
    return pltpu.roll(x, k, 0), None


def _shift_rows_bwd(k, _, g):
    return (pltpu.roll(g, g.shape[0] - k, 0),)


_shift_rows.defvjp(_shift_rows_fwd, _shift_rows_bwd)


def _sigmoid(x):
    return 1.0 / (1.0 + jnp.exp(-x))


def _silu(x):
    return x * _sigmoid(x)


def _softplus(x):
    return jnp.maximum(x, 0.0) + jnp.log(1.0 + jnp.exp(-jnp.abs(x)))


def _iota(shape, axis):
    return lax.broadcasted_iota(jnp.int32, shape, axis)


def _cparams(sem):
    return pltpu.CompilerParams(dimension_semantics=sem, vmem_limit_bytes=VMEM_LIMIT)


def _mm(name, a, b, form, out_dtype=F32, add=None, add_scale=1.0, hi=False, tm=1024, tn=1024, tk=2048):
    if form == "nn":
        (M, K), (K2, N) = a.shape, b.shape
    elif form == "nt":
        (M, K), (N, K2) = a.shape, b.shape
    else:
        (K, M), (K2, N) = a.shape, b.shape
    assert K == K2, (name, a.shape, b.shape, form)
    tm, tn, tk = _tile(M, tm), _tile(N, tn), _tile(K, tk)
    nk = K // tk
    a_spec = pl.BlockSpec((tk, tm), lambda i, j, k: (k, i)) if form == "tn" else pl.BlockSpec((tm, tk), lambda i, j, k: (i, k))
    b_spec = pl.BlockSpec((tn, tk), lambda i, j, k: (j, k)) if form == "nt" else pl.BlockSpec((tk, tn), lambda i, j, k: (k, j))
    o_spec = pl.BlockSpec((tm, tn), lambda i, j, k: (i, j))
    has_add = add is not None

    def finish(r, c_ref, o_ref):
        if has_add:
            r = r + add_scale * c_ref[...].astype(F32)
        o_ref[...] = r.astype(out_dtype)

    def body(*refs):
        a_ref, b_ref = refs[:2]
        c_ref = refs[2] if has_add else None
        o_ref = refs[3] if has_add else refs[2]
        if nk == 1:
            finish(_raw_dot(a_ref[...], b_ref[...], form, hi), c_ref, o_ref)
            return
        acc = refs[-1]
        k = pl.program_id(2)

        @pl.when(k == 0)
        def _():
            acc[...] = jnp.zeros_like(acc)

        acc[...] += _raw_dot(a_ref[...], b_ref[...], form, hi)

        @pl.when(k == nk - 1)
        def _():
            finish(acc[...], c_ref, o_ref)

    ins = [a, b] + ([add] if has_add else [])
    specs = [a_spec, b_spec] + ([o_spec] if has_add else [])
    return pl.pallas_call(
        body, name=name, grid=(M // tm, N // tn, nk), in_specs=specs, out_specs=o_spec,
        out_shape=jax.ShapeDtypeStruct((M, N), out_dtype),
        scratch_shapes=[pltpu.VMEM((tm, tn), F32)] if nk > 1 else [],
        compiler_params=_cparams(("parallel", "parallel", "arbitrary")),
    )(*ins)


class Row:
    def __init__(self, arr, blk, imap, hblk=None, hmap=None, gshape=None, gmap=None, gdt=(F32,)):
        self.arr, self.blk, self.imap, self.hblk, self.hmap, self.gshape, self.gmap = arr, blk, imap, hblk, hmap, gshape, gmap
        self.gdt = gdt


class Par:
    def __init__(self, arr, blk=None, imap=None, gshape=None, gmap=None):
        self.arr = arr
        self.blk = tuple(arr.shape) if blk is None else blk
        nd = len(self.blk)
        self.imap = (lambda j: (0,) * nd) if imap is None else imap
        self.gshape, self.gmap = gshape, gmap


class Out:
    def __init__(self, shape, dtype, blk, imap):
        self.shape, self.dtype, self.blk, self.imap = shape, dtype, blk, imap


def _rows_of(blk):
    return [d for d in blk if d is not None][0]


def _rowmap(name, fn, ncol, nblk, rows, pars, outs, accs=()):
    in_specs, ins = [], []
    for r in rows:
        ins.append(r.arr)
        in_specs.append(pl.BlockSpec(r.blk, r.imap))
        if r.hblk is not None:
            ins.append(r.arr)
            in_specs.append(pl.BlockSpec(r.hblk, r.hmap))
    for p in pars:
        ins.append(p.arr)
        in_specs.append(pl.BlockSpec(p.blk, (lambda im: (lambda j, n: im(j)))(p.imap)))
    out_specs = [pl.BlockSpec(o.blk, o.imap) for o in outs]
    out_shape = [jax.ShapeDtypeStruct(o.shape, o.dtype) for o in outs]
    for a in accs:
        out_specs.append(pl.BlockSpec(a, (lambda nd: (lambda j, n: (0,) * nd))(len(a))))
        out_shape.append(jax.ShapeDtypeStruct(a, F32))
    n_in = len(ins)

    def body(*refs):
        j, n = pl.program_id(0), pl.program_id(1)
        it = iter(refs[:n_in])
        rvals = []
        for r in rows:
            cur = next(it)[...]
            rvals.append((next(it)[...], cur) if r.hblk is not None else cur)
        pvals = [next(it)[...] for _ in pars]
        o_refs = refs[n_in:n_in + len(outs)]
        a_refs = refs[n_in + len(outs):]
        ovals, avals = fn(j, n == 0, rvals, pvals)
        for ref, v in zip(o_refs, ovals):
            ref[...] = v.astype(ref.dtype)
        if accs:
            @pl.when((j == 0) & (n == 0))
            def _():
                for ref in a_refs:
                    ref[...] = jnp.zeros_like(ref)
            for ref, v in zip(a_refs, avals):
                ref[...] += v

    res = pl.pallas_call(
        body, name=name, grid=(ncol, nblk), in_specs=in_specs, out_specs=out_specs, out_shape=out_shape,
        compiler_params=_cparams(("arbitrary", "arbitrary")),
    )(*ins)
    return res


def _rowmap_bwd(name, fn, ncol, nblk, rows, pars, cts):
    rev = lambda im: (lambda j, s: im(j, nblk - 1 - s))
    in_specs, ins = [], []
    for r in rows:
        ins.append(r.arr)
        in_specs.append(pl.BlockSpec(r.blk, rev(r.imap)))
        if r.hblk is not None:
            ins.append(r.arr)
            in_specs.append(pl.BlockSpec(r.hblk, rev(r.hmap)))
    for p in pars:
        ins.append(p.arr)
        in_specs.append(pl.BlockSpec(p.blk, (lambda im: (lambda j, s: im(j)))(p.imap)))
    for c in cts:
        ins.append(c.arr)
        in_specs.append(pl.BlockSpec(c.blk, rev(c.imap)))
    n_in = len(ins)
    drows = [i for i, r in enumerate(rows) if r.gshape is not None]
    dpars = [i for i, p in enumerate(pars) if p.gshape is not None]
    out_specs, out_shape, scratch = [], [], []
    for i in drows:
        r = rows[i]
        for dt in r.gdt:
            out_specs.append(pl.BlockSpec(r.blk, rev(r.gmap)))
            out_shape.append(jax.ShapeDtypeStruct(r.gshape, dt))
        if r.hblk is not None:
            scratch.append(pltpu.VMEM(tuple(d for d in r.hblk if d is not None), F32))
    n_drow_out = len(out_specs)
    for i in dpars:
        p = pars[i]
        out_specs.append(pl.BlockSpec(p.blk, (lambda im: (lambda j, s: im(j)))(p.gmap)))
        out_shape.append(jax.ShapeDtypeStruct(p.gshape, F32))

    def body(*refs):
        j, s = pl.program_id(0), pl.program_id(1)
        first = s == nblk - 1
        it = iter(refs[:n_in])
        rvals = []
        for r in rows:
            cur = next(it)[...]
            rvals.append((next(it)[...], cur) if r.hblk is not None else cur)
        pvals = [next(it)[...] for _ in pars]
        cvals = [next(it)[...].astype(F32) for _ in cts]
        g_refs = iter(refs[n_in:n_in + n_drow_out])
        p_refs = refs[n_in + n_drow_out:n_in + n_drow_out + len(dpars)]
        carries = iter(refs[n_in + n_drow_out + len(dpars):])

        def f(dr, dp):
            rv, pv = list(rvals), list(pvals)
            for i, v in zip(drows, dr):
                rv[i] = v
            for i, v in zip(dpars, dp):
                pv[i] = v
            return fn(j, first, rv, pv)

        _, vjp = jax.vjp(f, [rvals[i] for i in drows], [pvals[i] for i in dpars])
        g_r, g_p = vjp(cvals)
        for i, g in zip(drows, g_r):
            r = rows[i]
            if r.hblk is None:
                for _ in r.gdt:
                    ref = next(g_refs)
                    ref[...] = g.astype(ref.dtype)
            else:
                g_prev, g_cur = g
                carry = next(carries)
                nr, nh = g_cur.shape[0], g_prev.shape[0]
                tail = g_cur[nr - nh:nr] + jnp.where(s > 0, carry[...], 0.0)
                for _ in r.gdt:
                    ref = next(g_refs)
                    if nr > nh:
                        ref[0:nr - nh, :] = g_cur[0:nr - nh].astype(ref.dtype)
                    ref[nr - nh:nr, :] = tail.astype(ref.dtype)
                carry[...] = g_prev
        for ref, g in zip(p_refs, g_p):
            @pl.when(s == 0)
            def _():
                ref[...] = jnp.zeros_like(ref)
            ref[...] += g

    return pl.pallas_call(
        body, name=name, grid=(ncol, nblk), in_specs=in_specs, out_specs=out_specs, out_shape=out_shape,
        scratch_shapes=scratch, compiler_params=_cparams(("arbitrary", "arbitrary")),
    )(*ins)


def _rowspec(arr, tb, cw, c0, cstep=1, halo=0, grad=False, ncol=1, gdt=(F32,)):
    T = arr.shape[0]
    imap = lambda j, n: (n, c0 + cstep * j)
    hblk = hmap = None
    if halo:
        q = tb // halo
        hblk, hmap = (halo, cw), (lambda j, n: (jnp.maximum(n * q - 1, 0), c0 + cstep * j))
    gshape = (T, cw * (ncol if cstep else 1)) if grad else None
    gmap = (lambda j, n: (n, cstep * j)) if grad else None
    return Row(arr, (tb, cw), imap, hblk, hmap, gshape, gmap, gdt)


def _parspec(arr, cw=None, c0=0, grad=False, ncol=1):
    if cw is None:
        return Par(arr, gshape=tuple(arr.shape) if grad else None,
                   gmap=(lambda nd: (lambda j: (0,) * nd))(arr.ndim) if grad else None)
    r = arr.shape[0]
    return Par(arr, (r, cw), lambda j: (0, c0 + j), (r, cw * ncol) if grad else None, (lambda j: (0, j)) if grad else None)


def _ln(r, g, b):
    mu = jnp.mean(r, axis=-1, keepdims=True)
    xc = r - mu
    var = jnp.mean(xc * xc, axis=-1, keepdims=True)
    return xc * lax.rsqrt(var + NORM_EPS) * g + b


def _ln_fn(j, first, rv, pv):
    return [_ln(rv[0], pv[0], pv[1])]


def _ln_fwd_fn(j, first, rv, pv):
    y = _ln(rv[0], pv[0], pv[1])
    return [y, y], []


def _loss_fn(j, first, rv, pv):
    r3, tgt = rv
    g, b = pv
    y, vjp = jax.vjp(_ln, r3, g, b)
    diff = y - tgt
    part = 0.5 * jnp.sum(diff * diff) / D_MODEL
    dr, dg, db = vjp(diff * (1.0 / D_MODEL))
    return [dr, dr], [jnp.full((SUBLANES, LANES), part, F32), dg, db]


def _mix_fn(j, first, rv, pv):
    gs, gg, ys, yg = rv
    return [_sigmoid(gs) * ys + _sigmoid(gg) * yg]


def _row_pick(x, i):
    ax = x.ndim - 2
    return jnp.sum(jnp.where(_iota(x.shape, ax) == i, x, 0.0), axis=ax, keepdims=True)


def _causal_conv(prev, cur, w, first):
    width = w.shape[0]
    nh = prev.shape[0]
    row = _iota(cur.shape, 0)
    prev = jnp.where(first, 0.0, prev)
    y = cur * _row_pick(w, width - 1)
    for d in range(1, width):
        sh = _shift_rows(cur, d)
        for t in range(d):
            sh = jnp.where(row == t, _row_pick(prev, nh - d + t), sh)
        y = y + sh * _row_pick(w, width - 1 - d)
    return y


def _ffn_act_fn(j, first, rv, pv):
    (pg, cg), (pu, cu) = rv
    wg, wu, bg, bu = pv
    hg = _causal_conv(pg, cg, wg, first) + bg
    hu = _causal_conv(pu, cu, wu, first) + bu
    return [_silu(hg) * hu]


def _gdn_pre_fn(j, first, rv, pv):
    (prev, cur), = rv
    w, = pv
    t = _silu(_causal_conv(prev, cur, w, first))
    tn = t * lax.rsqrt(jnp.sum(t * t, axis=-1, keepdims=True) + 1e-6)
    return [jnp.where(j < 2 * GDN_HEADS, tn, t)]


def _gdn_gate_fn(j, first, rv, pv):
    gba, = rv
    alog, dtb, eb, eg = pv
    tb = gba.shape[0]
    beta = _sigmoid(gba)
    g = -jnp.exp(alog) * _softplus(gba + dtb)
    ri, ci = _iota((tb, tb), 0), _iota((tb, tb), 1)
    tril = jnp.where((ri // GDN_CHUNK == ci // GDN_CHUNK) & (ci <= ri), 1.0, 0.0)
    gc = _dot(tril, g, "nn", True)
    return [_dot(beta, eb, "nn", True), _dot(gc, eg, "nn", True)]


def _swa_fn(j, first, rv, pv):
    q, (kp, kc), (vp, vc) = rv
    bp, bc, sk = pv
    sp = _dot(q, kp, "nt") * (SWA_HEAD_DIM ** -0.5) + bp
    sc = _dot(q, kc, "nt") * (SWA_HEAD_DIM ** -0.5) + bc
    qi = _iota(sp.shape, 0) % SWA_BLOCK
    kj = _iota(sp.shape, 1)
    sp = jnp.where((kj > qi) & jnp.logical_not(first), sp, NEG_INF)
    sc = jnp.where(kj <= qi, sc, NEG_INF)
    m = jnp.maximum(jnp.maximum(jnp.max(sp, axis=-1, keepdims=True), jnp.max(sc, axis=-1, keepdims=True)), sk)
    m = lax.stop_gradient(m)
    ep, ec, es = jnp.exp(sp - m), jnp.exp(sc - m), jnp.exp(sk - m)
    inv = 1.0 / (jnp.sum(ep, axis=-1, keepdims=True) + jnp.sum(ec, axis=-1, keepdims=True) + es)
    vp = jnp.where(first, 0.0, vp)
    return [_dot(ep * inv, vp, "nn") + _dot(ec * inv, vc, "nn")]


def _memattn_fn(j, first, rv, pv):
    q, = rv
    k, v = pv
    s = _dot(q, k, "nt") * (MEM_HEAD_DIM ** -0.5)
    m = lax.stop_gradient(jnp.max(s, axis=-1, keepdims=True))
    e = jnp.exp(s - m)
    p = e / jnp.sum(e, axis=-1, keepdims=True)
    return [_dot(p, v, "nn")]


SOLVE_PREC = "x3"


def _gdn_heads(q, k, v, bx, gx, g64, z, nw, S):
    c = GDN_CHUNK
    q = q * (GDN_HEAD_DIM ** -0.5)
    kb, vb = k * bx, v * bx
    ri, ci = _iota((1, c, c), 1), _iota((1, c, c), 2)
    tril, strict, eye = ci <= ri, ci < ri, ci == ri
    grow = jnp.sum(jnp.where(eye, g64, 0.0), axis=1, keepdims=True)
    decay = jnp.where(tril, jnp.exp(jnp.where(tril, g64 - grow, 0.0)), 0.0)
    a = jnp.where(strict, _dot(kb, k, "nt") * decay, 0.0)
    tinv = jnp.where(eye, 1.0, 0.0) - a
    x = _dot(a, a, "nn", SOLVE_PREC)
    for i in range(5):
        tinv = tinv + _dot(tinv, x, "nn", SOLVE_PREC)
        if i < 4:
            x = _dot(x, x, "nn", SOLVE_PREC)
    eg = jnp.exp(gx)
    u = _dot(tinv, vb, "nn", SOLVE_PREC)
    w = _dot(tinv, kb * eg, "nn", SOLVE_PREC)
    ai = jnp.where(tril, _dot(q, k, "nt") * decay, 0.0)
    glast = _row_pick(gx, c - 1)
    v_new = u - _dot(w, S, "nn")
    o = _dot(q * eg, S, "nn") + _dot(ai, v_new, "nn")
    s_new = S * jnp.exp(glast) + _dot(k * jnp.exp(glast - gx), v_new, "tn")
    o = o * lax.rsqrt(jnp.mean(o * o, axis=-1, keepdims=True) + 1e-6) * nw
    return o * _silu(z), s_new


def _head_major(ref, off, width=GDN_HEAD_DIM):
    return jnp.stack([ref[:, off + h * GDN_HEAD_DIM:off + h * GDN_HEAD_DIM + width] for h in range(GDN_HEADS)])


def _gdn_chunks_fwd(qkv, bx, gx, proj, nw):
    T = qkv.shape[0]
    nc, c, hd, nh = T // GDN_CHUNK, GDN_CHUNK, GDN_HEAD_DIM, GDN_HEADS

    def body(qkv_ref, bx_ref, gx_ref, z_ref, nw_ref, y_ref, st_ref, S):
        @pl.when(pl.program_id(0) == 0)
        def _():
            S[...] = jnp.zeros_like(S)

        s_old = S[...]
        st_ref[...] = s_old
        y, s_new = _gdn_heads(_head_major(qkv_ref, 0), _head_major(qkv_ref, GDN_W), _head_major(qkv_ref, 2 * GDN_W),
                              _head_major(bx_ref, 0), _head_major(gx_ref, 0), _head_major(gx_ref, 0, c), _head_major(z_ref, 0),
                              nw_ref[...], s_old)
        for h in range(nh):
            y_ref[:, h * hd:(h + 1) * hd] = y[h].astype(y_ref.dtype)
        S[...] = s_new

    row = lambda w, cb: pl.BlockSpec((c, w), lambda n: (n, cb))
    return pl.pallas_call(
        body, name="gdn_chunks_fwd", grid=(nc,),
        in_specs=[row(3 * GDN_W, 0), row(GDN_W, 0), row(GDN_W, 0), row(GDN_W, P_GZ // GDN_W), pl.BlockSpec((1, hd), lambda n: (0, 0))],
        out_specs=[row(GDN_W, 0), pl.BlockSpec((None, nh, hd, hd), lambda n: (n, 0, 0, 0))],
        out_shape=[jax.ShapeDtypeStruct((T, GDN_W), BF16), jax.ShapeDtypeStruct((nc, nh, hd, hd), F32)],
        scratch_shapes=[pltpu.VMEM((nh, hd, hd), F32)],
        compiler_params=_cparams(("arbitrary",)),
    )(qkv, bx, gx, proj, nw)


def _gdn_chunks_bwd(qkv, bx, gx, proj, nw, states, dy):
    T = qkv.shape[0]
    nc, c, hd, nh = T // GDN_CHUNK, GDN_CHUNK, GDN_HEAD_DIM, GDN_HEADS

    def body(qkv_ref, bx_ref, gx_ref, z_ref, nw_ref, st_ref, dy_ref, dqkv_ref, dbx_ref, dgx_ref, dz_ref, dnw_ref, dS):
        @pl.when(pl.program_id(0) == 0)
        def _():
            dS[...] = jnp.zeros_like(dS)
            dnw_ref[...] = jnp.zeros_like(dnw_ref)

        args = (_head_major(qkv_ref, 0), _head_major(qkv_ref, GDN_W), _head_major(qkv_ref, 2 * GDN_W), _head_major(bx_ref, 0),
                _head_major(gx_ref, 0), _head_major(gx_ref, 0, c), _head_major(z_ref, 0), nw_ref[...], st_ref[...])
        _, vjp = jax.vjp(_gdn_heads, *args)
        dq, dk, dv, dbx, dgx, dg64, dz, dnw, dsp = vjp((_head_major(dy_ref, 0), dS[...]))
        for h in range(nh):
            sl = slice(h * hd, (h + 1) * hd)
            dqkv_ref[:, sl] = dq[h].astype(dqkv_ref.dtype)
            dqkv_ref[:, GDN_W + h * hd:GDN_W + (h + 1) * hd] = dk[h].astype(dqkv_ref.dtype)
            dqkv_ref[:, 2 * GDN_W + h * hd:2 * GDN_W + (h + 1) * hd] = dv[h].astype(dqkv_ref.dtype)
            dbx_ref[:, sl] = dbx[h]
            dgx_ref[:, sl] = dgx[h]
            dgx_ref[:, h * hd:h * hd + c] += dg64[h]
            dz_ref[:, sl] = dz[h].astype(dz_ref.dtype)
        dnw_ref[...] += dnw
        dS[...] = dsp

    row = lambda w, cb: pl.BlockSpec((c, w), lambda s: (nc - 1 - s, cb))
    return pl.pallas_call(
        body, name="gdn_chunks_bwd", grid=(nc,),
        in_specs=[row(3 * GDN_W, 0), row(GDN_W, 0), row(GDN_W, 0), row(GDN_W, P_GZ // GDN_W), pl.BlockSpec((1, hd), lambda s: (0, 0)),
                  pl.BlockSpec((None, nh, hd, hd), lambda s: (nc - 1 - s, 0, 0, 0)), row(GDN_W, 0)],
        out_specs=[row(3 * GDN_W, 0), row(GDN_W, 0), row(GDN_W, 0), row(GDN_W, 0), pl.BlockSpec((1, hd), lambda s: (0, 0))],
        out_shape=[jax.ShapeDtypeStruct((T, 3 * GDN_W), F32), jax.ShapeDtypeStruct((T, GDN_W), F32),
                   jax.ShapeDtypeStruct((T, GDN_W), F32), jax.ShapeDtypeStruct((T, GDN_W), _CDT), jax.ShapeDtypeStruct((1, hd), F32)],
        scratch_shapes=[pltpu.VMEM((nh, hd, hd), F32)],
        compiler_params=_cparams(("arbitrary",)),
    )(qkv, bx, gx, proj, nw, states, dy)


def _adamw(name, w, g, m, v):
    R, C = w.shape
    tr = _tile(R, 128, SUBLANES)

    def body(w_ref, g_ref, m_ref, v_ref, d_ref, m2_ref, v2_ref):
        g_ = g_ref[...]
        m2 = ADAM_B1 * m_ref[...] + (1.0 - ADAM_B1) * g_
        v2 = ADAM_B2 * v_ref[...] + (1.0 - ADAM_B2) * (g_ * g_)
        m_hat = m2 / (1.0 - ADAM_B1 ** ADAM_STEP)
        v_hat = v2 / (1.0 - ADAM_B2 ** ADAM_STEP)
        d_ref[...] = -ADAM_LR * (m_hat / (jnp.sqrt(v_hat) + ADAM_EPS) + ADAM_WD * w_ref[...])
        m2_ref[...] = m2
        v2_ref[...] = v2

    spec = pl.BlockSpec((tr, C), lambda i: (i, 0))
    return pl.pallas_call(
        body, name=name, grid=(R // tr,), in_specs=[spec] * 4, out_specs=[spec] * 3,
        out_shape=[jax.ShapeDtypeStruct((R, C), F32)] * 3, compiler_params=_cparams(("parallel",)),
    )(w, g, m, v)


def _addn(name, parts, out_dtype=F32):
    parts = [p if isinstance(p, tuple) else (p, None) for p in parts]
    a0, k0 = parts[0]
    R, C = a0.shape[-2:]
    tr = _tile(R, 256, 2 * SUBLANES)
    specs = []
    for a, k in parts:
        if k is None:
            specs.append(pl.BlockSpec((tr, C), lambda i: (i, 0)))
        else:
            specs.append(pl.BlockSpec((None, tr, C), (lambda kk: (lambda i: (kk, i, 0)))(k)))

    def body(*refs):
        acc = refs[0][...].astype(F32)
        for r in refs[1:-1]:
            acc = acc + r[...].astype(F32)
        refs[-1][...] = acc.astype(out_dtype)

    return pl.pallas_call(
        body, name=name, grid=(R // tr,), in_specs=specs, out_specs=pl.BlockSpec((tr, C), lambda i: (i, 0)),
        out_shape=jax.ShapeDtypeStruct((R, C), out_dtype), compiler_params=_cparams(("parallel",)),
    )(*[a for a, _ in parts])


MESH = pl.DeviceIdType.MESH
_HBM = pl.BlockSpec(memory_space=pltpu.HBM)


def _place():
    x, y, c = lax.axis_index("x"), lax.axis_index("y"), lax.axis_index("c")
    return x, y, c, [(1 - x, y), (x, 1 - y), (1 - x, 1 - y)]


def _chip_allgather(ts):
    nt = len(ts)

    def body(*refs):
        ins, outs = refs[:nt], refs[nt:2 * nt]
        send, recv, lsem = refs[2 * nt:]
        x, y, c, rel = _place()
        me, sib = 2 * x + y, (x, y, 1 - c)

        def half(t, hc):
            rh = ts[t].shape[0] // 2
            return pl.ds(pl.multiple_of(hc * rh, 16), rh)

        def rcopy(t, k, src, dst, to):
            return pltpu.make_async_remote_copy(src_ref=src, dst_ref=dst, send_sem=send.at[6 * t + k], recv_sem=recv.at[6 * t + k],
                                                device_id=to, device_id_type=MESH)

        started = []
        for t in range(nt):
            own = pltpu.make_async_copy(ins[t], outs[t].at[me], lsem.at[t])
            own.start()
            started.append(own)
        sends = []
        for t in range(nt):
            for r, (px, py) in enumerate(rel):
                cp = rcopy(t, r, ins[t].at[half(t, c)], outs[t].at[me, half(t, c)], (px, py, c))
                cp.start()
                sends.append(cp)
        for t in range(nt):
            for r, (px, py) in enumerate(rel):
                got = outs[t].at[2 * px + py, half(t, c)]
                rcopy(t, r, got, got, (px, py, c)).wait_recv()
                fw = rcopy(t, 3 + r, got, got, sib)
                fw.start()
                sends.append(fw)
        for t in range(nt):
            for r, (px, py) in enumerate(rel):
                got = outs[t].at[2 * px + py, half(t, 1 - c)]
                rcopy(t, 3 + r, got, got, sib).wait_recv()
        for cp in sends:
            cp.wait_send()
        for own in started:
            own.wait()

    return pl.pallas_call(
        body, name="chip_allgather", in_specs=[_HBM] * nt, out_specs=[_HBM] * nt,
        out_shape=[jax.ShapeDtypeStruct((4,) + tuple(t.shape), t.dtype) for t in ts],
        scratch_shapes=[pltpu.SemaphoreType.DMA((6 * nt,)), pltpu.SemaphoreType.DMA((6 * nt,)), pltpu.SemaphoreType.DMA((nt,))],
    )(*ts)


def _pair_swap(ts):
    nt = len(ts)

    def body(*refs):
        ins, outs = refs[:nt], refs[nt:2 * nt]
        send, recv = refs[2 * nt:]
        x, y, c, _ = _place()
        cps = []
        for t in range(nt):
            rh = ts[t].shape[1] // 2
            src = ins[t].at[:, pl.ds(pl.multiple_of((1 - c) * rh, 16), rh), :]
            cp = pltpu.make_async_remote_copy(src_ref=src, dst_ref=outs[t], send_sem=send.at[t], recv_sem=recv.at[t],
                                              device_id=(x, y, 1 - c), device_id_type=MESH)
            cp.start()
            cps.append(cp)
        for cp in cps:
            cp.wait()

    return pl.pallas_call(
        body, name="pair_swap", in_specs=[_HBM] * nt, out_specs=[_HBM] * nt,
        out_shape=[jax.ShapeDtypeStruct((4, t.shape[1] // 2, t.shape[2]), t.dtype) for t in ts],
        scratch_shapes=[pltpu.SemaphoreType.DMA((nt,)), pltpu.SemaphoreType.DMA((nt,))],
    )(*ts)


def _chip_scatter(ps):
    nt = len(ps)

    def body(*refs):
        ins, outs = refs[:nt], refs[nt:2 * nt]
        send, recv = refs[2 * nt:]
        x, y, c, rel = _place()
        cps = []
        for t in range(nt):
            for r, (px, py) in enumerate(rel):
                cp = pltpu.make_async_remote_copy(src_ref=ins[t].at[2 * px + py], dst_ref=outs[t].at[r], send_sem=send.at[3 * t + r],
                                                  recv_sem=recv.at[3 * t + r], device_id=(px, py, c), device_id_type=MESH)
                cp.start()
                cps.append(cp)
        for cp in cps:
            cp.wait()

    return pl.pallas_call(
        body, name="chip_scatter", in_specs=[_HBM] * nt, out_specs=[_HBM] * nt,
        out_shape=[jax.ShapeDtypeStruct((3,) + tuple(p.shape[1:]), p.dtype) for p in ps],
        scratch_shapes=[pltpu.SemaphoreType.DMA((3 * nt,)), pltpu.SemaphoreType.DMA((3 * nt,))],
    )(*ps)


def _pair_gather(gs):
    nt = len(gs)

    def body(*refs):
        ins, outs = refs[:nt], refs[nt:2 * nt]
        send, recv, lsem = refs[2 * nt:]
        x, y, c, _ = _place()
        cps, own = [], []
        for t in range(nt):
            o = pltpu.make_async_copy(ins[t], outs[t].at[c], lsem.at[t])
            o.start()
            own.append(o)
            cp = pltpu.make_async_remote_copy(src_ref=ins[t], dst_ref=outs[t].at[c], send_sem=send.at[t], recv_sem=recv.at[t],
                                              device_id=(x, y, 1 - c), device_id_type=MESH)
            cp.start()
            cps.append(cp)
        for t in range(nt):
            cps[t].wait_send()
            got = outs[t].at[1 - c]
            pltpu.make_async_remote_copy(src_ref=got, dst_ref=got, send_sem=send.at[t], recv_sem=recv.at[t],
                                         device_id=(x, y, 1 - c), device_id_type=MESH).wait_recv()
            own[t].wait()

    return pl.pallas_call(
        body, name="pair_gather", in_specs=[_HBM] * nt, out_specs=[_HBM] * nt,
        out_shape=[jax.ShapeDtypeStruct((2,) + tuple(g.shape), g.dtype) for g in gs],
        scratch_shapes=[pltpu.SemaphoreType.DMA((nt,)), pltpu.SemaphoreType.DMA((nt,)), pltpu.SemaphoreType.DMA((nt,))],
    )(*gs)


def _allgather8(v):
    m, n = v.shape

    def body(x_ref, out_ref, send, recv, lsem):
        x, y, c, rel = _place()
        me, sib = (x, y, c), (x, y, 1 - c)

        def blk(px, py, pc):
            return out_ref.at[4 * px + 2 * py + pc]

        def copy(k, block, to, src=None):
            return pltpu.make_async_remote_copy(src_ref=blk(*block) if src is None else src, dst_ref=blk(*block), send_sem=send.at[k],
                                                recv_sem=recv.at[k], device_id=to, device_id_type=MESH)

        mine = pltpu.make_async_copy(x_ref, blk(*me), lsem)
        mine.start()
        first = [copy(0, me, sib, src=x_ref)] + [copy(1 + r, me, (*ch, c), src=x_ref) for r, ch in enumerate(rel)]
        for cp in first:
            cp.start()
        passed = [copy(4 + r, (*ch, c), sib) for r, ch in enumerate(rel)]
        for r, ch in enumerate(rel):
            copy(1 + r, (*ch, c), me).wait_recv()
            passed[r].start()
        copy(0, sib, me).wait_recv()
        for r, ch in enumerate(rel):
            copy(4 + r, (*ch, 1 - c), me).wait_recv()
        for cp in first + passed:
            cp.wait_send()
        mine.wait()

    return pl.pallas_call(
        body, name="allgather8", in_specs=[pl.BlockSpec(memory_space=pltpu.VMEM)], out_specs=pl.BlockSpec(memory_space=pltpu.VMEM),
        out_shape=jax.ShapeDtypeStruct((8, m, n), v.dtype),
        scratch_shapes=[pltpu.SemaphoreType.DMA((7,)), pltpu.SemaphoreType.DMA((7,)), pltpu.SemaphoreType.DMA],
    )(v)


def _t5_bucket(dist):
    max_exact = REL_BUCKETS // 2
    d = jnp.maximum(dist, 1).astype(F32)
    large = max_exact + (jnp.log(d / max_exact) / math.log(REL_MAX_DIST / max_exact) * (REL_BUCKETS - max_exact)).astype(jnp.int32)
    large = jnp.minimum(large, REL_BUCKETS - 1)
    return jnp.where(dist < max_exact, dist, large)


def _bias_onehot():
    qi = jnp.arange(SWA_BLOCK)[:, None]
    kj = jnp.arange(SWA_BLOCK)[None, :]
    dist = jnp.concatenate([(qi + SWA_BLOCK - kj).reshape(-1), (qi - kj).reshape(-1)])
    bucket = _t5_bucket(jnp.maximum(dist, 0))
    return (bucket[None, :] == jnp.arange(REL_BUCKETS)[:, None]).astype(F32)


def _head_spread():
    lane = jnp.arange(LANES)[:, None]
    head = jnp.arange(GDN_W)[None, :] // GDN_HEAD_DIM
    return (lane == head).astype(F32), (lane == head + GDN_HEADS).astype(F32)


def _lane16(v8):
    return jnp.pad(v8.astype(F32), (GDN_HEADS, LANES - 2 * GDN_HEADS)).reshape(1, LANES)


def _stack_heads(t, nb):
    return t.reshape(nb, SWA_BLOCK, SWA_KV_HEADS, SWA_GRP, SWA_HEAD_DIM).transpose(2, 0, 3, 1, 4).reshape(
        SWA_KV_HEADS, nb * SWA_GRP * SWA_BLOCK, SWA_HEAD_DIM)


def _unstack_heads(t, nb):
    return t.reshape(SWA_KV_HEADS, nb, SWA_GRP, SWA_BLOCK, SWA_HEAD_DIM).transpose(1, 3, 0, 2, 4).reshape(nb * SWA_BLOCK, SWA_Q)


def _kv_heads(t):
    return t.reshape(t.shape[0], SWA_KV_HEADS, SWA_HEAD_DIM).transpose(1, 0, 2)


def _swa_specs(qs, ks, vs, bp, bc, sk, grad, gdt=(F32,)):
    T = ks.shape[1]
    qr = SWA_GRP * SWA_BLOCK
    g = lambda a: tuple(a.shape) if grad else None
    m3 = lambda j, n: (j, n, 0)
    h3 = lambda j, n: (j, jnp.maximum(n - 1, 0), 0)
    p3 = lambda j: (j, 0, 0)
    rows = [Row(qs, (None, qr, SWA_HEAD_DIM), m3, gshape=g(qs), gmap=m3, gdt=gdt),
            Row(ks, (None, SWA_BLOCK, SWA_HEAD_DIM), m3, (None, SWA_BLOCK, SWA_HEAD_DIM), h3, g(ks), m3, gdt),
            Row(vs, (None, SWA_BLOCK, SWA_HEAD_DIM), m3, (None, SWA_BLOCK, SWA_HEAD_DIM), h3, g(vs), m3, gdt)]
    pars = [Par(bp, (None, qr, SWA_BLOCK), p3, g(bp), p3), Par(bc, (None, qr, SWA_BLOCK), p3, g(bc), p3),
            Par(sk, (None, qr, 1), p3, g(sk), p3)]
    return rows, pars, T // SWA_BLOCK


def _fwd_bwd(x, mem, tgt, W):
    T = x.shape[0]
    nb = T // SWA_BLOCK
    tb = min(256, T)
    tbl = min(512, T)
    fwd = lambda f: (lambda *a: (f(*a), []))
    full = lambda cols, dt, t, cw: Out((T, cols), dt, (t, cw), lambda j, n: (n, j))

    xb = x.astype(_CDT)
    proj = _mm("proj", xb, W["in_p"], "nn")

    onehot_t = _bias_onehot()
    bias_flat = _mm("swa_bias", W["rel_bias"].T, onehot_t, "nn", hi=True)
    half = SWA_BLOCK * SWA_BLOCK
    bp = bias_flat[:, :half].reshape(SWA_KV_HEADS, SWA_GRP * SWA_BLOCK, SWA_BLOCK)
    bc = bias_flat[:, half:].reshape(SWA_KV_HEADS, SWA_GRP * SWA_BLOCK, SWA_BLOCK)
    sk = jnp.broadcast_to(W["swa_sinks"].reshape(SWA_KV_HEADS, SWA_GRP, 1, 1), (SWA_KV_HEADS, SWA_GRP, SWA_BLOCK, 1)).reshape(
        SWA_KV_HEADS, SWA_GRP * SWA_BLOCK, 1)
    qs = _stack_heads(proj[:, P_SQ:P_SQ + SWA_Q], nb)
    ks = _kv_heads(proj[:, P_SK:P_SK + SWA_KV])
    vs = _kv_heads(proj[:, P_SV:P_SV + SWA_KV])
    rows, pars, nblk = _swa_specs(qs, ks, vs, bp, bc, sk, False)
    o_s, = _rowmap("swa_fwd", fwd(_swa_fn), SWA_KV_HEADS, nblk, rows, pars,
                   [Out(tuple(qs.shape), F32, (None, SWA_GRP * SWA_BLOCK, SWA_HEAD_DIM), lambda j, n: (j, n, 0))])
    o_swa = _unstack_heads(o_s, nb).astype(_CDT)

    ncq = 3 * GDN_W // LANES
    pre_rows = lambda grad: [_rowspec(proj, tbl, LANES, P_GQKV // LANES, halo=SUBLANES, grad=grad, ncol=ncq, gdt=(_CDT,))]
    pre_pars = lambda grad: [_parspec(W["gdn_conv_w"], LANES, 0, grad=grad, ncol=ncq)]
    qkv_n, = _rowmap("gdn_pre_fwd", fwd(_gdn_pre_fn), ncq, T // tbl, pre_rows(False), pre_pars(False),
                     [full(3 * GDN_W, F32, tbl, LANES)])
    eb, eg = _head_spread()
    alog_row, dtb_row = _lane16(W["gdn_a_log"]), _lane16(W["gdn_dt_bias"])
    gate_rows = lambda grad: [_rowspec(proj, tbl, LANES, P_BA // LANES, cstep=0, grad=grad, gdt=(_CDT,))]
    gate_pars = lambda grad: [_parspec(alog_row, grad=grad), _parspec(dtb_row, grad=grad), _parspec(eb), _parspec(eg)]
    bx, gx = _rowmap("gdn_gate_fwd", fwd(_gdn_gate_fn), 1, T // tbl, gate_rows(False), gate_pars(False),
                     [full(GDN_W, F32, tbl, GDN_W), full(GDN_W, F32, tbl, GDN_W)])
    nw = W["gdn_norm_w"].reshape(1, GDN_HEAD_DIM)
    o_gdn, states = _gdn_chunks_fwd(qkv_n, bx, gx, proj, nw)

    ys = _mm("y_swa", o_swa, W["br_swa"], "nn")
    yg = _mm("y_gdn", o_gdn, W["br_gdn"], "nn")
    cwm = 512
    mix_rows = lambda grad: [_rowspec(proj, tb, cwm, P_GS // cwm, grad=grad, ncol=D_MODEL // cwm, gdt=(_CDT,)),
                             _rowspec(proj, tb, cwm, P_GG // cwm, grad=grad, ncol=D_MODEL // cwm, gdt=(_CDT,)),
                             _rowspec(ys, tb, cwm, 0, grad=grad, ncol=D_MODEL // cwm, gdt=(_CDT,)),
                             _rowspec(yg, tb, cwm, 0, grad=grad, ncol=D_MODEL // cwm, gdt=(_CDT,))]
    mixed, = _rowmap("mix_fwd", fwd(_mix_fn), D_MODEL // cwm, T // tb, mix_rows(False), [], [full(D_MODEL, _CDT, tb, cwm)])
    r1 = _mm("r1", mixed, W["mix_o"], "nn", add=x, add_scale=ALPHA)

    def ln_fwd(name, r, g, b):
        return _rowmap(name, _ln_fwd_fn, 1, T // tb, [_rowspec(r, tb, D_MODEL, 0)], [_parspec(g), _parspec(b)],
                       [full(D_MODEL, F32, tb, D_MODEL), full(D_MODEL, _CDT, tb, D_MODEL)])

    def ln_bwd(name, r, g, b, ct):
        return _rowmap_bwd(name, _ln_fn, 1, T // tb, [_rowspec(r, tb, D_MODEL, 0, grad=True, gdt=(F32, _CDT))],
                           [_parspec(g, grad=True), _parspec(b, grad=True)], [_rowspec(ct, tb, D_MODEL, 0)])

    g1, b1 = W["ln1_g"].reshape(1, -1), W["ln1_b"].reshape(1, -1)
    g2, b2 = W["ln2_g"].reshape(1, -1), W["ln2_b"].reshape(1, -1)
    g3, b3 = W["ln3_g"].reshape(1, -1), W["ln3_b"].reshape(1, -1)
    x1, x1b = ln_fwd("ln1_fwd", r1, g1, b1)

    qm = _mm("mem_q", x1b, W["mem_q"], "nn")
    kvm = _mm("mem_kv", mem, W["mem_kv"], "nn")
    ma_rows = lambda grad: [_rowspec(qm, tbl, MEM_HEAD_DIM, 0, grad=grad, ncol=MEM_HEADS, gdt=(_CDT,))]
    ma_pars = lambda grad: [_parspec(kvm, MEM_HEAD_DIM, 0, grad=grad, ncol=MEM_HEADS),
                            _parspec(kvm, MEM_HEAD_DIM, MEM_HEADS, grad=grad, ncol=MEM_HEADS)]
    om, = _rowmap("memattn_fwd", fwd(_memattn_fn), MEM_HEADS, T // tbl, ma_rows(False), ma_pars(False),
                  [full(MEM_W, _CDT, tbl, MEM_HEAD_DIM)])
    r2 = _mm("r2", om, W["mem_o"], "nn", add=x1, add_scale=ALPHA)
    x2, x2b = ln_fwd("ln2_fwd", r2, g2, b2)

    hcat = _mm("ffn_up", x2b, W["up_p"], "nn")
    cwf = 512
    ncf = D_FF_PAD // cwf
    cw_p, cb_p = W["ffn_conv_w_p"], W["ffn_conv_b_p"]
    ffn_rows = lambda grad: [_rowspec(hcat, tb, cwf, 0, halo=SUBLANES, grad=grad, ncol=ncf, gdt=(_CDT,)),
                             _rowspec(hcat, tb, cwf, ncf, halo=SUBLANES, grad=grad, ncol=ncf, gdt=(_CDT,))]
    ffn_pars = lambda grad: [_parspec(cw_p, cwf, 0, grad=grad, ncol=ncf), _parspec(cw_p, cwf, ncf, grad=grad, ncol=ncf),
                             _parspec(cb_p, cwf, 0, grad=grad, ncol=ncf), _parspec(cb_p, cwf, ncf, grad=grad, ncol=ncf)]
    act, = _rowmap("ffn_act_fwd", fwd(_ffn_act_fn), ncf, T // tb, ffn_rows(False), ffn_pars(False), [full(D_FF_PAD, _CDT, tb, cwf)])
    r3 = _mm("r3", act, W["down_p"], "nn", add=x2, add_scale=ALPHA)
    dr3, dr3b, lacc, dg3, db3 = _rowmap("ln3_loss", _loss_fn, 1, T // tb, [_rowspec(r3, tb, D_MODEL, 0), _rowspec(tgt, tb, D_MODEL, 0)],
                                        [_parspec(g3), _parspec(b3)], [full(D_MODEL, F32, tb, D_MODEL), full(D_MODEL, _CDT, tb, D_MODEL)],
                                  accs=[(SUBLANES, LANES), (1, D_MODEL), (1, D_MODEL)])
    loss = lacc[0, 0]

    G = {}
    G["down_p"] = _mm("dw_down", act, dr3b, "tn", out_dtype=_GDT)
    dact = _mm("d_act", dr3b, W["down_p"], "nt")
    dhg, dhu, dcwg, dcwu, dcbg, dcbu = _rowmap_bwd("ffn_act_bwd", _ffn_act_fn, ncf, T // tb, ffn_rows(True), ffn_pars(True),
                                                   [_rowspec(dact, tb, cwf, 0)])
    w_gate, w_upp = W["up_p"][:, :D_FF_PAD], W["up_p"][:, D_FF_PAD:]
    dx2 = _mm("dx2_gate", dhg, w_gate, "nt", add=dr3, add_scale=ALPHA)
    dx2 = _mm("dx2_up", dhu, w_upp, "nt", add=dx2)
    G["up_p"] = jnp.concatenate([_mm("dw_gate", x2b, dhg, "tn", out_dtype=_GDT), _mm("dw_up", x2b, dhu, "tn", out_dtype=_GDT)], axis=1)
    G["ffn_conv_w"] = jnp.concatenate([dcwg[:, :D_FF], dcwu[:, :D_FF]], axis=1)
    G["ffn_conv_b"] = jnp.concatenate([dcbg[0, :D_FF], dcbu[0, :D_FF]])
    G["ln3_g"], G["ln3_b"] = dg3[0], db3[0]

    dr2, dr2b, dg2, db2 = ln_bwd("ln2_bwd", r2, g2, b2, dx2)
    G["ln2_g"], G["ln2_b"] = dg2[0], db2[0]
    G["mem_o"] = _mm("dw_mem_o", om, dr2b, "tn", out_dtype=_GDT)
    dom = _mm("d_om", dr2b, W["mem_o"], "nt", out_dtype=_CDT)
    dqm, dkm, dvm = _rowmap_bwd("memattn_bwd", _memattn_fn, MEM_HEADS, T // tbl, ma_rows(True), ma_pars(True),
                                [_rowspec(dom, tbl, MEM_HEAD_DIM, 0)])
    G["mem_kv"] = _mm("dw_mem_kv", mem.astype(_CDT), jnp.concatenate([dkm, dvm], axis=1).astype(_CDT), "tn", out_dtype=_GDT)
    G["mem_q"] = _mm("dw_mem_q", x1b, dqm, "tn", out_dtype=_GDT)
    dx1 = _mm("dx1", dqm, W["mem_q"], "nt", add=dr2, add_scale=ALPHA)

    dr1, dr1b, dg1, db1 = ln_bwd("ln1_bwd", r1, g1, b1, dx1)
    G["ln1_g"], G["ln1_b"] = dg1[0], db1[0]
    G["mix_o"] = _mm("dw_mix_o", mixed, dr1b, "tn", out_dtype=_GDT)
    dmixed = _mm("d_mixed", dr1b, W["mix_o"], "nt")
    dgs, dgg, dys, dyg = _rowmap_bwd("mix_bwd", _mix_fn, D_MODEL // cwm, T // tb, mix_rows(True), [], [_rowspec(dmixed, tb, cwm, 0)])
    G["br_swa"] = _mm("dw_br_swa", o_swa, dys, "tn", out_dtype=_GDT)
    G["br_gdn"] = _mm("dw_br_gdn", o_gdn, dyg, "tn", out_dtype=_GDT)
    do_swa = _mm("d_o_swa", dys, W["br_swa"], "nt", out_dtype=_CDT)
    do_gdn = _mm("d_o_gdn", dyg, W["br_gdn"], "nt")

    rows, pars, nblk = _swa_specs(qs, ks, vs, bp, bc, sk, True, (_CDT,))
    m3 = lambda j, n: (j, n, 0)
    dqs, dks, dvs, dbp, dbc, dsk = _rowmap_bwd("swa_bwd", _swa_fn, SWA_KV_HEADS, nblk, rows, pars,
                                               [Row(_stack_heads(do_swa, nb), (None, SWA_GRP * SWA_BLOCK, SWA_HEAD_DIM), m3)])
    d_swa = jnp.concatenate([_unstack_heads(dqs, nb), dks.transpose(1, 0, 2).reshape(T, SWA_KV),
                             dvs.transpose(1, 0, 2).reshape(T, SWA_KV)], axis=1)
    dbias = jnp.concatenate([dbp.reshape(SWA_HEADS, half), dbc.reshape(SWA_HEADS, half)], axis=1)
    G["rel_bias"] = _mm("d_rel_bias", dbias, onehot_t.T, "nn", hi=True).T
    G["swa_sinks"] = _mm("d_sinks", dsk.reshape(SWA_HEADS, SWA_BLOCK), jnp.ones((SWA_BLOCK, LANES), F32), "nn", hi=True)[:, 0]

    dqkv_n, dbx, dgx, dz, dnw = _gdn_chunks_bwd(qkv_n, bx, gx, proj, nw, states, do_gdn)
    G["gdn_norm_w"] = dnw[0]
    dgba, dalog, ddtb = _rowmap_bwd("gdn_gate_bwd", _gdn_gate_fn, 1, T // tbl, gate_rows(True), gate_pars(True),
                                    [_rowspec(dbx, tbl, GDN_W, 0), _rowspec(dgx, tbl, GDN_W, 0)])
    G["gdn_a_log"], G["gdn_dt_bias"] = dalog[0, GDN_HEADS:2 * GDN_HEADS], ddtb[0, GDN_HEADS:2 * GDN_HEADS]
    dgqkv, dcw_gdn = _rowmap_bwd("gdn_pre_bwd", _gdn_pre_fn, ncq, T // tbl, pre_rows(True), pre_pars(True),
                                 [_rowspec(dqkv_n, tbl, LANES, 0)])
    G["gdn_conv_w"] = dcw_gdn

    dproj = jnp.concatenate([dgs, dgg, dgqkv, dz, d_swa, dgba, jnp.zeros((T, P_END - P_USED), _CDT)], axis=1)
    dx = _mm("dx", dproj, W["in_p"], "nt", add=dr1, add_scale=ALPHA)
    G["in_p"] = _mm("dw_in", xb, dproj, "tn", out_dtype=_GDT)
    return loss, dx, G


W_NAMES = ["w_in", "rel_bias", "swa_sinks", "gdn_conv_w", "gdn_a_log", "gdn_dt_bias", "gdn_norm_w", "w_br_swa", "w_br_gdn",
           "w_mix_o", "ln1_g", "ln1_b", "w_mem_q", "w_mem_kv", "w_mem_o", "ln2_g", "ln2_b", "w_up", "ffn_conv_w", "ffn_conv_b",
           "w_down", "ln3_g", "ln3_b"]
BIG = ["w_in", "w_br_swa", "w_br_gdn", "w_mix_o", "w_mem_q", "w_mem_kv", "w_mem_o", "w_up", "w_down"]
SMALL = [n for n in W_NAMES if n not in BIG]
COL_SHARDED = ["w_in", "w_br_swa", "w_br_gdn", "w_mem_o", "w_up"]


def _pack(arrs):
    flat = []
    for a in arrs:
        f = a.reshape(-1).astype(F32)
        flat.append(jnp.pad(f, (0, (-f.shape[0]) % LANES)))
    f = jnp.concatenate(flat)
    f = jnp.pad(f, (0, (-f.shape[0]) % (16 * LANES)))
    return f.reshape(-1, LANES)


def _unpack(p, shapes):
    f, out, off = p.reshape(-1), [], 0
    for s in shapes:
        n = int(np.prod(s)) if len(s) else 1
        out.append(f[off:off + n].reshape(s))
        off += n + (-n) % LANES
    return out


def _merge_shards(d):
    cat = lambda names: jnp.concatenate([d[n] for n in names], axis=-2)
    return [d["w_in"], d["w_up"], cat(["w_br_swa", "w_br_gdn", "w_mem_q", "w_mem_o"]), cat(["w_mix_o", "w_down"]), d["w_mem_kv"]]


def _split_shards(ts):
    a, b, c, dd, e = ts
    return {"w_in": a, "w_up": b, "w_br_swa": c[..., 0:1024, :], "w_br_gdn": c[..., 1024:2048, :], "w_mem_q": c[..., 2048:2560, :],
            "w_mem_o": c[..., 2560:3072, :], "w_mix_o": dd[..., 0:512, :], "w_down": dd[..., 512:, :], "w_mem_kv": e}


def _to_full(name, t):
    if name in COL_SHARDED:
        return t.transpose(1, 0, 2).reshape(t.shape[1], 4 * t.shape[2])
    return t.reshape(4 * t.shape[1], t.shape[2])


def _to_chips(name, t):
    if name in COL_SHARDED:
        return t.reshape(t.shape[0], 4, t.shape[1] // 4).transpose(1, 0, 2)
    return t.reshape(4, t.shape[0] // 4, t.shape[1])


_IN_OFF = np.cumsum((0,) + IN_WIDTHS)


def _in_to_padded(w):
    o = _IN_OFF
    cut = lambda i, k: w[:, o[i]:o[k]]
    return jnp.concatenate([cut(9, 10), cut(10, 11), cut(3, 6), cut(6, 7), cut(0, 1), cut(1, 2), cut(2, 3), cut(7, 9),
                            jnp.zeros((w.shape[0], P_END - P_BA - 2 * GDN_HEADS), w.dtype)], axis=1)


def _in_from_padded(p):
    return jnp.concatenate([p[:, P_SQ:P_SQ + SWA_Q], p[:, P_SK:P_SK + SWA_KV], p[:, P_SV:P_SV + SWA_KV], p[:, P_GQKV:P_GQKV + 3 * GDN_W],
                            p[:, P_GZ:P_GZ + GDN_W], p[:, P_BA:P_BA + 2 * GDN_HEADS], p[:, P_GS:P_GS + D_MODEL], p[:, P_GG:P_GG + D_MODEL]],
                           axis=1)


def _ff_pad(t, axis):
    g, u = jnp.split(t, 2, axis=axis)
    pad = [(0, 0)] * t.ndim
    pad[axis] = (0, D_FF_PAD - D_FF)
    return jnp.concatenate([jnp.pad(g, pad), jnp.pad(u, pad)], axis=axis)


def _ff_unpad(t, axis):
    g, u = jnp.split(t, 2, axis=axis)
    return jnp.concatenate([lax.slice_in_dim(g, 0, D_FF, axis=axis), lax.slice_in_dim(u, 0, D_FF, axis=axis)], axis=axis)


def _assemble_weights(full, small):
    W = dict(small)
    W["in_p"] = _in_to_padded(full["w_in"])
    W["up_p"] = _ff_pad(full["w_up"], 1)
    W["down_p"] = jnp.pad(full["w_down"], ((0, D_FF_PAD - D_FF), (0, 0)))
    W["br_swa"], W["br_gdn"], W["mix_o"] = full["w_br_swa"], full["w_br_gdn"], full["w_mix_o"]
    W["mem_q"], W["mem_kv"], W["mem_o"] = full["w_mem_q"], full["w_mem_kv"], full["w_mem_o"]
    W["ffn_conv_w_p"] = _ff_pad(small["ffn_conv_w"], 1)
    W["ffn_conv_b_p"] = _ff_pad(small["ffn_conv_b"].reshape(1, -1), 1)
    return W


def _full_grads(G):
    return {"w_in": _in_from_padded(G["in_p"]), "w_up": _ff_unpad(G["up_p"], 1), "w_down": G["down_p"][:D_FF],
            "w_br_swa": G["br_swa"], "w_br_gdn": G["br_gdn"], "w_mix_o": G["mix_o"], "w_mem_q": G["mem_q"],
            "w_mem_kv": G["mem_kv"], "w_mem_o": G["mem_o"]}


def kernel(x, mem, w_in, rel_bias, swa_sinks, gdn_conv_w, gdn_a_log, gdn_dt_bias, gdn_norm_w, w_br_swa, w_br_gdn, w_mix_o, ln1_g, ln1_b, w_mem_q, w_mem_kv, w_mem_o, ln2_g, ln2_b, w_up, ffn_conv_w, ffn_conv_b, w_down, ln3_g, ln3_b, loss_target, m_w_in, m_rel_bias, m_swa_sinks, m_gdn_conv_w, m_gdn_a_log, m_gdn_dt_bias, m_gdn_norm_w, m_w_br_swa, m_w_br_gdn, m_w_mix_o, m_ln1_g, m_ln1_b, m_w_mem_q, m_w_mem_kv, m_w_mem_o, m_ln2_g, m_ln2_b, m_w_up, m_ffn_conv_w, m_ffn_conv_b, m_w_down, m_ln3_g, m_ln3_b, v_w_in, v_rel_bias, v_swa_sinks, v_gdn_conv_w, v_gdn_a_log, v_gdn_dt_bias, v_gdn_norm_w, v_w_br_swa, v_w_br_gdn, v_w_mix_o, v_ln1_g, v_ln1_b, v_w_mem_q, v_w_mem_kv, v_w_mem_o, v_ln2_g, v_ln2_b, v_w_up, v_ffn_conv_w, v_ffn_conv_b, v_w_down, v_ln3_g, v_ln3_b):
    a = dict(locals())
    w = {n: a[n] for n in W_NAMES}
    m = {n: a["m_" + n] for n in W_NAMES}
    v = {n: a["v_" + n] for n in W_NAMES}
    chip = 2 * lax.axis_index("x") + lax.axis_index("y")
    core = lax.axis_index("c")
    sq = lambda t: t.reshape(t.shape[1:]) if (t.ndim > 1 and t.shape[0] == 1 and t is not rel_bias) else t

    shards = _merge_shards({n: sq(w[n]).astype(_CDT) for n in BIG})
    conv_pack = _pack([sq(ffn_conv_w), sq(gdn_conv_w)])
    gathered = _chip_allgather(shards + [conv_pack])
    full = {n: _to_full(n, t) for n, t in _split_shards(gathered[:-1]).items()}
    fcw_sh, gcw_sh = sq(ffn_conv_w).shape, sq(gdn_conv_w).shape
    conv = [_unpack(gathered[-1][k], [fcw_sh, gcw_sh]) for k in range(4)]
    small = {n: sq(w[n]) for n in SMALL}
    small["ffn_conv_w"] = jnp.concatenate([cv[0] for cv in conv], axis=1)
    small["gdn_conv_w"] = jnp.concatenate([cv[1] for cv in conv], axis=1)
    W = _assemble_weights(full, small)

    loss, dx, G = _fwd_bwd(x[0], mem[0], loss_target[0], W)

    small_names = SMALL
    small_shapes = [()] + [tuple(G[n].shape) for n in small_names]
    packed = _pack([loss] + [G[n] for n in small_names])
    allp = _allgather8(packed)
    tot = _addn("small_sum", [(allp, k) for k in range(8)])
    parts = _unpack(tot, small_shapes)
    loss_tot, gsmall = parts[0], dict(zip(small_names, parts[1:]))
    gsmall["ffn_conv_w"] = lax.dynamic_slice_in_dim(gsmall["ffn_conv_w"], chip * fcw_sh[1], fcw_sh[1], axis=1)
    gsmall["gdn_conv_w"] = lax.dynamic_slice_in_dim(gsmall["gdn_conv_w"], chip * gcw_sh[1], gcw_sh[1], axis=1)

    gfull = _full_grads(G)
    gch = _merge_shards({n: _to_chips(n, gfull[n]) for n in BIG})
    theirs = _pair_swap(gch)
    pair = []
    for t, (mine, got) in enumerate(zip(gch, theirs)):
        rh = mine.shape[1] // 2
        mine_h = lax.dynamic_slice_in_dim(mine, core * rh, rh, axis=1)
        pair.append(_addn(f"pair_sum{t}", [mine_h.reshape(4 * rh, -1), got.reshape(4 * rh, -1)], _GDT).reshape(4, rh, -1))
    others = _chip_scatter(pair)
    halves = []
    for t, (p, o) in enumerate(zip(pair, others)):
        own = lax.dynamic_index_in_dim(p, chip, 0, keepdims=False)
        halves.append(_addn(f"chip_sum{t}", [own, (o, 0), (o, 1), (o, 2)]))
    both = _pair_gather(halves)
    gbig = _split_shards([b.reshape(2 * b.shape[1], b.shape[2]) for b in both])

    outs = {}
    for n in BIG:
        d_, m_, v_ = _adamw("adamw_" + n, sq(w[n]), gbig[n], sq(m[n]), sq(v[n]))
        outs[n] = (gbig[n], d_, m_, v_)
    sm_shapes = [sq(w[n]).shape if n != "rel_bias" else w[n].shape for n in SMALL]
    sw, sg, sm_, sv = (_pack([sq(t[n]) if n != "rel_bias" else t[n] for n in SMALL]) for t in (w, gsmall, m, v))
    d_, m_, v_ = _adamw("adamw_small", sw, sg, sm_, sv)
    for n, g_, dd, mm_, vv in zip(SMALL, _unpack(sg, sm_shapes), _unpack(d_, sm_shapes), _unpack(m_, sm_shapes), _unpack(v_, sm_shapes)):
        outs[n] = (g_, dd, mm_, vv)

    res = [loss_tot.reshape(()), dx.reshape(x.shape)]
    for k in range(4):
        res += [outs[n][k].reshape(w[n].shape) for n in W_NAMES]
    return tuple(res)
```

```python
import functools
import math

import jax
import jax.numpy as jnp
import numpy as np
from jax import lax
from jax.experimental import pallas as pl
from jax.experimental.pallas import tpu as pltpu

F32 = jnp.float32
BF16 = jnp.bfloat16
_CDT = BF16
_GDT = BF16

D_MODEL = 2048
SWA_HEADS, SWA_KV_HEADS, SWA_HEAD_DIM, SWA_BLOCK = 16, 2, 64, 128
SWA_GRP = SWA_HEADS // SWA_KV_HEADS
REL_BUCKETS, REL_MAX_DIST = 32, 128
GDN_HEADS, GDN_HEAD_DIM, GDN_CONV, GDN_CHUNK = 8, 128, 4, 64
MEM_HEADS, MEM_HEAD_DIM = 4, 128
D_FF, D_FF_PAD, FFN_CONV = 5504, 5632, 3
SWA_Q, SWA_KV, GDN_W, MEM_W = 1024, 128, 1024, 512
IN_WIDTHS = (SWA_Q, SWA_KV, SWA_KV, GDN_W, GDN_W, GDN_W, GDN_W, GDN_HEADS, GDN_HEADS, D_MODEL, D_MODEL)
NORM_EPS = 1e-5
ALPHA = 2.0 ** 0.25
NEG_INF = -1e30
ADAM_LR, ADAM_B1, ADAM_B2, ADAM_EPS, ADAM_WD, ADAM_STEP = 0.001, 0.9, 0.999, 1e-08, 0.01, 10
LANES, SUBLANES = 128, 8
VMEM_LIMIT = 56 * 1024 * 1024

P_GS, P_GG, P_GQKV, P_GZ, P_SQ, P_SK, P_SV, P_BA, P_USED, P_END = 0, 2048, 4096, 7168, 8192, 9216, 9344, 9472, 9600, 9728


def _tile(dim, pref, align=LANES):
    if dim <= pref:
        return dim
    t = (pref // align) * align
    while t >= align:
        if dim % t == 0:
            return t
        t -= align
    return dim


_DIMS = {"nn": (((1,), (0,)), ((), ())), "nt": (((1,), (1,)), ((), ())), "tn": (((0,), (0,)), ((), ()))}
_BDIMS = {"nn": (((2,), (1,)), ((0,), (0,))), "nt": (((2,), (2,)), ((0,), (0,))), "tn": (((1,), (1,)), ((0,), (0,)))}


def _raw_dot(a, b, form, hi):
    dims = (_BDIMS if a.ndim == 3 else _DIMS)[form]
    if hi == "x3":
        a, b = a.astype(F32), b.astype(F32)
        ah, bh = a.astype(BF16), b.astype(BF16)
        al, bl = (a - ah.astype(F32)).astype(BF16), (b - bh.astype(F32)).astype(BF16)
        d = lambda p, q: lax.dot_general(p, q, dims, preferred_element_type=F32)
        return d(ah, bh) + (d(ah, bl) + d(al, bh))
    if hi:
        return lax.dot_general(a.astype(F32), b.astype(F32), dims, precision=lax.Precision.HIGHEST, preferred_element_type=F32)
    return lax.dot_general(a.astype(_CDT), b.astype(_CDT), dims, preferred_element_type=F32)


@functools.partial(jax.custom_vjp, nondiff_argnums=(2, 3))
def _dot(a, b, form, hi=False):
    return _raw_dot(a, b, form, hi)


def _dot_fwd(a, b, form, hi):
    return _raw_dot(a, b, form, hi), (a, b)


def _dot_bwd(form, hi, res, g):
    a, b = res
    if form == "nn":
        da, db = _raw_dot(g, b, "nt", hi), _raw_dot(a, g, "tn", hi)
    elif form == "nt":
        da, db = _raw_dot(g, b, "nn", hi), _raw_dot(g, a, "tn", hi)
    else:
        da, db = _raw_dot(b, g, "nt", hi), _raw_dot(a, g, "nn", hi)
    return da.astype(a.dtype), db.astype(b.dtype)


_dot.defvjp(_dot_fwd, _dot_bwd)


@functools.partial(jax.custom_vjp, nondiff_argnums=(2,))
def _shift_halo(prev, cur, d):
    assert prev.shape[0] == SUBLANES
    return pltpu.roll(jnp.concatenate([prev, cur], axis=0), d, 0)[SUBLANES:]


def _shift_halo_fwd(prev, cur, d):
    return _shift_halo(prev, cur, d), None


def _shift_halo_bwd(d, _, g):
    nh = SUBLANES
    ext = jnp.concatenate([jnp.zeros((nh, g.shape[1]), g.dtype), g], axis=0)
    r = pltpu.roll(ext, ext.shape[0] - d, 0)
    return r[:nh], r[nh:]


_shift_halo.defvjp(_shift_halo_fwd, _shift_halo_bwd)


def _sigmoid(x):
    return 1.0 / (1.0 + jnp.exp(-x))


def _silu(x):
    return x * _sigmoid(x)


def _softplus(x):
    return jnp.maximum(x, 0.0) + jnp.log(1.0 + jnp.exp(-jnp.abs(x)))


def _iota(shape, axis):
    return lax.broadcasted_iota(jnp.int32, shape, axis)


def _cparams(sem):
    return pltpu.CompilerParams(dimension_semantics=sem, vmem_limit_bytes=VMEM_LIMIT)


def _mm(name, a, b, form, out_dtype=F32, add=None, add_scale=1.0, hi=False, tm=1024, tn=1024, tk=2048):
    if form == "nn":
        (M, K), (K2, N) = a.shape, b.shape
    elif form == "nt":
        (M, K), (N, K2) = a.shape, b.shape
    else:
        (K, M), (K2, N) = a.shape, b.shape
    assert K == K2, (name, a.shape, b.shape, form)
    tm, tn, tk = _tile(M, tm), _tile(N, tn), _tile(K, tk)
    nk = K // tk
    a_spec = pl.BlockSpec((tk, tm), lambda i, j, k: (k, i)) if form == "tn" else pl.BlockSpec((tm, tk), lambda i, j, k: (i, k))
    b_spec = pl.BlockSpec((tn, tk), lambda i, j, k: (j, k)) if form == "nt" else pl.BlockSpec((tk, tn), lambda i, j, k: (k, j))
    o_spec = pl.BlockSpec((tm, tn), lambda i, j, k: (i, j))
    has_add = add is not None

    def finish(r, c_ref, o_ref):
        if has_add:
            r = r + add_scale * c_ref[...].astype(F32)
        o_ref[...] = r.astype(out_dtype)

    def body(*refs):
        a_ref, b_ref = refs[:2]
        c_ref = refs[2] if has_add else None
        o_ref = refs[3] if has_add else refs[2]
        if nk == 1:
            finish(_raw_dot(a_ref[...], b_ref[...], form, hi), c_ref, o_ref)
            return
        acc = refs[-1]
        k = pl.program_id(2)

        @pl.when(k == 0)
        def _():
            acc[...] = jnp.zeros_like(acc)

        acc[...] += _raw_dot(a_ref[...], b_ref[...], form, hi)

        @pl.when(k == nk - 1)
        def _():
            finish(acc[...], c_ref, o_ref)

    ins = [a, b] + ([add] if has_add else [])
    specs = [a_spec, b_spec] + ([o_spec] if has_add else [])
    return pl.pallas_call(
        body, name=name, grid=(M // tm, N // tn, nk), in_specs=specs, out_specs=o_spec,
        out_shape=jax.ShapeDtypeStruct((M, N), out_dtype),
        scratch_shapes=[pltpu.VMEM((tm, tn), F32)] if nk > 1 else [],
        compiler_params=_cparams(("parallel", "parallel", "arbitrary")),
    )(*ins)


class Row:
    def __init__(self, arr, blk, imap, hblk=None, hmap=None, gshape=None, gmap=None, gdt=(F32,)):
        self.arr, self.blk, self.imap, self.hblk, self.hmap, self.gshape, self.gmap = arr, blk, imap, hblk, hmap, gshape, gmap
        self.gdt = gdt


class Par:
    def __init__(self, arr, blk=None, imap=None, gshape=None, gmap=None):
        self.arr = arr
        self.blk = tuple(arr.shape) if blk is None else blk
        nd = len(self.blk)
        self.imap = (lambda j: (0,) * nd) if imap is None else imap
        self.gshape, self.gmap = gshape, gmap


class Out:
    def __init__(self, shape, dtype, blk, imap):
        self.shape, self.dtype, self.blk, self.imap = shape, dtype, blk, imap


def _rows_of(blk):
    return [d for d in blk if d is not None][0]


def _rowmap(name, fn, ncol, nblk, rows, pars, outs, accs=()):
    in_specs, ins = [], []
    for r in rows:
        ins.append(r.arr)
        in_specs.append(pl.BlockSpec(r.blk, r.imap))
        if r.hblk is not None:
            ins.append(r.arr)
            in_specs.append(pl.BlockSpec(r.hblk, r.hmap))
    for p in pars:
        ins.append(p.arr)
        in_specs.append(pl.BlockSpec(p.blk, (lambda im: (lambda j, n: im(j)))(p.imap)))
    out_specs = [pl.BlockSpec(o.blk, o.imap) for o in outs]
    out_shape = [jax.ShapeDtypeStruct(o.shape, o.dtype) for o in outs]
    for a in accs:
        out_specs.append(pl.BlockSpec(a, (lambda nd: (lambda j, n: (0,) * nd))(len(a))))
        out_shape.append(jax.ShapeDtypeStruct(a, F32))
    n_in = len(ins)

    def body(*refs):
        j, n = pl.program_id(0), pl.program_id(1)
        it = iter(refs[:n_in])
        rvals = []
        for r in rows:
            cur = next(it)[...]
            rvals.append((next(it)[...], cur) if r.hblk is not None else cur)
        pvals = [next(it)[...] for _ in pars]
        o_refs = refs[n_in:n_in + len(outs)]
        a_refs = refs[n_in + len(outs):]
        ovals, avals = fn(j, n == 0, rvals, pvals)
        for ref, v in zip(o_refs, ovals):
            ref[...] = v.astype(ref.dtype)
        if accs:
            @pl.when((j == 0) & (n == 0))
            def _():
                for ref in a_refs:
                    ref[...] = jnp.zeros_like(ref)
            for ref, v in zip(a_refs, avals):
                ref[...] += v

    res = pl.pallas_call(
        body, name=name, grid=(ncol, nblk), in_specs=in_specs, out_specs=out_specs, out_shape=out_shape,
        compiler_params=_cparams(("arbitrary", "arbitrary")),
    )(*ins)
    return res


def _rowmap_bwd(name, fn, ncol, nblk, rows, pars, cts):
    rev = lambda im: (lambda j, s: im(j, nblk - 1 - s))
    in_specs, ins = [], []
    for r in rows:
        ins.append(r.arr)
        in_specs.append(pl.BlockSpec(r.blk, rev(r.imap)))
        if r.hblk is not None:
            ins.append(r.arr)
            in_specs.append(pl.BlockSpec(r.hblk, rev(r.hmap)))
    for p in pars:
        ins.append(p.arr)
        in_specs.append(pl.BlockSpec(p.blk, (lambda im: (lambda j, s: im(j)))(p.imap)))
    for c in cts:
        ins.append(c.arr)
        in_specs.append(pl.BlockSpec(c.blk, rev(c.imap)))
    n_in = len(ins)
    drows = [i for i, r in enumerate(rows) if r.gshape is not None]
    dpars = [i for i, p in enumerate(pars) if p.gshape is not None]
    out_specs, out_shape, scratch = [], [], []
    for i in drows:
        r = rows[i]
        for dt in r.gdt:
            out_specs.append(pl.BlockSpec(r.blk, rev(r.gmap)))
            out_shape.append(jax.ShapeDtypeStruct(r.gshape, dt))
        if r.hblk is not None:
            scratch.append(pltpu.VMEM(tuple(d for d in r.hblk if d is not None), F32))
    n_drow_out = len(out_specs)
    for i in dpars:
        p = pars[i]
        out_specs.append(pl.BlockSpec(p.blk, (lambda im: (lambda j, s: im(j)))(p.gmap)))
        out_shape.append(jax.ShapeDtypeStruct(p.gshape, F32))

    def body(*refs):
        j, s = pl.program_id(0), pl.program_id(1)
        first = s == nblk - 1
        it = iter(refs[:n_in])
        rvals = []
        for r in rows:
            cur = next(it)[...]
            rvals.append((next(it)[...], cur) if r.hblk is not None else cur)
        pvals = [next(it)[...] for _ in pars]
        cvals = [next(it)[...].astype(F32) for _ in cts]
        g_refs = iter(refs[n_in:n_in + n_drow_out])
        p_refs = refs[n_in + n_drow_out:n_in + n_drow_out + len(dpars)]
        carries = iter(refs[n_in + n_drow_out + len(dpars):])

        def f(dr, dp):
            rv, pv = list(rvals), list(pvals)
            for i, v in zip(drows, dr):
                rv[i] = v
            for i, v in zip(dpars, dp):
                pv[i] = v
            return fn(j, first, rv, pv)

        _, vjp = jax.vjp(f, [rvals[i] for i in drows], [pvals[i] for i in dpars])
        g_r, g_p = vjp(cvals)
        for i, g in zip(drows, g_r):
            r = rows[i]
            if r.hblk is None:
                for _ in r.gdt:
                    ref = next(g_refs)
                    ref[...] = g.astype(ref.dtype)
            else:
                g_prev, g_cur = g
                carry = next(carries)
                nr, nh = g_cur.shape[0], g_prev.shape[0]
                tail = g_cur[nr - nh:nr] + jnp.where(s > 0, carry[...], 0.0)
                for _ in r.gdt:
                    ref = next(g_refs)
                    if nr > nh:
                        ref[0:nr - nh, :] = g_cur[0:nr - nh].astype(ref.dtype)
                    ref[nr - nh:nr, :] = tail.astype(ref.dtype)
                carry[...] = g_prev
        for ref, g in zip(p_refs, g_p):
            @pl.when(s == 0)
            def _():
                ref[...] = jnp.zeros_like(ref)
            ref[...] += g

    return pl.pallas_call(
        body, name=name, grid=(ncol, nblk), in_specs=in_specs, out_specs=out_specs, out_shape=out_shape,
        scratch_shapes=scratch, compiler_params=_cparams(("arbitrary", "arbitrary")),
    )(*ins)


def _rowspec(arr, tb, cw, c0, cstep=1, halo=0, grad=False, ncol=1, gdt=(F32,)):
    T = arr.shape[0]
    imap = lambda j, n: (n, c0 + cstep * j)
    hblk = hmap = None
    if halo:
        q = tb // halo
        hblk, hmap = (halo, cw), (lambda j, n: (jnp.maximum(n * q - 1, 0), c0 + cstep * j))
    gshape = (T, cw * (ncol if cstep else 1)) if grad else None
    gmap = (lambda j, n: (n, cstep * j)) if grad else None
    return Row(arr, (tb, cw), imap, hblk, hmap, gshape, gmap, gdt)


def _parspec(arr, cw=None, c0=0, grad=False, ncol=1):
    if cw is None:
        return Par(arr, gshape=tuple(arr.shape) if grad else None,
                   gmap=(lambda nd: (lambda j: (0,) * nd))(arr.ndim) if grad else None)
    r = arr.shape[0]
    return Par(arr, (r, cw), lambda j: (0, c0 + j), (r, cw * ncol) if grad else None, (lambda j: (0, j)) if grad else None)


def _ln(r, g, b):
    mu = jnp.mean(r, axis=-1, keepdims=True)
    xc = r - mu
    var = jnp.mean(xc * xc, axis=-1, keepdims=True)
    return xc * lax.rsqrt(var + NORM_EPS) * g + b


def _ln_fn(j, first, rv, pv):
    return [_ln(rv[0], pv[0], pv[1])]


def _ln_fwd_fn(j, first, rv, pv):
    y = _ln(rv[0], pv[0], pv[1])
    return [y, y], []


def _loss_fn(j, first, rv, pv):
    r3, tgt = rv
    g, b = pv
    y, vjp = jax.vjp(_ln, r3, g, b)
    diff = y - tgt
    part = 0.5 * jnp.sum(diff * diff) / D_MODEL
    dr, dg, db = vjp(diff * (1.0 / D_MODEL))
    return [dr, dr], [jnp.full((SUBLANES, LANES), part, F32), dg, db]


def _mix_fn(j, first, rv, pv):
    gs, gg, ys, yg = rv
    return [_sigmoid(gs) * ys + _sigmoid(gg) * yg]


def _row_pick(x, i):
    ax = x.ndim - 2
    return jnp.sum(jnp.where(_iota(x.shape, ax) == i, x, 0.0), axis=ax, keepdims=True)


def _causal_conv(prev, cur, w, first):
    width = w.shape[0]
    prev = jnp.where(first, 0.0, prev)
    y = cur * _row_pick(w, width - 1)
    for d in range(1, width):
        y = y + _shift_halo(prev, cur, d) * _row_pick(w, width - 1 - d)
    return y


def _ffn_act_fn(j, first, rv, pv):
    (pg, cg), (pu, cu) = rv
    wg, wu, bg, bu = pv
    hg = _causal_conv(pg, cg, wg, first) + bg
    hu = _causal_conv(pu, cu, wu, first) + bu
    return [_silu(hg) * hu]


def _gdn_pre_fn(j, first, rv, pv):
    (prev, cur), = rv
    w, = pv
    t = _silu(_causal_conv(prev, cur, w, first))
    tn = t * lax.rsqrt(jnp.sum(t * t, axis=-1, keepdims=True) + 1e-6)
    return [jnp.where(j < 2 * GDN_HEADS, tn, t)]


def _gdn_gate_fn(j, first, rv, pv):
    gba, = rv
    alog, dtb, eb, eg = pv
    tb = gba.shape[0]
    beta = _sigmoid(gba)
    g = -jnp.exp(alog) * _softplus(gba + dtb)
    ri, ci = _iota((tb, tb), 0), _iota((tb, tb), 1)
    tril = jnp.where((ri // GDN_CHUNK == ci // GDN_CHUNK) & (ci <= ri), 1.0, 0.0)
    gc = _dot(tril, g, "nn", True)
    return [_dot(beta, eb, "nn", True), _dot(gc, eg, "nn", True)]


def _swa_fn(j, first, rv, pv):
    q, (kp, kc), (vp, vc) = rv
    bp, bc, sk = pv
    sp = _dot(q, kp, "nt") * (SWA_HEAD_DIM ** -0.5) + bp
    sc = _dot(q, kc, "nt") * (SWA_HEAD_DIM ** -0.5) + bc
    qi = _iota(sp.shape, 0) % SWA_BLOCK
    kj = _iota(sp.shape, 1)
    sp = jnp.where((kj > qi) & jnp.logical_not(first), sp, NEG_INF)
    sc = jnp.where(kj <= qi, sc, NEG_INF)
    m = jnp.maximum(jnp.maximum(jnp.max(sp, axis=-1, keepdims=True), jnp.max(sc, axis=-1, keepdims=True)), sk)
    m = lax.stop_gradient(m)
    ep, ec, es = jnp.exp(sp - m), jnp.exp(sc - m), jnp.exp(sk - m)
    inv = 1.0 / (jnp.sum(ep, axis=-1, keepdims=True) + jnp.sum(ec, axis=-1, keepdims=True) + es)
    vp = jnp.where(first, 0.0, vp)
    return [_dot(ep * inv, vp, "nn") + _dot(ec * inv, vc, "nn")]


def _memattn_fn(j, first, rv, pv):
    q, = rv
    k, v = pv
    s = _dot(q, k, "nt") * (MEM_HEAD_DIM ** -0.5)
    m = lax.stop_gradient(jnp.max(s, axis=-1, keepdims=True))
    e = jnp.exp(s - m)
    p = e / jnp.sum(e, axis=-1, keepdims=True)
    return [_dot(p, v, "nn")]


SOLVE_PREC = "x3"


def _gdn_heads(q, k, v, bx, gx, g64, z, nw, S):
    c = GDN_CHUNK
    q = q * (GDN_HEAD_DIM ** -0.5)
    kb, vb = k * bx, v * bx
    ri, ci = _iota((1, c, c), 1), _iota((1, c, c), 2)
    tril, strict, eye = ci <= ri, ci < ri, ci == ri
    grow = jnp.sum(jnp.where(eye, g64, 0.0), axis=1, keepdims=True)
    decay = jnp.where(tril, jnp.exp(jnp.where(tril, g64 - grow, 0.0)), 0.0)
    a = jnp.where(strict, _dot(kb, k, "nt") * decay, 0.0)
    tinv = jnp.where(eye, 1.0, 0.0) - a
    x = _dot(a, a, "nn", SOLVE_PREC)
    for i in range(5):
        tinv = tinv + _dot(tinv, x, "nn", SOLVE_PREC)
        if i < 4:
            x = _dot(x, x, "nn", SOLVE_PREC)
    eg = jnp.exp(gx)
    u = _dot(tinv, vb, "nn", SOLVE_PREC)
    w = _dot(tinv, kb * eg, "nn", SOLVE_PREC)
    ai = jnp.where(tril, _dot(q, k, "nt") * decay, 0.0)
    glast = _row_pick(gx, c - 1)
    v_new = u - _dot(w, S, "nn")
    o = _dot(q * eg, S, "nn") + _dot(ai, v_new, "nn")
    s_new = S * jnp.exp(glast) + _dot(k * jnp.exp(glast - gx), v_new, "tn")
    o = o * lax.rsqrt(jnp.mean(o * o, axis=-1, keepdims=True) + 1e-6) * nw
    return o * _silu(z), s_new


def _head_major(ref, off, width=GDN_HEAD_DIM):
    return jnp.stack([ref[:, off + h * GDN_HEAD_DIM:off + h * GDN_HEAD_DIM + width] for h in range(GDN_HEADS)])


def _gdn_chunks_fwd(qkv, bx, gx, proj, nw):
    T = qkv.shape[0]
    nc, c, hd, nh = T // GDN_CHUNK, GDN_CHUNK, GDN_HEAD_DIM, GDN_HEADS

    def body(qkv_ref, bx_ref, gx_ref, z_ref, nw_ref, y_ref, st_ref, S):
        @pl.when(pl.program_id(0) == 0)
        def _():
            S[...] = jnp.zeros_like(S)

        s_old = S[...]
        st_ref[...] = s_old
        y, s_new = _gdn_heads(_head_major(qkv_ref, 0), _head_major(qkv_ref, GDN_W), _head_major(qkv_ref, 2 * GDN_W),
                              _head_major(bx_ref, 0), _head_major(gx_ref, 0), _head_major(gx_ref, 0, c), _head_major(z_ref, 0),
                              nw_ref[...], s_old)
        for h in range(nh):
            y_ref[:, h * hd:(h + 1) * hd] = y[h].astype(y_ref.dtype)
        S[...] = s_new

    row = lambda w, cb: pl.BlockSpec((c, w), lambda n: (n, cb))
    return pl.pallas_call(
        body, name="gdn_chunks_fwd", grid=(nc,),
        in_specs=[row(3 * GDN_W, 0), row(GDN_W, 0), row(GDN_W, 0), row(GDN_W, P_GZ // GDN_W), pl.BlockSpec((1, hd), lambda n: (0, 0))],
        out_specs=[row(GDN_W, 0), pl.BlockSpec((None, nh, hd, hd), lambda n: (n, 0, 0, 0))],
        out_shape=[jax.ShapeDtypeStruct((T, GDN_W), BF16), jax.ShapeDtypeStruct((nc, nh, hd, hd), F32)],
        scratch_shapes=[pltpu.VMEM((nh, hd, hd), F32)],
        compiler_params=_cparams(("arbitrary",)),
    )(qkv, bx, gx, proj, nw)


def _gdn_chunks_bwd(qkv, bx, gx, proj, nw, states, dy):
    T = qkv.shape[0]
    nc, c, hd, nh = T // GDN_CHUNK, GDN_CHUNK, GDN_HEAD_DIM, GDN_HEADS

    def body(qkv_ref, bx_ref, gx_ref, z_ref, nw_ref, st_ref, dy_ref, dqkv_ref, dbx_ref, dgx_ref, dz_ref, dnw_ref, dS):
        @pl.when(pl.program_id(0) == 0)
        def _():
            dS[...] = jnp.zeros_like(dS)
            dnw_ref[...] = jnp.zeros_like(dnw_ref)

        args = (_head_major(qkv_ref, 0), _head_major(qkv_ref, GDN_W), _head_major(qkv_ref, 2 * GDN_W), _head_major(bx_ref, 0),
                _head_major(gx_ref, 0), _head_major(gx_ref, 0, c), _head_major(z_ref, 0), nw_ref[...], st_ref[...])
        _, vjp = jax.vjp(_gdn_heads, *args)
        dq, dk, dv, dbx, dgx, dg64, dz, dnw, dsp = vjp((_head_major(dy_ref, 0), dS[...]))
        for h in range(nh):
            sl = slice(h * hd, (h + 1) * hd)
            dqkv_ref[:, sl] = dq[h].astype(dqkv_ref.dtype)
            dqkv_ref[:, GDN_W + h * hd:GDN_W + (h + 1) * hd] = dk[h].astype(dqkv_ref.dtype)
            dqkv_ref[:, 2 * GDN_W + h * hd:2 * GDN_W + (h + 1) * hd] = dv[h].astype(dqkv_ref.dtype)
            dbx_ref[:, sl] = dbx[h]
            dgx_ref[:, sl] = dgx[h]
            dgx_ref[:, h * hd:h * hd + c] += dg64[h]
            dz_ref[:, sl] = dz[h].astype(dz_ref.dtype)
        dnw_ref[...] += dnw
        dS[...] = dsp

    row = lambda w, cb: pl.BlockSpec((c, w), lambda s: (nc - 1 - s, cb))
    return pl.pallas_call(
        body, name="gdn_chunks_bwd", grid=(nc,),
        in_specs=[row(3 * GDN_W, 0), row(GDN_W, 0), row(GDN_W, 0), row(GDN_W, P_GZ // GDN_W), pl.BlockSpec((1, hd), lambda s: (0, 0)),
                  pl.BlockSpec((None, nh, hd, hd), lambda s: (nc - 1 - s, 0, 0, 0)), row(GDN_W, 0)],
        out_specs=[row(3 * GDN_W, 0), row(GDN_W, 0), row(GDN_W, 0), row(GDN_W, 0), pl.BlockSpec((1, hd), lambda s: (0, 0))],
        out_shape=[jax.ShapeDtypeStruct((T, 3 * GDN_W), F32), jax.ShapeDtypeStruct((T, GDN_W), F32),
                   jax.ShapeDtypeStruct((T, GDN_W), F32), jax.ShapeDtypeStruct((T, GDN_W), _CDT), jax.ShapeDtypeStruct((1, hd), F32)],
        scratch_shapes=[pltpu.VMEM((nh, hd, hd), F32)],
        compiler_params=_cparams(("arbitrary",)),
    )(qkv, bx, gx, proj, nw, states, dy)


def _adamw(name, w, g, m, v):
    R, C = w.shape
    tr = _tile(R, 128, SUBLANES)

    def body(w_ref, g_ref, m_ref, v_ref, d_ref, m2_ref, v2_ref):
        g_ = g_ref[...]
        m2 = ADAM_B1 * m_ref[...] + (1.0 - ADAM_B1) * g_
        v2 = ADAM_B2 * v_ref[...] + (1.0 - ADAM_B2) * (g_ * g_)
        m_hat = m2 / (1.0 - ADAM_B1 ** ADAM_STEP)
        v_hat = v2 / (1.0 - ADAM_B2 ** ADAM_STEP)
        d_ref[...] = -ADAM_LR * (m_hat / (jnp.sqrt(v_hat) + ADAM_EPS) + ADAM_WD * w_ref[...])
        m2_ref[...] = m2
        v2_ref[...] = v2

    spec = pl.BlockSpec((tr, C), lambda i: (i, 0))
    return pl.pallas_call(
        body, name=name, grid=(R // tr,), in_specs=[spec] * 4, out_specs=[spec] * 3,
        out_shape=[jax.ShapeDtypeStruct((R, C), F32)] * 3, compiler_params=_cparams(("parallel",)),
    )(w, g, m, v)


def _addn(name, parts, out_dtype=F32):
    parts = [p if isinstance(p, tuple) else (p, None) for p in parts]
    a0, k0 = parts[0]
    R, C = a0.shape[-2:]
    tr = _tile(R, 256, 2 * SUBLANES)
    specs = []
    for a, k in parts:
        if k is None:
            specs.append(pl.BlockSpec((tr, C), lambda i: (i, 0)))
        else:
            specs.append(pl.BlockSpec((None, tr, C), (lambda kk: (lambda i: (kk, i, 0)))(k)))

    def body(*refs):
        acc = refs[0][...].astype(F32)
        for r in refs[1:-1]:
            acc = acc + r[...].astype(F32)
        refs[-1][...] = acc.astype(out_dtype)

    return pl.pallas_call(
        body, name=name, grid=(R // tr,), in_specs=specs, out_specs=pl.BlockSpec((tr, C), lambda i: (i, 0)),
        out_shape=jax.ShapeDtypeStruct((R, C), out_dtype), compiler_params=_cparams(("parallel",)),
    )(*[a for a, _ in parts])


MESH = pl.DeviceIdType.MESH
_HBM = pl.BlockSpec(memory_space=pltpu.HBM)


def _place():
    x, y, c = lax.axis_index("x"), lax.axis_index("y"), lax.axis_index("c")
    return x, y, c, [(1 - x, y), (x, 1 - y), (1 - x, 1 - y)]


def _chip_allgather(ts):
    nt = len(ts)

    def body(*refs):
        ins, outs = refs[:nt], refs[nt:2 * nt]
        send, recv = refs[2 * nt:]
        x, y, c, rel = _place()
        me, sib = 2 * x + y, (x, y, 1 - c)

        def half(t, hc):
            rh = ts[t].shape[0] // 2
            return pl.ds(pl.multiple_of(hc * rh, 16), rh)

        def rcopy(t, k, src, dst, to):
            return pltpu.make_async_remote_copy(src_ref=src, dst_ref=dst, send_sem=send.at[6 * t + k], recv_sem=recv.at[6 * t + k],
                                                device_id=to, device_id_type=MESH)

        sends = []
        for t in range(nt):
            for r, (px, py) in enumerate(rel):
                cp = rcopy(t, r, ins[t].at[half(t, c)], outs[t].at[me, half(t, c)], (px, py, c))
                cp.start()
                sends.append(cp)
        for t in range(nt):
            for r, (px, py) in enumerate(rel):
                got = outs[t].at[2 * px + py, half(t, c)]
                rcopy(t, r, got, got, (px, py, c)).wait_recv()
                fw = rcopy(t, 3 + r, got, got, sib)
                fw.start()
                sends.append(fw)
        for t in range(nt):
            for r, (px, py) in enumerate(rel):
                got = outs[t].at[2 * px + py, half(t, 1 - c)]
                rcopy(t, 3 + r, got, got, sib).wait_recv()
        for cp in sends:
            cp.wait_send()

    return pl.pallas_call(
        body, name="chip_allgather", in_specs=[_HBM] * nt, out_specs=[_HBM] * nt,
        out_shape=[jax.ShapeDtypeStruct((4,) + tuple(t.shape), t.dtype) for t in ts],
        scratch_shapes=[pltpu.SemaphoreType.DMA((6 * nt,)), pltpu.SemaphoreType.DMA((6 * nt,))],
    )(*ts)


def _pair_swap(ts):
    nt = len(ts)

    def body(*refs):
        ins, outs = refs[:nt], refs[nt:2 * nt]
        send, recv = refs[2 * nt:]
        x, y, c, _ = _place()
        cps = []
        for t in range(nt):
            rh = ts[t].shape[1] // 2
            src = ins[t].at[:, pl.ds(pl.multiple_of((1 - c) * rh, 16), rh), :]
            cp = pltpu.make_async_remote_copy(src_ref=src, dst_ref=outs[t], send_sem=send.at[t], recv_sem=recv.at[t],
                                              device_id=(x, y, 1 - c), device_id_type=MESH)
            cp.start()
            cps.append(cp)
        for cp in cps:
            cp.wait()

    return pl.pallas_call(
        body, name="pair_swap", in_specs=[_HBM] * nt, out_specs=[_HBM] * nt,
        out_shape=[jax.ShapeDtypeStruct((4, t.shape[1] // 2, t.shape[2]), t.dtype) for t in ts],
        scratch_shapes=[pltpu.SemaphoreType.DMA((nt,)), pltpu.SemaphoreType.DMA((nt,))],
    )(*ts)


def _chip_scatter(ps):
    nt = len(ps)

    def body(*refs):
        ins, outs = refs[:nt], refs[nt:2 * nt]
        send, recv = refs[2 * nt:]
        x, y, c, rel = _place()
        cps = []
        for t in range(nt):
            for r, (px, py) in enumerate(rel):
                cp = pltpu.make_async_remote_copy(src_ref=ins[t].at[2 * px + py], dst_ref=outs[t].at[r], send_sem=send.at[3 * t + r],
                                                  recv_sem=recv.at[3 * t + r], device_id=(px, py, c), device_id_type=MESH)
                cp.start()
                cps.append(cp)
        for cp in cps:
            cp.wait()

    return pl.pallas_call(
        body, name="chip_scatter", in_specs=[_HBM] * nt, out_specs=[_HBM] * nt,
        out_shape=[jax.ShapeDtypeStruct((3,) + tuple(p.shape[1:]), p.dtype) for p in ps],
        scratch_shapes=[pltpu.SemaphoreType.DMA((3 * nt,)), pltpu.SemaphoreType.DMA((3 * nt,))],
    )(*ps)


def _pair_exchange(gs):
    nt = len(gs)

    def body(*refs):
        ins, outs = refs[:nt], refs[nt:2 * nt]
        send, recv = refs[2 * nt:]
        x, y, c, _ = _place()
        cps = []
        for t in range(nt):
            cp = pltpu.make_async_remote_copy(src_ref=ins[t], dst_ref=outs[t], send_sem=send.at[t], recv_sem=recv.at[t],
                                              device_id=(x, y, 1 - c), device_id_type=MESH)
            cp.start()
            cps.append(cp)
        for cp in cps:
            cp.wait()

    return pl.pallas_call(
        body, name="pair_exchange", in_specs=[_HBM] * nt, out_specs=[_HBM] * nt,
        out_shape=[jax.ShapeDtypeStruct(tuple(g.shape), g.dtype) for g in gs],
        scratch_shapes=[pltpu.SemaphoreType.DMA((nt,)), pltpu.SemaphoreType.DMA((nt,))],
    )(*gs)


def _allgather8(v):
    m, n = v.shape

    def body(x_ref, out_ref, send, recv, lsem):
        x, y, c, rel = _place()
        me, sib = (x, y, c), (x, y, 1 - c)

        def blk(px, py, pc):
            return out_ref.at[4 * px + 2 * py + pc]

        def copy(k, block, to, src=None):
            return pltpu.make_async_remote_copy(src_ref=blk(*block) if src is None else src, dst_ref=blk(*block), send_sem=send.at[k],
                                                recv_sem=recv.at[k], device_id=to, device_id_type=MESH)

        mine = pltpu.make_async_copy(x_ref, blk(*me), lsem)
        mine.start()
        first = [copy(0, me, sib, src=x_ref)] + [copy(1 + r, me, (*ch, c), src=x_ref) for r, ch in enumerate(rel)]
        for cp in first:
            cp.start()
        passed = [copy(4 + r, (*ch, c), sib) for r, ch in enumerate(rel)]
        for r, ch in enumerate(rel):
            copy(1 + r, (*ch, c), me).wait_recv()
            passed[r].start()
        copy(0, sib, me).wait_recv()
        for r, ch in enumerate(rel):
            copy(4 + r, (*ch, 1 - c), me).wait_recv()
        for cp in first + passed:
            cp.wait_send()
        mine.wait()

    return pl.pallas_call(
        body, name="allgather8", in_specs=[pl.BlockSpec(memory_space=pltpu.VMEM)], out_specs=pl.BlockSpec(memory_space=pltpu.VMEM),
        out_shape=jax.ShapeDtypeStruct((8, m, n), v.dtype),
        scratch_shapes=[pltpu.SemaphoreType.DMA((7,)), pltpu.SemaphoreType.DMA((7,)), pltpu.SemaphoreType.DMA],
    )(v)


def _t5_bucket(dist):
    max_exact = REL_BUCKETS // 2
    d = jnp.maximum(dist, 1).astype(F32)
    large = max_exact + (jnp.log(d / max_exact) / math.log(REL_MAX_DIST / max_exact) * (REL_BUCKETS - max_exact)).astype(jnp.int32)
    large = jnp.minimum(large, REL_BUCKETS - 1)
    return jnp.where(dist < max_exact, dist, large)


def _bias_onehot():
    qi = jnp.arange(SWA_BLOCK)[:, None]
    kj = jnp.arange(SWA_BLOCK)[None, :]
    dist = jnp.concatenate([(qi + SWA_BLOCK - kj).reshape(-1), (qi - kj).reshape(-1)])
    bucket = _t5_bucket(jnp.maximum(dist, 0))
    return (bucket[None, :] == jnp.arange(REL_BUCKETS)[:, None]).astype(F32)


def _head_spread():
    lane = jnp.arange(LANES)[:, None]
    head = jnp.arange(GDN_W)[None, :] // GDN_HEAD_DIM
    return (lane == head).astype(F32), (lane == head + GDN_HEADS).astype(F32)


def _lane16(v8):
    return jnp.pad(v8.astype(F32), (GDN_HEADS, LANES - 2 * GDN_HEADS)).reshape(1, LANES)


def _stack_heads(t, nb):
    return t.reshape(nb, SWA_BLOCK, SWA_KV_HEADS, SWA_GRP, SWA_HEAD_DIM).transpose(2, 0, 3, 1, 4).reshape(
        SWA_KV_HEADS, nb * SWA_GRP * SWA_BLOCK, SWA_HEAD_DIM)


def _unstack_heads(t, nb):
    return t.reshape(SWA_KV_HEADS, nb, SWA_GRP, SWA_BLOCK, SWA_HEAD_DIM).transpose(1, 3, 0, 2, 4).reshape(nb * SWA_BLOCK, SWA_Q)


def _kv_heads(t):
    return t.reshape(t.shape[0], SWA_KV_HEADS, SWA_HEAD_DIM).transpose(1, 0, 2)


def _swa_specs(qs, ks, vs, bp, bc, sk, grad, gdt=(F32,)):
    T = ks.shape[1]
    qr = SWA_GRP * SWA_BLOCK
    g = lambda a: tuple(a.shape) if grad else None
    m3 = lambda j, n: (j, n, 0)
    h3 = lambda j, n: (j, jnp.maximum(n - 1, 0), 0)
    p3 = lambda j: (j, 0, 0)
    rows = [Row(qs, (None, qr, SWA_HEAD_DIM), m3, gshape=g(qs), gmap=m3, gdt=gdt),
            Row(ks, (None, SWA_BLOCK, SWA_HEAD_DIM), m3, (None, SWA_BLOCK, SWA_HEAD_DIM), h3, g(ks), m3, gdt),
            Row(vs, (None, SWA_BLOCK, SWA_HEAD_DIM), m3, (None, SWA_BLOCK, SWA_HEAD_DIM), h3, g(vs), m3, gdt)]
    pars = [Par(bp, (None, qr, SWA_BLOCK), p3, g(bp), p3), Par(bc, (None, qr, SWA_BLOCK), p3, g(bc), p3),
            Par(sk, (None, qr, 1), p3, g(sk), p3)]
    return rows, pars, T // SWA_BLOCK


def _fwd_bwd(x, mem, tgt, W):
    T = x.shape[0]
    nb = T // SWA_BLOCK
    tb = min(256, T)
    tbl = min(512, T)
    fwd = lambda f: (lambda *a: (f(*a), []))
    full = lambda cols, dt, t, cw: Out((T, cols), dt, (t, cw), lambda j, n: (n, j))

    xb = x.astype(_CDT)
    proj = _mm("proj", xb, W["in_p"], "nn")

    onehot_t = _bias_onehot()
    bias_flat = _mm("swa_bias", W["rel_bias"].T, onehot_t, "nn", hi=True)
    half = SWA_BLOCK * SWA_BLOCK
    bp = bias_flat[:, :half].reshape(SWA_KV_HEADS, SWA_GRP * SWA_BLOCK, SWA_BLOCK)
    bc = bias_flat[:, half:].reshape(SWA_KV_HEADS, SWA_GRP * SWA_BLOCK, SWA_BLOCK)
    sk = jnp.broadcast_to(W["swa_sinks"].reshape(SWA_KV_HEADS, SWA_GRP, 1, 1), (SWA_KV_HEADS, SWA_GRP, SWA_BLOCK, 1)).reshape(
        SWA_KV_HEADS, SWA_GRP * SWA_BLOCK, 1)
    qs = _stack_heads(proj[:, P_SQ:P_SQ + SWA_Q], nb)
    ks = _kv_heads(proj[:, P_SK:P_SK + SWA_KV])
    vs = _kv_heads(proj[:, P_SV:P_SV + SWA_KV])
    rows, pars, nblk = _swa_specs(qs, ks, vs, bp, bc, sk, False)
    o_s, = _rowmap("swa_fwd", fwd(_swa_fn), SWA_KV_HEADS, nblk, rows, pars,
                   [Out(tuple(qs.shape), F32, (None, SWA_GRP * SWA_BLOCK, SWA_HEAD_DIM), lambda j, n: (j, n, 0))])
    o_swa = _unstack_heads(o_s, nb).astype(_CDT)

    ncq = 3 * GDN_W // LANES
    tbp = min(1024, T)
    pre_rows = lambda grad: [_rowspec(proj, tbp, LANES, P_GQKV // LANES, halo=SUBLANES, grad=grad, ncol=ncq, gdt=(_CDT,))]
    pre_pars = lambda grad: [_parspec(W["gdn_conv_w"], LANES, 0, grad=grad, ncol=ncq)]
    qkv_n, = _rowmap("gdn_pre_fwd", fwd(_gdn_pre_fn), ncq, T // tbp, pre_rows(False), pre_pars(False),
                     [full(3 * GDN_W, F32, tbp, LANES)])
    eb, eg = _head_spread()
    alog_row, dtb_row = _lane16(W["gdn_a_log"]), _lane16(W["gdn_dt_bias"])
    gate_rows = lambda grad: [_rowspec(proj, tbl, LANES, P_BA // LANES, cstep=0, grad=grad, gdt=(_CDT,))]
    gate_pars = lambda grad: [_parspec(alog_row, grad=grad), _parspec(dtb_row, grad=grad), _parspec(eb), _parspec(eg)]
    bx, gx = _rowmap("gdn_gate_fwd", fwd(_gdn_gate_fn), 1, T // tbl, gate_rows(False), gate_pars(False),
                     [full(GDN_W, F32, tbl, GDN_W), full(GDN_W, F32, tbl, GDN_W)])
    nw = W["gdn_norm_w"].reshape(1, GDN_HEAD_DIM)
    o_gdn, states = _gdn_chunks_fwd(qkv_n, bx, gx, proj, nw)

    ys = _mm("y_swa", o_swa, W["br_swa"], "nn")
    yg = _mm("y_gdn", o_gdn, W["br_gdn"], "nn")
    cwm = 512
    mix_rows = lambda grad: [_rowspec(proj, tb, cwm, P_GS // cwm, grad=grad, ncol=D_MODEL // cwm, gdt=(_CDT,)),
                             _rowspec(proj, tb, cwm, P_GG // cwm, grad=grad, ncol=D_MODEL // cwm, gdt=(_CDT,)),
                             _rowspec(ys, tb, cwm, 0, grad=grad, ncol=D_MODEL // cwm, gdt=(_CDT,)),
                             _rowspec(yg, tb, cwm, 0, grad=grad, ncol=D_MODEL // cwm, gdt=(_CDT,))]
    mixed, = _rowmap("mix_fwd", fwd(_mix_fn), D_MODEL // cwm, T // tb, mix_rows(False), [], [full(D_MODEL, _CDT, tb, cwm)])
    r1 = _mm("r1", mixed, W["mix_o"], "nn", add=x, add_scale=ALPHA)

    def ln_fwd(name, r, g, b):
        return _rowmap(name, _ln_fwd_fn, 1, T // tb, [_rowspec(r, tb, D_MODEL, 0)], [_parspec(g), _parspec(b)],
                       [full(D_MODEL, F32, tb, D_MODEL), full(D_MODEL, _CDT, tb, D_MODEL)])

    def ln_bwd(name, r, g, b, ct):
        return _rowmap_bwd(name, _ln_fn, 1, T // tb, [_rowspec(r, tb, D_MODEL, 0, grad=True, gdt=(F32, _CDT))],
                           [_parspec(g, grad=True), _parspec(b, grad=True)], [_rowspec(ct, tb, D_MODEL, 0)])

    g1, b1 = W["ln1_g"].reshape(1, -1), W["ln1_b"].reshape(1, -1)
    g2, b2 = W["ln2_g"].reshape(1, -1), W["ln2_b"].reshape(1, -1)
    g3, b3 = W["ln3_g"].reshape(1, -1), W["ln3_b"].reshape(1, -1)
    x1, x1b = ln_fwd("ln1_fwd", r1, g1, b1)

    qm = _mm("mem_q", x1b, W["mem_q"], "nn")
    kvm = _mm("mem_kv", mem, W["mem_kv"], "nn")
    ma_rows = lambda grad: [_rowspec(qm, tbl, MEM_HEAD_DIM, 0, grad=grad, ncol=MEM_HEADS, gdt=(_CDT,))]
    ma_pars = lambda grad: [_parspec(kvm, MEM_HEAD_DIM, 0, grad=grad, ncol=MEM_HEADS),
                            _parspec(kvm, MEM_HEAD_DIM, MEM_HEADS, grad=grad, ncol=MEM_HEADS)]
    om, = _rowmap("memattn_fwd", fwd(_memattn_fn), MEM_HEADS, T // tbl, ma_rows(False), ma_pars(False),
                  [full(MEM_W, _CDT, tbl, MEM_HEAD_DIM)])
    r2 = _mm("r2", om, W["mem_o"], "nn", add=x1, add_scale=ALPHA)
    x2, x2b = ln_fwd("ln2_fwd", r2, g2, b2)

    hcat = _mm("ffn_up", x2b, W["up_p"], "nn")
    cwf = 512
    ncf = D_FF_PAD // cwf
    cw_p, cb_p = W["ffn_conv_w_p"], W["ffn_conv_b_p"]
    tbf = min(512, T)
    ffn_rows = lambda grad: [_rowspec(hcat, tbf, cwf, 0, halo=SUBLANES, grad=grad, ncol=ncf, gdt=(_CDT,)),
                             _rowspec(hcat, tbf, cwf, ncf, halo=SUBLANES, grad=grad, ncol=ncf, gdt=(_CDT,))]
    ffn_pars = lambda grad: [_parspec(cw_p, cwf, 0, grad=grad, ncol=ncf), _parspec(cw_p, cwf, ncf, grad=grad, ncol=ncf),
                             _parspec(cb_p, cwf, 0, grad=grad, ncol=ncf), _parspec(cb_p, cwf, ncf, grad=grad, ncol=ncf)]
    act, = _rowmap("ffn_act_fwd", fwd(_ffn_act_fn), ncf, T // tbf, ffn_rows(False), ffn_pars(False), [full(D_FF_PAD, _CDT, tbf, cwf)])
    r3 = _mm("r3", act, W["down_p"], "nn", add=x2, add_scale=ALPHA)
    dr3, dr3b, lacc, dg3, db3 = _rowmap("ln3_loss", _loss_fn, 1, T // tb, [_rowspec(r3, tb, D_MODEL, 0), _rowspec(tgt, tb, D_MODEL, 0)],
                                        [_parspec(g3), _parspec(b3)], [full(D_MODEL, F32, tb, D_MODEL), full(D_MODEL, _CDT, tb, D_MODEL)],
                                  accs=[(SUBLANES, LANES), (1, D_MODEL), (1, D_MODEL)])
    loss = lacc[0, 0]

    G = {}
    G["down_p"] = _mm("dw_down", act, dr3b, "tn", out_dtype=_GDT)
    dact = _mm("d_act", dr3b, W["down_p"], "nt")
    dhg, dhu, dcwg, dcwu, dcbg, dcbu = _rowmap_bwd("ffn_act_bwd", _ffn_act_fn, ncf, T // tbf, ffn_rows(True), ffn_pars(True),
                                                   [_rowspec(dact, tbf, cwf, 0)])
    w_gate, w_upp = W["up_p"][:, :D_FF_PAD], W["up_p"][:, D_FF_PAD:]
    dx2 = _mm("dx2_gate", dhg, w_gate, "nt", add=dr3, add_scale=ALPHA)
    dx2 = _mm("dx2_up", dhu, w_upp, "nt", add=dx2)
    G["up_p"] = jnp.concatenate([_mm("dw_gate", x2b, dhg, "tn", out_dtype=_GDT), _mm("dw_up", x2b, dhu, "tn", out_dtype=_GDT)], axis=1)
    G["ffn_conv_w"] = jnp.concatenate([dcwg[:, :D_FF], dcwu[:, :D_FF]], axis=1)
    G["ffn_conv_b"] = jnp.concatenate([dcbg[0, :D_FF], dcbu[0, :D_FF]])
    G["ln3_g"], G["ln3_b"] = dg3[0], db3[0]

    dr2, dr2b, dg2, db2 = ln_bwd("ln2_bwd", r2, g2, b2, dx2)
    G["ln2_g"], G["ln2_b"] = dg2[0], db2[0]
    G["mem_o"] = _mm("dw_mem_o", om, dr2b, "tn", out_dtype=_GDT)
    dom = _mm("d_om", dr2b, W["mem_o"], "nt", out_dtype=_CDT)
    dqm, dkm, dvm = _rowmap_bwd("memattn_bwd", _memattn_fn, MEM_HEADS, T // tbl, ma_rows(True), ma_pars(True),
                                [_rowspec(dom, tbl, MEM_HEAD_DIM, 0)])
    G["mem_kv"] = _mm("dw_mem_kv", mem.astype(_CDT), jnp.concatenate([dkm, dvm], axis=1).astype(_CDT), "tn", out_dtype=_GDT)
    G["mem_q"] = _mm("dw_mem_q", x1b, dqm, "tn", out_dtype=_GDT)
    dx1 = _mm("dx1", dqm, W["mem_q"], "nt", add=dr2, add_scale=ALPHA)

    dr1, dr1b, dg1, db1 = ln_bwd("ln1_bwd", r1, g1, b1, dx1)
    G["ln1_g"], G["ln1_b"] = dg1[0], db1[0]
    G["mix_o"] = _mm("dw_mix_o", mixed, dr1b, "tn", out_dtype=_GDT)
    dmixed = _mm("d_mixed", dr1b, W["mix_o"], "nt")
    dgs, dgg, dys, dyg = _rowmap_bwd("mix_bwd", _mix_fn, D_MODEL // cwm, T // tb, mix_rows(True), [], [_rowspec(dmixed, tb, cwm, 0)])
    G["br_swa"] = _mm("dw_br_swa", o_swa, dys, "tn", out_dtype=_GDT)
    G["br_gdn"] = _mm("dw_br_gdn", o_gdn, dyg, "tn", out_dtype=_GDT)
    do_swa = _mm("d_o_swa", dys, W["br_swa"], "nt", out_dtype=_CDT)
    do_gdn = _mm("d_o_gdn", dyg, W["br_gdn"], "nt")

    rows, pars, nblk = _swa_specs(qs, ks, vs, bp, bc, sk, True, (_CDT,))
    m3 = lambda j, n: (j, n, 0)
    dqs, dks, dvs, dbp, dbc, dsk = _rowmap_bwd("swa_bwd", _swa_fn, SWA_KV_HEADS, nblk, rows, pars,
                                               [Row(_stack_heads(do_swa, nb), (None, SWA_GRP * SWA_BLOCK, SWA_HEAD_DIM), m3)])
    d_swa = jnp.concatenate([_unstack_heads(dqs, nb), dks.transpose(1, 0, 2).reshape(T, SWA_KV),
                             dvs.transpose(1, 0, 2).reshape(T, SWA_KV)], axis=1)
    dbias = jnp.concatenate([dbp.reshape(SWA_HEADS, half), dbc.reshape(SWA_HEADS, half)], axis=1)
    G["rel_bias"] = _mm("d_rel_bias", dbias, onehot_t.T, "nn", hi=True).T
    G["swa_sinks"] = _mm("d_sinks", dsk.reshape(SWA_HEADS, SWA_BLOCK), jnp.ones((SWA_BLOCK, LANES), F32), "nn", hi=True)[:, 0]

    dqkv_n, dbx, dgx, dz, dnw = _gdn_chunks_bwd(qkv_n, bx, gx, proj, nw, states, do_gdn)
    G["gdn_norm_w"] = dnw[0]
    dgba, dalog, ddtb = _rowmap_bwd("gdn_gate_bwd", _gdn_gate_fn, 1, T // tbl, gate_rows(True), gate_pars(True),
                                    [_rowspec(dbx, tbl, GDN_W, 0), _rowspec(dgx, tbl, GDN_W, 0)])
    G["gdn_a_log"], G["gdn_dt_bias"] = dalog[0, GDN_HEADS:2 * GDN_HEADS], ddtb[0, GDN_HEADS:2 * GDN_HEADS]
    dgqkv, dcw_gdn = _rowmap_bwd("gdn_pre_bwd", _gdn_pre_fn, ncq, T // tbp, pre_rows(True), pre_pars(True),
                                 [_rowspec(dqkv_n, tbp, LANES, 0)])
    G["gdn_conv_w"] = dcw_gdn

    dproj = jnp.concatenate([dgs, dgg, dgqkv, dz, d_swa, dgba, jnp.zeros((T, P_END - P_USED), _CDT)], axis=1)
    dx = _mm("dx", dproj, W["in_p"], "nt", add=dr1, add_scale=ALPHA)
    G["in_p"] = _mm("dw_in", xb, dproj, "tn", out_dtype=_GDT)
    return loss, dx, G


W_NAMES = ["w_in", "rel_bias", "swa_sinks", "gdn_conv_w", "gdn_a_log", "gdn_dt_bias", "gdn_norm_w", "w_br_swa", "w_br_gdn",
           "w_mix_o", "ln1_g", "ln1_b", "w_mem_q", "w_mem_kv", "w_mem_o", "ln2_g", "ln2_b", "w_up", "ffn_conv_w", "ffn_conv_b",
           "w_down", "ln3_g", "ln3_b"]
BIG = ["w_in", "w_br_swa", "w_br_gdn", "w_mix_o", "w_mem_q", "w_mem_kv", "w_mem_o", "w_up", "w_down"]
SMALL = [n for n in W_NAMES if n not in BIG]
COL_SHARDED = ["w_in", "w_br_swa", "w_br_gdn", "w_mem_o", "w_up"]


def _pack(arrs):
    flat = []
    for a in arrs:
        f = a.reshape(-1).astype(F32)
        flat.append(jnp.pad(f, (0, (-f.shape[0]) % LANES)))
    f = jnp.concatenate(flat)
    f = jnp.pad(f, (0, (-f.shape[0]) % (16 * LANES)))
    return f.reshape(-1, LANES)


def _unpack(p, shapes):
    f, out, off = p.reshape(-1), [], 0
    for s in shapes:
        n = int(np.prod(s)) if len(s) else 1
        out.append(f[off:off + n].reshape(s))
        off += n + (-n) % LANES
    return out


def _merge_shards(d):
    cat = lambda names: jnp.concatenate([d[n] for n in names], axis=-2)
    return [d["w_in"], d["w_up"], cat(["w_br_swa", "w_br_gdn", "w_mem_q", "w_mem_o"]), cat(["w_mix_o", "w_down"]), d["w_mem_kv"]]


def _split_shards(ts):
    a, b, c, dd, e = ts
    return {"w_in": a, "w_up": b, "w_br_swa": c[..., 0:1024, :], "w_br_gdn": c[..., 1024:2048, :], "w_mem_q": c[..., 2048:2560, :],
            "w_mem_o": c[..., 2560:3072, :], "w_mix_o": dd[..., 0:512, :], "w_down": dd[..., 512:, :], "w_mem_kv": e}


def _to_full(name, t):
    if name in COL_SHARDED:
        return t.transpose(1, 0, 2).reshape(t.shape[1], 4 * t.shape[2])
    return t.reshape(4 * t.shape[1], t.shape[2])


def _to_chips(name, t):
    if name in COL_SHARDED:
        return t.reshape(t.shape[0], 4, t.shape[1] // 4).transpose(1, 0, 2)
    return t.reshape(4, t.shape[0] // 4, t.shape[1])


_IN_OFF = np.cumsum((0,) + IN_WIDTHS)


def _in_to_padded(w):
    o = _IN_OFF
    cut = lambda i, k: w[:, o[i]:o[k]]
    return jnp.concatenate([cut(9, 10), cut(10, 11), cut(3, 6), cut(6, 7), cut(0, 1), cut(1, 2), cut(2, 3), cut(7, 9),
                            jnp.zeros((w.shape[0], P_END - P_BA - 2 * GDN_HEADS), w.dtype)], axis=1)


def _in_from_padded(p):
    return jnp.concatenate([p[:, P_SQ:P_SQ + SWA_Q], p[:, P_SK:P_SK + SWA_KV], p[:, P_SV:P_SV + SWA_KV], p[:, P_GQKV:P_GQKV + 3 * GDN_W],
                            p[:, P_GZ:P_GZ + GDN_W], p[:, P_BA:P_BA + 2 * GDN_HEADS], p[:, P_GS:P_GS + D_MODEL], p[:, P_GG:P_GG + D_MODEL]],
                           axis=1)


def _ff_pad(t, axis):
    g, u = jnp.split(t, 2, axis=axis)
    pad = [(0, 0)] * t.ndim
    pad[axis] = (0, D_FF_PAD - D_FF)
    return jnp.concatenate([jnp.pad(g, pad), jnp.pad(u, pad)], axis=axis)


def _ff_unpad(t, axis):
    g, u = jnp.split(t, 2, axis=axis)
    return jnp.concatenate([lax.slice_in_dim(g, 0, D_FF, axis=axis), lax.slice_in_dim(u, 0, D_FF, axis=axis)], axis=axis)


def _assemble_weights(full, small):
    W = dict(small)
    W["in_p"] = _in_to_padded(full["w_in"])
    W["up_p"] = _ff_pad(full["w_up"], 1)
    W["down_p"] = jnp.pad(full["w_down"], ((0, D_FF_PAD - D_FF), (0, 0)))
    W["br_swa"], W["br_gdn"], W["mix_o"] = full["w_br_swa"], full["w_br_gdn"], full["w_mix_o"]
    W["mem_q"], W["mem_kv"], W["mem_o"] = full["w_mem_q"], full["w_mem_kv"], full["w_mem_o"]
    W["ffn_conv_w_p"] = _ff_pad(small["ffn_conv_w"], 1)
    W["ffn_conv_b_p"] = _ff_pad(small["ffn_conv_b"].reshape(1, -1), 1)
    return W


def _full_grads(G):
    return {"w_in": _in_from_padded(G["in_p"]), "w_up": _ff_unpad(G["up_p"], 1), "w_down": G["down_p"][:D_FF],
            "w_br_swa": G["br_swa"], "w_br_gdn": G["br_gdn"], "w_mix_o": G["mix_o"], "w_mem_q": G["mem_q"],
            "w_mem_kv": G["mem_kv"], "w_mem_o": G["mem_o"]}


def kernel(x, mem, w_in, rel_bias, swa_sinks, gdn_conv_w, gdn_a_log, gdn_dt_bias, gdn_norm_w, w_br_swa, w_br_gdn, w_mix_o, ln1_g, ln1_b, w_mem_q, w_mem_kv, w_mem_o, ln2_g, ln2_b, w_up, ffn_conv_w, ffn_conv_b, w_down, ln3_g, ln3_b, loss_target, m_w_in, m_rel_bias, m_swa_sinks, m_gdn_conv_w, m_gdn_a_log, m_gdn_dt_bias, m_gdn_norm_w, m_w_br_swa, m_w_br_gdn, m_w_mix_o, m_ln1_g, m_ln1_b, m_w_mem_q, m_w_mem_kv, m_w_mem_o, m_ln2_g, m_ln2_b, m_w_up, m_ffn_conv_w, m_ffn_conv_b, m_w_down, m_ln3_g, m_ln3_b, v_w_in, v_rel_bias, v_swa_sinks, v_gdn_conv_w, v_gdn_a_log, v_gdn_dt_bias, v_gdn_norm_w, v_w_br_swa, v_w_br_gdn, v_w_mix_o, v_ln1_g, v_ln1_b, v_w_mem_q, v_w_mem_kv, v_w_mem_o, v_ln2_g, v_ln2_b, v_w_up, v_ffn_conv_w, v_ffn_conv_b, v_w_down, v_ln3_g, v_ln3_b):
    a = dict(locals())
    w = {n: a[n] for n in W_NAMES}
    m = {n: a["m_" + n] for n in W_NAMES}
    v = {n: a["v_" + n] for n in W_NAMES}
    chip = 2 * lax.axis_index("x") + lax.axis_index("y")
    core = lax.axis_index("c")
    sq = lambda t: t.reshape(t.shape[1:]) if (t.ndim > 1 and t.shape[0] == 1 and t is not rel_bias) else t

    shards = _merge_shards({n: sq(w[n]).astype(_CDT) for n in BIG})
    conv_pack = _pack([sq(ffn_conv_w), sq(gdn_conv_w)])
    mine = shards + [conv_pack]
    slot = lax.broadcasted_iota(jnp.int32, (4, 1, 1), 0)
    gathered = [jnp.where(slot == chip, t[None], g) for g, t in zip(_chip_allgather(mine), mine)]
    full = {n: _to_full(n, t) for n, t in _split_shards(gathered[:-1]).items()}
    fcw_sh, gcw_sh = sq(ffn_conv_w).shape, sq(gdn_conv_w).shape
    conv = [_unpack(gathered[-1][k], [fcw_sh, gcw_sh]) for k in range(4)]
    small = {n: sq(w[n]) for n in SMALL}
    small["ffn_conv_w"] = jnp.concatenate([cv[0] for cv in conv], axis=1)
    small["gdn_conv_w"] = jnp.concatenate([cv[1] for cv in conv], axis=1)
    W = _assemble_weights(full, small)

    loss, dx, G = _fwd_bwd(x[0], mem[0], loss_target[0], W)

    small_names = SMALL
    small_shapes = [()] + [tuple(G[n].shape) for n in small_names]
    packed = _pack([loss] + [G[n] for n in small_names])
    allp = _allgather8(packed)
    tot = _addn("small_sum", [(allp, k) for k in range(8)])
    parts = _unpack(tot, small_shapes)
    loss_tot, gsmall = parts[0], dict(zip(small_names, parts[1:]))
    gsmall["ffn_conv_w"] = lax.dynamic_slice_in_dim(gsmall["ffn_conv_w"], chip * fcw_sh[1], fcw_sh[1], axis=1)
    gsmall["gdn_conv_w"] = lax.dynamic_slice_in_dim(gsmall["gdn_conv_w"], chip * gcw_sh[1], gcw_sh[1], axis=1)

    gfull = _full_grads(G)
    gch = _merge_shards({n: _to_chips(n, gfull[n]) for n in BIG})
    theirs = _pair_swap(gch)
    pair = []
    for t, (mine, got) in enumerate(zip(gch, theirs)):
        rh = mine.shape[1] // 2
        mine_h = lax.dynamic_slice_in_dim(mine, core * rh, rh, axis=1)
        pair.append(_addn(f"pair_sum{t}", [mine_h.reshape(4 * rh, -1), got.reshape(4 * rh, -1)], _GDT).reshape(4, rh, -1))
    others = _chip_scatter(pair)
    halves = []
    for t, (p, o) in enumerate(zip(pair, others)):
        own = lax.dynamic_index_in_dim(p, chip, 0, keepdims=False)
        halves.append(_addn(f"chip_sum{t}", [own, (o, 0), (o, 1), (o, 2)]))
    both = []
    for h, o in zip(halves, _pair_exchange(halves)):
        both.append(jnp.concatenate([jnp.where(core == 0, h, o), jnp.where(core == 0, o, h)], axis=0))
    gbig = _split_shards(both)

    outs = {}
    for n in BIG:
        d_, m_, v_ = _adamw("adamw_" + n, sq(w[n]), gbig[n], sq(m[n]), sq(v[n]))
        outs[n] = (gbig[n], d_, m_, v_)
    sm_shapes = [sq(w[n]).shape if n != "rel_bias" else w[n].shape for n in SMALL]
    sw, sg, sm_, sv = (_pack([sq(t[n]) if n != "rel_bias" else t[n] for n in SMALL]) for t in (w, gsmall, m, v))
    d_, m_, v_ = _adamw("adamw_small", sw, sg, sm_, sv)
    for n, g_, dd, mm_, vv in zip(SMALL, _unpack(sg, sm_shapes), _unpack(d_, sm_shapes), _unpack(m_, sm_shapes), _unpack(v_, sm_shapes)):
        outs[n] = (g_, dd, mm_, vv)

    res = [loss_tot.reshape(()), dx.reshape(x.shape)]
    for k in range(4):
        res += [outs[n][k].reshape(w[n].shape) for n in W_NAMES]
    return tuple(res)
```

```python
import functools
import math

import jax
import jax.numpy as jnp
import numpy as np
from jax import lax
from jax.experimental import pallas as pl
from jax.experimental.pallas import tpu as pltpu

F32 = jnp.float32
BF16 = jnp.bfloat16
_CDT = BF16
_GDT = BF16

D_MODEL = 2048
SWA_HEADS, SWA_KV_HEADS, SWA_HEAD_DIM, SWA_BLOCK = 16, 2, 64, 128
SWA_GRP = SWA_HEADS // SWA_KV_HEADS
REL_BUCKETS, REL_MAX_DIST = 32, 128
GDN_HEADS, GDN_HEAD_DIM, GDN_CONV, GDN_CHUNK = 8, 128, 4, 64
MEM_HEADS, MEM_HEAD_DIM = 4, 128
D_FF, D_FF_PAD, FFN_CONV = 5504, 5632, 3
SWA_Q, SWA_KV, GDN_W, MEM_W = 1024, 128, 1024, 512
IN_WIDTHS = (SWA_Q, SWA_KV, SWA_KV, GDN_W, GDN_W, GDN_W, GDN_W, GDN_HEADS, GDN_HEADS, D_MODEL, D_MODEL)
NORM_EPS = 1e-5
ALPHA = 2.0 ** 0.25
NEG_INF = -1e30
ADAM_LR, ADAM_B1, ADAM_B2, ADAM_EPS, ADAM_WD, ADAM_STEP = 0.001, 0.9, 0.999, 1e-08, 0.01, 10
LANES, SUBLANES = 128, 8
VMEM_LIMIT = 56 * 1024 * 1024

P_GS, P_GG, P_GQKV, P_GZ, P_SQ, P_SK, P_SV, P_BA, P_USED, P_END = 0, 2048, 4096, 7168, 8192, 9216, 9344, 9472, 9600, 9728


def _tile(dim, pref, align=LANES):
    if dim <= pref:
        return dim
    t = (pref // align) * align
    while t >= align:
        if dim % t == 0:
            return t
        t -= align
    return dim


_DIMS = {"nn": (((1,), (0,)), ((), ())), "nt": (((1,), (1,)), ((), ())), "tn": (((0,), (0,)), ((), ()))}
_BDIMS = {"nn": (((2,), (1,)), ((0,), (0,))), "nt": (((2,), (2,)), ((0,), (0,))), "tn": (((1,), (1,)), ((0,), (0,)))}


def _raw_dot(a, b, form, hi):
    dims = (_BDIMS if a.ndim == 3 else _DIMS)[form]
    if hi == "x3":
        a, b = a.astype(F32), b.astype(F32)
        ah, bh = a.astype(BF16), b.astype(BF16)
        al, bl = (a - ah.astype(F32)).astype(BF16), (b - bh.astype(F32)).astype(BF16)
        d = lambda p, q: lax.dot_general(p, q, dims, preferred_element_type=F32)
        return d(ah, bh) + (d(ah, bl) + d(al, bh))
    if hi:
        return lax.dot_general(a.astype(F32), b.astype(F32), dims, precision=lax.Precision.HIGHEST, preferred_element_type=F32)
    return lax.dot_general(a.astype(_CDT), b.astype(_CDT), dims, preferred_element_type=F32)


@functools.partial(jax.custom_vjp, nondiff_argnums=(2, 3))
def _dot(a, b, form, hi=False):
    return _raw_dot(a, b, form, hi)


def _dot_fwd(a, b, form, hi):
    return _raw_dot(a, b, form, hi), (a, b)


def _dot_bwd(form, hi, res, g):
    a, b = res
    if form == "nn":
        da, db = _raw_dot(g, b, "nt", hi), _raw_dot(a, g, "tn", hi)
    elif form == "nt":
        da, db = _raw_dot(g, b, "nn", hi), _raw_dot(g, a, "tn", hi)
    else:
        da, db = _raw_dot(b, g, "nt", hi), _raw_dot(a, g, "nn", hi)
    return da.astype(a.dtype), db.astype(b.dtype)


_dot.defvjp(_dot_fwd, _dot_bwd)


@functools.partial(jax.custom_vjp, nondiff_argnums=(2,))
def _shift_halo(prev, cur, d):
    assert prev.shape[0] == SUBLANES
    return pltpu.roll(jnp.concatenate([prev, cur], axis=0), d, 0)[SUBLANES:]


def _shift_halo_fwd(prev, cur, d):
    return _shift_halo(prev, cur, d), None


def _shift_halo_bwd(d, _, g):
    nh = SUBLANES
    ext = jnp.concatenate([jnp.zeros((nh, g.shape[1]), g.dtype), g], axis=0)
    r = pltpu.roll(ext, ext.shape[0] - d, 0)
    return r[:nh], r[nh:]


_shift_halo.defvjp(_shift_halo_fwd, _shift_halo_bwd)


def _sigmoid(x):
    return 1.0 / (1.0 + jnp.exp(-x))


def _silu(x):
    return x * _sigmoid(x)


def _softplus(x):
    return jnp.maximum(x, 0.0) + jnp.log(1.0 + jnp.exp(-jnp.abs(x)))


def _iota(shape, axis):
    return lax.broadcasted_iota(jnp.int32, shape, axis)


def _cparams(sem):
    return pltpu.CompilerParams(dimension_semantics=sem, vmem_limit_bytes=VMEM_LIMIT)


class _ride:
    def __init__(self, rider, n_in, n_out, n_scr):
        self.rider = rider
        self.ins = rider.ins if rider else []
        n_rin = len(self.ins)
        self.out_shapes = rider.out_shapes if rider else []
        n_rout = len(self.out_shapes)
        self.in_specs, self.out_specs = [_HBM] * n_rin, [_HBM] * n_rout
        self.scratch = rider.sems() if rider else []
        self.o0 = n_in + n_rin
        self.s0 = self.o0 + n_out + n_rout
        self._rin = slice(n_in, n_in + n_rin)
        self._rout = slice(self.o0 + n_out, self.s0)
        self._sem = self.s0 + n_scr

    def _args(self, refs):
        return refs[self._rin], refs[self._rout], refs[self._sem], refs[self._sem + 1]

    def at_start(self, refs, cond):
        if self.rider:
            pl.when(cond)(lambda: self.rider.start(*self._args(refs)))

    def at_end(self, refs, cond):
        if self.rider:
            pl.when(cond)(lambda: self.rider.finish(*self._args(refs)))


def _mm(name, a, b, form, out_dtype=F32, add=None, add_scale=1.0, hi=False, tm=1024, tn=1024, tk=2048, rider=None):
    if form == "nn":
        (M, K), (K2, N) = a.shape, b.shape
    elif form == "nt":
        (M, K), (N, K2) = a.shape, b.shape
    else:
        (K, M), (K2, N) = a.shape, b.shape
    assert K == K2, (name, a.shape, b.shape, form)
    tm, tn, tk = _tile(M, tm), _tile(N, tn), _tile(K, tk)
    nk = K // tk
    a_spec = pl.BlockSpec((tk, tm), lambda i, j, k: (k, i)) if form == "tn" else pl.BlockSpec((tm, tk), lambda i, j, k: (i, k))
    b_spec = pl.BlockSpec((tn, tk), lambda i, j, k: (j, k)) if form == "nt" else pl.BlockSpec((tk, tn), lambda i, j, k: (k, j))
    o_spec = pl.BlockSpec((tm, tn), lambda i, j, k: (i, j))
    has_add = add is not None

    def finish(r, c_ref, o_ref):
        if has_add:
            r = r + add_scale * c_ref[...].astype(F32)
        o_ref[...] = r.astype(out_dtype)

    n_own = 3 if has_add else 2
    grid = (M // tm, N // tn, nk)
    rd = _ride(rider, n_own, 1, 1 if nk > 1 else 0)

    def body(*refs):
        a_ref, b_ref = refs[:2]
        c_ref = refs[2] if has_add else None
        o_ref = refs[rd.o0]
        pid = [pl.program_id(d) for d in range(3)]
        rd.at_start(refs, (pid[0] == 0) & (pid[1] == 0) & (pid[2] == 0))
        if nk == 1:
            finish(_raw_dot(a_ref[...], b_ref[...], form, hi), c_ref, o_ref)
        else:
            acc = refs[rd.s0]

            @pl.when(pid[2] == 0)
            def _():
                acc[...] = jnp.zeros_like(acc)

            acc[...] += _raw_dot(a_ref[...], b_ref[...], form, hi)

            @pl.when(pid[2] == nk - 1)
            def _():
                finish(acc[...], c_ref, o_ref)
        rd.at_end(refs, (pid[0] == grid[0] - 1) & (pid[1] == grid[1] - 1) & (pid[2] == nk - 1))

    ins = [a, b] + ([add] if has_add else [])
    specs = [a_spec, b_spec] + ([o_spec] if has_add else [])
    res = pl.pallas_call(
        body, name=name, grid=grid, in_specs=specs + rd.in_specs, out_specs=[o_spec] + rd.out_specs,
        out_shape=[jax.ShapeDtypeStruct((M, N), out_dtype)] + rd.out_shapes,
        scratch_shapes=([pltpu.VMEM((tm, tn), F32)] if nk > 1 else []) + rd.scratch,
        compiler_params=_cparams(("arbitrary",) * 3 if rider else ("parallel", "parallel", "arbitrary")),
    )(*ins, *rd.ins)
    return (res[0], res[1:]) if rider else res[0]


class Row:
    def __init__(self, arr, blk, imap, hblk=None, hmap=None, gshape=None, gmap=None, gdt=(F32,)):
        self.arr, self.blk, self.imap, self.hblk, self.hmap, self.gshape, self.gmap = arr, blk, imap, hblk, hmap, gshape, gmap
        self.gdt = gdt


class Par:
    def __init__(self, arr, blk=None, imap=None, gshape=None, gmap=None):
        self.arr = arr
        self.blk = tuple(arr.shape) if blk is None else blk
        nd = len(self.blk)
        self.imap = (lambda j: (0,) * nd) if imap is None else imap
        self.gshape, self.gmap = gshape, gmap


class Out:
    def __init__(self, shape, dtype, blk, imap):
        self.shape, self.dtype, self.blk, self.imap = shape, dtype, blk, imap


def _rows_of(blk):
    return [d for d in blk if d is not None][0]


def _rowmap(name, fn, ncol, nblk, rows, pars, outs, accs=()):
    in_specs, ins = [], []
    for r in rows:
        ins.append(r.arr)
        in_specs.append(pl.BlockSpec(r.blk, r.imap))
        if r.hblk is not None:
            ins.append(r.arr)
            in_specs.append(pl.BlockSpec(r.hblk, r.hmap))
    for p in pars:
        ins.append(p.arr)
        in_specs.append(pl.BlockSpec(p.blk, (lambda im: (lambda j, n: im(j)))(p.imap)))
    out_specs = [pl.BlockSpec(o.blk, o.imap) for o in outs]
    out_shape = [jax.ShapeDtypeStruct(o.shape, o.dtype) for o in outs]
    for a in accs:
        out_specs.append(pl.BlockSpec(a, (lambda nd: (lambda j, n: (0,) * nd))(len(a))))
        out_shape.append(jax.ShapeDtypeStruct(a, F32))
    n_in = len(ins)

    def body(*refs):
        j, n = pl.program_id(0), pl.program_id(1)
        it = iter(refs[:n_in])
        rvals = []
        for r in rows:
            cur = next(it)[...]
            rvals.append((next(it)[...], cur) if r.hblk is not None else cur)
        pvals = [next(it)[...] for _ in pars]
        o_refs = refs[n_in:n_in + len(outs)]
        a_refs = refs[n_in + len(outs):]
        ovals, avals = fn(j, n == 0, rvals, pvals)
        for ref, v in zip(o_refs, ovals):
            ref[...] = v.astype(ref.dtype)
        if accs:
            @pl.when((j == 0) & (n == 0))
            def _():
                for ref in a_refs:
                    ref[...] = jnp.zeros_like(ref)
            for ref, v in zip(a_refs, avals):
                ref[...] += v

    res = pl.pallas_call(
        body, name=name, grid=(ncol, nblk), in_specs=in_specs, out_specs=out_specs, out_shape=out_shape,
        compiler_params=_cparams(("arbitrary", "arbitrary")),
    )(*ins)
    return res


def _rowmap_bwd(name, fn, ncol, nblk, rows, pars, cts):
    rev = lambda im: (lambda j, s: im(j, nblk - 1 - s))
    in_specs, ins = [], []
    for r in rows:
        ins.append(r.arr)
        in_specs.append(pl.BlockSpec(r.blk, rev(r.imap)))
        if r.hblk is not None:
            ins.append(r.arr)
            in_specs.append(pl.BlockSpec(r.hblk, rev(r.hmap)))
    for p in pars:
        ins.append(p.arr)
        in_specs.append(pl.BlockSpec(p.blk, (lambda im: (lambda j, s: im(j)))(p.imap)))
    for c in cts:
        ins.append(c.arr)
        in_specs.append(pl.BlockSpec(c.blk, rev(c.imap)))
    n_in = len(ins)
    drows = [i for i, r in enumerate(rows) if r.gshape is not None]
    dpars = [i for i, p in enumerate(pars) if p.gshape is not None]
    out_specs, out_shape, scratch = [], [], []
    for i in drows:
        r = rows[i]
        for dt in r.gdt:
            out_specs.append(pl.BlockSpec(r.blk, rev(r.gmap)))
            out_shape.append(jax.ShapeDtypeStruct(r.gshape, dt))
        if r.hblk is not None:
            scratch.append(pltpu.VMEM(tuple(d for d in r.hblk if d is not None), F32))
    n_drow_out = len(out_specs)
    for i in dpars:
        p = pars[i]
        out_specs.append(pl.BlockSpec(p.blk, (lambda im: (lambda j, s: im(j)))(p.gmap)))
        out_shape.append(jax.ShapeDtypeStruct(p.gshape, F32))

    def body(*refs):
        j, s = pl.program_id(0), pl.program_id(1)
        first = s == nblk - 1
        it = iter(refs[:n_in])
        rvals = []
        for r in rows:
            cur = next(it)[...]
            rvals.append((next(it)[...], cur) if r.hblk is not None else cur)
        pvals = [next(it)[...] for _ in pars]
        cvals = [next(it)[...].astype(F32) for _ in cts]
        g_refs = iter(refs[n_in:n_in + n_drow_out])
        p_refs = refs[n_in + n_drow_out:n_in + n_drow_out + len(dpars)]
        carries = iter(refs[n_in + n_drow_out + len(dpars):])

        def f(dr, dp):
            rv, pv = list(rvals), list(pvals)
            for i, v in zip(drows, dr):
                rv[i] = v
            for i, v in zip(dpars, dp):
                pv[i] = v
            return fn(j, first, rv, pv)

        _, vjp = jax.vjp(f, [rvals[i] for i in drows], [pvals[i] for i in dpars])
        g_r, g_p = vjp(cvals)
        for i, g in zip(drows, g_r):
            r = rows[i]
            if r.hblk is None:
                for _ in r.gdt:
                    ref = next(g_refs)
                    ref[...] = g.astype(ref.dtype)
            else:
                g_prev, g_cur = g
                carry = next(carries)
                nr, nh = g_cur.shape[0], g_prev.shape[0]
                tail = g_cur[nr - nh:nr] + jnp.where(s > 0, carry[...], 0.0)
                for _ in r.gdt:
                    ref = next(g_refs)
                    if nr > nh:
                        ref[0:nr - nh, :] = g_cur[0:nr - nh].astype(ref.dtype)
                    ref[nr - nh:nr, :] = tail.astype(ref.dtype)
                carry[...] = g_prev
        for ref, g in zip(p_refs, g_p):
            @pl.when(s == 0)
            def _():
                ref[...] = jnp.zeros_like(ref)
            ref[...] += g

    return pl.pallas_call(
        body, name=name, grid=(ncol, nblk), in_specs=in_specs, out_specs=out_specs, out_shape=out_shape,
        scratch_shapes=scratch, compiler_params=_cparams(("arbitrary", "arbitrary")),
    )(*ins)


def _rowspec(arr, tb, cw, c0, cstep=1, halo=0, grad=False, ncol=1, gdt=(F32,)):
    T = arr.shape[0]
    imap = lambda j, n: (n, c0 + cstep * j)
    hblk = hmap = None
    if halo:
        q = tb // halo
        hblk, hmap = (halo, cw), (lambda j, n: (jnp.maximum(n * q - 1, 0), c0 + cstep * j))
    gshape = (T, cw * (ncol if cstep else 1)) if grad else None
    gmap = (lambda j, n: (n, cstep * j)) if grad else None
    return Row(arr, (tb, cw), imap, hblk, hmap, gshape, gmap, gdt)


def _parspec(arr, cw=None, c0=0, grad=False, ncol=1):
    if cw is None:
        return Par(arr, gshape=tuple(arr.shape) if grad else None,
                   gmap=(lambda nd: (lambda j: (0,) * nd))(arr.ndim) if grad else None)
    r = arr.shape[0]
    return Par(arr, (r, cw), lambda j: (0, c0 + j), (r, cw * ncol) if grad else None, (lambda j: (0, j)) if grad else None)


def _ln(r, g, b):
    mu = jnp.mean(r, axis=-1, keepdims=True)
    xc = r - mu
    var = jnp.mean(xc * xc, axis=-1, keepdims=True)
    return xc * lax.rsqrt(var + NORM_EPS) * g + b


def _ln_fn(j, first, rv, pv):
    return [_ln(rv[0], pv[0], pv[1])]


def _ln_fwd_fn(j, first, rv, pv):
    y = _ln(rv[0], pv[0], pv[1])
    return [y, y], []


def _loss_fn(j, first, rv, pv):
    r3, tgt = rv
    g, b = pv
    y, vjp = jax.vjp(_ln, r3, g, b)
    diff = y - tgt
    part = 0.5 * jnp.sum(diff * diff) / D_MODEL
    dr, dg, db = vjp(diff * (1.0 / D_MODEL))
    return [dr, dr], [jnp.full((SUBLANES, LANES), part, F32), dg, db]


def _mix_fn(j, first, rv, pv):
    gs, gg, ys, yg = rv
    return [_sigmoid(gs) * ys + _sigmoid(gg) * yg]


def _row_pick(x, i):
    ax = x.ndim - 2
    return jnp.sum(jnp.where(_iota(x.shape, ax) == i, x, 0.0), axis=ax, keepdims=True)


def _causal_conv(prev, cur, w, first):
    width = w.shape[0]
    prev = jnp.where(first, 0.0, prev)
    y = cur * _row_pick(w, width - 1)
    for d in range(1, width):
        y = y + _shift_halo(prev, cur, d) * _row_pick(w, width - 1 - d)
    return y


def _ffn_act_fn(j, first, rv, pv):
    (pg, cg), (pu, cu) = rv
    wg, wu, bg, bu = pv
    hg = _causal_conv(pg, cg, wg, first) + bg
    hu = _causal_conv(pu, cu, wu, first) + bu
    return [_silu(hg) * hu]


def _gdn_pre_fn(j, first, rv, pv):
    (prev, cur), = rv
    w, = pv
    t = _silu(_causal_conv(prev, cur, w, first))
    tn = t * lax.rsqrt(jnp.sum(t * t, axis=-1, keepdims=True) + 1e-6)
    return [jnp.where(j < 2 * GDN_HEADS, tn, t)]


def _gdn_gate_fn(j, first, rv, pv):
    gba, = rv
    alog, dtb, eb, eg = pv
    tb = gba.shape[0]
    beta = _sigmoid(gba)
    g = -jnp.exp(alog) * _softplus(gba + dtb)
    ri, ci = _iota((tb, tb), 0), _iota((tb, tb), 1)
    tril = jnp.where((ri // GDN_CHUNK == ci // GDN_CHUNK) & (ci <= ri), 1.0, 0.0)
    gc = _dot(tril, g, "nn", True)
    return [_dot(beta, eb, "nn", True), _dot(gc, eg, "nn", True)]


def _swa_fn(j, first, rv, pv):
    q, (kp, kc), (vp, vc) = rv
    bp, bc, sk = pv
    sp = _dot(q, kp, "nt") * (SWA_HEAD_DIM ** -0.5) + bp
    sc = _dot(q, kc, "nt") * (SWA_HEAD_DIM ** -0.5) + bc
    qi = _iota(sp.shape, 0) % SWA_BLOCK
    kj = _iota(sp.shape, 1)
    sp = jnp.where((kj > qi) & jnp.logical_not(first), sp, NEG_INF)
    sc = jnp.where(kj <= qi, sc, NEG_INF)
    m = jnp.maximum(jnp.maximum(jnp.max(sp, axis=-1, keepdims=True), jnp.max(sc, axis=-1, keepdims=True)), sk)
    m = lax.stop_gradient(m)
    ep, ec, es = jnp.exp(sp - m), jnp.exp(sc - m), jnp.exp(sk - m)
    inv = 1.0 / (jnp.sum(ep, axis=-1, keepdims=True) + jnp.sum(ec, axis=-1, keepdims=True) + es)
    vp = jnp.where(first, 0.0, vp)
    return [_dot(ep * inv, vp, "nn") + _dot(ec * inv, vc, "nn")]


def _memattn_fn(j, first, rv, pv):
    q, = rv
    k, v = pv
    s = _dot(q, k, "nt") * (MEM_HEAD_DIM ** -0.5)
    m = lax.stop_gradient(jnp.max(s, axis=-1, keepdims=True))
    e = jnp.exp(s - m)
    p = e / jnp.sum(e, axis=-1, keepdims=True)
    return [_dot(p, v, "nn")]


SOLVE_PREC = "x3"


def _gdn_heads(q, k, v, bx, gx, g64, z, nw, S):
    c = GDN_CHUNK
    q = q * (GDN_HEAD_DIM ** -0.5)
    kb, vb = k * bx, v * bx
    ri, ci = _iota((1, c, c), 1), _iota((1, c, c), 2)
    tril, strict, eye = ci <= ri, ci < ri, ci == ri
    grow = jnp.sum(jnp.where(eye, g64, 0.0), axis=1, keepdims=True)
    decay = jnp.where(tril, jnp.exp(jnp.where(tril, g64 - grow, 0.0)), 0.0)
    a = jnp.where(strict, _dot(kb, k, "nt") * decay, 0.0)
    tinv = jnp.where(eye, 1.0, 0.0) - a
    x = _dot(a, a, "nn", SOLVE_PREC)
    for i in range(5):
        tinv = tinv + _dot(tinv, x, "nn", SOLVE_PREC)
        if i < 4:
            x = _dot(x, x, "nn", SOLVE_PREC)
    eg = jnp.exp(gx)
    u = _dot(tinv, vb, "nn", SOLVE_PREC)
    w = _dot(tinv, kb * eg, "nn", SOLVE_PREC)
    ai = jnp.where(tril, _dot(q, k, "nt") * decay, 0.0)
    glast = _row_pick(gx, c - 1)
    v_new = u - _dot(w, S, "nn")
    o = _dot(q * eg, S, "nn") + _dot(ai, v_new, "nn")
    s_new = S * jnp.exp(glast) + _dot(k * jnp.exp(glast - gx), v_new, "tn")
    o = o * lax.rsqrt(jnp.mean(o * o, axis=-1, keepdims=True) + 1e-6) * nw
    return o * _silu(z), s_new


def _head_major(ref, off, width=GDN_HEAD_DIM):
    return jnp.stack([ref[:, off + h * GDN_HEAD_DIM:off + h * GDN_HEAD_DIM + width] for h in range(GDN_HEADS)])


def _gdn_chunks_fwd(qkv, bx, gx, proj, nw, rider=None):
    T = qkv.shape[0]
    nc, c, hd, nh = T // GDN_CHUNK, GDN_CHUNK, GDN_HEAD_DIM, GDN_HEADS
    rd = _ride(rider, 5, 2, 1)

    def body(*refs):
        qkv_ref, bx_ref, gx_ref, z_ref, nw_ref = refs[:5]
        y_ref, st_ref = refs[rd.o0:rd.o0 + 2]
        S = refs[rd.s0]
        rd.at_start(refs, pl.program_id(0) == 0)

        @pl.when(pl.program_id(0) == 0)
        def _():
            S[...] = jnp.zeros_like(S)

        s_old = S[...]
        st_ref[...] = s_old
        y, s_new = _gdn_heads(_head_major(qkv_ref, 0), _head_major(qkv_ref, GDN_W), _head_major(qkv_ref, 2 * GDN_W),
                              _head_major(bx_ref, 0), _head_major(gx_ref, 0), _head_major(gx_ref, 0, c), _head_major(z_ref, 0),
                              nw_ref[...], s_old)
        for h in range(nh):
            y_ref[:, h * hd:(h + 1) * hd] = y[h].astype(y_ref.dtype)
        S[...] = s_new
        rd.at_end(refs, pl.program_id(0) == nc - 1)

    row = lambda w, cb: pl.BlockSpec((c, w), lambda n: (n, cb))
    res = pl.pallas_call(
        body, name="gdn_chunks_fwd", grid=(nc,),
        in_specs=[row(3 * GDN_W, 0), row(GDN_W, 0), row(GDN_W, 0), row(GDN_W, P_GZ // GDN_W),
                  pl.BlockSpec((1, hd), lambda n: (0, 0))] + rd.in_specs,
        out_specs=[row(GDN_W, 0), pl.BlockSpec((None, nh, hd, hd), lambda n: (n, 0, 0, 0))] + rd.out_specs,
        out_shape=[jax.ShapeDtypeStruct((T, GDN_W), BF16), jax.ShapeDtypeStruct((nc, nh, hd, hd), F32)] + rd.out_shapes,
        scratch_shapes=[pltpu.VMEM((nh, hd, hd), F32)] + rd.scratch,
        compiler_params=_cparams(("arbitrary",)),
    )(qkv, bx, gx, proj, nw, *rd.ins)
    return res[0], res[1], res[2:]


def _gdn_chunks_bwd(qkv, bx, gx, proj, nw, states, dy, rider=None):
    T = qkv.shape[0]
    nc, c, hd, nh = T // GDN_CHUNK, GDN_CHUNK, GDN_HEAD_DIM, GDN_HEADS
    rd = _ride(rider, 7, 5, 1)

    def body(*refs):
        qkv_ref, bx_ref, gx_ref, z_ref, nw_ref, st_ref, dy_ref = refs[:7]
        dqkv_ref, dbx_ref, dgx_ref, dz_ref, dnw_ref = refs[rd.o0:rd.o0 + 5]
        dS = refs[rd.s0]
        rd.at_start(refs, pl.program_id(0) == 0)

        @pl.when(pl.program_id(0) == 0)
        def _():
            dS[...] = jnp.zeros_like(dS)
            dnw_ref[...] = jnp.zeros_like(dnw_ref)

        args = (_head_major(qkv_ref, 0), _head_major(qkv_ref, GDN_W), _head_major(qkv_ref, 2 * GDN_W), _head_major(bx_ref, 0),
                _head_major(gx_ref, 0), _head_major(gx_ref, 0, c), _head_major(z_ref, 0), nw_ref[...], st_ref[...])
        _, vjp = jax.vjp(_gdn_heads, *args)
        dq, dk, dv, dbx, dgx, dg64, dz, dnw, dsp = vjp((_head_major(dy_ref, 0), dS[...]))
        for h in range(nh):
            sl = slice(h * hd, (h + 1) * hd)
            dqkv_ref[:, sl] = dq[h].astype(dqkv_ref.dtype)
            dqkv_ref[:, GDN_W + h * hd:GDN_W + (h + 1) * hd] = dk[h].astype(dqkv_ref.dtype)
            dqkv_ref[:, 2 * GDN_W + h * hd:2 * GDN_W + (h + 1) * hd] = dv[h].astype(dqkv_ref.dtype)
            dbx_ref[:, sl] = dbx[h]
            dgx_ref[:, sl] = dgx[h]
            dgx_ref[:, h * hd:h * hd + c] += dg64[h]
            dz_ref[:, sl] = dz[h].astype(dz_ref.dtype)
        dnw_ref[...] += dnw
        dS[...] = dsp
        rd.at_end(refs, pl.program_id(0) == nc - 1)

    row = lambda w, cb: pl.BlockSpec((c, w), lambda s: (nc - 1 - s, cb))
    res = pl.pallas_call(
        body, name="gdn_chunks_bwd", grid=(nc,),
        in_specs=[row(3 * GDN_W, 0), row(GDN_W, 0), row(GDN_W, 0), row(GDN_W, P_GZ // GDN_W), pl.BlockSpec((1, hd), lambda s: (0, 0)),
                  pl.BlockSpec((None, nh, hd, hd), lambda s: (nc - 1 - s, 0, 0, 0)), row(GDN_W, 0)] + rd.in_specs,
        out_specs=[row(3 * GDN_W, 0), row(GDN_W, 0), row(GDN_W, 0), row(GDN_W, 0),
                   pl.BlockSpec((1, hd), lambda s: (0, 0))] + rd.out_specs,
        out_shape=[jax.ShapeDtypeStruct((T, 3 * GDN_W), F32), jax.ShapeDtypeStruct((T, GDN_W), F32),
                   jax.ShapeDtypeStruct((T, GDN_W), F32), jax.ShapeDtypeStruct((T, GDN_W), _CDT),
                   jax.ShapeDtypeStruct((1, hd), F32)] + rd.out_shapes,
        scratch_shapes=[pltpu.VMEM((nh, hd, hd), F32)] + rd.scratch,
        compiler_params=_cparams(("arbitrary",)),
    )(qkv, bx, gx, proj, nw, states, dy, *rd.ins)
    res = list(res)
    return res[:5] + [res[5:]]


def _adamw(name, w, g, m, v):
    R, C = w.shape
    tr = _tile(R, 128, SUBLANES)

    def body(w_ref, g_ref, m_ref, v_ref, d_ref, m2_ref, v2_ref):
        g_ = g_ref[...]
        m2 = ADAM_B1 * m_ref[...] + (1.0 - ADAM_B1) * g_
        v2 = ADAM_B2 * v_ref[...] + (1.0 - ADAM_B2) * (g_ * g_)
        m_hat = m2 / (1.0 - ADAM_B1 ** ADAM_STEP)
        v_hat = v2 / (1.0 - ADAM_B2 ** ADAM_STEP)
        d_ref[...] = -ADAM_LR * (m_hat / (jnp.sqrt(v_hat) + ADAM_EPS) + ADAM_WD * w_ref[...])
        m2_ref[...] = m2
        v2_ref[...] = v2

    spec = pl.BlockSpec((tr, C), lambda i: (i, 0))
    return pl.pallas_call(
        body, name=name, grid=(R // tr,), in_specs=[spec] * 4, out_specs=[spec] * 3,
        out_shape=[jax.ShapeDtypeStruct((R, C), F32)] * 3, compiler_params=_cparams(("parallel",)),
    )(w, g, m, v)


def _addn(name, parts, out_dtype=F32):
    parts = [p if isinstance(p, tuple) else (p, None) for p in parts]
    a0, k0 = parts[0]
    R, C = a0.shape[-2:]
    tr = _tile(R, 256, 2 * SUBLANES)
    specs = []
    for a, k in parts:
        if k is None:
            specs.append(pl.BlockSpec((tr, C), lambda i: (i, 0)))
        else:
            specs.append(pl.BlockSpec((None, tr, C), (lambda kk: (lambda i: (kk, i, 0)))(k)))

    def body(*refs):
        acc = refs[0][...].astype(F32)
        for r in refs[1:-1]:
            acc = acc + r[...].astype(F32)
        refs[-1][...] = acc.astype(out_dtype)

    return pl.pallas_call(
        body, name=name, grid=(R // tr,), in_specs=specs, out_specs=pl.BlockSpec((tr, C), lambda i: (i, 0)),
        out_shape=jax.ShapeDtypeStruct((R, C), out_dtype), compiler_params=_cparams(("parallel",)),
    )(*[a for a, _ in parts])


MESH = pl.DeviceIdType.MESH
_HBM = pl.BlockSpec(memory_space=pltpu.HBM)


def _place():
    x, y, c = lax.axis_index("x"), lax.axis_index("y"), lax.axis_index("c")
    return x, y, c, [(1 - x, y), (x, 1 - y), (1 - x, 1 - y)]


class _Rider:
    def __init__(self, ins, out_shapes, nsem, start, finish):
        self.ins, self.out_shapes, self.nsem, self.start, self.finish = list(ins), list(out_shapes), nsem, start, finish

    def sems(self):
        return [pltpu.SemaphoreType.DMA((self.nsem,)), pltpu.SemaphoreType.DMA((self.nsem,))]


def _run_rider(name, rd):
    n_in, n_out = len(rd.ins), len(rd.out_shapes)

    def body(*refs):
        ins, outs, (send, recv) = refs[:n_in], refs[n_in:n_in + n_out], refs[n_in + n_out:]
        rd.start(ins, outs, send, recv)
        rd.finish(ins, outs, send, recv)

    return pl.pallas_call(body, name=name, in_specs=[_HBM] * n_in, out_specs=[_HBM] * n_out, out_shape=rd.out_shapes,
                          scratch_shapes=rd.sems())(*rd.ins)


def _gather_rider(ts):
    nt = len(ts)

    def half(t, hc):
        rh = ts[t].shape[0] // 2
        return pl.ds(pl.multiple_of(hc * rh, 16), rh)

    def rcopy(send, recv, t, k, src, dst, to):
        return pltpu.make_async_remote_copy(src_ref=src, dst_ref=dst, send_sem=send.at[6 * t + k], recv_sem=recv.at[6 * t + k],
                                            device_id=to, device_id_type=MESH)

    def first_hop(ins, outs, send, recv, t, r, px, py, c, me):
        return rcopy(send, recv, t, r, ins[t].at[half(t, c)], outs[t].at[me, half(t, c)], (px, py, c))

    def start(ins, outs, send, recv):
        x, y, c, rel = _place()
        for t in range(nt):
            for r, (px, py) in enumerate(rel):
                first_hop(ins, outs, send, recv, t, r, px, py, c, 2 * x + y).start()

    def finish(ins, outs, send, recv):
        x, y, c, rel = _place()
        sib = (x, y, 1 - c)
        passed = []
        for t in range(nt):
            for r, (px, py) in enumerate(rel):
                got = outs[t].at[2 * px + py, half(t, c)]
                rcopy(send, recv, t, r, got, got, (px, py, c)).wait_recv()
                fw = rcopy(send, recv, t, 3 + r, got, got, sib)
                fw.start()
                passed.append(fw)
        for t in range(nt):
            for r, (px, py) in enumerate(rel):
                got = outs[t].at[2 * px + py, half(t, 1 - c)]
                rcopy(send, recv, t, 3 + r, got, got, sib).wait_recv()
        for t in range(nt):
            for r, (px, py) in enumerate(rel):
                first_hop(ins, outs, send, recv, t, r, px, py, c, 2 * x + y).wait_send()
        for fw in passed:
            fw.wait_send()

    return _Rider(ts, [jax.ShapeDtypeStruct((4,) + tuple(t.shape), t.dtype) for t in ts], 6 * nt, start, finish)


def _scatter_rider(ps):
    nt = len(ps)

    def copy(ins, outs, send, recv, t, r, px, py, c):
        return pltpu.make_async_remote_copy(src_ref=ins[t].at[2 * px + py], dst_ref=outs[t].at[r], send_sem=send.at[3 * t + r],
                                            recv_sem=recv.at[3 * t + r], device_id=(px, py, c), device_id_type=MESH)

    def start(ins, outs, send, recv):
        x, y, c, rel = _place()
        for t in range(nt):
            for r, (px, py) in enumerate(rel):
                copy(ins, outs, send, recv, t, r, px, py, c).start()

    def finish(ins, outs, send, recv):
        x, y, c, rel = _place()
        for t in range(nt):
            for r, (px, py) in enumerate(rel):
                copy(ins, outs, send, recv, t, r, px, py, c).wait()

    return _Rider(ps, [jax.ShapeDtypeStruct((3,) + tuple(p.shape[1:]), p.dtype) for p in ps], 3 * nt, start, finish)


def _pair_swap(tag, ts):
    nt = len(ts)

    def body(*refs):
        ins, outs = refs[:nt], refs[nt:2 * nt]
        send, recv = refs[2 * nt:]
        x, y, c, _ = _place()
        cps = []
        for t in range(nt):
            rh = ts[t].shape[1] // 2
            src = ins[t].at[:, pl.ds(pl.multiple_of((1 - c) * rh, 16), rh), :]
            cp = pltpu.make_async_remote_copy(src_ref=src, dst_ref=outs[t], send_sem=send.at[t], recv_sem=recv.at[t],
                                              device_id=(x, y, 1 - c), device_id_type=MESH)
            cp.start()
            cps.append(cp)
        for cp in cps:
            cp.wait()

    return pl.pallas_call(
        body, name="pair_swap_" + tag, in_specs=[_HBM] * nt, out_specs=[_HBM] * nt,
        out_shape=[jax.ShapeDtypeStruct((4, t.shape[1] // 2, t.shape[2]), t.dtype) for t in ts],
        scratch_shapes=[pltpu.SemaphoreType.DMA((nt,)), pltpu.SemaphoreType.DMA((nt,))],
    )(*ts)


def _pair_exchange(gs):
    nt = len(gs)

    def body(*refs):
        ins, outs = refs[:nt], refs[nt:2 * nt]
        send, recv = refs[2 * nt:]
        x, y, c, _ = _place()
        cps = []
        for t in range(nt):
            cp = pltpu.make_async_remote_copy(src_ref=ins[t], dst_ref=outs[t], send_sem=send.at[t], recv_sem=recv.at[t],
                                              device_id=(x, y, 1 - c), device_id_type=MESH)
            cp.start()
            cps.append(cp)
        for cp in cps:
            cp.wait()

    return pl.pallas_call(
        body, name="pair_exchange", in_specs=[_HBM] * nt, out_specs=[_HBM] * nt,
        out_shape=[jax.ShapeDtypeStruct(tuple(g.shape), g.dtype) for g in gs],
        scratch_shapes=[pltpu.SemaphoreType.DMA((nt,)), pltpu.SemaphoreType.DMA((nt,))],
    )(*gs)


def _allgather8(v):
    m, n = v.shape

    def body(x_ref, out_ref, send, recv, lsem):
        x, y, c, rel = _place()
        me, sib = (x, y, c), (x, y, 1 - c)

        def blk(px, py, pc):
            return out_ref.at[4 * px + 2 * py + pc]

        def copy(k, block, to, src=None):
            return pltpu.make_async_remote_copy(src_ref=blk(*block) if src is None else src, dst_ref=blk(*block), send_sem=send.at[k],
                                                recv_sem=recv.at[k], device_id=to, device_id_type=MESH)

        mine = pltpu.make_async_copy(x_ref, blk(*me), lsem)
        mine.start()
        first = [copy(0, me, sib, src=x_ref)] + [copy(1 + r, me, (*ch, c), src=x_ref) for r, ch in enumerate(rel)]
        for cp in first:
            cp.start()
        passed = [copy(4 + r, (*ch, c), sib) for r, ch in enumerate(rel)]
        for r, ch in enumerate(rel):
            copy(1 + r, (*ch, c), me).wait_recv()
            passed[r].start()
        copy(0, sib, me).wait_recv()
        for r, ch in enumerate(rel):
            copy(4 + r, (*ch, 1 - c), me).wait_recv()
        for cp in first + passed:
            cp.wait_send()
        mine.wait()

    return pl.pallas_call(
        body, name="allgather8", in_specs=[pl.BlockSpec(memory_space=pltpu.VMEM)], out_specs=pl.BlockSpec(memory_space=pltpu.VMEM),
        out_shape=jax.ShapeDtypeStruct((8, m, n), v.dtype),
        scratch_shapes=[pltpu.SemaphoreType.DMA((7,)), pltpu.SemaphoreType.DMA((7,)), pltpu.SemaphoreType.DMA],
    )(v)


def _t5_bucket(dist):
    max_exact = REL_BUCKETS // 2
    d = jnp.maximum(dist, 1).astype(F32)
    large = max_exact + (jnp.log(d / max_exact) / math.log(REL_MAX_DIST / max_exact) * (REL_BUCKETS - max_exact)).astype(jnp.int32)
    large = jnp.minimum(large, REL_BUCKETS - 1)
    return jnp.where(dist < max_exact, dist, large)


def _bias_onehot():
    qi = jnp.arange(SWA_BLOCK)[:, None]
    kj = jnp.arange(SWA_BLOCK)[None, :]
    dist = jnp.concatenate([(qi + SWA_BLOCK - kj).reshape(-1), (qi - kj).reshape(-1)])
    bucket = _t5_bucket(jnp.maximum(dist, 0))
    return (bucket[None, :] == jnp.arange(REL_BUCKETS)[:, None]).astype(F32)


def _head_spread():
    lane = jnp.arange(LANES)[:, None]
    head = jnp.arange(GDN_W)[None, :] // GDN_HEAD_DIM
    return (lane == head).astype(F32), (lane == head + GDN_HEADS).astype(F32)


def _lane16(v8):
    return jnp.pad(v8.astype(F32), (GDN_HEADS, LANES - 2 * GDN_HEADS)).reshape(1, LANES)


def _stack_heads(t, nb):
    return t.reshape(nb, SWA_BLOCK, SWA_KV_HEADS, SWA_GRP, SWA_HEAD_DIM).transpose(2, 0, 3, 1, 4).reshape(
        SWA_KV_HEADS, nb * SWA_GRP * SWA_BLOCK, SWA_HEAD_DIM)


def _unstack_heads(t, nb):
    return t.reshape(SWA_KV_HEADS, nb, SWA_GRP, SWA_BLOCK, SWA_HEAD_DIM).transpose(1, 3, 0, 2, 4).reshape(nb * SWA_BLOCK, SWA_Q)


def _kv_heads(t):
    return t.reshape(t.shape[0], SWA_KV_HEADS, SWA_HEAD_DIM).transpose(1, 0, 2)


def _swa_specs(qs, ks, vs, bp, bc, sk, grad, gdt=(F32,)):
    T = ks.shape[1]
    qr = SWA_GRP * SWA_BLOCK
    g = lambda a: tuple(a.shape) if grad else None
    m3 = lambda j, n: (j, n, 0)
    h3 = lambda j, n: (j, jnp.maximum(n - 1, 0), 0)
    p3 = lambda j: (j, 0, 0)
    rows = [Row(qs, (None, qr, SWA_HEAD_DIM), m3, gshape=g(qs), gmap=m3, gdt=gdt),
            Row(ks, (None, SWA_BLOCK, SWA_HEAD_DIM), m3, (None, SWA_BLOCK, SWA_HEAD_DIM), h3, g(ks), m3, gdt),
            Row(vs, (None, SWA_BLOCK, SWA_HEAD_DIM), m3, (None, SWA_BLOCK, SWA_HEAD_DIM), h3, g(vs), m3, gdt)]
    pars = [Par(bp, (None, qr, SWA_BLOCK), p3, g(bp), p3), Par(bc, (None, qr, SWA_BLOCK), p3, g(bc), p3),
            Par(sk, (None, qr, 1), p3, g(sk), p3)]
    return rows, pars, T // SWA_BLOCK


class _LocalWeights:
    def __init__(self, W):
        self.W = W

    def w1(self):
        return self.W

    def rider_a(self):
        return None

    def w2(self, got):
        return self.W

    def rider_b(self):
        return None

    def w3(self, got):
        return self.W

    def rider_g(self, G):
        return None

    def g_done(self, got):
        pass


def _fwd_bwd(x, mem, tgt, src):
    W = dict(src.w1())
    T = x.shape[0]
    nb = T // SWA_BLOCK
    tb = min(256, T)
    tbl = min(512, T)
    fwd = lambda f: (lambda *a: (f(*a), []))
    full = lambda cols, dt, t, cw: Out((T, cols), dt, (t, cw), lambda j, n: (n, j))

    xb = x.astype(_CDT)
    ra = src.rider_a()
    proj = _mm("proj", xb, W["in_p"], "nn", rider=ra)
    proj, got = proj if ra is not None else (proj, None)
    W.update(src.w2(got))

    onehot_t = _bias_onehot()
    bias_flat = _mm("swa_bias", W["rel_bias"].T, onehot_t, "nn", hi=True)
    half = SWA_BLOCK * SWA_BLOCK
    bp = bias_flat[:, :half].reshape(SWA_KV_HEADS, SWA_GRP * SWA_BLOCK, SWA_BLOCK)
    bc = bias_flat[:, half:].reshape(SWA_KV_HEADS, SWA_GRP * SWA_BLOCK, SWA_BLOCK)
    sk = jnp.broadcast_to(W["swa_sinks"].reshape(SWA_KV_HEADS, SWA_GRP, 1, 1), (SWA_KV_HEADS, SWA_GRP, SWA_BLOCK, 1)).reshape(
        SWA_KV_HEADS, SWA_GRP * SWA_BLOCK, 1)
    qs = _stack_heads(proj[:, P_SQ:P_SQ + SWA_Q], nb)
    ks = _kv_heads(proj[:, P_SK:P_SK + SWA_KV])
    vs = _kv_heads(proj[:, P_SV:P_SV + SWA_KV])
    rows, pars, nblk = _swa_specs(qs, ks, vs, bp, bc, sk, False)
    o_s, = _rowmap("swa_fwd", fwd(_swa_fn), SWA_KV_HEADS, nblk, rows, pars,
                   [Out(tuple(qs.shape), F32, (None, SWA_GRP * SWA_BLOCK, SWA_HEAD_DIM), lambda j, n: (j, n, 0))])
    o_swa = _unstack_heads(o_s, nb).astype(_CDT)

    ncq = 3 * GDN_W // LANES
    tbp = min(1024, T)
    pre_rows = lambda grad: [_rowspec(proj, tbp, LANES, P_GQKV // LANES, halo=SUBLANES, grad=grad, ncol=ncq, gdt=(_CDT,))]
    pre_pars = lambda grad: [_parspec(W["gdn_conv_w"], LANES, 0, grad=grad, ncol=ncq)]
    qkv_n, = _rowmap("gdn_pre_fwd", fwd(_gdn_pre_fn), ncq, T // tbp, pre_rows(False), pre_pars(False),
                     [full(3 * GDN_W, F32, tbp, LANES)])
    eb, eg = _head_spread()
    alog_row, dtb_row = _lane16(W["gdn_a_log"]), _lane16(W["gdn_dt_bias"])
    gate_rows = lambda grad: [_rowspec(proj, tbl, LANES, P_BA // LANES, cstep=0, grad=grad, gdt=(_CDT,))]
    gate_pars = lambda grad: [_parspec(alog_row, grad=grad), _parspec(dtb_row, grad=grad), _parspec(eb), _parspec(eg)]
    bx, gx = _rowmap("gdn_gate_fwd", fwd(_gdn_gate_fn), 1, T // tbl, gate_rows(False), gate_pars(False),
                     [full(GDN_W, F32, tbl, GDN_W), full(GDN_W, F32, tbl, GDN_W)])
    nw = W["gdn_norm_w"].reshape(1, GDN_HEAD_DIM)
    o_gdn, states, got = _gdn_chunks_fwd(qkv_n, bx, gx, proj, nw, rider=src.rider_b())
    W.update(src.w3(got))

    ys = _mm("y_swa", o_swa, W["br_swa"], "nn")
    yg = _mm("y_gdn", o_gdn, W["br_gdn"], "nn")
    cwm = 512
    mix_rows = lambda grad: [_rowspec(proj, tb, cwm, P_GS // cwm, grad=grad, ncol=D_MODEL // cwm, gdt=(_CDT,)),
                             _rowspec(proj, tb, cwm, P_GG // cwm, grad=grad, ncol=D_MODEL // cwm, gdt=(_CDT,)),
                             _rowspec(ys, tb, cwm, 0, grad=grad, ncol=D_MODEL // cwm, gdt=(_CDT,)),
                             _rowspec(yg, tb, cwm, 0, grad=grad, ncol=D_MODEL // cwm, gdt=(_CDT,))]
    mixed, = _rowmap("mix_fwd", fwd(_mix_fn), D_MODEL // cwm, T // tb, mix_rows(False), [], [full(D_MODEL, _CDT, tb, cwm)])
    r1 = _mm("r1", mixed, W["mix_o"], "nn", add=x, add_scale=ALPHA)

    def ln_fwd(name, r, g, b):
        return _rowmap(name, _ln_fwd_fn, 1, T // tb, [_rowspec(r, tb, D_MODEL, 0)], [_parspec(g), _parspec(b)],
                       [full(D_MODEL, F32, tb, D_MODEL), full(D_MODEL, _CDT, tb, D_MODEL)])

    def ln_bwd(name, r, g, b, ct):
        return _rowmap_bwd(name, _ln_fn, 1, T // tb, [_rowspec(r, tb, D_MODEL, 0, grad=True, gdt=(F32, _CDT))],
                           [_parspec(g, grad=True), _parspec(b, grad=True)], [_rowspec(ct, tb, D_MODEL, 0)])

    g1, b1 = W["ln1_g"].reshape(1, -1), W["ln1_b"].reshape(1, -1)
    g2, b2 = W["ln2_g"].reshape(1, -1), W["ln2_b"].reshape(1, -1)
    g3, b3 = W["ln3_g"].reshape(1, -1), W["ln3_b"].reshape(1, -1)
    x1, x1b = ln_fwd("ln1_fwd", r1, g1, b1)

    qm = _mm("mem_q", x1b, W["mem_q"], "nn")
    kvm = _mm("mem_kv", mem, W["mem_kv"], "nn")
    ma_rows = lambda grad: [_rowspec(qm, tbl, MEM_HEAD_DIM, 0, grad=grad, ncol=MEM_HEADS, gdt=(_CDT,))]
    ma_pars = lambda grad: [_parspec(kvm, MEM_HEAD_DIM, 0, grad=grad, ncol=MEM_HEADS),
                            _parspec(kvm, MEM_HEAD_DIM, MEM_HEADS, grad=grad, ncol=MEM_HEADS)]
    om, = _rowmap("memattn_fwd", fwd(_memattn_fn), MEM_HEADS, T // tbl, ma_rows(False), ma_pars(False),
                  [full(MEM_W, _CDT, tbl, MEM_HEAD_DIM)])
    r2 = _mm("r2", om, W["mem_o"], "nn", add=x1, add_scale=ALPHA)
    x2, x2b = ln_fwd("ln2_fwd", r2, g2, b2)

    hcat = _mm("ffn_up", x2b, W["up_p"], "nn")
    cwf = 512
    ncf = D_FF_PAD // cwf
    cw_p, cb_p = W["ffn_conv_w_p"], W["ffn_conv_b_p"]
    tbf = min(512, T)
    ffn_rows = lambda grad: [_rowspec(hcat, tbf, cwf, 0, halo=SUBLANES, grad=grad, ncol=ncf, gdt=(_CDT,)),
                             _rowspec(hcat, tbf, cwf, ncf, halo=SUBLANES, grad=grad, ncol=ncf, gdt=(_CDT,))]
    ffn_pars = lambda grad: [_parspec(cw_p, cwf, 0, grad=grad, ncol=ncf), _parspec(cw_p, cwf, ncf, grad=grad, ncol=ncf),
                             _parspec(cb_p, cwf, 0, grad=grad, ncol=ncf), _parspec(cb_p, cwf, ncf, grad=grad, ncol=ncf)]
    act, = _rowmap("ffn_act_fwd", fwd(_ffn_act_fn), ncf, T // tbf, ffn_rows(False), ffn_pars(False), [full(D_FF_PAD, _CDT, tbf, cwf)])
    r3 = _mm("r3", act, W["down_p"], "nn", add=x2, add_scale=ALPHA)
    dr3, dr3b, lacc, dg3, db3 = _rowmap("ln3_loss", _loss_fn, 1, T // tb, [_rowspec(r3, tb, D_MODEL, 0), _rowspec(tgt, tb, D_MODEL, 0)],
                                        [_parspec(g3), _parspec(b3)], [full(D_MODEL, F32, tb, D_MODEL), full(D_MODEL, _CDT, tb, D_MODEL)],
                                  accs=[(SUBLANES, LANES), (1, D_MODEL), (1, D_MODEL)])
    loss = lacc[0, 0]

    G = {}
    G["down_p"] = _mm("dw_down", act, dr3b, "tn", out_dtype=_GDT)
    dact = _mm("d_act", dr3b, W["down_p"], "nt")
    dhg, dhu, dcwg, dcwu, dcbg, dcbu = _rowmap_bwd("ffn_act_bwd", _ffn_act_fn, ncf, T // tbf, ffn_rows(True), ffn_pars(True),
                                                   [_rowspec(dact, tbf, cwf, 0)])
    w_gate, w_upp = W["up_p"][:, :D_FF_PAD], W["up_p"][:, D_FF_PAD:]
    dx2 = _mm("dx2_gate", dhg, w_gate, "nt", add=dr3, add_scale=ALPHA)
    dx2 = _mm("dx2_up", dhu, w_upp, "nt", add=dx2)
    G["up_p"] = jnp.concatenate([_mm("dw_gate", x2b, dhg, "tn", out_dtype=_GDT), _mm("dw_up", x2b, dhu, "tn", out_dtype=_GDT)], axis=1)
    G["ffn_conv_w"] = jnp.concatenate([dcwg[:, :D_FF], dcwu[:, :D_FF]], axis=1)
    G["ffn_conv_b"] = jnp.concatenate([dcbg[0, :D_FF], dcbu[0, :D_FF]])
    G["ln3_g"], G["ln3_b"] = dg3[0], db3[0]

    dr2, dr2b, dg2, db2 = ln_bwd("ln2_bwd", r2, g2, b2, dx2)
    G["ln2_g"], G["ln2_b"] = dg2[0], db2[0]
    G["mem_o"] = _mm("dw_mem_o", om, dr2b, "tn", out_dtype=_GDT)
    dom = _mm("d_om", dr2b, W["mem_o"], "nt", out_dtype=_CDT)
    dqm, dkm, dvm = _rowmap_bwd("memattn_bwd", _memattn_fn, MEM_HEADS, T // tbl, ma_rows(True), ma_pars(True),
                                [_rowspec(dom, tbl, MEM_HEAD_DIM, 0)])
    G["mem_kv"] = _mm("dw_mem_kv", mem.astype(_CDT), jnp.concatenate([dkm, dvm], axis=1).astype(_CDT), "tn", out_dtype=_GDT)
    G["mem_q"] = _mm("dw_mem_q", x1b, dqm, "tn", out_dtype=_GDT)
    dx1 = _mm("dx1", dqm, W["mem_q"], "nt", add=dr2, add_scale=ALPHA)

    dr1, dr1b, dg1, db1 = ln_bwd("ln1_bwd", r1, g1, b1, dx1)
    G["ln1_g"], G["ln1_b"] = dg1[0], db1[0]
    G["mix_o"] = _mm("dw_mix_o", mixed, dr1b, "tn", out_dtype=_GDT)
    dmixed = _mm("d_mixed", dr1b, W["mix_o"], "nt")
    dgs, dgg, dys, dyg = _rowmap_bwd("mix_bwd", _mix_fn, D_MODEL // cwm, T // tb, mix_rows(True), [], [_rowspec(dmixed, tb, cwm, 0)])
    G["br_swa"] = _mm("dw_br_swa", o_swa, dys, "tn", out_dtype=_GDT)
    G["br_gdn"] = _mm("dw_br_gdn", o_gdn, dyg, "tn", out_dtype=_GDT)
    do_swa = _mm("d_o_swa", dys, W["br_swa"], "nt", out_dtype=_CDT)
    do_gdn = _mm("d_o_gdn", dyg, W["br_gdn"], "nt")

    rows, pars, nblk = _swa_specs(qs, ks, vs, bp, bc, sk, True, (_CDT,))
    m3 = lambda j, n: (j, n, 0)
    dqs, dks, dvs, dbp, dbc, dsk = _rowmap_bwd("swa_bwd", _swa_fn, SWA_KV_HEADS, nblk, rows, pars,
                                               [Row(_stack_heads(do_swa, nb), (None, SWA_GRP * SWA_BLOCK, SWA_HEAD_DIM), m3)])
    d_swa = jnp.concatenate([_unstack_heads(dqs, nb), dks.transpose(1, 0, 2).reshape(T, SWA_KV),
                             dvs.transpose(1, 0, 2).reshape(T, SWA_KV)], axis=1)
    dbias = jnp.concatenate([dbp.reshape(SWA_HEADS, half), dbc.reshape(SWA_HEADS, half)], axis=1)
    G["rel_bias"] = _mm("d_rel_bias", dbias, onehot_t.T, "nn", hi=True).T
    G["swa_sinks"] = _mm("d_sinks", dsk.reshape(SWA_HEADS, SWA_BLOCK), jnp.ones((SWA_BLOCK, LANES), F32), "nn", hi=True)[:, 0]

    dqkv_n, dbx, dgx, dz, dnw, got = _gdn_chunks_bwd(qkv_n, bx, gx, proj, nw, states, do_gdn, rider=src.rider_g(G))
    src.g_done(got)
    G["gdn_norm_w"] = dnw[0]
    dgba, dalog, ddtb = _rowmap_bwd("gdn_gate_bwd", _gdn_gate_fn, 1, T // tbl, gate_rows(True), gate_pars(True),
                                    [_rowspec(dbx, tbl, GDN_W, 0), _rowspec(dgx, tbl, GDN_W, 0)])
    G["gdn_a_log"], G["gdn_dt_bias"] = dalog[0, GDN_HEADS:2 * GDN_HEADS], ddtb[0, GDN_HEADS:2 * GDN_HEADS]
    dgqkv, dcw_gdn = _rowmap_bwd("gdn_pre_bwd", _gdn_pre_fn, ncq, T // tbp, pre_rows(True), pre_pars(True),
                                 [_rowspec(dqkv_n, tbp, LANES, 0)])
    G["gdn_conv_w"] = dcw_gdn

    dproj = jnp.concatenate([dgs, dgg, dgqkv, dz, d_swa, dgba, jnp.zeros((T, P_END - P_USED), _CDT)], axis=1)
    dx = _mm("dx", dproj, W["in_p"], "nt", add=dr1, add_scale=ALPHA)
    G["in_p"] = _mm("dw_in", xb, dproj, "tn", out_dtype=_GDT)
    return loss, dx, G


W_NAMES = ["w_in", "rel_bias", "swa_sinks", "gdn_conv_w", "gdn_a_log", "gdn_dt_bias", "gdn_norm_w", "w_br_swa", "w_br_gdn",
           "w_mix_o", "ln1_g", "ln1_b", "w_mem_q", "w_mem_kv", "w_mem_o", "ln2_g", "ln2_b", "w_up", "ffn_conv_w", "ffn_conv_b",
           "w_down", "ln3_g", "ln3_b"]
BIG = ["w_in", "w_br_swa", "w_br_gdn", "w_mix_o", "w_mem_q", "w_mem_kv", "w_mem_o", "w_up", "w_down"]
SMALL = [n for n in W_NAMES if n not in BIG]
COL_SHARDED = ["w_in", "w_br_swa", "w_br_gdn", "w_mem_o", "w_up"]


def _pack(arrs):
    flat = []
    for a in arrs:
        f = a.reshape(-1).astype(F32)
        flat.append(jnp.pad(f, (0, (-f.shape[0]) % LANES)))
    f = jnp.concatenate(flat)
    f = jnp.pad(f, (0, (-f.shape[0]) % (16 * LANES)))
    return f.reshape(-1, LANES)


def _unpack(p, shapes):
    f, out, off = p.reshape(-1), [], 0
    for s in shapes:
        n = int(np.prod(s)) if len(s) else 1
        out.append(f[off:off + n].reshape(s))
        off += n + (-n) % LANES
    return out


def _merge_shards(d):
    cat = lambda names: jnp.concatenate([d[n] for n in names], axis=-2)
    return [d.get("w_in"), d["w_up"], cat(["w_br_swa", "w_br_gdn", "w_mem_q", "w_mem_o"]), cat(["w_mix_o", "w_down"]), d["w_mem_kv"]]


def _split_shards(ts):
    a, b, c, dd, e = ts
    return {"w_in": a, "w_up": b, "w_br_swa": c[..., 0:1024, :], "w_br_gdn": c[..., 1024:2048, :], "w_mem_q": c[..., 2048:2560, :],
            "w_mem_o": c[..., 2560:3072, :], "w_mix_o": dd[..., 0:512, :], "w_down": dd[..., 512:, :], "w_mem_kv": e}


def _to_full(name, t):
    if name in COL_SHARDED:
        return t.transpose(1, 0, 2).reshape(t.shape[1], 4 * t.shape[2])
    return t.reshape(4 * t.shape[1], t.shape[2])


def _to_chips(name, t):
    if name in COL_SHARDED:
        return t.reshape(t.shape[0], 4, t.shape[1] // 4).transpose(1, 0, 2)
    return t.reshape(4, t.shape[0] // 4, t.shape[1])


_IN_OFF = np.cumsum((0,) + IN_WIDTHS)


def _in_to_padded(w):
    o = _IN_OFF
    cut = lambda i, k: w[:, o[i]:o[k]]
    return jnp.concatenate([cut(9, 10), cut(10, 11), cut(3, 6), cut(6, 7), cut(0, 1), cut(1, 2), cut(2, 3), cut(7, 9),
                            jnp.zeros((w.shape[0], P_END - P_BA - 2 * GDN_HEADS), w.dtype)], axis=1)


def _in_from_padded(p):
    return jnp.concatenate([p[:, P_SQ:P_SQ + SWA_Q], p[:, P_SK:P_SK + SWA_KV], p[:, P_SV:P_SV + SWA_KV], p[:, P_GQKV:P_GQKV + 3 * GDN_W],
                            p[:, P_GZ:P_GZ + GDN_W], p[:, P_BA:P_BA + 2 * GDN_HEADS], p[:, P_GS:P_GS + D_MODEL], p[:, P_GG:P_GG + D_MODEL]],
                           axis=1)


def _ff_pad(t, axis):
    g, u = jnp.split(t, 2, axis=axis)
    pad = [(0, 0)] * t.ndim
    pad[axis] = (0, D_FF_PAD - D_FF)
    return jnp.concatenate([jnp.pad(g, pad), jnp.pad(u, pad)], axis=axis)


def _ff_unpad(t, axis):
    g, u = jnp.split(t, 2, axis=axis)
    return jnp.concatenate([lax.slice_in_dim(g, 0, D_FF, axis=axis), lax.slice_in_dim(u, 0, D_FF, axis=axis)], axis=axis)


def _assemble_weights(full, small):
    W = dict(small)
    W["in_p"] = _in_to_padded(full["w_in"])
    W["up_p"] = _ff_pad(full["w_up"], 1)
    W["down_p"] = jnp.pad(full["w_down"], ((0, D_FF_PAD - D_FF), (0, 0)))
    W["br_swa"], W["br_gdn"], W["mix_o"] = full["w_br_swa"], full["w_br_gdn"], full["w_mix_o"]
    W["mem_q"], W["mem_kv"], W["mem_o"] = full["w_mem_q"], full["w_mem_kv"], full["w_mem_o"]
    W["ffn_conv_w_p"] = _ff_pad(small["ffn_conv_w"], 1)
    W["ffn_conv_b_p"] = _ff_pad(small["ffn_conv_b"].reshape(1, -1), 1)
    return W


def _full_grads(G):
    out = {"w_in": _in_from_padded(G["in_p"])} if "in_p" in G else {}
    out.update({"w_up": _ff_unpad(G["up_p"], 1), "w_down": G["down_p"][:D_FF], "w_br_swa": G["br_swa"], "w_br_gdn": G["br_gdn"],
                "w_mix_o": G["mix_o"], "w_mem_q": G["mem_q"], "w_mem_kv": G["mem_kv"], "w_mem_o": G["mem_o"]})
    return out


def kernel(x, mem, w_in, rel_bias, swa_sinks, gdn_conv_w, gdn_a_log, gdn_dt_bias, gdn_norm_w, w_br_swa, w_br_gdn, w_mix_o, ln1_g, ln1_b, w_mem_q, w_mem_kv, w_mem_o, ln2_g, ln2_b, w_up, ffn_conv_w, ffn_conv_b, w_down, ln3_g, ln3_b, loss_target, m_w_in, m_rel_bias, m_swa_sinks, m_gdn_conv_w, m_gdn_a_log, m_gdn_dt_bias, m_gdn_norm_w, m_w_br_swa, m_w_br_gdn, m_w_mix_o, m_ln1_g, m_ln1_b, m_w_mem_q, m_w_mem_kv, m_w_mem_o, m_ln2_g, m_ln2_b, m_w_up, m_ffn_conv_w, m_ffn_conv_b, m_w_down, m_ln3_g, m_ln3_b, v_w_in, v_rel_bias, v_swa_sinks, v_gdn_conv_w, v_gdn_a_log, v_gdn_dt_bias, v_gdn_norm_w, v_w_br_swa, v_w_br_gdn, v_w_mix_o, v_ln1_g, v_ln1_b, v_w_mem_q, v_w_mem_kv, v_w_mem_o, v_ln2_g, v_ln2_b, v_w_up, v_ffn_conv_w, v_ffn_conv_b, v_w_down, v_ln3_g, v_ln3_b):
    a = dict(locals())
    w = {n: a[n] for n in W_NAMES}
    m = {n: a["m_" + n] for n in W_NAMES}
    v = {n: a["v_" + n] for n in W_NAMES}
    chip = 2 * lax.axis_index("x") + lax.axis_index("y")
    core = lax.axis_index("c")
    sq = lambda t: t.reshape(t.shape[1:]) if (t.ndim > 1 and t.shape[0] == 1 and t is not rel_bias) else t

    sh_a, sh_b, sh_c, sh_d, sh_e = _merge_shards({n: sq(w[n]).astype(_CDT) for n in BIG})
    fcw_sh, gcw_sh = sq(ffn_conv_w).shape, sq(gdn_conv_w).shape
    slot = lax.broadcasted_iota(jnp.int32, (4, 1, 1), 0)

    def with_own(got, mine):
        return [jnp.where(slot == chip, t[None], g) for g, t in zip(got, mine)]

    def reduce_start(tag, gch):
        pair = []
        for t, (mine, got) in enumerate(zip(gch, _pair_swap(tag, gch))):
            rh = mine.shape[1] // 2
            mine_h = lax.dynamic_slice_in_dim(mine, core * rh, rh, axis=1)
            pair.append(_addn(f"pair_sum_{tag}{t}", [mine_h.reshape(4 * rh, -1), got.reshape(4 * rh, -1)], _GDT).reshape(4, rh, -1))
        return pair

    def reduce_end(tag, pair, others):
        halves = []
        for t, (p, o) in enumerate(zip(pair, others)):
            own = lax.dynamic_index_in_dim(p, chip, 0, keepdims=False)
            halves.append(_addn(f"chip_sum_{tag}{t}", [own, (o, 0), (o, 1), (o, 2)]))
        return halves

    class MeshWeights:
        def w1(self):
            mine = [sh_a, _pack([sq(ffn_conv_w), sq(gdn_conv_w)])]
            got_a, got_f = with_own(_run_rider("gather_first", _gather_rider(mine)), mine)
            conv = [_unpack(got_f[k], [fcw_sh, gcw_sh]) for k in range(4)]
            W = {n: sq(w[n]) for n in SMALL}
            W["ffn_conv_w"] = jnp.concatenate([cv[0] for cv in conv], axis=1)
            W["gdn_conv_w"] = jnp.concatenate([cv[1] for cv in conv], axis=1)
            W["ffn_conv_w_p"] = _ff_pad(W["ffn_conv_w"], 1)
            W["ffn_conv_b_p"] = _ff_pad(W["ffn_conv_b"].reshape(1, -1), 1)
            W["in_p"] = _in_to_padded(_to_full("w_in", got_a))
            return W

        def rider_a(self):
            return _gather_rider([sh_c, sh_d, sh_e])

        def w2(self, got):
            c, d, e = with_own(got, [sh_c, sh_d, sh_e])
            f = {n: _to_full(n, t) for n, t in _split_shards([None, None, c, d, e]).items() if t is not None}
            return {"br_swa": f["w_br_swa"], "br_gdn": f["w_br_gdn"], "mix_o": f["w_mix_o"], "mem_q": f["w_mem_q"], "mem_kv": f["w_mem_kv"],
                    "mem_o": f["w_mem_o"], "down_p": jnp.pad(f["w_down"], ((0, D_FF_PAD - D_FF), (0, 0)))}

        def rider_b(self):
            return _gather_rider([sh_b])

        def w3(self, got):
            b, = with_own(got, [sh_b])
            return {"up_p": _ff_pad(_to_full("w_up", b), 1)}

        def rider_g(self, G):
            gf = _full_grads(G)
            self.pair = reduce_start("rest", _merge_shards({n: _to_chips(n, gf[n]) for n in BIG if n != "w_in"})[1:])
            return _scatter_rider(self.pair)

        def g_done(self, got):
            self.halves = reduce_end("rest", self.pair, got)

    src = MeshWeights()
    loss, dx, G = _fwd_bwd(x[0], mem[0], loss_target[0], src)

    small_names = SMALL
    small_shapes = [()] + [tuple(G[n].shape) for n in small_names]
    packed = _pack([loss] + [G[n] for n in small_names])
    allp = _allgather8(packed)
    tot = _addn("small_sum", [(allp, k) for k in range(8)])
    parts = _unpack(tot, small_shapes)
    loss_tot, gsmall = parts[0], dict(zip(small_names, parts[1:]))
    gsmall["ffn_conv_w"] = lax.dynamic_slice_in_dim(gsmall["ffn_conv_w"], chip * fcw_sh[1], fcw_sh[1], axis=1)
    gsmall["gdn_conv_w"] = lax.dynamic_slice_in_dim(gsmall["gdn_conv_w"], chip * gcw_sh[1], gcw_sh[1], axis=1)

    pair_in = reduce_start("in", [_to_chips("w_in", _full_grads(G)["w_in"])])
    halves = reduce_end("in", pair_in, _run_rider("chip_scatter_in", _scatter_rider(pair_in))) + src.halves
    both = []
    for h, o in zip(halves, _pair_exchange(halves)):
        both.append(jnp.concatenate([jnp.where(core == 0, h, o), jnp.where(core == 0, o, h)], axis=0))
    gbig = _split_shards(both)

    outs = {}
    for n in BIG:
        d_, m_, v_ = _adamw("adamw_" + n, sq(w[n]), gbig[n], sq(m[n]), sq(v[n]))
        outs[n] = (gbig[n], d_, m_, v_)
    sm_shapes = [sq(w[n]).shape if n != "rel_bias" else w[n].shape for n in SMALL]
    sw, sg, sm_, sv = (_pack([sq(t[n]) if n != "rel_bias" else t[n] for n in SMALL]) for t in (w, gsmall, m, v))
    d_, m_, v_ = _adamw("adamw_small", sw, sg, sm_, sv)
    for n, g_, dd, mm_, vv in zip(SMALL, _unpack(sg, sm_shapes), _unpack(d_, sm_shapes), _unpack(m_, sm_shapes), _unpack(v_, sm_shapes)):
        outs[n] = (g_, dd, mm_, vv)

    res = [loss_tot.reshape(()), dx.reshape(x.shape)]
    for k in range(4):
        res += [outs[n][k].reshape(w[n].shape) for n in W_NAMES]
    return tuple(res)
```

```python
import functools
import math

import jax
import jax.numpy as jnp
import numpy as np
from jax import lax
from jax.experimental import pallas as pl
from jax.experimental.pallas import tpu as pltpu

F32 = jnp.float32
BF16 = jnp.bfloat16
_CDT = BF16
_GDT = BF16

D_MODEL = 2048
SWA_HEADS, SWA_KV_HEADS, SWA_HEAD_DIM, SWA_BLOCK = 16, 2, 64, 128
SWA_GRP = SWA_HEADS // SWA_KV_HEADS
REL_BUCKETS, REL_MAX_DIST = 32, 128
GDN_HEADS, GDN_HEAD_DIM, GDN_CONV, GDN_CHUNK = 8, 128, 4, 64
MEM_HEADS, MEM_HEAD_DIM = 4, 128
D_FF, D_FF_PAD, FFN_CONV = 5504, 5632, 3
SWA_Q, SWA_KV, GDN_W, MEM_W = 1024, 128, 1024, 512
IN_WIDTHS = (SWA_Q, SWA_KV, SWA_KV, GDN_W, GDN_W, GDN_W, GDN_W, GDN_HEADS, GDN_HEADS, D_MODEL, D_MODEL)
NORM_EPS = 1e-5
ALPHA = 2.0 ** 0.25
NEG_INF = -1e30
ADAM_LR, ADAM_B1, ADAM_B2, ADAM_EPS, ADAM_WD, ADAM_STEP = 0.001, 0.9, 0.999, 1e-08, 0.01, 10
LANES, SUBLANES = 128, 8
VMEM_LIMIT = 56 * 1024 * 1024

P_GS, P_GG, P_GQKV, P_GZ, P_SQ, P_SK, P_SV, P_BA, P_USED, P_END = 0, 2048, 4096, 7168, 8192, 9216, 9344, 9472, 9600, 9728


def _tile(dim, pref, align=LANES):
    if dim <= pref:
        return dim
    t = (pref // align) * align
    while t >= align:
        if dim % t == 0:
            return t
        t -= align
    return dim


_DIMS = {"nn": (((1,), (0,)), ((), ())), "nt": (((1,), (1,)), ((), ())), "tn": (((0,), (0,)), ((), ()))}
_BDIMS = {"nn": (((2,), (1,)), ((0,), (0,))), "nt": (((2,), (2,)), ((0,), (0,))), "tn": (((1,), (1,)), ((0,), (0,)))}


def _raw_dot(a, b, form, hi):
    dims = (_BDIMS if a.ndim == 3 else _DIMS)[form]
    if hi == "x3":
        a, b = a.astype(F32), b.astype(F32)
        ah, bh = a.astype(BF16), b.astype(BF16)
        al, bl = (a - ah.astype(F32)).astype(BF16), (b - bh.astype(F32)).astype(BF16)
        d = lambda p, q: lax.dot_general(p, q, dims, preferred_element_type=F32)
        if form == "tn":
            return d(ah, bh) + (d(ah, bl) + d(al, bh))
        m = a.shape[-2]
        both = d(jnp.concatenate([ah, al], axis=-2), bh)
        return both[..., :m, :] + (d(ah, bl) + both[..., m:, :])
    if hi:
        return lax.dot_general(a.astype(F32), b.astype(F32), dims, precision=lax.Precision.HIGHEST, preferred_element_type=F32)
    return lax.dot_general(a.astype(_CDT), b.astype(_CDT), dims, preferred_element_type=F32)


@functools.partial(jax.custom_vjp, nondiff_argnums=(2, 3))
def _dot(a, b, form, hi=False):
    return _raw_dot(a, b, form, hi)


def _dot_fwd(a, b, form, hi):
    return _raw_dot(a, b, form, hi), (a, b)


def _dot_bwd(form, hi, res, g):
    a, b = res
    if form == "nn":
        da, db = _raw_dot(g, b, "nt", hi), _raw_dot(a, g, "tn", hi)
    elif form == "nt":
        da, db = _raw_dot(g, b, "nn", hi), _raw_dot(g, a, "tn", hi)
    else:
        da, db = _raw_dot(b, g, "nt", hi), _raw_dot(a, g, "nn", hi)
    return da.astype(a.dtype), db.astype(b.dtype)


_dot.defvjp(_dot_fwd, _dot_bwd)


@functools.partial(jax.custom_vjp, nondiff_argnums=(2,))
def _shift_halo(prev, cur, d):
    assert prev.shape[0] == SUBLANES
    return pltpu.roll(jnp.concatenate([prev, cur], axis=0), d, 0)[SUBLANES:]


def _shift_halo_fwd(prev, cur, d):
    return _shift_halo(prev, cur, d), None


def _shift_halo_bwd(d, _, g):
    nh = SUBLANES
    ext = jnp.concatenate([jnp.zeros((nh, g.shape[1]), g.dtype), g], axis=0)
    r = pltpu.roll(ext, ext.shape[0] - d, 0)
    return r[:nh], r[nh:]


_shift_halo.defvjp(_shift_halo_fwd, _shift_halo_bwd)


def _sigmoid(x):
    return 1.0 / (1.0 + jnp.exp(-x))


def _silu(x):
    return x * _sigmoid(x)


def _softplus(x):
    return jnp.maximum(x, 0.0) + jnp.log(1.0 + jnp.exp(-jnp.abs(x)))


def _iota(shape, axis):
    return lax.broadcasted_iota(jnp.int32, shape, axis)


def _cparams(sem):
    return pltpu.CompilerParams(dimension_semantics=sem, vmem_limit_bytes=VMEM_LIMIT)


class _ride:
    def __init__(self, rider, n_in, n_out, n_scr):
        self.rider = rider
        self.ins = rider.ins if rider else []
        n_rin = len(self.ins)
        self.out_shapes = rider.out_shapes if rider else []
        n_rout = len(self.out_shapes)
        self.in_specs, self.out_specs = [_HBM] * n_rin, [_HBM] * n_rout
        self.scratch = rider.sems() if rider else []
        self.o0 = n_in + n_rin
        self.s0 = self.o0 + n_out + n_rout
        self._rin = slice(n_in, n_in + n_rin)
        self._rout = slice(self.o0 + n_out, self.s0)
        self._sem = self.s0 + n_scr

    def _args(self, refs):
        return refs[self._rin], refs[self._rout], refs[self._sem], refs[self._sem + 1]

    def at_start(self, refs, cond):
        if self.rider:
            pl.when(cond)(lambda: self.rider.start(*self._args(refs)))

    def at_end(self, refs, cond):
        if self.rider:
            pl.when(cond)(lambda: self.rider.finish(*self._args(refs)))


def _mm(name, a, b, form, out_dtype=F32, add=None, add_scale=1.0, hi=False, tm=1024, tn=1024, tk=2816, rider=None):
    if form == "nn":
        (M, K), (K2, N) = a.shape, b.shape
    elif form == "nt":
        (M, K), (N, K2) = a.shape, b.shape
    else:
        (K, M), (K2, N) = a.shape, b.shape
    assert K == K2, (name, a.shape, b.shape, form)
    tm, tn, tk = _tile(M, tm), _tile(N, tn), _tile(K, tk)
    nk = K // tk
    a_spec = pl.BlockSpec((tk, tm), lambda i, j, k: (k, i)) if form == "tn" else pl.BlockSpec((tm, tk), lambda i, j, k: (i, k))
    b_spec = pl.BlockSpec((tn, tk), lambda i, j, k: (j, k)) if form == "nt" else pl.BlockSpec((tk, tn), lambda i, j, k: (k, j))
    o_spec = pl.BlockSpec((tm, tn), lambda i, j, k: (i, j))
    has_add = add is not None

    def finish(r, c_ref, o_ref):
        if has_add:
            r = r + add_scale * c_ref[...].astype(F32)
        o_ref[...] = r.astype(out_dtype)

    n_own = 3 if has_add else 2
    grid = (M // tm, N // tn, nk)
    rd = _ride(rider, n_own, 1, 1 if nk > 1 else 0)

    def body(*refs):
        a_ref, b_ref = refs[:2]
        c_ref = refs[2] if has_add else None
        o_ref = refs[rd.o0]
        pid = [pl.program_id(d) for d in range(3)]
        rd.at_start(refs, (pid[0] == 0) & (pid[1] == 0) & (pid[2] == 0))
        if nk == 1:
            finish(_raw_dot(a_ref[...], b_ref[...], form, hi), c_ref, o_ref)
        else:
            acc = refs[rd.s0]

            @pl.when(pid[2] == 0)
            def _():
                acc[...] = jnp.zeros_like(acc)

            acc[...] += _raw_dot(a_ref[...], b_ref[...], form, hi)

            @pl.when(pid[2] == nk - 1)
            def _():
                finish(acc[...], c_ref, o_ref)
        rd.at_end(refs, (pid[0] == grid[0] - 1) & (pid[1] == grid[1] - 1) & (pid[2] == nk - 1))

    ins = [a, b] + ([add] if has_add else [])
    specs = [a_spec, b_spec] + ([o_spec] if has_add else [])
    res = pl.pallas_call(
        body, name=name, grid=grid, in_specs=specs + rd.in_specs, out_specs=[o_spec] + rd.out_specs,
        out_shape=[jax.ShapeDtypeStruct((M, N), out_dtype)] + rd.out_shapes,
        scratch_shapes=([pltpu.VMEM((tm, tn), F32)] if nk > 1 else []) + rd.scratch,
        compiler_params=_cparams(("arbitrary",) * 3 if rider else ("parallel", "parallel", "arbitrary")),
    )(*ins, *rd.ins)
    return (res[0], res[1:]) if rider else res[0]


class Row:
    def __init__(self, arr, blk, imap, hblk=None, hmap=None, gshape=None, gmap=None, gdt=(F32,)):
        self.arr, self.blk, self.imap, self.hblk, self.hmap, self.gshape, self.gmap = arr, blk, imap, hblk, hmap, gshape, gmap
        self.gdt = gdt


class Par:
    def __init__(self, arr, blk=None, imap=None, gshape=None, gmap=None):
        self.arr = arr
        self.blk = tuple(arr.shape) if blk is None else blk
        nd = len(self.blk)
        self.imap = (lambda j: (0,) * nd) if imap is None else imap
        self.gshape, self.gmap = gshape, gmap


class Out:
    def __init__(self, shape, dtype, blk, imap):
        self.shape, self.dtype, self.blk, self.imap = shape, dtype, blk, imap


def _rows_of(blk):
    return [d for d in blk if d is not None][0]


def _rowmap(name, fn, ncol, nblk, rows, pars, outs, accs=()):
    in_specs, ins = [], []
    for r in rows:
        ins.append(r.arr)
        in_specs.append(pl.BlockSpec(r.blk, r.imap))
        if r.hblk is not None:
            ins.append(r.arr)
            in_specs.append(pl.BlockSpec(r.hblk, r.hmap))
    for p in pars:
        ins.append(p.arr)
        in_specs.append(pl.BlockSpec(p.blk, (lambda im: (lambda j, n: im(j)))(p.imap)))
    out_specs = [pl.BlockSpec(o.blk, o.imap) for o in outs]
    out_shape = [jax.ShapeDtypeStruct(o.shape, o.dtype) for o in outs]
    for a in accs:
        out_specs.append(pl.BlockSpec(a, (lambda nd: (lambda j, n: (0,) * nd))(len(a))))
        out_shape.append(jax.ShapeDtypeStruct(a, F32))
    n_in = len(ins)

    def body(*refs):
        j, n = pl.program_id(0), pl.program_id(1)
        it = iter(refs[:n_in])
        rvals = []
        for r in rows:
            cur = next(it)[...]
            rvals.append((next(it)[...], cur) if r.hblk is not None else cur)
        pvals = [next(it)[...] for _ in pars]
        o_refs = refs[n_in:n_in + len(outs)]
        a_refs = refs[n_in + len(outs):]
        ovals, avals = fn(j, n == 0, rvals, pvals)
        for ref, v in zip(o_refs, ovals):
            ref[...] = v.astype(ref.dtype)
        if accs:
            @pl.when((j == 0) & (n == 0))
            def _():
                for ref in a_refs:
                    ref[...] = jnp.zeros_like(ref)
            for ref, v in zip(a_refs, avals):
                ref[...] += v

    res = pl.pallas_call(
        body, name=name, grid=(ncol, nblk), in_specs=in_specs, out_specs=out_specs, out_shape=out_shape,
        compiler_params=_cparams(("arbitrary", "arbitrary")),
    )(*ins)
    return res


def _rowmap_bwd(name, fn, ncol, nblk, rows, pars, cts):
    rev = lambda im: (lambda j, s: im(j, nblk - 1 - s))
    in_specs, ins = [], []
    for r in rows:
        ins.append(r.arr)
        in_specs.append(pl.BlockSpec(r.blk, rev(r.imap)))
        if r.hblk is not None:
            ins.append(r.arr)
            in_specs.append(pl.BlockSpec(r.hblk, rev(r.hmap)))
    for p in pars:
        ins.append(p.arr)
        in_specs.append(pl.BlockSpec(p.blk, (lambda im: (lambda j, s: im(j)))(p.imap)))
    for c in cts:
        ins.append(c.arr)
        in_specs.append(pl.BlockSpec(c.blk, rev(c.imap)))
    n_in = len(ins)
    drows = [i for i, r in enumerate(rows) if r.gshape is not None]
    dpars = [i for i, p in enumerate(pars) if p.gshape is not None]
    out_specs, out_shape, scratch = [], [], []
    for i in drows:
        r = rows[i]
        for dt in r.gdt:
            out_specs.append(pl.BlockSpec(r.blk, rev(r.gmap)))
            out_shape.append(jax.ShapeDtypeStruct(r.gshape, dt))
        if r.hblk is not None:
            scratch.append(pltpu.VMEM(tuple(d for d in r.hblk if d is not None), F32))
    n_drow_out = len(out_specs)
    for i in dpars:
        p = pars[i]
        out_specs.append(pl.BlockSpec(p.blk, (lambda im: (lambda j, s: im(j)))(p.gmap)))
        out_shape.append(jax.ShapeDtypeStruct(p.gshape, F32))

    def body(*refs):
        j, s = pl.program_id(0), pl.program_id(1)
        first = s == nblk - 1
        it = iter(refs[:n_in])
        rvals = []
        for r in rows:
            cur = next(it)[...]
            rvals.append((next(it)[...], cur) if r.hblk is not None else cur)
        pvals = [next(it)[...] for _ in pars]
        cvals = [next(it)[...].astype(F32) for _ in cts]
        g_refs = iter(refs[n_in:n_in + n_drow_out])
        p_refs = refs[n_in + n_drow_out:n_in + n_drow_out + len(dpars)]
        carries = iter(refs[n_in + n_drow_out + len(dpars):])

        def f(dr, dp):
            rv, pv = list(rvals), list(pvals)
            for i, v in zip(drows, dr):
                rv[i] = v
            for i, v in zip(dpars, dp):
                pv[i] = v
            return fn(j, first, rv, pv)

        _, vjp = jax.vjp(f, [rvals[i] for i in drows], [pvals[i] for i in dpars])
        g_r, g_p = vjp(cvals)
        for i, g in zip(drows, g_r):
            r = rows[i]
            if r.hblk is None:
                for _ in r.gdt:
                    ref = next(g_refs)
                    ref[...] = g.astype(ref.dtype)
            else:
                g_prev, g_cur = g
                carry = next(carries)
                nr, nh = g_cur.shape[0], g_prev.shape[0]
                tail = g_cur[nr - nh:nr] + jnp.where(s > 0, carry[...], 0.0)
                for _ in r.gdt:
                    ref = next(g_refs)
                    if nr > nh:
                        ref[0:nr - nh, :] = g_cur[0:nr - nh].astype(ref.dtype)
                    ref[nr - nh:nr, :] = tail.astype(ref.dtype)
                carry[...] = g_prev
        for ref, g in zip(p_refs, g_p):
            @pl.when(s == 0)
            def _():
                ref[...] = jnp.zeros_like(ref)
            ref[...] += g

    return pl.pallas_call(
        body, name=name, grid=(ncol, nblk), in_specs=in_specs, out_specs=out_specs, out_shape=out_shape,
        scratch_shapes=scratch, compiler_params=_cparams(("arbitrary", "arbitrary")),
    )(*ins)


def _rowspec(arr, tb, cw, c0, cstep=1, halo=0, grad=False, ncol=1, gdt=(F32,)):
    T = arr.shape[0]
    imap = lambda j, n: (n, c0 + cstep * j)
    hblk = hmap = None
    if halo:
        q = tb // halo
        hblk, hmap = (halo, cw), (lambda j, n: (jnp.maximum(n * q - 1, 0), c0 + cstep * j))
    gshape = (T, cw * (ncol if cstep else 1)) if grad else None
    gmap = (lambda j, n: (n, cstep * j)) if grad else None
    return Row(arr, (tb, cw), imap, hblk, hmap, gshape, gmap, gdt)


def _parspec(arr, cw=None, c0=0, grad=False, ncol=1):
    if cw is None:
        return Par(arr, gshape=tuple(arr.shape) if grad else None,
                   gmap=(lambda nd: (lambda j: (0,) * nd))(arr.ndim) if grad else None)
    r = arr.shape[0]
    return Par(arr, (r, cw), lambda j: (0, c0 + j), (r, cw * ncol) if grad else None, (lambda j: (0, j)) if grad else None)


def _ln(r, g, b):
    mu = jnp.mean(r, axis=-1, keepdims=True)
    xc = r - mu
    var = jnp.mean(xc * xc, axis=-1, keepdims=True)
    return xc * lax.rsqrt(var + NORM_EPS) * g + b


def _ln_fn(j, first, rv, pv):
    return [_ln(rv[0], pv[0], pv[1])]


def _ln_fwd_fn(j, first, rv, pv):
    y = _ln(rv[0], pv[0], pv[1])
    return [y, y], []


def _loss_fn(j, first, rv, pv):
    r3, tgt = rv
    g, b = pv
    y, vjp = jax.vjp(_ln, r3, g, b)
    diff = y - tgt
    part = 0.5 * jnp.sum(diff * diff) / D_MODEL
    dr, dg, db = vjp(diff * (1.0 / D_MODEL))
    return [dr, dr], [jnp.full((SUBLANES, LANES), part, F32), dg, db]


def _mix_fn(j, first, rv, pv):
    gs, gg, ys, yg = rv
    return [_sigmoid(gs) * ys + _sigmoid(gg) * yg]


def _row_pick(x, i):
    ax = x.ndim - 2
    return jnp.sum(jnp.where(_iota(x.shape, ax) == i, x, 0.0), axis=ax, keepdims=True)


def _causal_conv(prev, cur, w, first):
    width = w.shape[0]
    prev = jnp.where(first, 0.0, prev)
    y = cur * _row_pick(w, width - 1)
    for d in range(1, width):
        y = y + _shift_halo(prev, cur, d) * _row_pick(w, width - 1 - d)
    return y


def _ffn_act_fn(j, first, rv, pv):
    (pg, cg), (pu, cu) = rv
    wg, wu, bg, bu = pv
    hg = _causal_conv(pg, cg, wg, first) + bg
    hu = _causal_conv(pu, cu, wu, first) + bu
    return [_silu(hg) * hu]


def _gdn_pre_fn(j, first, rv, pv):
    (prev, cur), = rv
    w, = pv
    t = _silu(_causal_conv(prev, cur, w, first))
    tn = t * lax.rsqrt(jnp.sum(t * t, axis=-1, keepdims=True) + 1e-6)
    return [jnp.where(j < 2 * GDN_HEADS, tn, t)]


def _gdn_gate_fn(j, first, rv, pv):
    gba, = rv
    alog, dtb, eb, eg = pv
    tb = gba.shape[0]
    beta = _sigmoid(gba)
    g = -jnp.exp(alog) * _softplus(gba + dtb)
    ri, ci = _iota((tb, tb), 0), _iota((tb, tb), 1)
    tril = jnp.where((ri // GDN_CHUNK == ci // GDN_CHUNK) & (ci <= ri), 1.0, 0.0)
    gc = _dot(tril, g, "nn", True)
    return [_dot(beta, eb, "nn", True), _dot(gc, eg, "nn", True)]


def _swa_fn(j, first, rv, pv):
    q, (kp, kc), (vp, vc) = rv
    bp, bc, sk = pv
    sp = _dot(q, kp, "nt") * (SWA_HEAD_DIM ** -0.5) + bp
    sc = _dot(q, kc, "nt") * (SWA_HEAD_DIM ** -0.5) + bc
    qi = _iota(sp.shape, 0) % SWA_BLOCK
    kj = _iota(sp.shape, 1)
    sp = jnp.where((kj > qi) & jnp.logical_not(first), sp, NEG_INF)
    sc = jnp.where(kj <= qi, sc, NEG_INF)
    m = jnp.maximum(jnp.maximum(jnp.max(sp, axis=-1, keepdims=True), jnp.max(sc, axis=-1, keepdims=True)), sk)
    m = lax.stop_gradient(m)
    ep, ec, es = jnp.exp(sp - m), jnp.exp(sc - m), jnp.exp(sk - m)
    inv = 1.0 / (jnp.sum(ep, axis=-1, keepdims=True) + jnp.sum(ec, axis=-1, keepdims=True) + es)
    vp = jnp.where(first, 0.0, vp)
    return [_dot(ep * inv, vp, "nn") + _dot(ec * inv, vc, "nn")]


def _memattn_fn(j, first, rv, pv):
    q, = rv
    k, v = pv
    s = _dot(q, k, "nt") * (MEM_HEAD_DIM ** -0.5)
    m = lax.stop_gradient(jnp.max(s, axis=-1, keepdims=True))
    e = jnp.exp(s - m)
    p = e / jnp.sum(e, axis=-1, keepdims=True)
    return [_dot(p, v, "nn")]


SOLVE_PREC = "x3"


def _gdn_heads(q, k, v, bx, gx, g64, z, nw, S):
    c = GDN_CHUNK
    q = q * (GDN_HEAD_DIM ** -0.5)
    kb, vb = k * bx, v * bx
    ri, ci = _iota((1, c, c), 1), _iota((1, c, c), 2)
    tril, strict, eye = ci <= ri, ci < ri, ci == ri
    grow = jnp.sum(jnp.where(eye, g64, 0.0), axis=1, keepdims=True)
    decay = jnp.where(tril, jnp.exp(jnp.where(tril, g64 - grow, 0.0)), 0.0)
    a = jnp.where(strict, _dot(kb, k, "nt") * decay, 0.0)
    tinv = jnp.where(eye, 1.0, 0.0) - a
    x = _dot(a, a, "nn", SOLVE_PREC)
    for i in range(5):
        tinv = tinv + _dot(tinv, x, "nn", SOLVE_PREC)
        if i < 4:
            x = _dot(x, x, "nn", SOLVE_PREC)
    eg = jnp.exp(gx)
    u = _dot(tinv, vb, "nn", SOLVE_PREC)
    w = _dot(tinv, kb * eg, "nn", SOLVE_PREC)
    ai = jnp.where(tril, _dot(q, k, "nt") * decay, 0.0)
    glast = _row_pick(gx, c - 1)
    v_new = u - _dot(w, S, "nn")
    o = _dot(q * eg, S, "nn") + _dot(ai, v_new, "nn")
    s_new = S * jnp.exp(glast) + _dot(k * jnp.exp(glast - gx), v_new, "tn")
    o = o * lax.rsqrt(jnp.mean(o * o, axis=-1, keepdims=True) + 1e-6) * nw
    return o * _silu(z), s_new


def _head_major(ref, off, width=GDN_HEAD_DIM):
    return jnp.stack([ref[:, off + h * GDN_HEAD_DIM:off + h * GDN_HEAD_DIM + width] for h in range(GDN_HEADS)])


def _gdn_chunks_fwd(qkv, bx, gx, proj, nw, rider=None):
    T = qkv.shape[0]
    nc, c, hd, nh = T // GDN_CHUNK, GDN_CHUNK, GDN_HEAD_DIM, GDN_HEADS
    rd = _ride(rider, 5, 2, 1)

    def body(*refs):
        qkv_ref, bx_ref, gx_ref, z_ref, nw_ref = refs[:5]
        y_ref, st_ref = refs[rd.o0:rd.o0 + 2]
        S = refs[rd.s0]
        rd.at_start(refs, pl.program_id(0) == 0)

        @pl.when(pl.program_id(0) == 0)
        def _():
            S[...] = jnp.zeros_like(S)

        s_old = S[...]
        st_ref[...] = s_old
        y, s_new = _gdn_heads(_head_major(qkv_ref, 0), _head_major(qkv_ref, GDN_W), _head_major(qkv_ref, 2 * GDN_W),
                              _head_major(bx_ref, 0), _head_major(gx_ref, 0), _head_major(gx_ref, 0, c), _head_major(z_ref, 0),
                              nw_ref[...], s_old)
        for h in range(nh):
            y_ref[:, h * hd:(h + 1) * hd] = y[h].astype(y_ref.dtype)
        S[...] = s_new
        rd.at_end(refs, pl.program_id(0) == nc - 1)

    row = lambda w, cb: pl.BlockSpec((c, w), lambda n: (n, cb))
    res = pl.pallas_call(
        body, name="gdn_chunks_fwd", grid=(nc,),
        in_specs=[row(3 * GDN_W, 0), row(GDN_W, 0), row(GDN_W, 0), row(GDN_W, P_GZ // GDN_W),
                  pl.BlockSpec((1, hd), lambda n: (0, 0))] + rd.in_specs,
        out_specs=[row(GDN_W, 0), pl.BlockSpec((None, nh, hd, hd), lambda n: (n, 0, 0, 0))] + rd.out_specs,
        out_shape=[jax.ShapeDtypeStruct((T, GDN_W), BF16), jax.ShapeDtypeStruct((nc, nh, hd, hd), F32)] + rd.out_shapes,
        scratch_shapes=[pltpu.VMEM((nh, hd, hd), F32)] + rd.scratch,
        compiler_params=_cparams(("arbitrary",)),
    )(qkv, bx, gx, proj, nw, *rd.ins)
    return res[0], res[1], res[2:]


def _gdn_chunks_bwd(qkv, bx, gx, proj, nw, states, dy, rider=None):
    T = qkv.shape[0]
    nc, c, hd, nh = T // GDN_CHUNK, GDN_CHUNK, GDN_HEAD_DIM, GDN_HEADS
    rd = _ride(rider, 7, 5, 1)

    def body(*refs):
        qkv_ref, bx_ref, gx_ref, z_ref, nw_ref, st_ref, dy_ref = refs[:7]
        dqkv_ref, dbx_ref, dgx_ref, dz_ref, dnw_ref = refs[rd.o0:rd.o0 + 5]
        dS = refs[rd.s0]
        rd.at_start(refs, pl.program_id(0) == 0)

        @pl.when(pl.program_id(0) == 0)
        def _():
            dS[...] = jnp.zeros_like(dS)
            dnw_ref[...] = jnp.zeros_like(dnw_ref)

        args = (_head_major(qkv_ref, 0), _head_major(qkv_ref, GDN_W), _head_major(qkv_ref, 2 * GDN_W), _head_major(bx_ref, 0),
                _head_major(gx_ref, 0), _head_major(gx_ref, 0, c), _head_major(z_ref, 0), nw_ref[...], st_ref[...])
        _, vjp = jax.vjp(_gdn_heads, *args)
        dq, dk, dv, dbx, dgx, dg64, dz, dnw, dsp = vjp((_head_major(dy_ref, 0), dS[...]))
        for h in range(nh):
            sl = slice(h * hd, (h + 1) * hd)
            dqkv_ref[:, sl] = dq[h].astype(dqkv_ref.dtype)
            dqkv_ref[:, GDN_W + h * hd:GDN_W + (h + 1) * hd] = dk[h].astype(dqkv_ref.dtype)
            dqkv_ref[:, 2 * GDN_W + h * hd:2 * GDN_W + (h + 1) * hd] = dv[h].astype(dqkv_ref.dtype)
            dbx_ref[:, sl] = dbx[h]
            dgx_ref[:, sl] = dgx[h]
            dgx_ref[:, h * hd:h * hd + c] += dg64[h]
            dz_ref[:, sl] = dz[h].astype(dz_ref.dtype)
        dnw_ref[...] += dnw
        dS[...] = dsp
        rd.at_end(refs, pl.program_id(0) == nc - 1)

    row = lambda w, cb: pl.BlockSpec((c, w), lambda s: (nc - 1 - s, cb))
    res = pl.pallas_call(
        body, name="gdn_chunks_bwd", grid=(nc,),
        in_specs=[row(3 * GDN_W, 0), row(GDN_W, 0), row(GDN_W, 0), row(GDN_W, P_GZ // GDN_W), pl.BlockSpec((1, hd), lambda s: (0, 0)),
                  pl.BlockSpec((None, nh, hd, hd), lambda s: (nc - 1 - s, 0, 0, 0)), row(GDN_W, 0)] + rd.in_specs,
        out_specs=[row(3 * GDN_W, 0), row(GDN_W, 0), row(GDN_W, 0), row(GDN_W, 0),
                   pl.BlockSpec((1, hd), lambda s: (0, 0))] + rd.out_specs,
        out_shape=[jax.ShapeDtypeStruct((T, 3 * GDN_W), F32), jax.ShapeDtypeStruct((T, GDN_W), F32),
                   jax.ShapeDtypeStruct((T, GDN_W), F32), jax.ShapeDtypeStruct((T, GDN_W), _CDT),
                   jax.ShapeDtypeStruct((1, hd), F32)] + rd.out_shapes,
        scratch_shapes=[pltpu.VMEM((nh, hd, hd), F32)] + rd.scratch,
        compiler_params=_cparams(("arbitrary",)),
    )(qkv, bx, gx, proj, nw, states, dy, *rd.ins)
    res = list(res)
    return res[:5] + [res[5:]]


def _adamw(name, w, g, m, v):
    R, C = w.shape
    tr = _tile(R, 128, SUBLANES)

    def body(w_ref, g_ref, m_ref, v_ref, d_ref, m2_ref, v2_ref):
        g_ = g_ref[...]
        m2 = ADAM_B1 * m_ref[...] + (1.0 - ADAM_B1) * g_
        v2 = ADAM_B2 * v_ref[...] + (1.0 - ADAM_B2) * (g_ * g_)
        m_hat = m2 / (1.0 - ADAM_B1 ** ADAM_STEP)
        v_hat = v2 / (1.0 - ADAM_B2 ** ADAM_STEP)
        d_ref[...] = -ADAM_LR * (m_hat / (jnp.sqrt(v_hat) + ADAM_EPS) + ADAM_WD * w_ref[...])
        m2_ref[...] = m2
        v2_ref[...] = v2

    spec = pl.BlockSpec((tr, C), lambda i: (i, 0))
    return pl.pallas_call(
        body, name=name, grid=(R // tr,), in_specs=[spec] * 4, out_specs=[spec] * 3,
        out_shape=[jax.ShapeDtypeStruct((R, C), F32)] * 3, compiler_params=_cparams(("parallel",)),
    )(w, g, m, v)


def _addn(name, parts, out_dtype=F32):
    parts = [p if isinstance(p, tuple) else (p, None) for p in parts]
    a0, k0 = parts[0]
    R, C = a0.shape[-2:]
    tr = _tile(R, 256, 2 * SUBLANES)
    specs = []
    for a, k in parts:
        if k is None:
            specs.append(pl.BlockSpec((tr, C), lambda i: (i, 0)))
        else:
            specs.append(pl.BlockSpec((None, tr, C), (lambda kk: (lambda i: (kk, i, 0)))(k)))

    def body(*refs):
        acc = refs[0][...].astype(F32)
        for r in refs[1:-1]:
            acc = acc + r[...].astype(F32)
        refs[-1][...] = acc.astype(out_dtype)

    return pl.pallas_call(
        body, name=name, grid=(R // tr,), in_specs=specs, out_specs=pl.BlockSpec((tr, C), lambda i: (i, 0)),
        out_shape=jax.ShapeDtypeStruct((R, C), out_dtype), compiler_params=_cparams(("parallel",)),
    )(*[a for a, _ in parts])


MESH = pl.DeviceIdType.MESH
_HBM = pl.BlockSpec(memory_space=pltpu.HBM)


def _place():
    x, y, c = lax.axis_index("x"), lax.axis_index("y"), lax.axis_index("c")
    return x, y, c, [(1 - x, y), (x, 1 - y), (1 - x, 1 - y)]


class _Rider:
    def __init__(self, ins, out_shapes, nsem, start, finish):
        self.ins, self.out_shapes, self.nsem, self.start, self.finish = list(ins), list(out_shapes), nsem, start, finish

    def sems(self):
        return [pltpu.SemaphoreType.DMA((self.nsem,)), pltpu.SemaphoreType.DMA((self.nsem,))]


def _run_rider(name, rd):
    n_in, n_out = len(rd.ins), len(rd.out_shapes)

    def body(*refs):
        ins, outs, (send, recv) = refs[:n_in], refs[n_in:n_in + n_out], refs[n_in + n_out:]
        rd.start(ins, outs, send, recv)
        rd.finish(ins, outs, send, recv)

    return pl.pallas_call(body, name=name, in_specs=[_HBM] * n_in, out_specs=[_HBM] * n_out, out_shape=rd.out_shapes,
                          scratch_shapes=rd.sems())(*rd.ins)


def _gather_rider(ts):
    nt = len(ts)

    def half(t, hc):
        rh = ts[t].shape[0] // 2
        return pl.ds(pl.multiple_of(hc * rh, 16), rh)

    def rcopy(send, recv, t, k, src, dst, to):
        return pltpu.make_async_remote_copy(src_ref=src, dst_ref=dst, send_sem=send.at[6 * t + k], recv_sem=recv.at[6 * t + k],
                                            device_id=to, device_id_type=MESH)

    def first_hop(ins, outs, send, recv, t, r, px, py, c, me):
        return rcopy(send, recv, t, r, ins[t].at[half(t, c)], outs[t].at[me, half(t, c)], (px, py, c))

    def start(ins, outs, send, recv):
        x, y, c, rel = _place()
        for t in range(nt):
            for r, (px, py) in enumerate(rel):
                first_hop(ins, outs, send, recv, t, r, px, py, c, 2 * x + y).start()

    def finish(ins, outs, send, recv):
        x, y, c, rel = _place()
        sib = (x, y, 1 - c)
        passed = []
        for t in range(nt):
            for r, (px, py) in enumerate(rel):
                got = outs[t].at[2 * px + py, half(t, c)]
                rcopy(send, recv, t, r, got, got, (px, py, c)).wait_recv()
                fw = rcopy(send, recv, t, 3 + r, got, got, sib)
                fw.start()
                passed.append(fw)
        for t in range(nt):
            for r, (px, py) in enumerate(rel):
                got = outs[t].at[2 * px + py, half(t, 1 - c)]
                rcopy(send, recv, t, 3 + r, got, got, sib).wait_recv()
        for t in range(nt):
            for r, (px, py) in enumerate(rel):
                first_hop(ins, outs, send, recv, t, r, px, py, c, 2 * x + y).wait_send()
        for fw in passed:
            fw.wait_send()

    return _Rider(ts, [jax.ShapeDtypeStruct((4,) + tuple(t.shape), t.dtype) for t in ts], 6 * nt, start, finish)


def _scatter_rider(ps):
    nt = len(ps)

    def copy(ins, outs, send, recv, t, r, px, py, c):
        return pltpu.make_async_remote_copy(src_ref=ins[t].at[2 * px + py], dst_ref=outs[t].at[r], send_sem=send.at[3 * t + r],
                                            recv_sem=recv.at[3 * t + r], device_id=(px, py, c), device_id_type=MESH)

    def start(ins, outs, send, recv):
        x, y, c, rel = _place()
        for t in range(nt):
            for r, (px, py) in enumerate(rel):
                copy(ins, outs, send, recv, t, r, px, py, c).start()

    def finish(ins, outs, send, recv):
        x, y, c, rel = _place()
        for t in range(nt):
            for r, (px, py) in enumerate(rel):
                copy(ins, outs, send, recv, t, r, px, py, c).wait()

    return _Rider(ps, [jax.ShapeDtypeStruct((3,) + tuple(p.shape[1:]), p.dtype) for p in ps], 3 * nt, start, finish)


def _pair_swap(tag, ts):
    nt = len(ts)

    def body(*refs):
        ins, outs = refs[:nt], refs[nt:2 * nt]
        send, recv = refs[2 * nt:]
        x, y, c, _ = _place()
        cps = []
        for t in range(nt):
            rh = ts[t].shape[1] // 2
            src = ins[t].at[:, pl.ds(pl.multiple_of((1 - c) * rh, 16), rh), :]
            cp = pltpu.make_async_remote_copy(src_ref=src, dst_ref=outs[t], send_sem=send.at[t], recv_sem=recv.at[t],
                                              device_id=(x, y, 1 - c), device_id_type=MESH)
            cp.start()
            cps.append(cp)
        for cp in cps:
            cp.wait()

    return pl.pallas_call(
        body, name="pair_swap_" + tag, in_specs=[_HBM] * nt, out_specs=[_HBM] * nt,
        out_shape=[jax.ShapeDtypeStruct((4, t.shape[1] // 2, t.shape[2]), t.dtype) for t in ts],
        scratch_shapes=[pltpu.SemaphoreType.DMA((nt,)), pltpu.SemaphoreType.DMA((nt,))],
    )(*ts)


def _pair_exchange(gs):
    nt = len(gs)

    def body(*refs):
        ins, outs = refs[:nt], refs[nt:2 * nt]
        send, recv = refs[2 * nt:]
        x, y, c, _ = _place()
        cps = []
        for t in range(nt):
            cp = pltpu.make_async_remote_copy(src_ref=ins[t], dst_ref=outs[t], send_sem=send.at[t], recv_sem=recv.at[t],
                                              device_id=(x, y, 1 - c), device_id_type=MESH)
            cp.start()
            cps.append(cp)
        for cp in cps:
            cp.wait()

    return pl.pallas_call(
        body, name="pair_exchange", in_specs=[_HBM] * nt, out_specs=[_HBM] * nt,
        out_shape=[jax.ShapeDtypeStruct(tuple(g.shape), g.dtype) for g in gs],
        scratch_shapes=[pltpu.SemaphoreType.DMA((nt,)), pltpu.SemaphoreType.DMA((nt,))],
    )(*gs)


def _allgather8(v):
    m, n = v.shape

    def body(x_ref, out_ref, send, recv, lsem):
        x, y, c, rel = _place()
        me, sib = (x, y, c), (x, y, 1 - c)

        def blk(px, py, pc):
            return out_ref.at[4 * px + 2 * py + pc]

        def copy(k, block, to, src=None):
            return pltpu.make_async_remote_copy(src_ref=blk(*block) if src is None else src, dst_ref=blk(*block), send_sem=send.at[k],
                                                recv_sem=recv.at[k], device_id=to, device_id_type=MESH)

        mine = pltpu.make_async_copy(x_ref, blk(*me), lsem)
        mine.start()
        first = [copy(0, me, sib, src=x_ref)] + [copy(1 + r, me, (*ch, c), src=x_ref) for r, ch in enumerate(rel)]
        for cp in first:
            cp.start()
        passed = [copy(4 + r, (*ch, c), sib) for r, ch in enumerate(rel)]
        for r, ch in enumerate(rel):
            copy(1 + r, (*ch, c), me).wait_recv()
            passed[r].start()
        copy(0, sib, me).wait_recv()
        for r, ch in enumerate(rel):
            copy(4 + r, (*ch, 1 - c), me).wait_recv()
        for cp in first + passed:
            cp.wait_send()
        mine.wait()

    return pl.pallas_call(
        body, name="allgather8", in_specs=[pl.BlockSpec(memory_space=pltpu.VMEM)], out_specs=pl.BlockSpec(memory_space=pltpu.VMEM),
        out_shape=jax.ShapeDtypeStruct((8, m, n), v.dtype),
        scratch_shapes=[pltpu.SemaphoreType.DMA((7,)), pltpu.SemaphoreType.DMA((7,)), pltpu.SemaphoreType.DMA],
    )(v)


def _t5_bucket(dist):
    max_exact = REL_BUCKETS // 2
    d = jnp.maximum(dist, 1).astype(F32)
    large = max_exact + (jnp.log(d / max_exact) / math.log(REL_MAX_DIST / max_exact) * (REL_BUCKETS - max_exact)).astype(jnp.int32)
    large = jnp.minimum(large, REL_BUCKETS - 1)
    return jnp.where(dist < max_exact, dist, large)


def _bias_onehot():
    qi = jnp.arange(SWA_BLOCK)[:, None]
    kj = jnp.arange(SWA_BLOCK)[None, :]
    dist = jnp.concatenate([(qi + SWA_BLOCK - kj).reshape(-1), (qi - kj).reshape(-1)])
    bucket = _t5_bucket(jnp.maximum(dist, 0))
    return (bucket[None, :] == jnp.arange(REL_BUCKETS)[:, None]).astype(F32)


def _head_spread():
    lane = jnp.arange(LANES)[:, None]
    head = jnp.arange(GDN_W)[None, :] // GDN_HEAD_DIM
    return (lane == head).astype(F32), (lane == head + GDN_HEADS).astype(F32)


def _lane16(v8):
    return jnp.pad(v8.astype(F32), (GDN_HEADS, LANES - 2 * GDN_HEADS)).reshape(1, LANES)


def _stack_heads(t, nb):
    return t.reshape(nb, SWA_BLOCK, SWA_KV_HEADS, SWA_GRP, SWA_HEAD_DIM).transpose(2, 0, 3, 1, 4).reshape(
        SWA_KV_HEADS, nb * SWA_GRP * SWA_BLOCK, SWA_HEAD_DIM)


def _unstack_heads(t, nb):
    return t.reshape(SWA_KV_HEADS, nb, SWA_GRP, SWA_BLOCK, SWA_HEAD_DIM).transpose(1, 3, 0, 2, 4).reshape(nb * SWA_BLOCK, SWA_Q)


def _kv_heads(t):
    return t.reshape(t.shape[0], SWA_KV_HEADS, SWA_HEAD_DIM).transpose(1, 0, 2)


def _swa_specs(qs, ks, vs, bp, bc, sk, grad, gdt=(F32,)):
    T = ks.shape[1]
    qr = SWA_GRP * SWA_BLOCK
    g = lambda a: tuple(a.shape) if grad else None
    m3 = lambda j, n: (j, n, 0)
    h3 = lambda j, n: (j, jnp.maximum(n - 1, 0), 0)
    p3 = lambda j: (j, 0, 0)
    rows = [Row(qs, (None, qr, SWA_HEAD_DIM), m3, gshape=g(qs), gmap=m3, gdt=gdt),
            Row(ks, (None, SWA_BLOCK, SWA_HEAD_DIM), m3, (None, SWA_BLOCK, SWA_HEAD_DIM), h3, g(ks), m3, gdt),
            Row(vs, (None, SWA_BLOCK, SWA_HEAD_DIM), m3, (None, SWA_BLOCK, SWA_HEAD_DIM), h3, g(vs), m3, gdt)]
    pars = [Par(bp, (None, qr, SWA_BLOCK), p3, g(bp), p3), Par(bc, (None, qr, SWA_BLOCK), p3, g(bc), p3),
            Par(sk, (None, qr, 1), p3, g(sk), p3)]
    return rows, pars, T // SWA_BLOCK


class _LocalWeights:
    def __init__(self, W):
        self.W = W

    def w1(self):
        return self.W

    def rider_a(self):
        return None

    def w2(self, got):
        return self.W

    def rider_b(self):
        return None

    def w3(self, got):
        return self.W

    def rider_g(self, G):
        return None

    def g_done(self, got):
        pass

    def rider_last(self, G):
        return None

    def last_done(self, got):
        pass


def _fwd_bwd(x, mem, tgt, src):
    W = dict(src.w1())
    T = x.shape[0]
    nb = T // SWA_BLOCK
    tb = min(256, T)
    tbl = min(512, T)
    fwd = lambda f: (lambda *a: (f(*a), []))
    full = lambda cols, dt, t, cw: Out((T, cols), dt, (t, cw), lambda j, n: (n, j))

    xb = x.astype(_CDT)
    ra = src.rider_a()
    proj = _mm("proj", xb, W["in_p"], "nn", rider=ra)
    proj, got = proj if ra is not None else (proj, None)
    W.update(src.w2(got))

    onehot_t = _bias_onehot()
    bias_flat = _mm("swa_bias", W["rel_bias"].T, onehot_t, "nn", hi=True)
    half = SWA_BLOCK * SWA_BLOCK
    bp = bias_flat[:, :half].reshape(SWA_KV_HEADS, SWA_GRP * SWA_BLOCK, SWA_BLOCK)
    bc = bias_flat[:, half:].reshape(SWA_KV_HEADS, SWA_GRP * SWA_BLOCK, SWA_BLOCK)
    sk = jnp.broadcast_to(W["swa_sinks"].reshape(SWA_KV_HEADS, SWA_GRP, 1, 1), (SWA_KV_HEADS, SWA_GRP, SWA_BLOCK, 1)).reshape(
        SWA_KV_HEADS, SWA_GRP * SWA_BLOCK, 1)
    qs = _stack_heads(proj[:, P_SQ:P_SQ + SWA_Q], nb)
    ks = _kv_heads(proj[:, P_SK:P_SK + SWA_KV])
    vs = _kv_heads(proj[:, P_SV:P_SV + SWA_KV])
    rows, pars, nblk = _swa_specs(qs, ks, vs, bp, bc, sk, False)
    o_s, = _rowmap("swa_fwd", fwd(_swa_fn), SWA_KV_HEADS, nblk, rows, pars,
                   [Out(tuple(qs.shape), F32, (None, SWA_GRP * SWA_BLOCK, SWA_HEAD_DIM), lambda j, n: (j, n, 0))])
    o_swa = _unstack_heads(o_s, nb).astype(_CDT)

    ncq = 3 * GDN_W // LANES
    tbp = min(1024, T)
    pre_rows = lambda grad: [_rowspec(proj, tbp, LANES, P_GQKV // LANES, halo=SUBLANES, grad=grad, ncol=ncq, gdt=(_CDT,))]
    pre_pars = lambda grad: [_parspec(W["gdn_conv_w"], LANES, 0, grad=grad, ncol=ncq)]
    qkv_n, = _rowmap("gdn_pre_fwd", fwd(_gdn_pre_fn), ncq, T // tbp, pre_rows(False), pre_pars(False),
                     [full(3 * GDN_W, F32, tbp, LANES)])
    eb, eg = _head_spread()
    alog_row, dtb_row = _lane16(W["gdn_a_log"]), _lane16(W["gdn_dt_bias"])
    gate_rows = lambda grad: [_rowspec(proj, tbl, LANES, P_BA // LANES, cstep=0, grad=grad, gdt=(_CDT,))]
    gate_pars = lambda grad: [_parspec(alog_row, grad=grad), _parspec(dtb_row, grad=grad), _parspec(eb), _parspec(eg)]
    bx, gx = _rowmap("gdn_gate_fwd", fwd(_gdn_gate_fn), 1, T // tbl, gate_rows(False), gate_pars(False),
                     [full(GDN_W, F32, tbl, GDN_W), full(GDN_W, F32, tbl, GDN_W)])
    nw = W["gdn_norm_w"].reshape(1, GDN_HEAD_DIM)
    o_gdn, states, got = _gdn_chunks_fwd(qkv_n, bx, gx, proj, nw, rider=src.rider_b())
    W.update(src.w3(got))

    ys = _mm("y_swa", o_swa, W["br_swa"], "nn")
    yg = _mm("y_gdn", o_gdn, W["br_gdn"], "nn")
    cwm = 512
    mix_rows = lambda grad: [_rowspec(proj, tb, cwm, P_GS // cwm, grad=grad, ncol=D_MODEL // cwm, gdt=(_CDT,)),
                             _rowspec(proj, tb, cwm, P_GG // cwm, grad=grad, ncol=D_MODEL // cwm, gdt=(_CDT,)),
                             _rowspec(ys, tb, cwm, 0, grad=grad, ncol=D_MODEL // cwm, gdt=(_CDT,)),
                             _rowspec(yg, tb, cwm, 0, grad=grad, ncol=D_MODEL // cwm, gdt=(_CDT,))]
    mixed, = _rowmap("mix_fwd", fwd(_mix_fn), D_MODEL // cwm, T // tb, mix_rows(False), [], [full(D_MODEL, _CDT, tb, cwm)])
    r1 = _mm("r1", mixed, W["mix_o"], "nn", add=x, add_scale=ALPHA)

    def ln_fwd(name, r, g, b):
        return _rowmap(name, _ln_fwd_fn, 1, T // tb, [_rowspec(r, tb, D_MODEL, 0)], [_parspec(g), _parspec(b)],
                       [full(D_MODEL, F32, tb, D_MODEL), full(D_MODEL, _CDT, tb, D_MODEL)])

    def ln_bwd(name, r, g, b, ct):
        return _rowmap_bwd(name, _ln_fn, 1, T // tb, [_rowspec(r, tb, D_MODEL, 0, grad=True, gdt=(F32, _CDT))],
                           [_parspec(g, grad=True), _parspec(b, grad=True)], [_rowspec(ct, tb, D_MODEL, 0)])

    g1, b1 = W["ln1_g"].reshape(1, -1), W["ln1_b"].reshape(1, -1)
    g2, b2 = W["ln2_g"].reshape(1, -1), W["ln2_b"].reshape(1, -1)
    g3, b3 = W["ln3_g"].reshape(1, -1), W["ln3_b"].reshape(1, -1)
    x1, x1b = ln_fwd("ln1_fwd", r1, g1, b1)

    qm = _mm("mem_q", x1b, W["mem_q"], "nn")
    kvm = _mm("mem_kv", mem, W["mem_kv"], "nn")
    ma_rows = lambda grad: [_rowspec(qm, tbl, MEM_HEAD_DIM, 0, grad=grad, ncol=MEM_HEADS, gdt=(_CDT,))]
    ma_pars = lambda grad: [_parspec(kvm, MEM_HEAD_DIM, 0, grad=grad, ncol=MEM_HEADS),
                            _parspec(kvm, MEM_HEAD_DIM, MEM_HEADS, grad=grad, ncol=MEM_HEADS)]
    om, = _rowmap("memattn_fwd", fwd(_memattn_fn), MEM_HEADS, T // tbl, ma_rows(False), ma_pars(False),
                  [full(MEM_W, _CDT, tbl, MEM_HEAD_DIM)])
    r2 = _mm("r2", om, W["mem_o"], "nn", add=x1, add_scale=ALPHA)
    x2, x2b = ln_fwd("ln2_fwd", r2, g2, b2)

    hcat = _mm("ffn_up", x2b, W["up_p"], "nn")
    cwf = 512
    ncf = D_FF_PAD // cwf
    cw_p, cb_p = W["ffn_conv_w_p"], W["ffn_conv_b_p"]
    tbf = min(512, T)
    ffn_rows = lambda grad: [_rowspec(hcat, tbf, cwf, 0, halo=SUBLANES, grad=grad, ncol=ncf, gdt=(_CDT,)),
                             _rowspec(hcat, tbf, cwf, ncf, halo=SUBLANES, grad=grad, ncol=ncf, gdt=(_CDT,))]
    ffn_pars = lambda grad: [_parspec(cw_p, cwf, 0, grad=grad, ncol=ncf), _parspec(cw_p, cwf, ncf, grad=grad, ncol=ncf),
                             _parspec(cb_p, cwf, 0, grad=grad, ncol=ncf), _parspec(cb_p, cwf, ncf, grad=grad, ncol=ncf)]
    act, = _rowmap("ffn_act_fwd", fwd(_ffn_act_fn), ncf, T // tbf, ffn_rows(False), ffn_pars(False), [full(D_FF_PAD, _CDT, tbf, cwf)])
    r3 = _mm("r3", act, W["down_p"], "nn", add=x2, add_scale=ALPHA)
    dr3, dr3b, lacc, dg3, db3 = _rowmap("ln3_loss", _loss_fn, 1, T // tb, [_rowspec(r3, tb, D_MODEL, 0), _rowspec(tgt, tb, D_MODEL, 0)],
                                        [_parspec(g3), _parspec(b3)], [full(D_MODEL, F32, tb, D_MODEL), full(D_MODEL, _CDT, tb, D_MODEL)],
                                  accs=[(SUBLANES, LANES), (1, D_MODEL), (1, D_MODEL)])
    loss = lacc[0, 0]

    G = {}
    G["down_p"] = _mm("dw_down", act, dr3b, "tn", out_dtype=_GDT)
    dact = _mm("d_act", dr3b, W["down_p"], "nt")
    dhg, dhu, dcwg, dcwu, dcbg, dcbu = _rowmap_bwd("ffn_act_bwd", _ffn_act_fn, ncf, T // tbf, ffn_rows(True), ffn_pars(True),
                                                   [_rowspec(dact, tbf, cwf, 0)])
    w_gate, w_upp = W["up_p"][:, :D_FF_PAD], W["up_p"][:, D_FF_PAD:]
    dx2 = _mm("dx2_gate", dhg, w_gate, "nt", add=dr3, add_scale=ALPHA)
    dx2 = _mm("dx2_up", dhu, w_upp, "nt", add=dx2)
    G["up_p"] = jnp.concatenate([_mm("dw_gate", x2b, dhg, "tn", out_dtype=_GDT), _mm("dw_up", x2b, dhu, "tn", out_dtype=_GDT)], axis=1)
    G["ffn_conv_w"] = jnp.concatenate([dcwg[:, :D_FF], dcwu[:, :D_FF]], axis=1)
    G["ffn_conv_b"] = jnp.concatenate([dcbg[0, :D_FF], dcbu[0, :D_FF]])
    G["ln3_g"], G["ln3_b"] = dg3[0], db3[0]

    dr2, dr2b, dg2, db2 = ln_bwd("ln2_bwd", r2, g2, b2, dx2)
    G["ln2_g"], G["ln2_b"] = dg2[0], db2[0]
    G["mem_o"] = _mm("dw_mem_o", om, dr2b, "tn", out_dtype=_GDT)
    dom = _mm("d_om", dr2b, W["mem_o"], "nt", out_dtype=_CDT)
    dqm, dkm, dvm = _rowmap_bwd("memattn_bwd", _memattn_fn, MEM_HEADS, T // tbl, ma_rows(True), ma_pars(True),
                                [_rowspec(dom, tbl, MEM_HEAD_DIM, 0)])
    G["mem_kv"] = _mm("dw_mem_kv", mem.astype(_CDT), jnp.concatenate([dkm, dvm], axis=1).astype(_CDT), "tn", out_dtype=_GDT)
    G["mem_q"] = _mm("dw_mem_q", x1b, dqm, "tn", out_dtype=_GDT)
    dx1 = _mm("dx1", dqm, W["mem_q"], "nt", add=dr2, add_scale=ALPHA)

    dr1, dr1b, dg1, db1 = ln_bwd("ln1_bwd", r1, g1, b1, dx1)
    G["ln1_g"], G["ln1_b"] = dg1[0], db1[0]
    G["mix_o"] = _mm("dw_mix_o", mixed, dr1b, "tn", out_dtype=_GDT)
    dmixed = _mm("d_mixed", dr1b, W["mix_o"], "nt")
    dgs, dgg, dys, dyg = _rowmap_bwd("mix_bwd", _mix_fn, D_MODEL // cwm, T // tb, mix_rows(True), [], [_rowspec(dmixed, tb, cwm, 0)])
    G["br_swa"] = _mm("dw_br_swa", o_swa, dys, "tn", out_dtype=_GDT)
    G["br_gdn"] = _mm("dw_br_gdn", o_gdn, dyg, "tn", out_dtype=_GDT)
    do_swa = _mm("d_o_swa", dys, W["br_swa"], "nt", out_dtype=_CDT)
    do_gdn = _mm("d_o_gdn", dyg, W["br_gdn"], "nt")

    rows, pars, nblk = _swa_specs(qs, ks, vs, bp, bc, sk, True, (_CDT,))
    m3 = lambda j, n: (j, n, 0)
    dqs, dks, dvs, dbp, dbc, dsk = _rowmap_bwd("swa_bwd", _swa_fn, SWA_KV_HEADS, nblk, rows, pars,
                                               [Row(_stack_heads(do_swa, nb), (None, SWA_GRP * SWA_BLOCK, SWA_HEAD_DIM), m3)])
    d_swa = jnp.concatenate([_unstack_heads(dqs, nb), dks.transpose(1, 0, 2).reshape(T, SWA_KV),
                             dvs.transpose(1, 0, 2).reshape(T, SWA_KV)], axis=1)
    dbias = jnp.concatenate([dbp.reshape(SWA_HEADS, half), dbc.reshape(SWA_HEADS, half)], axis=1)
    G["rel_bias"] = _mm("d_rel_bias", dbias, onehot_t.T, "nn", hi=True).T
    G["swa_sinks"] = _mm("d_sinks", dsk.reshape(SWA_HEADS, SWA_BLOCK), jnp.ones((SWA_BLOCK, LANES), F32), "nn", hi=True)[:, 0]

    dqkv_n, dbx, dgx, dz, dnw, got = _gdn_chunks_bwd(qkv_n, bx, gx, proj, nw, states, do_gdn, rider=src.rider_g(G))
    src.g_done(got)
    G["gdn_norm_w"] = dnw[0]
    dgba, dalog, ddtb = _rowmap_bwd("gdn_gate_bwd", _gdn_gate_fn, 1, T // tbl, gate_rows(True), gate_pars(True),
                                    [_rowspec(dbx, tbl, GDN_W, 0), _rowspec(dgx, tbl, GDN_W, 0)])
    G["gdn_a_log"], G["gdn_dt_bias"] = dalog[0, GDN_HEADS:2 * GDN_HEADS], ddtb[0, GDN_HEADS:2 * GDN_HEADS]
    dgqkv, dcw_gdn = _rowmap_bwd("gdn_pre_bwd", _gdn_pre_fn, ncq, T // tbp, pre_rows(True), pre_pars(True),
                                 [_rowspec(dqkv_n, tbp, LANES, 0)])
    G["gdn_conv_w"] = dcw_gdn

    dproj = jnp.concatenate([dgs, dgg, dgqkv, dz, d_swa, dgba, jnp.zeros((T, P_END - P_USED), _CDT)], axis=1)
    G["in_p"] = _mm("dw_in", xb, dproj, "tn", out_dtype=_GDT)
    rl = src.rider_last(G)
    dx = _mm("dx", dproj, W["in_p"], "nt", add=dr1, add_scale=ALPHA, rider=rl)
    if rl is not None:
        dx, got = dx
        src.last_done(got)
    return loss, dx, G


W_NAMES = ["w_in", "rel_bias", "swa_sinks", "gdn_conv_w", "gdn_a_log", "gdn_dt_bias", "gdn_norm_w", "w_br_swa", "w_br_gdn",
           "w_mix_o", "ln1_g", "ln1_b", "w_mem_q", "w_mem_kv", "w_mem_o", "ln2_g", "ln2_b", "w_up", "ffn_conv_w", "ffn_conv_b",
           "w_down", "ln3_g", "ln3_b"]
BIG = ["w_in", "w_br_swa", "w_br_gdn", "w_mix_o", "w_mem_q", "w_mem_kv", "w_mem_o", "w_up", "w_down"]
SMALL = [n for n in W_NAMES if n not in BIG]
COL_SHARDED = ["w_in", "w_br_swa", "w_br_gdn", "w_mem_o", "w_up"]


def _pack(arrs):
    rows = []
    for a in arrs:
        f = a.reshape(-1).astype(F32)
        rows.append(jnp.pad(f, (0, (-f.shape[0]) % LANES)).reshape(-1, LANES))
    n = sum(r.shape[0] for r in rows)
    if n % 16:
        rows.append(jnp.zeros((16 - n % 16, LANES), F32))
    return jnp.concatenate(rows, axis=0)


def _unpack(p, shapes):
    out, off = [], 0
    for s in shapes:
        n = int(np.prod(s)) if len(s) else 1
        r = -(-n // LANES)
        out.append(p[off:off + r].reshape(-1)[:n].reshape(s))
        off += r
    return out


def _merge_shards(d):
    cat = lambda names: jnp.concatenate([d[n] for n in names], axis=-2)
    return [d.get("w_in"), d["w_up"], cat(["w_br_swa", "w_br_gdn", "w_mem_q", "w_mem_o"]), cat(["w_mix_o", "w_down"]), d["w_mem_kv"]]


def _split_shards(ts):
    a, b, c, dd, e = ts
    return {"w_in": a, "w_up": b, "w_br_swa": c[..., 0:1024, :], "w_br_gdn": c[..., 1024:2048, :], "w_mem_q": c[..., 2048:2560, :],
            "w_mem_o": c[..., 2560:3072, :], "w_mix_o": dd[..., 0:512, :], "w_down": dd[..., 512:, :], "w_mem_kv": e}


def _to_full(name, t):
    if name in COL_SHARDED:
        return t.transpose(1, 0, 2).reshape(t.shape[1], 4 * t.shape[2])
    return t.reshape(4 * t.shape[1], t.shape[2])


def _to_chips(name, t):
    if name in COL_SHARDED:
        return t.reshape(t.shape[0], 4, t.shape[1] // 4).transpose(1, 0, 2)
    return t.reshape(4, t.shape[0] // 4, t.shape[1])


def _cols_from_chips(g, segs):
    C, parts = g.shape[2], []
    for s in segs:
        if isinstance(s, int):
            parts.append(jnp.zeros((g.shape[1], s), g.dtype))
            continue
        lo, hi = s
        while lo < hi:
            k = lo // C
            e = min(hi, (k + 1) * C)
            parts.append(g[k][:, lo - k * C:e - k * C])
            lo = e
    return jnp.concatenate(parts, axis=1)


def _chips_from_cols(p, segs, C):
    out = []
    for k in range(4):
        lo, hi, parts, o = k * C, (k + 1) * C, [], 0
        for plo, w in segs:
            a, b = max(lo, o), min(hi, o + w)
            if a < b:
                parts.append(p[:, plo + a - o:plo + b - o])
            o += w
        out.append(jnp.concatenate(parts, axis=1))
    return jnp.stack(out)


_IN_OFF = np.cumsum((0,) + IN_WIDTHS)
_IN_SEGS = [(P_SQ, SWA_Q), (P_SK, SWA_KV), (P_SV, SWA_KV), (P_GQKV, 3 * GDN_W), (P_GZ, GDN_W), (P_BA, 2 * GDN_HEADS),
            (P_GS, D_MODEL), (P_GG, D_MODEL)]
_IN_PADDED = [(int(_IN_OFF[i]), int(_IN_OFF[k])) for i, k in ((9, 10), (10, 11), (3, 6), (6, 7), (0, 1), (1, 2), (2, 3), (7, 9))] + [
    P_END - P_BA - 2 * GDN_HEADS]
_UP_SEGS = [(0, D_FF), (D_FF_PAD, D_FF)]
_UP_PADDED = [(0, D_FF), D_FF_PAD - D_FF, (D_FF, 2 * D_FF), D_FF_PAD - D_FF]


def _in_to_padded(w):
    o = _IN_OFF
    cut = lambda i, k: w[:, o[i]:o[k]]
    return jnp.concatenate([cut(9, 10), cut(10, 11), cut(3, 6), cut(6, 7), cut(0, 1), cut(1, 2), cut(2, 3), cut(7, 9),
                            jnp.zeros((w.shape[0], P_END - P_BA - 2 * GDN_HEADS), w.dtype)], axis=1)


def _in_from_padded(p):
    return jnp.concatenate([p[:, P_SQ:P_SQ + SWA_Q], p[:, P_SK:P_SK + SWA_KV], p[:, P_SV:P_SV + SWA_KV], p[:, P_GQKV:P_GQKV + 3 * GDN_W],
                            p[:, P_GZ:P_GZ + GDN_W], p[:, P_BA:P_BA + 2 * GDN_HEADS], p[:, P_GS:P_GS + D_MODEL], p[:, P_GG:P_GG + D_MODEL]],
                           axis=1)


def _ff_pad(t, axis):
    g, u = jnp.split(t, 2, axis=axis)
    pad = [(0, 0)] * t.ndim
    pad[axis] = (0, D_FF_PAD - D_FF)
    return jnp.concatenate([jnp.pad(g, pad), jnp.pad(u, pad)], axis=axis)


def _ff_unpad(t, axis):
    g, u = jnp.split(t, 2, axis=axis)
    return jnp.concatenate([lax.slice_in_dim(g, 0, D_FF, axis=axis), lax.slice_in_dim(u, 0, D_FF, axis=axis)], axis=axis)


def _assemble_weights(full, small):
    W = dict(small)
    W["in_p"] = _in_to_padded(full["w_in"])
    W["up_p"] = _ff_pad(full["w_up"], 1)
    W["down_p"] = jnp.pad(full["w_down"], ((0, D_FF_PAD - D_FF), (0, 0)))
    W["br_swa"], W["br_gdn"], W["mix_o"] = full["w_br_swa"], full["w_br_gdn"], full["w_mix_o"]
    W["mem_q"], W["mem_kv"], W["mem_o"] = full["w_mem_q"], full["w_mem_kv"], full["w_mem_o"]
    W["ffn_conv_w_p"] = _ff_pad(small["ffn_conv_w"], 1)
    W["ffn_conv_b_p"] = _ff_pad(small["ffn_conv_b"].reshape(1, -1), 1)
    return W


def _full_grads(G):
    out = {"w_in": _in_from_padded(G["in_p"])} if "in_p" in G else {}
    out.update({"w_up": _ff_unpad(G["up_p"], 1), "w_down": G["down_p"][:D_FF], "w_br_swa": G["br_swa"], "w_br_gdn": G["br_gdn"],
                "w_mix_o": G["mix_o"], "w_mem_q": G["mem_q"], "w_mem_kv": G["mem_kv"], "w_mem_o": G["mem_o"]})
    return out


def kernel(x, mem, w_in, rel_bias, swa_sinks, gdn_conv_w, gdn_a_log, gdn_dt_bias, gdn_norm_w, w_br_swa, w_br_gdn, w_mix_o, ln1_g, ln1_b, w_mem_q, w_mem_kv, w_mem_o, ln2_g, ln2_b, w_up, ffn_conv_w, ffn_conv_b, w_down, ln3_g, ln3_b, loss_target, m_w_in, m_rel_bias, m_swa_sinks, m_gdn_conv_w, m_gdn_a_log, m_gdn_dt_bias, m_gdn_norm_w, m_w_br_swa, m_w_br_gdn, m_w_mix_o, m_ln1_g, m_ln1_b, m_w_mem_q, m_w_mem_kv, m_w_mem_o, m_ln2_g, m_ln2_b, m_w_up, m_ffn_conv_w, m_ffn_conv_b, m_w_down, m_ln3_g, m_ln3_b, v_w_in, v_rel_bias, v_swa_sinks, v_gdn_conv_w, v_gdn_a_log, v_gdn_dt_bias, v_gdn_norm_w, v_w_br_swa, v_w_br_gdn, v_w_mix_o, v_ln1_g, v_ln1_b, v_w_mem_q, v_w_mem_kv, v_w_mem_o, v_ln2_g, v_ln2_b, v_w_up, v_ffn_conv_w, v_ffn_conv_b, v_w_down, v_ln3_g, v_ln3_b):
    a = dict(locals())
    w = {n: a[n] for n in W_NAMES}
    m = {n: a["m_" + n] for n in W_NAMES}
    v = {n: a["v_" + n] for n in W_NAMES}
    chip = 2 * lax.axis_index("x") + lax.axis_index("y")
    core = lax.axis_index("c")
    sq = lambda t: t.reshape(t.shape[1:]) if (t.ndim > 1 and t.shape[0] == 1 and t is not rel_bias) else t

    sh_a, sh_b, sh_c, sh_d, sh_e = _merge_shards({n: sq(w[n]).astype(_CDT) for n in BIG})
    fcw_sh, gcw_sh = sq(ffn_conv_w).shape, sq(gdn_conv_w).shape
    slot = lax.broadcasted_iota(jnp.int32, (4, 1, 1), 0)

    def with_own(got, mine):
        return [jnp.where(slot == chip, t[None], g) for g, t in zip(got, mine)]

    def reduce_start(tag, gch):
        pair = []
        for t, (mine, got) in enumerate(zip(gch, _pair_swap(tag, gch))):
            rh = mine.shape[1] // 2
            mine_h = lax.dynamic_slice_in_dim(mine, core * rh, rh, axis=1)
            pair.append(_addn(f"pair_sum_{tag}{t}", [mine_h.reshape(4 * rh, -1), got.reshape(4 * rh, -1)], _GDT).reshape(4, rh, -1))
        return pair

    def reduce_end(tag, pair, others):
        halves = []
        for t, (p, o) in enumerate(zip(pair, others)):
            own = lax.dynamic_index_in_dim(p, chip, 0, keepdims=False)
            halves.append(_addn(f"chip_sum_{tag}{t}", [own, (o, 0), (o, 1), (o, 2)]))
        return halves

    class MeshWeights:
        def w1(self):
            mine = [sh_a, _pack([sq(ffn_conv_w), sq(gdn_conv_w)])]
            got_a, got_f = with_own(_run_rider("gather_first", _gather_rider(mine)), mine)
            conv = [_unpack(got_f[k], [fcw_sh, gcw_sh]) for k in range(4)]
            W = {n: sq(w[n]) for n in SMALL}
            W["ffn_conv_w"] = jnp.concatenate([cv[0] for cv in conv], axis=1)
            W["gdn_conv_w"] = jnp.concatenate([cv[1] for cv in conv], axis=1)
            W["ffn_conv_w_p"] = _ff_pad(W["ffn_conv_w"], 1)
            W["ffn_conv_b_p"] = _ff_pad(W["ffn_conv_b"].reshape(1, -1), 1)
            W["in_p"] = _cols_from_chips(got_a, _IN_PADDED)
            return W

        def rider_a(self):
            return _gather_rider([sh_c, sh_d, sh_e])

        def w2(self, got):
            c, d, e = with_own(got, [sh_c, sh_d, sh_e])
            f = {n: _to_full(n, t) for n, t in _split_shards([None, None, c, d, e]).items() if t is not None}
            return {"br_swa": f["w_br_swa"], "br_gdn": f["w_br_gdn"], "mix_o": f["w_mix_o"], "mem_q": f["w_mem_q"], "mem_kv": f["w_mem_kv"],
                    "mem_o": f["w_mem_o"], "down_p": jnp.pad(f["w_down"], ((0, D_FF_PAD - D_FF), (0, 0)))}

        def rider_b(self):
            return _gather_rider([sh_b])

        def w3(self, got):
            b, = with_own(got, [sh_b])
            return {"up_p": _cols_from_chips(b, _UP_PADDED)}

        def rider_g(self, G):
            gf = _full_grads(G)
            gch = {n: _to_chips(n, gf[n]) for n in BIG if n not in ("w_in", "w_up")}
            gch["w_up"] = _chips_from_cols(G["up_p"], _UP_SEGS, 2 * D_FF // 4)
            self.pair = reduce_start("rest", _merge_shards(gch)[1:])
            return _scatter_rider(self.pair)

        def g_done(self, got):
            self.halves = reduce_end("rest", self.pair, got)

        def rider_last(self, G):
            self.pair_in = reduce_start("in", [_chips_from_cols(G["in_p"], _IN_SEGS, sum(IN_WIDTHS) // 4)])
            return _scatter_rider(self.pair_in)

        def last_done(self, got):
            self.halves = reduce_end("in", self.pair_in, got) + self.halves

    src = MeshWeights()
    loss, dx, G = _fwd_bwd(x[0], mem[0], loss_target[0], src)

    small_names = SMALL
    small_shapes = [()] + [tuple(G[n].shape) for n in small_names]
    packed = _pack([loss] + [G[n] for n in small_names])
    allp = _allgather8(packed)
    tot = _addn("small_sum", [(allp, k) for k in range(8)])
    parts = _unpack(tot, small_shapes)
    loss_tot, gsmall = parts[0], dict(zip(small_names, parts[1:]))
    gsmall["ffn_conv_w"] = lax.dynamic_slice_in_dim(gsmall["ffn_conv_w"], chip * fcw_sh[1], fcw_sh[1], axis=1)
    gsmall["gdn_conv_w"] = lax.dynamic_slice_in_dim(gsmall["gdn_conv_w"], chip * gcw_sh[1], gcw_sh[1], axis=1)

    both = []
    for h, o in zip(src.halves, _pair_exchange(src.halves)):
        both.append(jnp.concatenate([jnp.where(core == 0, h, o), jnp.where(core == 0, o, h)], axis=0))
    gbig = _split_shards(both)

    outs = {}
    for n in BIG:
        d_, m_, v_ = _adamw("adamw_" + n, sq(w[n]), gbig[n], sq(m[n]), sq(v[n]))
        outs[n] = (gbig[n], d_, m_, v_)
    for n in SMALL:
        two_d = (-1, w[n].shape[-1])
        g_ = gsmall[n].reshape(two_d)
        d_, m_, v_ = _adamw("adamw_" + n, w[n].reshape(two_d), g_, m[n].reshape(two_d), v[n].reshape(two_d))
        outs[n] = (g_, d_, m_, v_)

    res = [loss_tot.reshape(()), dx.reshape(x.shape)]
    for k in range(4):
        res += [outs[n][k].reshape(w[n].shape) for n in W_NAMES]
    return tuple(res)
```

```python
import functools
import math

import jax
import jax.numpy as jnp
import numpy as np
from jax import lax
from jax.experimental import pallas as pl
from jax.experimental.pallas import tpu as pltpu

F32 = jnp.float32
BF16 = jnp.bfloat16
_CDT = BF16
_GDT = BF16

D_MODEL = 2048
SWA_HEADS, SWA_KV_HEADS, SWA_HEAD_DIM, SWA_BLOCK = 16, 2, 64, 128
SWA_GRP = SWA_HEADS // SWA_KV_HEADS
REL_BUCKETS, REL_MAX_DIST = 32, 128
GDN_HEADS, GDN_HEAD_DIM, GDN_CONV, GDN_CHUNK = 8, 128, 4, 64
MEM_HEADS, MEM_HEAD_DIM = 4, 128
D_FF, D_FF_PAD, FFN_CONV = 5504, 5632, 3
SWA_Q, SWA_KV, GDN_W, MEM_W = 1024, 128, 1024, 512
IN_WIDTHS = (SWA_Q, SWA_KV, SWA_KV, GDN_W, GDN_W, GDN_W, GDN_W, GDN_HEADS, GDN_HEADS, D_MODEL, D_MODEL)
NORM_EPS = 1e-5
ALPHA = 2.0 ** 0.25
NEG_INF = -1e30
ADAM_LR, ADAM_B1, ADAM_B2, ADAM_EPS, ADAM_WD, ADAM_STEP = 0.001, 0.9, 0.999, 1e-08, 0.01, 10
LANES, SUBLANES = 128, 8
VMEM_LIMIT = 56 * 1024 * 1024

P_GS, P_GG, P_GQKV, P_GZ, P_SQ, P_SK, P_SV, P_BA, P_USED, P_END = 0, 2048, 4096, 7168, 8192, 9216, 9344, 9472, 9600, 9728


def _tile(dim, pref, align=LANES):
    if dim <= pref:
        return dim
    t = (pref // align) * align
    while t >= align:
        if dim % t == 0:
            return t
        t -= align
    return dim


_DIMS = {"nn": (((1,), (0,)), ((), ())), "nt": (((1,), (1,)), ((), ())), "tn": (((0,), (0,)), ((), ()))}
_BDIMS = {"nn": (((2,), (1,)), ((0,), (0,))), "nt": (((2,), (2,)), ((0,), (0,))), "tn": (((1,), (1,)), ((0,), (0,)))}


def _raw_dot(a, b, form, hi):
    dims = (_BDIMS if a.ndim == 3 else _DIMS)[form]
    if hi == "x3":
        a, b = a.astype(F32), b.astype(F32)
        ah, bh = a.astype(BF16), b.astype(BF16)
        al, bl = (a - ah.astype(F32)).astype(BF16), (b - bh.astype(F32)).astype(BF16)
        d = lambda p, q: lax.dot_general(p, q, dims, preferred_element_type=F32)
        if form == "tn":
            return d(ah, bh) + (d(ah, bl) + d(al, bh))
        m = a.shape[-2]
        both = d(jnp.concatenate([ah, al], axis=-2), bh)
        return both[..., :m, :] + (d(ah, bl) + both[..., m:, :])
    if hi:
        return lax.dot_general(a.astype(F32), b.astype(F32), dims, precision=lax.Precision.HIGHEST, preferred_element_type=F32)
    return lax.dot_general(a.astype(_CDT), b.astype(_CDT), dims, preferred_element_type=F32)


@functools.partial(jax.custom_vjp, nondiff_argnums=(2, 3))
def _dot(a, b, form, hi=False):
    return _raw_dot(a, b, form, hi)


def _dot_fwd(a, b, form, hi):
    return _raw_dot(a, b, form, hi), (a, b)


def _dot_bwd(form, hi, res, g):
    a, b = res
    if form == "nn":
        da, db = _raw_dot(g, b, "nt", hi), _raw_dot(a, g, "tn", hi)
    elif form == "nt":
        da, db = _raw_dot(g, b, "nn", hi), _raw_dot(g, a, "tn", hi)
    else:
        da, db = _raw_dot(b, g, "nt", hi), _raw_dot(a, g, "nn", hi)
    return da.astype(a.dtype), db.astype(b.dtype)


_dot.defvjp(_dot_fwd, _dot_bwd)


@functools.partial(jax.custom_vjp, nondiff_argnums=(2,))
def _shift_halo(prev, cur, d):
    assert prev.shape[0] == SUBLANES
    return pltpu.roll(jnp.concatenate([prev, cur], axis=0), d, 0)[SUBLANES:]


def _shift_halo_fwd(prev, cur, d):
    return _shift_halo(prev, cur, d), None


def _shift_halo_bwd(d, _, g):
    nh = SUBLANES
    ext = jnp.concatenate([jnp.zeros((nh, g.shape[1]), g.dtype), g], axis=0)
    r = pltpu.roll(ext, ext.shape[0] - d, 0)
    return r[:nh], r[nh:]


_shift_halo.defvjp(_shift_halo_fwd, _shift_halo_bwd)


@jax.custom_vjp
def _recip(x):
    return 1.0 / x


def _recip_fwd(x):
    r = 1.0 / x
    return r, r


def _recip_bwd(r, g):
    return (-g * r * r,)


_recip.defvjp(_recip_fwd, _recip_bwd)


def _sigmoid(x):
    return _recip(1.0 + jnp.exp(-x))


def _silu(x):
    return x * _sigmoid(x)


def _softplus(x):
    return jnp.maximum(x, 0.0) + jnp.log(1.0 + jnp.exp(-jnp.abs(x)))


def _iota(shape, axis):
    return lax.broadcasted_iota(jnp.int32, shape, axis)


def _cparams(sem, **kw):
    return pltpu.CompilerParams(dimension_semantics=sem, vmem_limit_bytes=VMEM_LIMIT, **kw)


class _ride:
    def __init__(self, rider, n_in, n_out, n_scr):
        self.rider = rider
        self.ins = rider.ins if rider else []
        n_rin = len(self.ins)
        self.out_shapes = rider.out_shapes if rider else []
        n_rout = len(self.out_shapes)
        self.in_specs, self.out_specs = [_HBM] * n_rin, [_HBM] * n_rout
        self.scratch = rider.sems() if rider else []
        self.o0 = n_in + n_rin
        self.s0 = self.o0 + n_out + n_rout
        self._rin = slice(n_in, n_in + n_rin)
        self._rout = slice(self.o0 + n_out, self.s0)
        self._sem = self.s0 + n_scr

    def _args(self, refs):
        return refs[self._rin], refs[self._rout], refs[self._sem], refs[self._sem + 1]

    def at_start(self, refs, cond):
        if self.rider:
            pl.when(cond)(lambda: self.rider.start(*self._args(refs)))

    def at_end(self, refs, cond):
        if self.rider:
            pl.when(cond)(lambda: self.rider.finish(*self._args(refs)))


def _mm(name, a, b, form, out_dtype=F32, add=None, add_scale=1.0, hi=False, tm=1024, tn=1024, tk=2816, rider=None):
    if form == "nn":
        (M, K), (K2, N) = a.shape, b.shape
    elif form == "nt":
        (M, K), (N, K2) = a.shape, b.shape
    else:
        (K, M), (K2, N) = a.shape, b.shape
    assert K == K2, (name, a.shape, b.shape, form)
    tm, tn, tk = _tile(M, tm), _tile(N, tn), _tile(K, tk)
    nk = K // tk
    a_spec = pl.BlockSpec((tk, tm), lambda i, j, k: (k, i)) if form == "tn" else pl.BlockSpec((tm, tk), lambda i, j, k: (i, k))
    b_spec = pl.BlockSpec((tn, tk), lambda i, j, k: (j, k)) if form == "nt" else pl.BlockSpec((tk, tn), lambda i, j, k: (k, j))
    o_spec = pl.BlockSpec((tm, tn), lambda i, j, k: (i, j))
    has_add = add is not None

    def finish(r, c_ref, o_ref):
        if has_add:
            r = r + add_scale * c_ref[...].astype(F32)
        o_ref[...] = r.astype(out_dtype)

    n_own = 3 if has_add else 2
    grid = (M // tm, N // tn, nk)
    rd = _ride(rider, n_own, 1, 1 if nk > 1 else 0)

    def body(*refs):
        a_ref, b_ref = refs[:2]
        c_ref = refs[2] if has_add else None
        o_ref = refs[rd.o0]
        pid = [pl.program_id(d) for d in range(3)]
        rd.at_start(refs, (pid[0] == 0) & (pid[1] == 0) & (pid[2] == 0))
        if nk == 1:
            finish(_raw_dot(a_ref[...], b_ref[...], form, hi), c_ref, o_ref)
        else:
            acc = refs[rd.s0]

            @pl.when(pid[2] == 0)
            def _():
                acc[...] = jnp.zeros_like(acc)

            acc[...] += _raw_dot(a_ref[...], b_ref[...], form, hi)

            @pl.when(pid[2] == nk - 1)
            def _():
                finish(acc[...], c_ref, o_ref)
        rd.at_end(refs, (pid[0] == grid[0] - 1) & (pid[1] == grid[1] - 1) & (pid[2] == nk - 1))

    ins = [a, b] + ([add] if has_add else [])
    specs = [a_spec, b_spec] + ([o_spec] if has_add else [])
    res = pl.pallas_call(
        body, name=name, grid=grid, in_specs=specs + rd.in_specs, out_specs=[o_spec] + rd.out_specs,
        out_shape=[jax.ShapeDtypeStruct((M, N), out_dtype)] + rd.out_shapes,
        scratch_shapes=([pltpu.VMEM((tm, tn), F32)] if nk > 1 else []) + rd.scratch,
        compiler_params=_cparams(("arbitrary",) * 3 if rider else ("parallel", "parallel", "arbitrary")),
    )(*ins, *rd.ins)
    return (res[0], res[1:]) if rider else res[0]


class Row:
    def __init__(self, arr, blk, imap, hblk=None, hmap=None, gshape=None, gmap=None, gdt=(F32,)):
        self.arr, self.blk, self.imap, self.hblk, self.hmap, self.gshape, self.gmap = arr, blk, imap, hblk, hmap, gshape, gmap
        self.gdt = gdt


class Par:
    def __init__(self, arr, blk=None, imap=None, gshape=None, gmap=None):
        self.arr = arr
        self.blk = tuple(arr.shape) if blk is None else blk
        nd = len(self.blk)
        self.imap = (lambda j: (0,) * nd) if imap is None else imap
        self.gshape, self.gmap = gshape, gmap


class Out:
    def __init__(self, shape, dtype, blk, imap):
        self.shape, self.dtype, self.blk, self.imap = shape, dtype, blk, imap


def _rows_of(blk):
    return [d for d in blk if d is not None][0]


def _rowmap(name, fn, ncol, nblk, rows, pars, outs, accs=()):
    in_specs, ins = [], []
    for r in rows:
        ins.append(r.arr)
        in_specs.append(pl.BlockSpec(r.blk, r.imap))
        if r.hblk is not None:
            ins.append(r.arr)
            in_specs.append(pl.BlockSpec(r.hblk, r.hmap))
    for p in pars:
        ins.append(p.arr)
        in_specs.append(pl.BlockSpec(p.blk, (lambda im: (lambda j, n: im(j)))(p.imap)))
    out_specs = [pl.BlockSpec(o.blk, o.imap) for o in outs]
    out_shape = [jax.ShapeDtypeStruct(o.shape, o.dtype) for o in outs]
    for a in accs:
        out_specs.append(pl.BlockSpec(a, (lambda nd: (lambda j, n: (0,) * nd))(len(a))))
        out_shape.append(jax.ShapeDtypeStruct(a, F32))
    n_in = len(ins)

    def body(*refs):
        j, n = pl.program_id(0), pl.program_id(1)
        it = iter(refs[:n_in])
        rvals = []
        for r in rows:
            cur = next(it)[...]
            rvals.append((next(it)[...], cur) if r.hblk is not None else cur)
        pvals = [next(it)[...] for _ in pars]
        o_refs = refs[n_in:n_in + len(outs)]
        a_refs = refs[n_in + len(outs):]
        ovals, avals = fn(j, n == 0, rvals, pvals)
        for ref, v in zip(o_refs, ovals):
            ref[...] = v.astype(ref.dtype)
        if accs:
            @pl.when((j == 0) & (n == 0))
            def _():
                for ref in a_refs:
                    ref[...] = jnp.zeros_like(ref)
            for ref, v in zip(a_refs, avals):
                ref[...] += v

    res = pl.pallas_call(
        body, name=name, grid=(ncol, nblk), in_specs=in_specs, out_specs=out_specs, out_shape=out_shape,
        compiler_params=_cparams(("arbitrary", "arbitrary")),
    )(*ins)
    return res


def _rowmap_bwd(name, fn, ncol, nblk, rows, pars, cts):
    rev = lambda im: (lambda j, s: im(j, nblk - 1 - s))
    in_specs, ins = [], []
    for r in rows:
        ins.append(r.arr)
        in_specs.append(pl.BlockSpec(r.blk, rev(r.imap)))
        if r.hblk is not None:
            ins.append(r.arr)
            in_specs.append(pl.BlockSpec(r.hblk, rev(r.hmap)))
    for p in pars:
        ins.append(p.arr)
        in_specs.append(pl.BlockSpec(p.blk, (lambda im: (lambda j, s: im(j)))(p.imap)))
    for c in cts:
        ins.append(c.arr)
        in_specs.append(pl.BlockSpec(c.blk, rev(c.imap)))
    n_in = len(ins)
    drows = [i for i, r in enumerate(rows) if r.gshape is not None]
    dpars = [i for i, p in enumerate(pars) if p.gshape is not None]
    out_specs, out_shape, scratch = [], [], []
    for i in drows:
        r = rows[i]
        for dt in r.gdt:
            out_specs.append(pl.BlockSpec(r.blk, rev(r.gmap)))
            out_shape.append(jax.ShapeDtypeStruct(r.gshape, dt))
        if r.hblk is not None:
            scratch.append(pltpu.VMEM(tuple(d for d in r.hblk if d is not None), F32))
    n_drow_out = len(out_specs)
    for i in dpars:
        p = pars[i]
        out_specs.append(pl.BlockSpec(p.blk, (lambda im: (lambda j, s: im(j)))(p.gmap)))
        out_shape.append(jax.ShapeDtypeStruct(p.gshape, F32))

    def body(*refs):
        j, s = pl.program_id(0), pl.program_id(1)
        first = s == nblk - 1
        it = iter(refs[:n_in])
        rvals = []
        for r in rows:
            cur = next(it)[...]
            rvals.append((next(it)[...], cur) if r.hblk is not None else cur)
        pvals = [next(it)[...] for _ in pars]
        cvals = [next(it)[...].astype(F32) for _ in cts]
        g_refs = iter(refs[n_in:n_in + n_drow_out])
        p_refs = refs[n_in + n_drow_out:n_in + n_drow_out + len(dpars)]
        carries = iter(refs[n_in + n_drow_out + len(dpars):])

        def f(dr, dp):
            rv, pv = list(rvals), list(pvals)
            for i, v in zip(drows, dr):
                rv[i] = v
            for i, v in zip(dpars, dp):
                pv[i] = v
            return fn(j, first, rv, pv)

        _, vjp = jax.vjp(f, [rvals[i] for i in drows], [pvals[i] for i in dpars])
        g_r, g_p = vjp(cvals)
        for i, g in zip(drows, g_r):
            r = rows[i]
            if r.hblk is None:
                for _ in r.gdt:
                    ref = next(g_refs)
                    ref[...] = g.astype(ref.dtype)
            else:
                g_prev, g_cur = g
                carry = next(carries)
                nr, nh = g_cur.shape[-2], g_prev.shape[-2]
                tail = g_cur[..., nr - nh:nr, :] + jnp.where(s > 0, carry[...], 0.0)
                for _ in r.gdt:
                    ref = next(g_refs)
                    if nr > nh:
                        ref[..., 0:nr - nh, :] = g_cur[..., 0:nr - nh, :].astype(ref.dtype)
                    ref[..., nr - nh:nr, :] = tail.astype(ref.dtype)
                carry[...] = g_prev
        for ref, g in zip(p_refs, g_p):
            @pl.when(s == 0)
            def _():
                ref[...] = jnp.zeros_like(ref)
            ref[...] += g

    return pl.pallas_call(
        body, name=name, grid=(ncol, nblk), in_specs=in_specs, out_specs=out_specs, out_shape=out_shape,
        scratch_shapes=scratch, compiler_params=_cparams(("arbitrary", "arbitrary")),
    )(*ins)


def _rowspec(arr, tb, cw, c0, cstep=1, halo=0, grad=False, ncol=1, gdt=(F32,)):
    T = arr.shape[0]
    imap = lambda j, n: (n, c0 + cstep * j)
    hblk = hmap = None
    if halo:
        q = tb // halo
        hblk, hmap = (halo, cw), (lambda j, n: (jnp.maximum(n * q - 1, 0), c0 + cstep * j))
    gshape = (T, cw * (ncol if cstep else 1)) if grad else None
    gmap = (lambda j, n: (n, cstep * j)) if grad else None
    return Row(arr, (tb, cw), imap, hblk, hmap, gshape, gmap, gdt)


def _parspec(arr, cw=None, c0=0, grad=False, ncol=1):
    if cw is None:
        return Par(arr, gshape=tuple(arr.shape) if grad else None,
                   gmap=(lambda nd: (lambda j: (0,) * nd))(arr.ndim) if grad else None)
    r = arr.shape[0]
    return Par(arr, (r, cw), lambda j: (0, c0 + j), (r, cw * ncol) if grad else None, (lambda j: (0, j)) if grad else None)


def _ln(r, g, b):
    mu = jnp.mean(r, axis=-1, keepdims=True)
    xc = r - mu
    var = jnp.mean(xc * xc, axis=-1, keepdims=True)
    return xc * lax.rsqrt(var + NORM_EPS) * g + b


def _ln_fn(j, first, rv, pv):
    return [_ln(rv[0], pv[0], pv[1])]


def _ln_fwd_fn(j, first, rv, pv):
    y = _ln(rv[0], pv[0], pv[1])
    return [y, y], []


def _loss_fn(j, first, rv, pv):
    r3, tgt = rv
    g, b = pv
    y, vjp = jax.vjp(_ln, r3, g, b)
    diff = y - tgt
    part = 0.5 * jnp.sum(diff * diff) / D_MODEL
    dr, dg, db = vjp(diff * (1.0 / D_MODEL))
    return [dr, dr], [jnp.full((SUBLANES, LANES), part, F32), dg, db]


def _mix_fn(j, first, rv, pv):
    gs, gg, ys, yg = rv
    return [_sigmoid(gs) * ys + _sigmoid(gg) * yg]


def _row_pick(x, i):
    ax = x.ndim - 2
    return jnp.sum(jnp.where(_iota(x.shape, ax) == i, x, 0.0), axis=ax, keepdims=True)


def _causal_conv(prev, cur, w, first):
    width = w.shape[0]
    prev = jnp.where(first, 0.0, prev)
    y = cur * _row_pick(w, width - 1)
    for d in range(1, width):
        y = y + _shift_halo(prev, cur, d) * _row_pick(w, width - 1 - d)
    return y


def _ffn_act_fn(j, first, rv, pv):
    (pg, cg), (pu, cu) = rv
    wg, wu, bg, bu = pv
    hg = _causal_conv(pg, cg, wg, first) + bg
    hu = _causal_conv(pu, cu, wu, first) + bu
    return [_silu(hg) * hu]


def _gdn_pre_fn(j, first, rv, pv):
    (prev, cur), = rv
    w, = pv
    t = _silu(_causal_conv(prev, cur, w, first))
    tn = t * lax.rsqrt(jnp.sum(t * t, axis=-1, keepdims=True) + 1e-6)
    return [jnp.where(j < 2 * GDN_HEADS, tn, t)]


def _gdn_gate_fn(j, first, rv, pv):
    gba, = rv
    alog, dtb, eb, eg = pv
    tb = gba.shape[0]
    beta = _sigmoid(gba)
    g = -jnp.exp(alog) * _softplus(gba + dtb)
    ri, ci = _iota((tb, tb), 0), _iota((tb, tb), 1)
    tril = jnp.where((ri // GDN_CHUNK == ci // GDN_CHUNK) & (ci <= ri), 1.0, 0.0)
    gc = _dot(tril, g, "nn", True)
    return [_dot(beta, eb, "nn", True), _dot(gc, eg, "nn", True)]


def _swa_fn(j, first, rv, pv):
    q, (kp, kc), (vp, vc) = rv
    bp, bc, sk = pv
    sp = _dot(q, kp, "nt") * (SWA_HEAD_DIM ** -0.5) + bp
    sc = _dot(q, kc, "nt") * (SWA_HEAD_DIM ** -0.5) + bc
    qi = _iota(sp.shape, sp.ndim - 2) % SWA_BLOCK
    kj = _iota(sp.shape, sp.ndim - 1)
    sp = jnp.where((kj > qi) & jnp.logical_not(first), sp, NEG_INF)
    sc = jnp.where(kj <= qi, sc, NEG_INF)
    m = jnp.maximum(jnp.maximum(jnp.max(sp, axis=-1, keepdims=True), jnp.max(sc, axis=-1, keepdims=True)), sk)
    m = lax.stop_gradient(m)
    ep, ec, es = jnp.exp(sp - m), jnp.exp(sc - m), jnp.exp(sk - m)
    inv = 1.0 / (jnp.sum(ep, axis=-1, keepdims=True) + jnp.sum(ec, axis=-1, keepdims=True) + es)
    vp = jnp.where(first, 0.0, vp)
    return [_dot(ep * inv, vp, "nn") + _dot(ec * inv, vc, "nn")]


def _memattn_fn(j, first, rv, pv):
    q, = rv
    k, v = pv
    s = _dot(q, k, "nt") * (MEM_HEAD_DIM ** -0.5)
    m = lax.stop_gradient(jnp.max(s, axis=-1, keepdims=True))
    e = jnp.exp(s - m)
    p = e * (1.0 / jnp.sum(e, axis=-1, keepdims=True))
    return [_dot(p, v, "nn")]


SOLVE_PREC = "x3"


def _gdn_heads(q, k, v, bx, gx, g64, z, nw, S):
    c = GDN_CHUNK
    q = q * (GDN_HEAD_DIM ** -0.5)
    kb, vb = k * bx, v * bx
    ri, ci = _iota((1, c, c), 1), _iota((1, c, c), 2)
    tril, strict, eye = ci <= ri, ci < ri, ci == ri
    grow = jnp.sum(jnp.where(eye, g64, 0.0), axis=1, keepdims=True)
    decay = jnp.where(tril, jnp.exp(jnp.where(tril, g64 - grow, 0.0)), 0.0)
    a = jnp.where(strict, _dot(kb, k, "nt") * decay, 0.0)
    tinv = jnp.where(eye, 1.0, 0.0) - a
    x = _dot(a, a, "nn", SOLVE_PREC)
    for i in range(5):
        tinv = tinv + _dot(tinv, x, "nn", SOLVE_PREC)
        if i < 4:
            x = _dot(x, x, "nn", SOLVE_PREC)
    eg = jnp.exp(gx)
    u = _dot(tinv, vb, "nn", SOLVE_PREC)
    w = _dot(tinv, kb * eg, "nn", SOLVE_PREC)
    ai = jnp.where(tril, _dot(q, k, "nt") * decay, 0.0)
    glast = _row_pick(gx, c - 1)
    v_new = u - _dot(w, S, "nn")
    o = _dot(q * eg, S, "nn") + _dot(ai, v_new, "nn")
    s_new = S * jnp.exp(glast) + _dot(k * jnp.exp(glast - gx), v_new, "tn")
    o = o * lax.rsqrt(jnp.mean(o * o, axis=-1, keepdims=True) + 1e-6) * nw
    return o * _silu(z), s_new


def _head_major(ref, off, width=GDN_HEAD_DIM):
    return jnp.stack([ref[:, off + h * GDN_HEAD_DIM:off + h * GDN_HEAD_DIM + width] for h in range(GDN_HEADS)])


def _gdn_chunks_fwd(qkv, bx, gx, proj, nw, rider=None):
    T = qkv.shape[0]
    nc, c, hd, nh = T // GDN_CHUNK, GDN_CHUNK, GDN_HEAD_DIM, GDN_HEADS
    rd = _ride(rider, 5, 2, 1)

    def body(*refs):
        qkv_ref, bx_ref, gx_ref, z_ref, nw_ref = refs[:5]
        y_ref, st_ref = refs[rd.o0:rd.o0 + 2]
        S = refs[rd.s0]
        rd.at_start(refs, pl.program_id(0) == 0)

        @pl.when(pl.program_id(0) == 0)
        def _():
            S[...] = jnp.zeros_like(S)

        s_old = S[...]
        st_ref[...] = s_old
        y, s_new = _gdn_heads(_head_major(qkv_ref, 0), _head_major(qkv_ref, GDN_W), _head_major(qkv_ref, 2 * GDN_W),
                              _head_major(bx_ref, 0), _head_major(gx_ref, 0), _head_major(gx_ref, 0, c), _head_major(z_ref, 0),
                              nw_ref[...], s_old)
        for h in range(nh):
            y_ref[:, h * hd:(h + 1) * hd] = y[h].astype(y_ref.dtype)
        S[...] = s_new
        rd.at_end(refs, pl.program_id(0) == nc - 1)

    row = lambda w, cb: pl.BlockSpec((c, w), lambda n: (n, cb))
    res = pl.pallas_call(
        body, name="gdn_chunks_fwd", grid=(nc,),
        in_specs=[row(3 * GDN_W, 0), row(GDN_W, 0), row(GDN_W, 0), row(GDN_W, P_GZ // GDN_W),
                  pl.BlockSpec((1, hd), lambda n: (0, 0))] + rd.in_specs,
        out_specs=[row(GDN_W, 0), pl.BlockSpec((None, nh, hd, hd), lambda n: (n, 0, 0, 0))] + rd.out_specs,
        out_shape=[jax.ShapeDtypeStruct((T, GDN_W), BF16), jax.ShapeDtypeStruct((nc, nh, hd, hd), F32)] + rd.out_shapes,
        scratch_shapes=[pltpu.VMEM((nh, hd, hd), F32)] + rd.scratch,
        compiler_params=_cparams(("arbitrary",)),
    )(qkv, bx, gx, proj, nw, *rd.ins)
    return res[0], res[1], res[2:]


def _gdn_chunks_bwd(qkv, bx, gx, proj, nw, states, dy, rider=None):
    T = qkv.shape[0]
    nc, c, hd, nh = T // GDN_CHUNK, GDN_CHUNK, GDN_HEAD_DIM, GDN_HEADS
    rd = _ride(rider, 7, 5, 1)

    def body(*refs):
        qkv_ref, bx_ref, gx_ref, z_ref, nw_ref, st_ref, dy_ref = refs[:7]
        dqkv_ref, dbx_ref, dgx_ref, dz_ref, dnw_ref = refs[rd.o0:rd.o0 + 5]
        dS = refs[rd.s0]
        rd.at_start(refs, pl.program_id(0) == 0)

        @pl.when(pl.program_id(0) == 0)
        def _():
            dS[...] = jnp.zeros_like(dS)
            dnw_ref[...] = jnp.zeros_like(dnw_ref)

        args = (_head_major(qkv_ref, 0), _head_major(qkv_ref, GDN_W), _head_major(qkv_ref, 2 * GDN_W), _head_major(bx_ref, 0),
                _head_major(gx_ref, 0), _head_major(gx_ref, 0, c), _head_major(z_ref, 0), nw_ref[...], st_ref[...])
        _, vjp = jax.vjp(_gdn_heads, *args)
        dq, dk, dv, dbx, dgx, dg64, dz, dnw, dsp = vjp((_head_major(dy_ref, 0), dS[...]))
        for h in range(nh):
            sl = slice(h * hd, (h + 1) * hd)
            dqkv_ref[:, sl] = dq[h].astype(dqkv_ref.dtype)
            dqkv_ref[:, GDN_W + h * hd:GDN_W + (h + 1) * hd] = dk[h].astype(dqkv_ref.dtype)
            dqkv_ref[:, 2 * GDN_W + h * hd:2 * GDN_W + (h + 1) * hd] = dv[h].astype(dqkv_ref.dtype)
            dbx_ref[:, sl] = dbx[h]
            dgx_ref[:, sl] = dgx[h]
            dgx_ref[:, h * hd:h * hd + c] += dg64[h]
            dz_ref[:, sl] = dz[h].astype(dz_ref.dtype)
        dnw_ref[...] += dnw
        dS[...] = dsp
        rd.at_end(refs, pl.program_id(0) == nc - 1)

    row = lambda w, cb: pl.BlockSpec((c, w), lambda s: (nc - 1 - s, cb))
    res = pl.pallas_call(
        body, name="gdn_chunks_bwd", grid=(nc,),
        in_specs=[row(3 * GDN_W, 0), row(GDN_W, 0), row(GDN_W, 0), row(GDN_W, P_GZ // GDN_W), pl.BlockSpec((1, hd), lambda s: (0, 0)),
                  pl.BlockSpec((None, nh, hd, hd), lambda s: (nc - 1 - s, 0, 0, 0)), row(GDN_W, 0)] + rd.in_specs,
        out_specs=[row(3 * GDN_W, 0), row(GDN_W, 0), row(GDN_W, 0), row(GDN_W, 0),
                   pl.BlockSpec((1, hd), lambda s: (0, 0))] + rd.out_specs,
        out_shape=[jax.ShapeDtypeStruct((T, 3 * GDN_W), F32), jax.ShapeDtypeStruct((T, GDN_W), F32),
                   jax.ShapeDtypeStruct((T, GDN_W), F32), jax.ShapeDtypeStruct((T, GDN_W), _CDT),
                   jax.ShapeDtypeStruct((1, hd), F32)] + rd.out_shapes,
        scratch_shapes=[pltpu.VMEM((nh, hd, hd), F32)] + rd.scratch,
        compiler_params=_cparams(("arbitrary",)),
    )(qkv, bx, gx, proj, nw, states, dy, *rd.ins)
    res = list(res)
    return res[:5] + [res[5:]]


def _adamw(name, w, g, m, v):
    R, C = w.shape
    tr = _tile(R, 128, SUBLANES)

    def body(w_ref, g_ref, m_ref, v_ref, d_ref, m2_ref, v2_ref):
        g_ = g_ref[...]
        m2 = ADAM_B1 * m_ref[...] + (1.0 - ADAM_B1) * g_
        v2 = ADAM_B2 * v_ref[...] + (1.0 - ADAM_B2) * (g_ * g_)
        m_hat = m2 / (1.0 - ADAM_B1 ** ADAM_STEP)
        v_hat = v2 / (1.0 - ADAM_B2 ** ADAM_STEP)
        d_ref[...] = -ADAM_LR * (m_hat / (jnp.sqrt(v_hat) + ADAM_EPS) + ADAM_WD * w_ref[...])
        m2_ref[...] = m2
        v2_ref[...] = v2

    spec = pl.BlockSpec((tr, C), lambda i: (i, 0))
    return pl.pallas_call(
        body, name=name, grid=(R // tr,), in_specs=[spec] * 4, out_specs=[spec] * 3,
        out_shape=[jax.ShapeDtypeStruct((R, C), F32)] * 3, compiler_params=_cparams(("parallel",)),
    )(w, g, m, v)


def _addn(name, parts, out_dtype=F32):
    parts = [p if isinstance(p, tuple) else (p, None) for p in parts]
    a0, k0 = parts[0]
    R, C = a0.shape[-2:]
    tr = _tile(R, 256, 2 * SUBLANES)
    specs = []
    for a, k in parts:
        if k is None:
            specs.append(pl.BlockSpec((tr, C), lambda i: (i, 0)))
        else:
            specs.append(pl.BlockSpec((None, tr, C), (lambda kk: (lambda i: (kk, i, 0)))(k)))

    def body(*refs):
        acc = refs[0][...].astype(F32)
        for r in refs[1:-1]:
            acc = acc + r[...].astype(F32)
        refs[-1][...] = acc.astype(out_dtype)

    return pl.pallas_call(
        body, name=name, grid=(R // tr,), in_specs=specs, out_specs=pl.BlockSpec((tr, C), lambda i: (i, 0)),
        out_shape=jax.ShapeDtypeStruct((R, C), out_dtype), compiler_params=_cparams(("parallel",)),
    )(*[a for a, _ in parts])


MESH = pl.DeviceIdType.MESH
_HBM = pl.BlockSpec(memory_space=pltpu.HBM)


def _place():
    x, y, c = lax.axis_index("x"), lax.axis_index("y"), lax.axis_index("c")
    return x, y, c, [(1 - x, y), (x, 1 - y), (1 - x, 1 - y)]


class _Rider:
    def __init__(self, ins, out_shapes, nsem, start, finish):
        self.ins, self.out_shapes, self.nsem, self.start, self.finish = list(ins), list(out_shapes), nsem, start, finish

    def sems(self):
        return [pltpu.SemaphoreType.DMA((self.nsem,)), pltpu.SemaphoreType.DMA((self.nsem,))]


def _run_rider(name, rd):
    n_in, n_out = len(rd.ins), len(rd.out_shapes)

    def body(*refs):
        ins, outs, (send, recv) = refs[:n_in], refs[n_in:n_in + n_out], refs[n_in + n_out:]
        rd.start(ins, outs, send, recv)
        rd.finish(ins, outs, send, recv)

    return pl.pallas_call(body, name=name, in_specs=[_HBM] * n_in, out_specs=[_HBM] * n_out, out_shape=rd.out_shapes,
                          scratch_shapes=rd.sems())(*rd.ins)


def _gather_rider(ts):
    nt = len(ts)

    def half(t, hc):
        rh = ts[t].shape[0] // 2
        return pl.ds(pl.multiple_of(hc * rh, 16), rh)

    def rcopy(send, recv, t, k, src, dst, to):
        return pltpu.make_async_remote_copy(src_ref=src, dst_ref=dst, send_sem=send.at[6 * t + k], recv_sem=recv.at[6 * t + k],
                                            device_id=to, device_id_type=MESH)

    def first_hop(ins, outs, send, recv, t, r, px, py, c, me):
        return rcopy(send, recv, t, r, ins[t].at[half(t, c)], outs[t].at[me, half(t, c)], (px, py, c))

    def start(ins, outs, send, recv):
        x, y, c, rel = _place()
        for t in range(nt):
            for r, (px, py) in enumerate(rel):
                first_hop(ins, outs, send, recv, t, r, px, py, c, 2 * x + y).start()

    def finish(ins, outs, send, recv):
        x, y, c, rel = _place()
        sib = (x, y, 1 - c)
        passed = []
        for t in range(nt):
            for r, (px, py) in enumerate(rel):
                got = outs[t].at[2 * px + py, half(t, c)]
                rcopy(send, recv, t, r, got, got, (px, py, c)).wait_recv()
                fw = rcopy(send, recv, t, 3 + r, got, got, sib)
                fw.start()
                passed.append(fw)
        for t in range(nt):
            for r, (px, py) in enumerate(rel):
                got = outs[t].at[2 * px + py, half(t, 1 - c)]
                rcopy(send, recv, t, 3 + r, got, got, sib).wait_recv()
        for t in range(nt):
            for r, (px, py) in enumerate(rel):
                first_hop(ins, outs, send, recv, t, r, px, py, c, 2 * x + y).wait_send()
        for fw in passed:
            fw.wait_send()

    return _Rider(ts, [jax.ShapeDtypeStruct((4,) + tuple(t.shape), t.dtype) for t in ts], 6 * nt, start, finish)


def _scatter_rider(ps):
    nt = len(ps)

    def copy(ins, outs, send, recv, t, r, px, py, c):
        return pltpu.make_async_remote_copy(src_ref=ins[t].at[2 * px + py], dst_ref=outs[t].at[r], send_sem=send.at[3 * t + r],
                                            recv_sem=recv.at[3 * t + r], device_id=(px, py, c), device_id_type=MESH)

    def start(ins, outs, send, recv):
        x, y, c, rel = _place()
        for t in range(nt):
            for r, (px, py) in enumerate(rel):
                copy(ins, outs, send, recv, t, r, px, py, c).start()

    def finish(ins, outs, send, recv):
        x, y, c, rel = _place()
        for t in range(nt):
            for r, (px, py) in enumerate(rel):
                copy(ins, outs, send, recv, t, r, px, py, c).wait()

    return _Rider(ps, [jax.ShapeDtypeStruct((3,) + tuple(p.shape[1:]), p.dtype) for p in ps], 3 * nt, start, finish)


def _pair_swap(tag, ts):
    nt = len(ts)

    def body(*refs):
        ins, outs = refs[:nt], refs[nt:2 * nt]
        send, recv = refs[2 * nt:]
        x, y, c, _ = _place()
        cps = []
        for t in range(nt):
            rh = ts[t].shape[1] // 2
            src = ins[t].at[:, pl.ds(pl.multiple_of((1 - c) * rh, 16), rh), :]
            cp = pltpu.make_async_remote_copy(src_ref=src, dst_ref=outs[t], send_sem=send.at[t], recv_sem=recv.at[t],
                                              device_id=(x, y, 1 - c), device_id_type=MESH)
            cp.start()
            cps.append(cp)
        for cp in cps:
            cp.wait()

    return pl.pallas_call(
        body, name="pair_swap_" + tag, in_specs=[_HBM] * nt, out_specs=[_HBM] * nt,
        out_shape=[jax.ShapeDtypeStruct((4, t.shape[1] // 2, t.shape[2]), t.dtype) for t in ts],
        scratch_shapes=[pltpu.SemaphoreType.DMA((nt,)), pltpu.SemaphoreType.DMA((nt,))],
    )(*ts)


def _pair_exchange(gs):
    nt = len(gs)

    def body(*refs):
        ins, outs = refs[:nt], refs[nt:2 * nt]
        send, recv = refs[2 * nt:]
        x, y, c, _ = _place()
        cps = []
        for t in range(nt):
            cp = pltpu.make_async_remote_copy(src_ref=ins[t], dst_ref=outs[t], send_sem=send.at[t], recv_sem=recv.at[t],
                                              device_id=(x, y, 1 - c), device_id_type=MESH)
            cp.start()
            cps.append(cp)
        for cp in cps:
            cp.wait()

    return pl.pallas_call(
        body, name="pair_exchange", in_specs=[_HBM] * nt, out_specs=[_HBM] * nt,
        out_shape=[jax.ShapeDtypeStruct(tuple(g.shape), g.dtype) for g in gs],
        scratch_shapes=[pltpu.SemaphoreType.DMA((nt,)), pltpu.SemaphoreType.DMA((nt,))],
    )(*gs)


def _allgather8(v):
    m, n = v.shape

    def body(x_ref, out_ref, send, recv, lsem):
        x, y, c, rel = _place()
        me, sib = (x, y, c), (x, y, 1 - c)

        def blk(px, py, pc):
            return out_ref.at[4 * px + 2 * py + pc]

        def copy(k, block, to, src=None):
            return pltpu.make_async_remote_copy(src_ref=blk(*block) if src is None else src, dst_ref=blk(*block), send_sem=send.at[k],
                                                recv_sem=recv.at[k], device_id=to, device_id_type=MESH)

        mine = pltpu.make_async_copy(x_ref, blk(*me), lsem)
        mine.start()
        first = [copy(0, me, sib, src=x_ref)] + [copy(1 + r, me, (*ch, c), src=x_ref) for r, ch in enumerate(rel)]
        for cp in first:
            cp.start()
        passed = [copy(4 + r, (*ch, c), sib) for r, ch in enumerate(rel)]
        for r, ch in enumerate(rel):
            copy(1 + r, (*ch, c), me).wait_recv()
            passed[r].start()
        copy(0, sib, me).wait_recv()
        for r, ch in enumerate(rel):
            copy(4 + r, (*ch, 1 - c), me).wait_recv()
        for cp in first + passed:
            cp.wait_send()
        mine.wait()

    return pl.pallas_call(
        body, name="allgather8", in_specs=[pl.BlockSpec(memory_space=pltpu.VMEM)], out_specs=pl.BlockSpec(memory_space=pltpu.VMEM),
        out_shape=jax.ShapeDtypeStruct((8, m, n), v.dtype),
        scratch_shapes=[pltpu.SemaphoreType.DMA((7,)), pltpu.SemaphoreType.DMA((7,)), pltpu.SemaphoreType.DMA],
    )(v)


def _t5_bucket(dist):
    max_exact = REL_BUCKETS // 2
    d = jnp.maximum(dist, 1).astype(F32)
    large = max_exact + (jnp.log(d / max_exact) / math.log(REL_MAX_DIST / max_exact) * (REL_BUCKETS - max_exact)).astype(jnp.int32)
    large = jnp.minimum(large, REL_BUCKETS - 1)
    return jnp.where(dist < max_exact, dist, large)


def _bias_onehot():
    qi = jnp.arange(SWA_BLOCK)[:, None]
    kj = jnp.arange(SWA_BLOCK)[None, :]
    dist = jnp.concatenate([(qi + SWA_BLOCK - kj).reshape(-1), (qi - kj).reshape(-1)])
    bucket = _t5_bucket(jnp.maximum(dist, 0))
    return (bucket[None, :] == jnp.arange(REL_BUCKETS)[:, None]).astype(F32)


def _head_spread():
    lane = jnp.arange(LANES)[:, None]
    head = jnp.arange(GDN_W)[None, :] // GDN_HEAD_DIM
    return (lane == head).astype(F32), (lane == head + GDN_HEADS).astype(F32)


def _lane16(v8):
    return jnp.pad(v8.astype(F32), (GDN_HEADS, LANES - 2 * GDN_HEADS)).reshape(1, LANES)


def _stack_heads(t, nb):
    return t.reshape(nb, SWA_BLOCK, SWA_KV_HEADS, SWA_GRP, SWA_HEAD_DIM).transpose(2, 0, 3, 1, 4).reshape(
        SWA_KV_HEADS, nb * SWA_GRP * SWA_BLOCK, SWA_HEAD_DIM)


def _unstack_heads(t, nb):
    return t.reshape(SWA_KV_HEADS, nb, SWA_GRP, SWA_BLOCK, SWA_HEAD_DIM).transpose(1, 3, 0, 2, 4).reshape(nb * SWA_BLOCK, SWA_Q)


def _kv_heads(t):
    return t.reshape(t.shape[0], SWA_KV_HEADS, SWA_HEAD_DIM).transpose(1, 0, 2)


def _swa_specs(qs, ks, vs, bp, bc, sk, grad, gdt=(F32,)):
    T = ks.shape[1]
    qr = SWA_GRP * SWA_BLOCK
    g = lambda a: tuple(a.shape) if grad else None
    nk = SWA_KV_HEADS
    m3 = lambda j, n: (0, n, 0)
    h3 = lambda j, n: (0, jnp.maximum(n - 1, 0), 0)
    p3 = lambda j: (0, 0, 0)
    rows = [Row(qs, (nk, qr, SWA_HEAD_DIM), m3, gshape=g(qs), gmap=m3, gdt=gdt),
            Row(ks, (nk, SWA_BLOCK, SWA_HEAD_DIM), m3, (nk, SWA_BLOCK, SWA_HEAD_DIM), h3, g(ks), m3, gdt),
            Row(vs, (nk, SWA_BLOCK, SWA_HEAD_DIM), m3, (nk, SWA_BLOCK, SWA_HEAD_DIM), h3, g(vs), m3, gdt)]
    pars = [Par(bp, (nk, qr, SWA_BLOCK), p3, g(bp), p3), Par(bc, (nk, qr, SWA_BLOCK), p3, g(bc), p3),
            Par(sk, (nk, qr, 1), p3, g(sk), p3)]
    return rows, pars, T // SWA_BLOCK


class _LocalWeights:
    def __init__(self, W):
        self.W = W

    def w1(self):
        return self.W

    def rider_a(self):
        return None

    def w2(self, got):
        return self.W

    def rider_b(self):
        return None

    def w3(self, got):
        return self.W

    def rider_g(self, G):
        return None

    def g_done(self, got):
        pass

    def rider_last(self, G):
        return None

    def last_done(self, got):
        pass


def _fwd_bwd(x, mem, tgt, src):
    W = dict(src.w1())
    T = x.shape[0]
    nb = T // SWA_BLOCK
    tb = min(256, T)
    tbl = min(512, T)
    fwd = lambda f: (lambda *a: (f(*a), []))
    full = lambda cols, dt, t, cw: Out((T, cols), dt, (t, cw), lambda j, n: (n, j))

    xb = x.astype(_CDT)
    ra = src.rider_a()
    proj = _mm("proj", xb, W["in_p"], "nn", rider=ra)
    proj, got = proj if ra is not None else (proj, None)
    W.update(src.w2(got))

    onehot_t = _bias_onehot()
    bias_flat = _mm("swa_bias", W["rel_bias"].T, onehot_t, "nn", hi=True)
    half = SWA_BLOCK * SWA_BLOCK
    bp = bias_flat[:, :half].reshape(SWA_KV_HEADS, SWA_GRP * SWA_BLOCK, SWA_BLOCK)
    bc = bias_flat[:, half:].reshape(SWA_KV_HEADS, SWA_GRP * SWA_BLOCK, SWA_BLOCK)
    sk = jnp.broadcast_to(W["swa_sinks"].reshape(SWA_KV_HEADS, SWA_GRP, 1, 1), (SWA_KV_HEADS, SWA_GRP, SWA_BLOCK, 1)).reshape(
        SWA_KV_HEADS, SWA_GRP * SWA_BLOCK, 1)
    qs = _stack_heads(proj[:, P_SQ:P_SQ + SWA_Q], nb)
    ks = _kv_heads(proj[:, P_SK:P_SK + SWA_KV])
    vs = _kv_heads(proj[:, P_SV:P_SV + SWA_KV])
    rows, pars, nblk = _swa_specs(qs, ks, vs, bp, bc, sk, False)
    o_s, = _rowmap("swa_fwd", fwd(_swa_fn), 1, nblk, rows, pars,
                   [Out(tuple(qs.shape), F32, (SWA_KV_HEADS, SWA_GRP * SWA_BLOCK, SWA_HEAD_DIM), lambda j, n: (0, n, 0))])
    o_swa = _unstack_heads(o_s, nb).astype(_CDT)

    ncq = 3 * GDN_W // LANES
    tbp = min(1024, T)
    pre_rows = lambda grad: [_rowspec(proj, tbp, LANES, P_GQKV // LANES, halo=SUBLANES, grad=grad, ncol=ncq, gdt=(_CDT,))]
    pre_pars = lambda grad: [_parspec(W["gdn_conv_w"], LANES, 0, grad=grad, ncol=ncq)]
    qkv_n, = _rowmap("gdn_pre_fwd", fwd(_gdn_pre_fn), ncq, T // tbp, pre_rows(False), pre_pars(False),
                     [full(3 * GDN_W, F32, tbp, LANES)])
    eb, eg = _head_spread()
    alog_row, dtb_row = _lane16(W["gdn_a_log"]), _lane16(W["gdn_dt_bias"])
    gate_rows = lambda grad: [_rowspec(proj, tbl, LANES, P_BA // LANES, cstep=0, grad=grad, gdt=(_CDT,))]
    gate_pars = lambda grad: [_parspec(alog_row, grad=grad), _parspec(dtb_row, grad=grad), _parspec(eb), _parspec(eg)]
    bx, gx = _rowmap("gdn_gate_fwd", fwd(_gdn_gate_fn), 1, T // tbl, gate_rows(False), gate_pars(False),
                     [full(GDN_W, F32, tbl, GDN_W), full(GDN_W, F32, tbl, GDN_W)])
    nw = W["gdn_norm_w"].reshape(1, GDN_HEAD_DIM)
    o_gdn, states, got = _gdn_chunks_fwd(qkv_n, bx, gx, proj, nw, rider=src.rider_b())
    W.update(src.w3(got))

    ys = _mm("y_swa", o_swa, W["br_swa"], "nn")
    yg = _mm("y_gdn", o_gdn, W["br_gdn"], "nn")
    cwm = 512
    mix_rows = lambda grad: [_rowspec(proj, tb, cwm, P_GS // cwm, grad=grad, ncol=D_MODEL // cwm, gdt=(_CDT,)),
                             _rowspec(proj, tb, cwm, P_GG // cwm, grad=grad, ncol=D_MODEL // cwm, gdt=(_CDT,)),
                             _rowspec(ys, tb, cwm, 0, grad=grad, ncol=D_MODEL // cwm, gdt=(_CDT,)),
                             _rowspec(yg, tb, cwm, 0, grad=grad, ncol=D_MODEL // cwm, gdt=(_CDT,))]
    mixed, = _rowmap("mix_fwd", fwd(_mix_fn), D_MODEL // cwm, T // tb, mix_rows(False), [], [full(D_MODEL, _CDT, tb, cwm)])
    r1 = _mm("r1", mixed, W["mix_o"], "nn", add=x, add_scale=ALPHA)

    def ln_fwd(name, r, g, b):
        return _rowmap(name, _ln_fwd_fn, 1, T // tb, [_rowspec(r, tb, D_MODEL, 0)], [_parspec(g), _parspec(b)],
                       [full(D_MODEL, F32, tb, D_MODEL), full(D_MODEL, _CDT, tb, D_MODEL)])

    def ln_bwd(name, r, g, b, ct):
        return _rowmap_bwd(name, _ln_fn, 1, T // tb, [_rowspec(r, tb, D_MODEL, 0, grad=True, gdt=(F32, _CDT))],
                           [_parspec(g, grad=True), _parspec(b, grad=True)], [_rowspec(ct, tb, D_MODEL, 0)])

    g1, b1 = W["ln1_g"].reshape(1, -1), W["ln1_b"].reshape(1, -1)
    g2, b2 = W["ln2_g"].reshape(1, -1), W["ln2_b"].reshape(1, -1)
    g3, b3 = W["ln3_g"].reshape(1, -1), W["ln3_b"].reshape(1, -1)
    x1, x1b = ln_fwd("ln1_fwd", r1, g1, b1)

    qm = _mm("mem_q", x1b, W["mem_q"], "nn")
    kvm = _mm("mem_kv", mem, W["mem_kv"], "nn")
    ma_rows = lambda grad: [_rowspec(qm, tbl, MEM_HEAD_DIM, 0, grad=grad, ncol=MEM_HEADS, gdt=(_CDT,))]
    ma_pars = lambda grad: [_parspec(kvm, MEM_HEAD_DIM, 0, grad=grad, ncol=MEM_HEADS),
                            _parspec(kvm, MEM_HEAD_DIM, MEM_HEADS, grad=grad, ncol=MEM_HEADS)]
    om, = _rowmap("memattn_fwd", fwd(_memattn_fn), MEM_HEADS, T // tbl, ma_rows(False), ma_pars(False),
                  [full(MEM_W, _CDT, tbl, MEM_HEAD_DIM)])
    r2 = _mm("r2", om, W["mem_o"], "nn", add=x1, add_scale=ALPHA)
    x2, x2b = ln_fwd("ln2_fwd", r2, g2, b2)

    hcat = _mm("ffn_up", x2b, W["up_p"], "nn")
    cwf = 512
    ncf = D_FF_PAD // cwf
    cw_p, cb_p = W["ffn_conv_w_p"], W["ffn_conv_b_p"]
    tbf = min(512, T)
    ffn_rows = lambda grad: [_rowspec(hcat, tbf, cwf, 0, halo=SUBLANES, grad=grad, ncol=ncf, gdt=(_CDT,)),
                             _rowspec(hcat, tbf, cwf, ncf, halo=SUBLANES, grad=grad, ncol=ncf, gdt=(_CDT,))]
    ffn_pars = lambda grad: [_parspec(cw_p, cwf, 0, grad=grad, ncol=ncf), _parspec(cw_p, cwf, ncf, grad=grad, ncol=ncf),
                             _parspec(cb_p, cwf, 0, grad=grad, ncol=ncf), _parspec(cb_p, cwf, ncf, grad=grad, ncol=ncf)]
    act, = _rowmap("ffn_act_fwd", fwd(_ffn_act_fn), ncf, T // tbf, ffn_rows(False), ffn_pars(False), [full(D_FF_PAD, _CDT, tbf, cwf)])
    r3 = _mm("r3", act, W["down_p"], "nn", add=x2, add_scale=ALPHA)
    dr3, dr3b, lacc, dg3, db3 = _rowmap("ln3_loss", _loss_fn, 1, T // tb, [_rowspec(r3, tb, D_MODEL, 0), _rowspec(tgt, tb, D_MODEL, 0)],
                                        [_parspec(g3), _parspec(b3)], [full(D_MODEL, F32, tb, D_MODEL), full(D_MODEL, _CDT, tb, D_MODEL)],
                                  accs=[(SUBLANES, LANES), (1, D_MODEL), (1, D_MODEL)])
    loss = lacc[0, 0]

    G = {}
    G["down_p"] = _mm("dw_down", act, dr3b, "tn", out_dtype=_GDT)
    dact = _mm("d_act", dr3b, W["down_p"], "nt")
    dhg, dhu, dcwg, dcwu, dcbg, dcbu = _rowmap_bwd("ffn_act_bwd", _ffn_act_fn, ncf, T // tbf, ffn_rows(True), ffn_pars(True),
                                                   [_rowspec(dact, tbf, cwf, 0)])
    w_gate, w_upp = W["up_p"][:, :D_FF_PAD], W["up_p"][:, D_FF_PAD:]
    dx2 = _mm("dx2_gate", dhg, w_gate, "nt", add=dr3, add_scale=ALPHA)
    dx2 = _mm("dx2_up", dhu, w_upp, "nt", add=dx2)
    G["up_p"] = jnp.concatenate([_mm("dw_gate", x2b, dhg, "tn", out_dtype=_GDT), _mm("dw_up", x2b, dhu, "tn", out_dtype=_GDT)], axis=1)
    G["ffn_conv_w"] = jnp.concatenate([dcwg[:, :D_FF], dcwu[:, :D_FF]], axis=1)
    G["ffn_conv_b"] = jnp.concatenate([dcbg[0, :D_FF], dcbu[0, :D_FF]])
    G["ln3_g"], G["ln3_b"] = dg3[0], db3[0]

    dr2, dr2b, dg2, db2 = ln_bwd("ln2_bwd", r2, g2, b2, dx2)
    G["ln2_g"], G["ln2_b"] = dg2[0], db2[0]
    G["mem_o"] = _mm("dw_mem_o", om, dr2b, "tn", out_dtype=_GDT)
    dom = _mm("d_om", dr2b, W["mem_o"], "nt", out_dtype=_CDT)
    dqm, dkm, dvm = _rowmap_bwd("memattn_bwd", _memattn_fn, MEM_HEADS, T // tbl, ma_rows(True), ma_pars(True),
                                [_rowspec(dom, tbl, MEM_HEAD_DIM, 0)])
    G["mem_kv"] = _mm("dw_mem_kv", mem.astype(_CDT), jnp.concatenate([dkm, dvm], axis=1).astype(_CDT), "tn", out_dtype=_GDT)
    G["mem_q"] = _mm("dw_mem_q", x1b, dqm, "tn", out_dtype=_GDT)
    dx1 = _mm("dx1", dqm, W["mem_q"], "nt", add=dr2, add_scale=ALPHA)

    dr1, dr1b, dg1, db1 = ln_bwd("ln1_bwd", r1, g1, b1, dx1)
    G["ln1_g"], G["ln1_b"] = dg1[0], db1[0]
    G["mix_o"] = _mm("dw_mix_o", mixed, dr1b, "tn", out_dtype=_GDT)
    dmixed = _mm("d_mixed", dr1b, W["mix_o"], "nt")
    dgs, dgg, dys, dyg = _rowmap_bwd("mix_bwd", _mix_fn, D_MODEL // cwm, T // tb, mix_rows(True), [], [_rowspec(dmixed, tb, cwm, 0)])
    G["br_swa"] = _mm("dw_br_swa", o_swa, dys, "tn", out_dtype=_GDT)
    G["br_gdn"] = _mm("dw_br_gdn", o_gdn, dyg, "tn", out_dtype=_GDT)
    do_swa = _mm("d_o_swa", dys, W["br_swa"], "nt", out_dtype=_CDT)
    do_gdn = _mm("d_o_gdn", dyg, W["br_gdn"], "nt")

    rows, pars, nblk = _swa_specs(qs, ks, vs, bp, bc, sk, True, (_CDT,))
    m3 = lambda j, n: (0, n, 0)
    dqs, dks, dvs, dbp, dbc, dsk = _rowmap_bwd("swa_bwd", _swa_fn, 1, nblk, rows, pars,
                                               [Row(_stack_heads(do_swa, nb), (SWA_KV_HEADS, SWA_GRP * SWA_BLOCK, SWA_HEAD_DIM), m3)])
    d_swa = jnp.concatenate([_unstack_heads(dqs, nb), dks.transpose(1, 0, 2).reshape(T, SWA_KV),
                             dvs.transpose(1, 0, 2).reshape(T, SWA_KV)], axis=1)
    dbias = jnp.concatenate([dbp.reshape(SWA_HEADS, half), dbc.reshape(SWA_HEADS, half)], axis=1)
    G["rel_bias"] = _mm("d_rel_bias", dbias, onehot_t.T, "nn", hi=True).T
    G["swa_sinks"] = _mm("d_sinks", dsk.reshape(SWA_HEADS, SWA_BLOCK), jnp.ones((SWA_BLOCK, LANES), F32), "nn", hi=True)[:, 0]

    dqkv_n, dbx, dgx, dz, dnw, got = _gdn_chunks_bwd(qkv_n, bx, gx, proj, nw, states, do_gdn, rider=src.rider_g(G))
    src.g_done(got)
    G["gdn_norm_w"] = dnw[0]
    dgba, dalog, ddtb = _rowmap_bwd("gdn_gate_bwd", _gdn_gate_fn, 1, T // tbl, gate_rows(True), gate_pars(True),
                                    [_rowspec(dbx, tbl, GDN_W, 0), _rowspec(dgx, tbl, GDN_W, 0)])
    G["gdn_a_log"], G["gdn_dt_bias"] = dalog[0, GDN_HEADS:2 * GDN_HEADS], ddtb[0, GDN_HEADS:2 * GDN_HEADS]
    dgqkv, dcw_gdn = _rowmap_bwd("gdn_pre_bwd", _gdn_pre_fn, ncq, T // tbp, pre_rows(True), pre_pars(True),
                                 [_rowspec(dqkv_n, tbp, LANES, 0)])
    G["gdn_conv_w"] = dcw_gdn

    dproj = jnp.concatenate([dgs, dgg, dgqkv, dz, d_swa, dgba, jnp.zeros((T, P_END - P_USED), _CDT)], axis=1)
    G["in_p"] = _mm("dw_in", xb, dproj, "tn", out_dtype=_GDT)
    rl = src.rider_last(G)
    dx = _mm("dx", dproj, W["in_p"], "nt", add=dr1, add_scale=ALPHA, rider=rl)
    if rl is not None:
        dx, got = dx
        src.last_done(got)
    return loss, dx, G


W_NAMES = ["w_in", "rel_bias", "swa_sinks", "gdn_conv_w", "gdn_a_log", "gdn_dt_bias", "gdn_norm_w", "w_br_swa", "w_br_gdn",
           "w_mix_o", "ln1_g", "ln1_b", "w_mem_q", "w_mem_kv", "w_mem_o", "ln2_g", "ln2_b", "w_up", "ffn_conv_w", "ffn_conv_b",
           "w_down", "ln3_g", "ln3_b"]
BIG = ["w_in", "w_br_swa", "w_br_gdn", "w_mix_o", "w_mem_q", "w_mem_kv", "w_mem_o", "w_up", "w_down"]
SMALL = [n for n in W_NAMES if n not in BIG]
COL_SHARDED = ["w_in", "w_br_swa", "w_br_gdn", "w_mem_o", "w_up"]


def _pack(arrs):
    rows = []
    for a in arrs:
        f = a.reshape(-1).astype(F32)
        rows.append(jnp.pad(f, (0, (-f.shape[0]) % LANES)).reshape(-1, LANES))
    n = sum(r.shape[0] for r in rows)
    if n % 16:
        rows.append(jnp.zeros((16 - n % 16, LANES), F32))
    return jnp.concatenate(rows, axis=0)


def _unpack(p, shapes):
    out, off = [], 0
    for s in shapes:
        n = int(np.prod(s)) if len(s) else 1
        r = -(-n // LANES)
        out.append(p[off:off + r].reshape(-1)[:n].reshape(s))
        off += r
    return out


def _merge_shards(d):
    cat = lambda names: jnp.concatenate([d[n] for n in names], axis=-2)
    return [d.get("w_in"), d["w_up"], cat(["w_br_swa", "w_br_gdn", "w_mem_q", "w_mem_o"]), cat(["w_mix_o", "w_down"]), d["w_mem_kv"]]


def _split_shards(ts):
    a, b, c, dd, e = ts
    return {"w_in": a, "w_up": b, "w_br_swa": c[..., 0:1024, :], "w_br_gdn": c[..., 1024:2048, :], "w_mem_q": c[..., 2048:2560, :],
            "w_mem_o": c[..., 2560:3072, :], "w_mix_o": dd[..., 0:512, :], "w_down": dd[..., 512:, :], "w_mem_kv": e}


def _to_full(name, t):
    if name in COL_SHARDED:
        return _cols_from_chips(t, [(0, 4 * t.shape[2])])
    return t.reshape(4 * t.shape[1], t.shape[2])


def _to_chips(name, t):
    if name in COL_SHARDED:
        return _chips_from_cols(t, [(0, t.shape[1])], t.shape[1] // 4)
    return t.reshape(4, t.shape[0] // 4, t.shape[1])


def _cols_from_chips(g, segs):
    C, parts = g.shape[2], []
    for s in segs:
        if isinstance(s, int):
            parts.append(jnp.zeros((g.shape[1], s), g.dtype))
            continue
        lo, hi = s
        while lo < hi:
            k = lo // C
            e = min(hi, (k + 1) * C)
            parts.append(g[k][:, lo - k * C:e - k * C])
            lo = e
    return jnp.concatenate(parts, axis=1)


def _chips_from_cols(p, segs, C):
    out = []
    for k in range(4):
        lo, hi, parts, o = k * C, (k + 1) * C, [], 0
        for plo, w in segs:
            a, b = max(lo, o), min(hi, o + w)
            if a < b:
                parts.append(p[:, plo + a - o:plo + b - o])
            o += w
        out.append(jnp.concatenate(parts, axis=1))
    return jnp.stack(out)


_IN_OFF = np.cumsum((0,) + IN_WIDTHS)
_IN_SEGS = [(P_SQ, SWA_Q), (P_SK, SWA_KV), (P_SV, SWA_KV), (P_GQKV, 3 * GDN_W), (P_GZ, GDN_W), (P_BA, 2 * GDN_HEADS),
            (P_GS, D_MODEL), (P_GG, D_MODEL)]
_IN_PADDED = [(int(_IN_OFF[i]), int(_IN_OFF[k])) for i, k in ((9, 10), (10, 11), (3, 6), (6, 7), (0, 1), (1, 2), (2, 3), (7, 9))] + [
    P_END - P_BA - 2 * GDN_HEADS]
_UP_SEGS = [(0, D_FF), (D_FF_PAD, D_FF)]
_UP_PADDED = [(0, D_FF), D_FF_PAD - D_FF, (D_FF, 2 * D_FF), D_FF_PAD - D_FF]


def _in_to_padded(w):
    o = _IN_OFF
    cut = lambda i, k: w[:, o[i]:o[k]]
    return jnp.concatenate([cut(9, 10), cut(10, 11), cut(3, 6), cut(6, 7), cut(0, 1), cut(1, 2), cut(2, 3), cut(7, 9),
                            jnp.zeros((w.shape[0], P_END - P_BA - 2 * GDN_HEADS), w.dtype)], axis=1)


def _in_from_padded(p):
    return jnp.concatenate([p[:, P_SQ:P_SQ + SWA_Q], p[:, P_SK:P_SK + SWA_KV], p[:, P_SV:P_SV + SWA_KV], p[:, P_GQKV:P_GQKV + 3 * GDN_W],
                            p[:, P_GZ:P_GZ + GDN_W], p[:, P_BA:P_BA + 2 * GDN_HEADS], p[:, P_GS:P_GS + D_MODEL], p[:, P_GG:P_GG + D_MODEL]],
                           axis=1)


def _ff_pad(t, axis):
    g, u = jnp.split(t, 2, axis=axis)
    pad = [(0, 0)] * t.ndim
    pad[axis] = (0, D_FF_PAD - D_FF)
    return jnp.concatenate([jnp.pad(g, pad), jnp.pad(u, pad)], axis=axis)


def _ff_unpad(t, axis):
    g, u = jnp.split(t, 2, axis=axis)
    return jnp.concatenate([lax.slice_in_dim(g, 0, D_FF, axis=axis), lax.slice_in_dim(u, 0, D_FF, axis=axis)], axis=axis)


def _assemble_weights(full, small):
    W = dict(small)
    W["in_p"] = _in_to_padded(full["w_in"])
    W["up_p"] = _ff_pad(full["w_up"], 1)
    W["down_p"] = jnp.pad(full["w_down"], ((0, D_FF_PAD - D_FF), (0, 0)))
    W["br_swa"], W["br_gdn"], W["mix_o"] = full["w_br_swa"], full["w_br_gdn"], full["w_mix_o"]
    W["mem_q"], W["mem_kv"], W["mem_o"] = full["w_mem_q"], full["w_mem_kv"], full["w_mem_o"]
    W["ffn_conv_w_p"] = _ff_pad(small["ffn_conv_w"], 1)
    W["ffn_conv_b_p"] = _ff_pad(small["ffn_conv_b"].reshape(1, -1), 1)
    return W


def _full_grads(G):
    out = {"w_in": _in_from_padded(G["in_p"])} if "in_p" in G else {}
    out.update({"w_up": _ff_unpad(G["up_p"], 1), "w_down": G["down_p"][:D_FF], "w_br_swa": G["br_swa"], "w_br_gdn": G["br_gdn"],
                "w_mix_o": G["mix_o"], "w_mem_q": G["mem_q"], "w_mem_kv": G["mem_kv"], "w_mem_o": G["mem_o"]})
    return out


def kernel(x, mem, w_in, rel_bias, swa_sinks, gdn_conv_w, gdn_a_log, gdn_dt_bias, gdn_norm_w, w_br_swa, w_br_gdn, w_mix_o, ln1_g, ln1_b, w_mem_q, w_mem_kv, w_mem_o, ln2_g, ln2_b, w_up, ffn_conv_w, ffn_conv_b, w_down, ln3_g, ln3_b, loss_target, m_w_in, m_rel_bias, m_swa_sinks, m_gdn_conv_w, m_gdn_a_log, m_gdn_dt_bias, m_gdn_norm_w, m_w_br_swa, m_w_br_gdn, m_w_mix_o, m_ln1_g, m_ln1_b, m_w_mem_q, m_w_mem_kv, m_w_mem_o, m_ln2_g, m_ln2_b, m_w_up, m_ffn_conv_w, m_ffn_conv_b, m_w_down, m_ln3_g, m_ln3_b, v_w_in, v_rel_bias, v_swa_sinks, v_gdn_conv_w, v_gdn_a_log, v_gdn_dt_bias, v_gdn_norm_w, v_w_br_swa, v_w_br_gdn, v_w_mix_o, v_ln1_g, v_ln1_b, v_w_mem_q, v_w_mem_kv, v_w_mem_o, v_ln2_g, v_ln2_b, v_w_up, v_ffn_conv_w, v_ffn_conv_b, v_w_down, v_ln3_g, v_ln3_b):
    a = dict(locals())
    w = {n: a[n] for n in W_NAMES}
    m = {n: a["m_" + n] for n in W_NAMES}
    v = {n: a["v_" + n] for n in W_NAMES}
    chip = 2 * lax.axis_index("x") + lax.axis_index("y")
    core = lax.axis_index("c")
    sq = lambda t: t.reshape(t.shape[1:]) if (t.ndim > 1 and t.shape[0] == 1 and t is not rel_bias) else t

    sh_a, sh_b, sh_c, sh_d, sh_e = _merge_shards({n: sq(w[n]).astype(_CDT) for n in BIG})
    fcw_sh, gcw_sh = sq(ffn_conv_w).shape, sq(gdn_conv_w).shape
    slot = lax.broadcasted_iota(jnp.int32, (4, 1, 1), 0)

    def with_own(got, mine):
        return [jnp.where(slot == chip, t[None], g) for g, t in zip(got, mine)]

    def reduce_start(tag, gch):
        pair = []
        for t, (mine, got) in enumerate(zip(gch, _pair_swap(tag, gch))):
            rh = mine.shape[1] // 2
            mine_h = lax.dynamic_slice_in_dim(mine, core * rh, rh, axis=1)
            pair.append(_addn(f"pair_sum_{tag}{t}", [mine_h.reshape(4 * rh, -1), got.reshape(4 * rh, -1)], _GDT).reshape(4, rh, -1))
        return pair

    def reduce_end(tag, pair, others):
        halves = []
        for t, (p, o) in enumerate(zip(pair, others)):
            own = lax.dynamic_index_in_dim(p, chip, 0, keepdims=False)
            halves.append(_addn(f"chip_sum_{tag}{t}", [own, (o, 0), (o, 1), (o, 2)]))
        return halves

    class MeshWeights:
        def w1(self):
            mine = [sh_a, _pack([sq(ffn_conv_w), sq(gdn_conv_w)])]
            got_a, got_f = with_own(_run_rider("gather_first", _gather_rider(mine)), mine)
            conv = [_unpack(got_f[k], [fcw_sh, gcw_sh]) for k in range(4)]
            W = {n: sq(w[n]) for n in SMALL}
            W["ffn_conv_w"] = jnp.concatenate([cv[0] for cv in conv], axis=1)
            W["gdn_conv_w"] = jnp.concatenate([cv[1] for cv in conv], axis=1)
            W["ffn_conv_w_p"] = _ff_pad(W["ffn_conv_w"], 1)
            W["ffn_conv_b_p"] = _ff_pad(W["ffn_conv_b"].reshape(1, -1), 1)
            W["in_p"] = _cols_from_chips(got_a, _IN_PADDED)
            return W

        def rider_a(self):
            return _gather_rider([sh_c, sh_d, sh_e])

        def w2(self, got):
            c, d, e = with_own(got, [sh_c, sh_d, sh_e])
            f = {n: _to_full(n, t) for n, t in _split_shards([None, None, c, d, e]).items() if t is not None}
            return {"br_swa": f["w_br_swa"], "br_gdn": f["w_br_gdn"], "mix_o": f["w_mix_o"], "mem_q": f["w_mem_q"], "mem_kv": f["w_mem_kv"],
                    "mem_o": f["w_mem_o"], "down_p": jnp.pad(f["w_down"], ((0, D_FF_PAD - D_FF), (0, 0)))}

        def rider_b(self):
            return _gather_rider([sh_b])

        def w3(self, got):
            b, = with_own(got, [sh_b])
            return {"up_p": _cols_from_chips(b, _UP_PADDED)}

        def rider_g(self, G):
            gf = _full_grads(G)
            gch = {n: _to_chips(n, gf[n]) for n in BIG if n not in ("w_in", "w_up")}
            gch["w_up"] = _chips_from_cols(G["up_p"], _UP_SEGS, 2 * D_FF // 4)
            self.pair = reduce_start("rest", _merge_shards(gch)[1:])
            return _scatter_rider(self.pair)

        def g_done(self, got):
            self.halves = reduce_end("rest", self.pair, got)

        def rider_last(self, G):
            self.pair_in = reduce_start("in", [_chips_from_cols(G["in_p"], _IN_SEGS, sum(IN_WIDTHS) // 4)])
            return _scatter_rider(self.pair_in)

        def last_done(self, got):
            self.halves = reduce_end("in", self.pair_in, got) + self.halves

    src = MeshWeights()
    loss, dx, G = _fwd_bwd(x[0], mem[0], loss_target[0], src)

    small_names = SMALL
    small_shapes = [()] + [tuple(G[n].shape) for n in small_names]
    packed = _pack([loss] + [G[n] for n in small_names])
    allp = _allgather8(packed)
    tot = _addn("small_sum", [(allp, k) for k in range(8)])
    parts = _unpack(tot, small_shapes)
    loss_tot, gsmall = parts[0], dict(zip(small_names, parts[1:]))
    gsmall["ffn_conv_w"] = lax.dynamic_slice_in_dim(gsmall["ffn_conv_w"], chip * fcw_sh[1], fcw_sh[1], axis=1)
    gsmall["gdn_conv_w"] = lax.dynamic_slice_in_dim(gsmall["gdn_conv_w"], chip * gcw_sh[1], gcw_sh[1], axis=1)

    both = []
    for h, o in zip(src.halves, _pair_exchange(src.halves)):
        both.append(jnp.concatenate([jnp.where(core == 0, h, o), jnp.where(core == 0, o, h)], axis=0))
    gbig = _split_shards(both)

    outs = {}
    for n in BIG:
        d_, m_, v_ = _adamw("adamw_" + n, sq(w[n]), gbig[n], sq(m[n]), sq(v[n]))
        outs[n] = (gbig[n], d_, m_, v_)
    for n in SMALL:
        two_d = (-1, w[n].shape[-1])
        g_ = gsmall[n].reshape(two_d)
        d_, m_, v_ = _adamw("adamw_" + n, w[n].reshape(two_d), g_, m[n].reshape(two_d), v[n].reshape(two_d))
        outs[n] = (g_, d_, m_, v_)

    res = [loss_tot.reshape(()), dx.reshape(x.shape)]
    for k in range(4):
        res += [outs[n][k].reshape(w[n].shape) for n in W_NAMES]
    return tuple(res)
```

```python
import functools
import math

import jax
import jax.numpy as jnp
import numpy as np
from jax import lax
from jax.experimental import pallas as pl
from jax.experimental.pallas import tpu as pltpu

F32 = jnp.float32
BF16 = jnp.bfloat16
_CDT = BF16
_GDT = BF16

D_MODEL = 2048
SWA_HEADS, SWA_KV_HEADS, SWA_HEAD_DIM, SWA_BLOCK = 16, 2, 64, 128
SWA_GRP = SWA_HEADS // SWA_KV_HEADS
REL_BUCKETS, REL_MAX_DIST = 32, 128
GDN_HEADS, GDN_HEAD_DIM, GDN_CONV, GDN_CHUNK = 8, 128, 4, 64
MEM_HEADS, MEM_HEAD_DIM = 4, 128
D_FF, D_FF_PAD, FFN_CONV = 5504, 5632, 3
SWA_Q, SWA_KV, GDN_W, MEM_W = 1024, 128, 1024, 512
IN_WIDTHS = (SWA_Q, SWA_KV, SWA_KV, GDN_W, GDN_W, GDN_W, GDN_W, GDN_HEADS, GDN_HEADS, D_MODEL, D_MODEL)
NORM_EPS = 1e-5
ALPHA = 2.0 ** 0.25
NEG_INF = -1e30
ADAM_LR, ADAM_B1, ADAM_B2, ADAM_EPS, ADAM_WD, ADAM_STEP = 0.001, 0.9, 0.999, 1e-08, 0.01, 10
LANES, SUBLANES = 128, 8
VMEM_LIMIT = 56 * 1024 * 1024

P_GS, P_GG, P_GQKV, P_GZ, P_SQ, P_SK, P_SV, P_BA, P_USED, P_END = 0, 2048, 4096, 7168, 8192, 9216, 9344, 9472, 9600, 9728


def _tile(dim, pref, align=LANES):
    if dim <= pref:
        return dim
    t = (pref // align) * align
    while t >= align:
        if dim % t == 0:
            return t
        t -= align
    return dim


_DIMS = {"nn": (((1,), (0,)), ((), ())), "nt": (((1,), (1,)), ((), ())), "tn": (((0,), (0,)), ((), ()))}
_BDIMS = {"nn": (((2,), (1,)), ((0,), (0,))), "nt": (((2,), (2,)), ((0,), (0,))), "tn": (((1,), (1,)), ((0,), (0,)))}


def _raw_dot(a, b, form, hi):
    dims = (_BDIMS if a.ndim == 3 else _DIMS)[form]
    if hi == "x3":
        a, b = a.astype(F32), b.astype(F32)
        ah, bh = a.astype(BF16), b.astype(BF16)
        al, bl = (a - ah.astype(F32)).astype(BF16), (b - bh.astype(F32)).astype(BF16)
        d = lambda p, q: lax.dot_general(p, q, dims, preferred_element_type=F32)
        if form == "tn":
            return d(ah, bh) + (d(ah, bl) + d(al, bh))
        m = a.shape[-2]
        both = d(jnp.concatenate([ah, al], axis=-2), bh)
        return both[..., :m, :] + (d(ah, bl) + both[..., m:, :])
    if hi:
        return lax.dot_general(a.astype(F32), b.astype(F32), dims, precision=lax.Precision.HIGHEST, preferred_element_type=F32)
    return lax.dot_general(a.astype(_CDT), b.astype(_CDT), dims, preferred_element_type=F32)


@functools.partial(jax.custom_vjp, nondiff_argnums=(2, 3))
def _dot(a, b, form, hi=False):
    return _raw_dot(a, b, form, hi)


def _dot_fwd(a, b, form, hi):
    return _raw_dot(a, b, form, hi), (a, b)


def _dot_bwd(form, hi, res, g):
    a, b = res
    if form == "nn":
        da, db = _raw_dot(g, b, "nt", hi), _raw_dot(a, g, "tn", hi)
    elif form == "nt":
        da, db = _raw_dot(g, b, "nn", hi), _raw_dot(g, a, "tn", hi)
    else:
        da, db = _raw_dot(b, g, "nt", hi), _raw_dot(a, g, "nn", hi)
    return da.astype(a.dtype), db.astype(b.dtype)


_dot.defvjp(_dot_fwd, _dot_bwd)


@functools.partial(jax.custom_vjp, nondiff_argnums=(2,))
def _shift_halo(prev, cur, d):
    assert prev.shape[0] == SUBLANES
    return pltpu.roll(jnp.concatenate([prev, cur], axis=0), d, 0)[SUBLANES:]


def _shift_halo_fwd(prev, cur, d):
    return _shift_halo(prev, cur, d), None


def _shift_halo_bwd(d, _, g):
    nh = SUBLANES
    ext = jnp.concatenate([jnp.zeros((nh, g.shape[1]), g.dtype), g], axis=0)
    r = pltpu.roll(ext, ext.shape[0] - d, 0)
    return r[:nh], r[nh:]


_shift_halo.defvjp(_shift_halo_fwd, _shift_halo_bwd)


@jax.custom_vjp
def _recip(x):
    return 1.0 / x


def _recip_fwd(x):
    r = 1.0 / x
    return r, r


def _recip_bwd(r, g):
    return (-g * r * r,)


_recip.defvjp(_recip_fwd, _recip_bwd)


def _sigmoid(x):
    return _recip(1.0 + jnp.exp(-x))


def _silu(x):
    return x * _sigmoid(x)


def _softplus(x):
    return jnp.maximum(x, 0.0) + jnp.log(1.0 + jnp.exp(-jnp.abs(x)))


def _iota(shape, axis):
    return lax.broadcasted_iota(jnp.int32, shape, axis)


def _cparams(sem, **kw):
    return pltpu.CompilerParams(dimension_semantics=sem, vmem_limit_bytes=VMEM_LIMIT, **kw)


class _ride:
    def __init__(self, rider, n_in, n_out, n_scr):
        self.rider = rider
        self.ins = rider.ins if rider else []
        n_rin = len(self.ins)
        self.out_shapes = rider.out_shapes if rider else []
        n_rout = len(self.out_shapes)
        self.in_specs, self.out_specs = [_HBM] * n_rin, [_HBM] * n_rout
        self.scratch = rider.sems() if rider else []
        self.o0 = n_in + n_rin
        self.s0 = self.o0 + n_out + n_rout
        self._rin = slice(n_in, n_in + n_rin)
        self._rout = slice(self.o0 + n_out, self.s0)
        self._sem = self.s0 + n_scr

    def _args(self, refs):
        return refs[self._rin], refs[self._rout], refs[self._sem], refs[self._sem + 1]

    def at_start(self, refs, cond):
        if self.rider:
            pl.when(cond)(lambda: self.rider.start(*self._args(refs)))

    def at_end(self, refs, cond):
        if self.rider:
            pl.when(cond)(lambda: self.rider.finish(*self._args(refs)))


def _mm(name, a, b, form, out_dtype=F32, add=None, add_scale=1.0, hi=False, tm=1024, tn=1024, tk=2816, rider=None):
    if form == "nn":
        (M, K), (K2, N) = a.shape, b.shape
    elif form == "nt":
        (M, K), (N, K2) = a.shape, b.shape
    else:
        (K, M), (K2, N) = a.shape, b.shape
    assert K == K2, (name, a.shape, b.shape, form)
    tm, tn, tk = _tile(M, tm), _tile(N, tn), _tile(K, tk)
    nk = K // tk
    a_spec = pl.BlockSpec((tk, tm), lambda i, j, k: (k, i)) if form == "tn" else pl.BlockSpec((tm, tk), lambda i, j, k: (i, k))
    b_spec = pl.BlockSpec((tn, tk), lambda i, j, k: (j, k)) if form == "nt" else pl.BlockSpec((tk, tn), lambda i, j, k: (k, j))
    o_spec = pl.BlockSpec((tm, tn), lambda i, j, k: (i, j))
    has_add = add is not None

    def finish(r, c_ref, o_ref):
        if has_add:
            r = r + add_scale * c_ref[...].astype(F32)
        o_ref[...] = r.astype(out_dtype)

    n_own = 3 if has_add else 2
    grid = (M // tm, N // tn, nk)
    rd = _ride(rider, n_own, 1, 1 if nk > 1 else 0)

    def body(*refs):
        a_ref, b_ref = refs[:2]
        c_ref = refs[2] if has_add else None
        o_ref = refs[rd.o0]
        pid = [pl.program_id(d) for d in range(3)]
        rd.at_start(refs, (pid[0] == 0) & (pid[1] == 0) & (pid[2] == 0))
        if nk == 1:
            finish(_raw_dot(a_ref[...], b_ref[...], form, hi), c_ref, o_ref)
        else:
            acc = refs[rd.s0]

            @pl.when(pid[2] == 0)
            def _():
                acc[...] = jnp.zeros_like(acc)

            acc[...] += _raw_dot(a_ref[...], b_ref[...], form, hi)

            @pl.when(pid[2] == nk - 1)
            def _():
                finish(acc[...], c_ref, o_ref)
        rd.at_end(refs, (pid[0] == grid[0] - 1) & (pid[1] == grid[1] - 1) & (pid[2] == nk - 1))

    ins = [a, b] + ([add] if has_add else [])
    specs = [a_spec, b_spec] + ([o_spec] if has_add else [])
    res = pl.pallas_call(
        body, name=name, grid=grid, in_specs=specs + rd.in_specs, out_specs=[o_spec] + rd.out_specs,
        out_shape=[jax.ShapeDtypeStruct((M, N), out_dtype)] + rd.out_shapes,
        scratch_shapes=([pltpu.VMEM((tm, tn), F32)] if nk > 1 else []) + rd.scratch,
        compiler_params=_cparams(("arbitrary",) * 3 if rider else ("parallel", "parallel", "arbitrary")),
    )(*ins, *rd.ins)
    return (res[0], res[1:]) if rider else res[0]


class Row:
    def __init__(self, arr, blk, imap, hblk=None, hmap=None, gshape=None, gmap=None, gdt=(F32,)):
        self.arr, self.blk, self.imap, self.hblk, self.hmap, self.gshape, self.gmap = arr, blk, imap, hblk, hmap, gshape, gmap
        self.gdt = gdt


class Par:
    def __init__(self, arr, blk=None, imap=None, gshape=None, gmap=None):
        self.arr = arr
        self.blk = tuple(arr.shape) if blk is None else blk
        nd = len(self.blk)
        self.imap = (lambda j: (0,) * nd) if imap is None else imap
        self.gshape, self.gmap = gshape, gmap


class Out:
    def __init__(self, shape, dtype, blk, imap):
        self.shape, self.dtype, self.blk, self.imap = shape, dtype, blk, imap


def _rows_of(blk):
    return [d for d in blk if d is not None][0]


def _rowmap(name, fn, ncol, nblk, rows, pars, outs, accs=()):
    in_specs, ins = [], []
    for r in rows:
        ins.append(r.arr)
        in_specs.append(pl.BlockSpec(r.blk, r.imap))
        if r.hblk is not None:
            ins.append(r.arr)
            in_specs.append(pl.BlockSpec(r.hblk, r.hmap))
    for p in pars:
        ins.append(p.arr)
        in_specs.append(pl.BlockSpec(p.blk, (lambda im: (lambda j, n: im(j)))(p.imap)))
    out_specs = [pl.BlockSpec(o.blk, o.imap) for o in outs]
    out_shape = [jax.ShapeDtypeStruct(o.shape, o.dtype) for o in outs]
    for a in accs:
        out_specs.append(pl.BlockSpec(a, (lambda nd: (lambda j, n: (0,) * nd))(len(a))))
        out_shape.append(jax.ShapeDtypeStruct(a, F32))
    n_in = len(ins)

    def body(*refs):
        j, n = pl.program_id(0), pl.program_id(1)
        it = iter(refs[:n_in])
        rvals = []
        for r in rows:
            cur = next(it)[...]
            rvals.append((next(it)[...], cur) if r.hblk is not None else cur)
        pvals = [next(it)[...] for _ in pars]
        o_refs = refs[n_in:n_in + len(outs)]
        a_refs = refs[n_in + len(outs):]
        ovals, avals = fn(j, n == 0, rvals, pvals)
        for ref, v in zip(o_refs, ovals):
            ref[...] = v.astype(ref.dtype)
        if accs:
            @pl.when((j == 0) & (n == 0))
            def _():
                for ref in a_refs:
                    ref[...] = jnp.zeros_like(ref)
            for ref, v in zip(a_refs, avals):
                ref[...] += v

    res = pl.pallas_call(
        body, name=name, grid=(ncol, nblk), in_specs=in_specs, out_specs=out_specs, out_shape=out_shape,
        compiler_params=_cparams(("arbitrary", "arbitrary")),
    )(*ins)
    return res


def _rowmap_bwd(name, fn, ncol, nblk, rows, pars, cts):
    rev = lambda im: (lambda j, s: im(j, nblk - 1 - s))
    in_specs, ins = [], []
    for r in rows:
        ins.append(r.arr)
        in_specs.append(pl.BlockSpec(r.blk, rev(r.imap)))
        if r.hblk is not None:
            ins.append(r.arr)
            in_specs.append(pl.BlockSpec(r.hblk, rev(r.hmap)))
    for p in pars:
        ins.append(p.arr)
        in_specs.append(pl.BlockSpec(p.blk, (lambda im: (lambda j, s: im(j)))(p.imap)))
    for c in cts:
        ins.append(c.arr)
        in_specs.append(pl.BlockSpec(c.blk, rev(c.imap)))
    n_in = len(ins)
    drows = [i for i, r in enumerate(rows) if r.gshape is not None]
    dpars = [i for i, p in enumerate(pars) if p.gshape is not None]
    out_specs, out_shape, scratch = [], [], []
    for i in drows:
        r = rows[i]
        for dt in r.gdt:
            out_specs.append(pl.BlockSpec(r.blk, rev(r.gmap)))
            out_shape.append(jax.ShapeDtypeStruct(r.gshape, dt))
        if r.hblk is not None:
            scratch.append(pltpu.VMEM(tuple(d for d in r.hblk if d is not None), F32))
    n_drow_out = len(out_specs)
    for i in dpars:
        p = pars[i]
        out_specs.append(pl.BlockSpec(p.blk, (lambda im: (lambda j, s: im(j)))(p.gmap)))
        out_shape.append(jax.ShapeDtypeStruct(p.gshape, F32))

    def body(*refs):
        j, s = pl.program_id(0), pl.program_id(1)
        first = s == nblk - 1
        it = iter(refs[:n_in])
        rvals = []
        for r in rows:
            cur = next(it)[...]
            rvals.append((next(it)[...], cur) if r.hblk is not None else cur)
        pvals = [next(it)[...] for _ in pars]
        cvals = [next(it)[...].astype(F32) for _ in cts]
        g_refs = iter(refs[n_in:n_in + n_drow_out])
        p_refs = refs[n_in + n_drow_out:n_in + n_drow_out + len(dpars)]
        carries = iter(refs[n_in + n_drow_out + len(dpars):])

        def f(dr, dp):
            rv, pv = list(rvals), list(pvals)
            for i, v in zip(drows, dr):
                rv[i] = v
            for i, v in zip(dpars, dp):
                pv[i] = v
            return fn(j, first, rv, pv)

        _, vjp = jax.vjp(f, [rvals[i] for i in drows], [pvals[i] for i in dpars])
        g_r, g_p = vjp(cvals)
        for i, g in zip(drows, g_r):
            r = rows[i]
            if r.hblk is None:
                for _ in r.gdt:
                    ref = next(g_refs)
                    ref[...] = g.astype(ref.dtype)
            else:
                g_prev, g_cur = g
                carry = next(carries)
                nr, nh = g_cur.shape[-2], g_prev.shape[-2]
                tail = g_cur[..., nr - nh:nr, :] + jnp.where(s > 0, carry[...], 0.0)
                for _ in r.gdt:
                    ref = next(g_refs)
                    if nr > nh:
                        ref[..., 0:nr - nh, :] = g_cur[..., 0:nr - nh, :].astype(ref.dtype)
                    ref[..., nr - nh:nr, :] = tail.astype(ref.dtype)
                carry[...] = g_prev
        for ref, g in zip(p_refs, g_p):
            @pl.when(s == 0)
            def _():
                ref[...] = jnp.zeros_like(ref)
            ref[...] += g

    return pl.pallas_call(
        body, name=name, grid=(ncol, nblk), in_specs=in_specs, out_specs=out_specs, out_shape=out_shape,
        scratch_shapes=scratch, compiler_params=_cparams(("arbitrary", "arbitrary")),
    )(*ins)


def _rowspec(arr, tb, cw, c0, cstep=1, halo=0, grad=False, ncol=1, gdt=(F32,)):
    T = arr.shape[0]
    imap = lambda j, n: (n, c0 + cstep * j)
    hblk = hmap = None
    if halo:
        q = tb // halo
        hblk, hmap = (halo, cw), (lambda j, n: (jnp.maximum(n * q - 1, 0), c0 + cstep * j))
    gshape = (T, cw * (ncol if cstep else 1)) if grad else None
    gmap = (lambda j, n: (n, cstep * j)) if grad else None
    return Row(arr, (tb, cw), imap, hblk, hmap, gshape, gmap, gdt)


def _parspec(arr, cw=None, c0=0, grad=False, ncol=1):
    if cw is None:
        return Par(arr, gshape=tuple(arr.shape) if grad else None,
                   gmap=(lambda nd: (lambda j: (0,) * nd))(arr.ndim) if grad else None)
    r = arr.shape[0]
    return Par(arr, (r, cw), lambda j: (0, c0 + j), (r, cw * ncol) if grad else None, (lambda j: (0, j)) if grad else None)


def _ln(r, g, b):
    mu = jnp.mean(r, axis=-1, keepdims=True)
    xc = r - mu
    var = jnp.mean(xc * xc, axis=-1, keepdims=True)
    return xc * lax.rsqrt(var + NORM_EPS) * g + b


def _ln_fn(j, first, rv, pv):
    return [_ln(rv[0], pv[0], pv[1])]


def _ln_fwd_fn(j, first, rv, pv):
    y = _ln(rv[0], pv[0], pv[1])
    return [y, y], []


def _loss_fn(j, first, rv, pv):
    r3, tgt = rv
    g, b = pv
    y, vjp = jax.vjp(_ln, r3, g, b)
    diff = y - tgt
    part = 0.5 * jnp.sum(diff * diff) / D_MODEL
    dr, dg, db = vjp(diff * (1.0 / D_MODEL))
    return [dr, dr], [jnp.full((SUBLANES, LANES), part, F32), dg, db]


def _mix_fn(j, first, rv, pv):
    gs, gg, ys, yg = rv
    return [_sigmoid(gs) * ys + _sigmoid(gg) * yg]


def _row_pick(x, i):
    ax = x.ndim - 2
    return jnp.sum(jnp.where(_iota(x.shape, ax) == i, x, 0.0), axis=ax, keepdims=True)


def _causal_conv(prev, cur, w, first):
    width = w.shape[0]
    prev = jnp.where(first, 0.0, prev)
    y = cur * _row_pick(w, width - 1)
    for d in range(1, width):
        y = y + _shift_halo(prev, cur, d) * _row_pick(w, width - 1 - d)
    return y


def _ffn_act_fn(j, first, rv, pv):
    (pg, cg), (pu, cu) = rv
    wg, wu, bg, bu = pv
    hg = _causal_conv(pg, cg, wg, first) + bg
    hu = _causal_conv(pu, cu, wu, first) + bu
    return [_silu(hg) * hu]


def _gdn_pre_fn(j, first, rv, pv):
    (prev, cur), = rv
    w, = pv
    t = _silu(_causal_conv(prev, cur, w, first))
    tn = t * lax.rsqrt(jnp.sum(t * t, axis=-1, keepdims=True) + 1e-6)
    return [jnp.where(j < 2 * GDN_HEADS, tn, t)]


def _gdn_gate_fn(j, first, rv, pv):
    gba, = rv
    alog, dtb, eb, eg = pv
    tb = gba.shape[0]
    beta = _sigmoid(gba)
    g = -jnp.exp(alog) * _softplus(gba + dtb)
    ri, ci = _iota((tb, tb), 0), _iota((tb, tb), 1)
    tril = jnp.where((ri // GDN_CHUNK == ci // GDN_CHUNK) & (ci <= ri), 1.0, 0.0)
    gc = _dot(tril, g, "nn", True)
    return [_dot(beta, eb, "nn", True), _dot(gc, eg, "nn", True)]


def _swa_fn(j, first, rv, pv):
    q, (kp, kc), (vp, vc) = rv
    bp, bc, sk = pv
    sp = _dot(q, kp, "nt") * (SWA_HEAD_DIM ** -0.5) + bp
    sc = _dot(q, kc, "nt") * (SWA_HEAD_DIM ** -0.5) + bc
    qi = _iota(sp.shape, sp.ndim - 2) % SWA_BLOCK
    kj = _iota(sp.shape, sp.ndim - 1)
    sp = jnp.where((kj > qi) & jnp.logical_not(first), sp, NEG_INF)
    sc = jnp.where(kj <= qi, sc, NEG_INF)
    m = jnp.maximum(jnp.maximum(jnp.max(sp, axis=-1, keepdims=True), jnp.max(sc, axis=-1, keepdims=True)), sk)
    m = lax.stop_gradient(m)
    ep, ec, es = jnp.exp(sp - m), jnp.exp(sc - m), jnp.exp(sk - m)
    inv = 1.0 / (jnp.sum(ep, axis=-1, keepdims=True) + jnp.sum(ec, axis=-1, keepdims=True) + es)
    vp = jnp.where(first, 0.0, vp)
    return [_dot(ep * inv, vp, "nn") + _dot(ec * inv, vc, "nn")]


def _memattn_fn(j, first, rv, pv):
    q, = rv
    k, v = pv
    s = _dot(q, k, "nt") * (MEM_HEAD_DIM ** -0.5)
    m = lax.stop_gradient(jnp.max(s, axis=-1, keepdims=True))
    e = jnp.exp(s - m)
    p = e * (1.0 / jnp.sum(e, axis=-1, keepdims=True))
    return [_dot(p, v, "nn")]


SOLVE_PREC = "x3"


@jax.custom_vjp
def _unit_lower_inv(a):
    c = a.shape[-1]
    eye = _iota((1, c, c), 1) == _iota((1, c, c), 2)
    tinv = jnp.where(eye, 1.0, 0.0) - a
    x = _raw_dot(a, a, "nn", SOLVE_PREC)
    for i in range(5):
        tinv = tinv + _raw_dot(tinv, x, "nn", SOLVE_PREC)
        if i < 4:
            x = _raw_dot(x, x, "nn", SOLVE_PREC)
    return tinv


def _unit_lower_inv_fwd(a):
    t = _unit_lower_inv(a)
    return t, t


def _unit_lower_inv_bwd(t, g):
    return (-_raw_dot(_raw_dot(t, g, "tn", SOLVE_PREC), t, "nt", SOLVE_PREC),)


_unit_lower_inv.defvjp(_unit_lower_inv_fwd, _unit_lower_inv_bwd)


@jax.custom_vjp
def _known_inv(a, t):
    return t


def _known_inv_fwd(a, t):
    return t, t


def _known_inv_bwd(t, g):
    return _unit_lower_inv_bwd(t, g) + (jnp.zeros_like(t),)


_known_inv.defvjp(_known_inv_fwd, _known_inv_bwd)


def _gdn_heads(q, k, v, bx, gx, g64, z, nw, S, tinv=None, keep_tinv=False):
    c = GDN_CHUNK
    q = q * (GDN_HEAD_DIM ** -0.5)
    kb, vb = k * bx, v * bx
    ri, ci = _iota((1, c, c), 1), _iota((1, c, c), 2)
    tril, strict, eye = ci <= ri, ci < ri, ci == ri
    grow = jnp.sum(jnp.where(eye, g64, 0.0), axis=1, keepdims=True)
    decay = jnp.where(tril, jnp.exp(jnp.where(tril, g64 - grow, 0.0)), 0.0)
    a = jnp.where(strict, _dot(kb, k, "nt") * decay, 0.0)
    tinv = _unit_lower_inv(a) if tinv is None else _known_inv(a, tinv)
    eg = jnp.exp(gx)
    u = _dot(tinv, vb, "nn", SOLVE_PREC)
    w = _dot(tinv, kb * eg, "nn", SOLVE_PREC)
    ai = jnp.where(tril, _dot(q, k, "nt") * decay, 0.0)
    glast = _row_pick(gx, c - 1)
    v_new = u - _dot(w, S, "nn")
    o = _dot(q * eg, S, "nn") + _dot(ai, v_new, "nn")
    s_new = S * jnp.exp(glast) + _dot(k * jnp.exp(glast - gx), v_new, "tn")
    o = o * lax.rsqrt(jnp.mean(o * o, axis=-1, keepdims=True) + 1e-6) * nw
    return (o * _silu(z), s_new, tinv) if keep_tinv else (o * _silu(z), s_new)


GDN_STEP_CHUNKS = 2


def _head_major(ref, off, width=GDN_HEAD_DIM, ci=0):
    r = slice(ci * GDN_CHUNK, (ci + 1) * GDN_CHUNK)
    return jnp.stack([ref[r, off + h * GDN_HEAD_DIM:off + h * GDN_HEAD_DIM + width] for h in range(GDN_HEADS)])


def _gdn_chunks_fwd(qkv, bx, gx, proj, nw, rider=None):
    T = qkv.shape[0]
    cps = GDN_STEP_CHUNKS
    nc, c, hd, nh = T // (cps * GDN_CHUNK), GDN_CHUNK, GDN_HEAD_DIM, GDN_HEADS
    rd = _ride(rider, 5, 3, 1)

    def body(*refs):
        qkv_ref, bx_ref, gx_ref, z_ref, nw_ref = refs[:5]
        y_ref, st_ref, ti_ref = refs[rd.o0:rd.o0 + 3]
        S = refs[rd.s0]
        rd.at_start(refs, pl.program_id(0) == 0)

        @pl.when(pl.program_id(0) == 0)
        def _():
            S[...] = jnp.zeros_like(S)

        s_new = S[...]
        for ci in range(cps):
            st_ref[ci] = s_new
            y, s_new, ti = _gdn_heads(_head_major(qkv_ref, 0, ci=ci), _head_major(qkv_ref, GDN_W, ci=ci),
                                      _head_major(qkv_ref, 2 * GDN_W, ci=ci), _head_major(bx_ref, 0, ci=ci),
                                      _head_major(gx_ref, 0, ci=ci), _head_major(gx_ref, 0, c, ci), _head_major(z_ref, 0, ci=ci),
                                      nw_ref[...], s_new, keep_tinv=True)
            ti_ref[ci] = ti
            for h in range(nh):
                y_ref[ci * c:(ci + 1) * c, h * hd:(h + 1) * hd] = y[h].astype(y_ref.dtype)
        S[...] = s_new
        rd.at_end(refs, pl.program_id(0) == nc - 1)

    row = lambda w, cb: pl.BlockSpec((cps * c, w), lambda n: (n, cb))
    res = pl.pallas_call(
        body, name="gdn_chunks_fwd", grid=(nc,),
        in_specs=[row(3 * GDN_W, 0), row(GDN_W, 0), row(GDN_W, 0), row(GDN_W, P_GZ // GDN_W),
                  pl.BlockSpec((1, hd), lambda n: (0, 0))] + rd.in_specs,
        out_specs=[row(GDN_W, 0), pl.BlockSpec((cps, nh, hd, hd), lambda n: (n, 0, 0, 0)),
                   pl.BlockSpec((cps, nh, c, c), lambda n: (n, 0, 0, 0))] + rd.out_specs,
        out_shape=[jax.ShapeDtypeStruct((T, GDN_W), BF16), jax.ShapeDtypeStruct((nc * cps, nh, hd, hd), F32),
                   jax.ShapeDtypeStruct((nc * cps, nh, c, c), F32)] + rd.out_shapes,
        scratch_shapes=[pltpu.VMEM((nh, hd, hd), F32)] + rd.scratch,
        compiler_params=_cparams(("arbitrary",)),
    )(qkv, bx, gx, proj, nw, *rd.ins)
    return res[0], (res[1], res[2]), res[3:]


def _gdn_chunks_bwd(qkv, bx, gx, proj, nw, saved, dy, rider=None):
    states, tinvs = saved
    T = qkv.shape[0]
    cps = GDN_STEP_CHUNKS
    nc, c, hd, nh = T // (cps * GDN_CHUNK), GDN_CHUNK, GDN_HEAD_DIM, GDN_HEADS
    rd = _ride(rider, 8, 5, 1)

    def body(*refs):
        qkv_ref, bx_ref, gx_ref, z_ref, nw_ref, st_ref, ti_ref, dy_ref = refs[:8]
        dqkv_ref, dbx_ref, dgx_ref, dz_ref, dnw_ref = refs[rd.o0:rd.o0 + 5]
        dS = refs[rd.s0]
        rd.at_start(refs, pl.program_id(0) == 0)

        @pl.when(pl.program_id(0) == 0)
        def _():
            dS[...] = jnp.zeros_like(dS)
            dnw_ref[...] = jnp.zeros_like(dnw_ref)

        dsp = dS[...]
        for ci in reversed(range(cps)):
            r = slice(ci * c, (ci + 1) * c)
            args = (_head_major(qkv_ref, 0, ci=ci), _head_major(qkv_ref, GDN_W, ci=ci), _head_major(qkv_ref, 2 * GDN_W, ci=ci),
                    _head_major(bx_ref, 0, ci=ci), _head_major(gx_ref, 0, ci=ci), _head_major(gx_ref, 0, c, ci),
                    _head_major(z_ref, 0, ci=ci), nw_ref[...], st_ref[ci])
            _, vjp = jax.vjp(functools.partial(_gdn_heads, tinv=ti_ref[ci]), *args)
            dq, dk, dv, dbx, dgx, dg64, dz, dnw, dsp = vjp((_head_major(dy_ref, 0, ci=ci), dsp))
            for h in range(nh):
                sl = slice(h * hd, (h + 1) * hd)
                dqkv_ref[r, sl] = dq[h].astype(dqkv_ref.dtype)
                dqkv_ref[r, GDN_W + h * hd:GDN_W + (h + 1) * hd] = dk[h].astype(dqkv_ref.dtype)
                dqkv_ref[r, 2 * GDN_W + h * hd:2 * GDN_W + (h + 1) * hd] = dv[h].astype(dqkv_ref.dtype)
                dbx_ref[r, sl] = dbx[h]
                dgx_ref[r, sl] = dgx[h]
                dgx_ref[r, h * hd:h * hd + c] += dg64[h]
                dz_ref[r, sl] = dz[h].astype(dz_ref.dtype)
            dnw_ref[...] += dnw
        dS[...] = dsp
        rd.at_end(refs, pl.program_id(0) == nc - 1)

    row = lambda w, cb: pl.BlockSpec((cps * c, w), lambda s: (nc - 1 - s, cb))
    res = pl.pallas_call(
        body, name="gdn_chunks_bwd", grid=(nc,),
        in_specs=[row(3 * GDN_W, 0), row(GDN_W, 0), row(GDN_W, 0), row(GDN_W, P_GZ // GDN_W), pl.BlockSpec((1, hd), lambda s: (0, 0)),
                  pl.BlockSpec((cps, nh, hd, hd), lambda s: (nc - 1 - s, 0, 0, 0)),
                  pl.BlockSpec((cps, nh, c, c), lambda s: (nc - 1 - s, 0, 0, 0)), row(GDN_W, 0)] + rd.in_specs,
        out_specs=[row(3 * GDN_W, 0), row(GDN_W, 0), row(GDN_W, 0), row(GDN_W, 0),
                   pl.BlockSpec((1, hd), lambda s: (0, 0))] + rd.out_specs,
        out_shape=[jax.ShapeDtypeStruct((T, 3 * GDN_W), F32), jax.ShapeDtypeStruct((T, GDN_W), F32),
                   jax.ShapeDtypeStruct((T, GDN_W), F32), jax.ShapeDtypeStruct((T, GDN_W), _CDT),
                   jax.ShapeDtypeStruct((1, hd), F32)] + rd.out_shapes,
        scratch_shapes=[pltpu.VMEM((nh, hd, hd), F32)] + rd.scratch,
        compiler_params=_cparams(("arbitrary",)),
    )(qkv, bx, gx, proj, nw, states, tinvs, dy, *rd.ins)
    res = list(res)
    return res[:5] + [res[5:]]


def _adamw(name, w, g, m, v):
    R, C = w.shape
    tr = _tile(R, 128, SUBLANES)

    def body(w_ref, g_ref, m_ref, v_ref, d_ref, m2_ref, v2_ref):
        g_ = g_ref[...]
        m2 = ADAM_B1 * m_ref[...] + (1.0 - ADAM_B1) * g_
        v2 = ADAM_B2 * v_ref[...] + (1.0 - ADAM_B2) * (g_ * g_)
        m_hat = m2 / (1.0 - ADAM_B1 ** ADAM_STEP)
        v_hat = v2 / (1.0 - ADAM_B2 ** ADAM_STEP)
        d_ref[...] = -ADAM_LR * (m_hat / (jnp.sqrt(v_hat) + ADAM_EPS) + ADAM_WD * w_ref[...])
        m2_ref[...] = m2
        v2_ref[...] = v2

    spec = pl.BlockSpec((tr, C), lambda i: (i, 0))
    return pl.pallas_call(
        body, name=name, grid=(R // tr,), in_specs=[spec] * 4, out_specs=[spec] * 3,
        out_shape=[jax.ShapeDtypeStruct((R, C), F32)] * 3, compiler_params=_cparams(("parallel",)),
    )(w, g, m, v)


def _addn(name, parts, out_dtype=F32):
    parts = [p if isinstance(p, tuple) else (p, None) for p in parts]
    a0, k0 = parts[0]
    R, C = a0.shape[-2:]
    tr = _tile(R, 256, 2 * SUBLANES)
    specs = []
    for a, k in parts:
        if k is None:
            specs.append(pl.BlockSpec((tr, C), lambda i: (i, 0)))
        else:
            specs.append(pl.BlockSpec((None, tr, C), (lambda kk: (lambda i: (kk, i, 0)))(k)))

    def body(*refs):
        acc = refs[0][...].astype(F32)
        for r in refs[1:-1]:
            acc = acc + r[...].astype(F32)
        refs[-1][...] = acc.astype(out_dtype)

    return pl.pallas_call(
        body, name=name, grid=(R // tr,), in_specs=specs, out_specs=pl.BlockSpec((tr, C), lambda i: (i, 0)),
        out_shape=jax.ShapeDtypeStruct((R, C), out_dtype), compiler_params=_cparams(("parallel",)),
    )(*[a for a, _ in parts])


MESH = pl.DeviceIdType.MESH
_HBM = pl.BlockSpec(memory_space=pltpu.HBM)


def _place():
    x, y, c = lax.axis_index("x"), lax.axis_index("y"), lax.axis_index("c")
    return x, y, c, [(1 - x, y), (x, 1 - y), (1 - x, 1 - y)]


class _Rider:
    def __init__(self, ins, out_shapes, nsem, start, finish):
        self.ins, self.out_shapes, self.nsem, self.start, self.finish = list(ins), list(out_shapes), nsem, start, finish

    def sems(self):
        return [pltpu.SemaphoreType.DMA((self.nsem,)), pltpu.SemaphoreType.DMA((self.nsem,))]


def _run_rider(name, rd):
    n_in, n_out = len(rd.ins), len(rd.out_shapes)

    def body(*refs):
        ins, outs, (send, recv) = refs[:n_in], refs[n_in:n_in + n_out], refs[n_in + n_out:]
        rd.start(ins, outs, send, recv)
        rd.finish(ins, outs, send, recv)

    return pl.pallas_call(body, name=name, in_specs=[_HBM] * n_in, out_specs=[_HBM] * n_out, out_shape=rd.out_shapes,
                          scratch_shapes=rd.sems())(*rd.ins)


def _gather_rider(ts):
    nt = len(ts)

    def half(t, hc):
        rh = ts[t].shape[0] // 2
        return pl.ds(pl.multiple_of(hc * rh, 16), rh)

    def rcopy(send, recv, t, k, src, dst, to):
        return pltpu.make_async_remote_copy(src_ref=src, dst_ref=dst, send_sem=send.at[6 * t + k], recv_sem=recv.at[6 * t + k],
                                            device_id=to, device_id_type=MESH)

    def first_hop(ins, outs, send, recv, t, r, px, py, c, me):
        return rcopy(send, recv, t, r, ins[t].at[half(t, c)], outs[t].at[me, half(t, c)], (px, py, c))

    def start(ins, outs, send, recv):
        x, y, c, rel = _place()
        for t in range(nt):
            for r, (px, py) in enumerate(rel):
                first_hop(ins, outs, send, recv, t, r, px, py, c, 2 * x + y).start()

    def finish(ins, outs, send, recv):
        x, y, c, rel = _place()
        sib = (x, y, 1 - c)
        passed = []
        for t in range(nt):
            for r, (px, py) in enumerate(rel):
                got = outs[t].at[2 * px + py, half(t, c)]
                rcopy(send, recv, t, r, got, got, (px, py, c)).wait_recv()
                fw = rcopy(send, recv, t, 3 + r, got, got, sib)
                fw.start()
                passed.append(fw)
        for t in range(nt):
            for r, (px, py) in enumerate(rel):
                got = outs[t].at[2 * px + py, half(t, 1 - c)]
                rcopy(send, recv, t, 3 + r, got, got, sib).wait_recv()
        for t in range(nt):
            for r, (px, py) in enumerate(rel):
                first_hop(ins, outs, send, recv, t, r, px, py, c, 2 * x + y).wait_send()
        for fw in passed:
            fw.wait_send()

    return _Rider(ts, [jax.ShapeDtypeStruct((4,) + tuple(t.shape), t.dtype) for t in ts], 6 * nt, start, finish)


def _scatter_rider(ps):
    nt = len(ps)

    def copy(ins, outs, send, recv, t, r, px, py, c):
        return pltpu.make_async_remote_copy(src_ref=ins[t].at[2 * px + py], dst_ref=outs[t].at[r], send_sem=send.at[3 * t + r],
                                            recv_sem=recv.at[3 * t + r], device_id=(px, py, c), device_id_type=MESH)

    def start(ins, outs, send, recv):
        x, y, c, rel = _place()
        for t in range(nt):
            for r, (px, py) in enumerate(rel):
                copy(ins, outs, send, recv, t, r, px, py, c).start()

    def finish(ins, outs, send, recv):
        x, y, c, rel = _place()
        for t in range(nt):
            for r, (px, py) in enumerate(rel):
                copy(ins, outs, send, recv, t, r, px, py, c).wait()

    return _Rider(ps, [jax.ShapeDtypeStruct((3,) + tuple(p.shape[1:]), p.dtype) for p in ps], 3 * nt, start, finish)


def _pair_swap(tag, ts):
    nt = len(ts)

    def body(*refs):
        ins, outs = refs[:nt], refs[nt:2 * nt]
        send, recv = refs[2 * nt:]
        x, y, c, _ = _place()
        cps = []
        for t in range(nt):
            rh = ts[t].shape[1] // 2
            src = ins[t].at[:, pl.ds(pl.multiple_of((1 - c) * rh, 16), rh), :]
            cp = pltpu.make_async_remote_copy(src_ref=src, dst_ref=outs[t], send_sem=send.at[t], recv_sem=recv.at[t],
                                              device_id=(x, y, 1 - c), device_id_type=MESH)
            cp.start()
            cps.append(cp)
        for cp in cps:
            cp.wait()

    return pl.pallas_call(
        body, name="pair_swap_" + tag, in_specs=[_HBM] * nt, out_specs=[_HBM] * nt,
        out_shape=[jax.ShapeDtypeStruct((4, t.shape[1] // 2, t.shape[2]), t.dtype) for t in ts],
        scratch_shapes=[pltpu.SemaphoreType.DMA((nt,)), pltpu.SemaphoreType.DMA((nt,))],
    )(*ts)


def _pair_exchange(gs):
    nt = len(gs)

    def body(*refs):
        ins, outs = refs[:nt], refs[nt:2 * nt]
        send, recv = refs[2 * nt:]
        x, y, c, _ = _place()
        cps = []
        for t in range(nt):
            cp = pltpu.make_async_remote_copy(src_ref=ins[t], dst_ref=outs[t], send_sem=send.at[t], recv_sem=recv.at[t],
                                              device_id=(x, y, 1 - c), device_id_type=MESH)
            cp.start()
            cps.append(cp)
        for cp in cps:
            cp.wait()

    return pl.pallas_call(
        body, name="pair_exchange", in_specs=[_HBM] * nt, out_specs=[_HBM] * nt,
        out_shape=[jax.ShapeDtypeStruct(tuple(g.shape), g.dtype) for g in gs],
        scratch_shapes=[pltpu.SemaphoreType.DMA((nt,)), pltpu.SemaphoreType.DMA((nt,))],
    )(*gs)


def _allgather8(v):
    m, n = v.shape

    def body(x_ref, out_ref, send, recv, lsem):
        x, y, c, rel = _place()
        me, sib = (x, y, c), (x, y, 1 - c)

        def blk(px, py, pc):
            return out_ref.at[4 * px + 2 * py + pc]

        def copy(k, block, to, src=None):
            return pltpu.make_async_remote_copy(src_ref=blk(*block) if src is None else src, dst_ref=blk(*block), send_sem=send.at[k],
                                                recv_sem=recv.at[k], device_id=to, device_id_type=MESH)

        mine = pltpu.make_async_copy(x_ref, blk(*me), lsem)
        mine.start()
        first = [copy(0, me, sib, src=x_ref)] + [copy(1 + r, me, (*ch, c), src=x_ref) for r, ch in enumerate(rel)]
        for cp in first:
            cp.start()
        passed = [copy(4 + r, (*ch, c), sib) for r, ch in enumerate(rel)]
        for r, ch in enumerate(rel):
            copy(1 + r, (*ch, c), me).wait_recv()
            passed[r].start()
        copy(0, sib, me).wait_recv()
        for r, ch in enumerate(rel):
            copy(4 + r, (*ch, 1 - c), me).wait_recv()
        for cp in first + passed:
            cp.wait_send()
        mine.wait()

    return pl.pallas_call(
        body, name="allgather8", in_specs=[pl.BlockSpec(memory_space=pltpu.VMEM)], out_specs=pl.BlockSpec(memory_space=pltpu.VMEM),
        out_shape=jax.ShapeDtypeStruct((8, m, n), v.dtype),
        scratch_shapes=[pltpu.SemaphoreType.DMA((7,)), pltpu.SemaphoreType.DMA((7,)), pltpu.SemaphoreType.DMA],
    )(v)


def _t5_bucket(dist):
    max_exact = REL_BUCKETS // 2
    d = jnp.maximum(dist, 1).astype(F32)
    large = max_exact + (jnp.log(d / max_exact) / math.log(REL_MAX_DIST / max_exact) * (REL_BUCKETS - max_exact)).astype(jnp.int32)
    large = jnp.minimum(large, REL_BUCKETS - 1)
    return jnp.where(dist < max_exact, dist, large)


def _bias_onehot():
    qi = jnp.arange(SWA_BLOCK)[:, None]
    kj = jnp.arange(SWA_BLOCK)[None, :]
    dist = jnp.concatenate([(qi + SWA_BLOCK - kj).reshape(-1), (qi - kj).reshape(-1)])
    bucket = _t5_bucket(jnp.maximum(dist, 0))
    return (bucket[None, :] == jnp.arange(REL_BUCKETS)[:, None]).astype(F32)


def _head_spread():
    lane = jnp.arange(LANES)[:, None]
    head = jnp.arange(GDN_W)[None, :] // GDN_HEAD_DIM
    return (lane == head).astype(F32), (lane == head + GDN_HEADS).astype(F32)


def _lane16(v8):
    return jnp.pad(v8.astype(F32), (GDN_HEADS, LANES - 2 * GDN_HEADS)).reshape(1, LANES)


def _stack_heads(t, nb):
    return t.reshape(nb, SWA_BLOCK, SWA_KV_HEADS, SWA_GRP, SWA_HEAD_DIM).transpose(2, 0, 3, 1, 4).reshape(
        SWA_KV_HEADS, nb * SWA_GRP * SWA_BLOCK, SWA_HEAD_DIM)


def _unstack_heads(t, nb):
    return t.reshape(SWA_KV_HEADS, nb, SWA_GRP, SWA_BLOCK, SWA_HEAD_DIM).transpose(1, 3, 0, 2, 4).reshape(nb * SWA_BLOCK, SWA_Q)


def _kv_heads(t):
    return t.reshape(t.shape[0], SWA_KV_HEADS, SWA_HEAD_DIM).transpose(1, 0, 2)


def _swa_specs(qs, ks, vs, bp, bc, sk, grad, gdt=(F32,)):
    T = ks.shape[1]
    qr = SWA_GRP * SWA_BLOCK
    g = lambda a: tuple(a.shape) if grad else None
    nk = SWA_KV_HEADS
    m3 = lambda j, n: (0, n, 0)
    h3 = lambda j, n: (0, jnp.maximum(n - 1, 0), 0)
    p3 = lambda j: (0, 0, 0)
    rows = [Row(qs, (nk, qr, SWA_HEAD_DIM), m3, gshape=g(qs), gmap=m3, gdt=gdt),
            Row(ks, (nk, SWA_BLOCK, SWA_HEAD_DIM), m3, (nk, SWA_BLOCK, SWA_HEAD_DIM), h3, g(ks), m3, gdt),
            Row(vs, (nk, SWA_BLOCK, SWA_HEAD_DIM), m3, (nk, SWA_BLOCK, SWA_HEAD_DIM), h3, g(vs), m3, gdt)]
    pars = [Par(bp, (nk, qr, SWA_BLOCK), p3, g(bp), p3), Par(bc, (nk, qr, SWA_BLOCK), p3, g(bc), p3),
            Par(sk, (nk, qr, 1), p3, g(sk), p3)]
    return rows, pars, T // SWA_BLOCK


class _LocalWeights:
    def __init__(self, W):
        self.W = W

    def w1(self):
        return self.W

    def rider_a(self):
        return None

    def w2(self, got):
        return self.W

    def rider_b(self):
        return None

    def w3(self, got):
        return self.W

    def rider_g(self, G):
        return None

    def g_done(self, got):
        pass

    def rider_last(self, G):
        return None

    def last_done(self, got):
        pass


def _fwd_bwd(x, mem, tgt, src):
    W = dict(src.w1())
    T = x.shape[0]
    nb = T // SWA_BLOCK
    tb = min(256, T)
    tbl = min(512, T)
    fwd = lambda f: (lambda *a: (f(*a), []))
    full = lambda cols, dt, t, cw: Out((T, cols), dt, (t, cw), lambda j, n: (n, j))

    xb = x.astype(_CDT)
    ra = src.rider_a()
    proj = _mm("proj", xb, W["in_p"], "nn", rider=ra)
    proj, got = proj if ra is not None else (proj, None)
    W.update(src.w2(got))

    onehot_t = _bias_onehot()
    bias_flat = _mm("swa_bias", W["rel_bias"].T, onehot_t, "nn", hi=True)
    half = SWA_BLOCK * SWA_BLOCK
    bp = bias_flat[:, :half].reshape(SWA_KV_HEADS, SWA_GRP * SWA_BLOCK, SWA_BLOCK)
    bc = bias_flat[:, half:].reshape(SWA_KV_HEADS, SWA_GRP * SWA_BLOCK, SWA_BLOCK)
    sk = jnp.broadcast_to(W["swa_sinks"].reshape(SWA_KV_HEADS, SWA_GRP, 1, 1), (SWA_KV_HEADS, SWA_GRP, SWA_BLOCK, 1)).reshape(
        SWA_KV_HEADS, SWA_GRP * SWA_BLOCK, 1)
    qs = _stack_heads(proj[:, P_SQ:P_SQ + SWA_Q], nb)
    ks = _kv_heads(proj[:, P_SK:P_SK + SWA_KV])
    vs = _kv_heads(proj[:, P_SV:P_SV + SWA_KV])
    rows, pars, nblk = _swa_specs(qs, ks, vs, bp, bc, sk, False)
    o_s, = _rowmap("swa_fwd", fwd(_swa_fn), 1, nblk, rows, pars,
                   [Out(tuple(qs.shape), F32, (SWA_KV_HEADS, SWA_GRP * SWA_BLOCK, SWA_HEAD_DIM), lambda j, n: (0, n, 0))])
    o_swa = _unstack_heads(o_s, nb).astype(_CDT)

    ncq = 3 * GDN_W // LANES
    tbp = min(1024, T)
    pre_rows = lambda grad: [_rowspec(proj, tbp, LANES, P_GQKV // LANES, halo=SUBLANES, grad=grad, ncol=ncq, gdt=(_CDT,))]
    pre_pars = lambda grad: [_parspec(W["gdn_conv_w"], LANES, 0, grad=grad, ncol=ncq)]
    qkv_n, = _rowmap("gdn_pre_fwd", fwd(_gdn_pre_fn), ncq, T // tbp, pre_rows(False), pre_pars(False),
                     [full(3 * GDN_W, F32, tbp, LANES)])
    eb, eg = _head_spread()
    alog_row, dtb_row = _lane16(W["gdn_a_log"]), _lane16(W["gdn_dt_bias"])
    gate_rows = lambda grad: [_rowspec(proj, tbl, LANES, P_BA // LANES, cstep=0, grad=grad, gdt=(_CDT,))]
    gate_pars = lambda grad: [_parspec(alog_row, grad=grad), _parspec(dtb_row, grad=grad), _parspec(eb), _parspec(eg)]
    bx, gx = _rowmap("gdn_gate_fwd", fwd(_gdn_gate_fn), 1, T // tbl, gate_rows(False), gate_pars(False),
                     [full(GDN_W, F32, tbl, GDN_W), full(GDN_W, F32, tbl, GDN_W)])
    nw = W["gdn_norm_w"].reshape(1, GDN_HEAD_DIM)
    o_gdn, states, got = _gdn_chunks_fwd(qkv_n, bx, gx, proj, nw, rider=src.rider_b())
    W.update(src.w3(got))

    ys = _mm("y_swa", o_swa, W["br_swa"], "nn")
    yg = _mm("y_gdn", o_gdn, W["br_gdn"], "nn")
    cwm = 512
    mix_rows = lambda grad: [_rowspec(proj, tb, cwm, P_GS // cwm, grad=grad, ncol=D_MODEL // cwm, gdt=(_CDT,)),
                             _rowspec(proj, tb, cwm, P_GG // cwm, grad=grad, ncol=D_MODEL // cwm, gdt=(_CDT,)),
                             _rowspec(ys, tb, cwm, 0, grad=grad, ncol=D_MODEL // cwm, gdt=(_CDT,)),
                             _rowspec(yg, tb, cwm, 0, grad=grad, ncol=D_MODEL // cwm, gdt=(_CDT,))]
    mixed, = _rowmap("mix_fwd", fwd(_mix_fn), D_MODEL // cwm, T // tb, mix_rows(False), [], [full(D_MODEL, _CDT, tb, cwm)])
    r1 = _mm("r1", mixed, W["mix_o"], "nn", add=x, add_scale=ALPHA)

    def ln_fwd(name, r, g, b):
        return _rowmap(name, _ln_fwd_fn, 1, T // tb, [_rowspec(r, tb, D_MODEL, 0)], [_parspec(g), _parspec(b)],
                       [full(D_MODEL, F32, tb, D_MODEL), full(D_MODEL, _CDT, tb, D_MODEL)])

    def ln_bwd(name, r, g, b, ct):
        return _rowmap_bwd(name, _ln_fn, 1, T // tb, [_rowspec(r, tb, D_MODEL, 0, grad=True, gdt=(F32, _CDT))],
                           [_parspec(g, grad=True), _parspec(b, grad=True)], [_rowspec(ct, tb, D_MODEL, 0)])

    g1, b1 = W["ln1_g"].reshape(1, -1), W["ln1_b"].reshape(1, -1)
    g2, b2 = W["ln2_g"].reshape(1, -1), W["ln2_b"].reshape(1, -1)
    g3, b3 = W["ln3_g"].reshape(1, -1), W["ln3_b"].reshape(1, -1)
    x1, x1b = ln_fwd("ln1_fwd", r1, g1, b1)

    qm = _mm("mem_q", x1b, W["mem_q"], "nn")
    kvm = _mm("mem_kv", mem, W["mem_kv"], "nn")
    ma_rows = lambda grad: [_rowspec(qm, tbl, MEM_HEAD_DIM, 0, grad=grad, ncol=MEM_HEADS, gdt=(_CDT,))]
    ma_pars = lambda grad: [_parspec(kvm, MEM_HEAD_DIM, 0, grad=grad, ncol=MEM_HEADS),
                            _parspec(kvm, MEM_HEAD_DIM, MEM_HEADS, grad=grad, ncol=MEM_HEADS)]
    om, = _rowmap("memattn_fwd", fwd(_memattn_fn), MEM_HEADS, T // tbl, ma_rows(False), ma_pars(False),
                  [full(MEM_W, _CDT, tbl, MEM_HEAD_DIM)])
    r2 = _mm("r2", om, W["mem_o"], "nn", add=x1, add_scale=ALPHA)
    x2, x2b = ln_fwd("ln2_fwd", r2, g2, b2)

    hcat = _mm("ffn_up", x2b, W["up_p"], "nn")
    cwf = 512
    ncf = D_FF_PAD // cwf
    cw_p, cb_p = W["ffn_conv_w_p"], W["ffn_conv_b_p"]
    tbf = min(512, T)
    ffn_rows = lambda grad: [_rowspec(hcat, tbf, cwf, 0, halo=SUBLANES, grad=grad, ncol=ncf, gdt=(_CDT,)),
                             _rowspec(hcat, tbf, cwf, ncf, halo=SUBLANES, grad=grad, ncol=ncf, gdt=(_CDT,))]
    ffn_pars = lambda grad: [_parspec(cw_p, cwf, 0, grad=grad, ncol=ncf), _parspec(cw_p, cwf, ncf, grad=grad, ncol=ncf),
                             _parspec(cb_p, cwf, 0, grad=grad, ncol=ncf), _parspec(cb_p, cwf, ncf, grad=grad, ncol=ncf)]
    act, = _rowmap("ffn_act_fwd", fwd(_ffn_act_fn), ncf, T // tbf, ffn_rows(False), ffn_pars(False), [full(D_FF_PAD, _CDT, tbf, cwf)])
    r3 = _mm("r3", act, W["down_p"], "nn", add=x2, add_scale=ALPHA)
    dr3, dr3b, lacc, dg3, db3 = _rowmap("ln3_loss", _loss_fn, 1, T // tb, [_rowspec(r3, tb, D_MODEL, 0), _rowspec(tgt, tb, D_MODEL, 0)],
                                        [_parspec(g3), _parspec(b3)], [full(D_MODEL, F32, tb, D_MODEL), full(D_MODEL, _CDT, tb, D_MODEL)],
                                  accs=[(SUBLANES, LANES), (1, D_MODEL), (1, D_MODEL)])
    loss = lacc[0, 0]

    G = {}
    G["down_p"] = _mm("dw_down", act, dr3b, "tn", out_dtype=_GDT)
    dact = _mm("d_act", dr3b, W["down_p"], "nt")
    dhg, dhu, dcwg, dcwu, dcbg, dcbu = _rowmap_bwd("ffn_act_bwd", _ffn_act_fn, ncf, T // tbf, ffn_rows(True), ffn_pars(True),
                                                   [_rowspec(dact, tbf, cwf, 0)])
    w_gate, w_upp = W["up_p"][:, :D_FF_PAD], W["up_p"][:, D_FF_PAD:]
    dx2 = _mm("dx2_gate", dhg, w_gate, "nt", add=dr3, add_scale=ALPHA)
    dx2 = _mm("dx2_up", dhu, w_upp, "nt", add=dx2)
    G["up_p"] = jnp.concatenate([_mm("dw_gate", x2b, dhg, "tn", out_dtype=_GDT), _mm("dw_up", x2b, dhu, "tn", out_dtype=_GDT)], axis=1)
    G["ffn_conv_w"] = jnp.concatenate([dcwg[:, :D_FF], dcwu[:, :D_FF]], axis=1)
    G["ffn_conv_b"] = jnp.concatenate([dcbg[0, :D_FF], dcbu[0, :D_FF]])
    G["ln3_g"], G["ln3_b"] = dg3[0], db3[0]

    dr2, dr2b, dg2, db2 = ln_bwd("ln2_bwd", r2, g2, b2, dx2)
    G["ln2_g"], G["ln2_b"] = dg2[0], db2[0]
    G["mem_o"] = _mm("dw_mem_o", om, dr2b, "tn", out_dtype=_GDT)
    dom = _mm("d_om", dr2b, W["mem_o"], "nt", out_dtype=_CDT)
    dqm, dkm, dvm = _rowmap_bwd("memattn_bwd", _memattn_fn, MEM_HEADS, T // tbl, ma_rows(True), ma_pars(True),
                                [_rowspec(dom, tbl, MEM_HEAD_DIM, 0)])
    G["mem_kv"] = _mm("dw_mem_kv", mem.astype(_CDT), jnp.concatenate([dkm, dvm], axis=1).astype(_CDT), "tn", out_dtype=_GDT)
    G["mem_q"] = _mm("dw_mem_q", x1b, dqm, "tn", out_dtype=_GDT)
    dx1 = _mm("dx1", dqm, W["mem_q"], "nt", add=dr2, add_scale=ALPHA)

    dr1, dr1b, dg1, db1 = ln_bwd("ln1_bwd", r1, g1, b1, dx1)
    G["ln1_g"], G["ln1_b"] = dg1[0], db1[0]
    G["mix_o"] = _mm("dw_mix_o", mixed, dr1b, "tn", out_dtype=_GDT)
    dmixed = _mm("d_mixed", dr1b, W["mix_o"], "nt")
    dgs, dgg, dys, dyg = _rowmap_bwd("mix_bwd", _mix_fn, D_MODEL // cwm, T // tb, mix_rows(True), [], [_rowspec(dmixed, tb, cwm, 0)])
    G["br_swa"] = _mm("dw_br_swa", o_swa, dys, "tn", out_dtype=_GDT)
    G["br_gdn"] = _mm("dw_br_gdn", o_gdn, dyg, "tn", out_dtype=_GDT)
    do_swa = _mm("d_o_swa", dys, W["br_swa"], "nt", out_dtype=_CDT)
    do_gdn = _mm("d_o_gdn", dyg, W["br_gdn"], "nt")

    rows, pars, nblk = _swa_specs(qs, ks, vs, bp, bc, sk, True, (_CDT,))
    m3 = lambda j, n: (0, n, 0)
    dqs, dks, dvs, dbp, dbc, dsk = _rowmap_bwd("swa_bwd", _swa_fn, 1, nblk, rows, pars,
                                               [Row(_stack_heads(do_swa, nb), (SWA_KV_HEADS, SWA_GRP * SWA_BLOCK, SWA_HEAD_DIM), m3)])
    d_swa = jnp.concatenate([_unstack_heads(dqs, nb), dks.transpose(1, 0, 2).reshape(T, SWA_KV),
                             dvs.transpose(1, 0, 2).reshape(T, SWA_KV)], axis=1)
    dbias = jnp.concatenate([dbp.reshape(SWA_HEADS, half), dbc.reshape(SWA_HEADS, half)], axis=1)
    G["rel_bias"] = _mm("d_rel_bias", dbias, onehot_t.T, "nn", hi=True).T
    G["swa_sinks"] = _mm("d_sinks", dsk.reshape(SWA_HEADS, SWA_BLOCK), jnp.ones((SWA_BLOCK, LANES), F32), "nn", hi=True)[:, 0]

    dqkv_n, dbx, dgx, dz, dnw, got = _gdn_chunks_bwd(qkv_n, bx, gx, proj, nw, states, do_gdn, rider=src.rider_g(G))
    src.g_done(got)
    G["gdn_norm_w"] = dnw[0]
    dgba, dalog, ddtb = _rowmap_bwd("gdn_gate_bwd", _gdn_gate_fn, 1, T // tbl, gate_rows(True), gate_pars(True),
                                    [_rowspec(dbx, tbl, GDN_W, 0), _rowspec(dgx, tbl, GDN_W, 0)])
    G["gdn_a_log"], G["gdn_dt_bias"] = dalog[0, GDN_HEADS:2 * GDN_HEADS], ddtb[0, GDN_HEADS:2 * GDN_HEADS]
    dgqkv, dcw_gdn = _rowmap_bwd("gdn_pre_bwd", _gdn_pre_fn, ncq, T // tbp, pre_rows(True), pre_pars(True),
                                 [_rowspec(dqkv_n, tbp, LANES, 0)])
    G["gdn_conv_w"] = dcw_gdn

    dproj = jnp.concatenate([dgs, dgg, dgqkv, dz, d_swa, dgba, jnp.zeros((T, P_END - P_USED), _CDT)], axis=1)
    G["in_p"] = _mm("dw_in", xb, dproj, "tn", out_dtype=_GDT)
    rl = src.rider_last(G)
    dx = _mm("dx", dproj, W["in_p"], "nt", add=dr1, add_scale=ALPHA, rider=rl)
    if rl is not None:
        dx, got = dx
        src.last_done(got)
    return loss, dx, G


W_NAMES = ["w_in", "rel_bias", "swa_sinks", "gdn_conv_w", "gdn_a_log", "gdn_dt_bias", "gdn_norm_w", "w_br_swa", "w_br_gdn",
           "w_mix_o", "ln1_g", "ln1_b", "w_mem_q", "w_mem_kv", "w_mem_o", "ln2_g", "ln2_b", "w_up", "ffn_conv_w", "ffn_conv_b",
           "w_down", "ln3_g", "ln3_b"]
BIG = ["w_in", "w_br_swa", "w_br_gdn", "w_mix_o", "w_mem_q", "w_mem_kv", "w_mem_o", "w_up", "w_down"]
SMALL = [n for n in W_NAMES if n not in BIG]
COL_SHARDED = ["w_in", "w_br_swa", "w_br_gdn", "w_mem_o", "w_up"]


def _pack(arrs):
    rows = []
    for a in arrs:
        f = a.reshape(-1).astype(F32)
        rows.append(jnp.pad(f, (0, (-f.shape[0]) % LANES)).reshape(-1, LANES))
    n = sum(r.shape[0] for r in rows)
    if n % 16:
        rows.append(jnp.zeros((16 - n % 16, LANES), F32))
    return jnp.concatenate(rows, axis=0)


def _unpack(p, shapes):
    out, off = [], 0
    for s in shapes:
        n = int(np.prod(s)) if len(s) else 1
        r = -(-n // LANES)
        out.append(p[off:off + r].reshape(-1)[:n].reshape(s))
        off += r
    return out


def _merge_shards(d):
    cat = lambda names: jnp.concatenate([d[n] for n in names], axis=-2)
    return [d.get("w_in"), d["w_up"], cat(["w_br_swa", "w_br_gdn", "w_mem_q", "w_mem_o"]), cat(["w_mix_o", "w_down"]), d["w_mem_kv"]]


def _split_shards(ts):
    a, b, c, dd, e = ts
    return {"w_in": a, "w_up": b, "w_br_swa": c[..., 0:1024, :], "w_br_gdn": c[..., 1024:2048, :], "w_mem_q": c[..., 2048:2560, :],
            "w_mem_o": c[..., 2560:3072, :], "w_mix_o": dd[..., 0:512, :], "w_down": dd[..., 512:, :], "w_mem_kv": e}


def _to_full(name, t):
    if name in COL_SHARDED:
        return _cols_from_chips(t, [(0, 4 * t.shape[2])])
    return t.reshape(4 * t.shape[1], t.shape[2])


def _to_chips(name, t):
    if name in COL_SHARDED:
        return _chips_from_cols(t, [(0, t.shape[1])], t.shape[1] // 4)
    return t.reshape(4, t.shape[0] // 4, t.shape[1])


def _cols_from_chips(g, segs):
    C, parts = g.shape[2], []
    for s in segs:
        if isinstance(s, int):
            parts.append(jnp.zeros((g.shape[1], s), g.dtype))
            continue
        lo, hi = s
        while lo < hi:
            k = lo // C
            e = min(hi, (k + 1) * C)
            parts.append(g[k][:, lo - k * C:e - k * C])
            lo = e
    return jnp.concatenate(parts, axis=1)


def _chips_from_cols(p, segs, C):
    out = []
    for k in range(4):
        lo, hi, parts, o = k * C, (k + 1) * C, [], 0
        for plo, w in segs:
            a, b = max(lo, o), min(hi, o + w)
            if a < b:
                parts.append(p[:, plo + a - o:plo + b - o])
            o += w
        out.append(jnp.concatenate(parts, axis=1))
    return jnp.stack(out)


_IN_OFF = np.cumsum((0,) + IN_WIDTHS)
_IN_SEGS = [(P_SQ, SWA_Q), (P_SK, SWA_KV), (P_SV, SWA_KV), (P_GQKV, 3 * GDN_W), (P_GZ, GDN_W), (P_BA, 2 * GDN_HEADS),
            (P_GS, D_MODEL), (P_GG, D_MODEL)]
_IN_PADDED = [(int(_IN_OFF[i]), int(_IN_OFF[k])) for i, k in ((9, 10), (10, 11), (3, 6), (6, 7), (0, 1), (1, 2), (2, 3), (7, 9))] + [
    P_END - P_BA - 2 * GDN_HEADS]
_UP_SEGS = [(0, D_FF), (D_FF_PAD, D_FF)]
_UP_PADDED = [(0, D_FF), D_FF_PAD - D_FF, (D_FF, 2 * D_FF), D_FF_PAD - D_FF]


def _in_to_padded(w):
    o = _IN_OFF
    cut = lambda i, k: w[:, o[i]:o[k]]
    return jnp.concatenate([cut(9, 10), cut(10, 11), cut(3, 6), cut(6, 7), cut(0, 1), cut(1, 2), cut(2, 3), cut(7, 9),
                            jnp.zeros((w.shape[0], P_END - P_BA - 2 * GDN_HEADS), w.dtype)], axis=1)


def _in_from_padded(p):
    return jnp.concatenate([p[:, P_SQ:P_SQ + SWA_Q], p[:, P_SK:P_SK + SWA_KV], p[:, P_SV:P_SV + SWA_KV], p[:, P_GQKV:P_GQKV + 3 * GDN_W],
                            p[:, P_GZ:P_GZ + GDN_W], p[:, P_BA:P_BA + 2 * GDN_HEADS], p[:, P_GS:P_GS + D_MODEL], p[:, P_GG:P_GG + D_MODEL]],
                           axis=1)


def _ff_pad(t, axis):
    g, u = jnp.split(t, 2, axis=axis)
    pad = [(0, 0)] * t.ndim
    pad[axis] = (0, D_FF_PAD - D_FF)
    return jnp.concatenate([jnp.pad(g, pad), jnp.pad(u, pad)], axis=axis)


def _ff_unpad(t, axis):
    g, u = jnp.split(t, 2, axis=axis)
    return jnp.concatenate([lax.slice_in_dim(g, 0, D_FF, axis=axis), lax.slice_in_dim(u, 0, D_FF, axis=axis)], axis=axis)


def _assemble_weights(full, small):
    W = dict(small)
    W["in_p"] = _in_to_padded(full["w_in"])
    W["up_p"] = _ff_pad(full["w_up"], 1)
    W["down_p"] = jnp.pad(full["w_down"], ((0, D_FF_PAD - D_FF), (0, 0)))
    W["br_swa"], W["br_gdn"], W["mix_o"] = full["w_br_swa"], full["w_br_gdn"], full["w_mix_o"]
    W["mem_q"], W["mem_kv"], W["mem_o"] = full["w_mem_q"], full["w_mem_kv"], full["w_mem_o"]
    W["ffn_conv_w_p"] = _ff_pad(small["ffn_conv_w"], 1)
    W["ffn_conv_b_p"] = _ff_pad(small["ffn_conv_b"].reshape(1, -1), 1)
    return W


def _full_grads(G):
    out = {"w_in": _in_from_padded(G["in_p"])} if "in_p" in G else {}
    out.update({"w_up": _ff_unpad(G["up_p"], 1), "w_down": G["down_p"][:D_FF], "w_br_swa": G["br_swa"], "w_br_gdn": G["br_gdn"],
                "w_mix_o": G["mix_o"], "w_mem_q": G["mem_q"], "w_mem_kv": G["mem_kv"], "w_mem_o": G["mem_o"]})
    return out


def kernel(x, mem, w_in, rel_bias, swa_sinks, gdn_conv_w, gdn_a_log, gdn_dt_bias, gdn_norm_w, w_br_swa, w_br_gdn, w_mix_o, ln1_g, ln1_b, w_mem_q, w_mem_kv, w_mem_o, ln2_g, ln2_b, w_up, ffn_conv_w, ffn_conv_b, w_down, ln3_g, ln3_b, loss_target, m_w_in, m_rel_bias, m_swa_sinks, m_gdn_conv_w, m_gdn_a_log, m_gdn_dt_bias, m_gdn_norm_w, m_w_br_swa, m_w_br_gdn, m_w_mix_o, m_ln1_g, m_ln1_b, m_w_mem_q, m_w_mem_kv, m_w_mem_o, m_ln2_g, m_ln2_b, m_w_up, m_ffn_conv_w, m_ffn_conv_b, m_w_down, m_ln3_g, m_ln3_b, v_w_in, v_rel_bias, v_swa_sinks, v_gdn_conv_w, v_gdn_a_log, v_gdn_dt_bias, v_gdn_norm_w, v_w_br_swa, v_w_br_gdn, v_w_mix_o, v_ln1_g, v_ln1_b, v_w_mem_q, v_w_mem_kv, v_w_mem_o, v_ln2_g, v_ln2_b, v_w_up, v_ffn_conv_w, v_ffn_conv_b, v_w_down, v_ln3_g, v_ln3_b):
    a = dict(locals())
    w = {n: a[n] for n in W_NAMES}
    m = {n: a["m_" + n] for n in W_NAMES}
    v = {n: a["v_" + n] for n in W_NAMES}
    chip = 2 * lax.axis_index("x") + lax.axis_index("y")
    core = lax.axis_index("c")
    sq = lambda t: t.reshape(t.shape[1:]) if (t.ndim > 1 and t.shape[0] == 1 and t is not rel_bias) else t

    sh_a, sh_b, sh_c, sh_d, sh_e = _merge_shards({n: sq(w[n]).astype(_CDT) for n in BIG})
    fcw_sh, gcw_sh = sq(ffn_conv_w).shape, sq(gdn_conv_w).shape
    slot = lax.broadcasted_iota(jnp.int32, (4, 1, 1), 0)

    def with_own(got, mine):
        return [jnp.where(slot == chip, t[None], g) for g, t in zip(got, mine)]

    def reduce_start(tag, gch):
        pair = []
        for t, (mine, got) in enumerate(zip(gch, _pair_swap(tag, gch))):
            rh = mine.shape[1] // 2
            mine_h = lax.dynamic_slice_in_dim(mine, core * rh, rh, axis=1)
            pair.append(_addn(f"pair_sum_{tag}{t}", [mine_h.reshape(4 * rh, -1), got.reshape(4 * rh, -1)], _GDT).reshape(4, rh, -1))
        return pair

    def reduce_end(tag, pair, others):
        halves = []
        for t, (p, o) in enumerate(zip(pair, others)):
            own = lax.dynamic_index_in_dim(p, chip, 0, keepdims=False)
            halves.append(_addn(f"chip_sum_{tag}{t}", [own, (o, 0), (o, 1), (o, 2)]))
        return halves

    class MeshWeights:
        def w1(self):
            mine = [sh_a, _pack([sq(ffn_conv_w), sq(gdn_conv_w)])]
            got_a, got_f = with_own(_run_rider("gather_first", _gather_rider(mine)), mine)
            conv = [_unpack(got_f[k], [fcw_sh, gcw_sh]) for k in range(4)]
            W = {n: sq(w[n]) for n in SMALL}
            W["ffn_conv_w"] = jnp.concatenate([cv[0] for cv in conv], axis=1)
            W["gdn_conv_w"] = jnp.concatenate([cv[1] for cv in conv], axis=1)
            W["ffn_conv_w_p"] = _ff_pad(W["ffn_conv_w"], 1)
            W["ffn_conv_b_p"] = _ff_pad(W["ffn_conv_b"].reshape(1, -1), 1)
            W["in_p"] = _cols_from_chips(got_a, _IN_PADDED)
            return W

        def rider_a(self):
            return _gather_rider([sh_c, sh_d, sh_e])

        def w2(self, got):
            c, d, e = with_own(got, [sh_c, sh_d, sh_e])
            f = {n: _to_full(n, t) for n, t in _split_shards([None, None, c, d, e]).items() if t is not None}
            return {"br_swa": f["w_br_swa"], "br_gdn": f["w_br_gdn"], "mix_o": f["w_mix_o"], "mem_q": f["w_mem_q"], "mem_kv": f["w_mem_kv"],
                    "mem_o": f["w_mem_o"], "down_p": jnp.pad(f["w_down"], ((0, D_FF_PAD - D_FF), (0, 0)))}

        def rider_b(self):
            return _gather_rider([sh_b])

        def w3(self, got):
            b, = with_own(got, [sh_b])
            return {"up_p": _cols_from_chips(b, _UP_PADDED)}

        def rider_g(self, G):
            gf = _full_grads(G)
            gch = {n: _to_chips(n, gf[n]) for n in BIG if n not in ("w_in", "w_up")}
            gch["w_up"] = _chips_from_cols(G["up_p"], _UP_SEGS, 2 * D_FF // 4)
            self.pair = reduce_start("rest", _merge_shards(gch)[1:])
            return _scatter_rider(self.pair)

        def g_done(self, got):
            self.halves = reduce_end("rest", self.pair, got)

        def rider_last(self, G):
            self.pair_in = reduce_start("in", [_chips_from_cols(G["in_p"], _IN_SEGS, sum(IN_WIDTHS) // 4)])
            return _scatter_rider(self.pair_in)

        def last_done(self, got):
            self.halves = reduce_end("in", self.pair_in, got) + self.halves

    src = MeshWeights()
    loss, dx, G = _fwd_bwd(x[0], mem[0], loss_target[0], src)

    small_names = SMALL
    small_shapes = [()] + [tuple(G[n].shape) for n in small_names]
    packed = _pack([loss] + [G[n] for n in small_names])
    allp = _allgather8(packed)
    tot = _addn("small_sum", [(allp, k) for k in range(8)])
    parts = _unpack(tot, small_shapes)
    loss_tot, gsmall = parts[0], dict(zip(small_names, parts[1:]))
    gsmall["ffn_conv_w"] = lax.dynamic_slice_in_dim(gsmall["ffn_conv_w"], chip * fcw_sh[1], fcw_sh[1], axis=1)
    gsmall["gdn_conv_w"] = lax.dynamic_slice_in_dim(gsmall["gdn_conv_w"], chip * gcw_sh[1], gcw_sh[1], axis=1)

    both = []
    for h, o in zip(src.halves, _pair_exchange(src.halves)):
        both.append(jnp.concatenate([jnp.where(core == 0, h, o), jnp.where(core == 0, o, h)], axis=0))
    gbig = _split_shards(both)

    outs = {}
    for n in BIG:
        d_, m_, v_ = _adamw("adamw_" + n, sq(w[n]), gbig[n], sq(m[n]), sq(v[n]))
        outs[n] = (gbig[n], d_, m_, v_)
    for n in SMALL:
        two_d = (-1, w[n].shape[-1])
        g_ = gsmall[n].reshape(two_d)
        d_, m_, v_ = _adamw("adamw_" + n, w[n].reshape(two_d), g_, m[n].reshape(two_d), v[n].reshape(two_d))
        outs[n] = (g_, d_, m_, v_)

    res = [loss_tot.reshape(()), dx.reshape(x.shape)]
    for k in range(4):
        res += [outs[n][k].reshape(w[n].shape) for n in W_NAMES]
    return tuple(res)
```

```python
import functools
import math

import jax
import jax.numpy as jnp
import numpy as np
from jax import lax
from jax.experimental import pallas as pl
from jax.experimental.pallas import tpu as pltpu

F32 = jnp.float32
BF16 = jnp.bfloat16
_CDT = BF16
_GDT = BF16

D_MODEL = 2048
SWA_HEADS, SWA_KV_HEADS, SWA_HEAD_DIM, SWA_BLOCK = 16, 2, 64, 128
SWA_GRP = SWA_HEADS // SWA_KV_HEADS
REL_BUCKETS, REL_MAX_DIST = 32, 128
GDN_HEADS, GDN_HEAD_DIM, GDN_CONV, GDN_CHUNK = 8, 128, 4, 64
MEM_HEADS, MEM_HEAD_DIM = 4, 128
D_FF, D_FF_PAD, FFN_CONV = 5504, 5632, 3
SWA_Q, SWA_KV, GDN_W, MEM_W = 1024, 128, 1024, 512
IN_WIDTHS = (SWA_Q, SWA_KV, SWA_KV, GDN_W, GDN_W, GDN_W, GDN_W, GDN_HEADS, GDN_HEADS, D_MODEL, D_MODEL)
NORM_EPS = 1e-5
ALPHA = 2.0 ** 0.25
NEG_INF = -1e30
ADAM_LR, ADAM_B1, ADAM_B2, ADAM_EPS, ADAM_WD, ADAM_STEP = 0.001, 0.9, 0.999, 1e-08, 0.01, 10
LANES, SUBLANES = 128, 8
VMEM_LIMIT = 56 * 1024 * 1024

P_GS, P_GG, P_GQKV, P_GZ, P_SQ, P_SK, P_SV, P_BA, P_USED, P_END = 0, 2048, 4096, 7168, 8192, 9216, 9344, 9472, 9600, 9728


def _tile(dim, pref, align=LANES):
    if dim <= pref:
        return dim
    t = (pref // align) * align
    while t >= align:
        if dim % t == 0:
            return t
        t -= align
    return dim


_DIMS = {"nn": (((1,), (0,)), ((), ())), "nt": (((1,), (1,)), ((), ())), "tn": (((0,), (0,)), ((), ()))}
_BDIMS = {"nn": (((2,), (1,)), ((0,), (0,))), "nt": (((2,), (2,)), ((0,), (0,))), "tn": (((1,), (1,)), ((0,), (0,)))}


def _raw_dot(a, b, form, hi):
    dims = (_BDIMS if a.ndim == 3 else _DIMS)[form]
    if hi == "x3":
        a, b = a.astype(F32), b.astype(F32)
        ah, bh = a.astype(BF16), b.astype(BF16)
        al, bl = (a - ah.astype(F32)).astype(BF16), (b - bh.astype(F32)).astype(BF16)
        d = lambda p, q: lax.dot_general(p, q, dims, preferred_element_type=F32)
        if form == "tn":
            return d(ah, bh) + (d(ah, bl) + d(al, bh))
        m = a.shape[-2]
        both = d(jnp.concatenate([ah, al], axis=-2), bh)
        return both[..., :m, :] + (d(ah, bl) + both[..., m:, :])
    if hi:
        return lax.dot_general(a.astype(F32), b.astype(F32), dims, precision=lax.Precision.HIGHEST, preferred_element_type=F32)
    return lax.dot_general(a.astype(_CDT), b.astype(_CDT), dims, preferred_element_type=F32)


@functools.partial(jax.custom_vjp, nondiff_argnums=(2, 3))
def _dot(a, b, form, hi=False):
    return _raw_dot(a, b, form, hi)


def _dot_fwd(a, b, form, hi):
    return _raw_dot(a, b, form, hi), (a, b)


def _dot_bwd(form, hi, res, g):
    a, b = res
    if form == "nn":
        da, db = _raw_dot(g, b, "nt", hi), _raw_dot(a, g, "tn", hi)
    elif form == "nt":
        da, db = _raw_dot(g, b, "nn", hi), _raw_dot(g, a, "tn", hi)
    else:
        da, db = _raw_dot(b, g, "nt", hi), _raw_dot(a, g, "nn", hi)
    return da.astype(a.dtype), db.astype(b.dtype)


_dot.defvjp(_dot_fwd, _dot_bwd)


@functools.partial(jax.custom_vjp, nondiff_argnums=(2,))
def _shift_halo(prev, cur, d):
    assert prev.shape[0] == SUBLANES
    return pltpu.roll(jnp.concatenate([prev, cur], axis=0), d, 0)[SUBLANES:]


def _shift_halo_fwd(prev, cur, d):
    return _shift_halo(prev, cur, d), None


def _shift_halo_bwd(d, _, g):
    nh = SUBLANES
    ext = jnp.concatenate([jnp.zeros((nh, g.shape[1]), g.dtype), g], axis=0)
    r = pltpu.roll(ext, ext.shape[0] - d, 0)
    return r[:nh], r[nh:]


_shift_halo.defvjp(_shift_halo_fwd, _shift_halo_bwd)


@jax.custom_vjp
def _recip(x):
    return 1.0 / x


def _recip_fwd(x):
    r = 1.0 / x
    return r, r


def _recip_bwd(r, g):
    return (-g * r * r,)


_recip.defvjp(_recip_fwd, _recip_bwd)


def _sigmoid(x):
    return _recip(1.0 + jnp.exp(-x))


def _silu(x):
    return x * _sigmoid(x)


def _softplus(x):
    return jnp.maximum(x, 0.0) + jnp.log(1.0 + jnp.exp(-jnp.abs(x)))


def _iota(shape, axis):
    return lax.broadcasted_iota(jnp.int32, shape, axis)


def _cparams(sem, **kw):
    return pltpu.CompilerParams(dimension_semantics=sem, vmem_limit_bytes=VMEM_LIMIT, **kw)


class _ride:
    def __init__(self, rider, n_in, n_out, n_scr):
        self.rider = rider
        self.ins = rider.ins if rider else []
        n_rin = len(self.ins)
        self.out_shapes = rider.out_shapes if rider else []
        n_rout = len(self.out_shapes)
        self.in_specs, self.out_specs = [_HBM] * n_rin, [_HBM] * n_rout
        self.scratch = rider.sems() if rider else []
        self.o0 = n_in + n_rin
        self.s0 = self.o0 + n_out + n_rout
        self._rin = slice(n_in, n_in + n_rin)
        self._rout = slice(self.o0 + n_out, self.s0)
        self._sem = self.s0 + n_scr

    def _args(self, refs):
        return refs[self._rin], refs[self._rout], refs[self._sem], refs[self._sem + 1]

    def at_start(self, refs, cond):
        if self.rider:
            pl.when(cond)(lambda: self.rider.start(*self._args(refs)))

    def at_end(self, refs, cond):
        if self.rider:
            pl.when(cond)(lambda: self.rider.finish(*self._args(refs)))


def _mm(name, a, b, form, out_dtype=F32, add=None, add_scale=1.0, hi=False, tm=1024, tn=1024, tk=2816, rider=None, b_k0=None):
    if form == "nn":
        (M, K), (K2, N) = a.shape, b.shape
    elif form == "nt":
        (M, K), (N, K2) = a.shape, b.shape
        K2 = K if b_k0 is not None else K2
    else:
        (K, M), (K2, N) = a.shape, b.shape
    assert K == K2, (name, a.shape, b.shape, form)
    tm, tn, tk = _tile(M, tm), _tile(N, tn), _tile(K, tk)
    nk = K // tk
    k0 = 0 if b_k0 is None else b_k0 // tk
    assert b_k0 is None or (form == "nt" and b_k0 % tk == 0)
    a_spec = pl.BlockSpec((tk, tm), lambda i, j, k: (k, i)) if form == "tn" else pl.BlockSpec((tm, tk), lambda i, j, k: (i, k))
    b_spec = pl.BlockSpec((tn, tk), lambda i, j, k: (j, k + k0)) if form == "nt" else pl.BlockSpec((tk, tn), lambda i, j, k: (k, j))
    o_spec = pl.BlockSpec((tm, tn), lambda i, j, k: (i, j))
    has_add = add is not None

    def finish(r, c_ref, o_ref):
        if has_add:
            r = r + add_scale * c_ref[...].astype(F32)
        o_ref[...] = r.astype(out_dtype)

    n_own = 3 if has_add else 2
    grid = (M // tm, N // tn, nk)
    rd = _ride(rider, n_own, 1, 1 if nk > 1 else 0)

    def body(*refs):
        a_ref, b_ref = refs[:2]
        c_ref = refs[2] if has_add else None
        o_ref = refs[rd.o0]
        pid = [pl.program_id(d) for d in range(3)]
        rd.at_start(refs, (pid[0] == 0) & (pid[1] == 0) & (pid[2] == 0))
        if nk == 1:
            finish(_raw_dot(a_ref[...], b_ref[...], form, hi), c_ref, o_ref)
        else:
            acc = refs[rd.s0]

            @pl.when(pid[2] == 0)
            def _():
                acc[...] = jnp.zeros_like(acc)

            acc[...] += _raw_dot(a_ref[...], b_ref[...], form, hi)

            @pl.when(pid[2] == nk - 1)
            def _():
                finish(acc[...], c_ref, o_ref)
        rd.at_end(refs, (pid[0] == grid[0] - 1) & (pid[1] == grid[1] - 1) & (pid[2] == nk - 1))

    ins = [a, b] + ([add] if has_add else [])
    specs = [a_spec, b_spec] + ([o_spec] if has_add else [])
    res = pl.pallas_call(
        body, name=name, grid=grid, in_specs=specs + rd.in_specs, out_specs=[o_spec] + rd.out_specs,
        out_shape=[jax.ShapeDtypeStruct((M, N), out_dtype)] + rd.out_shapes,
        scratch_shapes=([pltpu.VMEM((tm, tn), F32)] if nk > 1 else []) + rd.scratch,
        compiler_params=_cparams(("arbitrary",) * 3 if rider else ("parallel", "parallel", "arbitrary")),
    )(*ins, *rd.ins)
    return (res[0], res[1:]) if rider else res[0]


class Row:
    def __init__(self, arr, blk, imap, hblk=None, hmap=None, gshape=None, gmap=None, gdt=(F32,)):
        self.arr, self.blk, self.imap, self.hblk, self.hmap, self.gshape, self.gmap = arr, blk, imap, hblk, hmap, gshape, gmap
        self.gdt = gdt


class Par:
    def __init__(self, arr, blk=None, imap=None, gshape=None, gmap=None):
        self.arr = arr
        self.blk = tuple(arr.shape) if blk is None else blk
        nd = len(self.blk)
        self.imap = (lambda j: (0,) * nd) if imap is None else imap
        self.gshape, self.gmap = gshape, gmap


class Out:
    def __init__(self, shape, dtype, blk, imap):
        self.shape, self.dtype, self.blk, self.imap = shape, dtype, blk, imap


def _rows_of(blk):
    return [d for d in blk if d is not None][0]


def _rowmap(name, fn, ncol, nblk, rows, pars, outs, accs=()):
    in_specs, ins = [], []
    for r in rows:
        ins.append(r.arr)
        in_specs.append(pl.BlockSpec(r.blk, r.imap))
        if r.hblk is not None:
            ins.append(r.arr)
            in_specs.append(pl.BlockSpec(r.hblk, r.hmap))
    for p in pars:
        ins.append(p.arr)
        in_specs.append(pl.BlockSpec(p.blk, (lambda im: (lambda j, n: im(j)))(p.imap)))
    out_specs = [pl.BlockSpec(o.blk, o.imap) for o in outs]
    out_shape = [jax.ShapeDtypeStruct(o.shape, o.dtype) for o in outs]
    for a in accs:
        out_specs.append(pl.BlockSpec(a, (lambda nd: (lambda j, n: (0,) * nd))(len(a))))
        out_shape.append(jax.ShapeDtypeStruct(a, F32))
    n_in = len(ins)

    def body(*refs):
        j, n = pl.program_id(0), pl.program_id(1)
        it = iter(refs[:n_in])
        rvals = []
        for r in rows:
            cur = next(it)[...]
            rvals.append((next(it)[...], cur) if r.hblk is not None else cur)
        pvals = [next(it)[...] for _ in pars]
        o_refs = refs[n_in:n_in + len(outs)]
        a_refs = refs[n_in + len(outs):]
        ovals, avals = fn(j, n == 0, rvals, pvals)
        for ref, v in zip(o_refs, ovals):
            ref[...] = v.astype(ref.dtype)
        if accs:
            @pl.when((j == 0) & (n == 0))
            def _():
                for ref in a_refs:
                    ref[...] = jnp.zeros_like(ref)
            for ref, v in zip(a_refs, avals):
                ref[...] += v

    res = pl.pallas_call(
        body, name=name, grid=(ncol, nblk), in_specs=in_specs, out_specs=out_specs, out_shape=out_shape,
        compiler_params=_cparams(("arbitrary", "arbitrary")),
    )(*ins)
    return res


def _rowmap_bwd(name, fn, ncol, nblk, rows, pars, cts):
    rev = lambda im: (lambda j, s: im(j, nblk - 1 - s))
    in_specs, ins = [], []
    for r in rows:
        ins.append(r.arr)
        in_specs.append(pl.BlockSpec(r.blk, rev(r.imap)))
        if r.hblk is not None:
            ins.append(r.arr)
            in_specs.append(pl.BlockSpec(r.hblk, rev(r.hmap)))
    for p in pars:
        ins.append(p.arr)
        in_specs.append(pl.BlockSpec(p.blk, (lambda im: (lambda j, s: im(j)))(p.imap)))
    for c in cts:
        ins.append(c.arr)
        in_specs.append(pl.BlockSpec(c.blk, rev(c.imap)))
    n_in = len(ins)
    drows = [i for i, r in enumerate(rows) if r.gshape is not None]
    dpars = [i for i, p in enumerate(pars) if p.gshape is not None]
    out_specs, out_shape, scratch = [], [], []
    for i in drows:
        r = rows[i]
        for dt in r.gdt:
            out_specs.append(pl.BlockSpec(r.blk, rev(r.gmap)))
            out_shape.append(jax.ShapeDtypeStruct(r.gshape, dt))
        if r.hblk is not None:
            scratch.append(pltpu.VMEM(tuple(d for d in r.hblk if d is not None), F32))
    n_drow_out = len(out_specs)
    for i in dpars:
        p = pars[i]
        out_specs.append(pl.BlockSpec(p.blk, (lambda im: (lambda j, s: im(j)))(p.gmap)))
        out_shape.append(jax.ShapeDtypeStruct(p.gshape, F32))

    def body(*refs):
        j, s = pl.program_id(0), pl.program_id(1)
        first = s == nblk - 1
        it = iter(refs[:n_in])
        rvals = []
        for r in rows:
            cur = next(it)[...]
            rvals.append((next(it)[...], cur) if r.hblk is not None else cur)
        pvals = [next(it)[...] for _ in pars]
        cvals = [next(it)[...].astype(F32) for _ in cts]
        g_refs = iter(refs[n_in:n_in + n_drow_out])
        p_refs = refs[n_in + n_drow_out:n_in + n_drow_out + len(dpars)]
        carries = iter(refs[n_in + n_drow_out + len(dpars):])

        def f(dr, dp):
            rv, pv = list(rvals), list(pvals)
            for i, v in zip(drows, dr):
                rv[i] = v
            for i, v in zip(dpars, dp):
                pv[i] = v
            return fn(j, first, rv, pv)

        _, vjp = jax.vjp(f, [rvals[i] for i in drows], [pvals[i] for i in dpars])
        g_r, g_p = vjp(cvals)
        for i, g in zip(drows, g_r):
            r = rows[i]
            if r.hblk is None:
                for _ in r.gdt:
                    ref = next(g_refs)
                    ref[...] = g.astype(ref.dtype)
            else:
                g_prev, g_cur = g
                carry = next(carries)
                nr, nh = g_cur.shape[-2], g_prev.shape[-2]
                tail = g_cur[..., nr - nh:nr, :] + jnp.where(s > 0, carry[...], 0.0)
                for _ in r.gdt:
                    ref = next(g_refs)
                    if nr > nh:
                        ref[..., 0:nr - nh, :] = g_cur[..., 0:nr - nh, :].astype(ref.dtype)
                    ref[..., nr - nh:nr, :] = tail.astype(ref.dtype)
                carry[...] = g_prev
        for ref, g in zip(p_refs, g_p):
            @pl.when(s == 0)
            def _():
                ref[...] = jnp.zeros_like(ref)
            ref[...] += g

    return pl.pallas_call(
        body, name=name, grid=(ncol, nblk), in_specs=in_specs, out_specs=out_specs, out_shape=out_shape,
        scratch_shapes=scratch, compiler_params=_cparams(("arbitrary", "arbitrary")),
    )(*ins)


def _rowspec(arr, tb, cw, c0, cstep=1, halo=0, grad=False, ncol=1, gdt=(F32,)):
    T = arr.shape[0]
    imap = lambda j, n: (n, c0 + cstep * j)
    hblk = hmap = None
    if halo:
        q = tb // halo
        hblk, hmap = (halo, cw), (lambda j, n: (jnp.maximum(n * q - 1, 0), c0 + cstep * j))
    gshape = (T, cw * (ncol if cstep else 1)) if grad else None
    gmap = (lambda j, n: (n, cstep * j)) if grad else None
    return Row(arr, (tb, cw), imap, hblk, hmap, gshape, gmap, gdt)


def _parspec(arr, cw=None, c0=0, grad=False, ncol=1):
    if cw is None:
        return Par(arr, gshape=tuple(arr.shape) if grad else None,
                   gmap=(lambda nd: (lambda j: (0,) * nd))(arr.ndim) if grad else None)
    r = arr.shape[0]
    return Par(arr, (r, cw), lambda j: (0, c0 + j), (r, cw * ncol) if grad else None, (lambda j: (0, j)) if grad else None)


def _ln(r, g, b):
    mu = jnp.mean(r, axis=-1, keepdims=True)
    xc = r - mu
    var = jnp.mean(xc * xc, axis=-1, keepdims=True)
    return xc * lax.rsqrt(var + NORM_EPS) * g + b


def _ln_fn(j, first, rv, pv):
    return [_ln(rv[0], pv[0], pv[1])]


def _ln_fwd_fn(j, first, rv, pv):
    y = _ln(rv[0], pv[0], pv[1])
    return [y, y], []


def _loss_fn(j, first, rv, pv):
    r3, tgt = rv
    g, b = pv
    y, vjp = jax.vjp(_ln, r3, g, b)
    diff = y - tgt
    part = 0.5 * jnp.sum(diff * diff) / D_MODEL
    dr, dg, db = vjp(diff * (1.0 / D_MODEL))
    return [dr, dr], [jnp.full((SUBLANES, LANES), part, F32), dg, db]


def _mix_fn(j, first, rv, pv):
    gs, gg, ys, yg = rv
    return [_sigmoid(gs) * ys + _sigmoid(gg) * yg]


def _row_pick(x, i):
    ax = x.ndim - 2
    return jnp.sum(jnp.where(_iota(x.shape, ax) == i, x, 0.0), axis=ax, keepdims=True)


def _causal_conv(prev, cur, w, first):
    width = w.shape[0]
    prev = jnp.where(first, 0.0, prev)
    y = cur * _row_pick(w, width - 1)
    for d in range(1, width):
        y = y + _shift_halo(prev, cur, d) * _row_pick(w, width - 1 - d)
    return y


def _ffn_act_fn(j, first, rv, pv):
    (pg, cg), (pu, cu) = rv
    wg, wu, bg, bu = pv
    hg = _causal_conv(pg, cg, wg, first) + bg
    hu = _causal_conv(pu, cu, wu, first) + bu
    return [_silu(hg) * hu]


def _gdn_pre_fn(j, first, rv, pv):
    (prev, cur), = rv
    w, = pv
    t = _silu(_causal_conv(prev, cur, w, first))
    tn = t * lax.rsqrt(jnp.sum(t * t, axis=-1, keepdims=True) + 1e-6)
    return [jnp.where(j < 2 * GDN_HEADS, tn, t)]


def _gdn_gate_fn(j, first, rv, pv):
    gba, = rv
    alog, dtb, eb, eg = pv
    tb = gba.shape[0]
    beta = _sigmoid(gba)
    g = -jnp.exp(alog) * _softplus(gba + dtb)
    ri, ci = _iota((tb, tb), 0), _iota((tb, tb), 1)
    tril = jnp.where((ri // GDN_CHUNK == ci // GDN_CHUNK) & (ci <= ri), 1.0, 0.0)
    gc = _dot(tril, g, "nn", True)
    return [_dot(beta, eb, "nn", True), _dot(gc, eg, "nn", True)]


def _swa_fn(j, first, rv, pv):
    q, (kp, kc), (vp, vc) = rv
    bp, bc, sk = pv
    sp = _dot(q, kp, "nt") * (SWA_HEAD_DIM ** -0.5) + bp
    sc = _dot(q, kc, "nt") * (SWA_HEAD_DIM ** -0.5) + bc
    qi = _iota(sp.shape, sp.ndim - 2) % SWA_BLOCK
    kj = _iota(sp.shape, sp.ndim - 1)
    sp = jnp.where((kj > qi) & jnp.logical_not(first), sp, NEG_INF)
    sc = jnp.where(kj <= qi, sc, NEG_INF)
    m = jnp.maximum(jnp.maximum(jnp.max(sp, axis=-1, keepdims=True), jnp.max(sc, axis=-1, keepdims=True)), sk)
    m = lax.stop_gradient(m)
    ep, ec, es = jnp.exp(sp - m), jnp.exp(sc - m), jnp.exp(sk - m)
    inv = 1.0 / (jnp.sum(ep, axis=-1, keepdims=True) + jnp.sum(ec, axis=-1, keepdims=True) + es)
    vp = jnp.where(first, 0.0, vp)
    return [_dot(ep * inv, vp, "nn") + _dot(ec * inv, vc, "nn")]


def _memattn_fn(j, first, rv, pv):
    q, = rv
    k, v = pv
    s = _dot(q, k, "nt") * (MEM_HEAD_DIM ** -0.5)
    m = lax.stop_gradient(jnp.max(s, axis=-1, keepdims=True))
    e = jnp.exp(s - m)
    p = e * (1.0 / jnp.sum(e, axis=-1, keepdims=True))
    return [_dot(p, v, "nn")]


SOLVE_PREC = "x3"


@jax.custom_vjp
def _unit_lower_inv(a):
    c = a.shape[-1]
    eye = _iota((1, c, c), 1) == _iota((1, c, c), 2)
    tinv = jnp.where(eye, 1.0, 0.0) - a
    x = _raw_dot(a, a, "nn", SOLVE_PREC)
    for i in range(5):
        tinv = tinv + _raw_dot(tinv, x, "nn", SOLVE_PREC)
        if i < 4:
            x = _raw_dot(x, x, "nn", SOLVE_PREC)
    return tinv


def _unit_lower_inv_fwd(a):
    t = _unit_lower_inv(a)
    return t, t


def _unit_lower_inv_bwd(t, g):
    return (-_raw_dot(_raw_dot(t, g, "tn", SOLVE_PREC), t, "nt", SOLVE_PREC),)


_unit_lower_inv.defvjp(_unit_lower_inv_fwd, _unit_lower_inv_bwd)


@jax.custom_vjp
def _known_inv(a, t):
    return t


def _known_inv_fwd(a, t):
    return t, t


def _known_inv_bwd(t, g):
    return _unit_lower_inv_bwd(t, g) + (jnp.zeros_like(t),)


_known_inv.defvjp(_known_inv_fwd, _known_inv_bwd)


def _gdn_heads(q, k, v, bx, gx, g64, z, nw, S, tinv=None, keep_tinv=False):
    c = GDN_CHUNK
    q = q * (GDN_HEAD_DIM ** -0.5)
    kb, vb = k * bx, v * bx
    ri, ci = _iota((1, c, c), 1), _iota((1, c, c), 2)
    tril, strict, eye = ci <= ri, ci < ri, ci == ri
    grow = jnp.sum(jnp.where(eye, g64, 0.0), axis=1, keepdims=True)
    decay = jnp.where(tril, jnp.exp(jnp.where(tril, g64 - grow, 0.0)), 0.0)
    a = jnp.where(strict, _dot(kb, k, "nt") * decay, 0.0)
    tinv = _unit_lower_inv(a) if tinv is None else _known_inv(a, tinv)
    eg = jnp.exp(gx)
    u = _dot(tinv, vb, "nn", SOLVE_PREC)
    w = _dot(tinv, kb * eg, "nn", SOLVE_PREC)
    ai = jnp.where(tril, _dot(q, k, "nt") * decay, 0.0)
    glast = _row_pick(gx, c - 1)
    v_new = u - _dot(w, S, "nn")
    o = _dot(q * eg, S, "nn") + _dot(ai, v_new, "nn")
    s_new = S * jnp.exp(glast) + _dot(k * jnp.exp(glast - gx), v_new, "tn")
    o = o * lax.rsqrt(jnp.mean(o * o, axis=-1, keepdims=True) + 1e-6) * nw
    return (o * _silu(z), s_new, tinv) if keep_tinv else (o * _silu(z), s_new)


GDN_STEP_CHUNKS = 2


def _head_major(ref, off, width=GDN_HEAD_DIM, ci=0):
    r = slice(ci * GDN_CHUNK, (ci + 1) * GDN_CHUNK)
    return jnp.stack([ref[r, off + h * GDN_HEAD_DIM:off + h * GDN_HEAD_DIM + width] for h in range(GDN_HEADS)])


def _gdn_chunks_fwd(qkv, bx, gx, proj, nw, rider=None):
    T = qkv.shape[0]
    cps = GDN_STEP_CHUNKS
    nc, c, hd, nh = T // (cps * GDN_CHUNK), GDN_CHUNK, GDN_HEAD_DIM, GDN_HEADS
    rd = _ride(rider, 5, 3, 1)

    def body(*refs):
        qkv_ref, bx_ref, gx_ref, z_ref, nw_ref = refs[:5]
        y_ref, st_ref, ti_ref = refs[rd.o0:rd.o0 + 3]
        S = refs[rd.s0]
        rd.at_start(refs, pl.program_id(0) == 0)

        @pl.when(pl.program_id(0) == 0)
        def _():
            S[...] = jnp.zeros_like(S)

        s_new = S[...]
        for ci in range(cps):
            st_ref[ci] = s_new
            y, s_new, ti = _gdn_heads(_head_major(qkv_ref, 0, ci=ci), _head_major(qkv_ref, GDN_W, ci=ci),
                                      _head_major(qkv_ref, 2 * GDN_W, ci=ci), _head_major(bx_ref, 0, ci=ci),
                                      _head_major(gx_ref, 0, ci=ci), _head_major(gx_ref, 0, c, ci), _head_major(z_ref, 0, ci=ci),
                                      nw_ref[...], s_new, keep_tinv=True)
            ti_ref[ci] = ti
            for h in range(nh):
                y_ref[ci * c:(ci + 1) * c, h * hd:(h + 1) * hd] = y[h].astype(y_ref.dtype)
        S[...] = s_new
        rd.at_end(refs, pl.program_id(0) == nc - 1)

    row = lambda w, cb: pl.BlockSpec((cps * c, w), lambda n: (n, cb))
    res = pl.pallas_call(
        body, name="gdn_chunks_fwd", grid=(nc,),
        in_specs=[row(3 * GDN_W, 0), row(GDN_W, 0), row(GDN_W, 0), row(GDN_W, P_GZ // GDN_W),
                  pl.BlockSpec((1, hd), lambda n: (0, 0))] + rd.in_specs,
        out_specs=[row(GDN_W, 0), pl.BlockSpec((cps, nh, hd, hd), lambda n: (n, 0, 0, 0)),
                   pl.BlockSpec((cps, nh, c, c), lambda n: (n, 0, 0, 0))] + rd.out_specs,
        out_shape=[jax.ShapeDtypeStruct((T, GDN_W), BF16), jax.ShapeDtypeStruct((nc * cps, nh, hd, hd), F32),
                   jax.ShapeDtypeStruct((nc * cps, nh, c, c), F32)] + rd.out_shapes,
        scratch_shapes=[pltpu.VMEM((nh, hd, hd), F32)] + rd.scratch,
        compiler_params=_cparams(("arbitrary",)),
    )(qkv, bx, gx, proj, nw, *rd.ins)
    return res[0], (res[1], res[2]), res[3:]


def _gdn_chunks_bwd(qkv, bx, gx, proj, nw, saved, dy, rider=None):
    states, tinvs = saved
    T = qkv.shape[0]
    cps = GDN_STEP_CHUNKS
    nc, c, hd, nh = T // (cps * GDN_CHUNK), GDN_CHUNK, GDN_HEAD_DIM, GDN_HEADS
    rd = _ride(rider, 8, 5, 1)

    def body(*refs):
        qkv_ref, bx_ref, gx_ref, z_ref, nw_ref, st_ref, ti_ref, dy_ref = refs[:8]
        dqkv_ref, dbx_ref, dgx_ref, dz_ref, dnw_ref = refs[rd.o0:rd.o0 + 5]
        dS = refs[rd.s0]
        rd.at_start(refs, pl.program_id(0) == 0)

        @pl.when(pl.program_id(0) == 0)
        def _():
            dS[...] = jnp.zeros_like(dS)
            dnw_ref[...] = jnp.zeros_like(dnw_ref)

        dsp = dS[...]
        for ci in reversed(range(cps)):
            r = slice(ci * c, (ci + 1) * c)
            args = (_head_major(qkv_ref, 0, ci=ci), _head_major(qkv_ref, GDN_W, ci=ci), _head_major(qkv_ref, 2 * GDN_W, ci=ci),
                    _head_major(bx_ref, 0, ci=ci), _head_major(gx_ref, 0, ci=ci), _head_major(gx_ref, 0, c, ci),
                    _head_major(z_ref, 0, ci=ci), nw_ref[...], st_ref[ci])
            _, vjp = jax.vjp(functools.partial(_gdn_heads, tinv=ti_ref[ci]), *args)
            dq, dk, dv, dbx, dgx, dg64, dz, dnw, dsp = vjp((_head_major(dy_ref, 0, ci=ci), dsp))
            for h in range(nh):
                sl = slice(h * hd, (h + 1) * hd)
                dqkv_ref[r, sl] = dq[h].astype(dqkv_ref.dtype)
                dqkv_ref[r, GDN_W + h * hd:GDN_W + (h + 1) * hd] = dk[h].astype(dqkv_ref.dtype)
                dqkv_ref[r, 2 * GDN_W + h * hd:2 * GDN_W + (h + 1) * hd] = dv[h].astype(dqkv_ref.dtype)
                dbx_ref[r, sl] = dbx[h]
                dgx_ref[r, sl] = dgx[h]
                dgx_ref[r, h * hd:h * hd + c] += dg64[h]
                dz_ref[r, sl] = dz[h].astype(dz_ref.dtype)
            dnw_ref[...] += dnw
        dS[...] = dsp
        rd.at_end(refs, pl.program_id(0) == nc - 1)

    row = lambda w, cb: pl.BlockSpec((cps * c, w), lambda s: (nc - 1 - s, cb))
    res = pl.pallas_call(
        body, name="gdn_chunks_bwd", grid=(nc,),
        in_specs=[row(3 * GDN_W, 0), row(GDN_W, 0), row(GDN_W, 0), row(GDN_W, P_GZ // GDN_W), pl.BlockSpec((1, hd), lambda s: (0, 0)),
                  pl.BlockSpec((cps, nh, hd, hd), lambda s: (nc - 1 - s, 0, 0, 0)),
                  pl.BlockSpec((cps, nh, c, c), lambda s: (nc - 1 - s, 0, 0, 0)), row(GDN_W, 0)] + rd.in_specs,
        out_specs=[row(3 * GDN_W, 0), row(GDN_W, 0), row(GDN_W, 0), row(GDN_W, 0),
                   pl.BlockSpec((1, hd), lambda s: (0, 0))] + rd.out_specs,
        out_shape=[jax.ShapeDtypeStruct((T, 3 * GDN_W), F32), jax.ShapeDtypeStruct((T, GDN_W), F32),
                   jax.ShapeDtypeStruct((T, GDN_W), F32), jax.ShapeDtypeStruct((T, GDN_W), _CDT),
                   jax.ShapeDtypeStruct((1, hd), F32)] + rd.out_shapes,
        scratch_shapes=[pltpu.VMEM((nh, hd, hd), F32)] + rd.scratch,
        compiler_params=_cparams(("arbitrary",)),
    )(qkv, bx, gx, proj, nw, states, tinvs, dy, *rd.ins)
    res = list(res)
    return res[:5] + [res[5:]]


def _adamw(name, w, g, m, v):
    R, C = w.shape
    tr = _tile(R, 128, SUBLANES)

    def body(w_ref, g_ref, m_ref, v_ref, d_ref, m2_ref, v2_ref):
        g_ = g_ref[...]
        m2 = ADAM_B1 * m_ref[...] + (1.0 - ADAM_B1) * g_
        v2 = ADAM_B2 * v_ref[...] + (1.0 - ADAM_B2) * (g_ * g_)
        m_hat = m2 / (1.0 - ADAM_B1 ** ADAM_STEP)
        v_hat = v2 / (1.0 - ADAM_B2 ** ADAM_STEP)
        d_ref[...] = -ADAM_LR * (m_hat / (jnp.sqrt(v_hat) + ADAM_EPS) + ADAM_WD * w_ref[...])
        m2_ref[...] = m2
        v2_ref[...] = v2

    spec = pl.BlockSpec((tr, C), lambda i: (i, 0))
    return pl.pallas_call(
        body, name=name, grid=(R // tr,), in_specs=[spec] * 4, out_specs=[spec] * 3,
        out_shape=[jax.ShapeDtypeStruct((R, C), F32)] * 3, compiler_params=_cparams(("parallel",)),
    )(w, g, m, v)


def _addn(name, parts, out_dtype=F32):
    parts = [p if isinstance(p, tuple) else (p, None) for p in parts]
    a0, k0 = parts[0]
    R, C = a0.shape[-2:]
    tr = _tile(R, 256, 2 * SUBLANES)
    specs = []
    for a, k in parts:
        if k is None:
            specs.append(pl.BlockSpec((tr, C), lambda i: (i, 0)))
        else:
            specs.append(pl.BlockSpec((None, tr, C), (lambda kk: (lambda i: (kk, i, 0)))(k)))

    def body(*refs):
        acc = refs[0][...].astype(F32)
        for r in refs[1:-1]:
            acc = acc + r[...].astype(F32)
        refs[-1][...] = acc.astype(out_dtype)

    return pl.pallas_call(
        body, name=name, grid=(R // tr,), in_specs=specs, out_specs=pl.BlockSpec((tr, C), lambda i: (i, 0)),
        out_shape=jax.ShapeDtypeStruct((R, C), out_dtype), compiler_params=_cparams(("parallel",)),
    )(*[a for a, _ in parts])


MESH = pl.DeviceIdType.MESH
_HBM = pl.BlockSpec(memory_space=pltpu.HBM)


def _place():
    x, y, c = lax.axis_index("x"), lax.axis_index("y"), lax.axis_index("c")
    return x, y, c, [(1 - x, y), (x, 1 - y), (1 - x, 1 - y)]


class _Rider:
    def __init__(self, ins, out_shapes, nsem, start, finish):
        self.ins, self.out_shapes, self.nsem, self.start, self.finish = list(ins), list(out_shapes), nsem, start, finish

    def sems(self):
        return [pltpu.SemaphoreType.DMA((self.nsem,)), pltpu.SemaphoreType.DMA((self.nsem,))]


def _run_rider(name, rd):
    n_in, n_out = len(rd.ins), len(rd.out_shapes)

    def body(*refs):
        ins, outs, (send, recv) = refs[:n_in], refs[n_in:n_in + n_out], refs[n_in + n_out:]
        rd.start(ins, outs, send, recv)
        rd.finish(ins, outs, send, recv)

    return pl.pallas_call(body, name=name, in_specs=[_HBM] * n_in, out_specs=[_HBM] * n_out, out_shape=rd.out_shapes,
                          scratch_shapes=rd.sems())(*rd.ins)


def _gather_rider(ts):
    nt = len(ts)

    def half(t, hc):
        rh = ts[t].shape[0] // 2
        return pl.ds(pl.multiple_of(hc * rh, 16), rh)

    def rcopy(send, recv, t, k, src, dst, to):
        return pltpu.make_async_remote_copy(src_ref=src, dst_ref=dst, send_sem=send.at[6 * t + k], recv_sem=recv.at[6 * t + k],
                                            device_id=to, device_id_type=MESH)

    def first_hop(ins, outs, send, recv, t, r, px, py, c, me):
        return rcopy(send, recv, t, r, ins[t].at[half(t, c)], outs[t].at[me, half(t, c)], (px, py, c))

    def start(ins, outs, send, recv):
        x, y, c, rel = _place()
        for t in range(nt):
            for r, (px, py) in enumerate(rel):
                first_hop(ins, outs, send, recv, t, r, px, py, c, 2 * x + y).start()

    def finish(ins, outs, send, recv):
        x, y, c, rel = _place()
        sib = (x, y, 1 - c)
        passed = []
        for t in range(nt):
            for r, (px, py) in enumerate(rel):
                got = outs[t].at[2 * px + py, half(t, c)]
                rcopy(send, recv, t, r, got, got, (px, py, c)).wait_recv()
                fw = rcopy(send, recv, t, 3 + r, got, got, sib)
                fw.start()
                passed.append(fw)
        for t in range(nt):
            for r, (px, py) in enumerate(rel):
                got = outs[t].at[2 * px + py, half(t, 1 - c)]
                rcopy(send, recv, t, 3 + r, got, got, sib).wait_recv()
        for t in range(nt):
            for r, (px, py) in enumerate(rel):
                first_hop(ins, outs, send, recv, t, r, px, py, c, 2 * x + y).wait_send()
        for fw in passed:
            fw.wait_send()

    return _Rider(ts, [jax.ShapeDtypeStruct((4,) + tuple(t.shape), t.dtype) for t in ts], 6 * nt, start, finish)


def _scatter_rider(ps):
    nt = len(ps)

    def copy(ins, outs, send, recv, t, r, px, py, c):
        return pltpu.make_async_remote_copy(src_ref=ins[t].at[2 * px + py], dst_ref=outs[t].at[r], send_sem=send.at[3 * t + r],
                                            recv_sem=recv.at[3 * t + r], device_id=(px, py, c), device_id_type=MESH)

    def start(ins, outs, send, recv):
        x, y, c, rel = _place()
        for t in range(nt):
            for r, (px, py) in enumerate(rel):
                copy(ins, outs, send, recv, t, r, px, py, c).start()

    def finish(ins, outs, send, recv):
        x, y, c, rel = _place()
        for t in range(nt):
            for r, (px, py) in enumerate(rel):
                copy(ins, outs, send, recv, t, r, px, py, c).wait()

    return _Rider(ps, [jax.ShapeDtypeStruct((3,) + tuple(p.shape[1:]), p.dtype) for p in ps], 3 * nt, start, finish)


def _pair_swap(tag, ts):
    nt = len(ts)

    def body(*refs):
        ins, outs = refs[:nt], refs[nt:2 * nt]
        send, recv = refs[2 * nt:]
        x, y, c, _ = _place()
        cps = []
        for t in range(nt):
            rh = ts[t].shape[1] // 2
            src = ins[t].at[:, pl.ds(pl.multiple_of((1 - c) * rh, 16), rh), :]
            cp = pltpu.make_async_remote_copy(src_ref=src, dst_ref=outs[t], send_sem=send.at[t], recv_sem=recv.at[t],
                                              device_id=(x, y, 1 - c), device_id_type=MESH)
            cp.start()
            cps.append(cp)
        for cp in cps:
            cp.wait()

    return pl.pallas_call(
        body, name="pair_swap_" + tag, in_specs=[_HBM] * nt, out_specs=[_HBM] * nt,
        out_shape=[jax.ShapeDtypeStruct((4, t.shape[1] // 2, t.shape[2]), t.dtype) for t in ts],
        scratch_shapes=[pltpu.SemaphoreType.DMA((nt,)), pltpu.SemaphoreType.DMA((nt,))],
    )(*ts)


def _pair_exchange(gs):
    nt = len(gs)

    def body(*refs):
        ins, outs = refs[:nt], refs[nt:2 * nt]
        send, recv = refs[2 * nt:]
        x, y, c, _ = _place()
        cps = []
        for t in range(nt):
            cp = pltpu.make_async_remote_copy(src_ref=ins[t], dst_ref=outs[t], send_sem=send.at[t], recv_sem=recv.at[t],
                                              device_id=(x, y, 1 - c), device_id_type=MESH)
            cp.start()
            cps.append(cp)
        for cp in cps:
            cp.wait()

    return pl.pallas_call(
        body, name="pair_exchange", in_specs=[_HBM] * nt, out_specs=[_HBM] * nt,
        out_shape=[jax.ShapeDtypeStruct(tuple(g.shape), g.dtype) for g in gs],
        scratch_shapes=[pltpu.SemaphoreType.DMA((nt,)), pltpu.SemaphoreType.DMA((nt,))],
    )(*gs)


def _allgather8(v):
    m, n = v.shape

    def body(x_ref, out_ref, send, recv, lsem):
        x, y, c, rel = _place()
        me, sib = (x, y, c), (x, y, 1 - c)

        def blk(px, py, pc):
            return out_ref.at[4 * px + 2 * py + pc]

        def copy(k, block, to, src=None):
            return pltpu.make_async_remote_copy(src_ref=blk(*block) if src is None else src, dst_ref=blk(*block), send_sem=send.at[k],
                                                recv_sem=recv.at[k], device_id=to, device_id_type=MESH)

        mine = pltpu.make_async_copy(x_ref, blk(*me), lsem)
        mine.start()
        first = [copy(0, me, sib, src=x_ref)] + [copy(1 + r, me, (*ch, c), src=x_ref) for r, ch in enumerate(rel)]
        for cp in first:
            cp.start()
        passed = [copy(4 + r, (*ch, c), sib) for r, ch in enumerate(rel)]
        for r, ch in enumerate(rel):
            copy(1 + r, (*ch, c), me).wait_recv()
            passed[r].start()
        copy(0, sib, me).wait_recv()
        for r, ch in enumerate(rel):
            copy(4 + r, (*ch, 1 - c), me).wait_recv()
        for cp in first + passed:
            cp.wait_send()
        mine.wait()

    return pl.pallas_call(
        body, name="allgather8", in_specs=[pl.BlockSpec(memory_space=pltpu.VMEM)], out_specs=pl.BlockSpec(memory_space=pltpu.VMEM),
        out_shape=jax.ShapeDtypeStruct((8, m, n), v.dtype),
        scratch_shapes=[pltpu.SemaphoreType.DMA((7,)), pltpu.SemaphoreType.DMA((7,)), pltpu.SemaphoreType.DMA],
    )(v)


def _t5_bucket(dist):
    max_exact = REL_BUCKETS // 2
    d = jnp.maximum(dist, 1).astype(F32)
    large = max_exact + (jnp.log(d / max_exact) / math.log(REL_MAX_DIST / max_exact) * (REL_BUCKETS - max_exact)).astype(jnp.int32)
    large = jnp.minimum(large, REL_BUCKETS - 1)
    return jnp.where(dist < max_exact, dist, large)


def _bias_onehot():
    qi = jnp.arange(SWA_BLOCK)[:, None]
    kj = jnp.arange(SWA_BLOCK)[None, :]
    dist = jnp.concatenate([(qi + SWA_BLOCK - kj).reshape(-1), (qi - kj).reshape(-1)])
    bucket = _t5_bucket(jnp.maximum(dist, 0))
    return (bucket[None, :] == jnp.arange(REL_BUCKETS)[:, None]).astype(F32)


def _head_spread():
    lane = jnp.arange(LANES)[:, None]
    head = jnp.arange(GDN_W)[None, :] // GDN_HEAD_DIM
    return (lane == head).astype(F32), (lane == head + GDN_HEADS).astype(F32)


def _lane16(v8):
    return jnp.pad(v8.astype(F32), (GDN_HEADS, LANES - 2 * GDN_HEADS)).reshape(1, LANES)


def _stack_heads(t, nb):
    return t.reshape(nb, SWA_BLOCK, SWA_KV_HEADS, SWA_GRP, SWA_HEAD_DIM).transpose(2, 0, 3, 1, 4).reshape(
        SWA_KV_HEADS, nb * SWA_GRP * SWA_BLOCK, SWA_HEAD_DIM)


def _unstack_heads(t, nb):
    return t.reshape(SWA_KV_HEADS, nb, SWA_GRP, SWA_BLOCK, SWA_HEAD_DIM).transpose(1, 3, 0, 2, 4).reshape(nb * SWA_BLOCK, SWA_Q)


def _kv_heads(t):
    return t.reshape(t.shape[0], SWA_KV_HEADS, SWA_HEAD_DIM).transpose(1, 0, 2)


def _swa_specs(qs, ks, vs, bp, bc, sk, grad, gdt=(F32,)):
    T = ks.shape[1]
    qr = SWA_GRP * SWA_BLOCK
    g = lambda a: tuple(a.shape) if grad else None
    nk = SWA_KV_HEADS
    m3 = lambda j, n: (0, n, 0)
    h3 = lambda j, n: (0, jnp.maximum(n - 1, 0), 0)
    p3 = lambda j: (0, 0, 0)
    rows = [Row(qs, (nk, qr, SWA_HEAD_DIM), m3, gshape=g(qs), gmap=m3, gdt=gdt),
            Row(ks, (nk, SWA_BLOCK, SWA_HEAD_DIM), m3, (nk, SWA_BLOCK, SWA_HEAD_DIM), h3, g(ks), m3, gdt),
            Row(vs, (nk, SWA_BLOCK, SWA_HEAD_DIM), m3, (nk, SWA_BLOCK, SWA_HEAD_DIM), h3, g(vs), m3, gdt)]
    pars = [Par(bp, (nk, qr, SWA_BLOCK), p3, g(bp), p3), Par(bc, (nk, qr, SWA_BLOCK), p3, g(bc), p3),
            Par(sk, (nk, qr, 1), p3, g(sk), p3)]
    return rows, pars, T // SWA_BLOCK


class _LocalWeights:
    def __init__(self, W):
        self.W = W

    def w1(self):
        return self.W

    def rider_a(self):
        return None

    def w2(self, got):
        return self.W

    def rider_b(self):
        return None

    def w3(self, got):
        return self.W

    def rider_g(self, G):
        return None

    def g_done(self, got):
        pass

    def rider_up(self, G):
        return None

    def up_done(self, got):
        pass

    def rider_last(self, G):
        return None

    def last_done(self, got):
        pass


def _fwd_bwd(x, mem, tgt, src):
    W = dict(src.w1())
    T = x.shape[0]
    nb = T // SWA_BLOCK
    tb = min(256, T)
    tbl = min(512, T)
    fwd = lambda f: (lambda *a: (f(*a), []))
    full = lambda cols, dt, t, cw: Out((T, cols), dt, (t, cw), lambda j, n: (n, j))

    xb = x.astype(_CDT)
    ra = src.rider_a()
    proj = _mm("proj", xb, W["in_p"], "nn", rider=ra)
    proj, got = proj if ra is not None else (proj, None)
    W.update(src.w2(got))

    onehot_t = _bias_onehot()
    bias_flat = _mm("swa_bias", W["rel_bias"].T, onehot_t, "nn", hi=True)
    half = SWA_BLOCK * SWA_BLOCK
    bp = bias_flat[:, :half].reshape(SWA_KV_HEADS, SWA_GRP * SWA_BLOCK, SWA_BLOCK)
    bc = bias_flat[:, half:].reshape(SWA_KV_HEADS, SWA_GRP * SWA_BLOCK, SWA_BLOCK)
    sk = jnp.broadcast_to(W["swa_sinks"].reshape(SWA_KV_HEADS, SWA_GRP, 1, 1), (SWA_KV_HEADS, SWA_GRP, SWA_BLOCK, 1)).reshape(
        SWA_KV_HEADS, SWA_GRP * SWA_BLOCK, 1)
    qs = _stack_heads(proj[:, P_SQ:P_SQ + SWA_Q], nb)
    ks = _kv_heads(proj[:, P_SK:P_SK + SWA_KV])
    vs = _kv_heads(proj[:, P_SV:P_SV + SWA_KV])
    rows, pars, nblk = _swa_specs(qs, ks, vs, bp, bc, sk, False)
    o_s, = _rowmap("swa_fwd", fwd(_swa_fn), 1, nblk, rows, pars,
                   [Out(tuple(qs.shape), F32, (SWA_KV_HEADS, SWA_GRP * SWA_BLOCK, SWA_HEAD_DIM), lambda j, n: (0, n, 0))])
    o_swa = _unstack_heads(o_s, nb).astype(_CDT)

    ncq = 3 * GDN_W // LANES
    tbp = min(1024, T)
    pre_rows = lambda grad: [_rowspec(proj, tbp, LANES, P_GQKV // LANES, halo=SUBLANES, grad=grad, ncol=ncq, gdt=(_CDT,))]
    pre_pars = lambda grad: [_parspec(W["gdn_conv_w"], LANES, 0, grad=grad, ncol=ncq)]
    qkv_n, = _rowmap("gdn_pre_fwd", fwd(_gdn_pre_fn), ncq, T // tbp, pre_rows(False), pre_pars(False),
                     [full(3 * GDN_W, F32, tbp, LANES)])
    eb, eg = _head_spread()
    alog_row, dtb_row = _lane16(W["gdn_a_log"]), _lane16(W["gdn_dt_bias"])
    gate_rows = lambda grad: [_rowspec(proj, tbl, LANES, P_BA // LANES, cstep=0, grad=grad, gdt=(_CDT,))]
    gate_pars = lambda grad: [_parspec(alog_row, grad=grad), _parspec(dtb_row, grad=grad), _parspec(eb), _parspec(eg)]
    bx, gx = _rowmap("gdn_gate_fwd", fwd(_gdn_gate_fn), 1, T // tbl, gate_rows(False), gate_pars(False),
                     [full(GDN_W, F32, tbl, GDN_W), full(GDN_W, F32, tbl, GDN_W)])
    nw = W["gdn_norm_w"].reshape(1, GDN_HEAD_DIM)
    o_gdn, states, got = _gdn_chunks_fwd(qkv_n, bx, gx, proj, nw, rider=src.rider_b())
    W.update(src.w3(got))

    ys = _mm("y_swa", o_swa, W["br_swa"], "nn")
    yg = _mm("y_gdn", o_gdn, W["br_gdn"], "nn")
    cwm = 512
    mix_rows = lambda grad: [_rowspec(proj, tb, cwm, P_GS // cwm, grad=grad, ncol=D_MODEL // cwm, gdt=(_CDT,)),
                             _rowspec(proj, tb, cwm, P_GG // cwm, grad=grad, ncol=D_MODEL // cwm, gdt=(_CDT,)),
                             _rowspec(ys, tb, cwm, 0, grad=grad, ncol=D_MODEL // cwm, gdt=(_CDT,)),
                             _rowspec(yg, tb, cwm, 0, grad=grad, ncol=D_MODEL // cwm, gdt=(_CDT,))]
    mixed, = _rowmap("mix_fwd", fwd(_mix_fn), D_MODEL // cwm, T // tb, mix_rows(False), [], [full(D_MODEL, _CDT, tb, cwm)])
    r1 = _mm("r1", mixed, W["mix_o"], "nn", add=x, add_scale=ALPHA)

    def ln_fwd(name, r, g, b):
        return _rowmap(name, _ln_fwd_fn, 1, T // tb, [_rowspec(r, tb, D_MODEL, 0)], [_parspec(g), _parspec(b)],
                       [full(D_MODEL, F32, tb, D_MODEL), full(D_MODEL, _CDT, tb, D_MODEL)])

    def ln_bwd(name, r, g, b, ct):
        return _rowmap_bwd(name, _ln_fn, 1, T // tb, [_rowspec(r, tb, D_MODEL, 0, grad=True, gdt=(F32, _CDT))],
                           [_parspec(g, grad=True), _parspec(b, grad=True)], [_rowspec(ct, tb, D_MODEL, 0)])

    g1, b1 = W["ln1_g"].reshape(1, -1), W["ln1_b"].reshape(1, -1)
    g2, b2 = W["ln2_g"].reshape(1, -1), W["ln2_b"].reshape(1, -1)
    g3, b3 = W["ln3_g"].reshape(1, -1), W["ln3_b"].reshape(1, -1)
    x1, x1b = ln_fwd("ln1_fwd", r1, g1, b1)

    qm = _mm("mem_q", x1b, W["mem_q"], "nn")
    kvm = _mm("mem_kv", mem, W["mem_kv"], "nn")
    ma_rows = lambda grad: [_rowspec(qm, tbl, MEM_HEAD_DIM, 0, grad=grad, ncol=MEM_HEADS, gdt=(_CDT,))]
    ma_pars = lambda grad: [_parspec(kvm, MEM_HEAD_DIM, 0, grad=grad, ncol=MEM_HEADS),
                            _parspec(kvm, MEM_HEAD_DIM, MEM_HEADS, grad=grad, ncol=MEM_HEADS)]
    om, = _rowmap("memattn_fwd", fwd(_memattn_fn), MEM_HEADS, T // tbl, ma_rows(False), ma_pars(False),
                  [full(MEM_W, _CDT, tbl, MEM_HEAD_DIM)])
    r2 = _mm("r2", om, W["mem_o"], "nn", add=x1, add_scale=ALPHA)
    x2, x2b = ln_fwd("ln2_fwd", r2, g2, b2)

    hcat = _mm("ffn_up", x2b, W["up_p"], "nn")
    cwf = 512
    ncf = D_FF_PAD // cwf
    cw_p, cb_p = W["ffn_conv_w_p"], W["ffn_conv_b_p"]
    tbf = min(512, T)
    ffn_rows = lambda grad: [_rowspec(hcat, tbf, cwf, 0, halo=SUBLANES, grad=grad, ncol=ncf, gdt=(_CDT,)),
                             _rowspec(hcat, tbf, cwf, ncf, halo=SUBLANES, grad=grad, ncol=ncf, gdt=(_CDT,))]
    ffn_pars = lambda grad: [_parspec(cw_p, cwf, 0, grad=grad, ncol=ncf), _parspec(cw_p, cwf, ncf, grad=grad, ncol=ncf),
                             _parspec(cb_p, cwf, 0, grad=grad, ncol=ncf), _parspec(cb_p, cwf, ncf, grad=grad, ncol=ncf)]
    act, = _rowmap("ffn_act_fwd", fwd(_ffn_act_fn), ncf, T // tbf, ffn_rows(False), ffn_pars(False), [full(D_FF_PAD, _CDT, tbf, cwf)])
    r3 = _mm("r3", act, W["down_p"], "nn", add=x2, add_scale=ALPHA)
    dr3, dr3b, lacc, dg3, db3 = _rowmap("ln3_loss", _loss_fn, 1, T // tb, [_rowspec(r3, tb, D_MODEL, 0), _rowspec(tgt, tb, D_MODEL, 0)],
                                        [_parspec(g3), _parspec(b3)], [full(D_MODEL, F32, tb, D_MODEL), full(D_MODEL, _CDT, tb, D_MODEL)],
                                  accs=[(SUBLANES, LANES), (1, D_MODEL), (1, D_MODEL)])
    loss = lacc[0, 0]

    G = {}
    G["down_p"] = _mm("dw_down", act, dr3b, "tn", out_dtype=_GDT)
    dact = _mm("d_act", dr3b, W["down_p"], "nt")
    dhg, dhu, dcwg, dcwu, dcbg, dcbu = _rowmap_bwd("ffn_act_bwd", _ffn_act_fn, ncf, T // tbf, ffn_rows(True), ffn_pars(True),
                                                   [_rowspec(dact, tbf, cwf, 0)])
    dx2 = _mm("dx2_gate", dhg, W["up_p"], "nt", add=dr3, add_scale=ALPHA, b_k0=0)
    dx2 = _mm("dx2_up", dhu, W["up_p"], "nt", add=dx2, b_k0=D_FF_PAD)
    G["up_p"] = jnp.concatenate([_mm("dw_gate", x2b, dhg, "tn", out_dtype=_GDT), _mm("dw_up", x2b, dhu, "tn", out_dtype=_GDT)], axis=1)
    G["ffn_conv_w"] = jnp.concatenate([dcwg[:, :D_FF], dcwu[:, :D_FF]], axis=1)
    G["ffn_conv_b"] = jnp.concatenate([dcbg[0, :D_FF], dcbu[0, :D_FF]])
    G["ln3_g"], G["ln3_b"] = dg3[0], db3[0]

    dr2, dr2b, dg2, db2 = ln_bwd("ln2_bwd", r2, g2, b2, dx2)
    G["ln2_g"], G["ln2_b"] = dg2[0], db2[0]
    G["mem_o"] = _mm("dw_mem_o", om, dr2b, "tn", out_dtype=_GDT)
    dom = _mm("d_om", dr2b, W["mem_o"], "nt", out_dtype=_CDT)
    dqm, dkm, dvm = _rowmap_bwd("memattn_bwd", _memattn_fn, MEM_HEADS, T // tbl, ma_rows(True), ma_pars(True),
                                [_rowspec(dom, tbl, MEM_HEAD_DIM, 0)])
    G["mem_kv"] = _mm("dw_mem_kv", mem.astype(_CDT), jnp.concatenate([dkm, dvm], axis=1).astype(_CDT), "tn", out_dtype=_GDT)
    G["mem_q"] = _mm("dw_mem_q", x1b, dqm, "tn", out_dtype=_GDT)
    dx1 = _mm("dx1", dqm, W["mem_q"], "nt", add=dr2, add_scale=ALPHA)

    dr1, dr1b, dg1, db1 = ln_bwd("ln1_bwd", r1, g1, b1, dx1)
    G["ln1_g"], G["ln1_b"] = dg1[0], db1[0]
    G["mix_o"] = _mm("dw_mix_o", mixed, dr1b, "tn", out_dtype=_GDT)
    dmixed = _mm("d_mixed", dr1b, W["mix_o"], "nt")
    dgs, dgg, dys, dyg = _rowmap_bwd("mix_bwd", _mix_fn, D_MODEL // cwm, T // tb, mix_rows(True), [], [_rowspec(dmixed, tb, cwm, 0)])
    G["br_swa"] = _mm("dw_br_swa", o_swa, dys, "tn", out_dtype=_GDT)
    G["br_gdn"] = _mm("dw_br_gdn", o_gdn, dyg, "tn", out_dtype=_GDT)
    do_swa = _mm("d_o_swa", dys, W["br_swa"], "nt", out_dtype=_CDT)
    do_gdn = _mm("d_o_gdn", dyg, W["br_gdn"], "nt")

    rows, pars, nblk = _swa_specs(qs, ks, vs, bp, bc, sk, True, (_CDT,))
    m3 = lambda j, n: (0, n, 0)
    dqs, dks, dvs, dbp, dbc, dsk = _rowmap_bwd("swa_bwd", _swa_fn, 1, nblk, rows, pars,
                                               [Row(_stack_heads(do_swa, nb), (SWA_KV_HEADS, SWA_GRP * SWA_BLOCK, SWA_HEAD_DIM), m3)])
    d_swa = jnp.concatenate([_unstack_heads(dqs, nb), dks.transpose(1, 0, 2).reshape(T, SWA_KV),
                             dvs.transpose(1, 0, 2).reshape(T, SWA_KV)], axis=1)
    dbias = jnp.concatenate([dbp.reshape(SWA_HEADS, half), dbc.reshape(SWA_HEADS, half)], axis=1)
    G["rel_bias"] = _mm("d_rel_bias", dbias, onehot_t.T, "nn", hi=True).T
    G["swa_sinks"] = _mm("d_sinks", dsk.reshape(SWA_HEADS, SWA_BLOCK), jnp.ones((SWA_BLOCK, LANES), F32), "nn", hi=True)[:, 0]

    dqkv_n, dbx, dgx, dz, dnw, got = _gdn_chunks_bwd(qkv_n, bx, gx, proj, nw, states, do_gdn, rider=src.rider_g(G))
    src.g_done(got)
    G["gdn_norm_w"] = dnw[0]
    dgba, dalog, ddtb = _rowmap_bwd("gdn_gate_bwd", _gdn_gate_fn, 1, T // tbl, gate_rows(True), gate_pars(True),
                                    [_rowspec(dbx, tbl, GDN_W, 0), _rowspec(dgx, tbl, GDN_W, 0)])
    G["gdn_a_log"], G["gdn_dt_bias"] = dalog[0, GDN_HEADS:2 * GDN_HEADS], ddtb[0, GDN_HEADS:2 * GDN_HEADS]
    dgqkv, dcw_gdn = _rowmap_bwd("gdn_pre_bwd", _gdn_pre_fn, ncq, T // tbp, pre_rows(True), pre_pars(True),
                                 [_rowspec(dqkv_n, tbp, LANES, 0)])
    G["gdn_conv_w"] = dcw_gdn

    dproj = jnp.concatenate([dgs, dgg, dgqkv, dz, d_swa, dgba, jnp.zeros((T, P_END - P_USED), _CDT)], axis=1)
    ru = src.rider_up(G)
    G["in_p"] = _mm("dw_in", xb, dproj, "tn", out_dtype=_GDT, rider=ru)
    if ru is not None:
        G["in_p"], got = G["in_p"]
        src.up_done(got)
    rl = src.rider_last(G)
    dx = _mm("dx", dproj, W["in_p"], "nt", add=dr1, add_scale=ALPHA, rider=rl)
    if rl is not None:
        dx, got = dx
        src.last_done(got)
    return loss, dx, G


W_NAMES = ["w_in", "rel_bias", "swa_sinks", "gdn_conv_w", "gdn_a_log", "gdn_dt_bias", "gdn_norm_w", "w_br_swa", "w_br_gdn",
           "w_mix_o", "ln1_g", "ln1_b", "w_mem_q", "w_mem_kv", "w_mem_o", "ln2_g", "ln2_b", "w_up", "ffn_conv_w", "ffn_conv_b",
           "w_down", "ln3_g", "ln3_b"]
BIG = ["w_in", "w_br_swa", "w_br_gdn", "w_mix_o", "w_mem_q", "w_mem_kv", "w_mem_o", "w_up", "w_down"]
SMALL = [n for n in W_NAMES if n not in BIG]
COL_SHARDED = ["w_in", "w_br_swa", "w_br_gdn", "w_mem_o", "w_up"]


def _pack(arrs):
    rows = []
    for a in arrs:
        f = a.reshape(-1).astype(F32)
        rows.append(jnp.pad(f, (0, (-f.shape[0]) % LANES)).reshape(-1, LANES))
    n = sum(r.shape[0] for r in rows)
    if n % 16:
        rows.append(jnp.zeros((16 - n % 16, LANES), F32))
    return jnp.concatenate(rows, axis=0)


def _unpack(p, shapes):
    out, off = [], 0
    for s in shapes:
        n = int(np.prod(s)) if len(s) else 1
        r = -(-n // LANES)
        out.append(p[off:off + r].reshape(-1)[:n].reshape(s))
        off += r
    return out


def _merge_shards(d):
    cat = lambda names: jnp.concatenate([d[n] for n in names], axis=-2)
    return [d.get("w_in"), d["w_up"], cat(["w_br_swa", "w_br_gdn", "w_mem_q", "w_mem_o"]), cat(["w_mix_o", "w_down"]), d["w_mem_kv"]]


def _split_shards(ts):
    a, b, c, dd, e = ts
    return {"w_in": a, "w_up": b, "w_br_swa": c[..., 0:1024, :], "w_br_gdn": c[..., 1024:2048, :], "w_mem_q": c[..., 2048:2560, :],
            "w_mem_o": c[..., 2560:3072, :], "w_mix_o": dd[..., 0:512, :], "w_down": dd[..., 512:, :], "w_mem_kv": e}


def _to_full(name, t):
    if name in COL_SHARDED:
        return _cols_from_chips(t, [(0, 4 * t.shape[2])])
    return t.reshape(4 * t.shape[1], t.shape[2])


def _to_chips(name, t):
    if name in COL_SHARDED:
        return _chips_from_cols(t, [(0, t.shape[1])], t.shape[1] // 4)
    return t.reshape(4, t.shape[0] // 4, t.shape[1])


def _cols_from_chips(g, segs):
    C, parts = g.shape[2], []
    for s in segs:
        if isinstance(s, int):
            parts.append(jnp.zeros((g.shape[1], s), g.dtype))
            continue
        lo, hi = s
        while lo < hi:
            k = lo // C
            e = min(hi, (k + 1) * C)
            parts.append(g[k][:, lo - k * C:e - k * C])
            lo = e
    return jnp.concatenate(parts, axis=1)


def _chips_from_cols(p, segs, C):
    out = []
    for k in range(4):
        lo, hi, parts, o = k * C, (k + 1) * C, [], 0
        for plo, w in segs:
            a, b = max(lo, o), min(hi, o + w)
            if a < b:
                parts.append(p[:, plo + a - o:plo + b - o])
            o += w
        out.append(jnp.concatenate(parts, axis=1))
    return jnp.stack(out)


_IN_OFF = np.cumsum((0,) + IN_WIDTHS)
_IN_SEGS = [(P_SQ, SWA_Q), (P_SK, SWA_KV), (P_SV, SWA_KV), (P_GQKV, 3 * GDN_W), (P_GZ, GDN_W), (P_BA, 2 * GDN_HEADS),
            (P_GS, D_MODEL), (P_GG, D_MODEL)]
_IN_PADDED = [(int(_IN_OFF[i]), int(_IN_OFF[k])) for i, k in ((9, 10), (10, 11), (3, 6), (6, 7), (0, 1), (1, 2), (2, 3), (7, 9))] + [
    P_END - P_BA - 2 * GDN_HEADS]
_UP_SEGS = [(0, D_FF), (D_FF_PAD, D_FF)]
_UP_PADDED = [(0, D_FF), D_FF_PAD - D_FF, (D_FF, 2 * D_FF), D_FF_PAD - D_FF]


def _in_to_padded(w):
    o = _IN_OFF
    cut = lambda i, k: w[:, o[i]:o[k]]
    return jnp.concatenate([cut(9, 10), cut(10, 11), cut(3, 6), cut(6, 7), cut(0, 1), cut(1, 2), cut(2, 3), cut(7, 9),
                            jnp.zeros((w.shape[0], P_END - P_BA - 2 * GDN_HEADS), w.dtype)], axis=1)


def _in_from_padded(p):
    return jnp.concatenate([p[:, P_SQ:P_SQ + SWA_Q], p[:, P_SK:P_SK + SWA_KV], p[:, P_SV:P_SV + SWA_KV], p[:, P_GQKV:P_GQKV + 3 * GDN_W],
                            p[:, P_GZ:P_GZ + GDN_W], p[:, P_BA:P_BA + 2 * GDN_HEADS], p[:, P_GS:P_GS + D_MODEL], p[:, P_GG:P_GG + D_MODEL]],
                           axis=1)


def _ff_pad(t, axis):
    g, u = jnp.split(t, 2, axis=axis)
    pad = [(0, 0)] * t.ndim
    pad[axis] = (0, D_FF_PAD - D_FF)
    return jnp.concatenate([jnp.pad(g, pad), jnp.pad(u, pad)], axis=axis)


def _ff_unpad(t, axis):
    g, u = jnp.split(t, 2, axis=axis)
    return jnp.concatenate([lax.slice_in_dim(g, 0, D_FF, axis=axis), lax.slice_in_dim(u, 0, D_FF, axis=axis)], axis=axis)


def _assemble_weights(full, small):
    W = dict(small)
    W["in_p"] = _in_to_padded(full["w_in"])
    W["up_p"] = _ff_pad(full["w_up"], 1)
    W["down_p"] = jnp.pad(full["w_down"], ((0, D_FF_PAD - D_FF), (0, 0)))
    W["br_swa"], W["br_gdn"], W["mix_o"] = full["w_br_swa"], full["w_br_gdn"], full["w_mix_o"]
    W["mem_q"], W["mem_kv"], W["mem_o"] = full["w_mem_q"], full["w_mem_kv"], full["w_mem_o"]
    W["ffn_conv_w_p"] = _ff_pad(small["ffn_conv_w"], 1)
    W["ffn_conv_b_p"] = _ff_pad(small["ffn_conv_b"].reshape(1, -1), 1)
    return W


def _full_grads(G):
    out = {"w_in": _in_from_padded(G["in_p"])} if "in_p" in G else {}
    out.update({"w_up": _ff_unpad(G["up_p"], 1), "w_down": G["down_p"][:D_FF], "w_br_swa": G["br_swa"], "w_br_gdn": G["br_gdn"],
                "w_mix_o": G["mix_o"], "w_mem_q": G["mem_q"], "w_mem_kv": G["mem_kv"], "w_mem_o": G["mem_o"]})
    return out


def kernel(x, mem, w_in, rel_bias, swa_sinks, gdn_conv_w, gdn_a_log, gdn_dt_bias, gdn_norm_w, w_br_swa, w_br_gdn, w_mix_o, ln1_g, ln1_b, w_mem_q, w_mem_kv, w_mem_o, ln2_g, ln2_b, w_up, ffn_conv_w, ffn_conv_b, w_down, ln3_g, ln3_b, loss_target, m_w_in, m_rel_bias, m_swa_sinks, m_gdn_conv_w, m_gdn_a_log, m_gdn_dt_bias, m_gdn_norm_w, m_w_br_swa, m_w_br_gdn, m_w_mix_o, m_ln1_g, m_ln1_b, m_w_mem_q, m_w_mem_kv, m_w_mem_o, m_ln2_g, m_ln2_b, m_w_up, m_ffn_conv_w, m_ffn_conv_b, m_w_down, m_ln3_g, m_ln3_b, v_w_in, v_rel_bias, v_swa_sinks, v_gdn_conv_w, v_gdn_a_log, v_gdn_dt_bias, v_gdn_norm_w, v_w_br_swa, v_w_br_gdn, v_w_mix_o, v_ln1_g, v_ln1_b, v_w_mem_q, v_w_mem_kv, v_w_mem_o, v_ln2_g, v_ln2_b, v_w_up, v_ffn_conv_w, v_ffn_conv_b, v_w_down, v_ln3_g, v_ln3_b):
    a = dict(locals())
    w = {n: a[n] for n in W_NAMES}
    m = {n: a["m_" + n] for n in W_NAMES}
    v = {n: a["v_" + n] for n in W_NAMES}
    chip = 2 * lax.axis_index("x") + lax.axis_index("y")
    core = lax.axis_index("c")
    sq = lambda t: t.reshape(t.shape[1:]) if (t.ndim > 1 and t.shape[0] == 1 and t is not rel_bias) else t

    sh_a, sh_b, sh_c, sh_d, sh_e = _merge_shards({n: sq(w[n]).astype(_CDT) for n in BIG})
    fcw_sh, gcw_sh = sq(ffn_conv_w).shape, sq(gdn_conv_w).shape
    slot = lax.broadcasted_iota(jnp.int32, (4, 1, 1), 0)

    def with_own(got, mine):
        return [jnp.where(slot == chip, t[None], g) for g, t in zip(got, mine)]

    def reduce_start(tag, gch):
        pair = []
        for t, (mine, got) in enumerate(zip(gch, _pair_swap(tag, gch))):
            rh = mine.shape[1] // 2
            mine_h = lax.dynamic_slice_in_dim(mine, core * rh, rh, axis=1)
            pair.append(_addn(f"pair_sum_{tag}{t}", [mine_h.reshape(4 * rh, -1), got.reshape(4 * rh, -1)], _GDT).reshape(4, rh, -1))
        return pair

    def reduce_end(tag, pair, others):
        halves = []
        for t, (p, o) in enumerate(zip(pair, others)):
            own = lax.dynamic_index_in_dim(p, chip, 0, keepdims=False)
            halves.append(_addn(f"chip_sum_{tag}{t}", [own, (o, 0), (o, 1), (o, 2)]))
        return halves

    class MeshWeights:
        def w1(self):
            mine = [sh_a, _pack([sq(ffn_conv_w), sq(gdn_conv_w)])]
            got_a, got_f = with_own(_run_rider("gather_first", _gather_rider(mine)), mine)
            conv = [_unpack(got_f[k], [fcw_sh, gcw_sh]) for k in range(4)]
            W = {n: sq(w[n]) for n in SMALL}
            W["ffn_conv_w"] = jnp.concatenate([cv[0] for cv in conv], axis=1)
            W["gdn_conv_w"] = jnp.concatenate([cv[1] for cv in conv], axis=1)
            W["ffn_conv_w_p"] = _ff_pad(W["ffn_conv_w"], 1)
            W["ffn_conv_b_p"] = _ff_pad(W["ffn_conv_b"].reshape(1, -1), 1)
            W["in_p"] = _cols_from_chips(got_a, _IN_PADDED)
            return W

        def rider_a(self):
            return _gather_rider([sh_c, sh_d, sh_e])

        def w2(self, got):
            c, d, e = with_own(got, [sh_c, sh_d, sh_e])
            f = {n: _to_full(n, t) for n, t in _split_shards([None, None, c, d, e]).items() if t is not None}
            return {"br_swa": f["w_br_swa"], "br_gdn": f["w_br_gdn"], "mix_o": f["w_mix_o"], "mem_q": f["w_mem_q"], "mem_kv": f["w_mem_kv"],
                    "mem_o": f["w_mem_o"], "down_p": jnp.pad(f["w_down"], ((0, D_FF_PAD - D_FF), (0, 0)))}

        def rider_b(self):
            return _gather_rider([sh_b])

        def w3(self, got):
            b, = with_own(got, [sh_b])
            return {"up_p": _cols_from_chips(b, _UP_PADDED)}

        def rider_g(self, G):
            gf = _full_grads(G)
            gch = {n: _to_chips(n, gf[n]) for n in BIG if n not in ("w_in", "w_up")}
            gch["w_up"] = None
            self.pair = reduce_start("rest", _merge_shards(gch)[2:])
            return _scatter_rider(self.pair)

        def g_done(self, got):
            self.halves = reduce_end("rest", self.pair, got)

        def rider_up(self, G):
            self.pair_up = reduce_start("up", [_chips_from_cols(G["up_p"], _UP_SEGS, 2 * D_FF // 4)])
            return _scatter_rider(self.pair_up)

        def up_done(self, got):
            self.halves = reduce_end("up", self.pair_up, got) + self.halves

        def rider_last(self, G):
            self.pair_in = reduce_start("in", [_chips_from_cols(G["in_p"], _IN_SEGS, sum(IN_WIDTHS) // 4)])
            return _scatter_rider(self.pair_in)

        def last_done(self, got):
            self.halves = reduce_end("in", self.pair_in, got) + self.halves

    src = MeshWeights()
    loss, dx, G = _fwd_bwd(x[0], mem[0], loss_target[0], src)

    small_names = SMALL
    small_shapes = [()] + [tuple(G[n].shape) for n in small_names]
    packed = _pack([loss] + [G[n] for n in small_names])
    allp = _allgather8(packed)
    tot = _addn("small_sum", [(allp, k) for k in range(8)])
    parts = _unpack(tot, small_shapes)
    loss_tot, gsmall = parts[0], dict(zip(small_names, parts[1:]))
    gsmall["ffn_conv_w"] = lax.dynamic_slice_in_dim(gsmall["ffn_conv_w"], chip * fcw_sh[1], fcw_sh[1], axis=1)
    gsmall["gdn_conv_w"] = lax.dynamic_slice_in_dim(gsmall["gdn_conv_w"], chip * gcw_sh[1], gcw_sh[1], axis=1)

    both = []
    for h, o in zip(src.halves, _pair_exchange(src.halves)):
        both.append(jnp.concatenate([jnp.where(core == 0, h, o), jnp.where(core == 0, o, h)], axis=0))
    gbig = _split_shards(both)

    outs = {}
    for n in BIG:
        d_, m_, v_ = _adamw("adamw_" + n, sq(w[n]), gbig[n], sq(m[n]), sq(v[n]))
        outs[n] = (gbig[n], d_, m_, v_)
    for n in SMALL:
        two_d = (-1, w[n].shape[-1])
        g_ = gsmall[n].reshape(two_d)
        d_, m_, v_ = _adamw("adamw_" + n, w[n].reshape(two_d), g_, m[n].reshape(two_d), v[n].reshape(two_d))
        outs[n] = (g_, d_, m_, v_)

    res = [loss_tot.reshape(()), dx.reshape(x.shape)]
    for k in range(4):
        res += [outs[n][k].reshape(w[n].shape) for n in W_NAMES]
    return tuple(res)
```

```python
import functools
import math

import jax
import jax.numpy as jnp
import numpy as np
from jax import lax
from jax.experimental import pallas as pl
from jax.experimental.pallas import tpu as pltpu

F32 = jnp.float32
BF16 = jnp.bfloat16
_CDT = BF16
_GDT = BF16

D_MODEL = 2048
SWA_HEADS, SWA_KV_HEADS, SWA_HEAD_DIM, SWA_BLOCK = 16, 2, 64, 128
SWA_GRP = SWA_HEADS // SWA_KV_HEADS
REL_BUCKETS, REL_MAX_DIST = 32, 128
GDN_HEADS, GDN_HEAD_DIM, GDN_CONV, GDN_CHUNK = 8, 128, 4, 64
MEM_HEADS, MEM_HEAD_DIM = 4, 128
D_FF, D_FF_PAD, FFN_CONV = 5504, 5632, 3
SWA_Q, SWA_KV, GDN_W, MEM_W = 1024, 128, 1024, 512
IN_WIDTHS = (SWA_Q, SWA_KV, SWA_KV, GDN_W, GDN_W, GDN_W, GDN_W, GDN_HEADS, GDN_HEADS, D_MODEL, D_MODEL)
NORM_EPS = 1e-5
ALPHA = 2.0 ** 0.25
NEG_INF = -1e30
ADAM_LR, ADAM_B1, ADAM_B2, ADAM_EPS, ADAM_WD, ADAM_STEP = 0.001, 0.9, 0.999, 1e-08, 0.01, 10
LANES, SUBLANES = 128, 8
VMEM_LIMIT = 56 * 1024 * 1024

P_GS, P_GG, P_GQKV, P_GZ, P_SQ, P_SK, P_SV, P_BA, P_USED, P_END = 0, 2048, 4096, 7168, 8192, 9216, 9344, 9472, 9600, 9728


def _tile(dim, pref, align=LANES):
    if dim <= pref:
        return dim
    t = (pref // align) * align
    while t >= align:
        if dim % t == 0:
            return t
        t -= align
    return dim


_DIMS = {"nn": (((1,), (0,)), ((), ())), "nt": (((1,), (1,)), ((), ())), "tn": (((0,), (0,)), ((), ()))}
_BDIMS = {"nn": (((2,), (1,)), ((0,), (0,))), "nt": (((2,), (2,)), ((0,), (0,))), "tn": (((1,), (1,)), ((0,), (0,)))}


def _raw_dot(a, b, form, hi):
    dims = (_BDIMS if a.ndim == 3 else _DIMS)[form]
    if hi == "x3":
        a, b = a.astype(F32), b.astype(F32)
        ah, bh = a.astype(BF16), b.astype(BF16)
        al, bl = (a - ah.astype(F32)).astype(BF16), (b - bh.astype(F32)).astype(BF16)
        d = lambda p, q: lax.dot_general(p, q, dims, preferred_element_type=F32)
        if form == "tn":
            return d(ah, bh) + (d(ah, bl) + d(al, bh))
        m = a.shape[-2]
        both = d(jnp.concatenate([ah, al], axis=-2), bh)
        return both[..., :m, :] + (d(ah, bl) + both[..., m:, :])
    if hi:
        return lax.dot_general(a.astype(F32), b.astype(F32), dims, precision=lax.Precision.HIGHEST, preferred_element_type=F32)
    return lax.dot_general(a.astype(_CDT), b.astype(_CDT), dims, preferred_element_type=F32)


@functools.partial(jax.custom_vjp, nondiff_argnums=(2, 3))
def _dot(a, b, form, hi=False):
    return _raw_dot(a, b, form, hi)


def _dot_fwd(a, b, form, hi):
    return _raw_dot(a, b, form, hi), (a, b)


def _dot_bwd(form, hi, res, g):
    a, b = res
    if form == "nn":
        da, db = _raw_dot(g, b, "nt", hi), _raw_dot(a, g, "tn", hi)
    elif form == "nt":
        da, db = _raw_dot(g, b, "nn", hi), _raw_dot(g, a, "tn", hi)
    else:
        da, db = _raw_dot(b, g, "nt", hi), _raw_dot(a, g, "nn", hi)
    return da.astype(a.dtype), db.astype(b.dtype)


_dot.defvjp(_dot_fwd, _dot_bwd)


@functools.partial(jax.custom_vjp, nondiff_argnums=(2,))
def _shift_halo(prev, cur, d):
    assert prev.shape[0] == SUBLANES
    return pltpu.roll(jnp.concatenate([prev, cur], axis=0), d, 0)[SUBLANES:]


def _shift_halo_fwd(prev, cur, d):
    return _shift_halo(prev, cur, d), None


def _shift_halo_bwd(d, _, g):
    nh = SUBLANES
    ext = jnp.concatenate([jnp.zeros((nh, g.shape[1]), g.dtype), g], axis=0)
    r = pltpu.roll(ext, ext.shape[0] - d, 0)
    return r[:nh], r[nh:]


_shift_halo.defvjp(_shift_halo_fwd, _shift_halo_bwd)


@jax.custom_vjp
def _recip(x):
    return 1.0 / x


def _recip_fwd(x):
    r = 1.0 / x
    return r, r


def _recip_bwd(r, g):
    return (-g * r * r,)


_recip.defvjp(_recip_fwd, _recip_bwd)


def _sigmoid(x):
    return _recip(1.0 + jnp.exp(-x))


def _silu(x):
    return x * _sigmoid(x)


def _softplus(x):
    return jnp.maximum(x, 0.0) + jnp.log(1.0 + jnp.exp(-jnp.abs(x)))


def _iota(shape, axis):
    return lax.broadcasted_iota(jnp.int32, shape, axis)


def _cparams(sem, **kw):
    return pltpu.CompilerParams(dimension_semantics=sem, vmem_limit_bytes=VMEM_LIMIT, **kw)


class _ride:
    def __init__(self, rider, n_in, n_out, n_scr):
        self.rider = rider
        self.ins = rider.ins if rider else []
        n_rin = len(self.ins)
        self.out_shapes = rider.out_shapes if rider else []
        n_rout = len(self.out_shapes)
        self.in_specs, self.out_specs = [_HBM] * n_rin, [_HBM] * n_rout
        self.scratch = rider.sems() if rider else []
        self.o0 = n_in + n_rin
        self.s0 = self.o0 + n_out + n_rout
        self._rin = slice(n_in, n_in + n_rin)
        self._rout = slice(self.o0 + n_out, self.s0)
        self._sem = self.s0 + n_scr

    def _args(self, refs):
        return refs[self._rin], refs[self._rout], refs[self._sem], refs[self._sem + 1]

    def at_start(self, refs, cond):
        if self.rider:
            pl.when(cond)(lambda: self.rider.start(*self._args(refs)))

    def at_end(self, refs, cond):
        if self.rider:
            pl.when(cond)(lambda: self.rider.finish(*self._args(refs)))


def _mm(name, a, b, form, out_dtype=F32, add=None, add_scale=1.0, hi=False, tm=1024, tn=1024, tk=2816, rider=None, b_k0=None):
    if form == "nn":
        (M, K), (K2, N) = a.shape, b.shape
    elif form == "nt":
        (M, K), (N, K2) = a.shape, b.shape
        K2 = K if b_k0 is not None else K2
    else:
        (K, M), (K2, N) = a.shape, b.shape
    assert K == K2, (name, a.shape, b.shape, form)
    tm, tn, tk = _tile(M, tm), _tile(N, tn), _tile(K, tk)
    nk = K // tk
    k0 = 0 if b_k0 is None else b_k0 // tk
    assert b_k0 is None or (form == "nt" and b_k0 % tk == 0)
    a_spec = pl.BlockSpec((tk, tm), lambda i, j, k: (k, i)) if form == "tn" else pl.BlockSpec((tm, tk), lambda i, j, k: (i, k))
    b_spec = pl.BlockSpec((tn, tk), lambda i, j, k: (j, k + k0)) if form == "nt" else pl.BlockSpec((tk, tn), lambda i, j, k: (k, j))
    o_spec = pl.BlockSpec((tm, tn), lambda i, j, k: (i, j))
    has_add = add is not None

    def finish(r, c_ref, o_ref):
        if has_add:
            r = r + add_scale * c_ref[...].astype(F32)
        o_ref[...] = r.astype(out_dtype)

    n_own = 3 if has_add else 2
    grid = (M // tm, N // tn, nk)
    rd = _ride(rider, n_own, 1, 1 if nk > 1 else 0)

    def body(*refs):
        a_ref, b_ref = refs[:2]
        c_ref = refs[2] if has_add else None
        o_ref = refs[rd.o0]
        pid = [pl.program_id(d) for d in range(3)]
        rd.at_start(refs, (pid[0] == 0) & (pid[1] == 0) & (pid[2] == 0))
        if nk == 1:
            finish(_raw_dot(a_ref[...], b_ref[...], form, hi), c_ref, o_ref)
        else:
            acc = refs[rd.s0]

            @pl.when(pid[2] == 0)
            def _():
                acc[...] = jnp.zeros_like(acc)

            acc[...] += _raw_dot(a_ref[...], b_ref[...], form, hi)

            @pl.when(pid[2] == nk - 1)
            def _():
                finish(acc[...], c_ref, o_ref)
        rd.at_end(refs, (pid[0] == grid[0] - 1) & (pid[1] == grid[1] - 1) & (pid[2] == nk - 1))

    ins = [a, b] + ([add] if has_add else [])
    specs = [a_spec, b_spec] + ([o_spec] if has_add else [])
    res = pl.pallas_call(
        body, name=name, grid=grid, in_specs=specs + rd.in_specs, out_specs=[o_spec] + rd.out_specs,
        out_shape=[jax.ShapeDtypeStruct((M, N), out_dtype)] + rd.out_shapes,
        scratch_shapes=([pltpu.VMEM((tm, tn), F32)] if nk > 1 else []) + rd.scratch,
        compiler_params=_cparams(("arbitrary",) * 3 if rider else ("parallel", "parallel", "arbitrary")),
    )(*ins, *rd.ins)
    return (res[0], res[1:]) if rider else res[0]


class Row:
    def __init__(self, arr, blk, imap, hblk=None, hmap=None, gshape=None, gmap=None, gdt=(F32,)):
        self.arr, self.blk, self.imap, self.hblk, self.hmap, self.gshape, self.gmap = arr, blk, imap, hblk, hmap, gshape, gmap
        self.gdt = gdt


class Par:
    def __init__(self, arr, blk=None, imap=None, gshape=None, gmap=None):
        self.arr = arr
        self.blk = tuple(arr.shape) if blk is None else blk
        nd = len(self.blk)
        self.imap = (lambda j: (0,) * nd) if imap is None else imap
        self.gshape, self.gmap = gshape, gmap


class Out:
    def __init__(self, shape, dtype, blk, imap):
        self.shape, self.dtype, self.blk, self.imap = shape, dtype, blk, imap


def _rows_of(blk):
    return [d for d in blk if d is not None][0]


def _rowmap(name, fn, ncol, nblk, rows, pars, outs, accs=()):
    in_specs, ins = [], []
    for r in rows:
        ins.append(r.arr)
        in_specs.append(pl.BlockSpec(r.blk, r.imap))
        if r.hblk is not None:
            ins.append(r.arr)
            in_specs.append(pl.BlockSpec(r.hblk, r.hmap))
    for p in pars:
        ins.append(p.arr)
        in_specs.append(pl.BlockSpec(p.blk, (lambda im: (lambda j, n: im(j)))(p.imap)))
    out_specs = [pl.BlockSpec(o.blk, o.imap) for o in outs]
    out_shape = [jax.ShapeDtypeStruct(o.shape, o.dtype) for o in outs]
    for a in accs:
        out_specs.append(pl.BlockSpec(a, (lambda nd: (lambda j, n: (0,) * nd))(len(a))))
        out_shape.append(jax.ShapeDtypeStruct(a, F32))
    n_in = len(ins)

    def body(*refs):
        j, n = pl.program_id(0), pl.program_id(1)
        it = iter(refs[:n_in])
        rvals = []
        for r in rows:
            cur = next(it)[...]
            rvals.append((next(it)[...], cur) if r.hblk is not None else cur)
        pvals = [next(it)[...] for _ in pars]
        o_refs = refs[n_in:n_in + len(outs)]
        a_refs = refs[n_in + len(outs):]
        ovals, avals = fn(j, n == 0, rvals, pvals)
        for ref, v in zip(o_refs, ovals):
            ref[...] = v.astype(ref.dtype)
        if accs:
            @pl.when((j == 0) & (n == 0))
            def _():
                for ref in a_refs:
                    ref[...] = jnp.zeros_like(ref)
            for ref, v in zip(a_refs, avals):
                ref[...] += v

    res = pl.pallas_call(
        body, name=name, grid=(ncol, nblk), in_specs=in_specs, out_specs=out_specs, out_shape=out_shape,
        compiler_params=_cparams(("arbitrary", "arbitrary")),
    )(*ins)
    return res


def _rowmap_bwd(name, fn, ncol, nblk, rows, pars, cts):
    rev = lambda im: (lambda j, s: im(j, nblk - 1 - s))
    in_specs, ins = [], []
    for r in rows:
        ins.append(r.arr)
        in_specs.append(pl.BlockSpec(r.blk, rev(r.imap)))
        if r.hblk is not None:
            ins.append(r.arr)
            in_specs.append(pl.BlockSpec(r.hblk, rev(r.hmap)))
    for p in pars:
        ins.append(p.arr)
        in_specs.append(pl.BlockSpec(p.blk, (lambda im: (lambda j, s: im(j)))(p.imap)))
    for c in cts:
        ins.append(c.arr)
        in_specs.append(pl.BlockSpec(c.blk, rev(c.imap)))
    n_in = len(ins)
    drows = [i for i, r in enumerate(rows) if r.gshape is not None]
    dpars = [i for i, p in enumerate(pars) if p.gshape is not None]
    out_specs, out_shape, scratch = [], [], []
    for i in drows:
        r = rows[i]
        for dt in r.gdt:
            out_specs.append(pl.BlockSpec(r.blk, rev(r.gmap)))
            out_shape.append(jax.ShapeDtypeStruct(r.gshape, dt))
        if r.hblk is not None:
            scratch.append(pltpu.VMEM(tuple(d for d in r.hblk if d is not None), F32))
    n_drow_out = len(out_specs)
    for i in dpars:
        p = pars[i]
        out_specs.append(pl.BlockSpec(p.blk, (lambda im: (lambda j, s: im(j)))(p.gmap)))
        out_shape.append(jax.ShapeDtypeStruct(p.gshape, F32))

    def body(*refs):
        j, s = pl.program_id(0), pl.program_id(1)
        first = s == nblk - 1
        it = iter(refs[:n_in])
        rvals = []
        for r in rows:
            cur = next(it)[...]
            rvals.append((next(it)[...], cur) if r.hblk is not None else cur)
        pvals = [next(it)[...] for _ in pars]
        cvals = [next(it)[...].astype(F32) for _ in cts]
        g_refs = iter(refs[n_in:n_in + n_drow_out])
        p_refs = refs[n_in + n_drow_out:n_in + n_drow_out + len(dpars)]
        carries = iter(refs[n_in + n_drow_out + len(dpars):])

        def f(dr, dp):
            rv, pv = list(rvals), list(pvals)
            for i, v in zip(drows, dr):
                rv[i] = v
            for i, v in zip(dpars, dp):
                pv[i] = v
            return fn(j, first, rv, pv)

        _, vjp = jax.vjp(f, [rvals[i] for i in drows], [pvals[i] for i in dpars])
        g_r, g_p = vjp(cvals)
        for i, g in zip(drows, g_r):
            r = rows[i]
            if r.hblk is None:
                for _ in r.gdt:
                    ref = next(g_refs)
                    ref[...] = g.astype(ref.dtype)
            else:
                g_prev, g_cur = g
                carry = next(carries)
                nr, nh = g_cur.shape[-2], g_prev.shape[-2]
                tail = g_cur[..., nr - nh:nr, :] + jnp.where(s > 0, carry[...], 0.0)
                for _ in r.gdt:
                    ref = next(g_refs)
                    if nr > nh:
                        ref[..., 0:nr - nh, :] = g_cur[..., 0:nr - nh, :].astype(ref.dtype)
                    ref[..., nr - nh:nr, :] = tail.astype(ref.dtype)
                carry[...] = g_prev
        for ref, g in zip(p_refs, g_p):
            @pl.when(s == 0)
            def _():
                ref[...] = jnp.zeros_like(ref)
            ref[...] += g

    return pl.pallas_call(
        body, name=name, grid=(ncol, nblk), in_specs=in_specs, out_specs=out_specs, out_shape=out_shape,
        scratch_shapes=scratch, compiler_params=_cparams(("arbitrary", "arbitrary")),
    )(*ins)


def _rowspec(arr, tb, cw, c0, cstep=1, halo=0, grad=False, ncol=1, gdt=(F32,)):
    T = arr.shape[0]
    imap = lambda j, n: (n, c0 + cstep * j)
    hblk = hmap = None
    if halo:
        q = tb // halo
        hblk, hmap = (halo, cw), (lambda j, n: (jnp.maximum(n * q - 1, 0), c0 + cstep * j))
    gshape = (T, cw * (ncol if cstep else 1)) if grad else None
    gmap = (lambda j, n: (n, cstep * j)) if grad else None
    return Row(arr, (tb, cw), imap, hblk, hmap, gshape, gmap, gdt)


def _parspec(arr, cw=None, c0=0, grad=False, ncol=1):
    if cw is None:
        return Par(arr, gshape=tuple(arr.shape) if grad else None,
                   gmap=(lambda nd: (lambda j: (0,) * nd))(arr.ndim) if grad else None)
    r = arr.shape[0]
    return Par(arr, (r, cw), lambda j: (0, c0 + j), (r, cw * ncol) if grad else None, (lambda j: (0, j)) if grad else None)


def _ln(r, g, b):
    mu = jnp.mean(r, axis=-1, keepdims=True)
    xc = r - mu
    var = jnp.mean(xc * xc, axis=-1, keepdims=True)
    return xc * lax.rsqrt(var + NORM_EPS) * g + b


def _ln_fn(j, first, rv, pv):
    return [_ln(rv[0], pv[0], pv[1])]


def _ln_fwd_fn(j, first, rv, pv):
    y = _ln(rv[0], pv[0], pv[1])
    return [y, y], []


def _loss_fn(j, first, rv, pv):
    r3, tgt = rv
    g, b = pv
    y, vjp = jax.vjp(_ln, r3, g, b)
    diff = y - tgt
    part = 0.5 * jnp.sum(diff * diff) / D_MODEL
    dr, dg, db = vjp(diff * (1.0 / D_MODEL))
    return [dr, dr], [jnp.full((SUBLANES, LANES), part, F32), dg, db]


def _mix_fn(j, first, rv, pv):
    gs, gg, ys, yg = rv
    return [_sigmoid(gs) * ys + _sigmoid(gg) * yg]


def _row_pick(x, i):
    ax = x.ndim - 2
    return jnp.sum(jnp.where(_iota(x.shape, ax) == i, x, 0.0), axis=ax, keepdims=True)


def _causal_conv(prev, cur, w, first):
    width = w.shape[0]
    prev = jnp.where(first, 0.0, prev)
    y = cur * _row_pick(w, width - 1)
    for d in range(1, width):
        y = y + _shift_halo(prev, cur, d) * _row_pick(w, width - 1 - d)
    return y


def _ffn_act_fn(j, first, rv, pv):
    (pg, cg), (pu, cu) = rv
    wg, wu, bg, bu = pv
    hg = _causal_conv(pg, cg, wg, first) + bg
    hu = _causal_conv(pu, cu, wu, first) + bu
    return [_silu(hg) * hu]


def _gdn_pre_fn(j, first, rv, pv):
    (prev, cur), = rv
    w, = pv
    t = _silu(_causal_conv(prev, cur, w, first))
    tn = t * lax.rsqrt(jnp.sum(t * t, axis=-1, keepdims=True) + 1e-6)
    return [jnp.where(j < 2 * GDN_HEADS, tn, t)]


def _gdn_gate_fn(j, first, rv, pv):
    gba, = rv
    alog, dtb, eb, eg = pv
    tb = gba.shape[0]
    beta = _sigmoid(gba)
    g = -jnp.exp(alog) * _softplus(gba + dtb)
    ri, ci = _iota((tb, tb), 0), _iota((tb, tb), 1)
    tril = jnp.where((ri // GDN_CHUNK == ci // GDN_CHUNK) & (ci <= ri), 1.0, 0.0)
    gc = _dot(tril, g, "nn", True)
    return [_dot(beta, eb, "nn", True), _dot(gc, eg, "nn", True)]


def _swa_fn(j, first, rv, pv):
    q, (kp, kc), (vp, vc) = rv
    bp, bc, sk = pv
    sp = _dot(q, kp, "nt") * (SWA_HEAD_DIM ** -0.5) + bp
    sc = _dot(q, kc, "nt") * (SWA_HEAD_DIM ** -0.5) + bc
    qi = _iota(sp.shape, sp.ndim - 2) % SWA_BLOCK
    kj = _iota(sp.shape, sp.ndim - 1)
    sp = jnp.where((kj > qi) & jnp.logical_not(first), sp, NEG_INF)
    sc = jnp.where(kj <= qi, sc, NEG_INF)
    m = jnp.maximum(jnp.maximum(jnp.max(sp, axis=-1, keepdims=True), jnp.max(sc, axis=-1, keepdims=True)), sk)
    m = lax.stop_gradient(m)
    ep, ec, es = jnp.exp(sp - m), jnp.exp(sc - m), jnp.exp(sk - m)
    inv = 1.0 / (jnp.sum(ep, axis=-1, keepdims=True) + jnp.sum(ec, axis=-1, keepdims=True) + es)
    vp = jnp.where(first, 0.0, vp)
    return [_dot(ep * inv, vp, "nn") + _dot(ec * inv, vc, "nn")]


def _memattn_fn(j, first, rv, pv):
    q, = rv
    k, v = pv
    s = _dot(q, k, "nt") * (MEM_HEAD_DIM ** -0.5)
    m = lax.stop_gradient(jnp.max(s, axis=-1, keepdims=True))
    e = jnp.exp(s - m)
    p = e * (1.0 / jnp.sum(e, axis=-1, keepdims=True))
    return [_dot(p, v, "nn")]


SOLVE_PREC = "x3"


@jax.custom_vjp
def _unit_lower_inv(a):
    c = a.shape[-1]
    eye = _iota((1, c, c), 1) == _iota((1, c, c), 2)
    tinv = jnp.where(eye, 1.0, 0.0) - a
    x = _raw_dot(a, a, "nn", SOLVE_PREC)
    for i in range(5):
        tinv = tinv + _raw_dot(tinv, x, "nn", SOLVE_PREC)
        if i < 4:
            x = _raw_dot(x, x, "nn", SOLVE_PREC)
    return tinv


def _unit_lower_inv_fwd(a):
    t = _unit_lower_inv(a)
    return t, t


def _unit_lower_inv_bwd(t, g):
    return (-_raw_dot(_raw_dot(t, g, "tn", SOLVE_PREC), t, "nt", SOLVE_PREC),)


_unit_lower_inv.defvjp(_unit_lower_inv_fwd, _unit_lower_inv_bwd)


@jax.custom_vjp
def _known_inv(a, t):
    return t


def _known_inv_fwd(a, t):
    return t, t


def _known_inv_bwd(t, g):
    return _unit_lower_inv_bwd(t, g) + (jnp.zeros_like(t),)


_known_inv.defvjp(_known_inv_fwd, _known_inv_bwd)


def _gdn_heads(q, k, v, bx, gx, g64, z, nw, S, tinv=None, keep_tinv=False):
    c = GDN_CHUNK
    q = q * (GDN_HEAD_DIM ** -0.5)
    kb, vb = k * bx, v * bx
    ri, ci = _iota((1, c, c), 1), _iota((1, c, c), 2)
    tril, strict, eye = ci <= ri, ci < ri, ci == ri
    grow = jnp.sum(jnp.where(eye, g64, 0.0), axis=1, keepdims=True)
    decay = jnp.where(tril, jnp.exp(jnp.where(tril, g64 - grow, 0.0)), 0.0)
    a = jnp.where(strict, _dot(kb, k, "nt") * decay, 0.0)
    tinv = _unit_lower_inv(a) if tinv is None else _known_inv(a, tinv)
    eg = jnp.exp(gx)
    u = _dot(tinv, vb, "nn", SOLVE_PREC)
    w = _dot(tinv, kb * eg, "nn", SOLVE_PREC)
    ai = jnp.where(tril, _dot(q, k, "nt") * decay, 0.0)
    glast = _row_pick(gx, c - 1)
    v_new = u - _dot(w, S, "nn")
    o = _dot(q * eg, S, "nn") + _dot(ai, v_new, "nn")
    s_new = S * jnp.exp(glast) + _dot(k * jnp.exp(glast - gx), v_new, "tn")
    o = o * lax.rsqrt(jnp.mean(o * o, axis=-1, keepdims=True) + 1e-6) * nw
    return (o * _silu(z), s_new, tinv) if keep_tinv else (o * _silu(z), s_new)


GDN_STEP_CHUNKS = 2


def _head_major(ref, off, width=GDN_HEAD_DIM, ci=0):
    r = slice(ci * GDN_CHUNK, (ci + 1) * GDN_CHUNK)
    return jnp.stack([ref[r, off + h * GDN_HEAD_DIM:off + h * GDN_HEAD_DIM + width] for h in range(GDN_HEADS)])


def _gdn_chunks_fwd(qkv, bx, gx, proj, nw, rider=None):
    T = qkv.shape[0]
    cps = GDN_STEP_CHUNKS
    nc, c, hd, nh = T // (cps * GDN_CHUNK), GDN_CHUNK, GDN_HEAD_DIM, GDN_HEADS
    rd = _ride(rider, 5, 3, 1)

    def body(*refs):
        qkv_ref, bx_ref, gx_ref, z_ref, nw_ref = refs[:5]
        y_ref, st_ref, ti_ref = refs[rd.o0:rd.o0 + 3]
        S = refs[rd.s0]
        rd.at_start(refs, pl.program_id(0) == 0)

        @pl.when(pl.program_id(0) == 0)
        def _():
            S[...] = jnp.zeros_like(S)

        s_new = S[...]
        for ci in range(cps):
            st_ref[ci] = s_new
            y, s_new, ti = _gdn_heads(_head_major(qkv_ref, 0, ci=ci), _head_major(qkv_ref, GDN_W, ci=ci),
                                      _head_major(qkv_ref, 2 * GDN_W, ci=ci), _head_major(bx_ref, 0, ci=ci),
                                      _head_major(gx_ref, 0, ci=ci), _head_major(gx_ref, 0, c, ci), _head_major(z_ref, 0, ci=ci),
                                      nw_ref[...], s_new, keep_tinv=True)
            ti_ref[ci] = ti
            for h in range(nh):
                y_ref[ci * c:(ci + 1) * c, h * hd:(h + 1) * hd] = y[h].astype(y_ref.dtype)
        S[...] = s_new
        rd.at_end(refs, pl.program_id(0) == nc - 1)

    row = lambda w, cb: pl.BlockSpec((cps * c, w), lambda n: (n, cb))
    res = pl.pallas_call(
        body, name="gdn_chunks_fwd", grid=(nc,),
        in_specs=[row(3 * GDN_W, 0), row(GDN_W, 0), row(GDN_W, 0), row(GDN_W, P_GZ // GDN_W),
                  pl.BlockSpec((1, hd), lambda n: (0, 0))] + rd.in_specs,
        out_specs=[row(GDN_W, 0), pl.BlockSpec((cps, nh, hd, hd), lambda n: (n, 0, 0, 0)),
                   pl.BlockSpec((cps, nh, c, c), lambda n: (n, 0, 0, 0))] + rd.out_specs,
        out_shape=[jax.ShapeDtypeStruct((T, GDN_W), BF16), jax.ShapeDtypeStruct((nc * cps, nh, hd, hd), F32),
                   jax.ShapeDtypeStruct((nc * cps, nh, c, c), F32)] + rd.out_shapes,
        scratch_shapes=[pltpu.VMEM((nh, hd, hd), F32)] + rd.scratch,
        compiler_params=_cparams(("arbitrary",)),
    )(qkv, bx, gx, proj, nw, *rd.ins)
    return res[0], (res[1], res[2]), res[3:]


def _gdn_chunks_bwd(qkv, bx, gx, proj, nw, saved, dy, rider=None):
    states, tinvs = saved
    T = qkv.shape[0]
    cps = GDN_STEP_CHUNKS
    nc, c, hd, nh = T // (cps * GDN_CHUNK), GDN_CHUNK, GDN_HEAD_DIM, GDN_HEADS
    rd = _ride(rider, 8, 5, 1)

    def body(*refs):
        qkv_ref, bx_ref, gx_ref, z_ref, nw_ref, st_ref, ti_ref, dy_ref = refs[:8]
        dqkv_ref, dbx_ref, dgx_ref, dz_ref, dnw_ref = refs[rd.o0:rd.o0 + 5]
        dS = refs[rd.s0]
        rd.at_start(refs, pl.program_id(0) == 0)

        @pl.when(pl.program_id(0) == 0)
        def _():
            dS[...] = jnp.zeros_like(dS)
            dnw_ref[...] = jnp.zeros_like(dnw_ref)

        dsp = dS[...]
        for ci in reversed(range(cps)):
            r = slice(ci * c, (ci + 1) * c)
            args = (_head_major(qkv_ref, 0, ci=ci), _head_major(qkv_ref, GDN_W, ci=ci), _head_major(qkv_ref, 2 * GDN_W, ci=ci),
                    _head_major(bx_ref, 0, ci=ci), _head_major(gx_ref, 0, ci=ci), _head_major(gx_ref, 0, c, ci),
                    _head_major(z_ref, 0, ci=ci), nw_ref[...], st_ref[ci])
            _, vjp = jax.vjp(functools.partial(_gdn_heads, tinv=ti_ref[ci]), *args)
            dq, dk, dv, dbx, dgx, dg64, dz, dnw, dsp = vjp((_head_major(dy_ref, 0, ci=ci), dsp))
            for h in range(nh):
                sl = slice(h * hd, (h + 1) * hd)
                dqkv_ref[r, sl] = dq[h].astype(dqkv_ref.dtype)
                dqkv_ref[r, GDN_W + h * hd:GDN_W + (h + 1) * hd] = dk[h].astype(dqkv_ref.dtype)
                dqkv_ref[r, 2 * GDN_W + h * hd:2 * GDN_W + (h + 1) * hd] = dv[h].astype(dqkv_ref.dtype)
                dbx_ref[r, sl] = dbx[h]
                dgx_ref[r, sl] = dgx[h]
                dgx_ref[r, h * hd:h * hd + c] += dg64[h]
                dz_ref[r, sl] = dz[h].astype(dz_ref.dtype)
            dnw_ref[...] += dnw
        dS[...] = dsp
        rd.at_end(refs, pl.program_id(0) == nc - 1)

    row = lambda w, cb: pl.BlockSpec((cps * c, w), lambda s: (nc - 1 - s, cb))
    res = pl.pallas_call(
        body, name="gdn_chunks_bwd", grid=(nc,),
        in_specs=[row(3 * GDN_W, 0), row(GDN_W, 0), row(GDN_W, 0), row(GDN_W, P_GZ // GDN_W), pl.BlockSpec((1, hd), lambda s: (0, 0)),
                  pl.BlockSpec((cps, nh, hd, hd), lambda s: (nc - 1 - s, 0, 0, 0)),
                  pl.BlockSpec((cps, nh, c, c), lambda s: (nc - 1 - s, 0, 0, 0)), row(GDN_W, 0)] + rd.in_specs,
        out_specs=[row(3 * GDN_W, 0), row(GDN_W, 0), row(GDN_W, 0), row(GDN_W, 0),
                   pl.BlockSpec((1, hd), lambda s: (0, 0))] + rd.out_specs,
        out_shape=[jax.ShapeDtypeStruct((T, 3 * GDN_W), F32), jax.ShapeDtypeStruct((T, GDN_W), F32),
                   jax.ShapeDtypeStruct((T, GDN_W), F32), jax.ShapeDtypeStruct((T, GDN_W), _CDT),
                   jax.ShapeDtypeStruct((1, hd), F32)] + rd.out_shapes,
        scratch_shapes=[pltpu.VMEM((nh, hd, hd), F32)] + rd.scratch,
        compiler_params=_cparams(("arbitrary",)),
    )(qkv, bx, gx, proj, nw, states, tinvs, dy, *rd.ins)
    res = list(res)
    return res[:5] + [res[5:]]


def _adamw(name, w, g, m, v):
    if w.ndim == 3:
        C, _, R = w.shape
        blk = (_tile(C, 768, 1), 1, _tile(R, 512))
        grid = (C // blk[0], R // blk[2])
        spec = pl.BlockSpec(blk, lambda i, j: (i, 0, j))
    else:
        R, C = w.shape
        tr = _tile(R, 128, SUBLANES)
        grid = (R // tr,)
        spec = pl.BlockSpec((tr, C), lambda i: (i, 0))

    def body(w_ref, g_ref, m_ref, v_ref, d_ref, m2_ref, v2_ref):
        g_ = g_ref[...]
        m2 = ADAM_B1 * m_ref[...] + (1.0 - ADAM_B1) * g_
        v2 = ADAM_B2 * v_ref[...] + (1.0 - ADAM_B2) * (g_ * g_)
        m_hat = m2 / (1.0 - ADAM_B1 ** ADAM_STEP)
        v_hat = v2 / (1.0 - ADAM_B2 ** ADAM_STEP)
        d_ref[...] = -ADAM_LR * (m_hat / (jnp.sqrt(v_hat) + ADAM_EPS) + ADAM_WD * w_ref[...])
        m2_ref[...] = m2
        v2_ref[...] = v2

    return pl.pallas_call(
        body, name=name, grid=grid, in_specs=[spec] * 4, out_specs=[spec] * 3,
        out_shape=[jax.ShapeDtypeStruct(w.shape, F32)] * 3, compiler_params=_cparams(("parallel",) * len(grid)),
    )(w, g, m, v)


def _addn(name, parts, out_dtype=F32):
    parts = [p if isinstance(p, tuple) else (p, None) for p in parts]
    a0, k0 = parts[0]
    R, C = a0.shape[-2:]
    tr = _tile(R, 256, 2 * SUBLANES)
    specs = []
    for a, k in parts:
        if k is None:
            specs.append(pl.BlockSpec((tr, C), lambda i: (i, 0)))
        else:
            specs.append(pl.BlockSpec((None, tr, C), (lambda kk: (lambda i: (kk, i, 0)))(k)))

    def body(*refs):
        acc = refs[0][...].astype(F32)
        for r in refs[1:-1]:
            acc = acc + r[...].astype(F32)
        refs[-1][...] = acc.astype(out_dtype)

    return pl.pallas_call(
        body, name=name, grid=(R // tr,), in_specs=specs, out_specs=pl.BlockSpec((tr, C), lambda i: (i, 0)),
        out_shape=jax.ShapeDtypeStruct((R, C), out_dtype), compiler_params=_cparams(("parallel",)),
    )(*[a for a, _ in parts])


MESH = pl.DeviceIdType.MESH
_HBM = pl.BlockSpec(memory_space=pltpu.HBM)


def _place():
    x, y, c = lax.axis_index("x"), lax.axis_index("y"), lax.axis_index("c")
    return x, y, c, [(1 - x, y), (x, 1 - y), (1 - x, 1 - y)]


class _Rider:
    def __init__(self, ins, out_shapes, nsem, start, finish):
        self.ins, self.out_shapes, self.nsem, self.start, self.finish = list(ins), list(out_shapes), nsem, start, finish

    def sems(self):
        return [pltpu.SemaphoreType.DMA((self.nsem,)), pltpu.SemaphoreType.DMA((self.nsem,))]


def _run_rider(name, rd):
    n_in, n_out = len(rd.ins), len(rd.out_shapes)

    def body(*refs):
        ins, outs, (send, recv) = refs[:n_in], refs[n_in:n_in + n_out], refs[n_in + n_out:]
        rd.start(ins, outs, send, recv)
        rd.finish(ins, outs, send, recv)

    return pl.pallas_call(body, name=name, in_specs=[_HBM] * n_in, out_specs=[_HBM] * n_out, out_shape=rd.out_shapes,
                          scratch_shapes=rd.sems())(*rd.ins)


def _gather_rider(ts):
    nt = len(ts)

    def half(t, hc):
        rh = ts[t].shape[0] // 2
        return pl.ds(pl.multiple_of(hc * rh, 16), rh)

    def rcopy(send, recv, t, k, src, dst, to):
        return pltpu.make_async_remote_copy(src_ref=src, dst_ref=dst, send_sem=send.at[6 * t + k], recv_sem=recv.at[6 * t + k],
                                            device_id=to, device_id_type=MESH)

    def first_hop(ins, outs, send, recv, t, r, px, py, c, me):
        return rcopy(send, recv, t, r, ins[t].at[half(t, c)], outs[t].at[me, half(t, c)], (px, py, c))

    def start(ins, outs, send, recv):
        x, y, c, rel = _place()
        for t in range(nt):
            for r, (px, py) in enumerate(rel):
                first_hop(ins, outs, send, recv, t, r, px, py, c, 2 * x + y).start()

    def finish(ins, outs, send, recv):
        x, y, c, rel = _place()
        sib = (x, y, 1 - c)
        passed = []
        for t in range(nt):
            for r, (px, py) in enumerate(rel):
                got = outs[t].at[2 * px + py, half(t, c)]
                rcopy(send, recv, t, r, got, got, (px, py, c)).wait_recv()
                fw = rcopy(send, recv, t, 3 + r, got, got, sib)
                fw.start()
                passed.append(fw)
        for t in range(nt):
            for r, (px, py) in enumerate(rel):
                got = outs[t].at[2 * px + py, half(t, 1 - c)]
                rcopy(send, recv, t, 3 + r, got, got, sib).wait_recv()
        for t in range(nt):
            for r, (px, py) in enumerate(rel):
                first_hop(ins, outs, send, recv, t, r, px, py, c, 2 * x + y).wait_send()
        for fw in passed:
            fw.wait_send()

    return _Rider(ts, [jax.ShapeDtypeStruct((4,) + tuple(t.shape), t.dtype) for t in ts], 6 * nt, start, finish)


def _scatter_rider(ps):
    nt = len(ps)

    def copy(ins, outs, send, recv, t, r, px, py, c):
        return pltpu.make_async_remote_copy(src_ref=ins[t].at[2 * px + py], dst_ref=outs[t].at[r], send_sem=send.at[3 * t + r],
                                            recv_sem=recv.at[3 * t + r], device_id=(px, py, c), device_id_type=MESH)

    def start(ins, outs, send, recv):
        x, y, c, rel = _place()
        for t in range(nt):
            for r, (px, py) in enumerate(rel):
                copy(ins, outs, send, recv, t, r, px, py, c).start()

    def finish(ins, outs, send, recv):
        x, y, c, rel = _place()
        for t in range(nt):
            for r, (px, py) in enumerate(rel):
                copy(ins, outs, send, recv, t, r, px, py, c).wait()

    return _Rider(ps, [jax.ShapeDtypeStruct((3,) + tuple(p.shape[1:]), p.dtype) for p in ps], 3 * nt, start, finish)


def _pair_swap(tag, ts):
    nt = len(ts)

    def body(*refs):
        ins, outs = refs[:nt], refs[nt:2 * nt]
        send, recv = refs[2 * nt:]
        x, y, c, _ = _place()
        cps = []
        for t in range(nt):
            rh = ts[t].shape[1] // 2
            src = ins[t].at[:, pl.ds(pl.multiple_of((1 - c) * rh, 16), rh), :]
            cp = pltpu.make_async_remote_copy(src_ref=src, dst_ref=outs[t], send_sem=send.at[t], recv_sem=recv.at[t],
                                              device_id=(x, y, 1 - c), device_id_type=MESH)
            cp.start()
            cps.append(cp)
        for cp in cps:
            cp.wait()

    return pl.pallas_call(
        body, name="pair_swap_" + tag, in_specs=[_HBM] * nt, out_specs=[_HBM] * nt,
        out_shape=[jax.ShapeDtypeStruct((4, t.shape[1] // 2, t.shape[2]), t.dtype) for t in ts],
        scratch_shapes=[pltpu.SemaphoreType.DMA((nt,)), pltpu.SemaphoreType.DMA((nt,))],
    )(*ts)


def _pair_exchange(gs):
    nt = len(gs)

    def body(*refs):
        ins, outs = refs[:nt], refs[nt:2 * nt]
        send, recv = refs[2 * nt:]
        x, y, c, _ = _place()
        cps = []
        for t in range(nt):
            cp = pltpu.make_async_remote_copy(src_ref=ins[t], dst_ref=outs[t], send_sem=send.at[t], recv_sem=recv.at[t],
                                              device_id=(x, y, 1 - c), device_id_type=MESH)
            cp.start()
            cps.append(cp)
        for cp in cps:
            cp.wait()

    return pl.pallas_call(
        body, name="pair_exchange", in_specs=[_HBM] * nt, out_specs=[_HBM] * nt,
        out_shape=[jax.ShapeDtypeStruct(tuple(g.shape), g.dtype) for g in gs],
        scratch_shapes=[pltpu.SemaphoreType.DMA((nt,)), pltpu.SemaphoreType.DMA((nt,))],
    )(*gs)


def _allgather8(v):
    m, n = v.shape

    def body(x_ref, out_ref, send, recv, lsem):
        x, y, c, rel = _place()
        me, sib = (x, y, c), (x, y, 1 - c)

        def blk(px, py, pc):
            return out_ref.at[4 * px + 2 * py + pc]

        def copy(k, block, to, src=None):
            return pltpu.make_async_remote_copy(src_ref=blk(*block) if src is None else src, dst_ref=blk(*block), send_sem=send.at[k],
                                                recv_sem=recv.at[k], device_id=to, device_id_type=MESH)

        mine = pltpu.make_async_copy(x_ref, blk(*me), lsem)
        mine.start()
        first = [copy(0, me, sib, src=x_ref)] + [copy(1 + r, me, (*ch, c), src=x_ref) for r, ch in enumerate(rel)]
        for cp in first:
            cp.start()
        passed = [copy(4 + r, (*ch, c), sib) for r, ch in enumerate(rel)]
        for r, ch in enumerate(rel):
            copy(1 + r, (*ch, c), me).wait_recv()
            passed[r].start()
        copy(0, sib, me).wait_recv()
        for r, ch in enumerate(rel):
            copy(4 + r, (*ch, 1 - c), me).wait_recv()
        for cp in first + passed:
            cp.wait_send()
        mine.wait()

    return pl.pallas_call(
        body, name="allgather8", in_specs=[pl.BlockSpec(memory_space=pltpu.VMEM)], out_specs=pl.BlockSpec(memory_space=pltpu.VMEM),
        out_shape=jax.ShapeDtypeStruct((8, m, n), v.dtype),
        scratch_shapes=[pltpu.SemaphoreType.DMA((7,)), pltpu.SemaphoreType.DMA((7,)), pltpu.SemaphoreType.DMA],
    )(v)


def _t5_bucket(dist):
    max_exact = REL_BUCKETS // 2
    d = jnp.maximum(dist, 1).astype(F32)
    large = max_exact + (jnp.log(d / max_exact) / math.log(REL_MAX_DIST / max_exact) * (REL_BUCKETS - max_exact)).astype(jnp.int32)
    large = jnp.minimum(large, REL_BUCKETS - 1)
    return jnp.where(dist < max_exact, dist, large)


def _bias_onehot():
    qi = jnp.arange(SWA_BLOCK)[:, None]
    kj = jnp.arange(SWA_BLOCK)[None, :]
    dist = jnp.concatenate([(qi + SWA_BLOCK - kj).reshape(-1), (qi - kj).reshape(-1)])
    bucket = _t5_bucket(jnp.maximum(dist, 0))
    return (bucket[None, :] == jnp.arange(REL_BUCKETS)[:, None]).astype(F32)


def _head_spread():
    lane = jnp.arange(LANES)[:, None]
    head = jnp.arange(GDN_W)[None, :] // GDN_HEAD_DIM
    return (lane == head).astype(F32), (lane == head + GDN_HEADS).astype(F32)


def _lane16(v8):
    return jnp.pad(v8.astype(F32), (GDN_HEADS, LANES - 2 * GDN_HEADS)).reshape(1, LANES)


def _stack_heads(t, nb):
    return t.reshape(nb, SWA_BLOCK, SWA_KV_HEADS, SWA_GRP, SWA_HEAD_DIM).transpose(2, 0, 3, 1, 4).reshape(
        SWA_KV_HEADS, nb * SWA_GRP * SWA_BLOCK, SWA_HEAD_DIM)


def _unstack_heads(t, nb):
    return t.reshape(SWA_KV_HEADS, nb, SWA_GRP, SWA_BLOCK, SWA_HEAD_DIM).transpose(1, 3, 0, 2, 4).reshape(nb * SWA_BLOCK, SWA_Q)


def _kv_heads(t):
    return t.reshape(t.shape[0], SWA_KV_HEADS, SWA_HEAD_DIM).transpose(1, 0, 2)


def _swa_specs(qs, ks, vs, bp, bc, sk, grad, gdt=(F32,)):
    T = ks.shape[1]
    qr = SWA_GRP * SWA_BLOCK
    g = lambda a: tuple(a.shape) if grad else None
    nk = SWA_KV_HEADS
    m3 = lambda j, n: (0, n, 0)
    h3 = lambda j, n: (0, jnp.maximum(n - 1, 0), 0)
    p3 = lambda j: (0, 0, 0)
    rows = [Row(qs, (nk, qr, SWA_HEAD_DIM), m3, gshape=g(qs), gmap=m3, gdt=gdt),
            Row(ks, (nk, SWA_BLOCK, SWA_HEAD_DIM), m3, (nk, SWA_BLOCK, SWA_HEAD_DIM), h3, g(ks), m3, gdt),
            Row(vs, (nk, SWA_BLOCK, SWA_HEAD_DIM), m3, (nk, SWA_BLOCK, SWA_HEAD_DIM), h3, g(vs), m3, gdt)]
    pars = [Par(bp, (nk, qr, SWA_BLOCK), p3, g(bp), p3), Par(bc, (nk, qr, SWA_BLOCK), p3, g(bc), p3),
            Par(sk, (nk, qr, 1), p3, g(sk), p3)]
    return rows, pars, T // SWA_BLOCK


class _LocalWeights:
    def __init__(self, W):
        self.W = W

    def w1(self):
        return self.W

    def rider_a(self):
        return None

    def w2(self, got):
        return self.W

    def rider_b(self):
        return None

    def w3(self, got):
        return self.W

    def rider_g(self, G):
        return None

    def g_done(self, got):
        pass

    def rider_up(self, G):
        return None

    def up_done(self, got):
        pass

    def rider_last(self, G):
        return None

    def last_done(self, got):
        pass


def _fwd_bwd(x, mem, tgt, src):
    W = dict(src.w1())
    T = x.shape[0]
    nb = T // SWA_BLOCK
    tb = min(256, T)
    tbl = min(512, T)
    fwd = lambda f: (lambda *a: (f(*a), []))
    full = lambda cols, dt, t, cw: Out((T, cols), dt, (t, cw), lambda j, n: (n, j))

    xb = x.astype(_CDT)
    ra = src.rider_a()
    proj = _mm("proj", xb, W["in_p"], "nn", rider=ra)
    proj, got = proj if ra is not None else (proj, None)
    W.update(src.w2(got))

    onehot_t = _bias_onehot()
    bias_flat = _mm("swa_bias", W["rel_bias"].T, onehot_t, "nn", hi=True)
    half = SWA_BLOCK * SWA_BLOCK
    bp = bias_flat[:, :half].reshape(SWA_KV_HEADS, SWA_GRP * SWA_BLOCK, SWA_BLOCK)
    bc = bias_flat[:, half:].reshape(SWA_KV_HEADS, SWA_GRP * SWA_BLOCK, SWA_BLOCK)
    sk = jnp.broadcast_to(W["swa_sinks"].reshape(SWA_KV_HEADS, SWA_GRP, 1, 1), (SWA_KV_HEADS, SWA_GRP, SWA_BLOCK, 1)).reshape(
        SWA_KV_HEADS, SWA_GRP * SWA_BLOCK, 1)
    qs = _stack_heads(proj[:, P_SQ:P_SQ + SWA_Q], nb)
    ks = _kv_heads(proj[:, P_SK:P_SK + SWA_KV])
    vs = _kv_heads(proj[:, P_SV:P_SV + SWA_KV])
    rows, pars, nblk = _swa_specs(qs, ks, vs, bp, bc, sk, False)
    o_s, = _rowmap("swa_fwd", fwd(_swa_fn), 1, nblk, rows, pars,
                   [Out(tuple(qs.shape), F32, (SWA_KV_HEADS, SWA_GRP * SWA_BLOCK, SWA_HEAD_DIM), lambda j, n: (0, n, 0))])
    o_swa = _unstack_heads(o_s, nb).astype(_CDT)

    ncq = 3 * GDN_W // LANES
    tbp = min(1024, T)
    pre_rows = lambda grad: [_rowspec(proj, tbp, LANES, P_GQKV // LANES, halo=SUBLANES, grad=grad, ncol=ncq, gdt=(_CDT,))]
    pre_pars = lambda grad: [_parspec(W["gdn_conv_w"], LANES, 0, grad=grad, ncol=ncq)]
    qkv_n, = _rowmap("gdn_pre_fwd", fwd(_gdn_pre_fn), ncq, T // tbp, pre_rows(False), pre_pars(False),
                     [full(3 * GDN_W, F32, tbp, LANES)])
    eb, eg = _head_spread()
    alog_row, dtb_row = _lane16(W["gdn_a_log"]), _lane16(W["gdn_dt_bias"])
    gate_rows = lambda grad: [_rowspec(proj, tbl, LANES, P_BA // LANES, cstep=0, grad=grad, gdt=(_CDT,))]
    gate_pars = lambda grad: [_parspec(alog_row, grad=grad), _parspec(dtb_row, grad=grad), _parspec(eb), _parspec(eg)]
    bx, gx = _rowmap("gdn_gate_fwd", fwd(_gdn_gate_fn), 1, T // tbl, gate_rows(False), gate_pars(False),
                     [full(GDN_W, F32, tbl, GDN_W), full(GDN_W, F32, tbl, GDN_W)])
    nw = W["gdn_norm_w"].reshape(1, GDN_HEAD_DIM)
    o_gdn, states, got = _gdn_chunks_fwd(qkv_n, bx, gx, proj, nw, rider=src.rider_b())
    W.update(src.w3(got))

    ys = _mm("y_swa", o_swa, W["br_swa"], "nn")
    yg = _mm("y_gdn", o_gdn, W["br_gdn"], "nn")
    cwm = 512
    mix_rows = lambda grad: [_rowspec(proj, tb, cwm, P_GS // cwm, grad=grad, ncol=D_MODEL // cwm, gdt=(_CDT,)),
                             _rowspec(proj, tb, cwm, P_GG // cwm, grad=grad, ncol=D_MODEL // cwm, gdt=(_CDT,)),
                             _rowspec(ys, tb, cwm, 0, grad=grad, ncol=D_MODEL // cwm, gdt=(_CDT,)),
                             _rowspec(yg, tb, cwm, 0, grad=grad, ncol=D_MODEL // cwm, gdt=(_CDT,))]
    mixed, = _rowmap("mix_fwd", fwd(_mix_fn), D_MODEL // cwm, T // tb, mix_rows(False), [], [full(D_MODEL, _CDT, tb, cwm)])
    r1 = _mm("r1", mixed, W["mix_o"], "nn", add=x, add_scale=ALPHA)

    def ln_fwd(name, r, g, b):
        return _rowmap(name, _ln_fwd_fn, 1, T // tb, [_rowspec(r, tb, D_MODEL, 0)], [_parspec(g), _parspec(b)],
                       [full(D_MODEL, F32, tb, D_MODEL), full(D_MODEL, _CDT, tb, D_MODEL)])

    def ln_bwd(name, r, g, b, ct):
        return _rowmap_bwd(name, _ln_fn, 1, T // tb, [_rowspec(r, tb, D_MODEL, 0, grad=True, gdt=(F32, _CDT))],
                           [_parspec(g, grad=True), _parspec(b, grad=True)], [_rowspec(ct, tb, D_MODEL, 0)])

    g1, b1 = W["ln1_g"].reshape(1, -1), W["ln1_b"].reshape(1, -1)
    g2, b2 = W["ln2_g"].reshape(1, -1), W["ln2_b"].reshape(1, -1)
    g3, b3 = W["ln3_g"].reshape(1, -1), W["ln3_b"].reshape(1, -1)
    x1, x1b = ln_fwd("ln1_fwd", r1, g1, b1)

    qm = _mm("mem_q", x1b, W["mem_q"], "nn")
    kvm = _mm("mem_kv", mem, W["mem_kv"], "nn")
    ma_rows = lambda grad: [_rowspec(qm, tbl, MEM_HEAD_DIM, 0, grad=grad, ncol=MEM_HEADS, gdt=(_CDT,))]
    ma_pars = lambda grad: [_parspec(kvm, MEM_HEAD_DIM, 0, grad=grad, ncol=MEM_HEADS),
                            _parspec(kvm, MEM_HEAD_DIM, MEM_HEADS, grad=grad, ncol=MEM_HEADS)]
    om, = _rowmap("memattn_fwd", fwd(_memattn_fn), MEM_HEADS, T // tbl, ma_rows(False), ma_pars(False),
                  [full(MEM_W, _CDT, tbl, MEM_HEAD_DIM)])
    r2 = _mm("r2", om, W["mem_o"], "nn", add=x1, add_scale=ALPHA)
    x2, x2b = ln_fwd("ln2_fwd", r2, g2, b2)

    hcat = _mm("ffn_up", x2b, W["up_p"], "nn")
    cwf = 512
    ncf = D_FF_PAD // cwf
    cw_p, cb_p = W["ffn_conv_w_p"], W["ffn_conv_b_p"]
    tbf = min(512, T)
    ffn_rows = lambda grad: [_rowspec(hcat, tbf, cwf, 0, halo=SUBLANES, grad=grad, ncol=ncf, gdt=(_CDT,)),
                             _rowspec(hcat, tbf, cwf, ncf, halo=SUBLANES, grad=grad, ncol=ncf, gdt=(_CDT,))]
    ffn_pars = lambda grad: [_parspec(cw_p, cwf, 0, grad=grad, ncol=ncf), _parspec(cw_p, cwf, ncf, grad=grad, ncol=ncf),
                             _parspec(cb_p, cwf, 0, grad=grad, ncol=ncf), _parspec(cb_p, cwf, ncf, grad=grad, ncol=ncf)]
    act, = _rowmap("ffn_act_fwd", fwd(_ffn_act_fn), ncf, T // tbf, ffn_rows(False), ffn_pars(False), [full(D_FF_PAD, _CDT, tbf, cwf)])
    r3 = _mm("r3", act, W["down_p"], "nn", add=x2, add_scale=ALPHA)
    dr3, dr3b, lacc, dg3, db3 = _rowmap("ln3_loss", _loss_fn, 1, T // tb, [_rowspec(r3, tb, D_MODEL, 0), _rowspec(tgt, tb, D_MODEL, 0)],
                                        [_parspec(g3), _parspec(b3)], [full(D_MODEL, F32, tb, D_MODEL), full(D_MODEL, _CDT, tb, D_MODEL)],
                                  accs=[(SUBLANES, LANES), (1, D_MODEL), (1, D_MODEL)])
    loss = lacc[0, 0]

    G = {}
    G["down_p"] = _mm("dw_down", act, dr3b, "tn", out_dtype=_GDT)
    dact = _mm("d_act", dr3b, W["down_p"], "nt")
    dhg, dhu, dcwg, dcwu, dcbg, dcbu = _rowmap_bwd("ffn_act_bwd", _ffn_act_fn, ncf, T // tbf, ffn_rows(True), ffn_pars(True),
                                                   [_rowspec(dact, tbf, cwf, 0)])
    dx2 = _mm("dx2_gate", dhg, W["up_p"], "nt", add=dr3, add_scale=ALPHA, b_k0=0)
    dx2 = _mm("dx2_up", dhu, W["up_p"], "nt", add=dx2, b_k0=D_FF_PAD)
    G["up_p"] = jnp.concatenate([_mm("dw_gate", x2b, dhg, "tn", out_dtype=_GDT), _mm("dw_up", x2b, dhu, "tn", out_dtype=_GDT)], axis=1)
    G["ffn_conv_w"] = jnp.concatenate([dcwg[:, :D_FF], dcwu[:, :D_FF]], axis=1)
    G["ffn_conv_b"] = jnp.concatenate([dcbg[0, :D_FF], dcbu[0, :D_FF]])
    G["ln3_g"], G["ln3_b"] = dg3[0], db3[0]

    dr2, dr2b, dg2, db2 = ln_bwd("ln2_bwd", r2, g2, b2, dx2)
    G["ln2_g"], G["ln2_b"] = dg2[0], db2[0]
    G["mem_o"] = _mm("dw_mem_o", om, dr2b, "tn", out_dtype=_GDT)
    dom = _mm("d_om", dr2b, W["mem_o"], "nt", out_dtype=_CDT)
    dqm, dkm, dvm = _rowmap_bwd("memattn_bwd", _memattn_fn, MEM_HEADS, T // tbl, ma_rows(True), ma_pars(True),
                                [_rowspec(dom, tbl, MEM_HEAD_DIM, 0)])
    G["mem_kv"] = _mm("dw_mem_kv", mem.astype(_CDT), jnp.concatenate([dkm, dvm], axis=1).astype(_CDT), "tn", out_dtype=_GDT)
    G["mem_q"] = _mm("dw_mem_q", x1b, dqm, "tn", out_dtype=_GDT)
    dx1 = _mm("dx1", dqm, W["mem_q"], "nt", add=dr2, add_scale=ALPHA)

    dr1, dr1b, dg1, db1 = ln_bwd("ln1_bwd", r1, g1, b1, dx1)
    G["ln1_g"], G["ln1_b"] = dg1[0], db1[0]
    G["mix_o"] = _mm("dw_mix_o", mixed, dr1b, "tn", out_dtype=_GDT)
    dmixed = _mm("d_mixed", dr1b, W["mix_o"], "nt")
    dgs, dgg, dys, dyg = _rowmap_bwd("mix_bwd", _mix_fn, D_MODEL // cwm, T // tb, mix_rows(True), [], [_rowspec(dmixed, tb, cwm, 0)])
    G["br_swa"] = _mm("dw_br_swa", o_swa, dys, "tn", out_dtype=_GDT)
    G["br_gdn"] = _mm("dw_br_gdn", o_gdn, dyg, "tn", out_dtype=_GDT)
    do_swa = _mm("d_o_swa", dys, W["br_swa"], "nt", out_dtype=_CDT)
    do_gdn = _mm("d_o_gdn", dyg, W["br_gdn"], "nt")

    rows, pars, nblk = _swa_specs(qs, ks, vs, bp, bc, sk, True, (_CDT,))
    m3 = lambda j, n: (0, n, 0)
    dqs, dks, dvs, dbp, dbc, dsk = _rowmap_bwd("swa_bwd", _swa_fn, 1, nblk, rows, pars,
                                               [Row(_stack_heads(do_swa, nb), (SWA_KV_HEADS, SWA_GRP * SWA_BLOCK, SWA_HEAD_DIM), m3)])
    d_swa = jnp.concatenate([_unstack_heads(dqs, nb), dks.transpose(1, 0, 2).reshape(T, SWA_KV),
                             dvs.transpose(1, 0, 2).reshape(T, SWA_KV)], axis=1)
    dbias = jnp.concatenate([dbp.reshape(SWA_HEADS, half), dbc.reshape(SWA_HEADS, half)], axis=1)
    G["rel_bias"] = _mm("d_rel_bias", dbias, onehot_t.T, "nn", hi=True).T
    G["swa_sinks"] = _mm("d_sinks", dsk.reshape(SWA_HEADS, SWA_BLOCK), jnp.ones((SWA_BLOCK, LANES), F32), "nn", hi=True)[:, 0]

    dqkv_n, dbx, dgx, dz, dnw, got = _gdn_chunks_bwd(qkv_n, bx, gx, proj, nw, states, do_gdn, rider=src.rider_g(G))
    src.g_done(got)
    G["gdn_norm_w"] = dnw[0]
    dgba, dalog, ddtb = _rowmap_bwd("gdn_gate_bwd", _gdn_gate_fn, 1, T // tbl, gate_rows(True), gate_pars(True),
                                    [_rowspec(dbx, tbl, GDN_W, 0), _rowspec(dgx, tbl, GDN_W, 0)])
    G["gdn_a_log"], G["gdn_dt_bias"] = dalog[0, GDN_HEADS:2 * GDN_HEADS], ddtb[0, GDN_HEADS:2 * GDN_HEADS]
    dgqkv, dcw_gdn = _rowmap_bwd("gdn_pre_bwd", _gdn_pre_fn, ncq, T // tbp, pre_rows(True), pre_pars(True),
                                 [_rowspec(dqkv_n, tbp, LANES, 0)])
    G["gdn_conv_w"] = dcw_gdn

    dproj = jnp.concatenate([dgs, dgg, dgqkv, dz, d_swa, dgba, jnp.zeros((T, P_END - P_USED), _CDT)], axis=1)
    ru = src.rider_up(G)
    G["in_p"] = _mm("dw_in", xb, dproj, "tn", out_dtype=_GDT, rider=ru)
    if ru is not None:
        G["in_p"], got = G["in_p"]
        src.up_done(got)
    rl = src.rider_last(G)
    dx = _mm("dx", dproj, W["in_p"], "nt", add=dr1, add_scale=ALPHA, rider=rl)
    if rl is not None:
        dx, got = dx
        src.last_done(got)
    return loss, dx, G


W_NAMES = ["w_in", "rel_bias", "swa_sinks", "gdn_conv_w", "gdn_a_log", "gdn_dt_bias", "gdn_norm_w", "w_br_swa", "w_br_gdn",
           "w_mix_o", "ln1_g", "ln1_b", "w_mem_q", "w_mem_kv", "w_mem_o", "ln2_g", "ln2_b", "w_up", "ffn_conv_w", "ffn_conv_b",
           "w_down", "ln3_g", "ln3_b"]
BIG = ["w_in", "w_br_swa", "w_br_gdn", "w_mix_o", "w_mem_q", "w_mem_kv", "w_mem_o", "w_up", "w_down"]
SMALL = [n for n in W_NAMES if n not in BIG]
COL_SHARDED = ["w_in", "w_br_swa", "w_br_gdn", "w_mem_o", "w_up"]


def _pack(arrs):
    rows = []
    for a in arrs:
        f = a.reshape(-1).astype(F32)
        rows.append(jnp.pad(f, (0, (-f.shape[0]) % LANES)).reshape(-1, LANES))
    n = sum(r.shape[0] for r in rows)
    if n % 16:
        rows.append(jnp.zeros((16 - n % 16, LANES), F32))
    return jnp.concatenate(rows, axis=0)


def _unpack(p, shapes):
    out, off = [], 0
    for s in shapes:
        n = int(np.prod(s)) if len(s) else 1
        r = -(-n // LANES)
        out.append(p[off:off + r].reshape(-1)[:n].reshape(s))
        off += r
    return out


def _merge_shards(d):
    cat = lambda names: jnp.concatenate([d[n] for n in names], axis=-2)
    return [d.get("w_in"), d["w_up"], cat(["w_br_swa", "w_br_gdn", "w_mem_q", "w_mem_o"]), cat(["w_mix_o", "w_down"]), d["w_mem_kv"]]


def _split_shards(ts):
    a, b, c, dd, e = ts
    return {"w_in": a, "w_up": b, "w_br_swa": c[..., 0:1024, :], "w_br_gdn": c[..., 1024:2048, :], "w_mem_q": c[..., 2048:2560, :],
            "w_mem_o": c[..., 2560:3072, :], "w_mix_o": dd[..., 0:512, :], "w_down": dd[..., 512:, :], "w_mem_kv": e}


def _to_full(name, t):
    if name in COL_SHARDED:
        return _cols_from_chips(t, [(0, 4 * t.shape[2])])
    return t.reshape(4 * t.shape[1], t.shape[2])


def _to_chips(name, t):
    if name in COL_SHARDED:
        return _chips_from_cols(t, [(0, t.shape[1])], t.shape[1] // 4)
    return t.reshape(4, t.shape[0] // 4, t.shape[1])


def _cols_from_chips(g, segs):
    C, parts = g.shape[2], []
    for s in segs:
        if isinstance(s, int):
            parts.append(jnp.zeros((g.shape[1], s), g.dtype))
            continue
        lo, hi = s
        while lo < hi:
            k = lo // C
            e = min(hi, (k + 1) * C)
            parts.append(g[k][:, lo - k * C:e - k * C])
            lo = e
    return jnp.concatenate(parts, axis=1)


def _chips_from_cols(p, segs, C):
    out = []
    for k in range(4):
        lo, hi, parts, o = k * C, (k + 1) * C, [], 0
        for plo, w in segs:
            a, b = max(lo, o), min(hi, o + w)
            if a < b:
                parts.append(p[:, plo + a - o:plo + b - o])
            o += w
        out.append(jnp.concatenate(parts, axis=1))
    return jnp.stack(out)


_IN_OFF = np.cumsum((0,) + IN_WIDTHS)
_IN_SEGS = [(P_SQ, SWA_Q), (P_SK, SWA_KV), (P_SV, SWA_KV), (P_GQKV, 3 * GDN_W), (P_GZ, GDN_W), (P_BA, 2 * GDN_HEADS),
            (P_GS, D_MODEL), (P_GG, D_MODEL)]
_IN_PADDED = [(int(_IN_OFF[i]), int(_IN_OFF[k])) for i, k in ((9, 10), (10, 11), (3, 6), (6, 7), (0, 1), (1, 2), (2, 3), (7, 9))] + [
    P_END - P_BA - 2 * GDN_HEADS]
_UP_SEGS = [(0, D_FF), (D_FF_PAD, D_FF)]
_UP_PADDED = [(0, D_FF), D_FF_PAD - D_FF, (D_FF, 2 * D_FF), D_FF_PAD - D_FF]


def _in_to_padded(w):
    o = _IN_OFF
    cut = lambda i, k: w[:, o[i]:o[k]]
    return jnp.concatenate([cut(9, 10), cut(10, 11), cut(3, 6), cut(6, 7), cut(0, 1), cut(1, 2), cut(2, 3), cut(7, 9),
                            jnp.zeros((w.shape[0], P_END - P_BA - 2 * GDN_HEADS), w.dtype)], axis=1)


def _in_from_padded(p):
    return jnp.concatenate([p[:, P_SQ:P_SQ + SWA_Q], p[:, P_SK:P_SK + SWA_KV], p[:, P_SV:P_SV + SWA_KV], p[:, P_GQKV:P_GQKV + 3 * GDN_W],
                            p[:, P_GZ:P_GZ + GDN_W], p[:, P_BA:P_BA + 2 * GDN_HEADS], p[:, P_GS:P_GS + D_MODEL], p[:, P_GG:P_GG + D_MODEL]],
                           axis=1)


def _ff_pad(t, axis):
    g, u = jnp.split(t, 2, axis=axis)
    pad = [(0, 0)] * t.ndim
    pad[axis] = (0, D_FF_PAD - D_FF)
    return jnp.concatenate([jnp.pad(g, pad), jnp.pad(u, pad)], axis=axis)


def _ff_unpad(t, axis):
    g, u = jnp.split(t, 2, axis=axis)
    return jnp.concatenate([lax.slice_in_dim(g, 0, D_FF, axis=axis), lax.slice_in_dim(u, 0, D_FF, axis=axis)], axis=axis)


def _assemble_weights(full, small):
    W = dict(small)
    W["in_p"] = _in_to_padded(full["w_in"])
    W["up_p"] = _ff_pad(full["w_up"], 1)
    W["down_p"] = jnp.pad(full["w_down"], ((0, D_FF_PAD - D_FF), (0, 0)))
    W["br_swa"], W["br_gdn"], W["mix_o"] = full["w_br_swa"], full["w_br_gdn"], full["w_mix_o"]
    W["mem_q"], W["mem_kv"], W["mem_o"] = full["w_mem_q"], full["w_mem_kv"], full["w_mem_o"]
    W["ffn_conv_w_p"] = _ff_pad(small["ffn_conv_w"], 1)
    W["ffn_conv_b_p"] = _ff_pad(small["ffn_conv_b"].reshape(1, -1), 1)
    return W


def _full_grads(G):
    out = {"w_in": _in_from_padded(G["in_p"])} if "in_p" in G else {}
    out.update({"w_up": _ff_unpad(G["up_p"], 1), "w_down": G["down_p"][:D_FF], "w_br_swa": G["br_swa"], "w_br_gdn": G["br_gdn"],
                "w_mix_o": G["mix_o"], "w_mem_q": G["mem_q"], "w_mem_kv": G["mem_kv"], "w_mem_o": G["mem_o"]})
    return out


def kernel(x, mem, w_in, rel_bias, swa_sinks, gdn_conv_w, gdn_a_log, gdn_dt_bias, gdn_norm_w, w_br_swa, w_br_gdn, w_mix_o, ln1_g, ln1_b, w_mem_q, w_mem_kv, w_mem_o, ln2_g, ln2_b, w_up, ffn_conv_w, ffn_conv_b, w_down, ln3_g, ln3_b, loss_target, m_w_in, m_rel_bias, m_swa_sinks, m_gdn_conv_w, m_gdn_a_log, m_gdn_dt_bias, m_gdn_norm_w, m_w_br_swa, m_w_br_gdn, m_w_mix_o, m_ln1_g, m_ln1_b, m_w_mem_q, m_w_mem_kv, m_w_mem_o, m_ln2_g, m_ln2_b, m_w_up, m_ffn_conv_w, m_ffn_conv_b, m_w_down, m_ln3_g, m_ln3_b, v_w_in, v_rel_bias, v_swa_sinks, v_gdn_conv_w, v_gdn_a_log, v_gdn_dt_bias, v_gdn_norm_w, v_w_br_swa, v_w_br_gdn, v_w_mix_o, v_ln1_g, v_ln1_b, v_w_mem_q, v_w_mem_kv, v_w_mem_o, v_ln2_g, v_ln2_b, v_w_up, v_ffn_conv_w, v_ffn_conv_b, v_w_down, v_ln3_g, v_ln3_b):
    a = dict(locals())
    w = {n: a[n] for n in W_NAMES}
    m = {n: a["m_" + n] for n in W_NAMES}
    v = {n: a["v_" + n] for n in W_NAMES}
    chip = 2 * lax.axis_index("x") + lax.axis_index("y")
    core = lax.axis_index("c")
    sq = lambda t: t.reshape(t.shape[1:]) if (t.ndim > 1 and t.shape[0] == 1 and t is not rel_bias) else t

    sh_a, sh_b, sh_c, sh_d, sh_e = _merge_shards({n: sq(w[n]).astype(_CDT) for n in BIG})
    fcw_sh, gcw_sh = sq(ffn_conv_w).shape, sq(gdn_conv_w).shape
    slot = lax.broadcasted_iota(jnp.int32, (4, 1, 1), 0)

    def with_own(got, mine):
        return [jnp.where(slot == chip, t[None], g) for g, t in zip(got, mine)]

    def reduce_start(tag, gch):
        pair = []
        for t, (mine, got) in enumerate(zip(gch, _pair_swap(tag, gch))):
            rh = mine.shape[1] // 2
            mine_h = lax.dynamic_slice_in_dim(mine, core * rh, rh, axis=1)
            pair.append(_addn(f"pair_sum_{tag}{t}", [mine_h.reshape(4 * rh, -1), got.reshape(4 * rh, -1)], _GDT).reshape(4, rh, -1))
        return pair

    def reduce_end(tag, pair, others):
        halves = []
        for t, (p, o) in enumerate(zip(pair, others)):
            own = lax.dynamic_index_in_dim(p, chip, 0, keepdims=False)
            halves.append(_addn(f"chip_sum_{tag}{t}", [own, (o, 0), (o, 1), (o, 2)]))
        return halves

    class MeshWeights:
        def w1(self):
            mine = [sh_a, _pack([sq(ffn_conv_w), sq(gdn_conv_w)])]
            got_a, got_f = with_own(_run_rider("gather_first", _gather_rider(mine)), mine)
            conv = [_unpack(got_f[k], [fcw_sh, gcw_sh]) for k in range(4)]
            W = {n: sq(w[n]) for n in SMALL}
            W["ffn_conv_w"] = jnp.concatenate([cv[0] for cv in conv], axis=1)
            W["gdn_conv_w"] = jnp.concatenate([cv[1] for cv in conv], axis=1)
            W["ffn_conv_w_p"] = _ff_pad(W["ffn_conv_w"], 1)
            W["ffn_conv_b_p"] = _ff_pad(W["ffn_conv_b"].reshape(1, -1), 1)
            W["in_p"] = _cols_from_chips(got_a, _IN_PADDED)
            return W

        def rider_a(self):
            return _gather_rider([sh_c, sh_d, sh_e])

        def w2(self, got):
            c, d, e = with_own(got, [sh_c, sh_d, sh_e])
            f = {n: _to_full(n, t) for n, t in _split_shards([None, None, c, d, e]).items() if t is not None}
            return {"br_swa": f["w_br_swa"], "br_gdn": f["w_br_gdn"], "mix_o": f["w_mix_o"], "mem_q": f["w_mem_q"], "mem_kv": f["w_mem_kv"],
                    "mem_o": f["w_mem_o"], "down_p": jnp.pad(f["w_down"], ((0, D_FF_PAD - D_FF), (0, 0)))}

        def rider_b(self):
            return _gather_rider([sh_b])

        def w3(self, got):
            b, = with_own(got, [sh_b])
            return {"up_p": _cols_from_chips(b, _UP_PADDED)}

        def rider_g(self, G):
            gf = _full_grads(G)
            gch = {n: _to_chips(n, gf[n]) for n in BIG if n not in ("w_in", "w_up")}
            gch["w_up"] = None
            self.pair = reduce_start("rest", _merge_shards(gch)[2:])
            return _scatter_rider(self.pair)

        def g_done(self, got):
            self.halves = reduce_end("rest", self.pair, got)

        def rider_up(self, G):
            self.pair_up = reduce_start("up", [_chips_from_cols(G["up_p"], _UP_SEGS, 2 * D_FF // 4)])
            return _scatter_rider(self.pair_up)

        def up_done(self, got):
            self.halves = reduce_end("up", self.pair_up, got) + self.halves

        def rider_last(self, G):
            self.pair_in = reduce_start("in", [_chips_from_cols(G["in_p"], _IN_SEGS, sum(IN_WIDTHS) // 4)])
            return _scatter_rider(self.pair_in)

        def last_done(self, got):
            self.halves = reduce_end("in", self.pair_in, got) + self.halves

    src = MeshWeights()
    loss, dx, G = _fwd_bwd(x[0], mem[0], loss_target[0], src)

    small_names = SMALL
    small_shapes = [()] + [tuple(G[n].shape) for n in small_names]
    packed = _pack([loss] + [G[n] for n in small_names])
    allp = _allgather8(packed)
    tot = _addn("small_sum", [(allp, k) for k in range(8)])
    parts = _unpack(tot, small_shapes)
    loss_tot, gsmall = parts[0], dict(zip(small_names, parts[1:]))
    gsmall["ffn_conv_w"] = lax.dynamic_slice_in_dim(gsmall["ffn_conv_w"], chip * fcw_sh[1], fcw_sh[1], axis=1)
    gsmall["gdn_conv_w"] = lax.dynamic_slice_in_dim(gsmall["gdn_conv_w"], chip * gcw_sh[1], gcw_sh[1], axis=1)

    both = []
    for h, o in zip(src.halves, _pair_exchange(src.halves)):
        both.append(jnp.concatenate([jnp.where(core == 0, h, o), jnp.where(core == 0, o, h)], axis=0))
    gbig = _split_shards(both)

    outs = {}
    for n in BIG:
        if w[n].shape[-1] % LANES:
            cols_out = lambda t: jnp.transpose(t, (2, 0, 1))
            g_ = jnp.transpose(gbig[n])[:, None, :]
            d_, m_, v_ = _adamw("adamw_" + n, cols_out(w[n]), g_, cols_out(m[n]), cols_out(v[n]))
            outs[n] = tuple(jnp.transpose(t, (1, 2, 0)) for t in (g_, d_, m_, v_))
            continue
        d_, m_, v_ = _adamw("adamw_" + n, sq(w[n]), gbig[n], sq(m[n]), sq(v[n]))
        outs[n] = (gbig[n], d_, m_, v_)
    for n in SMALL:
        two_d = (-1, w[n].shape[-1])
        g_ = gsmall[n].reshape(two_d)
        d_, m_, v_ = _adamw("adamw_" + n, w[n].reshape(two_d), g_, m[n].reshape(two_d), v[n].reshape(two_d))
        outs[n] = (g_, d_, m_, v_)

    res = [loss_tot.reshape(()), dx.reshape(x.shape)]
    for k in range(4):
        res += [outs[n][k].reshape(w[n].shape) for n in W_NAMES]
    return tuple(res)
```

```python
import functools
import math

import jax
import jax.numpy as jnp
import numpy as np
from jax import lax
from jax.experimental import pallas as pl
from jax.experimental.pallas import tpu as pltpu

F32 = jnp.float32
BF16 = jnp.bfloat16
_CDT = BF16
_GDT = BF16

D_MODEL = 2048
SWA_HEADS, SWA_KV_HEADS, SWA_HEAD_DIM, SWA_BLOCK = 16, 2, 64, 128
SWA_GRP = SWA_HEADS // SWA_KV_HEADS
REL_BUCKETS, REL_MAX_DIST = 32, 128
GDN_HEADS, GDN_HEAD_DIM, GDN_CONV, GDN_CHUNK = 8, 128, 4, 64
MEM_HEADS, MEM_HEAD_DIM = 4, 128
D_FF, D_FF_PAD, FFN_CONV = 5504, 5632, 3
SWA_Q, SWA_KV, GDN_W, MEM_W = 1024, 128, 1024, 512
IN_WIDTHS = (SWA_Q, SWA_KV, SWA_KV, GDN_W, GDN_W, GDN_W, GDN_W, GDN_HEADS, GDN_HEADS, D_MODEL, D_MODEL)
NORM_EPS = 1e-5
ALPHA = 2.0 ** 0.25
NEG_INF = -1e30
ADAM_LR, ADAM_B1, ADAM_B2, ADAM_EPS, ADAM_WD, ADAM_STEP = 0.001, 0.9, 0.999, 1e-08, 0.01, 10
LANES, SUBLANES = 128, 8
VMEM_LIMIT = 56 * 1024 * 1024

P_GS, P_GG, P_GQKV, P_GZ, P_SQ, P_SK, P_SV, P_BA, P_USED, P_END = 0, 2048, 4096, 7168, 8192, 9216, 9344, 9472, 9600, 9728


def _tile(dim, pref, align=LANES):
    if dim <= pref:
        return dim
    t = (pref // align) * align
    while t >= align:
        if dim % t == 0:
            return t
        t -= align
    return dim


_DIMS = {"nn": (((1,), (0,)), ((), ())), "nt": (((1,), (1,)), ((), ())), "tn": (((0,), (0,)), ((), ()))}
_BDIMS = {"nn": (((2,), (1,)), ((0,), (0,))), "nt": (((2,), (2,)), ((0,), (0,))), "tn": (((1,), (1,)), ((0,), (0,)))}


def _raw_dot(a, b, form, hi):
    dims = (_BDIMS if a.ndim == 3 else _DIMS)[form]
    if hi == "x3":
        a, b = a.astype(F32), b.astype(F32)
        ah, bh = a.astype(BF16), b.astype(BF16)
        al, bl = (a - ah.astype(F32)).astype(BF16), (b - bh.astype(F32)).astype(BF16)
        d = lambda p, q: lax.dot_general(p, q, dims, preferred_element_type=F32)
        if form == "tn":
            return d(ah, bh) + (d(ah, bl) + d(al, bh))
        m = a.shape[-2]
        both = d(jnp.concatenate([ah, al], axis=-2), bh)
        return both[..., :m, :] + (d(ah, bl) + both[..., m:, :])
    if hi:
        return lax.dot_general(a.astype(F32), b.astype(F32), dims, precision=lax.Precision.HIGHEST, preferred_element_type=F32)
    return lax.dot_general(a.astype(_CDT), b.astype(_CDT), dims, preferred_element_type=F32)


@functools.partial(jax.custom_vjp, nondiff_argnums=(2, 3))
def _dot(a, b, form, hi=False):
    return _raw_dot(a, b, form, hi)


def _dot_fwd(a, b, form, hi):
    return _raw_dot(a, b, form, hi), (a, b)


def _dot_bwd(form, hi, res, g):
    a, b = res
    if form == "nn":
        da, db = _raw_dot(g, b, "nt", hi), _raw_dot(a, g, "tn", hi)
    elif form == "nt":
        da, db = _raw_dot(g, b, "nn", hi), _raw_dot(g, a, "tn", hi)
    else:
        da, db = _raw_dot(b, g, "nt", hi), _raw_dot(a, g, "nn", hi)
    return da.astype(a.dtype), db.astype(b.dtype)


_dot.defvjp(_dot_fwd, _dot_bwd)


@functools.partial(jax.custom_vjp, nondiff_argnums=(2,))
def _shift_halo(prev, cur, d):
    assert prev.shape[0] == SUBLANES
    return pltpu.roll(jnp.concatenate([prev, cur], axis=0), d, 0)[SUBLANES:]


def _shift_halo_fwd(prev, cur, d):
    return _shift_halo(prev, cur, d), None


def _shift_halo_bwd(d, _, g):
    nh = SUBLANES
    ext = jnp.concatenate([jnp.zeros((nh, g.shape[1]), g.dtype), g], axis=0)
    r = pltpu.roll(ext, ext.shape[0] - d, 0)
    return r[:nh], r[nh:]


_shift_halo.defvjp(_shift_halo_fwd, _shift_halo_bwd)


@jax.custom_vjp
def _recip(x):
    return 1.0 / x


def _recip_fwd(x):
    r = 1.0 / x
    return r, r


def _recip_bwd(r, g):
    return (-g * r * r,)


_recip.defvjp(_recip_fwd, _recip_bwd)


def _sigmoid(x):
    return _recip(1.0 + jnp.exp(-x))


def _silu(x):
    return x * _sigmoid(x)


def _softplus(x):
    return jnp.maximum(x, 0.0) + jnp.log(1.0 + jnp.exp(-jnp.abs(x)))


def _iota(shape, axis):
    return lax.broadcasted_iota(jnp.int32, shape, axis)


def _cparams(sem, **kw):
    return pltpu.CompilerParams(dimension_semantics=sem, vmem_limit_bytes=VMEM_LIMIT, **kw)


class _ride:
    def __init__(self, rider, n_in, n_out, n_scr):
        self.rider = rider
        self.ins = rider.ins if rider else []
        n_rin = len(self.ins)
        self.out_shapes = rider.out_shapes if rider else []
        n_rout = len(self.out_shapes)
        self.in_specs, self.out_specs = [_HBM] * n_rin, [_HBM] * n_rout
        self.scratch = rider.sems() if rider else []
        self.o0 = n_in + n_rin
        self.s0 = self.o0 + n_out + n_rout
        self._rin = slice(n_in, n_in + n_rin)
        self._rout = slice(self.o0 + n_out, self.s0)
        self._sem = self.s0 + n_scr

    def _args(self, refs):
        return refs[self._rin], refs[self._rout], refs[self._sem], refs[self._sem + 1]

    def at_start(self, refs, cond):
        if self.rider:
            pl.when(cond)(lambda: self.rider.start(*self._args(refs)))

    def at_end(self, refs, cond):
        if self.rider:
            pl.when(cond)(lambda: self.rider.finish(*self._args(refs)))


def _mm(name, a, b, form, out_dtype=F32, add=None, add_scale=1.0, hi=False, tm=1024, tn=1024, tk=2816, rider=None, b_k0=None):
    if form == "nn":
        (M, K), (K2, N) = a.shape, b.shape
    elif form == "nt":
        (M, K), (N, K2) = a.shape, b.shape
        K2 = K if b_k0 is not None else K2
    else:
        (K, M), (K2, N) = a.shape, b.shape
    assert K == K2, (name, a.shape, b.shape, form)
    tm, tn, tk = _tile(M, tm), _tile(N, tn), _tile(K, tk)
    nk = K // tk
    k0 = 0 if b_k0 is None else b_k0 // tk
    assert b_k0 is None or (form == "nt" and b_k0 % tk == 0)
    a_spec = pl.BlockSpec((tk, tm), lambda i, j, k: (k, i)) if form == "tn" else pl.BlockSpec((tm, tk), lambda i, j, k: (i, k))
    b_spec = pl.BlockSpec((tn, tk), lambda i, j, k: (j, k + k0)) if form == "nt" else pl.BlockSpec((tk, tn), lambda i, j, k: (k, j))
    o_spec = pl.BlockSpec((tm, tn), lambda i, j, k: (i, j))
    has_add = add is not None

    def finish(r, c_ref, o_ref):
        if has_add:
            r = r + add_scale * c_ref[...].astype(F32)
        o_ref[...] = r.astype(out_dtype)

    n_own = 3 if has_add else 2
    grid = (M // tm, N // tn, nk)
    rd = _ride(rider, n_own, 1, 1 if nk > 1 else 0)

    def body(*refs):
        a_ref, b_ref = refs[:2]
        c_ref = refs[2] if has_add else None
        o_ref = refs[rd.o0]
        pid = [pl.program_id(d) for d in range(3)]
        rd.at_start(refs, (pid[0] == 0) & (pid[1] == 0) & (pid[2] == 0))
        if nk == 1:
            finish(_raw_dot(a_ref[...], b_ref[...], form, hi), c_ref, o_ref)
        else:
            acc = refs[rd.s0]

            @pl.when(pid[2] == 0)
            def _():
                acc[...] = jnp.zeros_like(acc)

            acc[...] += _raw_dot(a_ref[...], b_ref[...], form, hi)

            @pl.when(pid[2] == nk - 1)
            def _():
                finish(acc[...], c_ref, o_ref)
        rd.at_end(refs, (pid[0] == grid[0] - 1) & (pid[1] == grid[1] - 1) & (pid[2] == nk - 1))

    ins = [a, b] + ([add] if has_add else [])
    specs = [a_spec, b_spec] + ([o_spec] if has_add else [])
    res = pl.pallas_call(
        body, name=name, grid=grid, in_specs=specs + rd.in_specs, out_specs=[o_spec] + rd.out_specs,
        out_shape=[jax.ShapeDtypeStruct((M, N), out_dtype)] + rd.out_shapes,
        scratch_shapes=([pltpu.VMEM((tm, tn), F32)] if nk > 1 else []) + rd.scratch,
        compiler_params=_cparams(("arbitrary",) * 3 if rider else ("parallel", "parallel", "arbitrary")),
    )(*ins, *rd.ins)
    return (res[0], res[1:]) if rider else res[0]


class Row:
    def __init__(self, arr, blk, imap, hblk=None, hmap=None, gshape=None, gmap=None, gdt=(F32,)):
        self.arr, self.blk, self.imap, self.hblk, self.hmap, self.gshape, self.gmap = arr, blk, imap, hblk, hmap, gshape, gmap
        self.gdt = gdt


class Par:
    def __init__(self, arr, blk=None, imap=None, gshape=None, gmap=None):
        self.arr = arr
        self.blk = tuple(arr.shape) if blk is None else blk
        nd = len(self.blk)
        self.imap = (lambda j: (0,) * nd) if imap is None else imap
        self.gshape, self.gmap = gshape, gmap


class Out:
    def __init__(self, shape, dtype, blk, imap):
        self.shape, self.dtype, self.blk, self.imap = shape, dtype, blk, imap


def _rows_of(blk):
    return [d for d in blk if d is not None][0]


def _rowmap(name, fn, ncol, nblk, rows, pars, outs, accs=()):
    in_specs, ins = [], []
    for r in rows:
        ins.append(r.arr)
        in_specs.append(pl.BlockSpec(r.blk, r.imap))
        if r.hblk is not None:
            ins.append(r.arr)
            in_specs.append(pl.BlockSpec(r.hblk, r.hmap))
    for p in pars:
        ins.append(p.arr)
        in_specs.append(pl.BlockSpec(p.blk, (lambda im: (lambda j, n: im(j)))(p.imap)))
    out_specs = [pl.BlockSpec(o.blk, o.imap) for o in outs]
    out_shape = [jax.ShapeDtypeStruct(o.shape, o.dtype) for o in outs]
    for a in accs:
        out_specs.append(pl.BlockSpec(a, (lambda nd: (lambda j, n: (0,) * nd))(len(a))))
        out_shape.append(jax.ShapeDtypeStruct(a, F32))
    n_in = len(ins)

    def body(*refs):
        j, n = pl.program_id(0), pl.program_id(1)
        it = iter(refs[:n_in])
        rvals = []
        for r in rows:
            cur = next(it)[...]
            rvals.append((next(it)[...], cur) if r.hblk is not None else cur)
        pvals = [next(it)[...] for _ in pars]
        o_refs = refs[n_in:n_in + len(outs)]
        a_refs = refs[n_in + len(outs):]
        ovals, avals = fn(j, n == 0, rvals, pvals)
        for ref, v in zip(o_refs, ovals):
            ref[...] = v.astype(ref.dtype)
        if accs:
            @pl.when((j == 0) & (n == 0))
            def _():
                for ref in a_refs:
                    ref[...] = jnp.zeros_like(ref)
            for ref, v in zip(a_refs, avals):
                ref[...] += v

    res = pl.pallas_call(
        body, name=name, grid=(ncol, nblk), in_specs=in_specs, out_specs=out_specs, out_shape=out_shape,
        compiler_params=_cparams(("arbitrary", "arbitrary")),
    )(*ins)
    return res


def _rowmap_bwd(name, fn, ncol, nblk, rows, pars, cts):
    rev = lambda im: (lambda j, s: im(j, nblk - 1 - s))
    in_specs, ins = [], []
    for r in rows:
        ins.append(r.arr)
        in_specs.append(pl.BlockSpec(r.blk, rev(r.imap)))
        if r.hblk is not None:
            ins.append(r.arr)
            in_specs.append(pl.BlockSpec(r.hblk, rev(r.hmap)))
    for p in pars:
        ins.append(p.arr)
        in_specs.append(pl.BlockSpec(p.blk, (lambda im: (lambda j, s: im(j)))(p.imap)))
    for c in cts:
        ins.append(c.arr)
        in_specs.append(pl.BlockSpec(c.blk, rev(c.imap)))
    n_in = len(ins)
    drows = [i for i, r in enumerate(rows) if r.gshape is not None]
    dpars = [i for i, p in enumerate(pars) if p.gshape is not None]
    out_specs, out_shape, scratch = [], [], []
    for i in drows:
        r = rows[i]
        for dt in r.gdt:
            out_specs.append(pl.BlockSpec(r.blk, rev(r.gmap)))
            out_shape.append(jax.ShapeDtypeStruct(r.gshape, dt))
        if r.hblk is not None:
            scratch.append(pltpu.VMEM(tuple(d for d in r.hblk if d is not None), F32))
    n_drow_out = len(out_specs)
    for i in dpars:
        p = pars[i]
        out_specs.append(pl.BlockSpec(p.blk, (lambda im: (lambda j, s: im(j)))(p.gmap)))
        out_shape.append(jax.ShapeDtypeStruct(p.gshape, F32))

    def body(*refs):
        j, s = pl.program_id(0), pl.program_id(1)
        first = s == nblk - 1
        it = iter(refs[:n_in])
        rvals = []
        for r in rows:
            cur = next(it)[...]
            rvals.append((next(it)[...], cur) if r.hblk is not None else cur)
        pvals = [next(it)[...] for _ in pars]
        cvals = [next(it)[...].astype(F32) for _ in cts]
        g_refs = iter(refs[n_in:n_in + n_drow_out])
        p_refs = refs[n_in + n_drow_out:n_in + n_drow_out + len(dpars)]
        carries = iter(refs[n_in + n_drow_out + len(dpars):])

        def f(dr, dp):
            rv, pv = list(rvals), list(pvals)
            for i, v in zip(drows, dr):
                rv[i] = v
            for i, v in zip(dpars, dp):
                pv[i] = v
            return fn(j, first, rv, pv)

        _, vjp = jax.vjp(f, [rvals[i] for i in drows], [pvals[i] for i in dpars])
        g_r, g_p = vjp(cvals)
        for i, g in zip(drows, g_r):
            r = rows[i]
            if r.hblk is None:
                for _ in r.gdt:
                    ref = next(g_refs)
                    ref[...] = g.astype(ref.dtype)
            else:
                g_prev, g_cur = g
                carry = next(carries)
                nr, nh = g_cur.shape[-2], g_prev.shape[-2]
                tail = g_cur[..., nr - nh:nr, :] + jnp.where(s > 0, carry[...], 0.0)
                for _ in r.gdt:
                    ref = next(g_refs)
                    if nr > nh:
                        ref[..., 0:nr - nh, :] = g_cur[..., 0:nr - nh, :].astype(ref.dtype)
                    ref[..., nr - nh:nr, :] = tail.astype(ref.dtype)
                carry[...] = g_prev
        for ref, g in zip(p_refs, g_p):
            @pl.when(s == 0)
            def _():
                ref[...] = jnp.zeros_like(ref)
            ref[...] += g

    return pl.pallas_call(
        body, name=name, grid=(ncol, nblk), in_specs=in_specs, out_specs=out_specs, out_shape=out_shape,
        scratch_shapes=scratch, compiler_params=_cparams(("arbitrary", "arbitrary")),
    )(*ins)


def _rowspec(arr, tb, cw, c0, cstep=1, halo=0, grad=False, ncol=1, gdt=(F32,)):
    T = arr.shape[0]
    imap = lambda j, n: (n, c0 + cstep * j)
    hblk = hmap = None
    if halo:
        q = tb // halo
        hblk, hmap = (halo, cw), (lambda j, n: (jnp.maximum(n * q - 1, 0), c0 + cstep * j))
    gshape = (T, cw * (ncol if cstep else 1)) if grad else None
    gmap = (lambda j, n: (n, cstep * j)) if grad else None
    return Row(arr, (tb, cw), imap, hblk, hmap, gshape, gmap, gdt)


def _parspec(arr, cw=None, c0=0, grad=False, ncol=1):
    if cw is None:
        return Par(arr, gshape=tuple(arr.shape) if grad else None,
                   gmap=(lambda nd: (lambda j: (0,) * nd))(arr.ndim) if grad else None)
    r = arr.shape[0]
    return Par(arr, (r, cw), lambda j: (0, c0 + j), (r, cw * ncol) if grad else None, (lambda j: (0, j)) if grad else None)


def _ln(r, g, b):
    mu = jnp.mean(r, axis=-1, keepdims=True)
    xc = r - mu
    var = jnp.mean(xc * xc, axis=-1, keepdims=True)
    return xc * lax.rsqrt(var + NORM_EPS) * g + b


def _ln_fn(j, first, rv, pv):
    return [_ln(rv[0], pv[0], pv[1])]


def _ln_fwd_fn(j, first, rv, pv):
    y = _ln(rv[0], pv[0], pv[1])
    return [y, y], []


def _loss_fn(j, first, rv, pv):
    r3, tgt = rv
    g, b = pv
    y, vjp = jax.vjp(_ln, r3, g, b)
    diff = y - tgt
    part = 0.5 * jnp.sum(diff * diff) / D_MODEL
    dr, dg, db = vjp(diff * (1.0 / D_MODEL))
    return [dr, dr], [jnp.full((SUBLANES, LANES), part, F32), dg, db]


def _mix_fn(j, first, rv, pv):
    gs, gg, ys, yg = rv
    return [_sigmoid(gs) * ys + _sigmoid(gg) * yg]


def _row_pick(x, i):
    ax = x.ndim - 2
    return jnp.sum(jnp.where(_iota(x.shape, ax) == i, x, 0.0), axis=ax, keepdims=True)


def _causal_conv(prev, cur, w, first):
    width = w.shape[0]
    prev = jnp.where(first, 0.0, prev)
    y = cur * _row_pick(w, width - 1)
    for d in range(1, width):
        y = y + _shift_halo(prev, cur, d) * _row_pick(w, width - 1 - d)
    return y


def _ffn_act_fn(j, first, rv, pv):
    (pg, cg), (pu, cu) = rv
    wg, wu, bg, bu = pv
    hg = _causal_conv(pg, cg, wg, first) + bg
    hu = _causal_conv(pu, cu, wu, first) + bu
    return [_silu(hg) * hu]


def _gdn_pre_fn(j, first, rv, pv):
    (prev, cur), = rv
    w, = pv
    t = _silu(_causal_conv(prev, cur, w, first))
    tn = t * lax.rsqrt(jnp.sum(t * t, axis=-1, keepdims=True) + 1e-6)
    return [jnp.where(j < 2 * GDN_HEADS, tn, t)]


def _gdn_gate_fn(j, first, rv, pv):
    gba, = rv
    alog, dtb, eb, eg = pv
    tb = gba.shape[0]
    beta = _sigmoid(gba)
    g = -jnp.exp(alog) * _softplus(gba + dtb)
    ri, ci = _iota((tb, tb), 0), _iota((tb, tb), 1)
    tril = jnp.where((ri // GDN_CHUNK == ci // GDN_CHUNK) & (ci <= ri), 1.0, 0.0)
    gc = _dot(tril, g, "nn", True)
    return [_dot(beta, eb, "nn", True), _dot(gc, eg, "nn", True)]


def _swa_fn(j, first, rv, pv):
    q, (kp, kc), (vp, vc) = rv
    bp, bc, sk = pv
    sp = _dot(q, kp, "nt") * (SWA_HEAD_DIM ** -0.5) + bp
    sc = _dot(q, kc, "nt") * (SWA_HEAD_DIM ** -0.5) + bc
    qi = _iota(sp.shape, sp.ndim - 2) % SWA_BLOCK
    kj = _iota(sp.shape, sp.ndim - 1)
    sp = jnp.where((kj > qi) & jnp.logical_not(first), sp, NEG_INF)
    sc = jnp.where(kj <= qi, sc, NEG_INF)
    m = jnp.maximum(jnp.maximum(jnp.max(sp, axis=-1, keepdims=True), jnp.max(sc, axis=-1, keepdims=True)), sk)
    m = lax.stop_gradient(m)
    ep, ec, es = jnp.exp(sp - m), jnp.exp(sc - m), jnp.exp(sk - m)
    inv = 1.0 / (jnp.sum(ep, axis=-1, keepdims=True) + jnp.sum(ec, axis=-1, keepdims=True) + es)
    vp = jnp.where(first, 0.0, vp)
    return [_dot(ep * inv, vp, "nn") + _dot(ec * inv, vc, "nn")]


def _memattn_fn(j, first, rv, pv):
    q, = rv
    k, v = pv
    s = _dot(q, k, "nt") * (MEM_HEAD_DIM ** -0.5)
    m = lax.stop_gradient(jnp.max(s, axis=-1, keepdims=True))
    e = jnp.exp(s - m)
    p = e * (1.0 / jnp.sum(e, axis=-1, keepdims=True))
    return [_dot(p, v, "nn")]


SOLVE_PREC = "x3"


@jax.custom_vjp
def _unit_lower_inv(a):
    c = a.shape[-1]
    eye = _iota((1, c, c), 1) == _iota((1, c, c), 2)
    tinv = jnp.where(eye, 1.0, 0.0) - a
    x = _raw_dot(a, a, "nn", SOLVE_PREC)
    for i in range(5):
        tinv = tinv + _raw_dot(tinv, x, "nn", SOLVE_PREC)
        if i < 4:
            x = _raw_dot(x, x, "nn", SOLVE_PREC)
    return tinv


def _unit_lower_inv_fwd(a):
    t = _unit_lower_inv(a)
    return t, t


def _unit_lower_inv_bwd(t, g):
    return (-_raw_dot(_raw_dot(t, g, "tn", SOLVE_PREC), t, "nt", SOLVE_PREC),)


_unit_lower_inv.defvjp(_unit_lower_inv_fwd, _unit_lower_inv_bwd)


@jax.custom_vjp
def _known_inv(a, t):
    return t


def _known_inv_fwd(a, t):
    return t, t


def _known_inv_bwd(t, g):
    return _unit_lower_inv_bwd(t, g) + (jnp.zeros_like(t),)


_known_inv.defvjp(_known_inv_fwd, _known_inv_bwd)


def _gdn_heads(q, k, v, bx, gx, g64, z, nw, S, tinv=None, keep_tinv=False):
    c = GDN_CHUNK
    q = q * (GDN_HEAD_DIM ** -0.5)
    kb, vb = k * bx, v * bx
    ri, ci = _iota((1, c, c), 1), _iota((1, c, c), 2)
    tril, strict, eye = ci <= ri, ci < ri, ci == ri
    grow = jnp.sum(jnp.where(eye, g64, 0.0), axis=1, keepdims=True)
    decay = jnp.where(tril, jnp.exp(jnp.where(tril, g64 - grow, 0.0)), 0.0)
    a = jnp.where(strict, _dot(kb, k, "nt") * decay, 0.0)
    tinv = _unit_lower_inv(a) if tinv is None else _known_inv(a, tinv)
    eg = jnp.exp(gx)
    u = _dot(tinv, vb, "nn", SOLVE_PREC)
    w = _dot(tinv, kb * eg, "nn", SOLVE_PREC)
    ai = jnp.where(tril, _dot(q, k, "nt") * decay, 0.0)
    glast = _row_pick(gx, c - 1)
    v_new = u - _dot(w, S, "nn")
    o = _dot(q * eg, S, "nn") + _dot(ai, v_new, "nn")
    s_new = S * jnp.exp(glast) + _dot(k * jnp.exp(glast - gx), v_new, "tn")
    o = o * lax.rsqrt(jnp.mean(o * o, axis=-1, keepdims=True) + 1e-6) * nw
    return (o * _silu(z), s_new, tinv) if keep_tinv else (o * _silu(z), s_new)


GDN_STEP_CHUNKS = 2


def _head_major(ref, off, width=GDN_HEAD_DIM, ci=0):
    r = slice(ci * GDN_CHUNK, (ci + 1) * GDN_CHUNK)
    return jnp.stack([ref[r, off + h * GDN_HEAD_DIM:off + h * GDN_HEAD_DIM + width] for h in range(GDN_HEADS)])


def _gdn_chunks_fwd(qkv, bx, gx, proj, nw, rider=None):
    T = qkv.shape[0]
    cps = GDN_STEP_CHUNKS
    nc, c, hd, nh = T // (cps * GDN_CHUNK), GDN_CHUNK, GDN_HEAD_DIM, GDN_HEADS
    rd = _ride(rider, 5, 3, 1)

    def body(*refs):
        qkv_ref, bx_ref, gx_ref, z_ref, nw_ref = refs[:5]
        y_ref, st_ref, ti_ref = refs[rd.o0:rd.o0 + 3]
        S = refs[rd.s0]
        rd.at_start(refs, pl.program_id(0) == 0)

        @pl.when(pl.program_id(0) == 0)
        def _():
            S[...] = jnp.zeros_like(S)

        s_new = S[...]
        for ci in range(cps):
            st_ref[ci] = s_new
            y, s_new, ti = _gdn_heads(_head_major(qkv_ref, 0, ci=ci), _head_major(qkv_ref, GDN_W, ci=ci),
                                      _head_major(qkv_ref, 2 * GDN_W, ci=ci), _head_major(bx_ref, 0, ci=ci),
                                      _head_major(gx_ref, 0, ci=ci), _head_major(gx_ref, 0, c, ci), _head_major(z_ref, 0, ci=ci),
                                      nw_ref[...], s_new, keep_tinv=True)
            ti_ref[ci] = ti
            for h in range(nh):
                y_ref[ci * c:(ci + 1) * c, h * hd:(h + 1) * hd] = y[h].astype(y_ref.dtype)
        S[...] = s_new
        rd.at_end(refs, pl.program_id(0) == nc - 1)

    row = lambda w, cb: pl.BlockSpec((cps * c, w), lambda n: (n, cb))
    res = pl.pallas_call(
        body, name="gdn_chunks_fwd", grid=(nc,),
        in_specs=[row(3 * GDN_W, 0), row(GDN_W, 0), row(GDN_W, 0), row(GDN_W, P_GZ // GDN_W),
                  pl.BlockSpec((1, hd), lambda n: (0, 0))] + rd.in_specs,
        out_specs=[row(GDN_W, 0), pl.BlockSpec((cps, nh, hd, hd), lambda n: (n, 0, 0, 0)),
                   pl.BlockSpec((cps, nh, c, c), lambda n: (n, 0, 0, 0))] + rd.out_specs,
        out_shape=[jax.ShapeDtypeStruct((T, GDN_W), BF16), jax.ShapeDtypeStruct((nc * cps, nh, hd, hd), F32),
                   jax.ShapeDtypeStruct((nc * cps, nh, c, c), F32)] + rd.out_shapes,
        scratch_shapes=[pltpu.VMEM((nh, hd, hd), F32)] + rd.scratch,
        compiler_params=_cparams(("arbitrary",)),
    )(qkv, bx, gx, proj, nw, *rd.ins)
    return res[0], (res[1], res[2]), res[3:]


def _gdn_chunks_bwd(qkv, bx, gx, proj, nw, saved, dy, rider=None):
    states, tinvs = saved
    T = qkv.shape[0]
    cps = GDN_STEP_CHUNKS
    nc, c, hd, nh = T // (cps * GDN_CHUNK), GDN_CHUNK, GDN_HEAD_DIM, GDN_HEADS
    rd = _ride(rider, 8, 5, 1)

    def body(*refs):
        qkv_ref, bx_ref, gx_ref, z_ref, nw_ref, st_ref, ti_ref, dy_ref = refs[:8]
        dqkv_ref, dbx_ref, dgx_ref, dz_ref, dnw_ref = refs[rd.o0:rd.o0 + 5]
        dS = refs[rd.s0]
        rd.at_start(refs, pl.program_id(0) == 0)

        @pl.when(pl.program_id(0) == 0)
        def _():
            dS[...] = jnp.zeros_like(dS)
            dnw_ref[...] = jnp.zeros_like(dnw_ref)

        dsp = dS[...]
        for ci in reversed(range(cps)):
            r = slice(ci * c, (ci + 1) * c)
            args = (_head_major(qkv_ref, 0, ci=ci), _head_major(qkv_ref, GDN_W, ci=ci), _head_major(qkv_ref, 2 * GDN_W, ci=ci),
                    _head_major(bx_ref, 0, ci=ci), _head_major(gx_ref, 0, ci=ci), _head_major(gx_ref, 0, c, ci),
                    _head_major(z_ref, 0, ci=ci), nw_ref[...], st_ref[ci])
            _, vjp = jax.vjp(functools.partial(_gdn_heads, tinv=ti_ref[ci]), *args)
            dq, dk, dv, dbx, dgx, dg64, dz, dnw, dsp = vjp((_head_major(dy_ref, 0, ci=ci), dsp))
            for h in range(nh):
                sl = slice(h * hd, (h + 1) * hd)
                dqkv_ref[r, sl] = dq[h].astype(dqkv_ref.dtype)
                dqkv_ref[r, GDN_W + h * hd:GDN_W + (h + 1) * hd] = dk[h].astype(dqkv_ref.dtype)
                dqkv_ref[r, 2 * GDN_W + h * hd:2 * GDN_W + (h + 1) * hd] = dv[h].astype(dqkv_ref.dtype)
                dbx_ref[r, sl] = dbx[h]
                dgx_ref[r, sl] = dgx[h]
                dgx_ref[r, h * hd:h * hd + c] += dg64[h]
                dz_ref[r, sl] = dz[h].astype(dz_ref.dtype)
            dnw_ref[...] += dnw
        dS[...] = dsp
        rd.at_end(refs, pl.program_id(0) == nc - 1)

    row = lambda w, cb: pl.BlockSpec((cps * c, w), lambda s: (nc - 1 - s, cb))
    res = pl.pallas_call(
        body, name="gdn_chunks_bwd", grid=(nc,),
        in_specs=[row(3 * GDN_W, 0), row(GDN_W, 0), row(GDN_W, 0), row(GDN_W, P_GZ // GDN_W), pl.BlockSpec((1, hd), lambda s: (0, 0)),
                  pl.BlockSpec((cps, nh, hd, hd), lambda s: (nc - 1 - s, 0, 0, 0)),
                  pl.BlockSpec((cps, nh, c, c), lambda s: (nc - 1 - s, 0, 0, 0)), row(GDN_W, 0)] + rd.in_specs,
        out_specs=[row(3 * GDN_W, 0), row(GDN_W, 0), row(GDN_W, 0), row(GDN_W, 0),
                   pl.BlockSpec((1, hd), lambda s: (0, 0))] + rd.out_specs,
        out_shape=[jax.ShapeDtypeStruct((T, 3 * GDN_W), F32), jax.ShapeDtypeStruct((T, GDN_W), F32),
                   jax.ShapeDtypeStruct((T, GDN_W), F32), jax.ShapeDtypeStruct((T, GDN_W), _CDT),
                   jax.ShapeDtypeStruct((1, hd), F32)] + rd.out_shapes,
        scratch_shapes=[pltpu.VMEM((nh, hd, hd), F32)] + rd.scratch,
        compiler_params=_cparams(("arbitrary",)),
    )(qkv, bx, gx, proj, nw, states, tinvs, dy, *rd.ins)
    res = list(res)
    return res[:5] + [res[5:]]


def _adamw(name, w, g, m, v):
    if w.ndim == 3:
        C, _, R = w.shape
        blk = (_tile(C, 768, 1), 1, _tile(R, 512))
        grid = (C // blk[0], R // blk[2])
        spec = pl.BlockSpec(blk, lambda i, j: (i, 0, j))
    else:
        R, C = w.shape
        tr = _tile(R, 128, SUBLANES)
        grid = (R // tr,)
        spec = pl.BlockSpec((tr, C), lambda i: (i, 0))

    def body(w_ref, g_ref, m_ref, v_ref, d_ref, m2_ref, v2_ref):
        g_ = g_ref[...]
        m2 = ADAM_B1 * m_ref[...] + (1.0 - ADAM_B1) * g_
        v2 = ADAM_B2 * v_ref[...] + (1.0 - ADAM_B2) * (g_ * g_)
        m_hat = m2 / (1.0 - ADAM_B1 ** ADAM_STEP)
        v_hat = v2 / (1.0 - ADAM_B2 ** ADAM_STEP)
        d_ref[...] = -ADAM_LR * (m_hat / (jnp.sqrt(v_hat) + ADAM_EPS) + ADAM_WD * w_ref[...])
        m2_ref[...] = m2
        v2_ref[...] = v2

    return pl.pallas_call(
        body, name=name, grid=grid, in_specs=[spec] * 4, out_specs=[spec] * 3,
        out_shape=[jax.ShapeDtypeStruct(w.shape, F32)] * 3, compiler_params=_cparams(("parallel",) * len(grid)),
    )(w, g, m, v)


def _addn(name, parts, out_dtype=F32):
    parts = [p if isinstance(p, tuple) else (p, None) for p in parts]
    a0, k0 = parts[0]
    R, C = a0.shape[-2:]
    tr = _tile(R, 256, 2 * SUBLANES)
    specs = []
    for a, k in parts:
        if k is None:
            specs.append(pl.BlockSpec((tr, C), lambda i: (i, 0)))
        else:
            specs.append(pl.BlockSpec((None, tr, C), (lambda kk: (lambda i: (kk, i, 0)))(k)))

    def body(*refs):
        acc = refs[0][...].astype(F32)
        for r in refs[1:-1]:
            acc = acc + r[...].astype(F32)
        refs[-1][...] = acc.astype(out_dtype)

    return pl.pallas_call(
        body, name=name, grid=(R // tr,), in_specs=specs, out_specs=pl.BlockSpec((tr, C), lambda i: (i, 0)),
        out_shape=jax.ShapeDtypeStruct((R, C), out_dtype), compiler_params=_cparams(("parallel",)),
    )(*[a for a, _ in parts])


MESH = pl.DeviceIdType.MESH
_HBM = pl.BlockSpec(memory_space=pltpu.HBM)


def _place():
    x, y, c = lax.axis_index("x"), lax.axis_index("y"), lax.axis_index("c")
    return x, y, c, [(1 - x, y), (x, 1 - y), (1 - x, 1 - y)]


class _Rider:
    def __init__(self, ins, out_shapes, nsem, start, finish):
        self.ins, self.out_shapes, self.nsem, self.start, self.finish = list(ins), list(out_shapes), nsem, start, finish

    def sems(self):
        return [pltpu.SemaphoreType.DMA((self.nsem,)), pltpu.SemaphoreType.DMA((self.nsem,))]


def _run_rider(name, rd):
    n_in, n_out = len(rd.ins), len(rd.out_shapes)

    def body(*refs):
        ins, outs, (send, recv) = refs[:n_in], refs[n_in:n_in + n_out], refs[n_in + n_out:]
        rd.start(ins, outs, send, recv)
        rd.finish(ins, outs, send, recv)

    return pl.pallas_call(body, name=name, in_specs=[_HBM] * n_in, out_specs=[_HBM] * n_out, out_shape=rd.out_shapes,
                          scratch_shapes=rd.sems())(*rd.ins)


def _gather_rider(ts):
    nt = len(ts)

    def half(t, hc):
        rh = ts[t].shape[0] // 2
        return pl.ds(pl.multiple_of(hc * rh, 16), rh)

    def rcopy(send, recv, t, k, src, dst, to):
        return pltpu.make_async_remote_copy(src_ref=src, dst_ref=dst, send_sem=send.at[6 * t + k], recv_sem=recv.at[6 * t + k],
                                            device_id=to, device_id_type=MESH)

    def first_hop(ins, outs, send, recv, t, r, px, py, c, me):
        return rcopy(send, recv, t, r, ins[t].at[half(t, c)], outs[t].at[me, half(t, c)], (px, py, c))

    def start(ins, outs, send, recv):
        x, y, c, rel = _place()
        for t in range(nt):
            for r, (px, py) in enumerate(rel):
                first_hop(ins, outs, send, recv, t, r, px, py, c, 2 * x + y).start()

    def finish(ins, outs, send, recv):
        x, y, c, rel = _place()
        sib = (x, y, 1 - c)
        passed = []
        for t in range(nt):
            for r, (px, py) in enumerate(rel):
                got = outs[t].at[2 * px + py, half(t, c)]
                rcopy(send, recv, t, r, got, got, (px, py, c)).wait_recv()
                fw = rcopy(send, recv, t, 3 + r, got, got, sib)
                fw.start()
                passed.append(fw)
        for t in range(nt):
            for r, (px, py) in enumerate(rel):
                got = outs[t].at[2 * px + py, half(t, 1 - c)]
                rcopy(send, recv, t, 3 + r, got, got, sib).wait_recv()
        for t in range(nt):
            for r, (px, py) in enumerate(rel):
                first_hop(ins, outs, send, recv, t, r, px, py, c, 2 * x + y).wait_send()
        for fw in passed:
            fw.wait_send()

    return _Rider(ts, [jax.ShapeDtypeStruct((4,) + tuple(t.shape), t.dtype) for t in ts], 6 * nt, start, finish)


def _scatter_rider(ps):
    nt = len(ps)

    def copy(ins, outs, send, recv, t, r, px, py, c):
        return pltpu.make_async_remote_copy(src_ref=ins[t].at[2 * px + py], dst_ref=outs[t].at[r], send_sem=send.at[3 * t + r],
                                            recv_sem=recv.at[3 * t + r], device_id=(px, py, c), device_id_type=MESH)

    def start(ins, outs, send, recv):
        x, y, c, rel = _place()
        for t in range(nt):
            for r, (px, py) in enumerate(rel):
                copy(ins, outs, send, recv, t, r, px, py, c).start()

    def finish(ins, outs, send, recv):
        x, y, c, rel = _place()
        for t in range(nt):
            for r, (px, py) in enumerate(rel):
                copy(ins, outs, send, recv, t, r, px, py, c).wait()

    return _Rider(ps, [jax.ShapeDtypeStruct((3,) + tuple(p.shape[1:]), p.dtype) for p in ps], 3 * nt, start, finish)


def _pair_swap(tag, ts):
    nt = len(ts)

    def body(*refs):
        ins, outs = refs[:nt], refs[nt:2 * nt]
        send, recv = refs[2 * nt:]
        x, y, c, _ = _place()
        cps = []
        for t in range(nt):
            rh = ts[t].shape[1] // 2
            src = ins[t].at[:, pl.ds(pl.multiple_of((1 - c) * rh, 16), rh), :]
            cp = pltpu.make_async_remote_copy(src_ref=src, dst_ref=outs[t], send_sem=send.at[t], recv_sem=recv.at[t],
                                              device_id=(x, y, 1 - c), device_id_type=MESH)
            cp.start()
            cps.append(cp)
        for cp in cps:
            cp.wait()

    return pl.pallas_call(
        body, name="pair_swap_" + tag, in_specs=[_HBM] * nt, out_specs=[_HBM] * nt,
        out_shape=[jax.ShapeDtypeStruct((4, t.shape[1] // 2, t.shape[2]), t.dtype) for t in ts],
        scratch_shapes=[pltpu.SemaphoreType.DMA((nt,)), pltpu.SemaphoreType.DMA((nt,))],
    )(*ts)


def _pair_exchange(gs):
    nt = len(gs)

    def body(*refs):
        ins, outs = refs[:nt], refs[nt:2 * nt]
        send, recv = refs[2 * nt:]
        x, y, c, _ = _place()
        cps = []
        for t in range(nt):
            cp = pltpu.make_async_remote_copy(src_ref=ins[t], dst_ref=outs[t], send_sem=send.at[t], recv_sem=recv.at[t],
                                              device_id=(x, y, 1 - c), device_id_type=MESH)
            cp.start()
            cps.append(cp)
        for cp in cps:
            cp.wait()

    return pl.pallas_call(
        body, name="pair_exchange", in_specs=[_HBM] * nt, out_specs=[_HBM] * nt,
        out_shape=[jax.ShapeDtypeStruct(tuple(g.shape), g.dtype) for g in gs],
        scratch_shapes=[pltpu.SemaphoreType.DMA((nt,)), pltpu.SemaphoreType.DMA((nt,))],
    )(*gs)


def _allgather8(v):
    m, n = v.shape

    def body(x_ref, out_ref, send, recv, lsem):
        x, y, c, rel = _place()
        me, sib = (x, y, c), (x, y, 1 - c)

        def blk(px, py, pc):
            return out_ref.at[4 * px + 2 * py + pc]

        def copy(k, block, to, src=None):
            return pltpu.make_async_remote_copy(src_ref=blk(*block) if src is None else src, dst_ref=blk(*block), send_sem=send.at[k],
                                                recv_sem=recv.at[k], device_id=to, device_id_type=MESH)

        mine = pltpu.make_async_copy(x_ref, blk(*me), lsem)
        mine.start()
        first = [copy(0, me, sib, src=x_ref)] + [copy(1 + r, me, (*ch, c), src=x_ref) for r, ch in enumerate(rel)]
        for cp in first:
            cp.start()
        passed = [copy(4 + r, (*ch, c), sib) for r, ch in enumerate(rel)]
        for r, ch in enumerate(rel):
            copy(1 + r, (*ch, c), me).wait_recv()
            passed[r].start()
        copy(0, sib, me).wait_recv()
        for r, ch in enumerate(rel):
            copy(4 + r, (*ch, 1 - c), me).wait_recv()
        for cp in first + passed:
            cp.wait_send()
        mine.wait()

    return pl.pallas_call(
        body, name="allgather8", in_specs=[pl.BlockSpec(memory_space=pltpu.VMEM)], out_specs=pl.BlockSpec(memory_space=pltpu.VMEM),
        out_shape=jax.ShapeDtypeStruct((8, m, n), v.dtype),
        scratch_shapes=[pltpu.SemaphoreType.DMA((7,)), pltpu.SemaphoreType.DMA((7,)), pltpu.SemaphoreType.DMA],
    )(v)


def _t5_bucket(dist):
    max_exact = REL_BUCKETS // 2
    d = jnp.maximum(dist, 1).astype(F32)
    large = max_exact + (jnp.log(d / max_exact) / math.log(REL_MAX_DIST / max_exact) * (REL_BUCKETS - max_exact)).astype(jnp.int32)
    large = jnp.minimum(large, REL_BUCKETS - 1)
    return jnp.where(dist < max_exact, dist, large)


def _bias_onehot():
    qi = jnp.arange(SWA_BLOCK)[:, None]
    kj = jnp.arange(SWA_BLOCK)[None, :]
    dist = jnp.concatenate([(qi + SWA_BLOCK - kj).reshape(-1), (qi - kj).reshape(-1)])
    bucket = _t5_bucket(jnp.maximum(dist, 0))
    return (bucket[None, :] == jnp.arange(REL_BUCKETS)[:, None]).astype(F32)


def _head_spread():
    lane = jnp.arange(LANES)[:, None]
    head = jnp.arange(GDN_W)[None, :] // GDN_HEAD_DIM
    return (lane == head).astype(F32), (lane == head + GDN_HEADS).astype(F32)


def _lane16(v8):
    return jnp.pad(v8.astype(F32), (GDN_HEADS, LANES - 2 * GDN_HEADS)).reshape(1, LANES)


def _stack_heads(t, nb):
    return t.reshape(nb, SWA_BLOCK, SWA_KV_HEADS, SWA_GRP, SWA_HEAD_DIM).transpose(2, 0, 3, 1, 4).reshape(
        SWA_KV_HEADS, nb * SWA_GRP * SWA_BLOCK, SWA_HEAD_DIM)


def _unstack_heads(t, nb):
    return t.reshape(SWA_KV_HEADS, nb, SWA_GRP, SWA_BLOCK, SWA_HEAD_DIM).transpose(1, 3, 0, 2, 4).reshape(nb * SWA_BLOCK, SWA_Q)


def _kv_heads(t):
    return t.reshape(t.shape[0], SWA_KV_HEADS, SWA_HEAD_DIM).transpose(1, 0, 2)


def _swa_specs(qs, ks, vs, bp, bc, sk, grad, gdt=(F32,)):
    T = ks.shape[1]
    qr = SWA_GRP * SWA_BLOCK
    g = lambda a: tuple(a.shape) if grad else None
    nk = SWA_KV_HEADS
    m3 = lambda j, n: (0, n, 0)
    h3 = lambda j, n: (0, jnp.maximum(n - 1, 0), 0)
    p3 = lambda j: (0, 0, 0)
    rows = [Row(qs, (nk, qr, SWA_HEAD_DIM), m3, gshape=g(qs), gmap=m3, gdt=gdt),
            Row(ks, (nk, SWA_BLOCK, SWA_HEAD_DIM), m3, (nk, SWA_BLOCK, SWA_HEAD_DIM), h3, g(ks), m3, gdt),
            Row(vs, (nk, SWA_BLOCK, SWA_HEAD_DIM), m3, (nk, SWA_BLOCK, SWA_HEAD_DIM), h3, g(vs), m3, gdt)]
    pars = [Par(bp, (nk, qr, SWA_BLOCK), p3, g(bp), p3), Par(bc, (nk, qr, SWA_BLOCK), p3, g(bc), p3),
            Par(sk, (nk, qr, 1), p3, g(sk), p3)]
    return rows, pars, T // SWA_BLOCK


class _LocalWeights:
    def __init__(self, W):
        self.W = W

    def w1(self):
        return self.W

    def rider_a(self):
        return None

    def w2(self, got):
        return self.W

    def rider_b(self):
        return None

    def w3(self, got):
        return self.W

    def rider_g(self, G):
        return None

    def g_done(self, got):
        pass

    def rider_up(self, G):
        return None

    def up_done(self, got):
        pass

    def rider_last(self, G):
        return None

    def last_done(self, got):
        pass


def _fwd_bwd(x, mem, tgt, src):
    W = dict(src.w1())
    T = x.shape[0]
    nb = T // SWA_BLOCK
    tb = min(256, T)
    tbl = min(512, T)
    fwd = lambda f: (lambda *a: (f(*a), []))
    full = lambda cols, dt, t, cw: Out((T, cols), dt, (t, cw), lambda j, n: (n, j))

    xb = x.astype(_CDT)
    ra = src.rider_a()
    proj = _mm("proj", xb, W["in_p"], "nn", rider=ra)
    proj, got = proj if ra is not None else (proj, None)
    W.update(src.w2(got))

    onehot_t = _bias_onehot()
    bias_flat = _mm("swa_bias", W["rel_bias"].T, onehot_t, "nn", hi=True)
    half = SWA_BLOCK * SWA_BLOCK
    bp = bias_flat[:, :half].reshape(SWA_KV_HEADS, SWA_GRP * SWA_BLOCK, SWA_BLOCK)
    bc = bias_flat[:, half:].reshape(SWA_KV_HEADS, SWA_GRP * SWA_BLOCK, SWA_BLOCK)
    sk = jnp.broadcast_to(W["swa_sinks"].reshape(SWA_KV_HEADS, SWA_GRP, 1, 1), (SWA_KV_HEADS, SWA_GRP, SWA_BLOCK, 1)).reshape(
        SWA_KV_HEADS, SWA_GRP * SWA_BLOCK, 1)
    qs = _stack_heads(proj[:, P_SQ:P_SQ + SWA_Q], nb)
    ks = _kv_heads(proj[:, P_SK:P_SK + SWA_KV])
    vs = _kv_heads(proj[:, P_SV:P_SV + SWA_KV])
    rows, pars, nblk = _swa_specs(qs, ks, vs, bp, bc, sk, False)
    o_s, = _rowmap("swa_fwd", fwd(_swa_fn), 1, nblk, rows, pars,
                   [Out(tuple(qs.shape), F32, (SWA_KV_HEADS, SWA_GRP * SWA_BLOCK, SWA_HEAD_DIM), lambda j, n: (0, n, 0))])
    o_swa = _unstack_heads(o_s, nb).astype(_CDT)

    ncq = 3 * GDN_W // LANES
    tbp = min(1024, T)
    pre_rows = lambda grad: [_rowspec(proj, tbp, LANES, P_GQKV // LANES, halo=SUBLANES, grad=grad, ncol=ncq, gdt=(_CDT,))]
    pre_pars = lambda grad: [_parspec(W["gdn_conv_w"], LANES, 0, grad=grad, ncol=ncq)]
    qkv_n, = _rowmap("gdn_pre_fwd", fwd(_gdn_pre_fn), ncq, T // tbp, pre_rows(False), pre_pars(False),
                     [full(3 * GDN_W, F32, tbp, LANES)])
    eb, eg = _head_spread()
    alog_row, dtb_row = _lane16(W["gdn_a_log"]), _lane16(W["gdn_dt_bias"])
    gate_rows = lambda grad: [_rowspec(proj, tbl, LANES, P_BA // LANES, cstep=0, grad=grad, gdt=(_CDT,))]
    gate_pars = lambda grad: [_parspec(alog_row, grad=grad), _parspec(dtb_row, grad=grad), _parspec(eb), _parspec(eg)]
    bx, gx = _rowmap("gdn_gate_fwd", fwd(_gdn_gate_fn), 1, T // tbl, gate_rows(False), gate_pars(False),
                     [full(GDN_W, F32, tbl, GDN_W), full(GDN_W, F32, tbl, GDN_W)])
    nw = W["gdn_norm_w"].reshape(1, GDN_HEAD_DIM)
    o_gdn, states, got = _gdn_chunks_fwd(qkv_n, bx, gx, proj, nw, rider=src.rider_b())
    W.update(src.w3(got))

    ys = _mm("y_swa", o_swa, W["br_swa"], "nn")
    yg = _mm("y_gdn", o_gdn, W["br_gdn"], "nn")
    cwm = 512
    mix_rows = lambda grad: [_rowspec(proj, tb, cwm, P_GS // cwm, grad=grad, ncol=D_MODEL // cwm, gdt=(_CDT,)),
                             _rowspec(proj, tb, cwm, P_GG // cwm, grad=grad, ncol=D_MODEL // cwm, gdt=(_CDT,)),
                             _rowspec(ys, tb, cwm, 0, grad=grad, ncol=D_MODEL // cwm, gdt=(_CDT,)),
                             _rowspec(yg, tb, cwm, 0, grad=grad, ncol=D_MODEL // cwm, gdt=(_CDT,))]
    mixed, = _rowmap("mix_fwd", fwd(_mix_fn), D_MODEL // cwm, T // tb, mix_rows(False), [], [full(D_MODEL, _CDT, tb, cwm)])
    r1 = _mm("r1", mixed, W["mix_o"], "nn", add=x, add_scale=ALPHA)

    def ln_fwd(name, r, g, b):
        return _rowmap(name, _ln_fwd_fn, 1, T // tb, [_rowspec(r, tb, D_MODEL, 0)], [_parspec(g), _parspec(b)],
                       [full(D_MODEL, F32, tb, D_MODEL), full(D_MODEL, _CDT, tb, D_MODEL)])

    def ln_bwd(name, r, g, b, ct):
        return _rowmap_bwd(name, _ln_fn, 1, T // tb, [_rowspec(r, tb, D_MODEL, 0, grad=True, gdt=(F32, _CDT))],
                           [_parspec(g, grad=True), _parspec(b, grad=True)], [_rowspec(ct, tb, D_MODEL, 0)])

    g1, b1 = W["ln1_g"].reshape(1, -1), W["ln1_b"].reshape(1, -1)
    g2, b2 = W["ln2_g"].reshape(1, -1), W["ln2_b"].reshape(1, -1)
    g3, b3 = W["ln3_g"].reshape(1, -1), W["ln3_b"].reshape(1, -1)
    x1, x1b = ln_fwd("ln1_fwd", r1, g1, b1)

    qm = _mm("mem_q", x1b, W["mem_q"], "nn")
    kvm = _mm("mem_kv", mem, W["mem_kv"], "nn")
    ma_rows = lambda grad: [_rowspec(qm, tbl, MEM_HEAD_DIM, 0, grad=grad, ncol=MEM_HEADS, gdt=(_CDT,))]
    ma_pars = lambda grad: [_parspec(kvm, MEM_HEAD_DIM, 0, grad=grad, ncol=MEM_HEADS),
                            _parspec(kvm, MEM_HEAD_DIM, MEM_HEADS, grad=grad, ncol=MEM_HEADS)]
    om, = _rowmap("memattn_fwd", fwd(_memattn_fn), MEM_HEADS, T // tbl, ma_rows(False), ma_pars(False),
                  [full(MEM_W, _CDT, tbl, MEM_HEAD_DIM)])
    r2 = _mm("r2", om, W["mem_o"], "nn", add=x1, add_scale=ALPHA)
    x2, x2b = ln_fwd("ln2_fwd", r2, g2, b2)

    hcat = _mm("ffn_up", x2b, W["up_p"], "nn")
    cwf = 512
    ncf = D_FF_PAD // cwf
    cw_p, cb_p = W["ffn_conv_w_p"], W["ffn_conv_b_p"]
    tbf = min(512, T)
    ffn_rows = lambda grad: [_rowspec(hcat, tbf, cwf, 0, halo=SUBLANES, grad=grad, ncol=ncf, gdt=(_CDT,)),
                             _rowspec(hcat, tbf, cwf, ncf, halo=SUBLANES, grad=grad, ncol=ncf, gdt=(_CDT,))]
    ffn_pars = lambda grad: [_parspec(cw_p, cwf, 0, grad=grad, ncol=ncf), _parspec(cw_p, cwf, ncf, grad=grad, ncol=ncf),
                             _parspec(cb_p, cwf, 0, grad=grad, ncol=ncf), _parspec(cb_p, cwf, ncf, grad=grad, ncol=ncf)]
    act, = _rowmap("ffn_act_fwd", fwd(_ffn_act_fn), ncf, T // tbf, ffn_rows(False), ffn_pars(False), [full(D_FF_PAD, _CDT, tbf, cwf)])
    r3 = _mm("r3", act, W["down_p"], "nn", add=x2, add_scale=ALPHA)
    dr3, dr3b, lacc, dg3, db3 = _rowmap("ln3_loss", _loss_fn, 1, T // tb, [_rowspec(r3, tb, D_MODEL, 0), _rowspec(tgt, tb, D_MODEL, 0)],
                                        [_parspec(g3), _parspec(b3)], [full(D_MODEL, F32, tb, D_MODEL), full(D_MODEL, _CDT, tb, D_MODEL)],
                                  accs=[(SUBLANES, LANES), (1, D_MODEL), (1, D_MODEL)])
    loss = lacc[0, 0]

    G = {}
    G["down_p"] = _mm("dw_down", act, dr3b, "tn", out_dtype=_GDT)
    dact = _mm("d_act", dr3b, W["down_p"], "nt")
    dhg, dhu, dcwg, dcwu, dcbg, dcbu = _rowmap_bwd("ffn_act_bwd", _ffn_act_fn, ncf, T // tbf, ffn_rows(True), ffn_pars(True),
                                                   [_rowspec(dact, tbf, cwf, 0)])
    dx2 = _mm("dx2_gate", dhg, W["up_p"], "nt", add=dr3, add_scale=ALPHA, b_k0=0)
    dx2 = _mm("dx2_up", dhu, W["up_p"], "nt", add=dx2, b_k0=D_FF_PAD)
    G["up_p"] = jnp.concatenate([_mm("dw_gate", x2b, dhg, "tn", out_dtype=_GDT), _mm("dw_up", x2b, dhu, "tn", out_dtype=_GDT)], axis=1)
    G["ffn_conv_w"] = jnp.concatenate([dcwg[:, :D_FF], dcwu[:, :D_FF]], axis=1)
    G["ffn_conv_b"] = jnp.concatenate([dcbg[0, :D_FF], dcbu[0, :D_FF]])
    G["ln3_g"], G["ln3_b"] = dg3[0], db3[0]

    dr2, dr2b, dg2, db2 = ln_bwd("ln2_bwd", r2, g2, b2, dx2)
    G["ln2_g"], G["ln2_b"] = dg2[0], db2[0]
    G["mem_o"] = _mm("dw_mem_o", om, dr2b, "tn", out_dtype=_GDT)
    dom = _mm("d_om", dr2b, W["mem_o"], "nt", out_dtype=_CDT)
    dqm, dkm, dvm = _rowmap_bwd("memattn_bwd", _memattn_fn, MEM_HEADS, T // tbl, ma_rows(True), ma_pars(True),
                                [_rowspec(dom, tbl, MEM_HEAD_DIM, 0)])
    G["mem_kv"] = _mm("dw_mem_kv", mem.astype(_CDT), jnp.concatenate([dkm, dvm], axis=1).astype(_CDT), "tn", out_dtype=_GDT)
    G["mem_q"] = _mm("dw_mem_q", x1b, dqm, "tn", out_dtype=_GDT)
    dx1 = _mm("dx1", dqm, W["mem_q"], "nt", add=dr2, add_scale=ALPHA)

    dr1, dr1b, dg1, db1 = ln_bwd("ln1_bwd", r1, g1, b1, dx1)
    G["ln1_g"], G["ln1_b"] = dg1[0], db1[0]
    G["mix_o"] = _mm("dw_mix_o", mixed, dr1b, "tn", out_dtype=_GDT)
    dmixed = _mm("d_mixed", dr1b, W["mix_o"], "nt")
    dgs, dgg, dys, dyg = _rowmap_bwd("mix_bwd", _mix_fn, D_MODEL // cwm, T // tb, mix_rows(True), [], [_rowspec(dmixed, tb, cwm, 0)])
    G["br_swa"] = _mm("dw_br_swa", o_swa, dys, "tn", out_dtype=_GDT)
    G["br_gdn"] = _mm("dw_br_gdn", o_gdn, dyg, "tn", out_dtype=_GDT)
    do_swa = _mm("d_o_swa", dys, W["br_swa"], "nt", out_dtype=_CDT)
    do_gdn = _mm("d_o_gdn", dyg, W["br_gdn"], "nt")

    rows, pars, nblk = _swa_specs(qs, ks, vs, bp, bc, sk, True, (_CDT,))
    m3 = lambda j, n: (0, n, 0)
    dqs, dks, dvs, dbp, dbc, dsk = _rowmap_bwd("swa_bwd", _swa_fn, 1, nblk, rows, pars,
                                               [Row(_stack_heads(do_swa, nb), (SWA_KV_HEADS, SWA_GRP * SWA_BLOCK, SWA_HEAD_DIM), m3)])
    d_swa = jnp.concatenate([_unstack_heads(dqs, nb), dks.transpose(1, 0, 2).reshape(T, SWA_KV),
                             dvs.transpose(1, 0, 2).reshape(T, SWA_KV)], axis=1)
    dbias = jnp.concatenate([dbp.reshape(SWA_HEADS, half), dbc.reshape(SWA_HEADS, half)], axis=1)
    G["rel_bias"] = _mm("d_rel_bias", dbias, onehot_t.T, "nn", hi=True).T
    G["swa_sinks"] = _mm("d_sinks", dsk.reshape(SWA_HEADS, SWA_BLOCK), jnp.ones((SWA_BLOCK, LANES), F32), "nn", hi=True)[:, 0]

    dqkv_n, dbx, dgx, dz, dnw, got = _gdn_chunks_bwd(qkv_n, bx, gx, proj, nw, states, do_gdn, rider=src.rider_g(G))
    src.g_done(got)
    G["gdn_norm_w"] = dnw[0]
    dgba, dalog, ddtb = _rowmap_bwd("gdn_gate_bwd", _gdn_gate_fn, 1, T // tbl, gate_rows(True), gate_pars(True),
                                    [_rowspec(dbx, tbl, GDN_W, 0), _rowspec(dgx, tbl, GDN_W, 0)])
    G["gdn_a_log"], G["gdn_dt_bias"] = dalog[0, GDN_HEADS:2 * GDN_HEADS], ddtb[0, GDN_HEADS:2 * GDN_HEADS]
    dgqkv, dcw_gdn = _rowmap_bwd("gdn_pre_bwd", _gdn_pre_fn, ncq, T // tbp, pre_rows(True), pre_pars(True),
                                 [_rowspec(dqkv_n, tbp, LANES, 0)])
    G["gdn_conv_w"] = dcw_gdn

    dproj = jnp.concatenate([dgs, dgg, dgqkv, dz, d_swa, dgba, jnp.zeros((T, P_END - P_USED), _CDT)], axis=1)
    ru = src.rider_up(G)
    G["in_p"] = _mm("dw_in", xb, dproj, "tn", out_dtype=_GDT, rider=ru)
    if ru is not None:
        G["in_p"], got = G["in_p"]
        src.up_done(got)
    rl = src.rider_last(G)
    dx = _mm("dx", dproj, W["in_p"], "nt", add=dr1, add_scale=ALPHA, rider=rl)
    if rl is not None:
        dx, got = dx
        src.last_done(got)
    return loss, dx, G


W_NAMES = ["w_in", "rel_bias", "swa_sinks", "gdn_conv_w", "gdn_a_log", "gdn_dt_bias", "gdn_norm_w", "w_br_swa", "w_br_gdn",
           "w_mix_o", "ln1_g", "ln1_b", "w_mem_q", "w_mem_kv", "w_mem_o", "ln2_g", "ln2_b", "w_up", "ffn_conv_w", "ffn_conv_b",
           "w_down", "ln3_g", "ln3_b"]
BIG = ["w_in", "w_br_swa", "w_br_gdn", "w_mix_o", "w_mem_q", "w_mem_kv", "w_mem_o", "w_up", "w_down"]
SMALL = [n for n in W_NAMES if n not in BIG]
COL_SHARDED = ["w_in", "w_br_swa", "w_br_gdn", "w_mem_o", "w_up"]


def _pack(arrs):
    rows = []
    for a in arrs:
        f = a.reshape(-1).astype(F32)
        rows.append(jnp.pad(f, (0, (-f.shape[0]) % LANES)).reshape(-1, LANES))
    n = sum(r.shape[0] for r in rows)
    if n % 16:
        rows.append(jnp.zeros((16 - n % 16, LANES), F32))
    return jnp.concatenate(rows, axis=0)


def _unpack(p, shapes):
    out, off = [], 0
    for s in shapes:
        n = int(np.prod(s)) if len(s) else 1
        r = -(-n // LANES)
        out.append(p[off:off + r].reshape(-1)[:n].reshape(s))
        off += r
    return out


def _merge_shards(d):
    cat = lambda names: jnp.concatenate([d[n] for n in names], axis=-2)
    return [d.get("w_in"), d["w_up"], cat(["w_br_swa", "w_br_gdn", "w_mem_q", "w_mem_o"]), cat(["w_mix_o", "w_down"]), d["w_mem_kv"]]


def _split_shards(ts):
    a, b, c, dd, e = ts
    return {"w_in": a, "w_up": b, "w_br_swa": c[..., 0:1024, :], "w_br_gdn": c[..., 1024:2048, :], "w_mem_q": c[..., 2048:2560, :],
            "w_mem_o": c[..., 2560:3072, :], "w_mix_o": dd[..., 0:512, :], "w_down": dd[..., 512:, :], "w_mem_kv": e}


def _to_full(name, t):
    if name in COL_SHARDED:
        return _cols_from_chips(t, [(0, 4 * t.shape[2])])
    return t.reshape(4 * t.shape[1], t.shape[2])


def _to_chips(name, t):
    if name in COL_SHARDED:
        return _chips_from_cols(t, [(0, t.shape[1])], t.shape[1] // 4)
    return t.reshape(4, t.shape[0] // 4, t.shape[1])


def _cols_from_chips(g, segs, own=None):
    C, parts = g.shape[2], []
    for s in segs:
        if isinstance(s, int):
            parts.append(jnp.zeros((g.shape[1], s), g.dtype))
            continue
        lo, hi = s
        while lo < hi:
            k = lo // C
            e = min(hi, (k + 1) * C)
            piece = g[k][:, lo - k * C:e - k * C]
            parts.append(piece if own is None else jnp.where(own[1] == k, own[0][:, lo - k * C:e - k * C], piece))
            lo = e
    return jnp.concatenate(parts, axis=1)


def _chips_from_cols(p, segs, C):
    out = []
    for k in range(4):
        lo, hi, parts, o = k * C, (k + 1) * C, [], 0
        for plo, w in segs:
            a, b = max(lo, o), min(hi, o + w)
            if a < b:
                parts.append(p[:, plo + a - o:plo + b - o])
            o += w
        out.append(jnp.concatenate(parts, axis=1))
    return jnp.stack(out)


_IN_OFF = np.cumsum((0,) + IN_WIDTHS)
_IN_SEGS = [(P_SQ, SWA_Q), (P_SK, SWA_KV), (P_SV, SWA_KV), (P_GQKV, 3 * GDN_W), (P_GZ, GDN_W), (P_BA, 2 * GDN_HEADS),
            (P_GS, D_MODEL), (P_GG, D_MODEL)]
_IN_PADDED = [(int(_IN_OFF[i]), int(_IN_OFF[k])) for i, k in ((9, 10), (10, 11), (3, 6), (6, 7), (0, 1), (1, 2), (2, 3), (7, 9))] + [
    P_END - P_BA - 2 * GDN_HEADS]
_UP_SEGS = [(0, D_FF), (D_FF_PAD, D_FF)]
_UP_PADDED = [(0, D_FF), D_FF_PAD - D_FF, (D_FF, 2 * D_FF), D_FF_PAD - D_FF]


def _in_to_padded(w):
    o = _IN_OFF
    cut = lambda i, k: w[:, o[i]:o[k]]
    return jnp.concatenate([cut(9, 10), cut(10, 11), cut(3, 6), cut(6, 7), cut(0, 1), cut(1, 2), cut(2, 3), cut(7, 9),
                            jnp.zeros((w.shape[0], P_END - P_BA - 2 * GDN_HEADS), w.dtype)], axis=1)


def _in_from_padded(p):
    return jnp.concatenate([p[:, P_SQ:P_SQ + SWA_Q], p[:, P_SK:P_SK + SWA_KV], p[:, P_SV:P_SV + SWA_KV], p[:, P_GQKV:P_GQKV + 3 * GDN_W],
                            p[:, P_GZ:P_GZ + GDN_W], p[:, P_BA:P_BA + 2 * GDN_HEADS], p[:, P_GS:P_GS + D_MODEL], p[:, P_GG:P_GG + D_MODEL]],
                           axis=1)


def _ff_pad(t, axis):
    g, u = jnp.split(t, 2, axis=axis)
    pad = [(0, 0)] * t.ndim
    pad[axis] = (0, D_FF_PAD - D_FF)
    return jnp.concatenate([jnp.pad(g, pad), jnp.pad(u, pad)], axis=axis)


def _ff_unpad(t, axis):
    g, u = jnp.split(t, 2, axis=axis)
    return jnp.concatenate([lax.slice_in_dim(g, 0, D_FF, axis=axis), lax.slice_in_dim(u, 0, D_FF, axis=axis)], axis=axis)


def _assemble_weights(full, small):
    W = dict(small)
    W["in_p"] = _in_to_padded(full["w_in"])
    W["up_p"] = _ff_pad(full["w_up"], 1)
    W["down_p"] = jnp.pad(full["w_down"], ((0, D_FF_PAD - D_FF), (0, 0)))
    W["br_swa"], W["br_gdn"], W["mix_o"] = full["w_br_swa"], full["w_br_gdn"], full["w_mix_o"]
    W["mem_q"], W["mem_kv"], W["mem_o"] = full["w_mem_q"], full["w_mem_kv"], full["w_mem_o"]
    W["ffn_conv_w_p"] = _ff_pad(small["ffn_conv_w"], 1)
    W["ffn_conv_b_p"] = _ff_pad(small["ffn_conv_b"].reshape(1, -1), 1)
    return W


def _full_grads(G):
    out = {"w_in": _in_from_padded(G["in_p"])} if "in_p" in G else {}
    out.update({"w_up": _ff_unpad(G["up_p"], 1), "w_down": G["down_p"][:D_FF], "w_br_swa": G["br_swa"], "w_br_gdn": G["br_gdn"],
                "w_mix_o": G["mix_o"], "w_mem_q": G["mem_q"], "w_mem_kv": G["mem_kv"], "w_mem_o": G["mem_o"]})
    return out


def kernel(x, mem, w_in, rel_bias, swa_sinks, gdn_conv_w, gdn_a_log, gdn_dt_bias, gdn_norm_w, w_br_swa, w_br_gdn, w_mix_o, ln1_g, ln1_b, w_mem_q, w_mem_kv, w_mem_o, ln2_g, ln2_b, w_up, ffn_conv_w, ffn_conv_b, w_down, ln3_g, ln3_b, loss_target, m_w_in, m_rel_bias, m_swa_sinks, m_gdn_conv_w, m_gdn_a_log, m_gdn_dt_bias, m_gdn_norm_w, m_w_br_swa, m_w_br_gdn, m_w_mix_o, m_ln1_g, m_ln1_b, m_w_mem_q, m_w_mem_kv, m_w_mem_o, m_ln2_g, m_ln2_b, m_w_up, m_ffn_conv_w, m_ffn_conv_b, m_w_down, m_ln3_g, m_ln3_b, v_w_in, v_rel_bias, v_swa_sinks, v_gdn_conv_w, v_gdn_a_log, v_gdn_dt_bias, v_gdn_norm_w, v_w_br_swa, v_w_br_gdn, v_w_mix_o, v_ln1_g, v_ln1_b, v_w_mem_q, v_w_mem_kv, v_w_mem_o, v_ln2_g, v_ln2_b, v_w_up, v_ffn_conv_w, v_ffn_conv_b, v_w_down, v_ln3_g, v_ln3_b):
    a = dict(locals())
    w = {n: a[n] for n in W_NAMES}
    m = {n: a["m_" + n] for n in W_NAMES}
    v = {n: a["v_" + n] for n in W_NAMES}
    chip = 2 * lax.axis_index("x") + lax.axis_index("y")
    core = lax.axis_index("c")
    sq = lambda t: t.reshape(t.shape[1:]) if (t.ndim > 1 and t.shape[0] == 1 and t is not rel_bias) else t

    sh_a, sh_b, sh_c, sh_d, sh_e = _merge_shards({n: sq(w[n]).astype(_CDT) for n in BIG})
    fcw_sh, gcw_sh = sq(ffn_conv_w).shape, sq(gdn_conv_w).shape
    slot = lax.broadcasted_iota(jnp.int32, (4, 1, 1), 0)

    def with_own(got, mine):
        return [jnp.where(slot == chip, t[None], g) for g, t in zip(got, mine)]

    def reduce_start(tag, gch):
        pair = []
        for t, (mine, got) in enumerate(zip(gch, _pair_swap(tag, gch))):
            rh = mine.shape[1] // 2
            mine_h = lax.dynamic_slice_in_dim(mine, core * rh, rh, axis=1)
            pair.append(_addn(f"pair_sum_{tag}{t}", [mine_h.reshape(4 * rh, -1), got.reshape(4 * rh, -1)], _GDT).reshape(4, rh, -1))
        return pair

    def reduce_end(tag, pair, others):
        halves = []
        for t, (p, o) in enumerate(zip(pair, others)):
            own = lax.dynamic_index_in_dim(p, chip, 0, keepdims=False)
            halves.append(_addn(f"chip_sum_{tag}{t}", [own, (o, 0), (o, 1), (o, 2)]))
        return halves

    class MeshWeights:
        def w1(self):
            mine = [sh_a, _pack([sq(ffn_conv_w), sq(gdn_conv_w)])]
            got_a, got_f = _run_rider("gather_first", _gather_rider(mine))
            got_f, = with_own([got_f], mine[1:])
            conv = [_unpack(got_f[k], [fcw_sh, gcw_sh]) for k in range(4)]
            W = {n: sq(w[n]) for n in SMALL}
            W["ffn_conv_w"] = jnp.concatenate([cv[0] for cv in conv], axis=1)
            W["gdn_conv_w"] = jnp.concatenate([cv[1] for cv in conv], axis=1)
            W["ffn_conv_w_p"] = _ff_pad(W["ffn_conv_w"], 1)
            W["ffn_conv_b_p"] = _ff_pad(W["ffn_conv_b"].reshape(1, -1), 1)
            W["in_p"] = _cols_from_chips(got_a, _IN_PADDED, own=(sh_a, chip))
            return W

        def rider_a(self):
            return _gather_rider([sh_c, sh_d, sh_e])

        def w2(self, got):
            c, d, e = with_own(got, [sh_c, sh_d, sh_e])
            f = {n: _to_full(n, t) for n, t in _split_shards([None, None, c, d, e]).items() if t is not None}
            return {"br_swa": f["w_br_swa"], "br_gdn": f["w_br_gdn"], "mix_o": f["w_mix_o"], "mem_q": f["w_mem_q"], "mem_kv": f["w_mem_kv"],
                    "mem_o": f["w_mem_o"], "down_p": jnp.pad(f["w_down"], ((0, D_FF_PAD - D_FF), (0, 0)))}

        def rider_b(self):
            return _gather_rider([sh_b])

        def w3(self, got):
            return {"up_p": _cols_from_chips(got[0], _UP_PADDED, own=(sh_b, chip))}

        def rider_g(self, G):
            gf = _full_grads(G)
            gch = {n: _to_chips(n, gf[n]) for n in BIG if n not in ("w_in", "w_up")}
            gch["w_up"] = None
            self.pair = reduce_start("rest", _merge_shards(gch)[2:])
            return _scatter_rider(self.pair)

        def g_done(self, got):
            self.halves = reduce_end("rest", self.pair, got)

        def rider_up(self, G):
            self.pair_up = reduce_start("up", [_chips_from_cols(G["up_p"], _UP_SEGS, 2 * D_FF // 4)])
            return _scatter_rider(self.pair_up)

        def up_done(self, got):
            self.halves = reduce_end("up", self.pair_up, got) + self.halves

        def rider_last(self, G):
            self.pair_in = reduce_start("in", [_chips_from_cols(G["in_p"], _IN_SEGS, sum(IN_WIDTHS) // 4)])
            return _scatter_rider(self.pair_in)

        def last_done(self, got):
            self.halves = reduce_end("in", self.pair_in, got) + self.halves

    src = MeshWeights()
    loss, dx, G = _fwd_bwd(x[0], mem[0], loss_target[0], src)

    small_names = SMALL
    small_shapes = [()] + [tuple(G[n].shape) for n in small_names]
    packed = _pack([loss] + [G[n] for n in small_names])
    allp = _allgather8(packed)
    tot = _addn("small_sum", [(allp, k) for k in range(8)])
    parts = _unpack(tot, small_shapes)
    loss_tot, gsmall = parts[0], dict(zip(small_names, parts[1:]))
    gsmall["ffn_conv_w"] = lax.dynamic_slice_in_dim(gsmall["ffn_conv_w"], chip * fcw_sh[1], fcw_sh[1], axis=1)
    gsmall["gdn_conv_w"] = lax.dynamic_slice_in_dim(gsmall["gdn_conv_w"], chip * gcw_sh[1], gcw_sh[1], axis=1)

    both = []
    for h, o in zip(src.halves, _pair_exchange(src.halves)):
        both.append(jnp.concatenate([jnp.where(core == 0, h, o), jnp.where(core == 0, o, h)], axis=0))
    gbig = _split_shards(both)

    outs = {}
    for n in BIG:
        if w[n].shape[-1] % LANES:
            cols_out = lambda t: jnp.transpose(t, (2, 0, 1))
            g_ = jnp.transpose(gbig[n])[:, None, :]
            d_, m_, v_ = _adamw("adamw_" + n, cols_out(w[n]), g_, cols_out(m[n]), cols_out(v[n]))
            outs[n] = tuple(jnp.transpose(t, (1, 2, 0)) for t in (g_, d_, m_, v_))
            continue
        d_, m_, v_ = _adamw("adamw_" + n, sq(w[n]), gbig[n], sq(m[n]), sq(v[n]))
        outs[n] = (gbig[n], d_, m_, v_)
    for n in SMALL:
        two_d = (-1, w[n].shape[-1])
        g_ = gsmall[n].reshape(two_d)
        d_, m_, v_ = _adamw("adamw_" + n, w[n].reshape(two_d), g_, m[n].reshape(two_d), v[n].reshape(two_d))
        outs[n] = (g_, d_, m_, v_)

    res = [loss_tot.reshape(()), dx.reshape(x.shape)]
    for k in range(4):
        res += [outs[n][k].reshape(w[n].shape) for n in W_NAMES]
    return tuple(res)
```

```python
import functools
import math

import jax
import jax.numpy as jnp
import numpy as np
from jax import lax
from jax.experimental import pallas as pl
from jax.experimental.pallas import tpu as pltpu

F32 = jnp.float32
BF16 = jnp.bfloat16
_CDT = BF16
_GDT = BF16

D_MODEL = 2048
SWA_HEADS, SWA_KV_HEADS, SWA_HEAD_DIM, SWA_BLOCK = 16, 2, 64, 128
SWA_GRP = SWA_HEADS // SWA_KV_HEADS
REL_BUCKETS, REL_MAX_DIST = 32, 128
GDN_HEADS, GDN_HEAD_DIM, GDN_CONV, GDN_CHUNK = 8, 128, 4, 64
MEM_HEADS, MEM_HEAD_DIM = 4, 128
D_FF, D_FF_PAD, FFN_CONV = 5504, 5632, 3
SWA_Q, SWA_KV, GDN_W, MEM_W = 1024, 128, 1024, 512
IN_WIDTHS = (SWA_Q, SWA_KV, SWA_KV, GDN_W, GDN_W, GDN_W, GDN_W, GDN_HEADS, GDN_HEADS, D_MODEL, D_MODEL)
NORM_EPS = 1e-5
ALPHA = 2.0 ** 0.25
NEG_INF = -1e30
ADAM_LR, ADAM_B1, ADAM_B2, ADAM_EPS, ADAM_WD, ADAM_STEP = 0.001, 0.9, 0.999, 1e-08, 0.01, 10
LANES, SUBLANES = 128, 8
VMEM_LIMIT = 56 * 1024 * 1024

P_GS, P_GG, P_GQKV, P_GZ, P_SQ, P_SK, P_SV, P_BA, P_USED, P_END = 0, 2048, 4096, 7168, 8192, 9216, 9344, 9472, 9600, 9728


def _tile(dim, pref, align=LANES):
    if dim <= pref:
        return dim
    t = (pref // align) * align
    while t >= align:
        if dim % t == 0:
            return t
        t -= align
    return dim


_DIMS = {"nn": (((1,), (0,)), ((), ())), "nt": (((1,), (1,)), ((), ())), "tn": (((0,), (0,)), ((), ()))}
_BDIMS = {"nn": (((2,), (1,)), ((0,), (0,))), "nt": (((2,), (2,)), ((0,), (0,))), "tn": (((1,), (1,)), ((0,), (0,)))}


def _raw_dot(a, b, form, hi):
    dims = (_BDIMS if a.ndim == 3 else _DIMS)[form]
    if hi == "x3":
        a, b = a.astype(F32), b.astype(F32)
        ah, bh = a.astype(BF16), b.astype(BF16)
        al, bl = (a - ah.astype(F32)).astype(BF16), (b - bh.astype(F32)).astype(BF16)
        d = lambda p, q: lax.dot_general(p, q, dims, preferred_element_type=F32)
        if form == "tn":
            return d(ah, bh) + (d(ah, bl) + d(al, bh))
        m = a.shape[-2]
        both = d(jnp.concatenate([ah, al], axis=-2), bh)
        return both[..., :m, :] + (d(ah, bl) + both[..., m:, :])
    if hi:
        return lax.dot_general(a.astype(F32), b.astype(F32), dims, precision=lax.Precision.HIGHEST, preferred_element_type=F32)
    return lax.dot_general(a.astype(_CDT), b.astype(_CDT), dims, preferred_element_type=F32)


@functools.partial(jax.custom_vjp, nondiff_argnums=(2, 3))
def _dot(a, b, form, hi=False):
    return _raw_dot(a, b, form, hi)


def _dot_fwd(a, b, form, hi):
    return _raw_dot(a, b, form, hi), (a, b)


def _dot_bwd(form, hi, res, g):
    a, b = res
    if form == "nn":
        da, db = _raw_dot(g, b, "nt", hi), _raw_dot(a, g, "tn", hi)
    elif form == "nt":
        da, db = _raw_dot(g, b, "nn", hi), _raw_dot(g, a, "tn", hi)
    else:
        da, db = _raw_dot(b, g, "nt", hi), _raw_dot(a, g, "nn", hi)
    return da.astype(a.dtype), db.astype(b.dtype)


_dot.defvjp(_dot_fwd, _dot_bwd)


@functools.partial(jax.custom_vjp, nondiff_argnums=(2,))
def _shift_halo(prev, cur, d):
    assert prev.shape[0] == SUBLANES
    return pltpu.roll(jnp.concatenate([prev, cur], axis=0), d, 0)[SUBLANES:]


def _shift_halo_fwd(prev, cur, d):
    return _shift_halo(prev, cur, d), None


def _shift_halo_bwd(d, _, g):
    nh = SUBLANES
    ext = jnp.concatenate([jnp.zeros((nh, g.shape[1]), g.dtype), g], axis=0)
    r = pltpu.roll(ext, ext.shape[0] - d, 0)
    return r[:nh], r[nh:]


_shift_halo.defvjp(_shift_halo_fwd, _shift_halo_bwd)


@jax.custom_vjp
def _recip(x):
    return 1.0 / x


def _recip_fwd(x):
    r = 1.0 / x
    return r, r


def _recip_bwd(r, g):
    return (-g * r * r,)


_recip.defvjp(_recip_fwd, _recip_bwd)


def _sigmoid(x):
    return _recip(1.0 + jnp.exp(-x))


def _silu(x):
    return x * _sigmoid(x)


def _softplus(x):
    return jnp.maximum(x, 0.0) + jnp.log(1.0 + jnp.exp(-jnp.abs(x)))


def _iota(shape, axis):
    return lax.broadcasted_iota(jnp.int32, shape, axis)


def _cparams(sem, **kw):
    return pltpu.CompilerParams(dimension_semantics=sem, vmem_limit_bytes=VMEM_LIMIT, **kw)


class _ride:
    def __init__(self, rider, n_in, n_out, n_scr):
        self.rider = rider
        self.ins = rider.ins if rider else []
        n_rin = len(self.ins)
        self.out_shapes = rider.out_shapes if rider else []
        n_rout = len(self.out_shapes)
        self.in_specs, self.out_specs = [_HBM] * n_rin, [_HBM] * n_rout
        self.scratch = rider.sems() if rider else []
        self.o0 = n_in + n_rin
        self.s0 = self.o0 + n_out + n_rout
        self._rin = slice(n_in, n_in + n_rin)
        self._rout = slice(self.o0 + n_out, self.s0)
        self._sem = self.s0 + n_scr

    def _args(self, refs):
        return refs[self._rin], refs[self._rout], refs[self._sem], refs[self._sem + 1]

    def at_start(self, refs, cond):
        if self.rider:
            pl.when(cond)(lambda: self.rider.start(*self._args(refs)))

    def at_end(self, refs, cond):
        if self.rider:
            pl.when(cond)(lambda: self.rider.finish(*self._args(refs)))


def _mm(name, a, b, form, out_dtype=F32, add=None, add_scale=1.0, hi=False, tm=1024, tn=1024, tk=2816, rider=None, b_k0=None):
    if form == "nn":
        (M, K), (K2, N) = a.shape, b.shape
    elif form == "nt":
        (M, K), (N, K2) = a.shape, b.shape
        K2 = K if b_k0 is not None else K2
    else:
        (K, M), (K2, N) = a.shape, b.shape
    assert K == K2, (name, a.shape, b.shape, form)
    tm, tn, tk = _tile(M, tm), _tile(N, tn), _tile(K, tk)
    nk = K // tk
    k0 = 0 if b_k0 is None else b_k0 // tk
    assert b_k0 is None or (form == "nt" and b_k0 % tk == 0)
    a_spec = pl.BlockSpec((tk, tm), lambda i, j, k: (k, i)) if form == "tn" else pl.BlockSpec((tm, tk), lambda i, j, k: (i, k))
    b_spec = pl.BlockSpec((tn, tk), lambda i, j, k: (j, k + k0)) if form == "nt" else pl.BlockSpec((tk, tn), lambda i, j, k: (k, j))
    o_spec = pl.BlockSpec((tm, tn), lambda i, j, k: (i, j))
    has_add = add is not None

    def finish(r, c_ref, o_ref):
        if has_add:
            r = r + add_scale * c_ref[...].astype(F32)
        o_ref[...] = r.astype(out_dtype)

    n_own = 3 if has_add else 2
    grid = (M // tm, N // tn, nk)
    rd = _ride(rider, n_own, 1, 1 if nk > 1 else 0)

    def body(*refs):
        a_ref, b_ref = refs[:2]
        c_ref = refs[2] if has_add else None
        o_ref = refs[rd.o0]
        pid = [pl.program_id(d) for d in range(3)]
        rd.at_start(refs, (pid[0] == 0) & (pid[1] == 0) & (pid[2] == 0))
        if nk == 1:
            finish(_raw_dot(a_ref[...], b_ref[...], form, hi), c_ref, o_ref)
        else:
            acc = refs[rd.s0]

            @pl.when(pid[2] == 0)
            def _():
                acc[...] = jnp.zeros_like(acc)

            acc[...] += _raw_dot(a_ref[...], b_ref[...], form, hi)

            @pl.when(pid[2] == nk - 1)
            def _():
                finish(acc[...], c_ref, o_ref)
        rd.at_end(refs, (pid[0] == grid[0] - 1) & (pid[1] == grid[1] - 1) & (pid[2] == nk - 1))

    ins = [a, b] + ([add] if has_add else [])
    specs = [a_spec, b_spec] + ([o_spec] if has_add else [])
    res = pl.pallas_call(
        body, name=name, grid=grid, in_specs=specs + rd.in_specs, out_specs=[o_spec] + rd.out_specs,
        out_shape=[jax.ShapeDtypeStruct((M, N), out_dtype)] + rd.out_shapes,
        scratch_shapes=([pltpu.VMEM((tm, tn), F32)] if nk > 1 else []) + rd.scratch,
        compiler_params=_cparams(("arbitrary",) * 3 if rider else ("parallel", "parallel", "arbitrary")),
    )(*ins, *rd.ins)
    return (res[0], res[1:]) if rider else res[0]


class Row:
    def __init__(self, arr, blk, imap, hblk=None, hmap=None, gshape=None, gmap=None, gdt=(F32,)):
        self.arr, self.blk, self.imap, self.hblk, self.hmap, self.gshape, self.gmap = arr, blk, imap, hblk, hmap, gshape, gmap
        self.gdt = gdt


class Par:
    def __init__(self, arr, blk=None, imap=None, gshape=None, gmap=None):
        self.arr = arr
        self.blk = tuple(arr.shape) if blk is None else blk
        nd = len(self.blk)
        self.imap = (lambda j: (0,) * nd) if imap is None else imap
        self.gshape, self.gmap = gshape, gmap


class Out:
    def __init__(self, shape, dtype, blk, imap):
        self.shape, self.dtype, self.blk, self.imap = shape, dtype, blk, imap


def _rows_of(blk):
    return [d for d in blk if d is not None][0]


def _rowmap(name, fn, ncol, nblk, rows, pars, outs, accs=()):
    in_specs, ins = [], []
    for r in rows:
        ins.append(r.arr)
        in_specs.append(pl.BlockSpec(r.blk, r.imap))
        if r.hblk is not None:
            ins.append(r.arr)
            in_specs.append(pl.BlockSpec(r.hblk, r.hmap))
    for p in pars:
        ins.append(p.arr)
        in_specs.append(pl.BlockSpec(p.blk, (lambda im: (lambda j, n: im(j)))(p.imap)))
    out_specs = [pl.BlockSpec(o.blk, o.imap) for o in outs]
    out_shape = [jax.ShapeDtypeStruct(o.shape, o.dtype) for o in outs]
    for a in accs:
        out_specs.append(pl.BlockSpec(a, (lambda nd: (lambda j, n: (0,) * nd))(len(a))))
        out_shape.append(jax.ShapeDtypeStruct(a, F32))
    n_in = len(ins)

    def body(*refs):
        j, n = pl.program_id(0), pl.program_id(1)
        it = iter(refs[:n_in])
        rvals = []
        for r in rows:
            cur = next(it)[...]
            rvals.append((next(it)[...], cur) if r.hblk is not None else cur)
        pvals = [next(it)[...] for _ in pars]
        o_refs = refs[n_in:n_in + len(outs)]
        a_refs = refs[n_in + len(outs):]
        ovals, avals = fn(j, n == 0, rvals, pvals)
        for ref, v in zip(o_refs, ovals):
            ref[...] = v.astype(ref.dtype)
        if accs:
            @pl.when((j == 0) & (n == 0))
            def _():
                for ref in a_refs:
                    ref[...] = jnp.zeros_like(ref)
            for ref, v in zip(a_refs, avals):
                ref[...] += v

    res = pl.pallas_call(
        body, name=name, grid=(ncol, nblk), in_specs=in_specs, out_specs=out_specs, out_shape=out_shape,
        compiler_params=_cparams(("arbitrary", "arbitrary")),
    )(*ins)
    return res


def _rowmap_bwd(name, fn, ncol, nblk, rows, pars, cts):
    rev = lambda im: (lambda j, s: im(j, nblk - 1 - s))
    in_specs, ins = [], []
    for r in rows:
        ins.append(r.arr)
        in_specs.append(pl.BlockSpec(r.blk, rev(r.imap)))
        if r.hblk is not None:
            ins.append(r.arr)
            in_specs.append(pl.BlockSpec(r.hblk, rev(r.hmap)))
    for p in pars:
        ins.append(p.arr)
        in_specs.append(pl.BlockSpec(p.blk, (lambda im: (lambda j, s: im(j)))(p.imap)))
    for c in cts:
        ins.append(c.arr)
        in_specs.append(pl.BlockSpec(c.blk, rev(c.imap)))
    n_in = len(ins)
    drows = [i for i, r in enumerate(rows) if r.gshape is not None]
    dpars = [i for i, p in enumerate(pars) if p.gshape is not None]
    out_specs, out_shape, scratch = [], [], []
    for i in drows:
        r = rows[i]
        for dt in r.gdt:
            out_specs.append(pl.BlockSpec(r.blk, rev(r.gmap)))
            out_shape.append(jax.ShapeDtypeStruct(r.gshape, dt))
        if r.hblk is not None:
            scratch.append(pltpu.VMEM(tuple(d for d in r.hblk if d is not None), F32))
    n_drow_out = len(out_specs)
    for i in dpars:
        p = pars[i]
        out_specs.append(pl.BlockSpec(p.blk, (lambda im: (lambda j, s: im(j)))(p.gmap)))
        out_shape.append(jax.ShapeDtypeStruct(p.gshape, F32))

    def body(*refs):
        j, s = pl.program_id(0), pl.program_id(1)
        first = s == nblk - 1
        it = iter(refs[:n_in])
        rvals = []
        for r in rows:
            cur = next(it)[...]
            rvals.append((next(it)[...], cur) if r.hblk is not None else cur)
        pvals = [next(it)[...] for _ in pars]
        cvals = [next(it)[...].astype(F32) for _ in cts]
        g_refs = iter(refs[n_in:n_in + n_drow_out])
        p_refs = refs[n_in + n_drow_out:n_in + n_drow_out + len(dpars)]
        carries = iter(refs[n_in + n_drow_out + len(dpars):])

        def f(dr, dp):
            rv, pv = list(rvals), list(pvals)
            for i, v in zip(drows, dr):
                rv[i] = v
            for i, v in zip(dpars, dp):
                pv[i] = v
            return fn(j, first, rv, pv)

        _, vjp = jax.vjp(f, [rvals[i] for i in drows], [pvals[i] for i in dpars])
        g_r, g_p = vjp(cvals)
        for i, g in zip(drows, g_r):
            r = rows[i]
            if r.hblk is None:
                for _ in r.gdt:
                    ref = next(g_refs)
                    ref[...] = g.astype(ref.dtype)
            else:
                g_prev, g_cur = g
                carry = next(carries)
                nr, nh = g_cur.shape[-2], g_prev.shape[-2]
                tail = g_cur[..., nr - nh:nr, :] + jnp.where(s > 0, carry[...], 0.0)
                for _ in r.gdt:
                    ref = next(g_refs)
                    if nr > nh:
                        ref[..., 0:nr - nh, :] = g_cur[..., 0:nr - nh, :].astype(ref.dtype)
                    ref[..., nr - nh:nr, :] = tail.astype(ref.dtype)
                carry[...] = g_prev
        for ref, g in zip(p_refs, g_p):
            @pl.when(s == 0)
            def _():
                ref[...] = jnp.zeros_like(ref)
            ref[...] += g

    return pl.pallas_call(
        body, name=name, grid=(ncol, nblk), in_specs=in_specs, out_specs=out_specs, out_shape=out_shape,
        scratch_shapes=scratch, compiler_params=_cparams(("arbitrary", "arbitrary")),
    )(*ins)


def _rowspec(arr, tb, cw, c0, cstep=1, halo=0, grad=False, ncol=1, gdt=(F32,)):
    T = arr.shape[0]
    imap = lambda j, n: (n, c0 + cstep * j)
    hblk = hmap = None
    if halo:
        q = tb // halo
        hblk, hmap = (halo, cw), (lambda j, n: (jnp.maximum(n * q - 1, 0), c0 + cstep * j))
    gshape = (T, cw * (ncol if cstep else 1)) if grad else None
    gmap = (lambda j, n: (n, cstep * j)) if grad else None
    return Row(arr, (tb, cw), imap, hblk, hmap, gshape, gmap, gdt)


def _parspec(arr, cw=None, c0=0, grad=False, ncol=1):
    if cw is None:
        return Par(arr, gshape=tuple(arr.shape) if grad else None,
                   gmap=(lambda nd: (lambda j: (0,) * nd))(arr.ndim) if grad else None)
    r = arr.shape[0]
    return Par(arr, (r, cw), lambda j: (0, c0 + j), (r, cw * ncol) if grad else None, (lambda j: (0, j)) if grad else None)


def _ln(r, g, b):
    mu = jnp.mean(r, axis=-1, keepdims=True)
    xc = r - mu
    var = jnp.mean(xc * xc, axis=-1, keepdims=True)
    return xc * lax.rsqrt(var + NORM_EPS) * g + b


def _ln_fn(j, first, rv, pv):
    return [_ln(rv[0], pv[0], pv[1])]


def _ln_fwd_fn(j, first, rv, pv):
    y = _ln(rv[0], pv[0], pv[1])
    return [y, y], []


def _loss_fn(j, first, rv, pv):
    r3, tgt = rv
    g, b = pv
    y, vjp = jax.vjp(_ln, r3, g, b)
    diff = y - tgt
    part = 0.5 * jnp.sum(diff * diff) / D_MODEL
    dr, dg, db = vjp(diff * (1.0 / D_MODEL))
    return [dr, dr], [jnp.full((SUBLANES, LANES), part, F32), dg, db]


def _mix_fn(j, first, rv, pv):
    gs, gg, ys, yg = rv
    return [_sigmoid(gs) * ys + _sigmoid(gg) * yg]


def _row_pick(x, i):
    ax = x.ndim - 2
    return jnp.sum(jnp.where(_iota(x.shape, ax) == i, x, 0.0), axis=ax, keepdims=True)


def _causal_conv(prev, cur, w, first):
    width = w.shape[0]
    prev = jnp.where(first, 0.0, prev)
    y = cur * _row_pick(w, width - 1)
    for d in range(1, width):
        y = y + _shift_halo(prev, cur, d) * _row_pick(w, width - 1 - d)
    return y


def _ffn_act_fn(j, first, rv, pv):
    (pg, cg), (pu, cu) = rv
    wg, wu, bg, bu = pv
    hg = _causal_conv(pg, cg, wg, first) + bg
    hu = _causal_conv(pu, cu, wu, first) + bu
    return [_silu(hg) * hu]


def _gdn_pre_fn(j, first, rv, pv):
    (prev, cur), = rv
    w, = pv
    t = _silu(_causal_conv(prev, cur, w, first))
    tn = t * lax.rsqrt(jnp.sum(t * t, axis=-1, keepdims=True) + 1e-6)
    return [jnp.where(j < 2 * GDN_HEADS, tn, t)]


def _gdn_gate_fn(j, first, rv, pv):
    gba, = rv
    alog, dtb, eb, eg = pv
    tb = gba.shape[0]
    beta = _sigmoid(gba)
    g = -jnp.exp(alog) * _softplus(gba + dtb)
    ri, ci = _iota((tb, tb), 0), _iota((tb, tb), 1)
    tril = jnp.where((ri // GDN_CHUNK == ci // GDN_CHUNK) & (ci <= ri), 1.0, 0.0)
    gc = _dot(tril, g, "nn", True)
    return [_dot(beta, eb, "nn", True), _dot(gc, eg, "nn", True)]


def _swa_fn(j, first, rv, pv):
    q, (kp, kc), (vp, vc) = rv
    bp, bc, sk = pv
    sp = _dot(q, kp, "nt") * (SWA_HEAD_DIM ** -0.5) + bp
    sc = _dot(q, kc, "nt") * (SWA_HEAD_DIM ** -0.5) + bc
    qi = _iota(sp.shape, sp.ndim - 2) % SWA_BLOCK
    kj = _iota(sp.shape, sp.ndim - 1)
    sp = jnp.where((kj > qi) & jnp.logical_not(first), sp, NEG_INF)
    sc = jnp.where(kj <= qi, sc, NEG_INF)
    m = jnp.maximum(jnp.maximum(jnp.max(sp, axis=-1, keepdims=True), jnp.max(sc, axis=-1, keepdims=True)), sk)
    m = lax.stop_gradient(m)
    ep, ec, es = jnp.exp(sp - m), jnp.exp(sc - m), jnp.exp(sk - m)
    inv = 1.0 / (jnp.sum(ep, axis=-1, keepdims=True) + jnp.sum(ec, axis=-1, keepdims=True) + es)
    vp = jnp.where(first, 0.0, vp)
    return [_dot(ep * inv, vp, "nn") + _dot(ec * inv, vc, "nn")]


def _memattn_fn(j, first, rv, pv):
    q, = rv
    k, v = pv
    s = _dot(q, k, "nt") * (MEM_HEAD_DIM ** -0.5)
    m = lax.stop_gradient(jnp.max(s, axis=-1, keepdims=True))
    e = jnp.exp(s - m)
    p = e * (1.0 / jnp.sum(e, axis=-1, keepdims=True))
    return [_dot(p, v, "nn")]


SOLVE_PREC = "x3"


@jax.custom_vjp
def _unit_lower_inv(a):
    c = a.shape[-1]
    eye = _iota((1, c, c), 1) == _iota((1, c, c), 2)
    tinv = jnp.where(eye, 1.0, 0.0) - a
    x = _raw_dot(a, a, "nn", SOLVE_PREC)
    for i in range(5):
        tinv = tinv + _raw_dot(tinv, x, "nn", SOLVE_PREC)
        if i < 4:
            x = _raw_dot(x, x, "nn", SOLVE_PREC)
    return tinv


def _unit_lower_inv_fwd(a):
    t = _unit_lower_inv(a)
    return t, t


def _unit_lower_inv_bwd(t, g):
    return (-_raw_dot(_raw_dot(t, g, "tn", SOLVE_PREC), t, "nt", SOLVE_PREC),)


_unit_lower_inv.defvjp(_unit_lower_inv_fwd, _unit_lower_inv_bwd)


@jax.custom_vjp
def _known_inv(a, t):
    return t


def _known_inv_fwd(a, t):
    return t, t


def _known_inv_bwd(t, g):
    return _unit_lower_inv_bwd(t, g) + (jnp.zeros_like(t),)


_known_inv.defvjp(_known_inv_fwd, _known_inv_bwd)


def _gdn_heads(q, k, v, bx, gx, g64, z, nw, S, tinv=None, keep_tinv=False):
    c = GDN_CHUNK
    q = q * (GDN_HEAD_DIM ** -0.5)
    kb, vb = k * bx, v * bx
    ri, ci = _iota((1, c, c), 1), _iota((1, c, c), 2)
    tril, strict, eye = ci <= ri, ci < ri, ci == ri
    grow = jnp.sum(jnp.where(eye, g64, 0.0), axis=1, keepdims=True)
    decay = jnp.where(tril, jnp.exp(jnp.where(tril, g64 - grow, 0.0)), 0.0)
    a = jnp.where(strict, _dot(kb, k, "nt") * decay, 0.0)
    tinv = _unit_lower_inv(a) if tinv is None else _known_inv(a, tinv)
    eg = jnp.exp(gx)
    u = _dot(tinv, vb, "nn", SOLVE_PREC)
    w = _dot(tinv, kb * eg, "nn", SOLVE_PREC)
    ai = jnp.where(tril, _dot(q, k, "nt") * decay, 0.0)
    glast = _row_pick(gx, c - 1)
    v_new = u - _dot(w, S, "nn")
    o = _dot(q * eg, S, "nn") + _dot(ai, v_new, "nn")
    s_new = S * jnp.exp(glast) + _dot(k * jnp.exp(glast - gx), v_new, "tn")
    o = o * lax.rsqrt(jnp.mean(o * o, axis=-1, keepdims=True) + 1e-6) * nw
    return (o * _silu(z), s_new, tinv) if keep_tinv else (o * _silu(z), s_new)


GDN_STEP_CHUNKS = 2


def _head_major(ref, off, width=GDN_HEAD_DIM, ci=0):
    r = slice(ci * GDN_CHUNK, (ci + 1) * GDN_CHUNK)
    return jnp.stack([ref[r, off + h * GDN_HEAD_DIM:off + h * GDN_HEAD_DIM + width] for h in range(GDN_HEADS)])


def _gdn_chunks_fwd(qkv, bx, gx, proj, nw, rider=None):
    T = qkv.shape[0]
    cps = GDN_STEP_CHUNKS
    nc, c, hd, nh = T // (cps * GDN_CHUNK), GDN_CHUNK, GDN_HEAD_DIM, GDN_HEADS
    rd = _ride(rider, 5, 3, 1)

    def body(*refs):
        qkv_ref, bx_ref, gx_ref, z_ref, nw_ref = refs[:5]
        y_ref, st_ref, ti_ref = refs[rd.o0:rd.o0 + 3]
        S = refs[rd.s0]
        rd.at_start(refs, pl.program_id(0) == 0)

        @pl.when(pl.program_id(0) == 0)
        def _():
            S[...] = jnp.zeros_like(S)

        s_new = S[...]
        for ci in range(cps):
            st_ref[ci] = s_new
            y, s_new, ti = _gdn_heads(_head_major(qkv_ref, 0, ci=ci), _head_major(qkv_ref, GDN_W, ci=ci),
                                      _head_major(qkv_ref, 2 * GDN_W, ci=ci), _head_major(bx_ref, 0, ci=ci),
                                      _head_major(gx_ref, 0, ci=ci), _head_major(gx_ref, 0, c, ci), _head_major(z_ref, 0, ci=ci),
                                      nw_ref[...], s_new, keep_tinv=True)
            ti_ref[ci] = ti
            for h in range(nh):
                y_ref[ci * c:(ci + 1) * c, h * hd:(h + 1) * hd] = y[h].astype(y_ref.dtype)
        S[...] = s_new
        rd.at_end(refs, pl.program_id(0) == nc - 1)

    row = lambda w, cb: pl.BlockSpec((cps * c, w), lambda n: (n, cb))
    res = pl.pallas_call(
        body, name="gdn_chunks_fwd", grid=(nc,),
        in_specs=[row(3 * GDN_W, 0), row(GDN_W, 0), row(GDN_W, 0), row(GDN_W, P_GZ // GDN_W),
                  pl.BlockSpec((1, hd), lambda n: (0, 0))] + rd.in_specs,
        out_specs=[row(GDN_W, 0), pl.BlockSpec((cps, nh, hd, hd), lambda n: (n, 0, 0, 0)),
                   pl.BlockSpec((cps, nh, c, c), lambda n: (n, 0, 0, 0))] + rd.out_specs,
        out_shape=[jax.ShapeDtypeStruct((T, GDN_W), BF16), jax.ShapeDtypeStruct((nc * cps, nh, hd, hd), F32),
                   jax.ShapeDtypeStruct((nc * cps, nh, c, c), F32)] + rd.out_shapes,
        scratch_shapes=[pltpu.VMEM((nh, hd, hd), F32)] + rd.scratch,
        compiler_params=_cparams(("arbitrary",)),
    )(qkv, bx, gx, proj, nw, *rd.ins)
    return res[0], (res[1], res[2]), res[3:]


def _gdn_chunks_bwd(qkv, bx, gx, proj, nw, saved, dy, rider=None):
    states, tinvs = saved
    T = qkv.shape[0]
    cps = GDN_STEP_CHUNKS
    nc, c, hd, nh = T // (cps * GDN_CHUNK), GDN_CHUNK, GDN_HEAD_DIM, GDN_HEADS
    rd = _ride(rider, 8, 5, 1)

    def body(*refs):
        qkv_ref, bx_ref, gx_ref, z_ref, nw_ref, st_ref, ti_ref, dy_ref = refs[:8]
        dqkv_ref, dbx_ref, dgx_ref, dz_ref, dnw_ref = refs[rd.o0:rd.o0 + 5]
        dS = refs[rd.s0]
        rd.at_start(refs, pl.program_id(0) == 0)

        @pl.when(pl.program_id(0) == 0)
        def _():
            dS[...] = jnp.zeros_like(dS)
            dnw_ref[...] = jnp.zeros_like(dnw_ref)

        dsp = dS[...]
        for ci in reversed(range(cps)):
            r = slice(ci * c, (ci + 1) * c)
            args = (_head_major(qkv_ref, 0, ci=ci), _head_major(qkv_ref, GDN_W, ci=ci), _head_major(qkv_ref, 2 * GDN_W, ci=ci),
                    _head_major(bx_ref, 0, ci=ci), _head_major(gx_ref, 0, ci=ci), _head_major(gx_ref, 0, c, ci),
                    _head_major(z_ref, 0, ci=ci), nw_ref[...], st_ref[ci])
            _, vjp = jax.vjp(functools.partial(_gdn_heads, tinv=ti_ref[ci]), *args)
            dq, dk, dv, dbx, dgx, dg64, dz, dnw, dsp = vjp((_head_major(dy_ref, 0, ci=ci), dsp))
            for h in range(nh):
                sl = slice(h * hd, (h + 1) * hd)
                dqkv_ref[r, sl] = dq[h].astype(dqkv_ref.dtype)
                dqkv_ref[r, GDN_W + h * hd:GDN_W + (h + 1) * hd] = dk[h].astype(dqkv_ref.dtype)
                dqkv_ref[r, 2 * GDN_W + h * hd:2 * GDN_W + (h + 1) * hd] = dv[h].astype(dqkv_ref.dtype)
                dbx_ref[r, sl] = dbx[h]
                dgx_ref[r, sl] = dgx[h]
                dgx_ref[r, h * hd:h * hd + c] += dg64[h]
                dz_ref[r, sl] = dz[h].astype(dz_ref.dtype)
            dnw_ref[...] += dnw
        dS[...] = dsp
        rd.at_end(refs, pl.program_id(0) == nc - 1)

    row = lambda w, cb: pl.BlockSpec((cps * c, w), lambda s: (nc - 1 - s, cb))
    res = pl.pallas_call(
        body, name="gdn_chunks_bwd", grid=(nc,),
        in_specs=[row(3 * GDN_W, 0), row(GDN_W, 0), row(GDN_W, 0), row(GDN_W, P_GZ // GDN_W), pl.BlockSpec((1, hd), lambda s: (0, 0)),
                  pl.BlockSpec((cps, nh, hd, hd), lambda s: (nc - 1 - s, 0, 0, 0)),
                  pl.BlockSpec((cps, nh, c, c), lambda s: (nc - 1 - s, 0, 0, 0)), row(GDN_W, 0)] + rd.in_specs,
        out_specs=[row(3 * GDN_W, 0), row(GDN_W, 0), row(GDN_W, 0), row(GDN_W, 0),
                   pl.BlockSpec((1, hd), lambda s: (0, 0))] + rd.out_specs,
        out_shape=[jax.ShapeDtypeStruct((T, 3 * GDN_W), F32), jax.ShapeDtypeStruct((T, GDN_W), F32),
                   jax.ShapeDtypeStruct((T, GDN_W), F32), jax.ShapeDtypeStruct((T, GDN_W), _CDT),
                   jax.ShapeDtypeStruct((1, hd), F32)] + rd.out_shapes,
        scratch_shapes=[pltpu.VMEM((nh, hd, hd), F32)] + rd.scratch,
        compiler_params=_cparams(("arbitrary",)),
    )(qkv, bx, gx, proj, nw, states, tinvs, dy, *rd.ins)
    res = list(res)
    return res[:5] + [res[5:]]


ADAMW_BLOCK_ELEMS = 700_000


def _adamw(name, w, g, m, v):
    if w.ndim == 3:
        C, _, R = w.shape
        blk = (_tile(C, 768, 1), 1, _tile(R, 512))
        grid = (C // blk[0], R // blk[2])
        spec = pl.BlockSpec(blk, lambda i, j: (i, 0, j))
    else:
        R, C = w.shape
        tr = _tile(R, max(SUBLANES, ADAMW_BLOCK_ELEMS // C // SUBLANES * SUBLANES), SUBLANES)
        grid = (R // tr,)
        spec = pl.BlockSpec((tr, C), lambda i: (i, 0))

    def body(w_ref, g_ref, m_ref, v_ref, d_ref, m2_ref, v2_ref):
        g_ = g_ref[...]
        m2 = ADAM_B1 * m_ref[...] + (1.0 - ADAM_B1) * g_
        v2 = ADAM_B2 * v_ref[...] + (1.0 - ADAM_B2) * (g_ * g_)
        m_hat = m2 / (1.0 - ADAM_B1 ** ADAM_STEP)
        v_hat = v2 / (1.0 - ADAM_B2 ** ADAM_STEP)
        d_ref[...] = -ADAM_LR * (m_hat / (jnp.sqrt(v_hat) + ADAM_EPS) + ADAM_WD * w_ref[...])
        m2_ref[...] = m2
        v2_ref[...] = v2

    return pl.pallas_call(
        body, name=name, grid=grid, in_specs=[spec] * 4, out_specs=[spec] * 3,
        out_shape=[jax.ShapeDtypeStruct(w.shape, F32)] * 3, compiler_params=_cparams(("parallel",) * len(grid)),
    )(w, g, m, v)


def _addn(name, parts, out_dtype=F32):
    parts = [p if isinstance(p, tuple) else (p, None) for p in parts]
    a0, k0 = parts[0]
    R, C = a0.shape[-2:]
    tr = _tile(R, 256, 2 * SUBLANES)
    specs = []
    for a, k in parts:
        if k is None:
            specs.append(pl.BlockSpec((tr, C), lambda i: (i, 0)))
        else:
            specs.append(pl.BlockSpec((None, tr, C), (lambda kk: (lambda i: (kk, i, 0)))(k)))

    def body(*refs):
        acc = refs[0][...].astype(F32)
        for r in refs[1:-1]:
            acc = acc + r[...].astype(F32)
        refs[-1][...] = acc.astype(out_dtype)

    return pl.pallas_call(
        body, name=name, grid=(R // tr,), in_specs=specs, out_specs=pl.BlockSpec((tr, C), lambda i: (i, 0)),
        out_shape=jax.ShapeDtypeStruct((R, C), out_dtype), compiler_params=_cparams(("parallel",)),
    )(*[a for a, _ in parts])


MESH = pl.DeviceIdType.MESH
_HBM = pl.BlockSpec(memory_space=pltpu.HBM)


def _place():
    x, y, c = lax.axis_index("x"), lax.axis_index("y"), lax.axis_index("c")
    return x, y, c, [(1 - x, y), (x, 1 - y), (1 - x, 1 - y)]


class _Rider:
    def __init__(self, ins, out_shapes, nsem, start, finish):
        self.ins, self.out_shapes, self.nsem, self.start, self.finish = list(ins), list(out_shapes), nsem, start, finish

    def sems(self):
        return [pltpu.SemaphoreType.DMA((self.nsem,)), pltpu.SemaphoreType.DMA((self.nsem,))]


def _run_rider(name, rd):
    n_in, n_out = len(rd.ins), len(rd.out_shapes)

    def body(*refs):
        ins, outs, (send, recv) = refs[:n_in], refs[n_in:n_in + n_out], refs[n_in + n_out:]
        rd.start(ins, outs, send, recv)
        rd.finish(ins, outs, send, recv)

    return pl.pallas_call(body, name=name, in_specs=[_HBM] * n_in, out_specs=[_HBM] * n_out, out_shape=rd.out_shapes,
                          scratch_shapes=rd.sems())(*rd.ins)


def _gather_rider(ts):
    nt = len(ts)

    def half(t, hc):
        rh = ts[t].shape[0] // 2
        return pl.ds(pl.multiple_of(hc * rh, 16), rh)

    def rcopy(send, recv, t, k, src, dst, to):
        return pltpu.make_async_remote_copy(src_ref=src, dst_ref=dst, send_sem=send.at[6 * t + k], recv_sem=recv.at[6 * t + k],
                                            device_id=to, device_id_type=MESH)

    def first_hop(ins, outs, send, recv, t, r, px, py, c, me):
        return rcopy(send, recv, t, r, ins[t].at[half(t, c)], outs[t].at[me, half(t, c)], (px, py, c))

    def start(ins, outs, send, recv):
        x, y, c, rel = _place()
        for t in range(nt):
            for r, (px, py) in enumerate(rel):
                first_hop(ins, outs, send, recv, t, r, px, py, c, 2 * x + y).start()

    def finish(ins, outs, send, recv):
        x, y, c, rel = _place()
        sib = (x, y, 1 - c)
        passed = []
        for t in range(nt):
            for r, (px, py) in enumerate(rel):
                got = outs[t].at[2 * px + py, half(t, c)]
                rcopy(send, recv, t, r, got, got, (px, py, c)).wait_recv()
                fw = rcopy(send, recv, t, 3 + r, got, got, sib)
                fw.start()
                passed.append(fw)
        for t in range(nt):
            for r, (px, py) in enumerate(rel):
                got = outs[t].at[2 * px + py, half(t, 1 - c)]
                rcopy(send, recv, t, 3 + r, got, got, sib).wait_recv()
        for t in range(nt):
            for r, (px, py) in enumerate(rel):
                first_hop(ins, outs, send, recv, t, r, px, py, c, 2 * x + y).wait_send()
        for fw in passed:
            fw.wait_send()

    return _Rider(ts, [jax.ShapeDtypeStruct((4,) + tuple(t.shape), t.dtype) for t in ts], 6 * nt, start, finish)


def _scatter_rider(ps):
    nt = len(ps)

    def copy(ins, outs, send, recv, t, r, px, py, c):
        return pltpu.make_async_remote_copy(src_ref=ins[t].at[2 * px + py], dst_ref=outs[t].at[r], send_sem=send.at[3 * t + r],
                                            recv_sem=recv.at[3 * t + r], device_id=(px, py, c), device_id_type=MESH)

    def start(ins, outs, send, recv):
        x, y, c, rel = _place()
        for t in range(nt):
            for r, (px, py) in enumerate(rel):
                copy(ins, outs, send, recv, t, r, px, py, c).start()

    def finish(ins, outs, send, recv):
        x, y, c, rel = _place()
        for t in range(nt):
            for r, (px, py) in enumerate(rel):
                copy(ins, outs, send, recv, t, r, px, py, c).wait()

    return _Rider(ps, [jax.ShapeDtypeStruct((3,) + tuple(p.shape[1:]), p.dtype) for p in ps], 3 * nt, start, finish)


def _swap_rider(ts):
    nt = len(ts)

    def copy(ins, outs, send, recv, t):
        x, y, c, _ = _place()
        rh = ts[t].shape[1] // 2
        src = ins[t].at[:, pl.ds(pl.multiple_of((1 - c) * rh, 16), rh), :]
        return pltpu.make_async_remote_copy(src_ref=src, dst_ref=outs[t], send_sem=send.at[t], recv_sem=recv.at[t],
                                            device_id=(x, y, 1 - c), device_id_type=MESH)

    def start(ins, outs, send, recv):
        for t in range(nt):
            copy(ins, outs, send, recv, t).start()

    def finish(ins, outs, send, recv):
        for t in range(nt):
            copy(ins, outs, send, recv, t).wait()

    return _Rider(ts, [jax.ShapeDtypeStruct((4, t.shape[1] // 2, t.shape[2]), t.dtype) for t in ts], nt, start, finish)


def _pair_exchange(gs):
    nt = len(gs)

    def body(*refs):
        ins, outs = refs[:nt], refs[nt:2 * nt]
        send, recv = refs[2 * nt:]
        x, y, c, _ = _place()
        cps = []
        for t in range(nt):
            cp = pltpu.make_async_remote_copy(src_ref=ins[t], dst_ref=outs[t], send_sem=send.at[t], recv_sem=recv.at[t],
                                              device_id=(x, y, 1 - c), device_id_type=MESH)
            cp.start()
            cps.append(cp)
        for cp in cps:
            cp.wait()

    return pl.pallas_call(
        body, name="pair_exchange", in_specs=[_HBM] * nt, out_specs=[_HBM] * nt,
        out_shape=[jax.ShapeDtypeStruct(tuple(g.shape), g.dtype) for g in gs],
        scratch_shapes=[pltpu.SemaphoreType.DMA((nt,)), pltpu.SemaphoreType.DMA((nt,))],
    )(*gs)


def _allgather8(v):
    m, n = v.shape

    def body(x_ref, out_ref, send, recv, lsem):
        x, y, c, rel = _place()
        me, sib = (x, y, c), (x, y, 1 - c)

        def blk(px, py, pc):
            return out_ref.at[4 * px + 2 * py + pc]

        def copy(k, block, to, src=None):
            return pltpu.make_async_remote_copy(src_ref=blk(*block) if src is None else src, dst_ref=blk(*block), send_sem=send.at[k],
                                                recv_sem=recv.at[k], device_id=to, device_id_type=MESH)

        mine = pltpu.make_async_copy(x_ref, blk(*me), lsem)
        mine.start()
        first = [copy(0, me, sib, src=x_ref)] + [copy(1 + r, me, (*ch, c), src=x_ref) for r, ch in enumerate(rel)]
        for cp in first:
            cp.start()
        passed = [copy(4 + r, (*ch, c), sib) for r, ch in enumerate(rel)]
        for r, ch in enumerate(rel):
            copy(1 + r, (*ch, c), me).wait_recv()
            passed[r].start()
        copy(0, sib, me).wait_recv()
        for r, ch in enumerate(rel):
            copy(4 + r, (*ch, 1 - c), me).wait_recv()
        for cp in first + passed:
            cp.wait_send()
        mine.wait()

    return pl.pallas_call(
        body, name="allgather8", in_specs=[pl.BlockSpec(memory_space=pltpu.VMEM)], out_specs=pl.BlockSpec(memory_space=pltpu.VMEM),
        out_shape=jax.ShapeDtypeStruct((8, m, n), v.dtype),
        scratch_shapes=[pltpu.SemaphoreType.DMA((7,)), pltpu.SemaphoreType.DMA((7,)), pltpu.SemaphoreType.DMA],
    )(v)


def _t5_bucket(dist):
    max_exact = REL_BUCKETS // 2
    d = jnp.maximum(dist, 1).astype(F32)
    large = max_exact + (jnp.log(d / max_exact) / math.log(REL_MAX_DIST / max_exact) * (REL_BUCKETS - max_exact)).astype(jnp.int32)
    large = jnp.minimum(large, REL_BUCKETS - 1)
    return jnp.where(dist < max_exact, dist, large)


def _bias_onehot():
    qi = jnp.arange(SWA_BLOCK)[:, None]
    kj = jnp.arange(SWA_BLOCK)[None, :]
    dist = jnp.concatenate([(qi + SWA_BLOCK - kj).reshape(-1), (qi - kj).reshape(-1)])
    bucket = _t5_bucket(jnp.maximum(dist, 0))
    return (bucket[None, :] == jnp.arange(REL_BUCKETS)[:, None]).astype(F32)


def _head_spread():
    lane = jnp.arange(LANES)[:, None]
    head = jnp.arange(GDN_W)[None, :] // GDN_HEAD_DIM
    return (lane == head).astype(F32), (lane == head + GDN_HEADS).astype(F32)


def _lane16(v8):
    return jnp.pad(v8.astype(F32), (GDN_HEADS, LANES - 2 * GDN_HEADS)).reshape(1, LANES)


def _stack_heads(t, nb):
    return t.reshape(nb, SWA_BLOCK, SWA_KV_HEADS, SWA_GRP, SWA_HEAD_DIM).transpose(2, 0, 3, 1, 4).reshape(
        SWA_KV_HEADS, nb * SWA_GRP * SWA_BLOCK, SWA_HEAD_DIM)


def _unstack_heads(t, nb):
    return t.reshape(SWA_KV_HEADS, nb, SWA_GRP, SWA_BLOCK, SWA_HEAD_DIM).transpose(1, 3, 0, 2, 4).reshape(nb * SWA_BLOCK, SWA_Q)


def _kv_heads(t):
    return t.reshape(t.shape[0], SWA_KV_HEADS, SWA_HEAD_DIM).transpose(1, 0, 2)


def _swa_specs(qs, ks, vs, bp, bc, sk, grad, gdt=(F32,)):
    T = ks.shape[1]
    qr = SWA_GRP * SWA_BLOCK
    g = lambda a: tuple(a.shape) if grad else None
    nk = SWA_KV_HEADS
    m3 = lambda j, n: (0, n, 0)
    h3 = lambda j, n: (0, jnp.maximum(n - 1, 0), 0)
    p3 = lambda j: (0, 0, 0)
    rows = [Row(qs, (nk, qr, SWA_HEAD_DIM), m3, gshape=g(qs), gmap=m3, gdt=gdt),
            Row(ks, (nk, SWA_BLOCK, SWA_HEAD_DIM), m3, (nk, SWA_BLOCK, SWA_HEAD_DIM), h3, g(ks), m3, gdt),
            Row(vs, (nk, SWA_BLOCK, SWA_HEAD_DIM), m3, (nk, SWA_BLOCK, SWA_HEAD_DIM), h3, g(vs), m3, gdt)]
    pars = [Par(bp, (nk, qr, SWA_BLOCK), p3, g(bp), p3), Par(bc, (nk, qr, SWA_BLOCK), p3, g(bc), p3),
            Par(sk, (nk, qr, 1), p3, g(sk), p3)]
    return rows, pars, T // SWA_BLOCK


class _LocalWeights:
    def __init__(self, W):
        self.W = W

    def w1(self):
        return self.W

    def rider_a(self):
        return None

    def w2(self, got):
        return self.W

    def rider_b(self):
        return None

    def w3(self, got):
        return self.W

    def rider_g(self, G):
        return None

    def g_done(self, got):
        pass

    def rider_up(self, G):
        return None

    def up_done(self, got):
        pass

    def swap_up(self, G):
        return None

    def swap_up_done(self, got):
        pass

    def swap_rest(self, G):
        return None

    def swap_rest_done(self, got):
        pass

    def rider_last(self, G):
        return None

    def last_done(self, got):
        pass


def _fwd_bwd(x, mem, tgt, src):
    W = dict(src.w1())
    T = x.shape[0]
    nb = T // SWA_BLOCK
    tb = min(256, T)
    tbl = min(512, T)
    fwd = lambda f: (lambda *a: (f(*a), []))
    full = lambda cols, dt, t, cw: Out((T, cols), dt, (t, cw), lambda j, n: (n, j))

    xb = x.astype(_CDT)
    ra = src.rider_a()
    proj = _mm("proj", xb, W["in_p"], "nn", rider=ra)
    proj, got = proj if ra is not None else (proj, None)
    W.update(src.w2(got))

    onehot_t = _bias_onehot()
    bias_flat = _mm("swa_bias", W["rel_bias"].T, onehot_t, "nn", hi=True)
    half = SWA_BLOCK * SWA_BLOCK
    bp = bias_flat[:, :half].reshape(SWA_KV_HEADS, SWA_GRP * SWA_BLOCK, SWA_BLOCK)
    bc = bias_flat[:, half:].reshape(SWA_KV_HEADS, SWA_GRP * SWA_BLOCK, SWA_BLOCK)
    sk = jnp.broadcast_to(W["swa_sinks"].reshape(SWA_KV_HEADS, SWA_GRP, 1, 1), (SWA_KV_HEADS, SWA_GRP, SWA_BLOCK, 1)).reshape(
        SWA_KV_HEADS, SWA_GRP * SWA_BLOCK, 1)
    qs = _stack_heads(proj[:, P_SQ:P_SQ + SWA_Q], nb)
    ks = _kv_heads(proj[:, P_SK:P_SK + SWA_KV])
    vs = _kv_heads(proj[:, P_SV:P_SV + SWA_KV])
    rows, pars, nblk = _swa_specs(qs, ks, vs, bp, bc, sk, False)
    o_s, = _rowmap("swa_fwd", fwd(_swa_fn), 1, nblk, rows, pars,
                   [Out(tuple(qs.shape), F32, (SWA_KV_HEADS, SWA_GRP * SWA_BLOCK, SWA_HEAD_DIM), lambda j, n: (0, n, 0))])
    o_swa = _unstack_heads(o_s, nb).astype(_CDT)

    ncq = 3 * GDN_W // LANES
    tbp = min(1024, T)
    pre_rows = lambda grad: [_rowspec(proj, tbp, LANES, P_GQKV // LANES, halo=SUBLANES, grad=grad, ncol=ncq, gdt=(_CDT,))]
    pre_pars = lambda grad: [_parspec(W["gdn_conv_w"], LANES, 0, grad=grad, ncol=ncq)]
    qkv_n, = _rowmap("gdn_pre_fwd", fwd(_gdn_pre_fn), ncq, T // tbp, pre_rows(False), pre_pars(False),
                     [full(3 * GDN_W, F32, tbp, LANES)])
    eb, eg = _head_spread()
    alog_row, dtb_row = _lane16(W["gdn_a_log"]), _lane16(W["gdn_dt_bias"])
    gate_rows = lambda grad: [_rowspec(proj, tbl, LANES, P_BA // LANES, cstep=0, grad=grad, gdt=(_CDT,))]
    gate_pars = lambda grad: [_parspec(alog_row, grad=grad), _parspec(dtb_row, grad=grad), _parspec(eb), _parspec(eg)]
    bx, gx = _rowmap("gdn_gate_fwd", fwd(_gdn_gate_fn), 1, T // tbl, gate_rows(False), gate_pars(False),
                     [full(GDN_W, F32, tbl, GDN_W), full(GDN_W, F32, tbl, GDN_W)])
    nw = W["gdn_norm_w"].reshape(1, GDN_HEAD_DIM)
    o_gdn, states, got = _gdn_chunks_fwd(qkv_n, bx, gx, proj, nw, rider=src.rider_b())
    W.update(src.w3(got))

    ys = _mm("y_swa", o_swa, W["br_swa"], "nn")
    yg = _mm("y_gdn", o_gdn, W["br_gdn"], "nn")
    cwm = 512
    mix_rows = lambda grad: [_rowspec(proj, tb, cwm, P_GS // cwm, grad=grad, ncol=D_MODEL // cwm, gdt=(_CDT,)),
                             _rowspec(proj, tb, cwm, P_GG // cwm, grad=grad, ncol=D_MODEL // cwm, gdt=(_CDT,)),
                             _rowspec(ys, tb, cwm, 0, grad=grad, ncol=D_MODEL // cwm, gdt=(_CDT,)),
                             _rowspec(yg, tb, cwm, 0, grad=grad, ncol=D_MODEL // cwm, gdt=(_CDT,))]
    mixed, = _rowmap("mix_fwd", fwd(_mix_fn), D_MODEL // cwm, T // tb, mix_rows(False), [], [full(D_MODEL, _CDT, tb, cwm)])
    r1 = _mm("r1", mixed, W["mix_o"], "nn", add=x, add_scale=ALPHA)

    def ln_fwd(name, r, g, b):
        return _rowmap(name, _ln_fwd_fn, 1, T // tb, [_rowspec(r, tb, D_MODEL, 0)], [_parspec(g), _parspec(b)],
                       [full(D_MODEL, F32, tb, D_MODEL), full(D_MODEL, _CDT, tb, D_MODEL)])

    def ln_bwd(name, r, g, b, ct):
        return _rowmap_bwd(name, _ln_fn, 1, T // tb, [_rowspec(r, tb, D_MODEL, 0, grad=True, gdt=(F32, _CDT))],
                           [_parspec(g, grad=True), _parspec(b, grad=True)], [_rowspec(ct, tb, D_MODEL, 0)])

    g1, b1 = W["ln1_g"].reshape(1, -1), W["ln1_b"].reshape(1, -1)
    g2, b2 = W["ln2_g"].reshape(1, -1), W["ln2_b"].reshape(1, -1)
    g3, b3 = W["ln3_g"].reshape(1, -1), W["ln3_b"].reshape(1, -1)
    x1, x1b = ln_fwd("ln1_fwd", r1, g1, b1)

    qm = _mm("mem_q", x1b, W["mem_q"], "nn")
    kvm = _mm("mem_kv", mem, W["mem_kv"], "nn")
    ma_rows = lambda grad: [_rowspec(qm, tbl, MEM_HEAD_DIM, 0, grad=grad, ncol=MEM_HEADS, gdt=(_CDT,))]
    ma_pars = lambda grad: [_parspec(kvm, MEM_HEAD_DIM, 0, grad=grad, ncol=MEM_HEADS),
                            _parspec(kvm, MEM_HEAD_DIM, MEM_HEADS, grad=grad, ncol=MEM_HEADS)]
    om, = _rowmap("memattn_fwd", fwd(_memattn_fn), MEM_HEADS, T // tbl, ma_rows(False), ma_pars(False),
                  [full(MEM_W, _CDT, tbl, MEM_HEAD_DIM)])
    r2 = _mm("r2", om, W["mem_o"], "nn", add=x1, add_scale=ALPHA)
    x2, x2b = ln_fwd("ln2_fwd", r2, g2, b2)

    hcat = _mm("ffn_up", x2b, W["up_p"], "nn")
    cwf = 512
    ncf = D_FF_PAD // cwf
    cw_p, cb_p = W["ffn_conv_w_p"], W["ffn_conv_b_p"]
    tbf = min(512, T)
    ffn_rows = lambda grad: [_rowspec(hcat, tbf, cwf, 0, halo=SUBLANES, grad=grad, ncol=ncf, gdt=(_CDT,)),
                             _rowspec(hcat, tbf, cwf, ncf, halo=SUBLANES, grad=grad, ncol=ncf, gdt=(_CDT,))]
    ffn_pars = lambda grad: [_parspec(cw_p, cwf, 0, grad=grad, ncol=ncf), _parspec(cw_p, cwf, ncf, grad=grad, ncol=ncf),
                             _parspec(cb_p, cwf, 0, grad=grad, ncol=ncf), _parspec(cb_p, cwf, ncf, grad=grad, ncol=ncf)]
    act, = _rowmap("ffn_act_fwd", fwd(_ffn_act_fn), ncf, T // tbf, ffn_rows(False), ffn_pars(False), [full(D_FF_PAD, _CDT, tbf, cwf)])
    r3 = _mm("r3", act, W["down_p"], "nn", add=x2, add_scale=ALPHA)
    dr3, dr3b, lacc, dg3, db3 = _rowmap("ln3_loss", _loss_fn, 1, T // tb, [_rowspec(r3, tb, D_MODEL, 0), _rowspec(tgt, tb, D_MODEL, 0)],
                                        [_parspec(g3), _parspec(b3)], [full(D_MODEL, F32, tb, D_MODEL), full(D_MODEL, _CDT, tb, D_MODEL)],
                                  accs=[(SUBLANES, LANES), (1, D_MODEL), (1, D_MODEL)])
    loss = lacc[0, 0]

    G = {}
    G["down_p"] = _mm("dw_down", act, dr3b, "tn", out_dtype=_GDT)
    dact = _mm("d_act", dr3b, W["down_p"], "nt")
    dhg, dhu, dcwg, dcwu, dcbg, dcbu = _rowmap_bwd("ffn_act_bwd", _ffn_act_fn, ncf, T // tbf, ffn_rows(True), ffn_pars(True),
                                                   [_rowspec(dact, tbf, cwf, 0)])
    dx2 = _mm("dx2_gate", dhg, W["up_p"], "nt", add=dr3, add_scale=ALPHA, b_k0=0)
    dx2 = _mm("dx2_up", dhu, W["up_p"], "nt", add=dx2, b_k0=D_FF_PAD)
    G["up_p"] = jnp.concatenate([_mm("dw_gate", x2b, dhg, "tn", out_dtype=_GDT), _mm("dw_up", x2b, dhu, "tn", out_dtype=_GDT)], axis=1)
    G["ffn_conv_w"] = jnp.concatenate([dcwg[:, :D_FF], dcwu[:, :D_FF]], axis=1)
    G["ffn_conv_b"] = jnp.concatenate([dcbg[0, :D_FF], dcbu[0, :D_FF]])
    G["ln3_g"], G["ln3_b"] = dg3[0], db3[0]

    dr2, dr2b, dg2, db2 = ln_bwd("ln2_bwd", r2, g2, b2, dx2)
    G["ln2_g"], G["ln2_b"] = dg2[0], db2[0]
    G["mem_o"] = _mm("dw_mem_o", om, dr2b, "tn", out_dtype=_GDT)
    dom = _mm("d_om", dr2b, W["mem_o"], "nt", out_dtype=_CDT)
    dqm, dkm, dvm = _rowmap_bwd("memattn_bwd", _memattn_fn, MEM_HEADS, T // tbl, ma_rows(True), ma_pars(True),
                                [_rowspec(dom, tbl, MEM_HEAD_DIM, 0)])
    G["mem_kv"] = _mm("dw_mem_kv", mem.astype(_CDT), jnp.concatenate([dkm, dvm], axis=1).astype(_CDT), "tn", out_dtype=_GDT)
    G["mem_q"] = _mm("dw_mem_q", x1b, dqm, "tn", out_dtype=_GDT)
    dx1 = _mm("dx1", dqm, W["mem_q"], "nt", add=dr2, add_scale=ALPHA)

    dr1, dr1b, dg1, db1 = ln_bwd("ln1_bwd", r1, g1, b1, dx1)
    G["ln1_g"], G["ln1_b"] = dg1[0], db1[0]
    G["mix_o"] = _mm("dw_mix_o", mixed, dr1b, "tn", out_dtype=_GDT)
    rs = src.swap_up(G)
    dmixed = _mm("d_mixed", dr1b, W["mix_o"], "nt", rider=rs)
    if rs is not None:
        dmixed, got = dmixed
        src.swap_up_done(got)
    dgs, dgg, dys, dyg = _rowmap_bwd("mix_bwd", _mix_fn, D_MODEL // cwm, T // tb, mix_rows(True), [], [_rowspec(dmixed, tb, cwm, 0)])
    G["br_swa"] = _mm("dw_br_swa", o_swa, dys, "tn", out_dtype=_GDT)
    G["br_gdn"] = _mm("dw_br_gdn", o_gdn, dyg, "tn", out_dtype=_GDT)
    do_swa = _mm("d_o_swa", dys, W["br_swa"], "nt", out_dtype=_CDT)
    rs = src.swap_rest(G)
    do_gdn = _mm("d_o_gdn", dyg, W["br_gdn"], "nt", rider=rs)
    if rs is not None:
        do_gdn, got = do_gdn
        src.swap_rest_done(got)

    rows, pars, nblk = _swa_specs(qs, ks, vs, bp, bc, sk, True, (_CDT,))
    m3 = lambda j, n: (0, n, 0)
    dqs, dks, dvs, dbp, dbc, dsk = _rowmap_bwd("swa_bwd", _swa_fn, 1, nblk, rows, pars,
                                               [Row(_stack_heads(do_swa, nb), (SWA_KV_HEADS, SWA_GRP * SWA_BLOCK, SWA_HEAD_DIM), m3)])
    d_swa = jnp.concatenate([_unstack_heads(dqs, nb), dks.transpose(1, 0, 2).reshape(T, SWA_KV),
                             dvs.transpose(1, 0, 2).reshape(T, SWA_KV)], axis=1)
    dbias = jnp.concatenate([dbp.reshape(SWA_HEADS, half), dbc.reshape(SWA_HEADS, half)], axis=1)
    G["rel_bias"] = _mm("d_rel_bias", dbias, onehot_t.T, "nn", hi=True).T
    G["swa_sinks"] = _mm("d_sinks", dsk.reshape(SWA_HEADS, SWA_BLOCK), jnp.ones((SWA_BLOCK, LANES), F32), "nn", hi=True)[:, 0]

    dqkv_n, dbx, dgx, dz, dnw, got = _gdn_chunks_bwd(qkv_n, bx, gx, proj, nw, states, do_gdn, rider=src.rider_g(G))
    src.g_done(got)
    G["gdn_norm_w"] = dnw[0]
    dgba, dalog, ddtb = _rowmap_bwd("gdn_gate_bwd", _gdn_gate_fn, 1, T // tbl, gate_rows(True), gate_pars(True),
                                    [_rowspec(dbx, tbl, GDN_W, 0), _rowspec(dgx, tbl, GDN_W, 0)])
    G["gdn_a_log"], G["gdn_dt_bias"] = dalog[0, GDN_HEADS:2 * GDN_HEADS], ddtb[0, GDN_HEADS:2 * GDN_HEADS]
    dgqkv, dcw_gdn = _rowmap_bwd("gdn_pre_bwd", _gdn_pre_fn, ncq, T // tbp, pre_rows(True), pre_pars(True),
                                 [_rowspec(dqkv_n, tbp, LANES, 0)])
    G["gdn_conv_w"] = dcw_gdn

    dproj = jnp.concatenate([dgs, dgg, dgqkv, dz, d_swa, dgba, jnp.zeros((T, P_END - P_USED), _CDT)], axis=1)
    ru = src.rider_up(G)
    G["in_p"] = _mm("dw_in", xb, dproj, "tn", out_dtype=_GDT, rider=ru)
    if ru is not None:
        G["in_p"], got = G["in_p"]
        src.up_done(got)
    rl = src.rider_last(G)
    dx = _mm("dx", dproj, W["in_p"], "nt", add=dr1, add_scale=ALPHA, rider=rl)
    if rl is not None:
        dx, got = dx
        src.last_done(got)
    return loss, dx, G


W_NAMES = ["w_in", "rel_bias", "swa_sinks", "gdn_conv_w", "gdn_a_log", "gdn_dt_bias", "gdn_norm_w", "w_br_swa", "w_br_gdn",
           "w_mix_o", "ln1_g", "ln1_b", "w_mem_q", "w_mem_kv", "w_mem_o", "ln2_g", "ln2_b", "w_up", "ffn_conv_w", "ffn_conv_b",
           "w_down", "ln3_g", "ln3_b"]
BIG = ["w_in", "w_br_swa", "w_br_gdn", "w_mix_o", "w_mem_q", "w_mem_kv", "w_mem_o", "w_up", "w_down"]
SMALL = [n for n in W_NAMES if n not in BIG]
COL_SHARDED = ["w_in", "w_br_swa", "w_br_gdn", "w_mem_o", "w_up"]


def _pack(arrs):
    rows = []
    for a in arrs:
        f = a.reshape(-1).astype(F32)
        rows.append(jnp.pad(f, (0, (-f.shape[0]) % LANES)).reshape(-1, LANES))
    n = sum(r.shape[0] for r in rows)
    if n % 16:
        rows.append(jnp.zeros((16 - n % 16, LANES), F32))
    return jnp.concatenate(rows, axis=0)


def _unpack(p, shapes):
    out, off = [], 0
    for s in shapes:
        n = int(np.prod(s)) if len(s) else 1
        r = -(-n // LANES)
        out.append(p[off:off + r].reshape(-1)[:n].reshape(s))
        off += r
    return out


def _merge_shards(d):
    cat = lambda names: jnp.concatenate([d[n] for n in names], axis=-2)
    return [d.get("w_in"), d["w_up"], cat(["w_br_swa", "w_br_gdn", "w_mem_q", "w_mem_o"]), cat(["w_mix_o", "w_down"]), d["w_mem_kv"]]


def _split_shards(ts):
    a, b, c, dd, e = ts
    return {"w_in": a, "w_up": b, "w_br_swa": c[..., 0:1024, :], "w_br_gdn": c[..., 1024:2048, :], "w_mem_q": c[..., 2048:2560, :],
            "w_mem_o": c[..., 2560:3072, :], "w_mix_o": dd[..., 0:512, :], "w_down": dd[..., 512:, :], "w_mem_kv": e}


def _to_full(name, t):
    if name in COL_SHARDED:
        return _cols_from_chips(t, [(0, 4 * t.shape[2])])
    return t.reshape(4 * t.shape[1], t.shape[2])


def _to_chips(name, t):
    if name in COL_SHARDED:
        return _chips_from_cols(t, [(0, t.shape[1])], t.shape[1] // 4)
    return t.reshape(4, t.shape[0] // 4, t.shape[1])


def _cols_from_chips(g, segs, own=None):
    C, parts = g.shape[2], []
    for s in segs:
        if isinstance(s, int):
            parts.append(jnp.zeros((g.shape[1], s), g.dtype))
            continue
        lo, hi = s
        while lo < hi:
            k = lo // C
            e = min(hi, (k + 1) * C)
            piece = g[k][:, lo - k * C:e - k * C]
            parts.append(piece if own is None else jnp.where(own[1] == k, own[0][:, lo - k * C:e - k * C], piece))
            lo = e
    return jnp.concatenate(parts, axis=1)


def _chips_from_cols(p, segs, C):
    out = []
    for k in range(4):
        lo, hi, parts, o = k * C, (k + 1) * C, [], 0
        for plo, w in segs:
            a, b = max(lo, o), min(hi, o + w)
            if a < b:
                parts.append(p[:, plo + a - o:plo + b - o])
            o += w
        out.append(jnp.concatenate(parts, axis=1))
    return jnp.stack(out)


_IN_OFF = np.cumsum((0,) + IN_WIDTHS)
_IN_SEGS = [(P_SQ, SWA_Q), (P_SK, SWA_KV), (P_SV, SWA_KV), (P_GQKV, 3 * GDN_W), (P_GZ, GDN_W), (P_BA, 2 * GDN_HEADS),
            (P_GS, D_MODEL), (P_GG, D_MODEL)]
_IN_PADDED = [(int(_IN_OFF[i]), int(_IN_OFF[k])) for i, k in ((9, 10), (10, 11), (3, 6), (6, 7), (0, 1), (1, 2), (2, 3), (7, 9))] + [
    P_END - P_BA - 2 * GDN_HEADS]
_UP_SEGS = [(0, D_FF), (D_FF_PAD, D_FF)]
_UP_PADDED = [(0, D_FF), D_FF_PAD - D_FF, (D_FF, 2 * D_FF), D_FF_PAD - D_FF]


def _in_to_padded(w):
    o = _IN_OFF
    cut = lambda i, k: w[:, o[i]:o[k]]
    return jnp.concatenate([cut(9, 10), cut(10, 11), cut(3, 6), cut(6, 7), cut(0, 1), cut(1, 2), cut(2, 3), cut(7, 9),
                            jnp.zeros((w.shape[0], P_END - P_BA - 2 * GDN_HEADS), w.dtype)], axis=1)


def _in_from_padded(p):
    return jnp.concatenate([p[:, P_SQ:P_SQ + SWA_Q], p[:, P_SK:P_SK + SWA_KV], p[:, P_SV:P_SV + SWA_KV], p[:, P_GQKV:P_GQKV + 3 * GDN_W],
                            p[:, P_GZ:P_GZ + GDN_W], p[:, P_BA:P_BA + 2 * GDN_HEADS], p[:, P_GS:P_GS + D_MODEL], p[:, P_GG:P_GG + D_MODEL]],
                           axis=1)


def _ff_pad(t, axis):
    g, u = jnp.split(t, 2, axis=axis)
    pad = [(0, 0)] * t.ndim
    pad[axis] = (0, D_FF_PAD - D_FF)
    return jnp.concatenate([jnp.pad(g, pad), jnp.pad(u, pad)], axis=axis)


def _ff_unpad(t, axis):
    g, u = jnp.split(t, 2, axis=axis)
    return jnp.concatenate([lax.slice_in_dim(g, 0, D_FF, axis=axis), lax.slice_in_dim(u, 0, D_FF, axis=axis)], axis=axis)


def _assemble_weights(full, small):
    W = dict(small)
    W["in_p"] = _in_to_padded(full["w_in"])
    W["up_p"] = _ff_pad(full["w_up"], 1)
    W["down_p"] = jnp.pad(full["w_down"], ((0, D_FF_PAD - D_FF), (0, 0)))
    W["br_swa"], W["br_gdn"], W["mix_o"] = full["w_br_swa"], full["w_br_gdn"], full["w_mix_o"]
    W["mem_q"], W["mem_kv"], W["mem_o"] = full["w_mem_q"], full["w_mem_kv"], full["w_mem_o"]
    W["ffn_conv_w_p"] = _ff_pad(small["ffn_conv_w"], 1)
    W["ffn_conv_b_p"] = _ff_pad(small["ffn_conv_b"].reshape(1, -1), 1)
    return W


def _full_grads(G):
    out = {"w_in": _in_from_padded(G["in_p"])} if "in_p" in G else {}
    out.update({"w_up": _ff_unpad(G["up_p"], 1), "w_down": G["down_p"][:D_FF], "w_br_swa": G["br_swa"], "w_br_gdn": G["br_gdn"],
                "w_mix_o": G["mix_o"], "w_mem_q": G["mem_q"], "w_mem_kv": G["mem_kv"], "w_mem_o": G["mem_o"]})
    return out


def kernel(x, mem, w_in, rel_bias, swa_sinks, gdn_conv_w, gdn_a_log, gdn_dt_bias, gdn_norm_w, w_br_swa, w_br_gdn, w_mix_o, ln1_g, ln1_b, w_mem_q, w_mem_kv, w_mem_o, ln2_g, ln2_b, w_up, ffn_conv_w, ffn_conv_b, w_down, ln3_g, ln3_b, loss_target, m_w_in, m_rel_bias, m_swa_sinks, m_gdn_conv_w, m_gdn_a_log, m_gdn_dt_bias, m_gdn_norm_w, m_w_br_swa, m_w_br_gdn, m_w_mix_o, m_ln1_g, m_ln1_b, m_w_mem_q, m_w_mem_kv, m_w_mem_o, m_ln2_g, m_ln2_b, m_w_up, m_ffn_conv_w, m_ffn_conv_b, m_w_down, m_ln3_g, m_ln3_b, v_w_in, v_rel_bias, v_swa_sinks, v_gdn_conv_w, v_gdn_a_log, v_gdn_dt_bias, v_gdn_norm_w, v_w_br_swa, v_w_br_gdn, v_w_mix_o, v_ln1_g, v_ln1_b, v_w_mem_q, v_w_mem_kv, v_w_mem_o, v_ln2_g, v_ln2_b, v_w_up, v_ffn_conv_w, v_ffn_conv_b, v_w_down, v_ln3_g, v_ln3_b):
    a = dict(locals())
    w = {n: a[n] for n in W_NAMES}
    m = {n: a["m_" + n] for n in W_NAMES}
    v = {n: a["v_" + n] for n in W_NAMES}
    chip = 2 * lax.axis_index("x") + lax.axis_index("y")
    core = lax.axis_index("c")
    sq = lambda t: t.reshape(t.shape[1:]) if (t.ndim > 1 and t.shape[0] == 1 and t is not rel_bias) else t

    sh_a, sh_b, sh_c, sh_d, sh_e = _merge_shards({n: sq(w[n]).astype(_CDT) for n in BIG})
    fcw_sh, gcw_sh = sq(ffn_conv_w).shape, sq(gdn_conv_w).shape
    slot = lax.broadcasted_iota(jnp.int32, (4, 1, 1), 0)

    def with_own(got, mine):
        return [jnp.where(slot == chip, t[None], g) for g, t in zip(got, mine)]

    def reduce_start(tag, gch, theirs=None):
        pair = []
        theirs = _run_rider("pair_swap_" + tag, _swap_rider(gch)) if theirs is None else theirs
        for t, (mine, got) in enumerate(zip(gch, theirs)):
            rh = mine.shape[1] // 2
            mine_h = lax.dynamic_slice_in_dim(mine, core * rh, rh, axis=1)
            pair.append(_addn(f"pair_sum_{tag}{t}", [mine_h.reshape(4 * rh, -1), got.reshape(4 * rh, -1)], _GDT).reshape(4, rh, -1))
        return pair

    def reduce_end(tag, pair, others):
        halves = []
        for t, (p, o) in enumerate(zip(pair, others)):
            own = lax.dynamic_index_in_dim(p, chip, 0, keepdims=False)
            halves.append(_addn(f"chip_sum_{tag}{t}", [own, (o, 0), (o, 1), (o, 2)]))
        return halves

    class MeshWeights:
        def w1(self):
            mine = [sh_a, _pack([sq(ffn_conv_w), sq(gdn_conv_w)])]
            got_a, got_f = _run_rider("gather_first", _gather_rider(mine))
            got_f, = with_own([got_f], mine[1:])
            conv = [_unpack(got_f[k], [fcw_sh, gcw_sh]) for k in range(4)]
            W = {n: sq(w[n]) for n in SMALL}
            W["ffn_conv_w"] = jnp.concatenate([cv[0] for cv in conv], axis=1)
            W["gdn_conv_w"] = jnp.concatenate([cv[1] for cv in conv], axis=1)
            W["ffn_conv_w_p"] = _ff_pad(W["ffn_conv_w"], 1)
            W["ffn_conv_b_p"] = _ff_pad(W["ffn_conv_b"].reshape(1, -1), 1)
            W["in_p"] = _cols_from_chips(got_a, _IN_PADDED, own=(sh_a, chip))
            return W

        def rider_a(self):
            return _gather_rider([sh_c, sh_d, sh_e])

        def w2(self, got):
            c, d, e = with_own(got, [sh_c, sh_d, sh_e])
            f = {n: _to_full(n, t) for n, t in _split_shards([None, None, c, d, e]).items() if t is not None}
            return {"br_swa": f["w_br_swa"], "br_gdn": f["w_br_gdn"], "mix_o": f["w_mix_o"], "mem_q": f["w_mem_q"], "mem_kv": f["w_mem_kv"],
                    "mem_o": f["w_mem_o"], "down_p": jnp.pad(f["w_down"], ((0, D_FF_PAD - D_FF), (0, 0)))}

        def rider_b(self):
            return _gather_rider([sh_b])

        def w3(self, got):
            return {"up_p": _cols_from_chips(got[0], _UP_PADDED, own=(sh_b, chip))}

        def swap_rest(self, G):
            gf = _full_grads(G)
            gch = {n: _to_chips(n, gf[n]) for n in BIG if n not in ("w_in", "w_up")}
            gch["w_up"] = None
            self.gch_rest = _merge_shards(gch)[2:]
            return _swap_rider(self.gch_rest)

        def swap_rest_done(self, got):
            self.theirs_rest = got

        def rider_g(self, G):
            self.pair = reduce_start("rest", self.gch_rest, self.theirs_rest)
            return _scatter_rider(self.pair)

        def g_done(self, got):
            self.halves = reduce_end("rest", self.pair, got)

        def swap_up(self, G):
            self.gch_up = [_chips_from_cols(G["up_p"], _UP_SEGS, 2 * D_FF // 4)]
            return _swap_rider(self.gch_up)

        def swap_up_done(self, got):
            self.theirs_up = got

        def rider_up(self, G):
            self.pair_up = reduce_start("up", self.gch_up, self.theirs_up)
            return _scatter_rider(self.pair_up)

        def up_done(self, got):
            self.halves = reduce_end("up", self.pair_up, got) + self.halves

        def rider_last(self, G):
            self.pair_in = reduce_start("in", [_chips_from_cols(G["in_p"], _IN_SEGS, sum(IN_WIDTHS) // 4)])
            return _scatter_rider(self.pair_in)

        def last_done(self, got):
            self.halves = reduce_end("in", self.pair_in, got) + self.halves

    src = MeshWeights()
    loss, dx, G = _fwd_bwd(x[0], mem[0], loss_target[0], src)

    small_names = SMALL
    small_shapes = [()] + [tuple(G[n].shape) for n in small_names]
    packed = _pack([loss] + [G[n] for n in small_names])
    allp = _allgather8(packed)
    tot = _addn("small_sum", [(allp, k) for k in range(8)])
    parts = _unpack(tot, small_shapes)
    loss_tot, gsmall = parts[0], dict(zip(small_names, parts[1:]))
    gsmall["ffn_conv_w"] = lax.dynamic_slice_in_dim(gsmall["ffn_conv_w"], chip * fcw_sh[1], fcw_sh[1], axis=1)
    gsmall["gdn_conv_w"] = lax.dynamic_slice_in_dim(gsmall["gdn_conv_w"], chip * gcw_sh[1], gcw_sh[1], axis=1)

    both = []
    for h, o in zip(src.halves, _pair_exchange(src.halves)):
        both.append(jnp.concatenate([jnp.where(core == 0, h, o), jnp.where(core == 0, o, h)], axis=0))
    gbig = _split_shards(both)

    outs = {}
    for n in BIG:
        if w[n].shape[-1] % LANES:
            cols_out = lambda t: jnp.transpose(t, (2, 0, 1))
            g_ = jnp.transpose(gbig[n])[:, None, :]
            d_, m_, v_ = _adamw("adamw_" + n, cols_out(w[n]), g_, cols_out(m[n]), cols_out(v[n]))
            outs[n] = tuple(jnp.transpose(t, (1, 2, 0)) for t in (g_, d_, m_, v_))
            continue
        d_, m_, v_ = _adamw("adamw_" + n, sq(w[n]), gbig[n], sq(m[n]), sq(v[n]))
        outs[n] = (gbig[n], d_, m_, v_)
    for n in SMALL:
        two_d = (-1, w[n].shape[-1])
        g_ = gsmall[n].reshape(two_d)
        d_, m_, v_ = _adamw("adamw_" + n, w[n].reshape(two_d), g_, m[n].reshape(two_d), v[n].reshape(two_d))
        outs[n] = (g_, d_, m_, v_)

    res = [loss_tot.reshape(()), dx.reshape(x.shape)]
    for k in range(4):
        res += [outs[n][k].reshape(w[n].shape) for n in W_NAMES]
    return tuple(res)
```

```python
import functools
import math

import jax
import jax.numpy as jnp
import numpy as np
from jax import lax
from jax.experimental import pallas as pl
from jax.experimental.pallas import tpu as pltpu

F32 = jnp.float32
BF16 = jnp.bfloat16
_CDT = BF16
_GDT = BF16

D_MODEL = 2048
SWA_HEADS, SWA_KV_HEADS, SWA_HEAD_DIM, SWA_BLOCK = 16, 2, 64, 128
SWA_GRP = SWA_HEADS // SWA_KV_HEADS
REL_BUCKETS, REL_MAX_DIST = 32, 128
GDN_HEADS, GDN_HEAD_DIM, GDN_CONV, GDN_CHUNK = 8, 128, 4, 64
MEM_HEADS, MEM_HEAD_DIM = 4, 128
D_FF, D_FF_PAD, FFN_CONV = 5504, 5632, 3
SWA_Q, SWA_KV, GDN_W, MEM_W = 1024, 128, 1024, 512
IN_WIDTHS = (SWA_Q, SWA_KV, SWA_KV, GDN_W, GDN_W, GDN_W, GDN_W, GDN_HEADS, GDN_HEADS, D_MODEL, D_MODEL)
NORM_EPS = 1e-5
ALPHA = 2.0 ** 0.25
NEG_INF = -1e30
ADAM_LR, ADAM_B1, ADAM_B2, ADAM_EPS, ADAM_WD, ADAM_STEP = 0.001, 0.9, 0.999, 1e-08, 0.01, 10
LANES, SUBLANES = 128, 8
VMEM_LIMIT = 56 * 1024 * 1024

P_GS, P_GG, P_GQKV, P_GZ, P_SQ, P_SK, P_SV, P_BA, P_USED, P_END = 0, 2048, 4096, 7168, 8192, 9216, 9344, 9472, 9600, 9728


def _tile(dim, pref, align=LANES):
    if dim <= pref:
        return dim
    t = (pref // align) * align
    while t >= align:
        if dim % t == 0:
            return t
        t -= align
    return dim


_DIMS = {"nn": (((1,), (0,)), ((), ())), "nt": (((1,), (1,)), ((), ())), "tn": (((0,), (0,)), ((), ()))}
_BDIMS = {"nn": (((2,), (1,)), ((0,), (0,))), "nt": (((2,), (2,)), ((0,), (0,))), "tn": (((1,), (1,)), ((0,), (0,)))}


def _raw_dot(a, b, form, hi):
    dims = (_BDIMS if a.ndim == 3 else _DIMS)[form]
    if hi == "x3":
        a, b = a.astype(F32), b.astype(F32)
        ah, bh = a.astype(BF16), b.astype(BF16)
        al, bl = (a - ah.astype(F32)).astype(BF16), (b - bh.astype(F32)).astype(BF16)
        d = lambda p, q: lax.dot_general(p, q, dims, preferred_element_type=F32)
        if form == "tn":
            return d(ah, bh) + (d(ah, bl) + d(al, bh))
        m = a.shape[-2]
        both = d(jnp.concatenate([ah, al], axis=-2), bh)
        return both[..., :m, :] + (d(ah, bl) + both[..., m:, :])
    if hi:
        return lax.dot_general(a.astype(F32), b.astype(F32), dims, precision=lax.Precision.HIGHEST, preferred_element_type=F32)
    return lax.dot_general(a.astype(_CDT), b.astype(_CDT), dims, preferred_element_type=F32)


@functools.partial(jax.custom_vjp, nondiff_argnums=(2, 3))
def _dot(a, b, form, hi=False):
    return _raw_dot(a, b, form, hi)


def _dot_fwd(a, b, form, hi):
    return _raw_dot(a, b, form, hi), (a, b)


def _dot_bwd(form, hi, res, g):
    a, b = res
    if form == "nn":
        da, db = _raw_dot(g, b, "nt", hi), _raw_dot(a, g, "tn", hi)
    elif form == "nt":
        da, db = _raw_dot(g, b, "nn", hi), _raw_dot(g, a, "tn", hi)
    else:
        da, db = _raw_dot(b, g, "nt", hi), _raw_dot(a, g, "nn", hi)
    return da.astype(a.dtype), db.astype(b.dtype)


_dot.defvjp(_dot_fwd, _dot_bwd)


@functools.partial(jax.custom_vjp, nondiff_argnums=(2,))
def _shift_halo(prev, cur, d):
    assert prev.shape[0] == SUBLANES
    return pltpu.roll(jnp.concatenate([prev, cur], axis=0), d, 0)[SUBLANES:]


def _shift_halo_fwd(prev, cur, d):
    return _shift_halo(prev, cur, d), None


def _shift_halo_bwd(d, _, g):
    nh = SUBLANES
    ext = jnp.concatenate([jnp.zeros((nh, g.shape[1]), g.dtype), g], axis=0)
    r = pltpu.roll(ext, ext.shape[0] - d, 0)
    return r[:nh], r[nh:]


_shift_halo.defvjp(_shift_halo_fwd, _shift_halo_bwd)


@jax.custom_vjp
def _recip(x):
    return 1.0 / x


def _recip_fwd(x):
    r = 1.0 / x
    return r, r


def _recip_bwd(r, g):
    return (-g * r * r,)


_recip.defvjp(_recip_fwd, _recip_bwd)


def _sigmoid(x):
    return _recip(1.0 + jnp.exp(-x))


def _silu(x):
    return x * _sigmoid(x)


def _softplus(x):
    return jnp.maximum(x, 0.0) + jnp.log(1.0 + jnp.exp(-jnp.abs(x)))


def _iota(shape, axis):
    return lax.broadcasted_iota(jnp.int32, shape, axis)


def _cparams(sem, **kw):
    return pltpu.CompilerParams(dimension_semantics=sem, vmem_limit_bytes=VMEM_LIMIT, **kw)


class _ride:
    def __init__(self, rider, n_in, n_out, n_scr):
        self.rider = rider
        self.ins = rider.ins if rider else []
        n_rin = len(self.ins)
        self.out_shapes = rider.out_shapes if rider else []
        n_rout = len(self.out_shapes)
        self.in_specs, self.out_specs = [_HBM] * n_rin, [_HBM] * n_rout
        self.scratch = rider.sems() if rider else []
        self.o0 = n_in + n_rin
        self.s0 = self.o0 + n_out + n_rout
        self._rin = slice(n_in, n_in + n_rin)
        self._rout = slice(self.o0 + n_out, self.s0)
        self._sem = self.s0 + n_scr

    def _args(self, refs):
        return refs[self._rin], refs[self._rout], refs[self._sem], refs[self._sem + 1]

    def at_start(self, refs, cond):
        if self.rider:
            pl.when(cond)(lambda: self.rider.start(*self._args(refs)))

    def at_end(self, refs, cond):
        if self.rider:
            pl.when(cond)(lambda: self.rider.finish(*self._args(refs)))


def _mm(name, a, b, form, out_dtype=F32, add=None, add_scale=1.0, hi=False, tm=None, tn=None, tk=2816, rider=None, b_k0=None):
    if form == "nn":
        (M, K), (K2, N) = a.shape, b.shape
    elif form == "nt":
        (M, K), (N, K2) = a.shape, b.shape
        K2 = K if b_k0 is not None else K2
    else:
        (K, M), (K2, N) = a.shape, b.shape
    assert K == K2, (name, a.shape, b.shape, form)
    tm = tm or (512 if form == "tn" else 1024)
    tn = tn or (2816 if form == "tn" else 1024)
    tm, tn, tk = _tile(M, tm), _tile(N, tn), _tile(K, tk)
    nk = K // tk
    k0 = 0 if b_k0 is None else b_k0 // tk
    assert b_k0 is None or (form == "nt" and b_k0 % tk == 0)
    a_spec = pl.BlockSpec((tk, tm), lambda i, j, k: (k, i)) if form == "tn" else pl.BlockSpec((tm, tk), lambda i, j, k: (i, k))
    b_spec = pl.BlockSpec((tn, tk), lambda i, j, k: (j, k + k0)) if form == "nt" else pl.BlockSpec((tk, tn), lambda i, j, k: (k, j))
    o_spec = pl.BlockSpec((tm, tn), lambda i, j, k: (i, j))
    has_add = add is not None

    def finish(r, c_ref, o_ref):
        if has_add:
            r = r + add_scale * c_ref[...].astype(F32)
        o_ref[...] = r.astype(out_dtype)

    n_own = 3 if has_add else 2
    grid = (M // tm, N // tn, nk)
    rd = _ride(rider, n_own, 1, 1 if nk > 1 else 0)

    def body(*refs):
        a_ref, b_ref = refs[:2]
        c_ref = refs[2] if has_add else None
        o_ref = refs[rd.o0]
        pid = [pl.program_id(d) for d in range(3)]
        rd.at_start(refs, (pid[0] == 0) & (pid[1] == 0) & (pid[2] == 0))
        if nk == 1:
            finish(_raw_dot(a_ref[...], b_ref[...], form, hi), c_ref, o_ref)
        else:
            acc = refs[rd.s0]

            @pl.when(pid[2] == 0)
            def _():
                acc[...] = jnp.zeros_like(acc)

            acc[...] += _raw_dot(a_ref[...], b_ref[...], form, hi)

            @pl.when(pid[2] == nk - 1)
            def _():
                finish(acc[...], c_ref, o_ref)
        rd.at_end(refs, (pid[0] == grid[0] - 1) & (pid[1] == grid[1] - 1) & (pid[2] == nk - 1))

    ins = [a, b] + ([add] if has_add else [])
    specs = [a_spec, b_spec] + ([o_spec] if has_add else [])
    res = pl.pallas_call(
        body, name=name, grid=grid, in_specs=specs + rd.in_specs, out_specs=[o_spec] + rd.out_specs,
        out_shape=[jax.ShapeDtypeStruct((M, N), out_dtype)] + rd.out_shapes,
        scratch_shapes=([pltpu.VMEM((tm, tn), F32)] if nk > 1 else []) + rd.scratch,
        compiler_params=_cparams(("arbitrary",) * 3 if rider else ("parallel", "parallel", "arbitrary")),
    )(*ins, *rd.ins)
    return (res[0], res[1:]) if rider else res[0]


class Row:
    def __init__(self, arr, blk, imap, hblk=None, hmap=None, gshape=None, gmap=None, gdt=(F32,)):
        self.arr, self.blk, self.imap, self.hblk, self.hmap, self.gshape, self.gmap = arr, blk, imap, hblk, hmap, gshape, gmap
        self.gdt = gdt


class Par:
    def __init__(self, arr, blk=None, imap=None, gshape=None, gmap=None):
        self.arr = arr
        self.blk = tuple(arr.shape) if blk is None else blk
        nd = len(self.blk)
        self.imap = (lambda j: (0,) * nd) if imap is None else imap
        self.gshape, self.gmap = gshape, gmap


class Out:
    def __init__(self, shape, dtype, blk, imap):
        self.shape, self.dtype, self.blk, self.imap = shape, dtype, blk, imap


def _rows_of(blk):
    return [d for d in blk if d is not None][0]


def _rowmap(name, fn, ncol, nblk, rows, pars, outs, accs=()):
    in_specs, ins = [], []
    for r in rows:
        ins.append(r.arr)
        in_specs.append(pl.BlockSpec(r.blk, r.imap))
        if r.hblk is not None:
            ins.append(r.arr)
            in_specs.append(pl.BlockSpec(r.hblk, r.hmap))
    for p in pars:
        ins.append(p.arr)
        in_specs.append(pl.BlockSpec(p.blk, (lambda im: (lambda j, n: im(j)))(p.imap)))
    out_specs = [pl.BlockSpec(o.blk, o.imap) for o in outs]
    out_shape = [jax.ShapeDtypeStruct(o.shape, o.dtype) for o in outs]
    for a in accs:
        out_specs.append(pl.BlockSpec(a, (lambda nd: (lambda j, n: (0,) * nd))(len(a))))
        out_shape.append(jax.ShapeDtypeStruct(a, F32))
    n_in = len(ins)

    def body(*refs):
        j, n = pl.program_id(0), pl.program_id(1)
        it = iter(refs[:n_in])
        rvals = []
        for r in rows:
            cur = next(it)[...]
            rvals.append((next(it)[...], cur) if r.hblk is not None else cur)
        pvals = [next(it)[...] for _ in pars]
        o_refs = refs[n_in:n_in + len(outs)]
        a_refs = refs[n_in + len(outs):]
        ovals, avals = fn(j, n == 0, rvals, pvals)
        for ref, v in zip(o_refs, ovals):
            ref[...] = v.astype(ref.dtype)
        if accs:
            @pl.when((j == 0) & (n == 0))
            def _():
                for ref in a_refs:
                    ref[...] = jnp.zeros_like(ref)
            for ref, v in zip(a_refs, avals):
                ref[...] += v

    res = pl.pallas_call(
        body, name=name, grid=(ncol, nblk), in_specs=in_specs, out_specs=out_specs, out_shape=out_shape,
        compiler_params=_cparams(("arbitrary", "arbitrary")),
    )(*ins)
    return res


def _rowmap_bwd(name, fn, ncol, nblk, rows, pars, cts):
    rev = lambda im: (lambda j, s: im(j, nblk - 1 - s))
    in_specs, ins = [], []
    for r in rows:
        ins.append(r.arr)
        in_specs.append(pl.BlockSpec(r.blk, rev(r.imap)))
        if r.hblk is not None:
            ins.append(r.arr)
            in_specs.append(pl.BlockSpec(r.hblk, rev(r.hmap)))
    for p in pars:
        ins.append(p.arr)
        in_specs.append(pl.BlockSpec(p.blk, (lambda im: (lambda j, s: im(j)))(p.imap)))
    for c in cts:
        ins.append(c.arr)
        in_specs.append(pl.BlockSpec(c.blk, rev(c.imap)))
    n_in = len(ins)
    drows = [i for i, r in enumerate(rows) if r.gshape is not None]
    dpars = [i for i, p in enumerate(pars) if p.gshape is not None]
    out_specs, out_shape, scratch = [], [], []
    for i in drows:
        r = rows[i]
        for dt in r.gdt:
            out_specs.append(pl.BlockSpec(r.blk, rev(r.gmap)))
            out_shape.append(jax.ShapeDtypeStruct(r.gshape, dt))
        if r.hblk is not None:
            scratch.append(pltpu.VMEM(tuple(d for d in r.hblk if d is not None), F32))
    n_drow_out = len(out_specs)
    for i in dpars:
        p = pars[i]
        out_specs.append(pl.BlockSpec(p.blk, (lambda im: (lambda j, s: im(j)))(p.gmap)))
        out_shape.append(jax.ShapeDtypeStruct(p.gshape, F32))

    def body(*refs):
        j, s = pl.program_id(0), pl.program_id(1)
        first = s == nblk - 1
        it = iter(refs[:n_in])
        rvals = []
        for r in rows:
            cur = next(it)[...]
            rvals.append((next(it)[...], cur) if r.hblk is not None else cur)
        pvals = [next(it)[...] for _ in pars]
        cvals = [next(it)[...].astype(F32) for _ in cts]
        g_refs = iter(refs[n_in:n_in + n_drow_out])
        p_refs = refs[n_in + n_drow_out:n_in + n_drow_out + len(dpars)]
        carries = iter(refs[n_in + n_drow_out + len(dpars):])

        def f(dr, dp):
            rv, pv = list(rvals), list(pvals)
            for i, v in zip(drows, dr):
                rv[i] = v
            for i, v in zip(dpars, dp):
                pv[i] = v
            return fn(j, first, rv, pv)

        _, vjp = jax.vjp(f, [rvals[i] for i in drows], [pvals[i] for i in dpars])
        g_r, g_p = vjp(cvals)
        for i, g in zip(drows, g_r):
            r = rows[i]
            if r.hblk is None:
                for _ in r.gdt:
                    ref = next(g_refs)
                    ref[...] = g.astype(ref.dtype)
            else:
                g_prev, g_cur = g
                carry = next(carries)
                nr, nh = g_cur.shape[-2], g_prev.shape[-2]
                tail = g_cur[..., nr - nh:nr, :] + jnp.where(s > 0, carry[...], 0.0)
                for _ in r.gdt:
                    ref = next(g_refs)
                    if nr > nh:
                        ref[..., 0:nr - nh, :] = g_cur[..., 0:nr - nh, :].astype(ref.dtype)
                    ref[..., nr - nh:nr, :] = tail.astype(ref.dtype)
                carry[...] = g_prev
        for ref, g in zip(p_refs, g_p):
            @pl.when(s == 0)
            def _():
                ref[...] = jnp.zeros_like(ref)
            ref[...] += g

    return pl.pallas_call(
        body, name=name, grid=(ncol, nblk), in_specs=in_specs, out_specs=out_specs, out_shape=out_shape,
        scratch_shapes=scratch, compiler_params=_cparams(("arbitrary", "arbitrary")),
    )(*ins)


def _rowspec(arr, tb, cw, c0, cstep=1, halo=0, grad=False, ncol=1, gdt=(F32,)):
    T = arr.shape[0]
    imap = lambda j, n: (n, c0 + cstep * j)
    hblk = hmap = None
    if halo:
        q = tb // halo
        hblk, hmap = (halo, cw), (lambda j, n: (jnp.maximum(n * q - 1, 0), c0 + cstep * j))
    gshape = (T, cw * (ncol if cstep else 1)) if grad else None
    gmap = (lambda j, n: (n, cstep * j)) if grad else None
    return Row(arr, (tb, cw), imap, hblk, hmap, gshape, gmap, gdt)


def _parspec(arr, cw=None, c0=0, grad=False, ncol=1):
    if cw is None:
        return Par(arr, gshape=tuple(arr.shape) if grad else None,
                   gmap=(lambda nd: (lambda j: (0,) * nd))(arr.ndim) if grad else None)
    r = arr.shape[0]
    return Par(arr, (r, cw), lambda j: (0, c0 + j), (r, cw * ncol) if grad else None, (lambda j: (0, j)) if grad else None)


def _ln(r, g, b):
    mu = jnp.mean(r, axis=-1, keepdims=True)
    xc = r - mu
    var = jnp.mean(xc * xc, axis=-1, keepdims=True)
    return xc * lax.rsqrt(var + NORM_EPS) * g + b


def _ln_fn(j, first, rv, pv):
    return [_ln(rv[0], pv[0], pv[1])]


def _ln_fwd_fn(j, first, rv, pv):
    y = _ln(rv[0], pv[0], pv[1])
    return [y, y], []


def _loss_fn(j, first, rv, pv):
    r3, tgt = rv
    g, b = pv
    y, vjp = jax.vjp(_ln, r3, g, b)
    diff = y - tgt
    part = 0.5 * jnp.sum(diff * diff) / D_MODEL
    dr, dg, db = vjp(diff * (1.0 / D_MODEL))
    return [dr, dr], [jnp.full((SUBLANES, LANES), part, F32), dg, db]


def _mix_fn(j, first, rv, pv):
    gs, gg, ys, yg = rv
    return [_sigmoid(gs) * ys + _sigmoid(gg) * yg]


def _row_pick(x, i):
    ax = x.ndim - 2
    return jnp.sum(jnp.where(_iota(x.shape, ax) == i, x, 0.0), axis=ax, keepdims=True)


def _causal_conv(prev, cur, w, first):
    width = w.shape[0]
    prev = jnp.where(first, 0.0, prev)
    y = cur * _row_pick(w, width - 1)
    for d in range(1, width):
        y = y + _shift_halo(prev, cur, d) * _row_pick(w, width - 1 - d)
    return y


def _ffn_act_fn(j, first, rv, pv):
    (pg, cg), (pu, cu) = rv
    wg, wu, bg, bu = pv
    hg = _causal_conv(pg, cg, wg, first) + bg
    hu = _causal_conv(pu, cu, wu, first) + bu
    return [_silu(hg) * hu]


def _gdn_pre_fn(j, first, rv, pv):
    (prev, cur), = rv
    w, = pv
    t = _silu(_causal_conv(prev, cur, w, first))
    tn = t * lax.rsqrt(jnp.sum(t * t, axis=-1, keepdims=True) + 1e-6)
    return [jnp.where(j < 2 * GDN_HEADS, tn, t)]


def _gdn_gate_fn(j, first, rv, pv):
    gba, = rv
    alog, dtb, eb, eg = pv
    tb = gba.shape[0]
    beta = _sigmoid(gba)
    g = -jnp.exp(alog) * _softplus(gba + dtb)
    ri, ci = _iota((tb, tb), 0), _iota((tb, tb), 1)
    tril = jnp.where((ri // GDN_CHUNK == ci // GDN_CHUNK) & (ci <= ri), 1.0, 0.0)
    gc = _dot(tril, g, "nn", True)
    return [_dot(beta, eb, "nn", True), _dot(gc, eg, "nn", True)]


def _swa_fn(j, first, rv, pv):
    q, (kp, kc), (vp, vc) = rv
    bp, bc, sk = pv
    sp = _dot(q, kp, "nt") * (SWA_HEAD_DIM ** -0.5) + bp
    sc = _dot(q, kc, "nt") * (SWA_HEAD_DIM ** -0.5) + bc
    qi = _iota(sp.shape, sp.ndim - 2) % SWA_BLOCK
    kj = _iota(sp.shape, sp.ndim - 1)
    sp = jnp.where((kj > qi) & jnp.logical_not(first), sp, NEG_INF)
    sc = jnp.where(kj <= qi, sc, NEG_INF)
    m = jnp.maximum(jnp.maximum(jnp.max(sp, axis=-1, keepdims=True), jnp.max(sc, axis=-1, keepdims=True)), sk)
    m = lax.stop_gradient(m)
    ep, ec, es = jnp.exp(sp - m), jnp.exp(sc - m), jnp.exp(sk - m)
    inv = 1.0 / (jnp.sum(ep, axis=-1, keepdims=True) + jnp.sum(ec, axis=-1, keepdims=True) + es)
    vp = jnp.where(first, 0.0, vp)
    return [_dot(ep * inv, vp, "nn") + _dot(ec * inv, vc, "nn")]


def _memattn_fn(j, first, rv, pv):
    q, = rv
    k, v = pv
    s = _dot(q, k, "nt") * (MEM_HEAD_DIM ** -0.5)
    m = lax.stop_gradient(jnp.max(s, axis=-1, keepdims=True))
    e = jnp.exp(s - m)
    p = e * (1.0 / jnp.sum(e, axis=-1, keepdims=True))
    return [_dot(p, v, "nn")]


SOLVE_PREC = "x3"


@jax.custom_vjp
def _unit_lower_inv(a):
    c = a.shape[-1]
    eye = _iota((1, c, c), 1) == _iota((1, c, c), 2)
    tinv = jnp.where(eye, 1.0, 0.0) - a
    x = _raw_dot(a, a, "nn", SOLVE_PREC)
    for i in range(5):
        tinv = tinv + _raw_dot(tinv, x, "nn", SOLVE_PREC)
        if i < 4:
            x = _raw_dot(x, x, "nn", SOLVE_PREC)
    return tinv


def _unit_lower_inv_fwd(a):
    t = _unit_lower_inv(a)
    return t, t


def _unit_lower_inv_bwd(t, g):
    return (-_raw_dot(_raw_dot(t, g, "tn", SOLVE_PREC), t, "nt", SOLVE_PREC),)


_unit_lower_inv.defvjp(_unit_lower_inv_fwd, _unit_lower_inv_bwd)


@jax.custom_vjp
def _known_inv(a, t):
    return t


def _known_inv_fwd(a, t):
    return t, t


def _known_inv_bwd(t, g):
    return _unit_lower_inv_bwd(t, g) + (jnp.zeros_like(t),)


_known_inv.defvjp(_known_inv_fwd, _known_inv_bwd)


def _gdn_heads(q, k, v, bx, gx, g64, z, nw, S, tinv=None, keep_tinv=False):
    c = GDN_CHUNK
    q = q * (GDN_HEAD_DIM ** -0.5)
    kb, vb = k * bx, v * bx
    ri, ci = _iota((1, c, c), 1), _iota((1, c, c), 2)
    tril, strict, eye = ci <= ri, ci < ri, ci == ri
    grow = jnp.sum(jnp.where(eye, g64, 0.0), axis=1, keepdims=True)
    decay = jnp.where(tril, jnp.exp(jnp.where(tril, g64 - grow, 0.0)), 0.0)
    a = jnp.where(strict, _dot(kb, k, "nt") * decay, 0.0)
    tinv = _unit_lower_inv(a) if tinv is None else _known_inv(a, tinv)
    eg = jnp.exp(gx)
    u = _dot(tinv, vb, "nn", SOLVE_PREC)
    w = _dot(tinv, kb * eg, "nn", SOLVE_PREC)
    ai = jnp.where(tril, _dot(q, k, "nt") * decay, 0.0)
    glast = _row_pick(gx, c - 1)
    v_new = u - _dot(w, S, "nn")
    o = _dot(q * eg, S, "nn") + _dot(ai, v_new, "nn")
    s_new = S * jnp.exp(glast) + _dot(k * jnp.exp(glast - gx), v_new, "tn")
    o = o * lax.rsqrt(jnp.mean(o * o, axis=-1, keepdims=True) + 1e-6) * nw
    return (o * _silu(z), s_new, tinv) if keep_tinv else (o * _silu(z), s_new)


GDN_STEP_CHUNKS = 2


def _head_major(ref, off, width=GDN_HEAD_DIM, ci=0):
    r = slice(ci * GDN_CHUNK, (ci + 1) * GDN_CHUNK)
    return jnp.stack([ref[r, off + h * GDN_HEAD_DIM:off + h * GDN_HEAD_DIM + width] for h in range(GDN_HEADS)])


def _gdn_chunks_fwd(qkv, bx, gx, proj, nw, rider=None):
    T = qkv.shape[0]
    cps = GDN_STEP_CHUNKS
    nc, c, hd, nh = T // (cps * GDN_CHUNK), GDN_CHUNK, GDN_HEAD_DIM, GDN_HEADS
    rd = _ride(rider, 5, 3, 1)

    def body(*refs):
        qkv_ref, bx_ref, gx_ref, z_ref, nw_ref = refs[:5]
        y_ref, st_ref, ti_ref = refs[rd.o0:rd.o0 + 3]
        S = refs[rd.s0]
        rd.at_start(refs, pl.program_id(0) == 0)

        @pl.when(pl.program_id(0) == 0)
        def _():
            S[...] = jnp.zeros_like(S)

        s_new = S[...]
        for ci in range(cps):
            st_ref[ci] = s_new
            y, s_new, ti = _gdn_heads(_head_major(qkv_ref, 0, ci=ci), _head_major(qkv_ref, GDN_W, ci=ci),
                                      _head_major(qkv_ref, 2 * GDN_W, ci=ci), _head_major(bx_ref, 0, ci=ci),
                                      _head_major(gx_ref, 0, ci=ci), _head_major(gx_ref, 0, c, ci), _head_major(z_ref, 0, ci=ci),
                                      nw_ref[...], s_new, keep_tinv=True)
            ti_ref[ci] = ti
            for h in range(nh):
                y_ref[ci * c:(ci + 1) * c, h * hd:(h + 1) * hd] = y[h].astype(y_ref.dtype)
        S[...] = s_new
        rd.at_end(refs, pl.program_id(0) == nc - 1)

    row = lambda w, cb: pl.BlockSpec((cps * c, w), lambda n: (n, cb))
    res = pl.pallas_call(
        body, name="gdn_chunks_fwd", grid=(nc,),
        in_specs=[row(3 * GDN_W, 0), row(GDN_W, 0), row(GDN_W, 0), row(GDN_W, P_GZ // GDN_W),
                  pl.BlockSpec((1, hd), lambda n: (0, 0))] + rd.in_specs,
        out_specs=[row(GDN_W, 0), pl.BlockSpec((cps, nh, hd, hd), lambda n: (n, 0, 0, 0)),
                   pl.BlockSpec((cps, nh, c, c), lambda n: (n, 0, 0, 0))] + rd.out_specs,
        out_shape=[jax.ShapeDtypeStruct((T, GDN_W), BF16), jax.ShapeDtypeStruct((nc * cps, nh, hd, hd), F32),
                   jax.ShapeDtypeStruct((nc * cps, nh, c, c), F32)] + rd.out_shapes,
        scratch_shapes=[pltpu.VMEM((nh, hd, hd), F32)] + rd.scratch,
        compiler_params=_cparams(("arbitrary",)),
    )(qkv, bx, gx, proj, nw, *rd.ins)
    return res[0], (res[1], res[2]), res[3:]


def _gdn_chunks_bwd(qkv, bx, gx, proj, nw, saved, dy, rider=None):
    states, tinvs = saved
    T = qkv.shape[0]
    cps = GDN_STEP_CHUNKS
    nc, c, hd, nh = T // (cps * GDN_CHUNK), GDN_CHUNK, GDN_HEAD_DIM, GDN_HEADS
    rd = _ride(rider, 8, 5, 1)

    def body(*refs):
        qkv_ref, bx_ref, gx_ref, z_ref, nw_ref, st_ref, ti_ref, dy_ref = refs[:8]
        dqkv_ref, dbx_ref, dgx_ref, dz_ref, dnw_ref = refs[rd.o0:rd.o0 + 5]
        dS = refs[rd.s0]
        rd.at_start(refs, pl.program_id(0) == 0)

        @pl.when(pl.program_id(0) == 0)
        def _():
            dS[...] = jnp.zeros_like(dS)
            dnw_ref[...] = jnp.zeros_like(dnw_ref)

        dsp = dS[...]
        for ci in reversed(range(cps)):
            r = slice(ci * c, (ci + 1) * c)
            args = (_head_major(qkv_ref, 0, ci=ci), _head_major(qkv_ref, GDN_W, ci=ci), _head_major(qkv_ref, 2 * GDN_W, ci=ci),
                    _head_major(bx_ref, 0, ci=ci), _head_major(gx_ref, 0, ci=ci), _head_major(gx_ref, 0, c, ci),
                    _head_major(z_ref, 0, ci=ci), nw_ref[...], st_ref[ci])
            _, vjp = jax.vjp(functools.partial(_gdn_heads, tinv=ti_ref[ci]), *args)
            dq, dk, dv, dbx, dgx, dg64, dz, dnw, dsp = vjp((_head_major(dy_ref, 0, ci=ci), dsp))
            for h in range(nh):
                sl = slice(h * hd, (h + 1) * hd)
                dqkv_ref[r, sl] = dq[h].astype(dqkv_ref.dtype)
                dqkv_ref[r, GDN_W + h * hd:GDN_W + (h + 1) * hd] = dk[h].astype(dqkv_ref.dtype)
                dqkv_ref[r, 2 * GDN_W + h * hd:2 * GDN_W + (h + 1) * hd] = dv[h].astype(dqkv_ref.dtype)
                dbx_ref[r, sl] = dbx[h]
                dgx_ref[r, sl] = dgx[h]
                dgx_ref[r, h * hd:h * hd + c] += dg64[h]
                dz_ref[r, sl] = dz[h].astype(dz_ref.dtype)
            dnw_ref[...] += dnw
        dS[...] = dsp
        rd.at_end(refs, pl.program_id(0) == nc - 1)

    row = lambda w, cb: pl.BlockSpec((cps * c, w), lambda s: (nc - 1 - s, cb))
    res = pl.pallas_call(
        body, name="gdn_chunks_bwd", grid=(nc,),
        in_specs=[row(3 * GDN_W, 0), row(GDN_W, 0), row(GDN_W, 0), row(GDN_W, P_GZ // GDN_W), pl.BlockSpec((1, hd), lambda s: (0, 0)),
                  pl.BlockSpec((cps, nh, hd, hd), lambda s: (nc - 1 - s, 0, 0, 0)),
                  pl.BlockSpec((cps, nh, c, c), lambda s: (nc - 1 - s, 0, 0, 0)), row(GDN_W, 0)] + rd.in_specs,
        out_specs=[row(3 * GDN_W, 0), row(GDN_W, 0), row(GDN_W, 0), row(GDN_W, 0),
                   pl.BlockSpec((1, hd), lambda s: (0, 0))] + rd.out_specs,
        out_shape=[jax.ShapeDtypeStruct((T, 3 * GDN_W), F32), jax.ShapeDtypeStruct((T, GDN_W), F32),
                   jax.ShapeDtypeStruct((T, GDN_W), F32), jax.ShapeDtypeStruct((T, GDN_W), _CDT),
                   jax.ShapeDtypeStruct((1, hd), F32)] + rd.out_shapes,
        scratch_shapes=[pltpu.VMEM((nh, hd, hd), F32)] + rd.scratch,
        compiler_params=_cparams(("arbitrary",)),
    )(qkv, bx, gx, proj, nw, states, tinvs, dy, *rd.ins)
    res = list(res)
    return res[:5] + [res[5:]]


ADAMW_BLOCK_ELEMS = 700_000


def _adamw(name, w, g, m, v):
    if w.ndim == 3:
        C, _, R = w.shape
        blk = (_tile(C, 768, 1), 1, _tile(R, 512))
        grid = (C // blk[0], R // blk[2])
        spec = pl.BlockSpec(blk, lambda i, j: (i, 0, j))
    else:
        R, C = w.shape
        tr = _tile(R, max(SUBLANES, ADAMW_BLOCK_ELEMS // C // SUBLANES * SUBLANES), SUBLANES)
        grid = (R // tr,)
        spec = pl.BlockSpec((tr, C), lambda i: (i, 0))

    def body(w_ref, g_ref, m_ref, v_ref, d_ref, m2_ref, v2_ref):
        g_ = g_ref[...]
        m2 = ADAM_B1 * m_ref[...] + (1.0 - ADAM_B1) * g_
        v2 = ADAM_B2 * v_ref[...] + (1.0 - ADAM_B2) * (g_ * g_)
        m_hat = m2 / (1.0 - ADAM_B1 ** ADAM_STEP)
        v_hat = v2 / (1.0 - ADAM_B2 ** ADAM_STEP)
        d_ref[...] = -ADAM_LR * (m_hat / (jnp.sqrt(v_hat) + ADAM_EPS) + ADAM_WD * w_ref[...])
        m2_ref[...] = m2
        v2_ref[...] = v2

    return pl.pallas_call(
        body, name=name, grid=grid, in_specs=[spec] * 4, out_specs=[spec] * 3,
        out_shape=[jax.ShapeDtypeStruct(w.shape, F32)] * 3, compiler_params=_cparams(("parallel",) * len(grid)),
    )(w, g, m, v)


def _addn(name, parts, out_dtype=F32):
    parts = [p if isinstance(p, tuple) else (p, None) for p in parts]
    a0, k0 = parts[0]
    R, C = a0.shape[-2:]
    tr = _tile(R, 256, 2 * SUBLANES)
    specs = []
    for a, k in parts:
        if k is None:
            specs.append(pl.BlockSpec((tr, C), lambda i: (i, 0)))
        else:
            specs.append(pl.BlockSpec((None, tr, C), (lambda kk: (lambda i: (kk, i, 0)))(k)))

    def body(*refs):
        acc = refs[0][...].astype(F32)
        for r in refs[1:-1]:
            acc = acc + r[...].astype(F32)
        refs[-1][...] = acc.astype(out_dtype)

    return pl.pallas_call(
        body, name=name, grid=(R // tr,), in_specs=specs, out_specs=pl.BlockSpec((tr, C), lambda i: (i, 0)),
        out_shape=jax.ShapeDtypeStruct((R, C), out_dtype), compiler_params=_cparams(("parallel",)),
    )(*[a for a, _ in parts])


MESH = pl.DeviceIdType.MESH
_HBM = pl.BlockSpec(memory_space=pltpu.HBM)


def _place():
    x, y, c = lax.axis_index("x"), lax.axis_index("y"), lax.axis_index("c")
    return x, y, c, [(1 - x, y), (x, 1 - y), (1 - x, 1 - y)]


class _Rider:
    def __init__(self, ins, out_shapes, nsem, start, finish):
        self.ins, self.out_shapes, self.nsem, self.start, self.finish = list(ins), list(out_shapes), nsem, start, finish

    def sems(self):
        return [pltpu.SemaphoreType.DMA((self.nsem,)), pltpu.SemaphoreType.DMA((self.nsem,))]


def _run_rider(name, rd):
    n_in, n_out = len(rd.ins), len(rd.out_shapes)

    def body(*refs):
        ins, outs, (send, recv) = refs[:n_in], refs[n_in:n_in + n_out], refs[n_in + n_out:]
        rd.start(ins, outs, send, recv)
        rd.finish(ins, outs, send, recv)

    return pl.pallas_call(body, name=name, in_specs=[_HBM] * n_in, out_specs=[_HBM] * n_out, out_shape=rd.out_shapes,
                          scratch_shapes=rd.sems())(*rd.ins)


def _gather_rider(ts):
    nt = len(ts)

    def half(t, hc):
        rh = ts[t].shape[0] // 2
        return pl.ds(pl.multiple_of(hc * rh, 16), rh)

    def rcopy(send, recv, t, k, src, dst, to):
        return pltpu.make_async_remote_copy(src_ref=src, dst_ref=dst, send_sem=send.at[6 * t + k], recv_sem=recv.at[6 * t + k],
                                            device_id=to, device_id_type=MESH)

    def first_hop(ins, outs, send, recv, t, r, px, py, c, me):
        return rcopy(send, recv, t, r, ins[t].at[half(t, c)], outs[t].at[me, half(t, c)], (px, py, c))

    def start(ins, outs, send, recv):
        x, y, c, rel = _place()
        for t in range(nt):
            for r, (px, py) in enumerate(rel):
                first_hop(ins, outs, send, recv, t, r, px, py, c, 2 * x + y).start()

    def finish(ins, outs, send, recv):
        x, y, c, rel = _place()
        sib = (x, y, 1 - c)
        passed = []
        for t in range(nt):
            for r, (px, py) in enumerate(rel):
                got = outs[t].at[2 * px + py, half(t, c)]
                rcopy(send, recv, t, r, got, got, (px, py, c)).wait_recv()
                fw = rcopy(send, recv, t, 3 + r, got, got, sib)
                fw.start()
                passed.append(fw)
        for t in range(nt):
            for r, (px, py) in enumerate(rel):
                got = outs[t].at[2 * px + py, half(t, 1 - c)]
                rcopy(send, recv, t, 3 + r, got, got, sib).wait_recv()
        for t in range(nt):
            for r, (px, py) in enumerate(rel):
                first_hop(ins, outs, send, recv, t, r, px, py, c, 2 * x + y).wait_send()
        for fw in passed:
            fw.wait_send()

    return _Rider(ts, [jax.ShapeDtypeStruct((4,) + tuple(t.shape), t.dtype) for t in ts], 6 * nt, start, finish)


def _scatter_rider(ps):
    nt = len(ps)

    def copy(ins, outs, send, recv, t, r, px, py, c):
        return pltpu.make_async_remote_copy(src_ref=ins[t].at[2 * px + py], dst_ref=outs[t].at[r], send_sem=send.at[3 * t + r],
                                            recv_sem=recv.at[3 * t + r], device_id=(px, py, c), device_id_type=MESH)

    def start(ins, outs, send, recv):
        x, y, c, rel = _place()
        for t in range(nt):
            for r, (px, py) in enumerate(rel):
                copy(ins, outs, send, recv, t, r, px, py, c).start()

    def finish(ins, outs, send, recv):
        x, y, c, rel = _place()
        for t in range(nt):
            for r, (px, py) in enumerate(rel):
                copy(ins, outs, send, recv, t, r, px, py, c).wait()

    return _Rider(ps, [jax.ShapeDtypeStruct((3,) + tuple(p.shape[1:]), p.dtype) for p in ps], 3 * nt, start, finish)


def _swap_rider(ts):
    nt = len(ts)

    def copy(ins, outs, send, recv, t):
        x, y, c, _ = _place()
        rh = ts[t].shape[1] // 2
        src = ins[t].at[:, pl.ds(pl.multiple_of((1 - c) * rh, 16), rh), :]
        return pltpu.make_async_remote_copy(src_ref=src, dst_ref=outs[t], send_sem=send.at[t], recv_sem=recv.at[t],
                                            device_id=(x, y, 1 - c), device_id_type=MESH)

    def start(ins, outs, send, recv):
        for t in range(nt):
            copy(ins, outs, send, recv, t).start()

    def finish(ins, outs, send, recv):
        for t in range(nt):
            copy(ins, outs, send, recv, t).wait()

    return _Rider(ts, [jax.ShapeDtypeStruct((4, t.shape[1] // 2, t.shape[2]), t.dtype) for t in ts], nt, start, finish)


def _pair_exchange(gs):
    nt = len(gs)

    def body(*refs):
        ins, outs = refs[:nt], refs[nt:2 * nt]
        send, recv = refs[2 * nt:]
        x, y, c, _ = _place()
        cps = []
        for t in range(nt):
            cp = pltpu.make_async_remote_copy(src_ref=ins[t], dst_ref=outs[t], send_sem=send.at[t], recv_sem=recv.at[t],
                                              device_id=(x, y, 1 - c), device_id_type=MESH)
            cp.start()
            cps.append(cp)
        for cp in cps:
            cp.wait()

    return pl.pallas_call(
        body, name="pair_exchange", in_specs=[_HBM] * nt, out_specs=[_HBM] * nt,
        out_shape=[jax.ShapeDtypeStruct(tuple(g.shape), g.dtype) for g in gs],
        scratch_shapes=[pltpu.SemaphoreType.DMA((nt,)), pltpu.SemaphoreType.DMA((nt,))],
    )(*gs)


def _allgather8(v):
    m, n = v.shape

    def body(x_ref, out_ref, send, recv, lsem):
        x, y, c, rel = _place()
        me, sib = (x, y, c), (x, y, 1 - c)

        def blk(px, py, pc):
            return out_ref.at[4 * px + 2 * py + pc]

        def copy(k, block, to, src=None):
            return pltpu.make_async_remote_copy(src_ref=blk(*block) if src is None else src, dst_ref=blk(*block), send_sem=send.at[k],
                                                recv_sem=recv.at[k], device_id=to, device_id_type=MESH)

        mine = pltpu.make_async_copy(x_ref, blk(*me), lsem)
        mine.start()
        first = [copy(0, me, sib, src=x_ref)] + [copy(1 + r, me, (*ch, c), src=x_ref) for r, ch in enumerate(rel)]
        for cp in first:
            cp.start()
        passed = [copy(4 + r, (*ch, c), sib) for r, ch in enumerate(rel)]
        for r, ch in enumerate(rel):
            copy(1 + r, (*ch, c), me).wait_recv()
            passed[r].start()
        copy(0, sib, me).wait_recv()
        for r, ch in enumerate(rel):
            copy(4 + r, (*ch, 1 - c), me).wait_recv()
        for cp in first + passed:
            cp.wait_send()
        mine.wait()

    return pl.pallas_call(
        body, name="allgather8", in_specs=[pl.BlockSpec(memory_space=pltpu.VMEM)], out_specs=pl.BlockSpec(memory_space=pltpu.VMEM),
        out_shape=jax.ShapeDtypeStruct((8, m, n), v.dtype),
        scratch_shapes=[pltpu.SemaphoreType.DMA((7,)), pltpu.SemaphoreType.DMA((7,)), pltpu.SemaphoreType.DMA],
    )(v)


def _t5_bucket(dist):
    max_exact = REL_BUCKETS // 2
    d = jnp.maximum(dist, 1).astype(F32)
    large = max_exact + (jnp.log(d / max_exact) / math.log(REL_MAX_DIST / max_exact) * (REL_BUCKETS - max_exact)).astype(jnp.int32)
    large = jnp.minimum(large, REL_BUCKETS - 1)
    return jnp.where(dist < max_exact, dist, large)


def _bias_onehot():
    qi = jnp.arange(SWA_BLOCK)[:, None]
    kj = jnp.arange(SWA_BLOCK)[None, :]
    dist = jnp.concatenate([(qi + SWA_BLOCK - kj).reshape(-1), (qi - kj).reshape(-1)])
    bucket = _t5_bucket(jnp.maximum(dist, 0))
    return (bucket[None, :] == jnp.arange(REL_BUCKETS)[:, None]).astype(F32)


def _head_spread():
    lane = jnp.arange(LANES)[:, None]
    head = jnp.arange(GDN_W)[None, :] // GDN_HEAD_DIM
    return (lane == head).astype(F32), (lane == head + GDN_HEADS).astype(F32)


def _lane16(v8):
    return jnp.pad(v8.astype(F32), (GDN_HEADS, LANES - 2 * GDN_HEADS)).reshape(1, LANES)


def _stack_heads(t, nb):
    return t.reshape(nb, SWA_BLOCK, SWA_KV_HEADS, SWA_GRP, SWA_HEAD_DIM).transpose(2, 0, 3, 1, 4).reshape(
        SWA_KV_HEADS, nb * SWA_GRP * SWA_BLOCK, SWA_HEAD_DIM)


def _unstack_heads(t, nb):
    return t.reshape(SWA_KV_HEADS, nb, SWA_GRP, SWA_BLOCK, SWA_HEAD_DIM).transpose(1, 3, 0, 2, 4).reshape(nb * SWA_BLOCK, SWA_Q)


def _kv_heads(t):
    return t.reshape(t.shape[0], SWA_KV_HEADS, SWA_HEAD_DIM).transpose(1, 0, 2)


def _swa_specs(qs, ks, vs, bp, bc, sk, grad, gdt=(F32,)):
    T = ks.shape[1]
    qr = SWA_GRP * SWA_BLOCK
    g = lambda a: tuple(a.shape) if grad else None
    nk = SWA_KV_HEADS
    m3 = lambda j, n: (0, n, 0)
    h3 = lambda j, n: (0, jnp.maximum(n - 1, 0), 0)
    p3 = lambda j: (0, 0, 0)
    rows = [Row(qs, (nk, qr, SWA_HEAD_DIM), m3, gshape=g(qs), gmap=m3, gdt=gdt),
            Row(ks, (nk, SWA_BLOCK, SWA_HEAD_DIM), m3, (nk, SWA_BLOCK, SWA_HEAD_DIM), h3, g(ks), m3, gdt),
            Row(vs, (nk, SWA_BLOCK, SWA_HEAD_DIM), m3, (nk, SWA_BLOCK, SWA_HEAD_DIM), h3, g(vs), m3, gdt)]
    pars = [Par(bp, (nk, qr, SWA_BLOCK), p3, g(bp), p3), Par(bc, (nk, qr, SWA_BLOCK), p3, g(bc), p3),
            Par(sk, (nk, qr, 1), p3, g(sk), p3)]
    return rows, pars, T // SWA_BLOCK


class _LocalWeights:
    def __init__(self, W):
        self.W = W

    def w1(self):
        return self.W

    def rider_a(self):
        return None

    def w2(self, got):
        return self.W

    def rider_b(self):
        return None

    def w3(self, got):
        return self.W

    def rider_g(self, G):
        return None

    def g_done(self, got):
        pass

    def rider_up(self, G):
        return None

    def up_done(self, got):
        pass

    def swap_up(self, G):
        return None

    def swap_up_done(self, got):
        pass

    def swap_rest(self, G):
        return None

    def swap_rest_done(self, got):
        pass

    def rider_last(self, G):
        return None

    def last_done(self, got):
        pass


def _fwd_bwd(x, mem, tgt, src):
    W = dict(src.w1())
    T = x.shape[0]
    nb = T // SWA_BLOCK
    tb = min(256, T)
    tbl = min(512, T)
    fwd = lambda f: (lambda *a: (f(*a), []))
    full = lambda cols, dt, t, cw: Out((T, cols), dt, (t, cw), lambda j, n: (n, j))

    xb = x.astype(_CDT)
    ra = src.rider_a()
    proj = _mm("proj", xb, W["in_p"], "nn", rider=ra)
    proj, got = proj if ra is not None else (proj, None)
    W.update(src.w2(got))

    onehot_t = _bias_onehot()
    bias_flat = _mm("swa_bias", W["rel_bias"].T, onehot_t, "nn", hi=True)
    half = SWA_BLOCK * SWA_BLOCK
    bp = bias_flat[:, :half].reshape(SWA_KV_HEADS, SWA_GRP * SWA_BLOCK, SWA_BLOCK)
    bc = bias_flat[:, half:].reshape(SWA_KV_HEADS, SWA_GRP * SWA_BLOCK, SWA_BLOCK)
    sk = jnp.broadcast_to(W["swa_sinks"].reshape(SWA_KV_HEADS, SWA_GRP, 1, 1), (SWA_KV_HEADS, SWA_GRP, SWA_BLOCK, 1)).reshape(
        SWA_KV_HEADS, SWA_GRP * SWA_BLOCK, 1)
    qs = _stack_heads(proj[:, P_SQ:P_SQ + SWA_Q], nb)
    ks = _kv_heads(proj[:, P_SK:P_SK + SWA_KV])
    vs = _kv_heads(proj[:, P_SV:P_SV + SWA_KV])
    rows, pars, nblk = _swa_specs(qs, ks, vs, bp, bc, sk, False)
    o_s, = _rowmap("swa_fwd", fwd(_swa_fn), 1, nblk, rows, pars,
                   [Out(tuple(qs.shape), F32, (SWA_KV_HEADS, SWA_GRP * SWA_BLOCK, SWA_HEAD_DIM), lambda j, n: (0, n, 0))])
    o_swa = _unstack_heads(o_s, nb).astype(_CDT)

    ncq = 3 * GDN_W // LANES
    tbp = min(1024, T)
    pre_rows = lambda grad: [_rowspec(proj, tbp, LANES, P_GQKV // LANES, halo=SUBLANES, grad=grad, ncol=ncq, gdt=(_CDT,))]
    pre_pars = lambda grad: [_parspec(W["gdn_conv_w"], LANES, 0, grad=grad, ncol=ncq)]
    qkv_n, = _rowmap("gdn_pre_fwd", fwd(_gdn_pre_fn), ncq, T // tbp, pre_rows(False), pre_pars(False),
                     [full(3 * GDN_W, F32, tbp, LANES)])
    eb, eg = _head_spread()
    alog_row, dtb_row = _lane16(W["gdn_a_log"]), _lane16(W["gdn_dt_bias"])
    gate_rows = lambda grad: [_rowspec(proj, tbl, LANES, P_BA // LANES, cstep=0, grad=grad, gdt=(_CDT,))]
    gate_pars = lambda grad: [_parspec(alog_row, grad=grad), _parspec(dtb_row, grad=grad), _parspec(eb), _parspec(eg)]
    bx, gx = _rowmap("gdn_gate_fwd", fwd(_gdn_gate_fn), 1, T // tbl, gate_rows(False), gate_pars(False),
                     [full(GDN_W, F32, tbl, GDN_W), full(GDN_W, F32, tbl, GDN_W)])
    nw = W["gdn_norm_w"].reshape(1, GDN_HEAD_DIM)
    o_gdn, states, got = _gdn_chunks_fwd(qkv_n, bx, gx, proj, nw, rider=src.rider_b())
    W.update(src.w3(got))

    ys = _mm("y_swa", o_swa, W["br_swa"], "nn")
    yg = _mm("y_gdn", o_gdn, W["br_gdn"], "nn")
    cwm = 512
    mix_rows = lambda grad: [_rowspec(proj, tb, cwm, P_GS // cwm, grad=grad, ncol=D_MODEL // cwm, gdt=(_CDT,)),
                             _rowspec(proj, tb, cwm, P_GG // cwm, grad=grad, ncol=D_MODEL // cwm, gdt=(_CDT,)),
                             _rowspec(ys, tb, cwm, 0, grad=grad, ncol=D_MODEL // cwm, gdt=(_CDT,)),
                             _rowspec(yg, tb, cwm, 0, grad=grad, ncol=D_MODEL // cwm, gdt=(_CDT,))]
    mixed, = _rowmap("mix_fwd", fwd(_mix_fn), D_MODEL // cwm, T // tb, mix_rows(False), [], [full(D_MODEL, _CDT, tb, cwm)])
    r1 = _mm("r1", mixed, W["mix_o"], "nn", add=x, add_scale=ALPHA)

    def ln_fwd(name, r, g, b):
        return _rowmap(name, _ln_fwd_fn, 1, T // tb, [_rowspec(r, tb, D_MODEL, 0)], [_parspec(g), _parspec(b)],
                       [full(D_MODEL, F32, tb, D_MODEL), full(D_MODEL, _CDT, tb, D_MODEL)])

    def ln_bwd(name, r, g, b, ct):
        return _rowmap_bwd(name, _ln_fn, 1, T // tb, [_rowspec(r, tb, D_MODEL, 0, grad=True, gdt=(F32, _CDT))],
                           [_parspec(g, grad=True), _parspec(b, grad=True)], [_rowspec(ct, tb, D_MODEL, 0)])

    g1, b1 = W["ln1_g"].reshape(1, -1), W["ln1_b"].reshape(1, -1)
    g2, b2 = W["ln2_g"].reshape(1, -1), W["ln2_b"].reshape(1, -1)
    g3, b3 = W["ln3_g"].reshape(1, -1), W["ln3_b"].reshape(1, -1)
    x1, x1b = ln_fwd("ln1_fwd", r1, g1, b1)

    qm = _mm("mem_q", x1b, W["mem_q"], "nn")
    kvm = _mm("mem_kv", mem, W["mem_kv"], "nn")
    ma_rows = lambda grad: [_rowspec(qm, tbl, MEM_HEAD_DIM, 0, grad=grad, ncol=MEM_HEADS, gdt=(_CDT,))]
    ma_pars = lambda grad: [_parspec(kvm, MEM_HEAD_DIM, 0, grad=grad, ncol=MEM_HEADS),
                            _parspec(kvm, MEM_HEAD_DIM, MEM_HEADS, grad=grad, ncol=MEM_HEADS)]
    om, = _rowmap("memattn_fwd", fwd(_memattn_fn), MEM_HEADS, T // tbl, ma_rows(False), ma_pars(False),
                  [full(MEM_W, _CDT, tbl, MEM_HEAD_DIM)])
    r2 = _mm("r2", om, W["mem_o"], "nn", add=x1, add_scale=ALPHA)
    x2, x2b = ln_fwd("ln2_fwd", r2, g2, b2)

    hcat = _mm("ffn_up", x2b, W["up_p"], "nn")
    cwf = 512
    ncf = D_FF_PAD // cwf
    cw_p, cb_p = W["ffn_conv_w_p"], W["ffn_conv_b_p"]
    tbf = min(512, T)
    ffn_rows = lambda grad: [_rowspec(hcat, tbf, cwf, 0, halo=SUBLANES, grad=grad, ncol=ncf, gdt=(_CDT,)),
                             _rowspec(hcat, tbf, cwf, ncf, halo=SUBLANES, grad=grad, ncol=ncf, gdt=(_CDT,))]
    ffn_pars = lambda grad: [_parspec(cw_p, cwf, 0, grad=grad, ncol=ncf), _parspec(cw_p, cwf, ncf, grad=grad, ncol=ncf),
                             _parspec(cb_p, cwf, 0, grad=grad, ncol=ncf), _parspec(cb_p, cwf, ncf, grad=grad, ncol=ncf)]
    act, = _rowmap("ffn_act_fwd", fwd(_ffn_act_fn), ncf, T // tbf, ffn_rows(False), ffn_pars(False), [full(D_FF_PAD, _CDT, tbf, cwf)])
    r3 = _mm("r3", act, W["down_p"], "nn", add=x2, add_scale=ALPHA)
    dr3, dr3b, lacc, dg3, db3 = _rowmap("ln3_loss", _loss_fn, 1, T // tb, [_rowspec(r3, tb, D_MODEL, 0), _rowspec(tgt, tb, D_MODEL, 0)],
                                        [_parspec(g3), _parspec(b3)], [full(D_MODEL, F32, tb, D_MODEL), full(D_MODEL, _CDT, tb, D_MODEL)],
                                  accs=[(SUBLANES, LANES), (1, D_MODEL), (1, D_MODEL)])
    loss = lacc[0, 0]

    G = {}
    G["down_p"] = _mm("dw_down", act, dr3b, "tn", out_dtype=_GDT)
    dact = _mm("d_act", dr3b, W["down_p"], "nt")
    dhg, dhu, dcwg, dcwu, dcbg, dcbu = _rowmap_bwd("ffn_act_bwd", _ffn_act_fn, ncf, T // tbf, ffn_rows(True), ffn_pars(True),
                                                   [_rowspec(dact, tbf, cwf, 0)])
    dx2 = _mm("dx2_gate", dhg, W["up_p"], "nt", add=dr3, add_scale=ALPHA, b_k0=0)
    dx2 = _mm("dx2_up", dhu, W["up_p"], "nt", add=dx2, b_k0=D_FF_PAD)
    G["up_p"] = jnp.concatenate([_mm("dw_gate", x2b, dhg, "tn", out_dtype=_GDT), _mm("dw_up", x2b, dhu, "tn", out_dtype=_GDT)], axis=1)
    G["ffn_conv_w"] = jnp.concatenate([dcwg[:, :D_FF], dcwu[:, :D_FF]], axis=1)
    G["ffn_conv_b"] = jnp.concatenate([dcbg[0, :D_FF], dcbu[0, :D_FF]])
    G["ln3_g"], G["ln3_b"] = dg3[0], db3[0]

    dr2, dr2b, dg2, db2 = ln_bwd("ln2_bwd", r2, g2, b2, dx2)
    G["ln2_g"], G["ln2_b"] = dg2[0], db2[0]
    G["mem_o"] = _mm("dw_mem_o", om, dr2b, "tn", out_dtype=_GDT)
    dom = _mm("d_om", dr2b, W["mem_o"], "nt", out_dtype=_CDT)
    dqm, dkm, dvm = _rowmap_bwd("memattn_bwd", _memattn_fn, MEM_HEADS, T // tbl, ma_rows(True), ma_pars(True),
                                [_rowspec(dom, tbl, MEM_HEAD_DIM, 0)])
    G["mem_kv"] = _mm("dw_mem_kv", mem.astype(_CDT), jnp.concatenate([dkm, dvm], axis=1).astype(_CDT), "tn", out_dtype=_GDT)
    G["mem_q"] = _mm("dw_mem_q", x1b, dqm, "tn", out_dtype=_GDT)
    dx1 = _mm("dx1", dqm, W["mem_q"], "nt", add=dr2, add_scale=ALPHA)

    dr1, dr1b, dg1, db1 = ln_bwd("ln1_bwd", r1, g1, b1, dx1)
    G["ln1_g"], G["ln1_b"] = dg1[0], db1[0]
    G["mix_o"] = _mm("dw_mix_o", mixed, dr1b, "tn", out_dtype=_GDT)
    rs = src.swap_up(G)
    dmixed = _mm("d_mixed", dr1b, W["mix_o"], "nt", rider=rs)
    if rs is not None:
        dmixed, got = dmixed
        src.swap_up_done(got)
    dgs, dgg, dys, dyg = _rowmap_bwd("mix_bwd", _mix_fn, D_MODEL // cwm, T // tb, mix_rows(True), [], [_rowspec(dmixed, tb, cwm, 0)])
    G["br_swa"] = _mm("dw_br_swa", o_swa, dys, "tn", out_dtype=_GDT)
    G["br_gdn"] = _mm("dw_br_gdn", o_gdn, dyg, "tn", out_dtype=_GDT)
    do_swa = _mm("d_o_swa", dys, W["br_swa"], "nt", out_dtype=_CDT)
    rs = src.swap_rest(G)
    do_gdn = _mm("d_o_gdn", dyg, W["br_gdn"], "nt", rider=rs)
    if rs is not None:
        do_gdn, got = do_gdn
        src.swap_rest_done(got)

    rows, pars, nblk = _swa_specs(qs, ks, vs, bp, bc, sk, True, (_CDT,))
    m3 = lambda j, n: (0, n, 0)
    dqs, dks, dvs, dbp, dbc, dsk = _rowmap_bwd("swa_bwd", _swa_fn, 1, nblk, rows, pars,
                                               [Row(_stack_heads(do_swa, nb), (SWA_KV_HEADS, SWA_GRP * SWA_BLOCK, SWA_HEAD_DIM), m3)])
    d_swa = jnp.concatenate([_unstack_heads(dqs, nb), dks.transpose(1, 0, 2).reshape(T, SWA_KV),
                             dvs.transpose(1, 0, 2).reshape(T, SWA_KV)], axis=1)
    dbias = jnp.concatenate([dbp.reshape(SWA_HEADS, half), dbc.reshape(SWA_HEADS, half)], axis=1)
    G["rel_bias"] = _mm("d_rel_bias", dbias, onehot_t.T, "nn", hi=True).T
    G["swa_sinks"] = _mm("d_sinks", dsk.reshape(SWA_HEADS, SWA_BLOCK), jnp.ones((SWA_BLOCK, LANES), F32), "nn", hi=True)[:, 0]

    dqkv_n, dbx, dgx, dz, dnw, got = _gdn_chunks_bwd(qkv_n, bx, gx, proj, nw, states, do_gdn, rider=src.rider_g(G))
    src.g_done(got)
    G["gdn_norm_w"] = dnw[0]
    dgba, dalog, ddtb = _rowmap_bwd("gdn_gate_bwd", _gdn_gate_fn, 1, T // tbl, gate_rows(True), gate_pars(True),
                                    [_rowspec(dbx, tbl, GDN_W, 0), _rowspec(dgx, tbl, GDN_W, 0)])
    G["gdn_a_log"], G["gdn_dt_bias"] = dalog[0, GDN_HEADS:2 * GDN_HEADS], ddtb[0, GDN_HEADS:2 * GDN_HEADS]
    dgqkv, dcw_gdn = _rowmap_bwd("gdn_pre_bwd", _gdn_pre_fn, ncq, T // tbp, pre_rows(True), pre_pars(True),
                                 [_rowspec(dqkv_n, tbp, LANES, 0)])
    G["gdn_conv_w"] = dcw_gdn

    dproj = jnp.concatenate([dgs, dgg, dgqkv, dz, d_swa, dgba, jnp.zeros((T, P_END - P_USED), _CDT)], axis=1)
    ru = src.rider_up(G)
    G["in_p"] = _mm("dw_in", xb, dproj, "tn", out_dtype=_GDT, rider=ru)
    if ru is not None:
        G["in_p"], got = G["in_p"]
        src.up_done(got)
    rl = src.rider_last(G)
    dx = _mm("dx", dproj, W["in_p"], "nt", add=dr1, add_scale=ALPHA, rider=rl)
    if rl is not None:
        dx, got = dx
        src.last_done(got)
    return loss, dx, G


W_NAMES = ["w_in", "rel_bias", "swa_sinks", "gdn_conv_w", "gdn_a_log", "gdn_dt_bias", "gdn_norm_w", "w_br_swa", "w_br_gdn",
           "w_mix_o", "ln1_g", "ln1_b", "w_mem_q", "w_mem_kv", "w_mem_o", "ln2_g", "ln2_b", "w_up", "ffn_conv_w", "ffn_conv_b",
           "w_down", "ln3_g", "ln3_b"]
BIG = ["w_in", "w_br_swa", "w_br_gdn", "w_mix_o", "w_mem_q", "w_mem_kv", "w_mem_o", "w_up", "w_down"]
SMALL = [n for n in W_NAMES if n not in BIG]
COL_SHARDED = ["w_in", "w_br_swa", "w_br_gdn", "w_mem_o", "w_up"]


def _pack(arrs):
    rows = []
    for a in arrs:
        f = a.reshape(-1).astype(F32)
        rows.append(jnp.pad(f, (0, (-f.shape[0]) % LANES)).reshape(-1, LANES))
    n = sum(r.shape[0] for r in rows)
    if n % 16:
        rows.append(jnp.zeros((16 - n % 16, LANES), F32))
    return jnp.concatenate(rows, axis=0)


def _unpack(p, shapes):
    out, off = [], 0
    for s in shapes:
        n = int(np.prod(s)) if len(s) else 1
        r = -(-n // LANES)
        out.append(p[off:off + r].reshape(-1)[:n].reshape(s))
        off += r
    return out


def _merge_shards(d):
    cat = lambda names: jnp.concatenate([d[n] for n in names], axis=-2)
    return [d.get("w_in"), d["w_up"], cat(["w_br_swa", "w_br_gdn", "w_mem_q", "w_mem_o"]), cat(["w_mix_o", "w_down"]), d["w_mem_kv"]]


def _split_shards(ts):
    a, b, c, dd, e = ts
    return {"w_in": a, "w_up": b, "w_br_swa": c[..., 0:1024, :], "w_br_gdn": c[..., 1024:2048, :], "w_mem_q": c[..., 2048:2560, :],
            "w_mem_o": c[..., 2560:3072, :], "w_mix_o": dd[..., 0:512, :], "w_down": dd[..., 512:, :], "w_mem_kv": e}


def _to_full(name, t):
    if name in COL_SHARDED:
        return _cols_from_chips(t, [(0, 4 * t.shape[2])])
    return t.reshape(4 * t.shape[1], t.shape[2])


def _to_chips(name, t):
    if name in COL_SHARDED:
        return _chips_from_cols(t, [(0, t.shape[1])], t.shape[1] // 4)
    return t.reshape(4, t.shape[0] // 4, t.shape[1])


def _cols_from_chips(g, segs, own=None):
    C, parts = g.shape[2], []
    for s in segs:
        if isinstance(s, int):
            parts.append(jnp.zeros((g.shape[1], s), g.dtype))
            continue
        lo, hi = s
        while lo < hi:
            k = lo // C
            e = min(hi, (k + 1) * C)
            piece = g[k][:, lo - k * C:e - k * C]
            parts.append(piece if own is None else jnp.where(own[1] == k, own[0][:, lo - k * C:e - k * C], piece))
            lo = e
    return jnp.concatenate(parts, axis=1)


def _chips_from_cols(p, segs, C):
    out = []
    for k in range(4):
        lo, hi, parts, o = k * C, (k + 1) * C, [], 0
        for plo, w in segs:
            a, b = max(lo, o), min(hi, o + w)
            if a < b:
                parts.append(p[:, plo + a - o:plo + b - o])
            o += w
        out.append(jnp.concatenate(parts, axis=1))
    return jnp.stack(out)


_IN_OFF = np.cumsum((0,) + IN_WIDTHS)
_IN_SEGS = [(P_SQ, SWA_Q), (P_SK, SWA_KV), (P_SV, SWA_KV), (P_GQKV, 3 * GDN_W), (P_GZ, GDN_W), (P_BA, 2 * GDN_HEADS),
            (P_GS, D_MODEL), (P_GG, D_MODEL)]
_IN_PADDED = [(int(_IN_OFF[i]), int(_IN_OFF[k])) for i, k in ((9, 10), (10, 11), (3, 6), (6, 7), (0, 1), (1, 2), (2, 3), (7, 9))] + [
    P_END - P_BA - 2 * GDN_HEADS]
_UP_SEGS = [(0, D_FF), (D_FF_PAD, D_FF)]
_UP_PADDED = [(0, D_FF), D_FF_PAD - D_FF, (D_FF, 2 * D_FF), D_FF_PAD - D_FF]


def _in_to_padded(w):
    o = _IN_OFF
    cut = lambda i, k: w[:, o[i]:o[k]]
    return jnp.concatenate([cut(9, 10), cut(10, 11), cut(3, 6), cut(6, 7), cut(0, 1), cut(1, 2), cut(2, 3), cut(7, 9),
                            jnp.zeros((w.shape[0], P_END - P_BA - 2 * GDN_HEADS), w.dtype)], axis=1)


def _in_from_padded(p):
    return jnp.concatenate([p[:, P_SQ:P_SQ + SWA_Q], p[:, P_SK:P_SK + SWA_KV], p[:, P_SV:P_SV + SWA_KV], p[:, P_GQKV:P_GQKV + 3 * GDN_W],
                            p[:, P_GZ:P_GZ + GDN_W], p[:, P_BA:P_BA + 2 * GDN_HEADS], p[:, P_GS:P_GS + D_MODEL], p[:, P_GG:P_GG + D_MODEL]],
                           axis=1)


def _ff_pad(t, axis):
    g, u = jnp.split(t, 2, axis=axis)
    pad = [(0, 0)] * t.ndim
    pad[axis] = (0, D_FF_PAD - D_FF)
    return jnp.concatenate([jnp.pad(g, pad), jnp.pad(u, pad)], axis=axis)


def _ff_unpad(t, axis):
    g, u = jnp.split(t, 2, axis=axis)
    return jnp.concatenate([lax.slice_in_dim(g, 0, D_FF, axis=axis), lax.slice_in_dim(u, 0, D_FF, axis=axis)], axis=axis)


def _assemble_weights(full, small):
    W = dict(small)
    W["in_p"] = _in_to_padded(full["w_in"])
    W["up_p"] = _ff_pad(full["w_up"], 1)
    W["down_p"] = jnp.pad(full["w_down"], ((0, D_FF_PAD - D_FF), (0, 0)))
    W["br_swa"], W["br_gdn"], W["mix_o"] = full["w_br_swa"], full["w_br_gdn"], full["w_mix_o"]
    W["mem_q"], W["mem_kv"], W["mem_o"] = full["w_mem_q"], full["w_mem_kv"], full["w_mem_o"]
    W["ffn_conv_w_p"] = _ff_pad(small["ffn_conv_w"], 1)
    W["ffn_conv_b_p"] = _ff_pad(small["ffn_conv_b"].reshape(1, -1), 1)
    return W


def _full_grads(G):
    out = {"w_in": _in_from_padded(G["in_p"])} if "in_p" in G else {}
    out.update({"w_up": _ff_unpad(G["up_p"], 1), "w_down": G["down_p"][:D_FF], "w_br_swa": G["br_swa"], "w_br_gdn": G["br_gdn"],
                "w_mix_o": G["mix_o"], "w_mem_q": G["mem_q"], "w_mem_kv": G["mem_kv"], "w_mem_o": G["mem_o"]})
    return out


def kernel(x, mem, w_in, rel_bias, swa_sinks, gdn_conv_w, gdn_a_log, gdn_dt_bias, gdn_norm_w, w_br_swa, w_br_gdn, w_mix_o, ln1_g, ln1_b, w_mem_q, w_mem_kv, w_mem_o, ln2_g, ln2_b, w_up, ffn_conv_w, ffn_conv_b, w_down, ln3_g, ln3_b, loss_target, m_w_in, m_rel_bias, m_swa_sinks, m_gdn_conv_w, m_gdn_a_log, m_gdn_dt_bias, m_gdn_norm_w, m_w_br_swa, m_w_br_gdn, m_w_mix_o, m_ln1_g, m_ln1_b, m_w_mem_q, m_w_mem_kv, m_w_mem_o, m_ln2_g, m_ln2_b, m_w_up, m_ffn_conv_w, m_ffn_conv_b, m_w_down, m_ln3_g, m_ln3_b, v_w_in, v_rel_bias, v_swa_sinks, v_gdn_conv_w, v_gdn_a_log, v_gdn_dt_bias, v_gdn_norm_w, v_w_br_swa, v_w_br_gdn, v_w_mix_o, v_ln1_g, v_ln1_b, v_w_mem_q, v_w_mem_kv, v_w_mem_o, v_ln2_g, v_ln2_b, v_w_up, v_ffn_conv_w, v_ffn_conv_b, v_w_down, v_ln3_g, v_ln3_b):
    a = dict(locals())
    w = {n: a[n] for n in W_NAMES}
    m = {n: a["m_" + n] for n in W_NAMES}
    v = {n: a["v_" + n] for n in W_NAMES}
    chip = 2 * lax.axis_index("x") + lax.axis_index("y")
    core = lax.axis_index("c")
    sq = lambda t: t.reshape(t.shape[1:]) if (t.ndim > 1 and t.shape[0] == 1 and t is not rel_bias) else t

    sh_a, sh_b, sh_c, sh_d, sh_e = _merge_shards({n: sq(w[n]).astype(_CDT) for n in BIG})
    fcw_sh, gcw_sh = sq(ffn_conv_w).shape, sq(gdn_conv_w).shape
    slot = lax.broadcasted_iota(jnp.int32, (4, 1, 1), 0)

    def with_own(got, mine):
        return [jnp.where(slot == chip, t[None], g) for g, t in zip(got, mine)]

    def reduce_start(tag, gch, theirs=None):
        pair = []
        theirs = _run_rider("pair_swap_" + tag, _swap_rider(gch)) if theirs is None else theirs
        for t, (mine, got) in enumerate(zip(gch, theirs)):
            rh = mine.shape[1] // 2
            mine_h = lax.dynamic_slice_in_dim(mine, core * rh, rh, axis=1)
            pair.append(_addn(f"pair_sum_{tag}{t}", [mine_h.reshape(4 * rh, -1), got.reshape(4 * rh, -1)], _GDT).reshape(4, rh, -1))
        return pair

    def reduce_end(tag, pair, others):
        halves = []
        for t, (p, o) in enumerate(zip(pair, others)):
            own = lax.dynamic_index_in_dim(p, chip, 0, keepdims=False)
            halves.append(_addn(f"chip_sum_{tag}{t}", [own, (o, 0), (o, 1), (o, 2)]))
        return halves

    class MeshWeights:
        def w1(self):
            mine = [sh_a, _pack([sq(ffn_conv_w), sq(gdn_conv_w)])]
            got_a, got_f = _run_rider("gather_first", _gather_rider(mine))
            got_f, = with_own([got_f], mine[1:])
            conv = [_unpack(got_f[k], [fcw_sh, gcw_sh]) for k in range(4)]
            W = {n: sq(w[n]) for n in SMALL}
            W["ffn_conv_w"] = jnp.concatenate([cv[0] for cv in conv], axis=1)
            W["gdn_conv_w"] = jnp.concatenate([cv[1] for cv in conv], axis=1)
            W["ffn_conv_w_p"] = _ff_pad(W["ffn_conv_w"], 1)
            W["ffn_conv_b_p"] = _ff_pad(W["ffn_conv_b"].reshape(1, -1), 1)
            W["in_p"] = _cols_from_chips(got_a, _IN_PADDED, own=(sh_a, chip))
            return W

        def rider_a(self):
            return _gather_rider([sh_c, sh_d, sh_e])

        def w2(self, got):
            c, d, e = with_own(got, [sh_c, sh_d, sh_e])
            f = {n: _to_full(n, t) for n, t in _split_shards([None, None, c, d, e]).items() if t is not None}
            return {"br_swa": f["w_br_swa"], "br_gdn": f["w_br_gdn"], "mix_o": f["w_mix_o"], "mem_q": f["w_mem_q"], "mem_kv": f["w_mem_kv"],
                    "mem_o": f["w_mem_o"], "down_p": jnp.pad(f["w_down"], ((0, D_FF_PAD - D_FF), (0, 0)))}

        def rider_b(self):
            return _gather_rider([sh_b])

        def w3(self, got):
            return {"up_p": _cols_from_chips(got[0], _UP_PADDED, own=(sh_b, chip))}

        def swap_rest(self, G):
            gf = _full_grads(G)
            gch = {n: _to_chips(n, gf[n]) for n in BIG if n not in ("w_in", "w_up")}
            gch["w_up"] = None
            self.gch_rest = _merge_shards(gch)[2:]
            return _swap_rider(self.gch_rest)

        def swap_rest_done(self, got):
            self.theirs_rest = got

        def rider_g(self, G):
            self.pair = reduce_start("rest", self.gch_rest, self.theirs_rest)
            return _scatter_rider(self.pair)

        def g_done(self, got):
            self.halves = reduce_end("rest", self.pair, got)

        def swap_up(self, G):
            self.gch_up = [_chips_from_cols(G["up_p"], _UP_SEGS, 2 * D_FF // 4)]
            return _swap_rider(self.gch_up)

        def swap_up_done(self, got):
            self.theirs_up = got

        def rider_up(self, G):
            self.pair_up = reduce_start("up", self.gch_up, self.theirs_up)
            return _scatter_rider(self.pair_up)

        def up_done(self, got):
            self.halves = reduce_end("up", self.pair_up, got) + self.halves

        def rider_last(self, G):
            self.pair_in = reduce_start("in", [_chips_from_cols(G["in_p"], _IN_SEGS, sum(IN_WIDTHS) // 4)])
            return _scatter_rider(self.pair_in)

        def last_done(self, got):
            self.halves = reduce_end("in", self.pair_in, got) + self.halves

    src = MeshWeights()
    loss, dx, G = _fwd_bwd(x[0], mem[0], loss_target[0], src)

    small_names = SMALL
    small_shapes = [()] + [tuple(G[n].shape) for n in small_names]
    packed = _pack([loss] + [G[n] for n in small_names])
    allp = _allgather8(packed)
    tot = _addn("small_sum", [(allp, k) for k in range(8)])
    parts = _unpack(tot, small_shapes)
    loss_tot, gsmall = parts[0], dict(zip(small_names, parts[1:]))
    gsmall["ffn_conv_w"] = lax.dynamic_slice_in_dim(gsmall["ffn_conv_w"], chip * fcw_sh[1], fcw_sh[1], axis=1)
    gsmall["gdn_conv_w"] = lax.dynamic_slice_in_dim(gsmall["gdn_conv_w"], chip * gcw_sh[1], gcw_sh[1], axis=1)

    both = []
    for h, o in zip(src.halves, _pair_exchange(src.halves)):
        both.append(jnp.concatenate([jnp.where(core == 0, h, o), jnp.where(core == 0, o, h)], axis=0))
    gbig = _split_shards(both)

    outs = {}
    for n in BIG:
        if w[n].shape[-1] % LANES:
            cols_out = lambda t: jnp.transpose(t, (2, 0, 1))
            g_ = jnp.transpose(gbig[n])[:, None, :]
            d_, m_, v_ = _adamw("adamw_" + n, cols_out(w[n]), g_, cols_out(m[n]), cols_out(v[n]))
            outs[n] = tuple(jnp.transpose(t, (1, 2, 0)) for t in (g_, d_, m_, v_))
            continue
        d_, m_, v_ = _adamw("adamw_" + n, sq(w[n]), gbig[n], sq(m[n]), sq(v[n]))
        outs[n] = (gbig[n], d_, m_, v_)
    for n in SMALL:
        two_d = (-1, w[n].shape[-1])
        g_ = gsmall[n].reshape(two_d)
        d_, m_, v_ = _adamw("adamw_" + n, w[n].reshape(two_d), g_, m[n].reshape(two_d), v[n].reshape(two_d))
        outs[n] = (g_, d_, m_, v_)

    res = [loss_tot.reshape(()), dx.reshape(x.shape)]
    for k in range(4):
        res += [outs[n][k].reshape(w[n].shape) for n in W_NAMES]
    return tuple(res)
```

```python
import functools
import math

import jax
import jax.numpy as jnp
import numpy as np
from jax import lax
from jax.experimental import pallas as pl
from jax.experimental.pallas import tpu as pltpu

F32 = jnp.float32
BF16 = jnp.bfloat16
_CDT = BF16
_GDT = BF16

D_MODEL = 2048
SWA_HEADS, SWA_KV_HEADS, SWA_HEAD_DIM, SWA_BLOCK = 16, 2, 64, 128
SWA_GRP = SWA_HEADS // SWA_KV_HEADS
REL_BUCKETS, REL_MAX_DIST = 32, 128
GDN_HEADS, GDN_HEAD_DIM, GDN_CONV, GDN_CHUNK = 8, 128, 4, 64
MEM_HEADS, MEM_HEAD_DIM = 4, 128
D_FF, D_FF_PAD, FFN_CONV = 5504, 5632, 3
SWA_Q, SWA_KV, GDN_W, MEM_W = 1024, 128, 1024, 512
IN_WIDTHS = (SWA_Q, SWA_KV, SWA_KV, GDN_W, GDN_W, GDN_W, GDN_W, GDN_HEADS, GDN_HEADS, D_MODEL, D_MODEL)
NORM_EPS = 1e-5
ALPHA = 2.0 ** 0.25
NEG_INF = -1e30
ADAM_LR, ADAM_B1, ADAM_B2, ADAM_EPS, ADAM_WD, ADAM_STEP = 0.001, 0.9, 0.999, 1e-08, 0.01, 10
LANES, SUBLANES = 128, 8
VMEM_LIMIT = 56 * 1024 * 1024

P_GS, P_GG, P_GQKV, P_GZ, P_SQ, P_SK, P_SV, P_BA, P_USED, P_END = 0, 2048, 4096, 7168, 8192, 9216, 9344, 9472, 9600, 9728


def _tile(dim, pref, align=LANES):
    if dim <= pref:
        return dim
    t = (pref // align) * align
    while t >= align:
        if dim % t == 0:
            return t
        t -= align
    return dim


_DIMS = {"nn": (((1,), (0,)), ((), ())), "nt": (((1,), (1,)), ((), ())), "tn": (((0,), (0,)), ((), ()))}
_BDIMS = {"nn": (((2,), (1,)), ((0,), (0,))), "nt": (((2,), (2,)), ((0,), (0,))), "tn": (((1,), (1,)), ((0,), (0,)))}


def _raw_dot(a, b, form, hi):
    dims = (_BDIMS if a.ndim == 3 else _DIMS)[form]
    if hi == "x3":
        a, b = a.astype(F32), b.astype(F32)
        ah, bh = a.astype(BF16), b.astype(BF16)
        al, bl = (a - ah.astype(F32)).astype(BF16), (b - bh.astype(F32)).astype(BF16)
        d = lambda p, q: lax.dot_general(p, q, dims, preferred_element_type=F32)
        if form == "tn":
            return d(ah, bh) + (d(ah, bl) + d(al, bh))
        m = a.shape[-2]
        both = d(jnp.concatenate([ah, al], axis=-2), bh)
        return both[..., :m, :] + (d(ah, bl) + both[..., m:, :])
    if hi:
        return lax.dot_general(a.astype(F32), b.astype(F32), dims, precision=lax.Precision.HIGHEST, preferred_element_type=F32)
    return lax.dot_general(a.astype(_CDT), b.astype(_CDT), dims, preferred_element_type=F32)


@functools.partial(jax.custom_vjp, nondiff_argnums=(2, 3))
def _dot(a, b, form, hi=False):
    return _raw_dot(a, b, form, hi)


def _dot_fwd(a, b, form, hi):
    return _raw_dot(a, b, form, hi), (a, b)


def _dot_bwd(form, hi, res, g):
    a, b = res
    if form == "nn":
        da, db = _raw_dot(g, b, "nt", hi), _raw_dot(a, g, "tn", hi)
    elif form == "nt":
        da, db = _raw_dot(g, b, "nn", hi), _raw_dot(g, a, "tn", hi)
    else:
        da, db = _raw_dot(b, g, "nt", hi), _raw_dot(a, g, "nn", hi)
    return da.astype(a.dtype), db.astype(b.dtype)


_dot.defvjp(_dot_fwd, _dot_bwd)


@functools.partial(jax.custom_vjp, nondiff_argnums=(2,))
def _shift_halo(prev, cur, d):
    assert prev.shape[0] == SUBLANES
    return pltpu.roll(jnp.concatenate([prev, cur], axis=0), d, 0)[SUBLANES:]


def _shift_halo_fwd(prev, cur, d):
    return _shift_halo(prev, cur, d), None


def _shift_halo_bwd(d, _, g):
    nh = SUBLANES
    ext = jnp.concatenate([jnp.zeros((nh, g.shape[1]), g.dtype), g], axis=0)
    r = pltpu.roll(ext, ext.shape[0] - d, 0)
    return r[:nh], r[nh:]


_shift_halo.defvjp(_shift_halo_fwd, _shift_halo_bwd)


@jax.custom_vjp
def _recip(x):
    return 1.0 / x


def _recip_fwd(x):
    r = 1.0 / x
    return r, r


def _recip_bwd(r, g):
    return (-g * r * r,)


_recip.defvjp(_recip_fwd, _recip_bwd)


def _sigmoid(x):
    return _recip(1.0 + jnp.exp(-x))


def _silu(x):
    return x * _sigmoid(x)


def _softplus(x):
    return jnp.maximum(x, 0.0) + jnp.log(1.0 + jnp.exp(-jnp.abs(x)))


def _iota(shape, axis):
    return lax.broadcasted_iota(jnp.int32, shape, axis)


def _cparams(sem, **kw):
    return pltpu.CompilerParams(dimension_semantics=sem, vmem_limit_bytes=VMEM_LIMIT, **kw)


class _ride:
    def __init__(self, rider, n_in, n_out, n_scr):
        self.rider = rider
        self.ins = rider.ins if rider else []
        n_rin = len(self.ins)
        self.out_shapes = rider.out_shapes if rider else []
        n_rout = len(self.out_shapes)
        self.in_specs, self.out_specs = [_HBM] * n_rin, [_HBM] * n_rout
        self.scratch = rider.sems() if rider else []
        self.o0 = n_in + n_rin
        self.s0 = self.o0 + n_out + n_rout
        self._rin = slice(n_in, n_in + n_rin)
        self._rout = slice(self.o0 + n_out, self.s0)
        self._sem = self.s0 + n_scr

    def _args(self, refs):
        return refs[self._rin], refs[self._rout], refs[self._sem], refs[self._sem + 1]

    def at_start(self, refs, cond):
        if self.rider:
            pl.when(cond)(lambda: self.rider.start(*self._args(refs)))

    def at_end(self, refs, cond):
        if self.rider:
            pl.when(cond)(lambda: self.rider.finish(*self._args(refs)))


def _mm(name, a, b, form, out_dtype=F32, add=None, add_scale=1.0, hi=False, tm=None, tn=None, tk=2816, rider=None, b_k0=None):
    if form == "nn":
        (M, K), (K2, N) = a.shape, b.shape
    elif form == "nt":
        (M, K), (N, K2) = a.shape, b.shape
        K2 = K if b_k0 is not None else K2
    else:
        (K, M), (K2, N) = a.shape, b.shape
    assert K == K2, (name, a.shape, b.shape, form)
    tm = tm or (512 if form == "tn" else 1024)
    tn = tn or (2816 if form == "tn" else 1024)
    tm, tn, tk = _tile(M, tm), _tile(N, tn), _tile(K, tk)
    nk = K // tk
    k0 = 0 if b_k0 is None else b_k0 // tk
    assert b_k0 is None or (form == "nt" and b_k0 % tk == 0)
    a_spec = pl.BlockSpec((tk, tm), lambda i, j, k: (k, i)) if form == "tn" else pl.BlockSpec((tm, tk), lambda i, j, k: (i, k))
    b_spec = pl.BlockSpec((tn, tk), lambda i, j, k: (j, k + k0)) if form == "nt" else pl.BlockSpec((tk, tn), lambda i, j, k: (k, j))
    o_spec = pl.BlockSpec((tm, tn), lambda i, j, k: (i, j))
    has_add = add is not None

    def finish(r, c_ref, o_ref):
        if has_add:
            r = r + add_scale * c_ref[...].astype(F32)
        o_ref[...] = r.astype(out_dtype)

    n_own = 3 if has_add else 2
    grid = (M // tm, N // tn, nk)
    rd = _ride(rider, n_own, 1, 1 if nk > 1 else 0)

    def body(*refs):
        a_ref, b_ref = refs[:2]
        c_ref = refs[2] if has_add else None
        o_ref = refs[rd.o0]
        pid = [pl.program_id(d) for d in range(3)]
        rd.at_start(refs, (pid[0] == 0) & (pid[1] == 0) & (pid[2] == 0))
        if nk == 1:
            finish(_raw_dot(a_ref[...], b_ref[...], form, hi), c_ref, o_ref)
        else:
            acc = refs[rd.s0]

            @pl.when(pid[2] == 0)
            def _():
                acc[...] = jnp.zeros_like(acc)

            acc[...] += _raw_dot(a_ref[...], b_ref[...], form, hi)

            @pl.when(pid[2] == nk - 1)
            def _():
                finish(acc[...], c_ref, o_ref)
        rd.at_end(refs, (pid[0] == grid[0] - 1) & (pid[1] == grid[1] - 1) & (pid[2] == nk - 1))

    ins = [a, b] + ([add] if has_add else [])
    specs = [a_spec, b_spec] + ([o_spec] if has_add else [])
    res = pl.pallas_call(
        body, name=name, grid=grid, in_specs=specs + rd.in_specs, out_specs=[o_spec] + rd.out_specs,
        out_shape=[jax.ShapeDtypeStruct((M, N), out_dtype)] + rd.out_shapes,
        scratch_shapes=([pltpu.VMEM((tm, tn), F32)] if nk > 1 else []) + rd.scratch,
        compiler_params=_cparams(("arbitrary",) * 3 if rider else ("parallel", "parallel", "arbitrary")),
    )(*ins, *rd.ins)
    return (res[0], res[1:]) if rider else res[0]


class Row:
    def __init__(self, arr, blk, imap, hblk=None, hmap=None, gshape=None, gmap=None, gdt=(F32,)):
        self.arr, self.blk, self.imap, self.hblk, self.hmap, self.gshape, self.gmap = arr, blk, imap, hblk, hmap, gshape, gmap
        self.gdt = gdt


class Par:
    def __init__(self, arr, blk=None, imap=None, gshape=None, gmap=None):
        self.arr = arr
        self.blk = tuple(arr.shape) if blk is None else blk
        nd = len(self.blk)
        self.imap = (lambda j: (0,) * nd) if imap is None else imap
        self.gshape, self.gmap = gshape, gmap


class Out:
    def __init__(self, shape, dtype, blk, imap):
        self.shape, self.dtype, self.blk, self.imap = shape, dtype, blk, imap


def _rows_of(blk):
    return [d for d in blk if d is not None][0]


def _rowmap(name, fn, ncol, nblk, rows, pars, outs, accs=()):
    in_specs, ins = [], []
    for r in rows:
        ins.append(r.arr)
        in_specs.append(pl.BlockSpec(r.blk, r.imap))
        if r.hblk is not None:
            ins.append(r.arr)
            in_specs.append(pl.BlockSpec(r.hblk, r.hmap))
    for p in pars:
        ins.append(p.arr)
        in_specs.append(pl.BlockSpec(p.blk, (lambda im: (lambda j, n: im(j)))(p.imap)))
    out_specs = [pl.BlockSpec(o.blk, o.imap) for o in outs]
    out_shape = [jax.ShapeDtypeStruct(o.shape, o.dtype) for o in outs]
    for a in accs:
        out_specs.append(pl.BlockSpec(a, (lambda nd: (lambda j, n: (0,) * nd))(len(a))))
        out_shape.append(jax.ShapeDtypeStruct(a, F32))
    n_in = len(ins)

    def body(*refs):
        j, n = pl.program_id(0), pl.program_id(1)
        it = iter(refs[:n_in])
        rvals = []
        for r in rows:
            cur = next(it)[...]
            rvals.append((next(it)[...], cur) if r.hblk is not None else cur)
        pvals = [next(it)[...] for _ in pars]
        o_refs = refs[n_in:n_in + len(outs)]
        a_refs = refs[n_in + len(outs):]
        ovals, avals = fn(j, n == 0, rvals, pvals)
        for ref, v in zip(o_refs, ovals):
            ref[...] = v.astype(ref.dtype)
        if accs:
            @pl.when((j == 0) & (n == 0))
            def _():
                for ref in a_refs:
                    ref[...] = jnp.zeros_like(ref)
            for ref, v in zip(a_refs, avals):
                ref[...] += v

    res = pl.pallas_call(
        body, name=name, grid=(ncol, nblk), in_specs=in_specs, out_specs=out_specs, out_shape=out_shape,
        compiler_params=_cparams(("arbitrary", "arbitrary")),
    )(*ins)
    return res


def _rowmap_bwd(name, fn, ncol, nblk, rows, pars, cts):
    rev = lambda im: (lambda j, s: im(j, nblk - 1 - s))
    in_specs, ins = [], []
    for r in rows:
        ins.append(r.arr)
        in_specs.append(pl.BlockSpec(r.blk, rev(r.imap)))
        if r.hblk is not None:
            ins.append(r.arr)
            in_specs.append(pl.BlockSpec(r.hblk, rev(r.hmap)))
    for p in pars:
        ins.append(p.arr)
        in_specs.append(pl.BlockSpec(p.blk, (lambda im: (lambda j, s: im(j)))(p.imap)))
    for c in cts:
        ins.append(c.arr)
        in_specs.append(pl.BlockSpec(c.blk, rev(c.imap)))
    n_in = len(ins)
    drows = [i for i, r in enumerate(rows) if r.gshape is not None]
    dpars = [i for i, p in enumerate(pars) if p.gshape is not None]
    out_specs, out_shape, scratch = [], [], []
    for i in drows:
        r = rows[i]
        for dt in r.gdt:
            out_specs.append(pl.BlockSpec(r.blk, rev(r.gmap)))
            out_shape.append(jax.ShapeDtypeStruct(r.gshape, dt))
        if r.hblk is not None:
            scratch.append(pltpu.VMEM(tuple(d for d in r.hblk if d is not None), F32))
    n_drow_out = len(out_specs)
    for i in dpars:
        p = pars[i]
        out_specs.append(pl.BlockSpec(p.blk, (lambda im: (lambda j, s: im(j)))(p.gmap)))
        out_shape.append(jax.ShapeDtypeStruct(p.gshape, F32))

    def body(*refs):
        j, s = pl.program_id(0), pl.program_id(1)
        first = s == nblk - 1
        it = iter(refs[:n_in])
        rvals = []
        for r in rows:
            cur = next(it)[...]
            rvals.append((next(it)[...], cur) if r.hblk is not None else cur)
        pvals = [next(it)[...] for _ in pars]
        cvals = [next(it)[...].astype(F32) for _ in cts]
        g_refs = iter(refs[n_in:n_in + n_drow_out])
        p_refs = refs[n_in + n_drow_out:n_in + n_drow_out + len(dpars)]
        carries = iter(refs[n_in + n_drow_out + len(dpars):])

        def f(dr, dp):
            rv, pv = list(rvals), list(pvals)
            for i, v in zip(drows, dr):
                rv[i] = v
            for i, v in zip(dpars, dp):
                pv[i] = v
            return fn(j, first, rv, pv)

        _, vjp = jax.vjp(f, [rvals[i] for i in drows], [pvals[i] for i in dpars])
        g_r, g_p = vjp(cvals)
        for i, g in zip(drows, g_r):
            r = rows[i]
            if r.hblk is None:
                for _ in r.gdt:
                    ref = next(g_refs)
                    ref[...] = g.astype(ref.dtype)
            else:
                g_prev, g_cur = g
                carry = next(carries)
                nr, nh = g_cur.shape[-2], g_prev.shape[-2]
                tail = g_cur[..., nr - nh:nr, :] + jnp.where(s > 0, carry[...], 0.0)
                for _ in r.gdt:
                    ref = next(g_refs)
                    if nr > nh:
                        ref[..., 0:nr - nh, :] = g_cur[..., 0:nr - nh, :].astype(ref.dtype)
                    ref[..., nr - nh:nr, :] = tail.astype(ref.dtype)
                carry[...] = g_prev
        for ref, g in zip(p_refs, g_p):
            @pl.when(s == 0)
            def _():
                ref[...] = jnp.zeros_like(ref)
            ref[...] += g

    return pl.pallas_call(
        body, name=name, grid=(ncol, nblk), in_specs=in_specs, out_specs=out_specs, out_shape=out_shape,
        scratch_shapes=scratch, compiler_params=_cparams(("arbitrary", "arbitrary")),
    )(*ins)


def _rowspec(arr, tb, cw, c0, cstep=1, halo=0, grad=False, ncol=1, gdt=(F32,)):
    T = arr.shape[0]
    imap = lambda j, n: (n, c0 + cstep * j)
    hblk = hmap = None
    if halo:
        q = tb // halo
        hblk, hmap = (halo, cw), (lambda j, n: (jnp.maximum(n * q - 1, 0), c0 + cstep * j))
    gshape = (T, cw * (ncol if cstep else 1)) if grad else None
    gmap = (lambda j, n: (n, cstep * j)) if grad else None
    return Row(arr, (tb, cw), imap, hblk, hmap, gshape, gmap, gdt)


def _parspec(arr, cw=None, c0=0, grad=False, ncol=1):
    if cw is None:
        return Par(arr, gshape=tuple(arr.shape) if grad else None,
                   gmap=(lambda nd: (lambda j: (0,) * nd))(arr.ndim) if grad else None)
    r = arr.shape[0]
    return Par(arr, (r, cw), lambda j: (0, c0 + j), (r, cw * ncol) if grad else None, (lambda j: (0, j)) if grad else None)


def _ln(r, g, b):
    mu = jnp.mean(r, axis=-1, keepdims=True)
    xc = r - mu
    var = jnp.mean(xc * xc, axis=-1, keepdims=True)
    return xc * lax.rsqrt(var + NORM_EPS) * g + b


def _ln_fn(j, first, rv, pv):
    return [_ln(rv[0], pv[0], pv[1])]


def _ln_fwd_fn(j, first, rv, pv):
    y = _ln(rv[0], pv[0], pv[1])
    return [y, y], []


def _loss_fn(j, first, rv, pv):
    r3, tgt = rv
    g, b = pv
    y, vjp = jax.vjp(_ln, r3, g, b)
    diff = y - tgt
    part = 0.5 * jnp.sum(diff * diff) / D_MODEL
    dr, dg, db = vjp(diff * (1.0 / D_MODEL))
    return [dr, dr], [jnp.full((SUBLANES, LANES), part, F32), dg, db]


def _mix_fn(j, first, rv, pv):
    gs, gg, ys, yg = rv
    return [_sigmoid(gs) * ys + _sigmoid(gg) * yg]


def _row_pick(x, i):
    ax = x.ndim - 2
    return jnp.sum(jnp.where(_iota(x.shape, ax) == i, x, 0.0), axis=ax, keepdims=True)


def _causal_conv(prev, cur, w, first):
    width = w.shape[0]
    prev = jnp.where(first, 0.0, prev)
    y = cur * _row_pick(w, width - 1)
    for d in range(1, width):
        y = y + _shift_halo(prev, cur, d) * _row_pick(w, width - 1 - d)
    return y


def _ffn_act_fn(j, first, rv, pv):
    (pg, cg), (pu, cu) = rv
    wg, wu, bg, bu = pv
    hg = _causal_conv(pg, cg, wg, first) + bg
    hu = _causal_conv(pu, cu, wu, first) + bu
    return [_silu(hg) * hu]


def _gdn_pre_fn(j, first, rv, pv):
    (prev, cur), = rv
    w, = pv
    t = _silu(_causal_conv(prev, cur, w, first))
    tn = t * lax.rsqrt(jnp.sum(t * t, axis=-1, keepdims=True) + 1e-6)
    return [jnp.where(j < 2 * GDN_HEADS, tn, t)]


def _gdn_gate_fn(j, first, rv, pv):
    gba, = rv
    alog, dtb, eb, eg = pv
    tb = gba.shape[0]
    beta = _sigmoid(gba)
    g = -jnp.exp(alog) * _softplus(gba + dtb)
    ri, ci = _iota((tb, tb), 0), _iota((tb, tb), 1)
    tril = jnp.where((ri // GDN_CHUNK == ci // GDN_CHUNK) & (ci <= ri), 1.0, 0.0)
    gc = _dot(tril, g, "nn", True)
    return [_dot(beta, eb, "nn", True), _dot(gc, eg, "nn", True)]


def _swa_fn(j, first, rv, pv):
    q, (kp, kc), (vp, vc) = rv
    bp, bc, sk = pv
    sp = _dot(q, kp, "nt") * (SWA_HEAD_DIM ** -0.5) + bp
    sc = _dot(q, kc, "nt") * (SWA_HEAD_DIM ** -0.5) + bc
    qi = _iota(sp.shape, sp.ndim - 2) % SWA_BLOCK
    kj = _iota(sp.shape, sp.ndim - 1)
    sp = jnp.where((kj > qi) & jnp.logical_not(first), sp, NEG_INF)
    sc = jnp.where(kj <= qi, sc, NEG_INF)
    m = jnp.maximum(jnp.maximum(jnp.max(sp, axis=-1, keepdims=True), jnp.max(sc, axis=-1, keepdims=True)), sk)
    m = lax.stop_gradient(m)
    ep, ec, es = jnp.exp(sp - m), jnp.exp(sc - m), jnp.exp(sk - m)
    inv = 1.0 / (jnp.sum(ep, axis=-1, keepdims=True) + jnp.sum(ec, axis=-1, keepdims=True) + es)
    vp = jnp.where(first, 0.0, vp)
    return [_dot(ep * inv, vp, "nn") + _dot(ec * inv, vc, "nn")]


def _memattn_fn(j, first, rv, pv):
    q, = rv
    k, v = pv
    s = _dot(q, k, "nt") * (MEM_HEAD_DIM ** -0.5)
    m = lax.stop_gradient(jnp.max(s, axis=-1, keepdims=True))
    e = jnp.exp(s - m)
    p = e * (1.0 / jnp.sum(e, axis=-1, keepdims=True))
    return [_dot(p, v, "nn")]


SOLVE_PREC = "x3"


@jax.custom_vjp
def _unit_lower_inv(a):
    c = a.shape[-1]
    eye = _iota((1, c, c), 1) == _iota((1, c, c), 2)
    tinv = jnp.where(eye, 1.0, 0.0) - a
    x = _raw_dot(a, a, "nn", SOLVE_PREC)
    for i in range(5):
        tinv = tinv + _raw_dot(tinv, x, "nn", SOLVE_PREC)
        if i < 4:
            x = _raw_dot(x, x, "nn", SOLVE_PREC)
    return tinv


def _unit_lower_inv_fwd(a):
    t = _unit_lower_inv(a)
    return t, t


def _unit_lower_inv_bwd(t, g):
    return (-_raw_dot(_raw_dot(t, g, "tn", SOLVE_PREC), t, "nt", SOLVE_PREC),)


_unit_lower_inv.defvjp(_unit_lower_inv_fwd, _unit_lower_inv_bwd)


@jax.custom_vjp
def _known_inv(a, t):
    return t


def _known_inv_fwd(a, t):
    return t, t


def _known_inv_bwd(t, g):
    return _unit_lower_inv_bwd(t, g) + (jnp.zeros_like(t),)


_known_inv.defvjp(_known_inv_fwd, _known_inv_bwd)


def _gdn_heads(q, k, v, bx, gx, g64, z, nw, S, tinv=None, keep_tinv=False):
    c = GDN_CHUNK
    q = q * (GDN_HEAD_DIM ** -0.5)
    kb, vb = k * bx, v * bx
    ri, ci = _iota((1, c, c), 1), _iota((1, c, c), 2)
    tril, strict, eye = ci <= ri, ci < ri, ci == ri
    grow = jnp.sum(jnp.where(eye, g64, 0.0), axis=1, keepdims=True)
    decay = jnp.where(tril, jnp.exp(jnp.where(tril, g64 - grow, 0.0)), 0.0)
    a = jnp.where(strict, _dot(kb, k, "nt") * decay, 0.0)
    tinv = _unit_lower_inv(a) if tinv is None else _known_inv(a, tinv)
    eg = jnp.exp(gx)
    u = _dot(tinv, vb, "nn", SOLVE_PREC)
    w = _dot(tinv, kb * eg, "nn", SOLVE_PREC)
    ai = jnp.where(tril, _dot(q, k, "nt") * decay, 0.0)
    glast = _row_pick(gx, c - 1)
    v_new = u - _dot(w, S, "nn")
    o = _dot(q * eg, S, "nn") + _dot(ai, v_new, "nn")
    s_new = S * jnp.exp(glast) + _dot(k * jnp.exp(glast - gx), v_new, "tn")
    o = o * lax.rsqrt(jnp.mean(o * o, axis=-1, keepdims=True) + 1e-6) * nw
    return (o * _silu(z), s_new, tinv) if keep_tinv else (o * _silu(z), s_new)


GDN_STEP_CHUNKS = 4


def _head_major(ref, off, width=GDN_HEAD_DIM, ci=0):
    r = slice(ci * GDN_CHUNK, (ci + 1) * GDN_CHUNK)
    return jnp.stack([ref[r, off + h * GDN_HEAD_DIM:off + h * GDN_HEAD_DIM + width] for h in range(GDN_HEADS)])


def _gdn_chunks_fwd(qkv, bx, gx, proj, nw, rider=None):
    T = qkv.shape[0]
    cps = GDN_STEP_CHUNKS
    nc, c, hd, nh = T // (cps * GDN_CHUNK), GDN_CHUNK, GDN_HEAD_DIM, GDN_HEADS
    rd = _ride(rider, 5, 3, 1)

    def body(*refs):
        qkv_ref, bx_ref, gx_ref, z_ref, nw_ref = refs[:5]
        y_ref, st_ref, ti_ref = refs[rd.o0:rd.o0 + 3]
        S = refs[rd.s0]
        rd.at_start(refs, pl.program_id(0) == 0)

        @pl.when(pl.program_id(0) == 0)
        def _():
            S[...] = jnp.zeros_like(S)

        s_new = S[...]
        for ci in range(cps):
            st_ref[ci] = s_new
            y, s_new, ti = _gdn_heads(_head_major(qkv_ref, 0, ci=ci), _head_major(qkv_ref, GDN_W, ci=ci),
                                      _head_major(qkv_ref, 2 * GDN_W, ci=ci), _head_major(bx_ref, 0, ci=ci),
                                      _head_major(gx_ref, 0, ci=ci), _head_major(gx_ref, 0, c, ci), _head_major(z_ref, 0, ci=ci),
                                      nw_ref[...], s_new, keep_tinv=True)
            ti_ref[ci] = ti
            for h in range(nh):
                y_ref[ci * c:(ci + 1) * c, h * hd:(h + 1) * hd] = y[h].astype(y_ref.dtype)
        S[...] = s_new
        rd.at_end(refs, pl.program_id(0) == nc - 1)

    row = lambda w, cb: pl.BlockSpec((cps * c, w), lambda n: (n, cb))
    res = pl.pallas_call(
        body, name="gdn_chunks_fwd", grid=(nc,),
        in_specs=[row(3 * GDN_W, 0), row(GDN_W, 0), row(GDN_W, 0), row(GDN_W, P_GZ // GDN_W),
                  pl.BlockSpec((1, hd), lambda n: (0, 0))] + rd.in_specs,
        out_specs=[row(GDN_W, 0), pl.BlockSpec((cps, nh, hd, hd), lambda n: (n, 0, 0, 0)),
                   pl.BlockSpec((cps, nh, c, c), lambda n: (n, 0, 0, 0))] + rd.out_specs,
        out_shape=[jax.ShapeDtypeStruct((T, GDN_W), BF16), jax.ShapeDtypeStruct((nc * cps, nh, hd, hd), F32),
                   jax.ShapeDtypeStruct((nc * cps, nh, c, c), F32)] + rd.out_shapes,
        scratch_shapes=[pltpu.VMEM((nh, hd, hd), F32)] + rd.scratch,
        compiler_params=_cparams(("arbitrary",)),
    )(qkv, bx, gx, proj, nw, *rd.ins)
    return res[0], (res[1], res[2]), res[3:]


def _gdn_chunks_bwd(qkv, bx, gx, proj, nw, saved, dy, rider=None):
    states, tinvs = saved
    T = qkv.shape[0]
    cps = GDN_STEP_CHUNKS
    nc, c, hd, nh = T // (cps * GDN_CHUNK), GDN_CHUNK, GDN_HEAD_DIM, GDN_HEADS
    rd = _ride(rider, 8, 5, 1)

    def body(*refs):
        qkv_ref, bx_ref, gx_ref, z_ref, nw_ref, st_ref, ti_ref, dy_ref = refs[:8]
        dqkv_ref, dbx_ref, dgx_ref, dz_ref, dnw_ref = refs[rd.o0:rd.o0 + 5]
        dS = refs[rd.s0]
        rd.at_start(refs, pl.program_id(0) == 0)

        @pl.when(pl.program_id(0) == 0)
        def _():
            dS[...] = jnp.zeros_like(dS)
            dnw_ref[...] = jnp.zeros_like(dnw_ref)

        dsp = dS[...]
        for ci in reversed(range(cps)):
            r = slice(ci * c, (ci + 1) * c)
            args = (_head_major(qkv_ref, 0, ci=ci), _head_major(qkv_ref, GDN_W, ci=ci), _head_major(qkv_ref, 2 * GDN_W, ci=ci),
                    _head_major(bx_ref, 0, ci=ci), _head_major(gx_ref, 0, ci=ci), _head_major(gx_ref, 0, c, ci),
                    _head_major(z_ref, 0, ci=ci), nw_ref[...], st_ref[ci])
            _, vjp = jax.vjp(functools.partial(_gdn_heads, tinv=ti_ref[ci]), *args)
            dq, dk, dv, dbx, dgx, dg64, dz, dnw, dsp = vjp((_head_major(dy_ref, 0, ci=ci), dsp))
            for h in range(nh):
                sl = slice(h * hd, (h + 1) * hd)
                dqkv_ref[r, sl] = dq[h].astype(dqkv_ref.dtype)
                dqkv_ref[r, GDN_W + h * hd:GDN_W + (h + 1) * hd] = dk[h].astype(dqkv_ref.dtype)
                dqkv_ref[r, 2 * GDN_W + h * hd:2 * GDN_W + (h + 1) * hd] = dv[h].astype(dqkv_ref.dtype)
                dbx_ref[r, sl] = dbx[h]
                dgx_ref[r, sl] = dgx[h]
                dgx_ref[r, h * hd:h * hd + c] += dg64[h]
                dz_ref[r, sl] = dz[h].astype(dz_ref.dtype)
            dnw_ref[...] += dnw
        dS[...] = dsp
        rd.at_end(refs, pl.program_id(0) == nc - 1)

    row = lambda w, cb: pl.BlockSpec((cps * c, w), lambda s: (nc - 1 - s, cb))
    res = pl.pallas_call(
        body, name="gdn_chunks_bwd", grid=(nc,),
        in_specs=[row(3 * GDN_W, 0), row(GDN_W, 0), row(GDN_W, 0), row(GDN_W, P_GZ // GDN_W), pl.BlockSpec((1, hd), lambda s: (0, 0)),
                  pl.BlockSpec((cps, nh, hd, hd), lambda s: (nc - 1 - s, 0, 0, 0)),
                  pl.BlockSpec((cps, nh, c, c), lambda s: (nc - 1 - s, 0, 0, 0)), row(GDN_W, 0)] + rd.in_specs,
        out_specs=[row(3 * GDN_W, 0), row(GDN_W, 0), row(GDN_W, 0), row(GDN_W, 0),
                   pl.BlockSpec((1, hd), lambda s: (0, 0))] + rd.out_specs,
        out_shape=[jax.ShapeDtypeStruct((T, 3 * GDN_W), F32), jax.ShapeDtypeStruct((T, GDN_W), F32),
                   jax.ShapeDtypeStruct((T, GDN_W), F32), jax.ShapeDtypeStruct((T, GDN_W), _CDT),
                   jax.ShapeDtypeStruct((1, hd), F32)] + rd.out_shapes,
        scratch_shapes=[pltpu.VMEM((nh, hd, hd), F32)] + rd.scratch,
        compiler_params=_cparams(("arbitrary",)),
    )(qkv, bx, gx, proj, nw, states, tinvs, dy, *rd.ins)
    res = list(res)
    return res[:5] + [res[5:]]


ADAMW_BLOCK_ELEMS = 700_000


def _adamw(name, w, g, m, v):
    if w.ndim == 3:
        C, _, R = w.shape
        blk = (_tile(C, 768, 1), 1, _tile(R, 512))
        grid = (C // blk[0], R // blk[2])
        spec = pl.BlockSpec(blk, lambda i, j: (i, 0, j))
    else:
        R, C = w.shape
        tr = _tile(R, max(SUBLANES, ADAMW_BLOCK_ELEMS // C // SUBLANES * SUBLANES), SUBLANES)
        grid = (R // tr,)
        spec = pl.BlockSpec((tr, C), lambda i: (i, 0))

    def body(w_ref, g_ref, m_ref, v_ref, d_ref, m2_ref, v2_ref):
        g_ = g_ref[...]
        m2 = ADAM_B1 * m_ref[...] + (1.0 - ADAM_B1) * g_
        v2 = ADAM_B2 * v_ref[...] + (1.0 - ADAM_B2) * (g_ * g_)
        m_hat = m2 / (1.0 - ADAM_B1 ** ADAM_STEP)
        v_hat = v2 / (1.0 - ADAM_B2 ** ADAM_STEP)
        d_ref[...] = -ADAM_LR * (m_hat / (jnp.sqrt(v_hat) + ADAM_EPS) + ADAM_WD * w_ref[...])
        m2_ref[...] = m2
        v2_ref[...] = v2

    return pl.pallas_call(
        body, name=name, grid=grid, in_specs=[spec] * 4, out_specs=[spec] * 3,
        out_shape=[jax.ShapeDtypeStruct(w.shape, F32)] * 3, compiler_params=_cparams(("parallel",) * len(grid)),
    )(w, g, m, v)


def _addn(name, parts, out_dtype=F32):
    parts = [p if isinstance(p, tuple) else (p, None) for p in parts]
    a0, k0 = parts[0]
    R, C = a0.shape[-2:]
    tr = _tile(R, 256, 2 * SUBLANES)
    specs = []
    for a, k in parts:
        if k is None:
            specs.append(pl.BlockSpec((tr, C), lambda i: (i, 0)))
        else:
            specs.append(pl.BlockSpec((None, tr, C), (lambda kk: (lambda i: (kk, i, 0)))(k)))

    def body(*refs):
        acc = refs[0][...].astype(F32)
        for r in refs[1:-1]:
            acc = acc + r[...].astype(F32)
        refs[-1][...] = acc.astype(out_dtype)

    return pl.pallas_call(
        body, name=name, grid=(R // tr,), in_specs=specs, out_specs=pl.BlockSpec((tr, C), lambda i: (i, 0)),
        out_shape=jax.ShapeDtypeStruct((R, C), out_dtype), compiler_params=_cparams(("parallel",)),
    )(*[a for a, _ in parts])


MESH = pl.DeviceIdType.MESH
_HBM = pl.BlockSpec(memory_space=pltpu.HBM)


def _place():
    x, y, c = lax.axis_index("x"), lax.axis_index("y"), lax.axis_index("c")
    return x, y, c, [(1 - x, y), (x, 1 - y), (1 - x, 1 - y)]


class _Rider:
    def __init__(self, ins, out_shapes, nsem, start, finish):
        self.ins, self.out_shapes, self.nsem, self.start, self.finish = list(ins), list(out_shapes), nsem, start, finish

    def sems(self):
        return [pltpu.SemaphoreType.DMA((self.nsem,)), pltpu.SemaphoreType.DMA((self.nsem,))]


def _run_rider(name, rd):
    n_in, n_out = len(rd.ins), len(rd.out_shapes)

    def body(*refs):
        ins, outs, (send, recv) = refs[:n_in], refs[n_in:n_in + n_out], refs[n_in + n_out:]
        rd.start(ins, outs, send, recv)
        rd.finish(ins, outs, send, recv)

    return pl.pallas_call(body, name=name, in_specs=[_HBM] * n_in, out_specs=[_HBM] * n_out, out_shape=rd.out_shapes,
                          scratch_shapes=rd.sems())(*rd.ins)


def _gather_rider(ts):
    nt = len(ts)

    def half(t, hc):
        rh = ts[t].shape[0] // 2
        return pl.ds(pl.multiple_of(hc * rh, 16), rh)

    def rcopy(send, recv, t, k, src, dst, to):
        return pltpu.make_async_remote_copy(src_ref=src, dst_ref=dst, send_sem=send.at[6 * t + k], recv_sem=recv.at[6 * t + k],
                                            device_id=to, device_id_type=MESH)

    def first_hop(ins, outs, send, recv, t, r, px, py, c, me):
        return rcopy(send, recv, t, r, ins[t].at[half(t, c)], outs[t].at[me, half(t, c)], (px, py, c))

    def start(ins, outs, send, recv):
        x, y, c, rel = _place()
        for t in range(nt):
            for r, (px, py) in enumerate(rel):
                first_hop(ins, outs, send, recv, t, r, px, py, c, 2 * x + y).start()

    def finish(ins, outs, send, recv):
        x, y, c, rel = _place()
        sib = (x, y, 1 - c)
        passed = []
        for t in range(nt):
            for r, (px, py) in enumerate(rel):
                got = outs[t].at[2 * px + py, half(t, c)]
                rcopy(send, recv, t, r, got, got, (px, py, c)).wait_recv()
                fw = rcopy(send, recv, t, 3 + r, got, got, sib)
                fw.start()
                passed.append(fw)
        for t in range(nt):
            for r, (px, py) in enumerate(rel):
                got = outs[t].at[2 * px + py, half(t, 1 - c)]
                rcopy(send, recv, t, 3 + r, got, got, sib).wait_recv()
        for t in range(nt):
            for r, (px, py) in enumerate(rel):
                first_hop(ins, outs, send, recv, t, r, px, py, c, 2 * x + y).wait_send()
        for fw in passed:
            fw.wait_send()

    return _Rider(ts, [jax.ShapeDtypeStruct((4,) + tuple(t.shape), t.dtype) for t in ts], 6 * nt, start, finish)


def _scatter_rider(ps):
    nt = len(ps)

    def copy(ins, outs, send, recv, t, r, px, py, c):
        return pltpu.make_async_remote_copy(src_ref=ins[t].at[2 * px + py], dst_ref=outs[t].at[r], send_sem=send.at[3 * t + r],
                                            recv_sem=recv.at[3 * t + r], device_id=(px, py, c), device_id_type=MESH)

    def start(ins, outs, send, recv):
        x, y, c, rel = _place()
        for t in range(nt):
            for r, (px, py) in enumerate(rel):
                copy(ins, outs, send, recv, t, r, px, py, c).start()

    def finish(ins, outs, send, recv):
        x, y, c, rel = _place()
        for t in range(nt):
            for r, (px, py) in enumerate(rel):
                copy(ins, outs, send, recv, t, r, px, py, c).wait()

    return _Rider(ps, [jax.ShapeDtypeStruct((3,) + tuple(p.shape[1:]), p.dtype) for p in ps], 3 * nt, start, finish)


def _swap_rider(ts):
    nt = len(ts)

    def copy(ins, outs, send, recv, t):
        x, y, c, _ = _place()
        rh = ts[t].shape[1] // 2
        src = ins[t].at[:, pl.ds(pl.multiple_of((1 - c) * rh, 16), rh), :]
        return pltpu.make_async_remote_copy(src_ref=src, dst_ref=outs[t], send_sem=send.at[t], recv_sem=recv.at[t],
                                            device_id=(x, y, 1 - c), device_id_type=MESH)

    def start(ins, outs, send, recv):
        for t in range(nt):
            copy(ins, outs, send, recv, t).start()

    def finish(ins, outs, send, recv):
        for t in range(nt):
            copy(ins, outs, send, recv, t).wait()

    return _Rider(ts, [jax.ShapeDtypeStruct((4, t.shape[1] // 2, t.shape[2]), t.dtype) for t in ts], nt, start, finish)


def _pair_exchange(gs):
    nt = len(gs)

    def body(*refs):
        ins, outs = refs[:nt], refs[nt:2 * nt]
        send, recv = refs[2 * nt:]
        x, y, c, _ = _place()
        cps = []
        for t in range(nt):
            cp = pltpu.make_async_remote_copy(src_ref=ins[t], dst_ref=outs[t], send_sem=send.at[t], recv_sem=recv.at[t],
                                              device_id=(x, y, 1 - c), device_id_type=MESH)
            cp.start()
            cps.append(cp)
        for cp in cps:
            cp.wait()

    return pl.pallas_call(
        body, name="pair_exchange", in_specs=[_HBM] * nt, out_specs=[_HBM] * nt,
        out_shape=[jax.ShapeDtypeStruct(tuple(g.shape), g.dtype) for g in gs],
        scratch_shapes=[pltpu.SemaphoreType.DMA((nt,)), pltpu.SemaphoreType.DMA((nt,))],
    )(*gs)


def _allgather8(v):
    m, n = v.shape

    def body(x_ref, out_ref, send, recv, lsem):
        x, y, c, rel = _place()
        me, sib = (x, y, c), (x, y, 1 - c)

        def blk(px, py, pc):
            return out_ref.at[4 * px + 2 * py + pc]

        def copy(k, block, to, src=None):
            return pltpu.make_async_remote_copy(src_ref=blk(*block) if src is None else src, dst_ref=blk(*block), send_sem=send.at[k],
                                                recv_sem=recv.at[k], device_id=to, device_id_type=MESH)

        mine = pltpu.make_async_copy(x_ref, blk(*me), lsem)
        mine.start()
        first = [copy(0, me, sib, src=x_ref)] + [copy(1 + r, me, (*ch, c), src=x_ref) for r, ch in enumerate(rel)]
        for cp in first:
            cp.start()
        passed = [copy(4 + r, (*ch, c), sib) for r, ch in enumerate(rel)]
        for r, ch in enumerate(rel):
            copy(1 + r, (*ch, c), me).wait_recv()
            passed[r].start()
        copy(0, sib, me).wait_recv()
        for r, ch in enumerate(rel):
            copy(4 + r, (*ch, 1 - c), me).wait_recv()
        for cp in first + passed:
            cp.wait_send()
        mine.wait()

    return pl.pallas_call(
        body, name="allgather8", in_specs=[pl.BlockSpec(memory_space=pltpu.VMEM)], out_specs=pl.BlockSpec(memory_space=pltpu.VMEM),
        out_shape=jax.ShapeDtypeStruct((8, m, n), v.dtype),
        scratch_shapes=[pltpu.SemaphoreType.DMA((7,)), pltpu.SemaphoreType.DMA((7,)), pltpu.SemaphoreType.DMA],
    )(v)


def _t5_bucket(dist):
    max_exact = REL_BUCKETS // 2
    d = jnp.maximum(dist, 1).astype(F32)
    large = max_exact + (jnp.log(d / max_exact) / math.log(REL_MAX_DIST / max_exact) * (REL_BUCKETS - max_exact)).astype(jnp.int32)
    large = jnp.minimum(large, REL_BUCKETS - 1)
    return jnp.where(dist < max_exact, dist, large)


def _bias_onehot():
    qi = jnp.arange(SWA_BLOCK)[:, None]
    kj = jnp.arange(SWA_BLOCK)[None, :]
    dist = jnp.concatenate([(qi + SWA_BLOCK - kj).reshape(-1), (qi - kj).reshape(-1)])
    bucket = _t5_bucket(jnp.maximum(dist, 0))
    return (bucket[None, :] == jnp.arange(REL_BUCKETS)[:, None]).astype(F32)


def _head_spread():
    lane = jnp.arange(LANES)[:, None]
    head = jnp.arange(GDN_W)[None, :] // GDN_HEAD_DIM
    return (lane == head).astype(F32), (lane == head + GDN_HEADS).astype(F32)


def _lane16(v8):
    return jnp.pad(v8.astype(F32), (GDN_HEADS, LANES - 2 * GDN_HEADS)).reshape(1, LANES)


def _stack_heads(t, nb):
    return t.reshape(nb, SWA_BLOCK, SWA_KV_HEADS, SWA_GRP, SWA_HEAD_DIM).transpose(2, 0, 3, 1, 4).reshape(
        SWA_KV_HEADS, nb * SWA_GRP * SWA_BLOCK, SWA_HEAD_DIM)


def _unstack_heads(t, nb):
    return t.reshape(SWA_KV_HEADS, nb, SWA_GRP, SWA_BLOCK, SWA_HEAD_DIM).transpose(1, 3, 0, 2, 4).reshape(nb * SWA_BLOCK, SWA_Q)


def _kv_heads(t):
    return t.reshape(t.shape[0], SWA_KV_HEADS, SWA_HEAD_DIM).transpose(1, 0, 2)


def _swa_specs(qs, ks, vs, bp, bc, sk, grad, gdt=(F32,)):
    T = ks.shape[1]
    qr = SWA_GRP * SWA_BLOCK
    g = lambda a: tuple(a.shape) if grad else None
    nk = SWA_KV_HEADS
    m3 = lambda j, n: (0, n, 0)
    h3 = lambda j, n: (0, jnp.maximum(n - 1, 0), 0)
    p3 = lambda j: (0, 0, 0)
    rows = [Row(qs, (nk, qr, SWA_HEAD_DIM), m3, gshape=g(qs), gmap=m3, gdt=gdt),
            Row(ks, (nk, SWA_BLOCK, SWA_HEAD_DIM), m3, (nk, SWA_BLOCK, SWA_HEAD_DIM), h3, g(ks), m3, gdt),
            Row(vs, (nk, SWA_BLOCK, SWA_HEAD_DIM), m3, (nk, SWA_BLOCK, SWA_HEAD_DIM), h3, g(vs), m3, gdt)]
    pars = [Par(bp, (nk, qr, SWA_BLOCK), p3, g(bp), p3), Par(bc, (nk, qr, SWA_BLOCK), p3, g(bc), p3),
            Par(sk, (nk, qr, 1), p3, g(sk), p3)]
    return rows, pars, T // SWA_BLOCK


class _LocalWeights:
    def __init__(self, W):
        self.W = W

    def w1(self):
        return self.W

    def rider_a(self):
        return None

    def w2(self, got):
        return self.W

    def rider_b(self):
        return None

    def w3(self, got):
        return self.W

    def rider_g(self, G):
        return None

    def g_done(self, got):
        pass

    def rider_up(self, G):
        return None

    def up_done(self, got):
        pass

    def swap_up(self, G):
        return None

    def swap_up_done(self, got):
        pass

    def swap_rest(self, G):
        return None

    def swap_rest_done(self, got):
        pass

    def rider_last(self, G):
        return None

    def last_done(self, got):
        pass


def _fwd_bwd(x, mem, tgt, src):
    W = dict(src.w1())
    T = x.shape[0]
    nb = T // SWA_BLOCK
    tb = min(256, T)
    tbl = min(512, T)
    fwd = lambda f: (lambda *a: (f(*a), []))
    full = lambda cols, dt, t, cw: Out((T, cols), dt, (t, cw), lambda j, n: (n, j))

    xb = x.astype(_CDT)
    ra = src.rider_a()
    proj = _mm("proj", xb, W["in_p"], "nn", rider=ra)
    proj, got = proj if ra is not None else (proj, None)
    W.update(src.w2(got))

    onehot_t = _bias_onehot()
    bias_flat = _mm("swa_bias", W["rel_bias"].T, onehot_t, "nn", hi=True)
    half = SWA_BLOCK * SWA_BLOCK
    bp = bias_flat[:, :half].reshape(SWA_KV_HEADS, SWA_GRP * SWA_BLOCK, SWA_BLOCK)
    bc = bias_flat[:, half:].reshape(SWA_KV_HEADS, SWA_GRP * SWA_BLOCK, SWA_BLOCK)
    sk = jnp.broadcast_to(W["swa_sinks"].reshape(SWA_KV_HEADS, SWA_GRP, 1, 1), (SWA_KV_HEADS, SWA_GRP, SWA_BLOCK, 1)).reshape(
        SWA_KV_HEADS, SWA_GRP * SWA_BLOCK, 1)
    qs = _stack_heads(proj[:, P_SQ:P_SQ + SWA_Q], nb)
    ks = _kv_heads(proj[:, P_SK:P_SK + SWA_KV])
    vs = _kv_heads(proj[:, P_SV:P_SV + SWA_KV])
    rows, pars, nblk = _swa_specs(qs, ks, vs, bp, bc, sk, False)
    o_s, = _rowmap("swa_fwd", fwd(_swa_fn), 1, nblk, rows, pars,
                   [Out(tuple(qs.shape), F32, (SWA_KV_HEADS, SWA_GRP * SWA_BLOCK, SWA_HEAD_DIM), lambda j, n: (0, n, 0))])
    o_swa = _unstack_heads(o_s, nb).astype(_CDT)

    ncq = 3 * GDN_W // LANES
    tbp = min(1024, T)
    pre_rows = lambda grad: [_rowspec(proj, tbp, LANES, P_GQKV // LANES, halo=SUBLANES, grad=grad, ncol=ncq, gdt=(_CDT,))]
    pre_pars = lambda grad: [_parspec(W["gdn_conv_w"], LANES, 0, grad=grad, ncol=ncq)]
    qkv_n, = _rowmap("gdn_pre_fwd", fwd(_gdn_pre_fn), ncq, T // tbp, pre_rows(False), pre_pars(False),
                     [full(3 * GDN_W, F32, tbp, LANES)])
    eb, eg = _head_spread()
    alog_row, dtb_row = _lane16(W["gdn_a_log"]), _lane16(W["gdn_dt_bias"])
    gate_rows = lambda grad: [_rowspec(proj, tbl, LANES, P_BA // LANES, cstep=0, grad=grad, gdt=(_CDT,))]
    gate_pars = lambda grad: [_parspec(alog_row, grad=grad), _parspec(dtb_row, grad=grad), _parspec(eb), _parspec(eg)]
    bx, gx = _rowmap("gdn_gate_fwd", fwd(_gdn_gate_fn), 1, T // tbl, gate_rows(False), gate_pars(False),
                     [full(GDN_W, F32, tbl, GDN_W), full(GDN_W, F32, tbl, GDN_W)])
    nw = W["gdn_norm_w"].reshape(1, GDN_HEAD_DIM)
    o_gdn, states, got = _gdn_chunks_fwd(qkv_n, bx, gx, proj, nw, rider=src.rider_b())
    W.update(src.w3(got))

    ys = _mm("y_swa", o_swa, W["br_swa"], "nn")
    yg = _mm("y_gdn", o_gdn, W["br_gdn"], "nn")
    cwm = 512
    mix_rows = lambda grad: [_rowspec(proj, tb, cwm, P_GS // cwm, grad=grad, ncol=D_MODEL // cwm, gdt=(_CDT,)),
                             _rowspec(proj, tb, cwm, P_GG // cwm, grad=grad, ncol=D_MODEL // cwm, gdt=(_CDT,)),
                             _rowspec(ys, tb, cwm, 0, grad=grad, ncol=D_MODEL // cwm, gdt=(_CDT,)),
                             _rowspec(yg, tb, cwm, 0, grad=grad, ncol=D_MODEL // cwm, gdt=(_CDT,))]
    mixed, = _rowmap("mix_fwd", fwd(_mix_fn), D_MODEL // cwm, T // tb, mix_rows(False), [], [full(D_MODEL, _CDT, tb, cwm)])
    r1 = _mm("r1", mixed, W["mix_o"], "nn", add=x, add_scale=ALPHA)

    def ln_fwd(name, r, g, b):
        return _rowmap(name, _ln_fwd_fn, 1, T // tb, [_rowspec(r, tb, D_MODEL, 0)], [_parspec(g), _parspec(b)],
                       [full(D_MODEL, F32, tb, D_MODEL), full(D_MODEL, _CDT, tb, D_MODEL)])

    def ln_bwd(name, r, g, b, ct):
        return _rowmap_bwd(name, _ln_fn, 1, T // tb, [_rowspec(r, tb, D_MODEL, 0, grad=True, gdt=(F32, _CDT))],
                           [_parspec(g, grad=True), _parspec(b, grad=True)], [_rowspec(ct, tb, D_MODEL, 0)])

    g1, b1 = W["ln1_g"].reshape(1, -1), W["ln1_b"].reshape(1, -1)
    g2, b2 = W["ln2_g"].reshape(1, -1), W["ln2_b"].reshape(1, -1)
    g3, b3 = W["ln3_g"].reshape(1, -1), W["ln3_b"].reshape(1, -1)
    x1, x1b = ln_fwd("ln1_fwd", r1, g1, b1)

    qm = _mm("mem_q", x1b, W["mem_q"], "nn")
    kvm = _mm("mem_kv", mem, W["mem_kv"], "nn")
    ma_rows = lambda grad: [_rowspec(qm, tbl, MEM_HEAD_DIM, 0, grad=grad, ncol=MEM_HEADS, gdt=(_CDT,))]
    ma_pars = lambda grad: [_parspec(kvm, MEM_HEAD_DIM, 0, grad=grad, ncol=MEM_HEADS),
                            _parspec(kvm, MEM_HEAD_DIM, MEM_HEADS, grad=grad, ncol=MEM_HEADS)]
    om, = _rowmap("memattn_fwd", fwd(_memattn_fn), MEM_HEADS, T // tbl, ma_rows(False), ma_pars(False),
                  [full(MEM_W, _CDT, tbl, MEM_HEAD_DIM)])
    r2 = _mm("r2", om, W["mem_o"], "nn", add=x1, add_scale=ALPHA)
    x2, x2b = ln_fwd("ln2_fwd", r2, g2, b2)

    hcat = _mm("ffn_up", x2b, W["up_p"], "nn")
    cwf = 512
    ncf = D_FF_PAD // cwf
    cw_p, cb_p = W["ffn_conv_w_p"], W["ffn_conv_b_p"]
    tbf = min(512, T)
    ffn_rows = lambda grad: [_rowspec(hcat, tbf, cwf, 0, halo=SUBLANES, grad=grad, ncol=ncf, gdt=(_CDT,)),
                             _rowspec(hcat, tbf, cwf, ncf, halo=SUBLANES, grad=grad, ncol=ncf, gdt=(_CDT,))]
    ffn_pars = lambda grad: [_parspec(cw_p, cwf, 0, grad=grad, ncol=ncf), _parspec(cw_p, cwf, ncf, grad=grad, ncol=ncf),
                             _parspec(cb_p, cwf, 0, grad=grad, ncol=ncf), _parspec(cb_p, cwf, ncf, grad=grad, ncol=ncf)]
    act, = _rowmap("ffn_act_fwd", fwd(_ffn_act_fn), ncf, T // tbf, ffn_rows(False), ffn_pars(False), [full(D_FF_PAD, _CDT, tbf, cwf)])
    r3 = _mm("r3", act, W["down_p"], "nn", add=x2, add_scale=ALPHA)
    dr3, dr3b, lacc, dg3, db3 = _rowmap("ln3_loss", _loss_fn, 1, T // tb, [_rowspec(r3, tb, D_MODEL, 0), _rowspec(tgt, tb, D_MODEL, 0)],
                                        [_parspec(g3), _parspec(b3)], [full(D_MODEL, F32, tb, D_MODEL), full(D_MODEL, _CDT, tb, D_MODEL)],
                                  accs=[(SUBLANES, LANES), (1, D_MODEL), (1, D_MODEL)])
    loss = lacc[0, 0]

    G = {}
    G["down_p"] = _mm("dw_down", act, dr3b, "tn", out_dtype=_GDT)
    dact = _mm("d_act", dr3b, W["down_p"], "nt")
    dhg, dhu, dcwg, dcwu, dcbg, dcbu = _rowmap_bwd("ffn_act_bwd", _ffn_act_fn, ncf, T // tbf, ffn_rows(True), ffn_pars(True),
                                                   [_rowspec(dact, tbf, cwf, 0)])
    dx2 = _mm("dx2_gate", dhg, W["up_p"], "nt", add=dr3, add_scale=ALPHA, b_k0=0)
    dx2 = _mm("dx2_up", dhu, W["up_p"], "nt", add=dx2, b_k0=D_FF_PAD)
    G["up_p"] = jnp.concatenate([_mm("dw_gate", x2b, dhg, "tn", out_dtype=_GDT), _mm("dw_up", x2b, dhu, "tn", out_dtype=_GDT)], axis=1)
    G["ffn_conv_w"] = jnp.concatenate([dcwg[:, :D_FF], dcwu[:, :D_FF]], axis=1)
    G["ffn_conv_b"] = jnp.concatenate([dcbg[0, :D_FF], dcbu[0, :D_FF]])
    G["ln3_g"], G["ln3_b"] = dg3[0], db3[0]

    dr2, dr2b, dg2, db2 = ln_bwd("ln2_bwd", r2, g2, b2, dx2)
    G["ln2_g"], G["ln2_b"] = dg2[0], db2[0]
    G["mem_o"] = _mm("dw_mem_o", om, dr2b, "tn", out_dtype=_GDT)
    dom = _mm("d_om", dr2b, W["mem_o"], "nt", out_dtype=_CDT)
    dqm, dkm, dvm = _rowmap_bwd("memattn_bwd", _memattn_fn, MEM_HEADS, T // tbl, ma_rows(True), ma_pars(True),
                                [_rowspec(dom, tbl, MEM_HEAD_DIM, 0)])
    G["mem_kv"] = _mm("dw_mem_kv", mem.astype(_CDT), jnp.concatenate([dkm, dvm], axis=1).astype(_CDT), "tn", out_dtype=_GDT)
    G["mem_q"] = _mm("dw_mem_q", x1b, dqm, "tn", out_dtype=_GDT)
    dx1 = _mm("dx1", dqm, W["mem_q"], "nt", add=dr2, add_scale=ALPHA)

    dr1, dr1b, dg1, db1 = ln_bwd("ln1_bwd", r1, g1, b1, dx1)
    G["ln1_g"], G["ln1_b"] = dg1[0], db1[0]
    G["mix_o"] = _mm("dw_mix_o", mixed, dr1b, "tn", out_dtype=_GDT)
    rs = src.swap_up(G)
    dmixed = _mm("d_mixed", dr1b, W["mix_o"], "nt", rider=rs)
    if rs is not None:
        dmixed, got = dmixed
        src.swap_up_done(got)
    dgs, dgg, dys, dyg = _rowmap_bwd("mix_bwd", _mix_fn, D_MODEL // cwm, T // tb, mix_rows(True), [], [_rowspec(dmixed, tb, cwm, 0)])
    G["br_swa"] = _mm("dw_br_swa", o_swa, dys, "tn", out_dtype=_GDT)
    G["br_gdn"] = _mm("dw_br_gdn", o_gdn, dyg, "tn", out_dtype=_GDT)
    do_swa = _mm("d_o_swa", dys, W["br_swa"], "nt", out_dtype=_CDT)
    rs = src.swap_rest(G)
    do_gdn = _mm("d_o_gdn", dyg, W["br_gdn"], "nt", rider=rs)
    if rs is not None:
        do_gdn, got = do_gdn
        src.swap_rest_done(got)

    rows, pars, nblk = _swa_specs(qs, ks, vs, bp, bc, sk, True, (_CDT,))
    m3 = lambda j, n: (0, n, 0)
    dqs, dks, dvs, dbp, dbc, dsk = _rowmap_bwd("swa_bwd", _swa_fn, 1, nblk, rows, pars,
                                               [Row(_stack_heads(do_swa, nb), (SWA_KV_HEADS, SWA_GRP * SWA_BLOCK, SWA_HEAD_DIM), m3)])
    d_swa = jnp.concatenate([_unstack_heads(dqs, nb), dks.transpose(1, 0, 2).reshape(T, SWA_KV),
                             dvs.transpose(1, 0, 2).reshape(T, SWA_KV)], axis=1)
    dbias = jnp.concatenate([dbp.reshape(SWA_HEADS, half), dbc.reshape(SWA_HEADS, half)], axis=1)
    G["rel_bias"] = _mm("d_rel_bias", dbias, onehot_t.T, "nn", hi=True).T
    G["swa_sinks"] = _mm("d_sinks", dsk.reshape(SWA_HEADS, SWA_BLOCK), jnp.ones((SWA_BLOCK, LANES), F32), "nn", hi=True)[:, 0]

    dqkv_n, dbx, dgx, dz, dnw, got = _gdn_chunks_bwd(qkv_n, bx, gx, proj, nw, states, do_gdn, rider=src.rider_g(G))
    src.g_done(got)
    G["gdn_norm_w"] = dnw[0]
    dgba, dalog, ddtb = _rowmap_bwd("gdn_gate_bwd", _gdn_gate_fn, 1, T // tbl, gate_rows(True), gate_pars(True),
                                    [_rowspec(dbx, tbl, GDN_W, 0), _rowspec(dgx, tbl, GDN_W, 0)])
    G["gdn_a_log"], G["gdn_dt_bias"] = dalog[0, GDN_HEADS:2 * GDN_HEADS], ddtb[0, GDN_HEADS:2 * GDN_HEADS]
    dgqkv, dcw_gdn = _rowmap_bwd("gdn_pre_bwd", _gdn_pre_fn, ncq, T // tbp, pre_rows(True), pre_pars(True),
                                 [_rowspec(dqkv_n, tbp, LANES, 0)])
    G["gdn_conv_w"] = dcw_gdn

    dproj = jnp.concatenate([dgs, dgg, dgqkv, dz, d_swa, dgba, jnp.zeros((T, P_END - P_USED), _CDT)], axis=1)
    ru = src.rider_up(G)
    G["in_p"] = _mm("dw_in", xb, dproj, "tn", out_dtype=_GDT, rider=ru)
    if ru is not None:
        G["in_p"], got = G["in_p"]
        src.up_done(got)
    rl = src.rider_last(G)
    dx = _mm("dx", dproj, W["in_p"], "nt", add=dr1, add_scale=ALPHA, rider=rl)
    if rl is not None:
        dx, got = dx
        src.last_done(got)
    return loss, dx, G


W_NAMES = ["w_in", "rel_bias", "swa_sinks", "gdn_conv_w", "gdn_a_log", "gdn_dt_bias", "gdn_norm_w", "w_br_swa", "w_br_gdn",
           "w_mix_o", "ln1_g", "ln1_b", "w_mem_q", "w_mem_kv", "w_mem_o", "ln2_g", "ln2_b", "w_up", "ffn_conv_w", "ffn_conv_b",
           "w_down", "ln3_g", "ln3_b"]
BIG = ["w_in", "w_br_swa", "w_br_gdn", "w_mix_o", "w_mem_q", "w_mem_kv", "w_mem_o", "w_up", "w_down"]
SMALL = [n for n in W_NAMES if n not in BIG]
COL_SHARDED = ["w_in", "w_br_swa", "w_br_gdn", "w_mem_o", "w_up"]


def _pack(arrs):
    rows = []
    for a in arrs:
        f = a.reshape(-1).astype(F32)
        rows.append(jnp.pad(f, (0, (-f.shape[0]) % LANES)).reshape(-1, LANES))
    n = sum(r.shape[0] for r in rows)
    if n % 16:
        rows.append(jnp.zeros((16 - n % 16, LANES), F32))
    return jnp.concatenate(rows, axis=0)


def _unpack(p, shapes):
    out, off = [], 0
    for s in shapes:
        n = int(np.prod(s)) if len(s) else 1
        r = -(-n // LANES)
        out.append(p[off:off + r].reshape(-1)[:n].reshape(s))
        off += r
    return out


def _merge_shards(d):
    cat = lambda names: jnp.concatenate([d[n] for n in names], axis=-2)
    return [d.get("w_in"), d["w_up"], cat(["w_br_swa", "w_br_gdn", "w_mem_q", "w_mem_o"]), cat(["w_mix_o", "w_down"]), d["w_mem_kv"]]


def _split_shards(ts):
    a, b, c, dd, e = ts
    return {"w_in": a, "w_up": b, "w_br_swa": c[..., 0:1024, :], "w_br_gdn": c[..., 1024:2048, :], "w_mem_q": c[..., 2048:2560, :],
            "w_mem_o": c[..., 2560:3072, :], "w_mix_o": dd[..., 0:512, :], "w_down": dd[..., 512:, :], "w_mem_kv": e}


def _to_full(name, t):
    if name in COL_SHARDED:
        return _cols_from_chips(t, [(0, 4 * t.shape[2])])
    return t.reshape(4 * t.shape[1], t.shape[2])


def _to_chips(name, t):
    if name in COL_SHARDED:
        return _chips_from_cols(t, [(0, t.shape[1])], t.shape[1] // 4)
    return t.reshape(4, t.shape[0] // 4, t.shape[1])


def _cols_from_chips(g, segs, own=None):
    C, parts = g.shape[2], []
    for s in segs:
        if isinstance(s, int):
            parts.append(jnp.zeros((g.shape[1], s), g.dtype))
            continue
        lo, hi = s
        while lo < hi:
            k = lo // C
            e = min(hi, (k + 1) * C)
            piece = g[k][:, lo - k * C:e - k * C]
            parts.append(piece if own is None else jnp.where(own[1] == k, own[0][:, lo - k * C:e - k * C], piece))
            lo = e
    return jnp.concatenate(parts, axis=1)


def _chips_from_cols(p, segs, C):
    out = []
    for k in range(4):
        lo, hi, parts, o = k * C, (k + 1) * C, [], 0
        for plo, w in segs:
            a, b = max(lo, o), min(hi, o + w)
            if a < b:
                parts.append(p[:, plo + a - o:plo + b - o])
            o += w
        out.append(jnp.concatenate(parts, axis=1))
    return jnp.stack(out)


_IN_OFF = np.cumsum((0,) + IN_WIDTHS)
_IN_SEGS = [(P_SQ, SWA_Q), (P_SK, SWA_KV), (P_SV, SWA_KV), (P_GQKV, 3 * GDN_W), (P_GZ, GDN_W), (P_BA, 2 * GDN_HEADS),
            (P_GS, D_MODEL), (P_GG, D_MODEL)]
_IN_PADDED = [(int(_IN_OFF[i]), int(_IN_OFF[k])) for i, k in ((9, 10), (10, 11), (3, 6), (6, 7), (0, 1), (1, 2), (2, 3), (7, 9))] + [
    P_END - P_BA - 2 * GDN_HEADS]
_UP_SEGS = [(0, D_FF), (D_FF_PAD, D_FF)]
_UP_PADDED = [(0, D_FF), D_FF_PAD - D_FF, (D_FF, 2 * D_FF), D_FF_PAD - D_FF]


def _in_to_padded(w):
    o = _IN_OFF
    cut = lambda i, k: w[:, o[i]:o[k]]
    return jnp.concatenate([cut(9, 10), cut(10, 11), cut(3, 6), cut(6, 7), cut(0, 1), cut(1, 2), cut(2, 3), cut(7, 9),
                            jnp.zeros((w.shape[0], P_END - P_BA - 2 * GDN_HEADS), w.dtype)], axis=1)


def _in_from_padded(p):
    return jnp.concatenate([p[:, P_SQ:P_SQ + SWA_Q], p[:, P_SK:P_SK + SWA_KV], p[:, P_SV:P_SV + SWA_KV], p[:, P_GQKV:P_GQKV + 3 * GDN_W],
                            p[:, P_GZ:P_GZ + GDN_W], p[:, P_BA:P_BA + 2 * GDN_HEADS], p[:, P_GS:P_GS + D_MODEL], p[:, P_GG:P_GG + D_MODEL]],
                           axis=1)


def _ff_pad(t, axis):
    g, u = jnp.split(t, 2, axis=axis)
    pad = [(0, 0)] * t.ndim
    pad[axis] = (0, D_FF_PAD - D_FF)
    return jnp.concatenate([jnp.pad(g, pad), jnp.pad(u, pad)], axis=axis)


def _ff_unpad(t, axis):
    g, u = jnp.split(t, 2, axis=axis)
    return jnp.concatenate([lax.slice_in_dim(g, 0, D_FF, axis=axis), lax.slice_in_dim(u, 0, D_FF, axis=axis)], axis=axis)


def _assemble_weights(full, small):
    W = dict(small)
    W["in_p"] = _in_to_padded(full["w_in"])
    W["up_p"] = _ff_pad(full["w_up"], 1)
    W["down_p"] = jnp.pad(full["w_down"], ((0, D_FF_PAD - D_FF), (0, 0)))
    W["br_swa"], W["br_gdn"], W["mix_o"] = full["w_br_swa"], full["w_br_gdn"], full["w_mix_o"]
    W["mem_q"], W["mem_kv"], W["mem_o"] = full["w_mem_q"], full["w_mem_kv"], full["w_mem_o"]
    W["ffn_conv_w_p"] = _ff_pad(small["ffn_conv_w"], 1)
    W["ffn_conv_b_p"] = _ff_pad(small["ffn_conv_b"].reshape(1, -1), 1)
    return W


def _full_grads(G):
    out = {"w_in": _in_from_padded(G["in_p"])} if "in_p" in G else {}
    out.update({"w_up": _ff_unpad(G["up_p"], 1), "w_down": G["down_p"][:D_FF], "w_br_swa": G["br_swa"], "w_br_gdn": G["br_gdn"],
                "w_mix_o": G["mix_o"], "w_mem_q": G["mem_q"], "w_mem_kv": G["mem_kv"], "w_mem_o": G["mem_o"]})
    return out


def kernel(x, mem, w_in, rel_bias, swa_sinks, gdn_conv_w, gdn_a_log, gdn_dt_bias, gdn_norm_w, w_br_swa, w_br_gdn, w_mix_o, ln1_g, ln1_b, w_mem_q, w_mem_kv, w_mem_o, ln2_g, ln2_b, w_up, ffn_conv_w, ffn_conv_b, w_down, ln3_g, ln3_b, loss_target, m_w_in, m_rel_bias, m_swa_sinks, m_gdn_conv_w, m_gdn_a_log, m_gdn_dt_bias, m_gdn_norm_w, m_w_br_swa, m_w_br_gdn, m_w_mix_o, m_ln1_g, m_ln1_b, m_w_mem_q, m_w_mem_kv, m_w_mem_o, m_ln2_g, m_ln2_b, m_w_up, m_ffn_conv_w, m_ffn_conv_b, m_w_down, m_ln3_g, m_ln3_b, v_w_in, v_rel_bias, v_swa_sinks, v_gdn_conv_w, v_gdn_a_log, v_gdn_dt_bias, v_gdn_norm_w, v_w_br_swa, v_w_br_gdn, v_w_mix_o, v_ln1_g, v_ln1_b, v_w_mem_q, v_w_mem_kv, v_w_mem_o, v_ln2_g, v_ln2_b, v_w_up, v_ffn_conv_w, v_ffn_conv_b, v_w_down, v_ln3_g, v_ln3_b):
    a = dict(locals())
    w = {n: a[n] for n in W_NAMES}
    m = {n: a["m_" + n] for n in W_NAMES}
    v = {n: a["v_" + n] for n in W_NAMES}
    chip = 2 * lax.axis_index("x") + lax.axis_index("y")
    core = lax.axis_index("c")
    sq = lambda t: t.reshape(t.shape[1:]) if (t.ndim > 1 and t.shape[0] == 1 and t is not rel_bias) else t

    sh_a, sh_b, sh_c, sh_d, sh_e = _merge_shards({n: sq(w[n]).astype(_CDT) for n in BIG})
    fcw_sh, gcw_sh = sq(ffn_conv_w).shape, sq(gdn_conv_w).shape
    slot = lax.broadcasted_iota(jnp.int32, (4, 1, 1), 0)

    def with_own(got, mine):
        return [jnp.where(slot == chip, t[None], g) for g, t in zip(got, mine)]

    def reduce_start(tag, gch, theirs=None):
        pair = []
        theirs = _run_rider("pair_swap_" + tag, _swap_rider(gch)) if theirs is None else theirs
        for t, (mine, got) in enumerate(zip(gch, theirs)):
            rh = mine.shape[1] // 2
            mine_h = lax.dynamic_slice_in_dim(mine, core * rh, rh, axis=1)
            pair.append(_addn(f"pair_sum_{tag}{t}", [mine_h.reshape(4 * rh, -1), got.reshape(4 * rh, -1)], _GDT).reshape(4, rh, -1))
        return pair

    def reduce_end(tag, pair, others):
        halves = []
        for t, (p, o) in enumerate(zip(pair, others)):
            own = lax.dynamic_index_in_dim(p, chip, 0, keepdims=False)
            halves.append(_addn(f"chip_sum_{tag}{t}", [own, (o, 0), (o, 1), (o, 2)]))
        return halves

    class MeshWeights:
        def w1(self):
            mine = [sh_a, _pack([sq(ffn_conv_w), sq(gdn_conv_w)])]
            got_a, got_f = _run_rider("gather_first", _gather_rider(mine))
            got_f, = with_own([got_f], mine[1:])
            conv = [_unpack(got_f[k], [fcw_sh, gcw_sh]) for k in range(4)]
            W = {n: sq(w[n]) for n in SMALL}
            W["ffn_conv_w"] = jnp.concatenate([cv[0] for cv in conv], axis=1)
            W["gdn_conv_w"] = jnp.concatenate([cv[1] for cv in conv], axis=1)
            W["ffn_conv_w_p"] = _ff_pad(W["ffn_conv_w"], 1)
            W["ffn_conv_b_p"] = _ff_pad(W["ffn_conv_b"].reshape(1, -1), 1)
            W["in_p"] = _cols_from_chips(got_a, _IN_PADDED, own=(sh_a, chip))
            return W

        def rider_a(self):
            return _gather_rider([sh_c, sh_d, sh_e])

        def w2(self, got):
            c, d, e = with_own(got, [sh_c, sh_d, sh_e])
            f = {n: _to_full(n, t) for n, t in _split_shards([None, None, c, d, e]).items() if t is not None}
            return {"br_swa": f["w_br_swa"], "br_gdn": f["w_br_gdn"], "mix_o": f["w_mix_o"], "mem_q": f["w_mem_q"], "mem_kv": f["w_mem_kv"],
                    "mem_o": f["w_mem_o"], "down_p": jnp.pad(f["w_down"], ((0, D_FF_PAD - D_FF), (0, 0)))}

        def rider_b(self):
            return _gather_rider([sh_b])

        def w3(self, got):
            return {"up_p": _cols_from_chips(got[0], _UP_PADDED, own=(sh_b, chip))}

        def swap_rest(self, G):
            gf = _full_grads(G)
            gch = {n: _to_chips(n, gf[n]) for n in BIG if n not in ("w_in", "w_up")}
            gch["w_up"] = None
            self.gch_rest = _merge_shards(gch)[2:]
            return _swap_rider(self.gch_rest)

        def swap_rest_done(self, got):
            self.theirs_rest = got

        def rider_g(self, G):
            self.pair = reduce_start("rest", self.gch_rest, self.theirs_rest)
            return _scatter_rider(self.pair)

        def g_done(self, got):
            self.halves = reduce_end("rest", self.pair, got)

        def swap_up(self, G):
            self.gch_up = [_chips_from_cols(G["up_p"], _UP_SEGS, 2 * D_FF // 4)]
            return _swap_rider(self.gch_up)

        def swap_up_done(self, got):
            self.theirs_up = got

        def rider_up(self, G):
            self.pair_up = reduce_start("up", self.gch_up, self.theirs_up)
            return _scatter_rider(self.pair_up)

        def up_done(self, got):
            self.halves = reduce_end("up", self.pair_up, got) + self.halves

        def rider_last(self, G):
            self.pair_in = reduce_start("in", [_chips_from_cols(G["in_p"], _IN_SEGS, sum(IN_WIDTHS) // 4)])
            return _scatter_rider(self.pair_in)

        def last_done(self, got):
            self.halves = reduce_end("in", self.pair_in, got) + self.halves

    src = MeshWeights()
    loss, dx, G = _fwd_bwd(x[0], mem[0], loss_target[0], src)

    small_names = SMALL
    small_shapes = [()] + [tuple(G[n].shape) for n in small_names]
    packed = _pack([loss] + [G[n] for n in small_names])
    allp = _allgather8(packed)
    tot = _addn("small_sum", [(allp, k) for k in range(8)])
    parts = _unpack(tot, small_shapes)
    loss_tot, gsmall = parts[0], dict(zip(small_names, parts[1:]))
    gsmall["ffn_conv_w"] = lax.dynamic_slice_in_dim(gsmall["ffn_conv_w"], chip * fcw_sh[1], fcw_sh[1], axis=1)
    gsmall["gdn_conv_w"] = lax.dynamic_slice_in_dim(gsmall["gdn_conv_w"], chip * gcw_sh[1], gcw_sh[1], axis=1)

    both = []
    for h, o in zip(src.halves, _pair_exchange(src.halves)):
        both.append(jnp.concatenate([jnp.where(core == 0, h, o), jnp.where(core == 0, o, h)], axis=0))
    gbig = _split_shards(both)

    outs = {}
    for n in BIG:
        if w[n].shape[-1] % LANES:
            cols_out = lambda t: jnp.transpose(t, (2, 0, 1))
            g_ = jnp.transpose(gbig[n])[:, None, :]
            d_, m_, v_ = _adamw("adamw_" + n, cols_out(w[n]), g_, cols_out(m[n]), cols_out(v[n]))
            outs[n] = tuple(jnp.transpose(t, (1, 2, 0)) for t in (g_, d_, m_, v_))
            continue
        d_, m_, v_ = _adamw("adamw_" + n, sq(w[n]), gbig[n], sq(m[n]), sq(v[n]))
        outs[n] = (gbig[n], d_, m_, v_)
    for n in SMALL:
        two_d = (-1, w[n].shape[-1])
        g_ = gsmall[n].reshape(two_d)
        d_, m_, v_ = _adamw("adamw_" + n, w[n].reshape(two_d), g_, m[n].reshape(two_d), v[n].reshape(two_d))
        outs[n] = (g_, d_, m_, v_)

    res = [loss_tot.reshape(()), dx.reshape(x.shape)]
    for k in range(4):
        res += [outs[n][k].reshape(w[n].shape) for n in W_NAMES]
    return tuple(res)
```

```python
import functools
import math

import jax
import jax.numpy as jnp
import numpy as np
from jax import lax
from jax.experimental import pallas as pl
from jax.experimental.pallas import tpu as pltpu

F32 = jnp.float32
BF16 = jnp.bfloat16
_CDT = BF16
_GDT = BF16

D_MODEL = 2048
SWA_HEADS, SWA_KV_HEADS, SWA_HEAD_DIM, SWA_BLOCK = 16, 2, 64, 128
SWA_GRP = SWA_HEADS // SWA_KV_HEADS
REL_BUCKETS, REL_MAX_DIST = 32, 128
GDN_HEADS, GDN_HEAD_DIM, GDN_CONV, GDN_CHUNK = 8, 128, 4, 64
MEM_HEADS, MEM_HEAD_DIM = 4, 128
D_FF, D_FF_PAD, FFN_CONV = 5504, 5632, 3
SWA_Q, SWA_KV, GDN_W, MEM_W = 1024, 128, 1024, 512
IN_WIDTHS = (SWA_Q, SWA_KV, SWA_KV, GDN_W, GDN_W, GDN_W, GDN_W, GDN_HEADS, GDN_HEADS, D_MODEL, D_MODEL)
NORM_EPS = 1e-5
ALPHA = 2.0 ** 0.25
NEG_INF = -1e30
ADAM_LR, ADAM_B1, ADAM_B2, ADAM_EPS, ADAM_WD, ADAM_STEP = 0.001, 0.9, 0.999, 1e-08, 0.01, 10
LANES, SUBLANES = 128, 8
VMEM_LIMIT = 56 * 1024 * 1024

P_GS, P_GG, P_GQKV, P_GZ, P_SQ, P_SK, P_SV, P_BA, P_USED, P_END = 0, 2048, 4096, 7168, 8192, 9216, 9344, 9472, 9600, 9728


def _tile(dim, pref, align=LANES):
    if dim <= pref:
        return dim
    t = (pref // align) * align
    while t >= align:
        if dim % t == 0:
            return t
        t -= align
    return dim


_DIMS = {"nn": (((1,), (0,)), ((), ())), "nt": (((1,), (1,)), ((), ())), "tn": (((0,), (0,)), ((), ()))}
_BDIMS = {"nn": (((2,), (1,)), ((0,), (0,))), "nt": (((2,), (2,)), ((0,), (0,))), "tn": (((1,), (1,)), ((0,), (0,)))}


def _raw_dot(a, b, form, hi):
    dims = (_BDIMS if a.ndim == 3 else _DIMS)[form]
    if hi == "x3":
        a, b = a.astype(F32), b.astype(F32)
        ah, bh = a.astype(BF16), b.astype(BF16)
        al, bl = (a - ah.astype(F32)).astype(BF16), (b - bh.astype(F32)).astype(BF16)
        d = lambda p, q: lax.dot_general(p, q, dims, preferred_element_type=F32)
        if form == "tn":
            return d(ah, bh) + (d(ah, bl) + d(al, bh))
        m = a.shape[-2]
        both = d(jnp.concatenate([ah, al], axis=-2), bh)
        return both[..., :m, :] + (d(ah, bl) + both[..., m:, :])
    if hi:
        return lax.dot_general(a.astype(F32), b.astype(F32), dims, precision=lax.Precision.HIGHEST, preferred_element_type=F32)
    return lax.dot_general(a.astype(_CDT), b.astype(_CDT), dims, preferred_element_type=F32)


@functools.partial(jax.custom_vjp, nondiff_argnums=(2, 3))
def _dot(a, b, form, hi=False):
    return _raw_dot(a, b, form, hi)


def _dot_fwd(a, b, form, hi):
    return _raw_dot(a, b, form, hi), (a, b)


def _dot_bwd(form, hi, res, g):
    a, b = res
    if form == "nn":
        da, db = _raw_dot(g, b, "nt", hi), _raw_dot(a, g, "tn", hi)
    elif form == "nt":
        da, db = _raw_dot(g, b, "nn", hi), _raw_dot(g, a, "tn", hi)
    else:
        da, db = _raw_dot(b, g, "nt", hi), _raw_dot(a, g, "nn", hi)
    return da.astype(a.dtype), db.astype(b.dtype)


_dot.defvjp(_dot_fwd, _dot_bwd)


@functools.partial(jax.custom_vjp, nondiff_argnums=(2,))
def _shift_halo(prev, cur, d):
    assert prev.shape[0] == SUBLANES
    return pltpu.roll(jnp.concatenate([prev, cur], axis=0), d, 0)[SUBLANES:]


def _shift_halo_fwd(prev, cur, d):
    return _shift_halo(prev, cur, d), None


def _shift_halo_bwd(d, _, g):
    nh = SUBLANES
    ext = jnp.concatenate([jnp.zeros((nh, g.shape[1]), g.dtype), g], axis=0)
    r = pltpu.roll(ext, ext.shape[0] - d, 0)
    return r[:nh], r[nh:]


_shift_halo.defvjp(_shift_halo_fwd, _shift_halo_bwd)


@jax.custom_vjp
def _recip(x):
    return 1.0 / x


def _recip_fwd(x):
    r = 1.0 / x
    return r, r


def _recip_bwd(r, g):
    return (-g * r * r,)


_recip.defvjp(_recip_fwd, _recip_bwd)


def _sigmoid(x):
    return _recip(1.0 + jnp.exp(-x))


def _silu(x):
    return x * _sigmoid(x)


def _softplus(x):
    return jnp.maximum(x, 0.0) + jnp.log(1.0 + jnp.exp(-jnp.abs(x)))


def _iota(shape, axis):
    return lax.broadcasted_iota(jnp.int32, shape, axis)


def _cparams(sem, **kw):
    return pltpu.CompilerParams(dimension_semantics=sem, vmem_limit_bytes=VMEM_LIMIT, **kw)


class _ride:
    def __init__(self, rider, n_in, n_out, n_scr):
        self.rider = rider
        self.ins = rider.ins if rider else []
        n_rin = len(self.ins)
        self.out_shapes = rider.out_shapes if rider else []
        n_rout = len(self.out_shapes)
        self.in_specs, self.out_specs = [_HBM] * n_rin, [_HBM] * n_rout
        self.scratch = rider.sems() if rider else []
        self.o0 = n_in + n_rin
        self.s0 = self.o0 + n_out + n_rout
        self._rin = slice(n_in, n_in + n_rin)
        self._rout = slice(self.o0 + n_out, self.s0)
        self._sem = self.s0 + n_scr

    def _args(self, refs):
        return refs[self._rin], refs[self._rout], refs[self._sem], refs[self._sem + 1]

    def at_start(self, refs, cond):
        if self.rider:
            pl.when(cond)(lambda: self.rider.start(*self._args(refs)))

    def at_end(self, refs, cond):
        if self.rider:
            pl.when(cond)(lambda: self.rider.finish(*self._args(refs)))


def _mm(name, a, b, form, out_dtype=F32, add=None, add_scale=1.0, hi=False, tm=None, tn=None, tk=2816, rider=None, b_k0=None):
    if form == "nn":
        (M, K), (K2, N) = a.shape, b.shape
    elif form == "nt":
        (M, K), (N, K2) = a.shape, b.shape
        K2 = K if b_k0 is not None else K2
    else:
        (K, M), (K2, N) = a.shape, b.shape
    assert K == K2, (name, a.shape, b.shape, form)
    tm = tm or 512
    tn = tn or 2816
    tm, tn, tk = _tile(M, tm), _tile(N, tn), _tile(K, tk)
    nk = K // tk
    k0 = 0 if b_k0 is None else b_k0 // tk
    assert b_k0 is None or (form == "nt" and b_k0 % tk == 0)
    a_spec = pl.BlockSpec((tk, tm), lambda i, j, k: (k, i)) if form == "tn" else pl.BlockSpec((tm, tk), lambda i, j, k: (i, k))
    b_spec = pl.BlockSpec((tn, tk), lambda i, j, k: (j, k + k0)) if form == "nt" else pl.BlockSpec((tk, tn), lambda i, j, k: (k, j))
    o_spec = pl.BlockSpec((tm, tn), lambda i, j, k: (i, j))
    has_add = add is not None

    def finish(r, c_ref, o_ref):
        if has_add:
            r = r + add_scale * c_ref[...].astype(F32)
        o_ref[...] = r.astype(out_dtype)

    n_own = 3 if has_add else 2
    grid = (M // tm, N // tn, nk)
    rd = _ride(rider, n_own, 1, 1 if nk > 1 else 0)

    def body(*refs):
        a_ref, b_ref = refs[:2]
        c_ref = refs[2] if has_add else None
        o_ref = refs[rd.o0]
        pid = [pl.program_id(d) for d in range(3)]
        rd.at_start(refs, (pid[0] == 0) & (pid[1] == 0) & (pid[2] == 0))
        if nk == 1:
            finish(_raw_dot(a_ref[...], b_ref[...], form, hi), c_ref, o_ref)
        else:
            acc = refs[rd.s0]

            @pl.when(pid[2] == 0)
            def _():
                acc[...] = jnp.zeros_like(acc)

            acc[...] += _raw_dot(a_ref[...], b_ref[...], form, hi)

            @pl.when(pid[2] == nk - 1)
            def _():
                finish(acc[...], c_ref, o_ref)
        rd.at_end(refs, (pid[0] == grid[0] - 1) & (pid[1] == grid[1] - 1) & (pid[2] == nk - 1))

    ins = [a, b] + ([add] if has_add else [])
    specs = [a_spec, b_spec] + ([o_spec] if has_add else [])
    res = pl.pallas_call(
        body, name=name, grid=grid, in_specs=specs + rd.in_specs, out_specs=[o_spec] + rd.out_specs,
        out_shape=[jax.ShapeDtypeStruct((M, N), out_dtype)] + rd.out_shapes,
        scratch_shapes=([pltpu.VMEM((tm, tn), F32)] if nk > 1 else []) + rd.scratch,
        compiler_params=_cparams(("arbitrary",) * 3 if rider else ("parallel", "parallel", "arbitrary")),
    )(*ins, *rd.ins)
    return (res[0], res[1:]) if rider else res[0]


class Row:
    def __init__(self, arr, blk, imap, hblk=None, hmap=None, gshape=None, gmap=None, gdt=(F32,)):
        self.arr, self.blk, self.imap, self.hblk, self.hmap, self.gshape, self.gmap = arr, blk, imap, hblk, hmap, gshape, gmap
        self.gdt = gdt


class Par:
    def __init__(self, arr, blk=None, imap=None, gshape=None, gmap=None):
        self.arr = arr
        self.blk = tuple(arr.shape) if blk is None else blk
        nd = len(self.blk)
        self.imap = (lambda j: (0,) * nd) if imap is None else imap
        self.gshape, self.gmap = gshape, gmap


class Out:
    def __init__(self, shape, dtype, blk, imap):
        self.shape, self.dtype, self.blk, self.imap = shape, dtype, blk, imap


def _rows_of(blk):
    return [d for d in blk if d is not None][0]


def _rowmap(name, fn, ncol, nblk, rows, pars, outs, accs=()):
    in_specs, ins = [], []
    for r in rows:
        ins.append(r.arr)
        in_specs.append(pl.BlockSpec(r.blk, r.imap))
        if r.hblk is not None:
            ins.append(r.arr)
            in_specs.append(pl.BlockSpec(r.hblk, r.hmap))
    for p in pars:
        ins.append(p.arr)
        in_specs.append(pl.BlockSpec(p.blk, (lambda im: (lambda j, n: im(j)))(p.imap)))
    out_specs = [pl.BlockSpec(o.blk, o.imap) for o in outs]
    out_shape = [jax.ShapeDtypeStruct(o.shape, o.dtype) for o in outs]
    for a in accs:
        out_specs.append(pl.BlockSpec(a, (lambda nd: (lambda j, n: (0,) * nd))(len(a))))
        out_shape.append(jax.ShapeDtypeStruct(a, F32))
    n_in = len(ins)

    def body(*refs):
        j, n = pl.program_id(0), pl.program_id(1)
        it = iter(refs[:n_in])
        rvals = []
        for r in rows:
            cur = next(it)[...]
            rvals.append((next(it)[...], cur) if r.hblk is not None else cur)
        pvals = [next(it)[...] for _ in pars]
        o_refs = refs[n_in:n_in + len(outs)]
        a_refs = refs[n_in + len(outs):]
        ovals, avals = fn(j, n == 0, rvals, pvals)
        for ref, v in zip(o_refs, ovals):
            ref[...] = v.astype(ref.dtype)
        if accs:
            @pl.when((j == 0) & (n == 0))
            def _():
                for ref in a_refs:
                    ref[...] = jnp.zeros_like(ref)
            for ref, v in zip(a_refs, avals):
                ref[...] += v

    res = pl.pallas_call(
        body, name=name, grid=(ncol, nblk), in_specs=in_specs, out_specs=out_specs, out_shape=out_shape,
        compiler_params=_cparams(("arbitrary", "arbitrary")),
    )(*ins)
    return res


def _rowmap_bwd(name, fn, ncol, nblk, rows, pars, cts):
    rev = lambda im: (lambda j, s: im(j, nblk - 1 - s))
    in_specs, ins = [], []
    for r in rows:
        ins.append(r.arr)
        in_specs.append(pl.BlockSpec(r.blk, rev(r.imap)))
        if r.hblk is not None:
            ins.append(r.arr)
            in_specs.append(pl.BlockSpec(r.hblk, rev(r.hmap)))
    for p in pars:
        ins.append(p.arr)
        in_specs.append(pl.BlockSpec(p.blk, (lambda im: (lambda j, s: im(j)))(p.imap)))
    for c in cts:
        ins.append(c.arr)
        in_specs.append(pl.BlockSpec(c.blk, rev(c.imap)))
    n_in = len(ins)
    drows = [i for i, r in enumerate(rows) if r.gshape is not None]
    dpars = [i for i, p in enumerate(pars) if p.gshape is not None]
    out_specs, out_shape, scratch = [], [], []
    for i in drows:
        r = rows[i]
        for dt in r.gdt:
            out_specs.append(pl.BlockSpec(r.blk, rev(r.gmap)))
            out_shape.append(jax.ShapeDtypeStruct(r.gshape, dt))
        if r.hblk is not None:
            scratch.append(pltpu.VMEM(tuple(d for d in r.hblk if d is not None), F32))
    n_drow_out = len(out_specs)
    for i in dpars:
        p = pars[i]
        out_specs.append(pl.BlockSpec(p.blk, (lambda im: (lambda j, s: im(j)))(p.gmap)))
        out_shape.append(jax.ShapeDtypeStruct(p.gshape, F32))

    def body(*refs):
        j, s = pl.program_id(0), pl.program_id(1)
        first = s == nblk - 1
        it = iter(refs[:n_in])
        rvals = []
        for r in rows:
            cur = next(it)[...]
            rvals.append((next(it)[...], cur) if r.hblk is not None else cur)
        pvals = [next(it)[...] for _ in pars]
        cvals = [next(it)[...].astype(F32) for _ in cts]
        g_refs = iter(refs[n_in:n_in + n_drow_out])
        p_refs = refs[n_in + n_drow_out:n_in + n_drow_out + len(dpars)]
        carries = iter(refs[n_in + n_drow_out + len(dpars):])

        def f(dr, dp):
            rv, pv = list(rvals), list(pvals)
            for i, v in zip(drows, dr):
                rv[i] = v
            for i, v in zip(dpars, dp):
                pv[i] = v
            return fn(j, first, rv, pv)

        _, vjp = jax.vjp(f, [rvals[i] for i in drows], [pvals[i] for i in dpars])
        g_r, g_p = vjp(cvals)
        for i, g in zip(drows, g_r):
            r = rows[i]
            if r.hblk is None:
                for _ in r.gdt:
                    ref = next(g_refs)
                    ref[...] = g.astype(ref.dtype)
            else:
                g_prev, g_cur = g
                carry = next(carries)
                nr, nh = g_cur.shape[-2], g_prev.shape[-2]
                tail = g_cur[..., nr - nh:nr, :] + jnp.where(s > 0, carry[...], 0.0)
                for _ in r.gdt:
                    ref = next(g_refs)
                    if nr > nh:
                        ref[..., 0:nr - nh, :] = g_cur[..., 0:nr - nh, :].astype(ref.dtype)
                    ref[..., nr - nh:nr, :] = tail.astype(ref.dtype)
                carry[...] = g_prev
        for ref, g in zip(p_refs, g_p):
            @pl.when(s == 0)
            def _():
                ref[...] = jnp.zeros_like(ref)
            ref[...] += g

    return pl.pallas_call(
        body, name=name, grid=(ncol, nblk), in_specs=in_specs, out_specs=out_specs, out_shape=out_shape,
        scratch_shapes=scratch, compiler_params=_cparams(("arbitrary", "arbitrary")),
    )(*ins)


def _rowspec(arr, tb, cw, c0, cstep=1, halo=0, grad=False, ncol=1, gdt=(F32,)):
    T = arr.shape[0]
    imap = lambda j, n: (n, c0 + cstep * j)
    hblk = hmap = None
    if halo:
        q = tb // halo
        hblk, hmap = (halo, cw), (lambda j, n: (jnp.maximum(n * q - 1, 0), c0 + cstep * j))
    gshape = (T, cw * (ncol if cstep else 1)) if grad else None
    gmap = (lambda j, n: (n, cstep * j)) if grad else None
    return Row(arr, (tb, cw), imap, hblk, hmap, gshape, gmap, gdt)


def _parspec(arr, cw=None, c0=0, grad=False, ncol=1):
    if cw is None:
        return Par(arr, gshape=tuple(arr.shape) if grad else None,
                   gmap=(lambda nd: (lambda j: (0,) * nd))(arr.ndim) if grad else None)
    r = arr.shape[0]
    return Par(arr, (r, cw), lambda j: (0, c0 + j), (r, cw * ncol) if grad else None, (lambda j: (0, j)) if grad else None)


def _ln(r, g, b):
    mu = jnp.mean(r, axis=-1, keepdims=True)
    xc = r - mu
    var = jnp.mean(xc * xc, axis=-1, keepdims=True)
    return xc * lax.rsqrt(var + NORM_EPS) * g + b


def _ln_fn(j, first, rv, pv):
    return [_ln(rv[0], pv[0], pv[1])]


def _ln_fwd_fn(j, first, rv, pv):
    y = _ln(rv[0], pv[0], pv[1])
    return [y, y], []


def _loss_fn(j, first, rv, pv):
    r3, tgt = rv
    g, b = pv
    y, vjp = jax.vjp(_ln, r3, g, b)
    diff = y - tgt
    part = 0.5 * jnp.sum(diff * diff) / D_MODEL
    dr, dg, db = vjp(diff * (1.0 / D_MODEL))
    return [dr, dr], [jnp.full((SUBLANES, LANES), part, F32), dg, db]


def _mix_fn(j, first, rv, pv):
    gs, gg, ys, yg = rv
    return [_sigmoid(gs) * ys + _sigmoid(gg) * yg]


def _row_pick(x, i):
    ax = x.ndim - 2
    return jnp.sum(jnp.where(_iota(x.shape, ax) == i, x, 0.0), axis=ax, keepdims=True)


def _causal_conv(prev, cur, w, first):
    width = w.shape[0]
    prev = jnp.where(first, 0.0, prev)
    y = cur * _row_pick(w, width - 1)
    for d in range(1, width):
        y = y + _shift_halo(prev, cur, d) * _row_pick(w, width - 1 - d)
    return y


def _ffn_act_fn(j, first, rv, pv):
    (pg, cg), (pu, cu) = rv
    wg, wu, bg, bu = pv
    hg = _causal_conv(pg, cg, wg, first) + bg
    hu = _causal_conv(pu, cu, wu, first) + bu
    return [_silu(hg) * hu]


def _gdn_pre_fn(j, first, rv, pv):
    (prev, cur), = rv
    w, = pv
    t = _silu(_causal_conv(prev, cur, w, first))
    tn = t * lax.rsqrt(jnp.sum(t * t, axis=-1, keepdims=True) + 1e-6)
    return [jnp.where(j < 2 * GDN_HEADS, tn, t)]


def _gdn_gate_fn(j, first, rv, pv):
    gba, = rv
    alog, dtb, eb, eg = pv
    tb = gba.shape[0]
    beta = _sigmoid(gba)
    g = -jnp.exp(alog) * _softplus(gba + dtb)
    ri, ci = _iota((tb, tb), 0), _iota((tb, tb), 1)
    tril = jnp.where((ri // GDN_CHUNK == ci // GDN_CHUNK) & (ci <= ri), 1.0, 0.0)
    gc = _dot(tril, g, "nn", True)
    return [_dot(beta, eb, "nn", True), _dot(gc, eg, "nn", True)]


def _swa_fn(j, first, rv, pv):
    q, (kp, kc), (vp, vc) = rv
    bp, bc, sk = pv
    sp = _dot(q, kp, "nt") * (SWA_HEAD_DIM ** -0.5) + bp
    sc = _dot(q, kc, "nt") * (SWA_HEAD_DIM ** -0.5) + bc
    qi = _iota(sp.shape, sp.ndim - 2) % SWA_BLOCK
    kj = _iota(sp.shape, sp.ndim - 1)
    sp = jnp.where((kj > qi) & jnp.logical_not(first), sp, NEG_INF)
    sc = jnp.where(kj <= qi, sc, NEG_INF)
    m = jnp.maximum(jnp.maximum(jnp.max(sp, axis=-1, keepdims=True), jnp.max(sc, axis=-1, keepdims=True)), sk)
    m = lax.stop_gradient(m)
    ep, ec, es = jnp.exp(sp - m), jnp.exp(sc - m), jnp.exp(sk - m)
    inv = 1.0 / (jnp.sum(ep, axis=-1, keepdims=True) + jnp.sum(ec, axis=-1, keepdims=True) + es)
    vp = jnp.where(first, 0.0, vp)
    return [_dot(ep * inv, vp, "nn") + _dot(ec * inv, vc, "nn")]


def _memattn_fn(j, first, rv, pv):
    q, = rv
    k, v = pv
    s = _dot(q, k, "nt") * (MEM_HEAD_DIM ** -0.5)
    m = lax.stop_gradient(jnp.max(s, axis=-1, keepdims=True))
    e = jnp.exp(s - m)
    p = e * (1.0 / jnp.sum(e, axis=-1, keepdims=True))
    return [_dot(p, v, "nn")]


SOLVE_PREC = "x3"


@jax.custom_vjp
def _unit_lower_inv(a):
    c = a.shape[-1]
    eye = _iota((1, c, c), 1) == _iota((1, c, c), 2)
    tinv = jnp.where(eye, 1.0, 0.0) - a
    x = _raw_dot(a, a, "nn", SOLVE_PREC)
    for i in range(5):
        tinv = tinv + _raw_dot(tinv, x, "nn", SOLVE_PREC)
        if i < 4:
            x = _raw_dot(x, x, "nn", SOLVE_PREC)
    return tinv


def _unit_lower_inv_fwd(a):
    t = _unit_lower_inv(a)
    return t, t


def _unit_lower_inv_bwd(t, g):
    return (-_raw_dot(_raw_dot(t, g, "tn", SOLVE_PREC), t, "nt", SOLVE_PREC),)


_unit_lower_inv.defvjp(_unit_lower_inv_fwd, _unit_lower_inv_bwd)


@jax.custom_vjp
def _known_inv(a, t):
    return t


def _known_inv_fwd(a, t):
    return t, t


def _known_inv_bwd(t, g):
    return _unit_lower_inv_bwd(t, g) + (jnp.zeros_like(t),)


_known_inv.defvjp(_known_inv_fwd, _known_inv_bwd)


def _gdn_heads(q, k, v, bx, gx, g64, z, nw, S, tinv=None, keep_tinv=False):
    c = GDN_CHUNK
    q = q * (GDN_HEAD_DIM ** -0.5)
    kb, vb = k * bx, v * bx
    ri, ci = _iota((1, c, c), 1), _iota((1, c, c), 2)
    tril, strict, eye = ci <= ri, ci < ri, ci == ri
    grow = jnp.sum(jnp.where(eye, g64, 0.0), axis=1, keepdims=True)
    decay = jnp.where(tril, jnp.exp(jnp.where(tril, g64 - grow, 0.0)), 0.0)
    a = jnp.where(strict, _dot(kb, k, "nt") * decay, 0.0)
    tinv = _unit_lower_inv(a) if tinv is None else _known_inv(a, tinv)
    eg = jnp.exp(gx)
    u = _dot(tinv, vb, "nn", SOLVE_PREC)
    w = _dot(tinv, kb * eg, "nn", SOLVE_PREC)
    ai = jnp.where(tril, _dot(q, k, "nt") * decay, 0.0)
    glast = _row_pick(gx, c - 1)
    v_new = u - _dot(w, S, "nn")
    o = _dot(q * eg, S, "nn") + _dot(ai, v_new, "nn")
    s_new = S * jnp.exp(glast) + _dot(k * jnp.exp(glast - gx), v_new, "tn")
    o = o * lax.rsqrt(jnp.mean(o * o, axis=-1, keepdims=True) + 1e-6) * nw
    return (o * _silu(z), s_new, tinv) if keep_tinv else (o * _silu(z), s_new)


GDN_STEP_CHUNKS = 4


def _head_major(ref, off, width=GDN_HEAD_DIM, ci=0):
    r = slice(ci * GDN_CHUNK, (ci + 1) * GDN_CHUNK)
    return jnp.stack([ref[r, off + h * GDN_HEAD_DIM:off + h * GDN_HEAD_DIM + width] for h in range(GDN_HEADS)])


def _gdn_chunks_fwd(qkv, bx, gx, proj, nw, rider=None):
    T = qkv.shape[0]
    cps = GDN_STEP_CHUNKS
    nc, c, hd, nh = T // (cps * GDN_CHUNK), GDN_CHUNK, GDN_HEAD_DIM, GDN_HEADS
    rd = _ride(rider, 5, 3, 1)

    def body(*refs):
        qkv_ref, bx_ref, gx_ref, z_ref, nw_ref = refs[:5]
        y_ref, st_ref, ti_ref = refs[rd.o0:rd.o0 + 3]
        S = refs[rd.s0]
        rd.at_start(refs, pl.program_id(0) == 0)

        @pl.when(pl.program_id(0) == 0)
        def _():
            S[...] = jnp.zeros_like(S)

        s_new = S[...]
        for ci in range(cps):
            st_ref[ci] = s_new
            y, s_new, ti = _gdn_heads(_head_major(qkv_ref, 0, ci=ci), _head_major(qkv_ref, GDN_W, ci=ci),
                                      _head_major(qkv_ref, 2 * GDN_W, ci=ci), _head_major(bx_ref, 0, ci=ci),
                                      _head_major(gx_ref, 0, ci=ci), _head_major(gx_ref, 0, c, ci), _head_major(z_ref, 0, ci=ci),
                                      nw_ref[...], s_new, keep_tinv=True)
            ti_ref[ci] = ti
            for h in range(nh):
                y_ref[ci * c:(ci + 1) * c, h * hd:(h + 1) * hd] = y[h].astype(y_ref.dtype)
        S[...] = s_new
        rd.at_end(refs, pl.program_id(0) == nc - 1)

    row = lambda w, cb: pl.BlockSpec((cps * c, w), lambda n: (n, cb))
    res = pl.pallas_call(
        body, name="gdn_chunks_fwd", grid=(nc,),
        in_specs=[row(3 * GDN_W, 0), row(GDN_W, 0), row(GDN_W, 0), row(GDN_W, P_GZ // GDN_W),
                  pl.BlockSpec((1, hd), lambda n: (0, 0))] + rd.in_specs,
        out_specs=[row(GDN_W, 0), pl.BlockSpec((cps, nh, hd, hd), lambda n: (n, 0, 0, 0)),
                   pl.BlockSpec((cps, nh, c, c), lambda n: (n, 0, 0, 0))] + rd.out_specs,
        out_shape=[jax.ShapeDtypeStruct((T, GDN_W), BF16), jax.ShapeDtypeStruct((nc * cps, nh, hd, hd), F32),
                   jax.ShapeDtypeStruct((nc * cps, nh, c, c), F32)] + rd.out_shapes,
        scratch_shapes=[pltpu.VMEM((nh, hd, hd), F32)] + rd.scratch,
        compiler_params=_cparams(("arbitrary",)),
    )(qkv, bx, gx, proj, nw, *rd.ins)
    return res[0], (res[1], res[2]), res[3:]


def _gdn_chunks_bwd(qkv, bx, gx, proj, nw, saved, dy, rider=None):
    states, tinvs = saved
    T = qkv.shape[0]
    cps = GDN_STEP_CHUNKS
    nc, c, hd, nh = T // (cps * GDN_CHUNK), GDN_CHUNK, GDN_HEAD_DIM, GDN_HEADS
    rd = _ride(rider, 8, 5, 1)

    def body(*refs):
        qkv_ref, bx_ref, gx_ref, z_ref, nw_ref, st_ref, ti_ref, dy_ref = refs[:8]
        dqkv_ref, dbx_ref, dgx_ref, dz_ref, dnw_ref = refs[rd.o0:rd.o0 + 5]
        dS = refs[rd.s0]
        rd.at_start(refs, pl.program_id(0) == 0)

        @pl.when(pl.program_id(0) == 0)
        def _():
            dS[...] = jnp.zeros_like(dS)
            dnw_ref[...] = jnp.zeros_like(dnw_ref)

        dsp = dS[...]
        for ci in reversed(range(cps)):
            r = slice(ci * c, (ci + 1) * c)
            args = (_head_major(qkv_ref, 0, ci=ci), _head_major(qkv_ref, GDN_W, ci=ci), _head_major(qkv_ref, 2 * GDN_W, ci=ci),
                    _head_major(bx_ref, 0, ci=ci), _head_major(gx_ref, 0, ci=ci), _head_major(gx_ref, 0, c, ci),
                    _head_major(z_ref, 0, ci=ci), nw_ref[...], st_ref[ci])
            _, vjp = jax.vjp(functools.partial(_gdn_heads, tinv=ti_ref[ci]), *args)
            dq, dk, dv, dbx, dgx, dg64, dz, dnw, dsp = vjp((_head_major(dy_ref, 0, ci=ci), dsp))
            for h in range(nh):
                sl = slice(h * hd, (h + 1) * hd)
                dqkv_ref[r, sl] = dq[h].astype(dqkv_ref.dtype)
                dqkv_ref[r, GDN_W + h * hd:GDN_W + (h + 1) * hd] = dk[h].astype(dqkv_ref.dtype)
                dqkv_ref[r, 2 * GDN_W + h * hd:2 * GDN_W + (h + 1) * hd] = dv[h].astype(dqkv_ref.dtype)
                dbx_ref[r, sl] = dbx[h]
                dgx_ref[r, sl] = dgx[h]
                dgx_ref[r, h * hd:h * hd + c] += dg64[h]
                dz_ref[r, sl] = dz[h].astype(dz_ref.dtype)
            dnw_ref[...] += dnw
        dS[...] = dsp
        rd.at_end(refs, pl.program_id(0) == nc - 1)

    row = lambda w, cb: pl.BlockSpec((cps * c, w), lambda s: (nc - 1 - s, cb))
    res = pl.pallas_call(
        body, name="gdn_chunks_bwd", grid=(nc,),
        in_specs=[row(3 * GDN_W, 0), row(GDN_W, 0), row(GDN_W, 0), row(GDN_W, P_GZ // GDN_W), pl.BlockSpec((1, hd), lambda s: (0, 0)),
                  pl.BlockSpec((cps, nh, hd, hd), lambda s: (nc - 1 - s, 0, 0, 0)),
                  pl.BlockSpec((cps, nh, c, c), lambda s: (nc - 1 - s, 0, 0, 0)), row(GDN_W, 0)] + rd.in_specs,
        out_specs=[row(3 * GDN_W, 0), row(GDN_W, 0), row(GDN_W, 0), row(GDN_W, 0),
                   pl.BlockSpec((1, hd), lambda s: (0, 0))] + rd.out_specs,
        out_shape=[jax.ShapeDtypeStruct((T, 3 * GDN_W), F32), jax.ShapeDtypeStruct((T, GDN_W), F32),
                   jax.ShapeDtypeStruct((T, GDN_W), F32), jax.ShapeDtypeStruct((T, GDN_W), _CDT),
                   jax.ShapeDtypeStruct((1, hd), F32)] + rd.out_shapes,
        scratch_shapes=[pltpu.VMEM((nh, hd, hd), F32)] + rd.scratch,
        compiler_params=_cparams(("arbitrary",)),
    )(qkv, bx, gx, proj, nw, states, tinvs, dy, *rd.ins)
    res = list(res)
    return res[:5] + [res[5:]]


ADAMW_BLOCK_ELEMS = 700_000


def _adamw(name, w, g, m, v):
    if w.ndim == 3:
        C, _, R = w.shape
        blk = (_tile(C, 768, 1), 1, _tile(R, 512))
        grid = (C // blk[0], R // blk[2])
        spec = pl.BlockSpec(blk, lambda i, j: (i, 0, j))
    else:
        R, C = w.shape
        tr = _tile(R, max(SUBLANES, ADAMW_BLOCK_ELEMS // C // SUBLANES * SUBLANES), SUBLANES)
        grid = (R // tr,)
        spec = pl.BlockSpec((tr, C), lambda i: (i, 0))

    def body(w_ref, g_ref, m_ref, v_ref, d_ref, m2_ref, v2_ref):
        g_ = g_ref[...]
        m2 = ADAM_B1 * m_ref[...] + (1.0 - ADAM_B1) * g_
        v2 = ADAM_B2 * v_ref[...] + (1.0 - ADAM_B2) * (g_ * g_)
        m_hat = m2 / (1.0 - ADAM_B1 ** ADAM_STEP)
        v_hat = v2 / (1.0 - ADAM_B2 ** ADAM_STEP)
        d_ref[...] = -ADAM_LR * (m_hat / (jnp.sqrt(v_hat) + ADAM_EPS) + ADAM_WD * w_ref[...])
        m2_ref[...] = m2
        v2_ref[...] = v2

    return pl.pallas_call(
        body, name=name, grid=grid, in_specs=[spec] * 4, out_specs=[spec] * 3,
        out_shape=[jax.ShapeDtypeStruct(w.shape, F32)] * 3, compiler_params=_cparams(("parallel",) * len(grid)),
    )(w, g, m, v)


def _addn(name, parts, out_dtype=F32):
    parts = [p if isinstance(p, tuple) else (p, None) for p in parts]
    a0, k0 = parts[0]
    R, C = a0.shape[-2:]
    tr = _tile(R, 256, 2 * SUBLANES)
    specs = []
    for a, k in parts:
        if k is None:
            specs.append(pl.BlockSpec((tr, C), lambda i: (i, 0)))
        else:
            specs.append(pl.BlockSpec((None, tr, C), (lambda kk: (lambda i: (kk, i, 0)))(k)))

    def body(*refs):
        acc = refs[0][...].astype(F32)
        for r in refs[1:-1]:
            acc = acc + r[...].astype(F32)
        refs[-1][...] = acc.astype(out_dtype)

    return pl.pallas_call(
        body, name=name, grid=(R // tr,), in_specs=specs, out_specs=pl.BlockSpec((tr, C), lambda i: (i, 0)),
        out_shape=jax.ShapeDtypeStruct((R, C), out_dtype), compiler_params=_cparams(("parallel",)),
    )(*[a for a, _ in parts])


MESH = pl.DeviceIdType.MESH
_HBM = pl.BlockSpec(memory_space=pltpu.HBM)


def _place():
    x, y, c = lax.axis_index("x"), lax.axis_index("y"), lax.axis_index("c")
    return x, y, c, [(1 - x, y), (x, 1 - y), (1 - x, 1 - y)]


class _Rider:
    def __init__(self, ins, out_shapes, nsem, start, finish):
        self.ins, self.out_shapes, self.nsem, self.start, self.finish = list(ins), list(out_shapes), nsem, start, finish

    def sems(self):
        return [pltpu.SemaphoreType.DMA((self.nsem,)), pltpu.SemaphoreType.DMA((self.nsem,))]


def _run_rider(name, rd):
    n_in, n_out = len(rd.ins), len(rd.out_shapes)

    def body(*refs):
        ins, outs, (send, recv) = refs[:n_in], refs[n_in:n_in + n_out], refs[n_in + n_out:]
        rd.start(ins, outs, send, recv)
        rd.finish(ins, outs, send, recv)

    return pl.pallas_call(body, name=name, in_specs=[_HBM] * n_in, out_specs=[_HBM] * n_out, out_shape=rd.out_shapes,
                          scratch_shapes=rd.sems())(*rd.ins)


def _gather_rider(ts):
    nt = len(ts)

    def half(t, hc):
        rh = ts[t].shape[0] // 2
        return pl.ds(pl.multiple_of(hc * rh, 16), rh)

    def rcopy(send, recv, t, k, src, dst, to):
        return pltpu.make_async_remote_copy(src_ref=src, dst_ref=dst, send_sem=send.at[6 * t + k], recv_sem=recv.at[6 * t + k],
                                            device_id=to, device_id_type=MESH)

    def first_hop(ins, outs, send, recv, t, r, px, py, c, me):
        return rcopy(send, recv, t, r, ins[t].at[half(t, c)], outs[t].at[me, half(t, c)], (px, py, c))

    def start(ins, outs, send, recv):
        x, y, c, rel = _place()
        for t in range(nt):
            for r, (px, py) in enumerate(rel):
                first_hop(ins, outs, send, recv, t, r, px, py, c, 2 * x + y).start()

    def finish(ins, outs, send, recv):
        x, y, c, rel = _place()
        sib = (x, y, 1 - c)
        passed = []
        for t in range(nt):
            for r, (px, py) in enumerate(rel):
                got = outs[t].at[2 * px + py, half(t, c)]
                rcopy(send, recv, t, r, got, got, (px, py, c)).wait_recv()
                fw = rcopy(send, recv, t, 3 + r, got, got, sib)
                fw.start()
                passed.append(fw)
        for t in range(nt):
            for r, (px, py) in enumerate(rel):
                got = outs[t].at[2 * px + py, half(t, 1 - c)]
                rcopy(send, recv, t, 3 + r, got, got, sib).wait_recv()
        for t in range(nt):
            for r, (px, py) in enumerate(rel):
                first_hop(ins, outs, send, recv, t, r, px, py, c, 2 * x + y).wait_send()
        for fw in passed:
            fw.wait_send()

    return _Rider(ts, [jax.ShapeDtypeStruct((4,) + tuple(t.shape), t.dtype) for t in ts], 6 * nt, start, finish)


def _scatter_rider(ps):
    nt = len(ps)

    def copy(ins, outs, send, recv, t, r, px, py, c):
        return pltpu.make_async_remote_copy(src_ref=ins[t].at[2 * px + py], dst_ref=outs[t].at[r], send_sem=send.at[3 * t + r],
                                            recv_sem=recv.at[3 * t + r], device_id=(px, py, c), device_id_type=MESH)

    def start(ins, outs, send, recv):
        x, y, c, rel = _place()
        for t in range(nt):
            for r, (px, py) in enumerate(rel):
                copy(ins, outs, send, recv, t, r, px, py, c).start()

    def finish(ins, outs, send, recv):
        x, y, c, rel = _place()
        for t in range(nt):
            for r, (px, py) in enumerate(rel):
                copy(ins, outs, send, recv, t, r, px, py, c).wait()

    return _Rider(ps, [jax.ShapeDtypeStruct((3,) + tuple(p.shape[1:]), p.dtype) for p in ps], 3 * nt, start, finish)


def _swap_rider(ts):
    nt = len(ts)

    def copy(ins, outs, send, recv, t):
        x, y, c, _ = _place()
        rh = ts[t].shape[1] // 2
        src = ins[t].at[:, pl.ds(pl.multiple_of((1 - c) * rh, 16), rh), :]
        return pltpu.make_async_remote_copy(src_ref=src, dst_ref=outs[t], send_sem=send.at[t], recv_sem=recv.at[t],
                                            device_id=(x, y, 1 - c), device_id_type=MESH)

    def start(ins, outs, send, recv):
        for t in range(nt):
            copy(ins, outs, send, recv, t).start()

    def finish(ins, outs, send, recv):
        for t in range(nt):
            copy(ins, outs, send, recv, t).wait()

    return _Rider(ts, [jax.ShapeDtypeStruct((4, t.shape[1] // 2, t.shape[2]), t.dtype) for t in ts], nt, start, finish)


def _pair_exchange(gs):
    nt = len(gs)

    def body(*refs):
        ins, outs = refs[:nt], refs[nt:2 * nt]
        send, recv = refs[2 * nt:]
        x, y, c, _ = _place()
        cps = []
        for t in range(nt):
            cp = pltpu.make_async_remote_copy(src_ref=ins[t], dst_ref=outs[t], send_sem=send.at[t], recv_sem=recv.at[t],
                                              device_id=(x, y, 1 - c), device_id_type=MESH)
            cp.start()
            cps.append(cp)
        for cp in cps:
            cp.wait()

    return pl.pallas_call(
        body, name="pair_exchange", in_specs=[_HBM] * nt, out_specs=[_HBM] * nt,
        out_shape=[jax.ShapeDtypeStruct(tuple(g.shape), g.dtype) for g in gs],
        scratch_shapes=[pltpu.SemaphoreType.DMA((nt,)), pltpu.SemaphoreType.DMA((nt,))],
    )(*gs)


def _allgather8(v):
    m, n = v.shape

    def body(x_ref, out_ref, send, recv, lsem):
        x, y, c, rel = _place()
        me, sib = (x, y, c), (x, y, 1 - c)

        def blk(px, py, pc):
            return out_ref.at[4 * px + 2 * py + pc]

        def copy(k, block, to, src=None):
            return pltpu.make_async_remote_copy(src_ref=blk(*block) if src is None else src, dst_ref=blk(*block), send_sem=send.at[k],
                                                recv_sem=recv.at[k], device_id=to, device_id_type=MESH)

        mine = pltpu.make_async_copy(x_ref, blk(*me), lsem)
        mine.start()
        first = [copy(0, me, sib, src=x_ref)] + [copy(1 + r, me, (*ch, c), src=x_ref) for r, ch in enumerate(rel)]
        for cp in first:
            cp.start()
        passed = [copy(4 + r, (*ch, c), sib) for r, ch in enumerate(rel)]
        for r, ch in enumerate(rel):
            copy(1 + r, (*ch, c), me).wait_recv()
            passed[r].start()
        copy(0, sib, me).wait_recv()
        for r, ch in enumerate(rel):
            copy(4 + r, (*ch, 1 - c), me).wait_recv()
        for cp in first + passed:
            cp.wait_send()
        mine.wait()

    return pl.pallas_call(
        body, name="allgather8", in_specs=[pl.BlockSpec(memory_space=pltpu.VMEM)], out_specs=pl.BlockSpec(memory_space=pltpu.VMEM),
        out_shape=jax.ShapeDtypeStruct((8, m, n), v.dtype),
        scratch_shapes=[pltpu.SemaphoreType.DMA((7,)), pltpu.SemaphoreType.DMA((7,)), pltpu.SemaphoreType.DMA],
    )(v)


def _t5_bucket(dist):
    max_exact = REL_BUCKETS // 2
    d = jnp.maximum(dist, 1).astype(F32)
    large = max_exact + (jnp.log(d / max_exact) / math.log(REL_MAX_DIST / max_exact) * (REL_BUCKETS - max_exact)).astype(jnp.int32)
    large = jnp.minimum(large, REL_BUCKETS - 1)
    return jnp.where(dist < max_exact, dist, large)


def _bias_onehot():
    qi = jnp.arange(SWA_BLOCK)[:, None]
    kj = jnp.arange(SWA_BLOCK)[None, :]
    dist = jnp.concatenate([(qi + SWA_BLOCK - kj).reshape(-1), (qi - kj).reshape(-1)])
    bucket = _t5_bucket(jnp.maximum(dist, 0))
    return (bucket[None, :] == jnp.arange(REL_BUCKETS)[:, None]).astype(F32)


def _head_spread():
    lane = jnp.arange(LANES)[:, None]
    head = jnp.arange(GDN_W)[None, :] // GDN_HEAD_DIM
    return (lane == head).astype(F32), (lane == head + GDN_HEADS).astype(F32)


def _lane16(v8):
    return jnp.pad(v8.astype(F32), (GDN_HEADS, LANES - 2 * GDN_HEADS)).reshape(1, LANES)


def _stack_heads(t, nb):
    return t.reshape(nb, SWA_BLOCK, SWA_KV_HEADS, SWA_GRP, SWA_HEAD_DIM).transpose(2, 0, 3, 1, 4).reshape(
        SWA_KV_HEADS, nb * SWA_GRP * SWA_BLOCK, SWA_HEAD_DIM)


def _unstack_heads(t, nb):
    return t.reshape(SWA_KV_HEADS, nb, SWA_GRP, SWA_BLOCK, SWA_HEAD_DIM).transpose(1, 3, 0, 2, 4).reshape(nb * SWA_BLOCK, SWA_Q)


def _kv_heads(t):
    return t.reshape(t.shape[0], SWA_KV_HEADS, SWA_HEAD_DIM).transpose(1, 0, 2)


def _swa_specs(qs, ks, vs, bp, bc, sk, grad, gdt=(F32,)):
    T = ks.shape[1]
    qr = SWA_GRP * SWA_BLOCK
    g = lambda a: tuple(a.shape) if grad else None
    nk = SWA_KV_HEADS
    m3 = lambda j, n: (0, n, 0)
    h3 = lambda j, n: (0, jnp.maximum(n - 1, 0), 0)
    p3 = lambda j: (0, 0, 0)
    rows = [Row(qs, (nk, qr, SWA_HEAD_DIM), m3, gshape=g(qs), gmap=m3, gdt=gdt),
            Row(ks, (nk, SWA_BLOCK, SWA_HEAD_DIM), m3, (nk, SWA_BLOCK, SWA_HEAD_DIM), h3, g(ks), m3, gdt),
            Row(vs, (nk, SWA_BLOCK, SWA_HEAD_DIM), m3, (nk, SWA_BLOCK, SWA_HEAD_DIM), h3, g(vs), m3, gdt)]
    pars = [Par(bp, (nk, qr, SWA_BLOCK), p3, g(bp), p3), Par(bc, (nk, qr, SWA_BLOCK), p3, g(bc), p3),
            Par(sk, (nk, qr, 1), p3, g(sk), p3)]
    return rows, pars, T // SWA_BLOCK


class _LocalWeights:
    def __init__(self, W):
        self.W = W

    def w1(self):
        return self.W

    def rider_a(self):
        return None

    def w2(self, got):
        return self.W

    def rider_b(self):
        return None

    def w3(self, got):
        return self.W

    def rider_g(self, G):
        return None

    def g_done(self, got):
        pass

    def rider_up(self, G):
        return None

    def up_done(self, got):
        pass

    def swap_up(self, G):
        return None

    def swap_up_done(self, got):
        pass

    def swap_rest(self, G):
        return None

    def swap_rest_done(self, got):
        pass

    def rider_last(self, G):
        return None

    def last_done(self, got):
        pass


def _fwd_bwd(x, mem, tgt, src):
    W = dict(src.w1())
    T = x.shape[0]
    nb = T // SWA_BLOCK
    tb = min(256, T)
    tbl = min(512, T)
    fwd = lambda f: (lambda *a: (f(*a), []))
    full = lambda cols, dt, t, cw: Out((T, cols), dt, (t, cw), lambda j, n: (n, j))

    xb = x.astype(_CDT)
    ra = src.rider_a()
    proj = _mm("proj", xb, W["in_p"], "nn", rider=ra)
    proj, got = proj if ra is not None else (proj, None)
    W.update(src.w2(got))

    onehot_t = _bias_onehot()
    bias_flat = _mm("swa_bias", W["rel_bias"].T, onehot_t, "nn", hi=True)
    half = SWA_BLOCK * SWA_BLOCK
    bp = bias_flat[:, :half].reshape(SWA_KV_HEADS, SWA_GRP * SWA_BLOCK, SWA_BLOCK)
    bc = bias_flat[:, half:].reshape(SWA_KV_HEADS, SWA_GRP * SWA_BLOCK, SWA_BLOCK)
    sk = jnp.broadcast_to(W["swa_sinks"].reshape(SWA_KV_HEADS, SWA_GRP, 1, 1), (SWA_KV_HEADS, SWA_GRP, SWA_BLOCK, 1)).reshape(
        SWA_KV_HEADS, SWA_GRP * SWA_BLOCK, 1)
    qs = _stack_heads(proj[:, P_SQ:P_SQ + SWA_Q], nb)
    ks = _kv_heads(proj[:, P_SK:P_SK + SWA_KV])
    vs = _kv_heads(proj[:, P_SV:P_SV + SWA_KV])
    rows, pars, nblk = _swa_specs(qs, ks, vs, bp, bc, sk, False)
    o_s, = _rowmap("swa_fwd", fwd(_swa_fn), 1, nblk, rows, pars,
                   [Out(tuple(qs.shape), F32, (SWA_KV_HEADS, SWA_GRP * SWA_BLOCK, SWA_HEAD_DIM), lambda j, n: (0, n, 0))])
    o_swa = _unstack_heads(o_s, nb).astype(_CDT)

    ncq = 3 * GDN_W // LANES
    tbp = min(1024, T)
    pre_rows = lambda grad: [_rowspec(proj, tbp, LANES, P_GQKV // LANES, halo=SUBLANES, grad=grad, ncol=ncq, gdt=(_CDT,))]
    pre_pars = lambda grad: [_parspec(W["gdn_conv_w"], LANES, 0, grad=grad, ncol=ncq)]
    qkv_n, = _rowmap("gdn_pre_fwd", fwd(_gdn_pre_fn), ncq, T // tbp, pre_rows(False), pre_pars(False),
                     [full(3 * GDN_W, F32, tbp, LANES)])
    eb, eg = _head_spread()
    alog_row, dtb_row = _lane16(W["gdn_a_log"]), _lane16(W["gdn_dt_bias"])
    gate_rows = lambda grad: [_rowspec(proj, tbl, LANES, P_BA // LANES, cstep=0, grad=grad, gdt=(_CDT,))]
    gate_pars = lambda grad: [_parspec(alog_row, grad=grad), _parspec(dtb_row, grad=grad), _parspec(eb), _parspec(eg)]
    bx, gx = _rowmap("gdn_gate_fwd", fwd(_gdn_gate_fn), 1, T // tbl, gate_rows(False), gate_pars(False),
                     [full(GDN_W, F32, tbl, GDN_W), full(GDN_W, F32, tbl, GDN_W)])
    nw = W["gdn_norm_w"].reshape(1, GDN_HEAD_DIM)
    o_gdn, states, got = _gdn_chunks_fwd(qkv_n, bx, gx, proj, nw, rider=src.rider_b())
    W.update(src.w3(got))

    ys = _mm("y_swa", o_swa, W["br_swa"], "nn")
    yg = _mm("y_gdn", o_gdn, W["br_gdn"], "nn")
    cwm = 512
    mix_rows = lambda grad: [_rowspec(proj, tb, cwm, P_GS // cwm, grad=grad, ncol=D_MODEL // cwm, gdt=(_CDT,)),
                             _rowspec(proj, tb, cwm, P_GG // cwm, grad=grad, ncol=D_MODEL // cwm, gdt=(_CDT,)),
                             _rowspec(ys, tb, cwm, 0, grad=grad, ncol=D_MODEL // cwm, gdt=(_CDT,)),
                             _rowspec(yg, tb, cwm, 0, grad=grad, ncol=D_MODEL // cwm, gdt=(_CDT,))]
    mixed, = _rowmap("mix_fwd", fwd(_mix_fn), D_MODEL // cwm, T // tb, mix_rows(False), [], [full(D_MODEL, _CDT, tb, cwm)])
    r1 = _mm("r1", mixed, W["mix_o"], "nn", add=x, add_scale=ALPHA)

    def ln_fwd(name, r, g, b):
        return _rowmap(name, _ln_fwd_fn, 1, T // tb, [_rowspec(r, tb, D_MODEL, 0)], [_parspec(g), _parspec(b)],
                       [full(D_MODEL, F32, tb, D_MODEL), full(D_MODEL, _CDT, tb, D_MODEL)])

    def ln_bwd(name, r, g, b, ct):
        return _rowmap_bwd(name, _ln_fn, 1, T // tb, [_rowspec(r, tb, D_MODEL, 0, grad=True, gdt=(F32, _CDT))],
                           [_parspec(g, grad=True), _parspec(b, grad=True)], [_rowspec(ct, tb, D_MODEL, 0)])

    g1, b1 = W["ln1_g"].reshape(1, -1), W["ln1_b"].reshape(1, -1)
    g2, b2 = W["ln2_g"].reshape(1, -1), W["ln2_b"].reshape(1, -1)
    g3, b3 = W["ln3_g"].reshape(1, -1), W["ln3_b"].reshape(1, -1)
    x1, x1b = ln_fwd("ln1_fwd", r1, g1, b1)

    qm = _mm("mem_q", x1b, W["mem_q"], "nn")
    kvm = _mm("mem_kv", mem, W["mem_kv"], "nn")
    ma_rows = lambda grad: [_rowspec(qm, tbl, MEM_HEAD_DIM, 0, grad=grad, ncol=MEM_HEADS, gdt=(_CDT,))]
    ma_pars = lambda grad: [_parspec(kvm, MEM_HEAD_DIM, 0, grad=grad, ncol=MEM_HEADS),
                            _parspec(kvm, MEM_HEAD_DIM, MEM_HEADS, grad=grad, ncol=MEM_HEADS)]
    om, = _rowmap("memattn_fwd", fwd(_memattn_fn), MEM_HEADS, T // tbl, ma_rows(False), ma_pars(False),
                  [full(MEM_W, _CDT, tbl, MEM_HEAD_DIM)])
    r2 = _mm("r2", om, W["mem_o"], "nn", add=x1, add_scale=ALPHA)
    x2, x2b = ln_fwd("ln2_fwd", r2, g2, b2)

    hcat = _mm("ffn_up", x2b, W["up_p"], "nn")
    cwf = 512
    ncf = D_FF_PAD // cwf
    cw_p, cb_p = W["ffn_conv_w_p"], W["ffn_conv_b_p"]
    tbf = min(512, T)
    ffn_rows = lambda grad: [_rowspec(hcat, tbf, cwf, 0, halo=SUBLANES, grad=grad, ncol=ncf, gdt=(_CDT,)),
                             _rowspec(hcat, tbf, cwf, ncf, halo=SUBLANES, grad=grad, ncol=ncf, gdt=(_CDT,))]
    ffn_pars = lambda grad: [_parspec(cw_p, cwf, 0, grad=grad, ncol=ncf), _parspec(cw_p, cwf, ncf, grad=grad, ncol=ncf),
                             _parspec(cb_p, cwf, 0, grad=grad, ncol=ncf), _parspec(cb_p, cwf, ncf, grad=grad, ncol=ncf)]
    act, = _rowmap("ffn_act_fwd", fwd(_ffn_act_fn), ncf, T // tbf, ffn_rows(False), ffn_pars(False), [full(D_FF_PAD, _CDT, tbf, cwf)])
    r3 = _mm("r3", act, W["down_p"], "nn", add=x2, add_scale=ALPHA)
    dr3, dr3b, lacc, dg3, db3 = _rowmap("ln3_loss", _loss_fn, 1, T // tb, [_rowspec(r3, tb, D_MODEL, 0), _rowspec(tgt, tb, D_MODEL, 0)],
                                        [_parspec(g3), _parspec(b3)], [full(D_MODEL, F32, tb, D_MODEL), full(D_MODEL, _CDT, tb, D_MODEL)],
                                  accs=[(SUBLANES, LANES), (1, D_MODEL), (1, D_MODEL)])
    loss = lacc[0, 0]

    G = {}
    G["down_p"] = _mm("dw_down", act, dr3b, "tn", out_dtype=_GDT)
    dact = _mm("d_act", dr3b, W["down_p"], "nt")
    dhg, dhu, dcwg, dcwu, dcbg, dcbu = _rowmap_bwd("ffn_act_bwd", _ffn_act_fn, ncf, T // tbf, ffn_rows(True), ffn_pars(True),
                                                   [_rowspec(dact, tbf, cwf, 0)])
    dx2 = _mm("dx2_gate", dhg, W["up_p"], "nt", add=dr3, add_scale=ALPHA, b_k0=0)
    dx2 = _mm("dx2_up", dhu, W["up_p"], "nt", add=dx2, b_k0=D_FF_PAD)
    G["up_p"] = jnp.concatenate([_mm("dw_gate", x2b, dhg, "tn", out_dtype=_GDT), _mm("dw_up", x2b, dhu, "tn", out_dtype=_GDT)], axis=1)
    G["ffn_conv_w"] = jnp.concatenate([dcwg[:, :D_FF], dcwu[:, :D_FF]], axis=1)
    G["ffn_conv_b"] = jnp.concatenate([dcbg[0, :D_FF], dcbu[0, :D_FF]])
    G["ln3_g"], G["ln3_b"] = dg3[0], db3[0]

    dr2, dr2b, dg2, db2 = ln_bwd("ln2_bwd", r2, g2, b2, dx2)
    G["ln2_g"], G["ln2_b"] = dg2[0], db2[0]
    G["mem_o"] = _mm("dw_mem_o", om, dr2b, "tn", out_dtype=_GDT)
    dom = _mm("d_om", dr2b, W["mem_o"], "nt", out_dtype=_CDT)
    dqm, dkm, dvm = _rowmap_bwd("memattn_bwd", _memattn_fn, MEM_HEADS, T // tbl, ma_rows(True), ma_pars(True),
                                [_rowspec(dom, tbl, MEM_HEAD_DIM, 0)])
    G["mem_kv"] = _mm("dw_mem_kv", mem.astype(_CDT), jnp.concatenate([dkm, dvm], axis=1).astype(_CDT), "tn", out_dtype=_GDT)
    G["mem_q"] = _mm("dw_mem_q", x1b, dqm, "tn", out_dtype=_GDT)
    dx1 = _mm("dx1", dqm, W["mem_q"], "nt", add=dr2, add_scale=ALPHA)

    dr1, dr1b, dg1, db1 = ln_bwd("ln1_bwd", r1, g1, b1, dx1)
    G["ln1_g"], G["ln1_b"] = dg1[0], db1[0]
    G["mix_o"] = _mm("dw_mix_o", mixed, dr1b, "tn", out_dtype=_GDT)
    rs = src.swap_up(G)
    dmixed = _mm("d_mixed", dr1b, W["mix_o"], "nt", rider=rs)
    if rs is not None:
        dmixed, got = dmixed
        src.swap_up_done(got)
    dgs, dgg, dys, dyg = _rowmap_bwd("mix_bwd", _mix_fn, D_MODEL // cwm, T // tb, mix_rows(True), [], [_rowspec(dmixed, tb, cwm, 0)])
    G["br_swa"] = _mm("dw_br_swa", o_swa, dys, "tn", out_dtype=_GDT)
    G["br_gdn"] = _mm("dw_br_gdn", o_gdn, dyg, "tn", out_dtype=_GDT)
    do_swa = _mm("d_o_swa", dys, W["br_swa"], "nt", out_dtype=_CDT)
    rs = src.swap_rest(G)
    do_gdn = _mm("d_o_gdn", dyg, W["br_gdn"], "nt", rider=rs)
    if rs is not None:
        do_gdn, got = do_gdn
        src.swap_rest_done(got)

    rows, pars, nblk = _swa_specs(qs, ks, vs, bp, bc, sk, True, (_CDT,))
    m3 = lambda j, n: (0, n, 0)
    dqs, dks, dvs, dbp, dbc, dsk = _rowmap_bwd("swa_bwd", _swa_fn, 1, nblk, rows, pars,
                                               [Row(_stack_heads(do_swa, nb), (SWA_KV_HEADS, SWA_GRP * SWA_BLOCK, SWA_HEAD_DIM), m3)])
    d_swa = jnp.concatenate([_unstack_heads(dqs, nb), dks.transpose(1, 0, 2).reshape(T, SWA_KV),
                             dvs.transpose(1, 0, 2).reshape(T, SWA_KV)], axis=1)
    dbias = jnp.concatenate([dbp.reshape(SWA_HEADS, half), dbc.reshape(SWA_HEADS, half)], axis=1)
    G["rel_bias"] = _mm("d_rel_bias", dbias, onehot_t.T, "nn", hi=True).T
    G["swa_sinks"] = _mm("d_sinks", dsk.reshape(SWA_HEADS, SWA_BLOCK), jnp.ones((SWA_BLOCK, LANES), F32), "nn", hi=True)[:, 0]

    dqkv_n, dbx, dgx, dz, dnw, got = _gdn_chunks_bwd(qkv_n, bx, gx, proj, nw, states, do_gdn, rider=src.rider_g(G))
    src.g_done(got)
    G["gdn_norm_w"] = dnw[0]
    dgba, dalog, ddtb = _rowmap_bwd("gdn_gate_bwd", _gdn_gate_fn, 1, T // tbl, gate_rows(True), gate_pars(True),
                                    [_rowspec(dbx, tbl, GDN_W, 0), _rowspec(dgx, tbl, GDN_W, 0)])
    G["gdn_a_log"], G["gdn_dt_bias"] = dalog[0, GDN_HEADS:2 * GDN_HEADS], ddtb[0, GDN_HEADS:2 * GDN_HEADS]
    dgqkv, dcw_gdn = _rowmap_bwd("gdn_pre_bwd", _gdn_pre_fn, ncq, T // tbp, pre_rows(True), pre_pars(True),
                                 [_rowspec(dqkv_n, tbp, LANES, 0)])
    G["gdn_conv_w"] = dcw_gdn

    dproj = jnp.concatenate([dgs, dgg, dgqkv, dz, d_swa, dgba, jnp.zeros((T, P_END - P_USED), _CDT)], axis=1)
    ru = src.rider_up(G)
    G["in_p"] = _mm("dw_in", xb, dproj, "tn", out_dtype=_GDT, rider=ru)
    if ru is not None:
        G["in_p"], got = G["in_p"]
        src.up_done(got)
    rl = src.rider_last(G)
    dx = _mm("dx", dproj, W["in_p"], "nt", add=dr1, add_scale=ALPHA, rider=rl)
    if rl is not None:
        dx, got = dx
        src.last_done(got)
    return loss, dx, G


W_NAMES = ["w_in", "rel_bias", "swa_sinks", "gdn_conv_w", "gdn_a_log", "gdn_dt_bias", "gdn_norm_w", "w_br_swa", "w_br_gdn",
           "w_mix_o", "ln1_g", "ln1_b", "w_mem_q", "w_mem_kv", "w_mem_o", "ln2_g", "ln2_b", "w_up", "ffn_conv_w", "ffn_conv_b",
           "w_down", "ln3_g", "ln3_b"]
BIG = ["w_in", "w_br_swa", "w_br_gdn", "w_mix_o", "w_mem_q", "w_mem_kv", "w_mem_o", "w_up", "w_down"]
SMALL = [n for n in W_NAMES if n not in BIG]
COL_SHARDED = ["w_in", "w_br_swa", "w_br_gdn", "w_mem_o", "w_up"]


def _pack(arrs):
    rows = []
    for a in arrs:
        f = a.reshape(-1).astype(F32)
        rows.append(jnp.pad(f, (0, (-f.shape[0]) % LANES)).reshape(-1, LANES))
    n = sum(r.shape[0] for r in rows)
    if n % 16:
        rows.append(jnp.zeros((16 - n % 16, LANES), F32))
    return jnp.concatenate(rows, axis=0)


def _unpack(p, shapes):
    out, off = [], 0
    for s in shapes:
        n = int(np.prod(s)) if len(s) else 1
        r = -(-n // LANES)
        out.append(p[off:off + r].reshape(-1)[:n].reshape(s))
        off += r
    return out


def _merge_shards(d):
    cat = lambda names: jnp.concatenate([d[n] for n in names], axis=-2)
    return [d.get("w_in"), d["w_up"], cat(["w_br_swa", "w_br_gdn", "w_mem_q", "w_mem_o"]), cat(["w_mix_o", "w_down"]), d["w_mem_kv"]]


def _split_shards(ts):
    a, b, c, dd, e = ts
    return {"w_in": a, "w_up": b, "w_br_swa": c[..., 0:1024, :], "w_br_gdn": c[..., 1024:2048, :], "w_mem_q": c[..., 2048:2560, :],
            "w_mem_o": c[..., 2560:3072, :], "w_mix_o": dd[..., 0:512, :], "w_down": dd[..., 512:, :], "w_mem_kv": e}


def _to_full(name, t):
    if name in COL_SHARDED:
        return _cols_from_chips(t, [(0, 4 * t.shape[2])])
    return t.reshape(4 * t.shape[1], t.shape[2])


def _to_chips(name, t):
    if name in COL_SHARDED:
        return _chips_from_cols(t, [(0, t.shape[1])], t.shape[1] // 4)
    return t.reshape(4, t.shape[0] // 4, t.shape[1])


def _cols_from_chips(g, segs, own=None):
    C, parts = g.shape[2], []
    for s in segs:
        if isinstance(s, int):
            parts.append(jnp.zeros((g.shape[1], s), g.dtype))
            continue
        lo, hi = s
        while lo < hi:
            k = lo // C
            e = min(hi, (k + 1) * C)
            piece = g[k][:, lo - k * C:e - k * C]
            parts.append(piece if own is None else jnp.where(own[1] == k, own[0][:, lo - k * C:e - k * C], piece))
            lo = e
    return jnp.concatenate(parts, axis=1)


def _chips_from_cols(p, segs, C):
    out = []
    for k in range(4):
        lo, hi, parts, o = k * C, (k + 1) * C, [], 0
        for plo, w in segs:
            a, b = max(lo, o), min(hi, o + w)
            if a < b:
                parts.append(p[:, plo + a - o:plo + b - o])
            o += w
        out.append(jnp.concatenate(parts, axis=1))
    return jnp.stack(out)


_IN_OFF = np.cumsum((0,) + IN_WIDTHS)
_IN_SEGS = [(P_SQ, SWA_Q), (P_SK, SWA_KV), (P_SV, SWA_KV), (P_GQKV, 3 * GDN_W), (P_GZ, GDN_W), (P_BA, 2 * GDN_HEADS),
            (P_GS, D_MODEL), (P_GG, D_MODEL)]
_IN_PADDED = [(int(_IN_OFF[i]), int(_IN_OFF[k])) for i, k in ((9, 10), (10, 11), (3, 6), (6, 7), (0, 1), (1, 2), (2, 3), (7, 9))] + [
    P_END - P_BA - 2 * GDN_HEADS]
_UP_SEGS = [(0, D_FF), (D_FF_PAD, D_FF)]
_UP_PADDED = [(0, D_FF), D_FF_PAD - D_FF, (D_FF, 2 * D_FF), D_FF_PAD - D_FF]


def _in_to_padded(w):
    o = _IN_OFF
    cut = lambda i, k: w[:, o[i]:o[k]]
    return jnp.concatenate([cut(9, 10), cut(10, 11), cut(3, 6), cut(6, 7), cut(0, 1), cut(1, 2), cut(2, 3), cut(7, 9),
                            jnp.zeros((w.shape[0], P_END - P_BA - 2 * GDN_HEADS), w.dtype)], axis=1)


def _in_from_padded(p):
    return jnp.concatenate([p[:, P_SQ:P_SQ + SWA_Q], p[:, P_SK:P_SK + SWA_KV], p[:, P_SV:P_SV + SWA_KV], p[:, P_GQKV:P_GQKV + 3 * GDN_W],
                            p[:, P_GZ:P_GZ + GDN_W], p[:, P_BA:P_BA + 2 * GDN_HEADS], p[:, P_GS:P_GS + D_MODEL], p[:, P_GG:P_GG + D_MODEL]],
                           axis=1)


def _ff_pad(t, axis):
    g, u = jnp.split(t, 2, axis=axis)
    pad = [(0, 0)] * t.ndim
    pad[axis] = (0, D_FF_PAD - D_FF)
    return jnp.concatenate([jnp.pad(g, pad), jnp.pad(u, pad)], axis=axis)


def _ff_unpad(t, axis):
    g, u = jnp.split(t, 2, axis=axis)
    return jnp.concatenate([lax.slice_in_dim(g, 0, D_FF, axis=axis), lax.slice_in_dim(u, 0, D_FF, axis=axis)], axis=axis)


def _assemble_weights(full, small):
    W = dict(small)
    W["in_p"] = _in_to_padded(full["w_in"])
    W["up_p"] = _ff_pad(full["w_up"], 1)
    W["down_p"] = jnp.pad(full["w_down"], ((0, D_FF_PAD - D_FF), (0, 0)))
    W["br_swa"], W["br_gdn"], W["mix_o"] = full["w_br_swa"], full["w_br_gdn"], full["w_mix_o"]
    W["mem_q"], W["mem_kv"], W["mem_o"] = full["w_mem_q"], full["w_mem_kv"], full["w_mem_o"]
    W["ffn_conv_w_p"] = _ff_pad(small["ffn_conv_w"], 1)
    W["ffn_conv_b_p"] = _ff_pad(small["ffn_conv_b"].reshape(1, -1), 1)
    return W


def _full_grads(G):
    out = {"w_in": _in_from_padded(G["in_p"])} if "in_p" in G else {}
    out.update({"w_up": _ff_unpad(G["up_p"], 1), "w_down": G["down_p"][:D_FF], "w_br_swa": G["br_swa"], "w_br_gdn": G["br_gdn"],
                "w_mix_o": G["mix_o"], "w_mem_q": G["mem_q"], "w_mem_kv": G["mem_kv"], "w_mem_o": G["mem_o"]})
    return out


def kernel(x, mem, w_in, rel_bias, swa_sinks, gdn_conv_w, gdn_a_log, gdn_dt_bias, gdn_norm_w, w_br_swa, w_br_gdn, w_mix_o, ln1_g, ln1_b, w_mem_q, w_mem_kv, w_mem_o, ln2_g, ln2_b, w_up, ffn_conv_w, ffn_conv_b, w_down, ln3_g, ln3_b, loss_target, m_w_in, m_rel_bias, m_swa_sinks, m_gdn_conv_w, m_gdn_a_log, m_gdn_dt_bias, m_gdn_norm_w, m_w_br_swa, m_w_br_gdn, m_w_mix_o, m_ln1_g, m_ln1_b, m_w_mem_q, m_w_mem_kv, m_w_mem_o, m_ln2_g, m_ln2_b, m_w_up, m_ffn_conv_w, m_ffn_conv_b, m_w_down, m_ln3_g, m_ln3_b, v_w_in, v_rel_bias, v_swa_sinks, v_gdn_conv_w, v_gdn_a_log, v_gdn_dt_bias, v_gdn_norm_w, v_w_br_swa, v_w_br_gdn, v_w_mix_o, v_ln1_g, v_ln1_b, v_w_mem_q, v_w_mem_kv, v_w_mem_o, v_ln2_g, v_ln2_b, v_w_up, v_ffn_conv_w, v_ffn_conv_b, v_w_down, v_ln3_g, v_ln3_b):
    a = dict(locals())
    w = {n: a[n] for n in W_NAMES}
    m = {n: a["m_" + n] for n in W_NAMES}
    v = {n: a["v_" + n] for n in W_NAMES}
    chip = 2 * lax.axis_index("x") + lax.axis_index("y")
    core = lax.axis_index("c")
    sq = lambda t: t.reshape(t.shape[1:]) if (t.ndim > 1 and t.shape[0] == 1 and t is not rel_bias) else t

    sh_a, sh_b, sh_c, sh_d, sh_e = _merge_shards({n: sq(w[n]).astype(_CDT) for n in BIG})
    fcw_sh, gcw_sh = sq(ffn_conv_w).shape, sq(gdn_conv_w).shape
    slot = lax.broadcasted_iota(jnp.int32, (4, 1, 1), 0)

    def with_own(got, mine):
        return [jnp.where(slot == chip, t[None], g) for g, t in zip(got, mine)]

    def reduce_start(tag, gch, theirs=None):
        pair = []
        theirs = _run_rider("pair_swap_" + tag, _swap_rider(gch)) if theirs is None else theirs
        for t, (mine, got) in enumerate(zip(gch, theirs)):
            rh = mine.shape[1] // 2
            mine_h = lax.dynamic_slice_in_dim(mine, core * rh, rh, axis=1)
            pair.append(_addn(f"pair_sum_{tag}{t}", [mine_h.reshape(4 * rh, -1), got.reshape(4 * rh, -1)], _GDT).reshape(4, rh, -1))
        return pair

    def reduce_end(tag, pair, others):
        halves = []
        for t, (p, o) in enumerate(zip(pair, others)):
            own = lax.dynamic_index_in_dim(p, chip, 0, keepdims=False)
            halves.append(_addn(f"chip_sum_{tag}{t}", [own, (o, 0), (o, 1), (o, 2)]))
        return halves

    class MeshWeights:
        def w1(self):
            mine = [sh_a, _pack([sq(ffn_conv_w), sq(gdn_conv_w)])]
            got_a, got_f = _run_rider("gather_first", _gather_rider(mine))
            got_f, = with_own([got_f], mine[1:])
            conv = [_unpack(got_f[k], [fcw_sh, gcw_sh]) for k in range(4)]
            W = {n: sq(w[n]) for n in SMALL}
            W["ffn_conv_w"] = jnp.concatenate([cv[0] for cv in conv], axis=1)
            W["gdn_conv_w"] = jnp.concatenate([cv[1] for cv in conv], axis=1)
            W["ffn_conv_w_p"] = _ff_pad(W["ffn_conv_w"], 1)
            W["ffn_conv_b_p"] = _ff_pad(W["ffn_conv_b"].reshape(1, -1), 1)
            W["in_p"] = _cols_from_chips(got_a, _IN_PADDED, own=(sh_a, chip))
            return W

        def rider_a(self):
            return _gather_rider([sh_c, sh_d, sh_e])

        def w2(self, got):
            c, d, e = with_own(got, [sh_c, sh_d, sh_e])
            f = {n: _to_full(n, t) for n, t in _split_shards([None, None, c, d, e]).items() if t is not None}
            return {"br_swa": f["w_br_swa"], "br_gdn": f["w_br_gdn"], "mix_o": f["w_mix_o"], "mem_q": f["w_mem_q"], "mem_kv": f["w_mem_kv"],
                    "mem_o": f["w_mem_o"], "down_p": jnp.pad(f["w_down"], ((0, D_FF_PAD - D_FF), (0, 0)))}

        def rider_b(self):
            return _gather_rider([sh_b])

        def w3(self, got):
            return {"up_p": _cols_from_chips(got[0], _UP_PADDED, own=(sh_b, chip))}

        def swap_rest(self, G):
            gf = _full_grads(G)
            gch = {n: _to_chips(n, gf[n]) for n in BIG if n not in ("w_in", "w_up")}
            gch["w_up"] = None
            self.gch_rest = _merge_shards(gch)[2:]
            return _swap_rider(self.gch_rest)

        def swap_rest_done(self, got):
            self.theirs_rest = got

        def rider_g(self, G):
            self.pair = reduce_start("rest", self.gch_rest, self.theirs_rest)
            return _scatter_rider(self.pair)

        def g_done(self, got):
            self.halves = reduce_end("rest", self.pair, got)

        def swap_up(self, G):
            self.gch_up = [_chips_from_cols(G["up_p"], _UP_SEGS, 2 * D_FF // 4)]
            return _swap_rider(self.gch_up)

        def swap_up_done(self, got):
            self.theirs_up = got

        def rider_up(self, G):
            self.pair_up = reduce_start("up", self.gch_up, self.theirs_up)
            return _scatter_rider(self.pair_up)

        def up_done(self, got):
            self.halves = reduce_end("up", self.pair_up, got) + self.halves

        def rider_last(self, G):
            self.pair_in = reduce_start("in", [_chips_from_cols(G["in_p"], _IN_SEGS, sum(IN_WIDTHS) // 4)])
            return _scatter_rider(self.pair_in)

        def last_done(self, got):
            self.halves = reduce_end("in", self.pair_in, got) + self.halves

    src = MeshWeights()
    loss, dx, G = _fwd_bwd(x[0], mem[0], loss_target[0], src)

    small_names = SMALL
    small_shapes = [()] + [tuple(G[n].shape) for n in small_names]
    packed = _pack([loss] + [G[n] for n in small_names])
    allp = _allgather8(packed)
    tot = _addn("small_sum", [(allp, k) for k in range(8)])
    parts = _unpack(tot, small_shapes)
    loss_tot, gsmall = parts[0], dict(zip(small_names, parts[1:]))
    gsmall["ffn_conv_w"] = lax.dynamic_slice_in_dim(gsmall["ffn_conv_w"], chip * fcw_sh[1], fcw_sh[1], axis=1)
    gsmall["gdn_conv_w"] = lax.dynamic_slice_in_dim(gsmall["gdn_conv_w"], chip * gcw_sh[1], gcw_sh[1], axis=1)

    both = []
    for h, o in zip(src.halves, _pair_exchange(src.halves)):
        both.append(jnp.concatenate([jnp.where(core == 0, h, o), jnp.where(core == 0, o, h)], axis=0))
    gbig = _split_shards(both)

    outs = {}
    for n in BIG:
        if w[n].shape[-1] % LANES:
            cols_out = lambda t: jnp.transpose(t, (2, 0, 1))
            g_ = jnp.transpose(gbig[n])[:, None, :]
            d_, m_, v_ = _adamw("adamw_" + n, cols_out(w[n]), g_, cols_out(m[n]), cols_out(v[n]))
            outs[n] = tuple(jnp.transpose(t, (1, 2, 0)) for t in (g_, d_, m_, v_))
            continue
        d_, m_, v_ = _adamw("adamw_" + n, sq(w[n]), gbig[n], sq(m[n]), sq(v[n]))
        outs[n] = (gbig[n], d_, m_, v_)
    for n in SMALL:
        two_d = (-1, w[n].shape[-1])
        g_ = gsmall[n].reshape(two_d)
        d_, m_, v_ = _adamw("adamw_" + n, w[n].reshape(two_d), g_, m[n].reshape(two_d), v[n].reshape(two_d))
        outs[n] = (g_, d_, m_, v_)

    res = [loss_tot.reshape(()), dx.reshape(x.shape)]
    for k in range(4):
        res += [outs[n][k].reshape(w[n].shape) for n in W_NAMES]
    return tuple(res)
```
